```python
import jax, jax.numpy as jnp
from jax import lax
import numpy as np

D_MODEL = 1024
BATCH = 4
SEQ = 4096
DEPTH = 1

PLE_DIM = 256
EPS = 1e-6
RET_HEADS = 4
RET_DK = 128
RET_DV = 128
RET_CHUNK = 128
SWA_HEADS = 8
SWA_KV_HEADS = 2
SWA_GROUP = SWA_HEADS // SWA_KV_HEADS
SWA_HD = 64
WINDOW = 128
N_GROUPS = 4
EXPERTS_PER_GROUP = 8
N_EXPERTS = N_GROUPS * EXPERTS_PER_GROUP
EXPERT_FF = 256
TOP_K = 2

RET_QK = RET_HEADS * RET_DK
RET_V = RET_HEADS * RET_DV
SWA_Q = SWA_HEADS * SWA_HD
SWA_KV = SWA_KV_HEADS * SWA_HD
IN_SPLITS = (RET_QK, RET_QK, RET_V, RET_V, SWA_Q, SWA_KV, SWA_KV, D_MODEL, D_MODEL)
IN_WIDTH = sum(IN_SPLITS)

kernel_name = "hybrid_retention_swa_hmoe_block"


def rms_norm(x, g):
    xf = x.astype(jnp.float32)
    y = xf * lax.rsqrt(jnp.mean(xf * xf, axis=-1, keepdims=True) + EPS)
    return (y * g.astype(jnp.float32)).astype(x.dtype)


def head_group_norm(y, g, b):
    yf = y.astype(jnp.float32)
    mu = jnp.mean(yf, axis=-1, keepdims=True)
    var = jnp.mean(jnp.square(yf - mu), axis=-1, keepdims=True)
    yn = ((yf - mu) * lax.rsqrt(var + EPS)).reshape(y.shape[0], y.shape[1], -1)
    return (yn * g.astype(jnp.float32) + b.astype(jnp.float32)).astype(y.dtype)


def retention_chunkwise(q, k, v):
    dtype = v.dtype
    q = q.astype(jnp.float32)
    k = k.astype(jnp.float32) * (RET_DK ** -0.5)
    v = v.astype(jnp.float32)
    b, s, h, dk = q.shape
    nc = s // RET_CHUNK
    q = q.reshape(b, nc, RET_CHUNK, h, dk)
    k = k.reshape(b, nc, RET_CHUNK, h, dk)
    v = v.reshape(b, nc, RET_CHUNK, h, RET_DV)
    log_gamma = jnp.log1p(-jnp.exp2(-5.0 - jnp.arange(h, dtype=jnp.float32)))
    pos = jnp.arange(RET_CHUNK, dtype=jnp.float32)
    diff = pos[:, None] - pos[None, :]
    decay_mask = jnp.where(diff[None] >= 0.0,
                           jnp.exp(jnp.maximum(diff, 0.0)[None] * log_gamma[:, None, None]), 0.0)
    scores = jnp.einsum('bnihd,bnjhd->bnhij', q, k) * decay_mask
    inner = jnp.einsum('bnhij,bnjhv->bnihv', scores, v)
    zeta = jnp.exp((RET_CHUNK - 1.0 - pos)[None, :] * log_gamma[:, None])
    chunk_kv = jnp.einsum('bnjhd,bnjhv,hj->nbhdv', k, v, zeta)
    chunk_decay = jnp.exp(RET_CHUNK * log_gamma)[None, :, None, None]

    def step(state, kv):
        return chunk_decay * state + kv, state

    _, prev_states = lax.scan(step, jnp.zeros_like(chunk_kv[0]), chunk_kv)
    xi = jnp.exp((pos + 1.0)[None, :] * log_gamma[:, None])
    cross = jnp.einsum('bnihd,nbhdv,hi->bnihv', q, prev_states, xi)
    return (inner + cross).reshape(b, s, h, RET_DV).astype(dtype)


def sliding_window_gqa(q, k, v, sinks):
    b, s = q.shape[:2]
    nb = s // WINDOW
    qb = q.reshape(b, nb, WINDOW, SWA_KV_HEADS, SWA_GROUP, SWA_HD)
    kb = k.reshape(b, nb, WINDOW, SWA_KV_HEADS, SWA_HD)
    vb = v.reshape(b, nb, WINDOW, SWA_KV_HEADS, SWA_HD)
    k_band = jnp.concatenate([jnp.concatenate([jnp.zeros_like(kb[:, :1]), kb[:, :-1]], axis=1), kb], axis=2)
    v_band = jnp.concatenate([jnp.concatenate([jnp.zeros_like(vb[:, :1]), vb[:, :-1]], axis=1), vb], axis=2)
    scores = jnp.einsum('bnqkgd,bnskd->bnkgqs', qb, k_band).astype(jnp.float32) * (SWA_HD ** -0.5)
    qi = jnp.arange(WINDOW)
    sj = jnp.arange(2 * WINDOW)
    rel = qi[:, None] + WINDOW - sj[None, :]
    in_window = (rel >= 0) & (rel < WINDOW)
    key_exists = (jnp.arange(nb)[:, None] * WINDOW - WINDOW + sj[None, :]) >= 0
    mask = in_window[None] & key_exists[:, None, :]
    slopes = jnp.exp2(-8.0 * jnp.arange(1, SWA_HEADS + 1, dtype=jnp.float32) / SWA_HEADS)
    slopes = slopes.reshape(SWA_KV_HEADS, SWA_GROUP)
    alibi = -slopes[:, :, None, None] * rel.astype(jnp.float32)[None, None]
    scores = jnp.where(mask[None, :, None, None], scores + alibi[None, None], -1e30)
    sink = jnp.broadcast_to(sinks.astype(jnp.float32).reshape(1, 1, SWA_KV_HEADS, SWA_GROUP, 1, 1),
                            scores.shape[:-1] + (1,))
    probs = jax.nn.softmax(jnp.concatenate([scores, sink], axis=-1), axis=-1)[..., :-1]
    out = jnp.einsum('bnkgqs,bnskd->bnqkgd', probs.astype(v.dtype), v_band)
    return out.reshape(b, s, SWA_Q)


def hierarchical_moe(h, w_rg, b_rg, w_re, b_re, w_gate, w_up, w_down):
    b, s, d = h.shape
    hf = h.reshape(-1, d)
    g_prob = jax.nn.softmax((hf @ w_rg + b_rg).astype(jnp.float32), axis=-1)
    g_sel = jnp.argmax(g_prob, axis=-1)
    g_w = jnp.take_along_axis(g_prob, g_sel[:, None], axis=-1)
    e_logits = (hf @ w_re + b_re).astype(jnp.float32).reshape(-1, N_GROUPS, EXPERTS_PER_GROUP)
    e_in_group = jnp.take_along_axis(e_logits, g_sel[:, None, None], axis=1)[:, 0]
    top_p, top_i = lax.top_k(jax.nn.softmax(e_in_group, axis=-1), TOP_K)
    top_p = top_p / jnp.sum(top_p, axis=-1, keepdims=True)
    expert_id = g_sel[:, None] * EXPERTS_PER_GROUP + top_i
    combine = jnp.sum(jax.nn.one_hot(expert_id, N_EXPERTS, dtype=jnp.float32)
                      * (g_w * top_p)[..., None], axis=1)
    y = jnp.zeros_like(hf)
    for grp in range(N_GROUPS):
        sl = slice(grp * EXPERTS_PER_GROUP, (grp + 1) * EXPERTS_PER_GROUP)
        hid = jax.nn.silu(jnp.einsum('td,edf->tef', hf, w_gate[sl])) * jnp.einsum('td,edf->tef', hf, w_up[sl])
        y = y + jnp.einsum('tef,efd->td', hid * combine[:, sl, None].astype(hid.dtype), w_down[sl])
    return y.reshape(b, s, d)


def setup_inputs(seed: int = 0) -> dict:
    key = jax.random.key(seed)
    ks = jax.random.split(key, 26)
    f32 = jnp.float32
    L = DEPTH

    def nrm(k, shape, scale):
        return jax.random.normal(k, shape, f32) * scale

    def gain(k, shape):
        return 1.0 + 0.02 * jax.random.normal(k, shape, f32)

    return dict(
        x=nrm(ks[0], (BATCH, SEQ, D_MODEL), 1.0),
        p=nrm(ks[1], (L, BATCH, SEQ, PLE_DIM), 1.0),
        mix_norm_g=gain(ks[2], (L, D_MODEL)),
        w_in=nrm(ks[3], (L, D_MODEL, IN_WIDTH), D_MODEL ** -0.5),
        ret_gn_g=gain(ks[4], (L, RET_V)),
        ret_gn_b=nrm(ks[5], (L, RET_V), 0.02),
        w_ret_o=nrm(ks[6], (L, RET_V, D_MODEL), RET_V ** -0.5),
        q_norm_g=gain(ks[7], (L, SWA_HD)),
        k_norm_g=gain(ks[8], (L, SWA_HD)),
        attn_sinks=nrm(ks[9], (L, SWA_HEADS), 0.5),
        w_swa_o=nrm(ks[10], (L, SWA_Q, D_MODEL), SWA_Q ** -0.5),
        w_out=nrm(ks[11], (L, D_MODEL, D_MODEL), D_MODEL ** -0.5),
        ffn_norm_g=gain(ks[12], (L, D_MODEL)),
        w_router_group=nrm(ks[13], (L, D_MODEL, N_GROUPS), D_MODEL ** -0.5),
        b_router_group=nrm(ks[14], (L, N_GROUPS), 0.01),
        w_router_expert=nrm(ks[15], (L, D_MODEL, N_EXPERTS), D_MODEL ** -0.5),
        b_router_expert=nrm(ks[16], (L, N_EXPERTS), 0.01),
        w_exp_gate=nrm(ks[17], (L, N_EXPERTS, D_MODEL, EXPERT_FF), D_MODEL ** -0.5),
        w_exp_up=nrm(ks[18], (L, N_EXPERTS, D_MODEL, EXPERT_FF), D_MODEL ** -0.5),
        w_exp_down=nrm(ks[19], (L, N_EXPERTS, EXPERT_FF, D_MODEL), EXPERT_FF ** -0.5),
        ple_gate_norm_g=gain(ks[20], (L, D_MODEL)),
        w_ple_gate=nrm(ks[21], (L, D_MODEL, D_MODEL), D_MODEL ** -0.5),
        w_ple=nrm(ks[22], (L, PLE_DIM, D_MODEL), PLE_DIM ** -0.5),
        ple_norm_g=gain(ks[23], (L, D_MODEL)),
    )


def reference(x, p, mix_norm_g, w_in, ret_gn_g, ret_gn_b, w_ret_o, q_norm_g, k_norm_g, attn_sinks,
              w_swa_o, w_out, ffn_norm_g, w_router_group, b_router_group, w_router_expert,
              b_router_expert, w_exp_gate, w_exp_up, w_exp_down, ple_gate_norm_g, w_ple_gate,
              w_ple, ple_norm_g):
    b, s, _ = x.shape
    split_at = [int(c) for c in np.cumsum(IN_SPLITS)[:-1]]
    for i in range(DEPTH):
        h = rms_norm(x, mix_norm_g[i])
        z = h @ w_in[i]
        rq, rk, rv, rg, sq, sk, sv, gate_r, gate_s = jnp.split(z, split_at, axis=-1)
        y_r = retention_chunkwise(rq.reshape(b, s, RET_HEADS, RET_DK),
                                  rk.reshape(b, s, RET_HEADS, RET_DK),
                                  rv.reshape(b, s, RET_HEADS, RET_DV))
        y_r = (jax.nn.silu(rg) * head_group_norm(y_r, ret_gn_g[i], ret_gn_b[i])) @ w_ret_o[i]
        q = rms_norm(sq.reshape(b, s, SWA_HEADS, SWA_HD), q_norm_g[i])
        k = rms_norm(sk.reshape(b, s, SWA_KV_HEADS, SWA_HD), k_norm_g[i])
        v = sv.reshape(b, s, SWA_KV_HEADS, SWA_HD)
        y_s = sliding_window_gqa(q, k, v, attn_sinks[i]) @ w_swa_o[i]
        merged = jax.nn.sigmoid(gate_r) * y_r + jax.nn.sigmoid(gate_s) * y_s
        x = x + merged @ w_out[i]
        x = x + hierarchical_moe(rms_norm(x, ffn_norm_g[i]), w_router_group[i], b_router_group[i],
                                 w_router_expert[i], b_router_expert[i], w_exp_gate[i],
                                 w_exp_up[i], w_exp_down[i])
        ple = rms_norm(p[i] @ w_ple[i], ple_norm_g[i])
        x = x + jax.nn.sigmoid(rms_norm(x, ple_gate_norm_g[i]) @ w_ple_gate[i]) * ple
    return x
```

```python
import functools

import jax
import jax.numpy as jnp
from jax import lax
from jax.experimental import pallas as pl
from jax.experimental.pallas import tpu as pltpu

F32 = jnp.float32
BF16 = jnp.bfloat16

EPS = 1e-6
D_MODEL = 1024
PLE_DIM = 256
RET_HEADS = 4
RET_DK = 128
RET_DV = 128
RET_CHUNK = 128
SWA_HEADS = 8
SWA_KV_HEADS = 2
SWA_GROUP = SWA_HEADS // SWA_KV_HEADS
SWA_HD = 64
WINDOW = 128
N_GROUPS = 4
EXPERTS_PER_GROUP = 8
N_EXPERTS = N_GROUPS * EXPERTS_PER_GROUP
EXPERT_FF = 256

RET_W = RET_HEADS * RET_DK
SWA_Q = SWA_HEADS * SWA_HD
SWA_KV = SWA_KV_HEADS * SWA_HD
Z_RET_W = 4 * RET_W
Z_SKV_W = 2 * SWA_KV
Z_GATE_W = 2 * D_MODEL
IN_WIDTH = Z_RET_W + SWA_Q + Z_SKV_W + Z_GATE_W

ROUTE_LANES = 128
NEG_INF = -1e30

VMEM_LIMIT = 56 * 1024 * 1024

IN_TM = 512
RET_ROWS = 512
SWA_ROWS = 256
MERGE_TM = 512
MOE_TM = 1024
PLE_TM = 512


def _rms(x, g):
    ms = jnp.mean(x * x, axis=-1, keepdims=True)
    return x * lax.rsqrt(ms + EPS) * g


def _sigmoid(x):
    return 1.0 / (1.0 + jnp.exp(-x))


def _dot(a, b):
    return jnp.dot(a, b, preferred_element_type=F32)


def _dot_nt(a, b):
    return lax.dot_general(a, b, (((1,), (1,)), ((), ())), preferred_element_type=F32)


def _dot_tn(a, b):
    return lax.dot_general(a, b, (((0,), (0,)), ((), ())), preferred_element_type=F32)


def _in_proj_kernel(x_ref, g_ref, w_ref, zret_ref, zsq_ref, zskv_ref, zgate_ref):
    h = _rms(x_ref[...], g_ref[...]).astype(BF16)

    def proj(c0, width):
        return _dot(h, w_ref[:, c0:c0 + width]).astype(BF16)

    cw = 512
    for j in range(Z_RET_W // cw):
        zret_ref[:, j * cw:(j + 1) * cw] = proj(j * cw, cw)
    zsq_ref[...] = proj(Z_RET_W, SWA_Q)
    zskv_ref[...] = proj(Z_RET_W + SWA_Q, Z_SKV_W)
    g0 = Z_RET_W + SWA_Q + Z_SKV_W
    for j in range(Z_GATE_W // cw):
        zgate_ref[:, j * cw:(j + 1) * cw] = proj(g0 + j * cw, cw)


def _in_proj(x2d, g, w_bf):
    t = x2d.shape[0]
    row = lambda w: pl.BlockSpec((IN_TM, w), lambda i: (i, 0))
    full = lambda a: pl.BlockSpec(a.shape, lambda i: (0,) * a.ndim)
    return pl.pallas_call(
        _in_proj_kernel,
        grid=(t // IN_TM,),
        in_specs=[row(D_MODEL), full(g), full(w_bf)],
        out_specs=[row(Z_RET_W), row(SWA_Q), row(Z_SKV_W), row(Z_GATE_W)],
        out_shape=[jax.ShapeDtypeStruct((t, w), BF16) for w in (Z_RET_W, SWA_Q, Z_SKV_W, Z_GATE_W)],
        compiler_params=pltpu.CompilerParams(dimension_semantics=("parallel",),
                                             vmem_limit_bytes=VMEM_LIMIT),
        name="in_proj",
    )(x2d, g, w_bf)


def _retention_kernel(z_ref, dmask_ref, zeta_ref, xi_ref, cdec_ref, gng_ref, gnb_ref, o_ref, state_ref):
    @pl.when(pl.program_id(1) == 0)
    def _():
        state_ref[...] = jnp.zeros_like(state_ref)

    def chunk(c, carry):
        r0 = pl.multiple_of(c * RET_CHUNK, RET_CHUNK)
        rows = pl.ds(r0, RET_CHUNK)
        for h in range(RET_HEADS):
            cols = lambda part: slice(part * RET_W + h * RET_DK, part * RET_W + (h + 1) * RET_DK)
            q = z_ref[rows, cols(0)]
            k = z_ref[rows, cols(1)]
            v = z_ref[rows, cols(2)]
            gate = z_ref[rows, cols(3)].astype(F32)
            state = state_ref[h]
            scores = _dot_nt(q, k) * dmask_ref[h]
            inner = _dot(scores.astype(BF16), v)
            cross = _dot(q, state.astype(BF16)) * xi_ref[h]
            y = inner + cross
            kz = (k.astype(F32) * zeta_ref[h]).astype(BF16)
            state_ref[h] = cdec_ref[h] * state + _dot_tn(kz, v)
            mu = jnp.mean(y, axis=-1, keepdims=True)
            d = y - mu
            var = jnp.mean(d * d, axis=-1, keepdims=True)
            hs = slice(h * RET_DV, (h + 1) * RET_DV)
            yn = d * lax.rsqrt(var + EPS) * gng_ref[:, hs] + gnb_ref[:, hs]
            o_ref[rows, hs] = (gate * _sigmoid(gate) * yn).astype(BF16)
        return carry

    lax.fori_loop(0, RET_ROWS // RET_CHUNK, chunk, 0)


def _retention_tables():
    h = RET_HEADS
    c = RET_CHUNK
    log_gamma = jnp.log1p(-jnp.exp2(-5.0 - jnp.arange(h, dtype=F32)))
    pos = jnp.arange(c, dtype=F32)
    diff = pos[:, None] - pos[None, :]
    decay = jnp.where(diff[None] >= 0.0,
                      jnp.exp(jnp.maximum(diff, 0.0)[None] * log_gamma[:, None, None]), 0.0)
    scale = RET_DK ** -0.5
    dmask = decay * scale
    zeta = jnp.exp((c - 1.0 - pos)[None, :] * log_gamma[:, None]) * scale
    xi = jnp.exp((pos + 1.0)[None, :] * log_gamma[:, None])
    cdec = jnp.exp(c * log_gamma)
    bc = lambda a: jnp.broadcast_to(a[:, :, None], (h, c, c))
    return dmask, bc(zeta), bc(xi), jnp.broadcast_to(cdec[:, None, None], (h, c, c))


def _retention(z_ret, gn_g, gn_b, batch, seq):
    z3 = z_ret.reshape(batch, seq, Z_RET_W)
    tables = _retention_tables()
    tab_spec = pl.BlockSpec((RET_HEADS, RET_CHUNK, RET_CHUNK), lambda b, i: (0, 0, 0))
    vec_spec = pl.BlockSpec((1, RET_W), lambda b, i: (0, 0))
    out = pl.pallas_call(
        _retention_kernel,
        grid=(batch, seq // RET_ROWS),
        in_specs=[pl.BlockSpec((None, RET_ROWS, Z_RET_W), lambda b, i: (b, i, 0)),
                  tab_spec, tab_spec, tab_spec, tab_spec, vec_spec, vec_spec],
        out_specs=pl.BlockSpec((None, RET_ROWS, RET_W), lambda b, i: (b, i, 0)),
        out_shape=jax.ShapeDtypeStruct((batch, seq, RET_W), BF16),
        scratch_shapes=[pltpu.VMEM((RET_HEADS, RET_DK, RET_DV), F32)],
        compiler_params=pltpu.CompilerParams(dimension_semantics=("parallel", "arbitrary"),
                                             vmem_limit_bytes=VMEM_LIMIT),
        name="retention",
    )(z3, *tables, gn_g, gn_b)
    return out.reshape(batch * seq, RET_W)


def _swa_kernel(sinks_ref, q_ref, kvp_ref, kvc_ref, qg_ref, kg_ref, o_ref):
    first = pl.program_id(1) == 0
    qi = lax.broadcasted_iota(jnp.int32, (WINDOW, WINDOW), 0)
    kj = lax.broadcasted_iota(jnp.int32, (WINDOW, WINDOW), 1)
    cur_ok = kj <= qi
    prev_ok = kj > qi
    rel_cur = (qi - kj).astype(F32)
    rel_prev = rel_cur + float(WINDOW)
    qg = qg_ref[...]
    kg = kg_ref[...]
    scale = SWA_HD ** -0.5

    def norm_heads(kv_rows):
        return [_rms(kv_rows[:, kh * SWA_HD:(kh + 1) * SWA_HD].astype(F32), kg).astype(BF16)
                for kh in range(SWA_KV_HEADS)]

    def values(kv_rows):
        return [kv_rows[:, SWA_KV + kh * SWA_HD:SWA_KV + (kh + 1) * SWA_HD] for kh in range(SWA_KV_HEADS)]

    kv_prev = kvp_ref[...]
    k_prev, v_prev = norm_heads(kv_prev), values(kv_prev)
    for n in range(SWA_ROWS // WINDOW):
        rows = slice(n * WINDOW, (n + 1) * WINDOW)
        kv_cur = kvc_ref[rows, :]
        k_cur, v_cur = norm_heads(kv_cur), values(kv_cur)
        p_ok = jnp.logical_and(prev_ok, jnp.logical_not(first)) if n == 0 else prev_ok
        for kh in range(SWA_KV_HEADS):
            for g in range(SWA_GROUP):
                hd = kh * SWA_GROUP + g
                slope = 2.0 ** (-8.0 * (hd + 1) / SWA_HEADS)
                sink = sinks_ref[hd]
                hs = slice(hd * SWA_HD, (hd + 1) * SWA_HD)
                qn = (_rms(q_ref[rows, hs].astype(F32), qg) * scale).astype(BF16)
                sp = jnp.where(p_ok, _dot_nt(qn, k_prev[kh]) - slope * rel_prev, NEG_INF)
                sc = jnp.where(cur_ok, _dot_nt(qn, k_cur[kh]) - slope * rel_cur, NEG_INF)
                m = jnp.maximum(jnp.maximum(jnp.max(sp, axis=-1, keepdims=True),
                                            jnp.max(sc, axis=-1, keepdims=True)), sink)
                pp = jnp.exp(sp - m)
                pc = jnp.exp(sc - m)
                denom = (jnp.sum(pp, axis=-1, keepdims=True) + jnp.sum(pc, axis=-1, keepdims=True)
                         + jnp.exp(sink - m))
                acc = _dot(pp.astype(BF16), v_prev[kh]) + _dot(pc.astype(BF16), v_cur[kh])
                o_ref[rows, hs] = (acc / denom).astype(BF16)
        k_prev, v_prev = k_cur, v_cur


def _swa(z_sq, z_skv, sinks, q_g, k_g, batch, seq):
    q3 = z_sq.reshape(batch, seq, SWA_Q)
    kv3 = z_skv.reshape(batch, seq, Z_SKV_W)
    per = SWA_ROWS // WINDOW
    vec = pl.BlockSpec((1, SWA_HD), lambda b, i: (0, 0))
    out = pl.pallas_call(
        _swa_kernel,
        grid=(batch, seq // SWA_ROWS),
        in_specs=[pl.BlockSpec(memory_space=pltpu.SMEM),
                  pl.BlockSpec((None, SWA_ROWS, SWA_Q), lambda b, i: (b, i, 0)),
                  pl.BlockSpec((None, WINDOW, Z_SKV_W), lambda b, i: (b, jnp.maximum(i * per - 1, 0), 0)),
                  pl.BlockSpec((None, SWA_ROWS, Z_SKV_W), lambda b, i: (b, i, 0)),
                  vec, vec],
        out_specs=pl.BlockSpec((None, SWA_ROWS, SWA_Q), lambda b, i: (b, i, 0)),
        out_shape=jax.ShapeDtypeStruct((batch, seq, SWA_Q), BF16),
        compiler_params=pltpu.CompilerParams(dimension_semantics=("parallel", "arbitrary"),
                                             vmem_limit_bytes=VMEM_LIMIT),
        name="swa",
    )(sinks, q3, kv3, kv3, q_g, k_g)
    return out.reshape(batch * seq, SWA_Q)


def _merge_kernel(yr_ref, ys_ref, zg_ref, x_ref, wro_ref, wso_ref, wout_ref, fg_ref, wr_ref, br_ref,
                  x1_ref, h2_ref, comb_ref):
    a = _dot(yr_ref[...], wro_ref[...])
    b = _dot(ys_ref[...], wso_ref[...])
    gate_r = zg_ref[:, :D_MODEL].astype(F32)
    gate_s = zg_ref[:, D_MODEL:].astype(F32)
    merged = (_sigmoid(gate_r) * a + _sigmoid(gate_s) * b).astype(BF16)
    x1 = x_ref[...] + _dot(merged, wout_ref[...])
    x1_ref[...] = x1
    h2 = _rms(x1, fg_ref[...])
    h2_ref[...] = h2.astype(BF16)

    logits = jnp.dot(h2, wr_ref[...], precision=lax.Precision.HIGHEST,
                     preferred_element_type=F32) + br_ref[...]
    lane = lax.broadcasted_iota(jnp.int32, logits.shape, 1)
    big = jnp.int32(ROUTE_LANES)
    gl = jnp.where(lane < N_GROUPS, logits, NEG_INF)
    gmax = jnp.max(gl, axis=-1, keepdims=True)
    g_w = 1.0 / jnp.sum(jnp.exp(gl - gmax), axis=-1, keepdims=True)
    g_sel = jnp.min(jnp.where(gl == gmax, lane, big), axis=-1, keepdims=True)
    e_lane = lane - N_GROUPS
    in_group = jnp.logical_and(jnp.logical_and(e_lane >= 0, e_lane < N_EXPERTS),
                               (e_lane >> 3) == g_sel)
    el = jnp.where(in_group, logits, NEG_INF)
    m1 = jnp.max(el, axis=-1, keepdims=True)
    i1 = jnp.min(jnp.where(el == m1, lane, big), axis=-1, keepdims=True)
    el2 = jnp.where(lane == i1, NEG_INF, el)
    m2 = jnp.max(el2, axis=-1, keepdims=True)
    i2 = jnp.min(jnp.where(el2 == m2, lane, big), axis=-1, keepdims=True)
    e2 = jnp.exp(m2 - m1)
    c1 = g_w / (1.0 + e2)
    c2 = g_w * e2 / (1.0 + e2)
    comb_ref[...] = jnp.where(lane == i1, c1, 0.0) + jnp.where(lane == i2, c2, 0.0)


def _merge(y_r, y_s, z_gate, x2d, w_ret_o, w_swa_o, w_out, ffn_g, w_router, b_router):
    t = x2d.shape[0]
    row = lambda w: pl.BlockSpec((MERGE_TM, w), lambda i: (i, 0))
    full = lambda a: pl.BlockSpec(a.shape, lambda i: (0,) * a.ndim)
    return pl.pallas_call(
        _merge_kernel,
        grid=(t // MERGE_TM,),
        in_specs=[row(RET_W), row(SWA_Q), row(Z_GATE_W), row(D_MODEL),
                  full(w_ret_o), full(w_swa_o), full(w_out), full(ffn_g), full(w_router), full(b_router)],
        out_specs=[row(D_MODEL), row(D_MODEL), row(ROUTE_LANES)],
        out_shape=[jax.ShapeDtypeStruct((t, D_MODEL), F32),
                   jax.ShapeDtypeStruct((t, D_MODEL), BF16),
                   jax.ShapeDtypeStruct((t, ROUTE_LANES), F32)],
        compiler_params=pltpu.CompilerParams(dimension_semantics=("parallel",),
                                             vmem_limit_bytes=VMEM_LIMIT),
        name="merge_router",
    )(y_r, y_s, z_gate, x2d, w_ret_o, w_swa_o, w_out, ffn_g, w_router, b_router)


def _moe_kernel(x1_ref, h_ref, comb_ref, wg_ref, wu_ref, wd_ref, o_ref):
    e = pl.program_id(1)

    @pl.when(e == 0)
    def _():
        o_ref[...] = x1_ref[...]

    h = h_ref[...]
    comb = comb_ref[...]
    lane = lax.broadcasted_iota(jnp.int32, comb.shape, 1)
    w_e = jnp.sum(jnp.where(lane == e + N_GROUPS, comb, 0.0), axis=-1, keepdims=True)
    hg = _dot(h, wg_ref[...])
    hu = _dot(h, wu_ref[...])
    hid = (hg * _sigmoid(hg) * hu * w_e).astype(BF16)
    o_ref[...] += _dot(hid, wd_ref[...])


def _moe(x1, h2, comb, wg, wu, wd):
    t = x1.shape[0]
    return pl.pallas_call(
        _moe_kernel,
        grid=(t // MOE_TM, N_EXPERTS),
        in_specs=[pl.BlockSpec((MOE_TM, D_MODEL), lambda i, e: (i, 0)),
                  pl.BlockSpec((MOE_TM, D_MODEL), lambda i, e: (i, 0)),
                  pl.BlockSpec((MOE_TM, ROUTE_LANES), lambda i, e: (i, 0)),
                  pl.BlockSpec((None, D_MODEL, EXPERT_FF), lambda i, e: (e, 0, 0)),
                  pl.BlockSpec((None, D_MODEL, EXPERT_FF), lambda i, e: (e, 0, 0)),
                  pl.BlockSpec((None, EXPERT_FF, D_MODEL), lambda i, e: (e, 0, 0))],
        out_specs=pl.BlockSpec((MOE_TM, D_MODEL), lambda i, e: (i, 0)),
        out_shape=jax.ShapeDtypeStruct((t, D_MODEL), F32),
        compiler_params=pltpu.CompilerParams(dimension_semantics=("parallel", "arbitrary"),
                                             vmem_limit_bytes=VMEM_LIMIT),
        name="moe",
    )(x1, h2, comb, wg, wu, wd)


def _ple_kernel(x_ref, p_ref, wp_ref, pg_ref, gg_ref, wgate_ref, o_ref):
    x = x_ref[...]
    ple = _rms(_dot(p_ref[...].astype(BF16), wp_ref[...]), pg_ref[...])
    gate = _sigmoid(_dot(_rms(x, gg_ref[...]).astype(BF16), wgate_ref[...]))
    o_ref[...] = x + gate * ple


def _ple(x2, p2d, w_ple, ple_g, gate_g, w_gate):
    t = x2.shape[0]
    row = lambda w: pl.BlockSpec((PLE_TM, w), lambda i: (i, 0))
    full = lambda a: pl.BlockSpec(a.shape, lambda i: (0,) * a.ndim)
    return pl.pallas_call(
        _ple_kernel,
        grid=(t // PLE_TM,),
        in_specs=[row(D_MODEL), row(PLE_DIM), full(w_ple), full(ple_g), full(gate_g), full(w_gate)],
        out_specs=row(D_MODEL),
        out_shape=jax.ShapeDtypeStruct((t, D_MODEL), F32),
        compiler_params=pltpu.CompilerParams(dimension_semantics=("parallel",),
                                             vmem_limit_bytes=VMEM_LIMIT),
        name="ple",
    )(x2, p2d, w_ple, ple_g, gate_g, w_gate)


def kernel(x, p, mix_norm_g, w_in, ret_gn_g, ret_gn_b, w_ret_o, q_norm_g, k_norm_g, attn_sinks,
           w_swa_o, w_out, ffn_norm_g, w_router_group, b_router_group, w_router_expert,
           b_router_expert, w_exp_gate, w_exp_up, w_exp_down, ple_gate_norm_g, w_ple_gate,
           w_ple, ple_norm_g):
    batch, seq, d = x.shape
    t = batch * seq
    depth = w_in.shape[0]
    row = lambda a: a.reshape(1, -1)
    x2d = x.reshape(t, d)
    for i in range(depth):
        z_ret, z_sq, z_skv, z_gate = _in_proj(x2d, row(mix_norm_g[i]), w_in[i].astype(BF16))
        y_r = _retention(z_ret, row(ret_gn_g[i]), row(ret_gn_b[i]), batch, seq)
        y_s = _swa(z_sq, z_skv, attn_sinks[i], row(q_norm_g[i]), row(k_norm_g[i]), batch, seq)
        pad = ROUTE_LANES - N_GROUPS - N_EXPERTS
        w_router = jnp.pad(jnp.concatenate([w_router_group[i], w_router_expert[i]], axis=1), ((0, 0), (0, pad)))
        b_router = jnp.pad(jnp.concatenate([b_router_group[i], b_router_expert[i]]), (0, pad)).reshape(1, -1)
        x1, h2, comb = _merge(y_r, y_s, z_gate, x2d, w_ret_o[i].astype(BF16), w_swa_o[i].astype(BF16),
                              w_out[i].astype(BF16), row(ffn_norm_g[i]), w_router, b_router)
        x2 = _moe(x1, h2, comb, w_exp_gate[i].astype(BF16), w_exp_up[i].astype(BF16),
                  w_exp_down[i].astype(BF16))
        x2d = _ple(x2, p[i].reshape(t, PLE_DIM), w_ple[i].astype(BF16), row(ple_norm_g[i]),
                   row(ple_gate_norm_g[i]), w_ple_gate[i].astype(BF16))
    return x2d.reshape(batch, seq, d)
```

```python
import functools

import jax
import jax.numpy as jnp
from jax import lax
from jax.experimental import pallas as pl
from jax.experimental.pallas import tpu as pltpu
from jax.experimental.pallas import tpu_sc as plsc

F32 = jnp.float32
BF16 = jnp.bfloat16
I32 = jnp.int32

EPS = 1e-6
D_MODEL = 1024
PLE_DIM = 256
RET_HEADS = 4
RET_DK = 128
RET_DV = 128
RET_CHUNK = 128
SWA_HEADS = 8
SWA_KV_HEADS = 2
SWA_GROUP = SWA_HEADS // SWA_KV_HEADS
SWA_HD = 64
WINDOW = 128
N_GROUPS = 4
EXPERTS_PER_GROUP = 8
N_EXPERTS = N_GROUPS * EXPERTS_PER_GROUP
EXPERT_FF = 256

RET_W = RET_HEADS * RET_DK
SWA_Q = SWA_HEADS * SWA_HD
SWA_KV = SWA_KV_HEADS * SWA_HD
Z_RET_W = 4 * RET_W
Z_SKV_W = 2 * SWA_KV
Z_GATE_W = 2 * D_MODEL
IN_WIDTH = Z_RET_W + SWA_Q + Z_SKV_W + Z_GATE_W

ROUTE_LANES = 128
LANE_GROUP0 = N_EXPERTS
LANE_E1, LANE_E2, LANE_C1, LANE_C2 = 32, 33, 34, 35
NEG_INF = -1e30

HALF_D = D_MODEL // 2
ROW_TILE = 256
SC_CORES = 2
SC_SUBCORES = 16
SC_WORKERS = SC_CORES * SC_SUBCORES
SC_CHUNK = 128

VMEM_LIMIT = 56 * 1024 * 1024

IN_TM = 512
RET_ROWS = 512
SWA_ROWS = 256
MERGE_TM = 512
ROUTE_TM = 512
PLE_TM = 512


def _rms(x, g):
    ms = jnp.mean(x * x, axis=-1, keepdims=True)
    return x * lax.rsqrt(ms + EPS) * g


def _sigmoid(x):
    return 1.0 / (1.0 + jnp.exp(-x))


def _dot(a, b):
    return jnp.dot(a, b, preferred_element_type=F32)


def _dot_nt(a, b):
    return lax.dot_general(a, b, (((1,), (1,)), ((), ())), preferred_element_type=F32)


def _dot_tn(a, b):
    return lax.dot_general(a, b, (((0,), (0,)), ((), ())), preferred_element_type=F32)


def _pack_halves(v):
    hi = lax.bitcast_convert_type(v[:, :HALF_D].astype(BF16).astype(F32), jnp.uint32)
    lo = lax.bitcast_convert_type(v[:, HALF_D:].astype(BF16).astype(F32), jnp.uint32)
    return lax.bitcast_convert_type(hi | (lo >> 16), I32)


def _unpack_halves(p):
    u = lax.bitcast_convert_type(p, jnp.uint32)
    hi = lax.bitcast_convert_type(u & jnp.uint32(0xFFFF0000), F32)
    lo = lax.bitcast_convert_type(u << 16, F32)
    return hi, lo


def _in_proj_kernel(x_ref, g_ref, w_ref, zret_ref, zsq_ref, zskv_ref, zgate_ref):
    h = _rms(x_ref[...], g_ref[...]).astype(BF16)

    def proj(c0, width):
        return _dot(h, w_ref[:, c0:c0 + width]).astype(BF16)

    cw = 512
    for j in range(Z_RET_W // cw):
        zret_ref[:, j * cw:(j + 1) * cw] = proj(j * cw, cw)
    zsq_ref[...] = proj(Z_RET_W, SWA_Q)
    zskv_ref[...] = proj(Z_RET_W + SWA_Q, Z_SKV_W)
    g0 = Z_RET_W + SWA_Q + Z_SKV_W
    for j in range(Z_GATE_W // cw):
        zgate_ref[:, j * cw:(j + 1) * cw] = proj(g0 + j * cw, cw)


def _in_proj(x2d, g, w_bf):
    t = x2d.shape[0]
    row = lambda w: pl.BlockSpec((IN_TM, w), lambda i: (i, 0))
    full = lambda a: pl.BlockSpec(a.shape, lambda i: (0,) * a.ndim)
    return pl.pallas_call(
        _in_proj_kernel,
        grid=(t // IN_TM,),
        in_specs=[row(D_MODEL), full(g), full(w_bf)],
        out_specs=[row(Z_RET_W), row(SWA_Q), row(Z_SKV_W), row(Z_GATE_W)],
        out_shape=[jax.ShapeDtypeStruct((t, w), BF16) for w in (Z_RET_W, SWA_Q, Z_SKV_W, Z_GATE_W)],
        compiler_params=pltpu.CompilerParams(dimension_semantics=("parallel",),
                                             vmem_limit_bytes=VMEM_LIMIT),
        name="in_proj",
    )(x2d, g, w_bf)


def _retention_kernel(z_ref, dmask_ref, zeta_ref, xi_ref, cdec_ref, gng_ref, gnb_ref, o_ref, state_ref):
    @pl.when(pl.program_id(1) == 0)
    def _():
        state_ref[...] = jnp.zeros_like(state_ref)

    def chunk(c, carry):
        r0 = pl.multiple_of(c * RET_CHUNK, RET_CHUNK)
        rows = pl.ds(r0, RET_CHUNK)
        for h in range(RET_HEADS):
            cols = lambda part: slice(part * RET_W + h * RET_DK, part * RET_W + (h + 1) * RET_DK)
            q = z_ref[rows, cols(0)]
            k = z_ref[rows, cols(1)]
            v = z_ref[rows, cols(2)]
            gate = z_ref[rows, cols(3)].astype(F32)
            state = state_ref[h]
            scores = _dot_nt(q, k) * dmask_ref[h]
            inner = _dot(scores.astype(BF16), v)
            cross = _dot(q, state.astype(BF16)) * xi_ref[h]
            y = inner + cross
            kz = (k.astype(F32) * zeta_ref[h]).astype(BF16)
            state_ref[h] = cdec_ref[h] * state + _dot_tn(kz, v)
            mu = jnp.mean(y, axis=-1, keepdims=True)
            d = y - mu
            var = jnp.mean(d * d, axis=-1, keepdims=True)
            hs = slice(h * RET_DV, (h + 1) * RET_DV)
            yn = d * lax.rsqrt(var + EPS) * gng_ref[:, hs] + gnb_ref[:, hs]
            o_ref[rows, hs] = (gate * _sigmoid(gate) * yn).astype(BF16)
        return carry

    lax.fori_loop(0, RET_ROWS // RET_CHUNK, chunk, 0)


def _retention_tables():
    h = RET_HEADS
    c = RET_CHUNK
    log_gamma = jnp.log1p(-jnp.exp2(-5.0 - jnp.arange(h, dtype=F32)))
    pos = jnp.arange(c, dtype=F32)
    diff = pos[:, None] - pos[None, :]
    decay = jnp.where(diff[None] >= 0.0,
                      jnp.exp(jnp.maximum(diff, 0.0)[None] * log_gamma[:, None, None]), 0.0)
    scale = RET_DK ** -0.5
    dmask = decay * scale
    zeta = jnp.exp((c - 1.0 - pos)[None, :] * log_gamma[:, None]) * scale
    xi = jnp.exp((pos + 1.0)[None, :] * log_gamma[:, None])
    cdec = jnp.exp(c * log_gamma)
    bc = lambda a: jnp.broadcast_to(a[:, :, None], (h, c, c))
    return dmask, bc(zeta), bc(xi), jnp.broadcast_to(cdec[:, None, None], (h, c, c))


def _retention(z_ret, gn_g, gn_b, batch, seq):
    z3 = z_ret.reshape(batch, seq, Z_RET_W)
    tables = _retention_tables()
    tab_spec = pl.BlockSpec((RET_HEADS, RET_CHUNK, RET_CHUNK), lambda b, i: (0, 0, 0))
    vec_spec = pl.BlockSpec((1, RET_W), lambda b, i: (0, 0))
    out = pl.pallas_call(
        _retention_kernel,
        grid=(batch, seq // RET_ROWS),
        in_specs=[pl.BlockSpec((None, RET_ROWS, Z_RET_W), lambda b, i: (b, i, 0)),
                  tab_spec, tab_spec, tab_spec, tab_spec, vec_spec, vec_spec],
        out_specs=pl.BlockSpec((None, RET_ROWS, RET_W), lambda b, i: (b, i, 0)),
        out_shape=jax.ShapeDtypeStruct((batch, seq, RET_W), BF16),
        scratch_shapes=[pltpu.VMEM((RET_HEADS, RET_DK, RET_DV), F32)],
        compiler_params=pltpu.CompilerParams(dimension_semantics=("parallel", "arbitrary"),
                                             vmem_limit_bytes=VMEM_LIMIT),
        name="retention",
    )(z3, *tables, gn_g, gn_b)
    return out.reshape(batch * seq, RET_W)


def _swa_kernel(sinks_ref, q_ref, kvp_ref, kvc_ref, qg_ref, kg_ref, o_ref):
    first = pl.program_id(1) == 0
    qi = lax.broadcasted_iota(jnp.int32, (WINDOW, WINDOW), 0)
    kj = lax.broadcasted_iota(jnp.int32, (WINDOW, WINDOW), 1)
    cur_ok = kj <= qi
    prev_ok = kj > qi
    rel_cur = (qi - kj).astype(F32)
    rel_prev = rel_cur + float(WINDOW)
    qg = qg_ref[...]
    kg = kg_ref[...]
    scale = SWA_HD ** -0.5

    def norm_heads(kv_rows):
        return [_rms(kv_rows[:, kh * SWA_HD:(kh + 1) * SWA_HD].astype(F32), kg).astype(BF16)
                for kh in range(SWA_KV_HEADS)]

    def values(kv_rows):
        return [kv_rows[:, SWA_KV + kh * SWA_HD:SWA_KV + (kh + 1) * SWA_HD] for kh in range(SWA_KV_HEADS)]

    kv_prev = kvp_ref[...]
    k_prev, v_prev = norm_heads(kv_prev), values(kv_prev)
    for n in range(SWA_ROWS // WINDOW):
        rows = slice(n * WINDOW, (n + 1) * WINDOW)
        kv_cur = kvc_ref[rows, :]
        k_cur, v_cur = norm_heads(kv_cur), values(kv_cur)
        p_ok = jnp.logical_and(prev_ok, jnp.logical_not(first)) if n == 0 else prev_ok
        for kh in range(SWA_KV_HEADS):
            for g in range(SWA_GROUP):
                hd = kh * SWA_GROUP + g
                slope = 2.0 ** (-8.0 * (hd + 1) / SWA_HEADS)
                sink = sinks_ref[hd]
                hs = slice(hd * SWA_HD, (hd + 1) * SWA_HD)
                qn = (_rms(q_ref[rows, hs].astype(F32), qg) * scale).astype(BF16)
                sp = jnp.where(p_ok, _dot_nt(qn, k_prev[kh]) - slope * rel_prev, NEG_INF)
                sc = jnp.where(cur_ok, _dot_nt(qn, k_cur[kh]) - slope * rel_cur, NEG_INF)
                m = jnp.maximum(jnp.maximum(jnp.max(sp, axis=-1, keepdims=True),
                                            jnp.max(sc, axis=-1, keepdims=True)), sink)
                pp = jnp.exp(sp - m)
                pc = jnp.exp(sc - m)
                denom = (jnp.sum(pp, axis=-1, keepdims=True) + jnp.sum(pc, axis=-1, keepdims=True)
                         + jnp.exp(sink - m))
                acc = _dot(pp.astype(BF16), v_prev[kh]) + _dot(pc.astype(BF16), v_cur[kh])
                o_ref[rows, hs] = (acc / denom).astype(BF16)
        k_prev, v_prev = k_cur, v_cur


def _swa(z_sq, z_skv, sinks, q_g, k_g, batch, seq):
    q3 = z_sq.reshape(batch, seq, SWA_Q)
    kv3 = z_skv.reshape(batch, seq, Z_SKV_W)
    per = SWA_ROWS // WINDOW
    vec = pl.BlockSpec((1, SWA_HD), lambda b, i: (0, 0))
    out = pl.pallas_call(
        _swa_kernel,
        grid=(batch, seq // SWA_ROWS),
        in_specs=[pl.BlockSpec(memory_space=pltpu.SMEM),
                  pl.BlockSpec((None, SWA_ROWS, SWA_Q), lambda b, i: (b, i, 0)),
                  pl.BlockSpec((None, WINDOW, Z_SKV_W), lambda b, i: (b, jnp.maximum(i * per - 1, 0), 0)),
                  pl.BlockSpec((None, SWA_ROWS, Z_SKV_W), lambda b, i: (b, i, 0)),
                  vec, vec],
        out_specs=pl.BlockSpec((None, SWA_ROWS, SWA_Q), lambda b, i: (b, i, 0)),
        out_shape=jax.ShapeDtypeStruct((batch, seq, SWA_Q), BF16),
        compiler_params=pltpu.CompilerParams(dimension_semantics=("parallel", "arbitrary"),
                                             vmem_limit_bytes=VMEM_LIMIT),
        name="swa",
    )(sinks, q3, kv3, kv3, q_g, k_g)
    return out.reshape(batch * seq, SWA_Q)


def _merge_kernel(yr_ref, ys_ref, zg_ref, x_ref, wro_ref, wso_ref, wout_ref, fg_ref, wr_ref, br_ref,
                  x1_ref, hp_ref, route_ref, cnt_ref):
    a = _dot(yr_ref[...], wro_ref[...])
    b = _dot(ys_ref[...], wso_ref[...])
    gate_r = zg_ref[:, :D_MODEL].astype(F32)
    gate_s = zg_ref[:, D_MODEL:].astype(F32)
    merged = (_sigmoid(gate_r) * a + _sigmoid(gate_s) * b).astype(BF16)
    x1 = x_ref[...] + _dot(merged, wout_ref[...])
    x1_ref[...] = x1
    h2 = _rms(x1, fg_ref[...])
    hp_ref[...] = _pack_halves(h2)

    logits = jnp.dot(h2, wr_ref[...], precision=lax.Precision.HIGHEST,
                     preferred_element_type=F32) + br_ref[...]
    lane = lax.broadcasted_iota(jnp.int32, logits.shape, 1)
    big = jnp.int32(ROUTE_LANES)
    is_group = jnp.logical_and(lane >= LANE_GROUP0, lane < LANE_GROUP0 + N_GROUPS)
    gl = jnp.where(is_group, logits, NEG_INF)
    gmax = jnp.max(gl, axis=-1, keepdims=True)
    g_w = 1.0 / jnp.sum(jnp.exp(gl - gmax), axis=-1, keepdims=True)
    g_sel = jnp.min(jnp.where(gl == gmax, lane, big), axis=-1, keepdims=True) - LANE_GROUP0
    in_group = jnp.logical_and(lane < N_EXPERTS, (lane >> 3) == g_sel)
    el = jnp.where(in_group, logits, NEG_INF)
    m1 = jnp.max(el, axis=-1, keepdims=True)
    i1 = jnp.min(jnp.where(el == m1, lane, big), axis=-1, keepdims=True)
    el2 = jnp.where(lane == i1, NEG_INF, el)
    m2 = jnp.max(el2, axis=-1, keepdims=True)
    i2 = jnp.min(jnp.where(el2 == m2, lane, big), axis=-1, keepdims=True)
    e2 = jnp.exp(m2 - m1)
    c1 = g_w / (1.0 + e2)
    c2 = g_w * e2 / (1.0 + e2)
    chosen = jnp.where(jnp.logical_or(lane == i1, lane == i2), 1.0, 0.0)
    route_ref[...] = (chosen
                      + jnp.where(lane == LANE_E1, i1.astype(F32), 0.0)
                      + jnp.where(lane == LANE_E2, i2.astype(F32), 0.0)
                      + jnp.where(lane == LANE_C1, c1, 0.0)
                      + jnp.where(lane == LANE_C2, c2, 0.0))
    cnt_ref[...] = jnp.sum(chosen, axis=0, keepdims=True)


def _merge(y_r, y_s, z_gate, x2d, w_ret_o, w_swa_o, w_out, ffn_g, w_router, b_router):
    t = x2d.shape[0]
    row = lambda w: pl.BlockSpec((MERGE_TM, w), lambda i: (i, 0))
    full = lambda a: pl.BlockSpec(a.shape, lambda i: (0,) * a.ndim)
    return pl.pallas_call(
        _merge_kernel,
        grid=(t // MERGE_TM,),
        in_specs=[row(RET_W), row(SWA_Q), row(Z_GATE_W), row(D_MODEL),
                  full(w_ret_o), full(w_swa_o), full(w_out), full(ffn_g), full(w_router), full(b_router)],
        out_specs=[row(D_MODEL), row(HALF_D), row(ROUTE_LANES),
                   pl.BlockSpec((None, 1, ROUTE_LANES), lambda i: (i, 0, 0))],
        out_shape=[jax.ShapeDtypeStruct((t, D_MODEL), F32),
                   jax.ShapeDtypeStruct((t, HALF_D), I32),
                   jax.ShapeDtypeStruct((t, ROUTE_LANES), F32),
                   jax.ShapeDtypeStruct((t // MERGE_TM, 1, ROUTE_LANES), F32)],
        compiler_params=pltpu.CompilerParams(dimension_semantics=("parallel",),
                                             vmem_limit_bytes=VMEM_LIMIT),
        name="merge_router",
    )(y_r, y_s, z_gate, x2d, w_ret_o, w_swa_o, w_out, ffn_g, w_router, b_router)


def _route_pos_kernel(route_ref, base_ref, pos_ref):
    route = route_ref[...]
    lane = lax.broadcasted_iota(jnp.int32, route.shape, 1)
    lane_f = lane.astype(F32)
    chosen = jnp.where(lane < N_EXPERTS, route, 0.0)
    r = lax.broadcasted_iota(jnp.int32, (ROUTE_TM, ROUTE_TM), 0)
    c = lax.broadcasted_iota(jnp.int32, (ROUTE_TM, ROUTE_TM), 1)
    tri = jnp.where(c <= r, 1.0, 0.0).astype(BF16)
    incl = _dot(tri, chosen.astype(BF16))
    posmat = base_ref[...] + incl - 1.0
    pick = lambda ln: jnp.sum(jnp.where(lane == ln, route, 0.0), axis=-1, keepdims=True)
    pos_of = lambda e: jnp.sum(jnp.where(lane_f == e, posmat, 0.0), axis=-1, keepdims=True)
    p1 = pos_of(pick(LANE_E1))
    p2 = pos_of(pick(LANE_E2))
    rec = jnp.where(lane == 0, p1, 0.0) + jnp.where(lane == 1, p2, 0.0)
    pos_ref[...] = rec.T[:8, :].astype(I32)


def _route_pos(route, base):
    t = route.shape[0]
    n = t // ROUTE_TM
    return pl.pallas_call(
        _route_pos_kernel,
        grid=(n,),
        in_specs=[pl.BlockSpec((ROUTE_TM, ROUTE_LANES), lambda i: (i, 0)),
                  pl.BlockSpec((None, 1, ROUTE_LANES), lambda i: (i, 0, 0))],
        out_specs=pl.BlockSpec((None, 8, ROUTE_TM), lambda i: (i, 0, 0)),
        out_shape=jax.ShapeDtypeStruct((n, 8, ROUTE_TM), I32),
        compiler_params=pltpu.CompilerParams(dimension_semantics=("parallel",),
                                             vmem_limit_bytes=VMEM_LIMIT),
        name="route_pos",
    )(route, base)


def _tile_plan(cnt, n_tiles):
    counts = cnt[:, 0, :N_EXPERTS]
    total = jnp.sum(counts, axis=0)
    tiles_e = jnp.ceil(total / ROW_TILE)
    tile_end = jnp.cumsum(tiles_e)
    tile_start = tile_end - tiles_e
    offs = tile_start * ROW_TILE
    before = jnp.cumsum(counts, axis=0) - counts
    base = jnp.zeros(cnt.shape, F32).at[:, 0, :N_EXPERTS].set(offs[None, :] + before)
    tile = jnp.arange(n_tiles, dtype=F32)
    used = tile_end[-1]
    last_used = jnp.maximum(used - 1.0, 0.0)
    owner = jnp.sum((jnp.minimum(tile, last_used)[:, None] >= tile_end[None, :]).astype(I32), axis=1)
    owner = jnp.minimum(owner, N_EXPERTS - 1)
    valid = jnp.clip(total[owner] - (tile - tile_start[owner]) * ROW_TILE, 0.0, float(ROW_TILE))
    valid = jnp.where(tile < used, valid, 0.0)
    return base, owner.astype(I32), valid.astype(I32)


def _sc_mesh():
    return plsc.VectorSubcoreMesh(core_axis_name="c", subcore_axis_name="s")


def _sc_worker():
    return lax.axis_index("s") * SC_CORES + lax.axis_index("c")


def _dispatch(hp, pos1, pos2, n_rows):
    t = hp.shape[0]
    per_worker = t // SC_WORKERS
    k = per_worker // SC_CHUNK
    idx = pltpu.VMEM((k, SC_CHUNK), I32)

    @functools.partial(pl.kernel, mesh=_sc_mesh(), out_type=jax.ShapeDtypeStruct((n_rows, HALF_D), I32),
                       scratch_types=[idx, idx, pltpu.VMEM((SC_CHUNK, HALF_D), I32)], name="moe_dispatch")
    def run(h_hbm, p1_hbm, p2_hbm, xs_hbm, p1_v, p2_v, rows_v):
        wid = _sc_worker()
        pltpu.sync_copy(p1_hbm.at[wid], p1_v)
        pltpu.sync_copy(p2_hbm.at[wid], p2_v)
        for j in range(k):
            pltpu.sync_copy(h_hbm.at[pl.ds(wid * per_worker + j * SC_CHUNK, SC_CHUNK)], rows_v)
            pltpu.sync_copy(rows_v, xs_hbm.at[p1_v.at[j]])
            pltpu.sync_copy(rows_v, xs_hbm.at[p2_v.at[j]])

    return run(hp, pos1.reshape(SC_WORKERS, k, SC_CHUNK), pos2.reshape(SC_WORKERS, k, SC_CHUNK))


def _combine_gather(ys, pos1, pos2, t):
    per_worker = t // SC_WORKERS
    k = per_worker // SC_CHUNK
    idx = pltpu.VMEM((k, SC_CHUNK), I32)
    out = jax.ShapeDtypeStruct((t, HALF_D), I32)

    @functools.partial(pl.kernel, mesh=_sc_mesh(), out_type=(out, out),
                       scratch_types=[idx, idx, pltpu.VMEM((SC_CHUNK, HALF_D), I32)], name="moe_combine")
    def run(ys_hbm, p1_hbm, p2_hbm, g1_hbm, g2_hbm, p1_v, p2_v, rows_v):
        wid = _sc_worker()
        pltpu.sync_copy(p1_hbm.at[wid], p1_v)
        pltpu.sync_copy(p2_hbm.at[wid], p2_v)
        for j in range(k):
            dst = pl.ds(wid * per_worker + j * SC_CHUNK, SC_CHUNK)
            pltpu.sync_copy(ys_hbm.at[p1_v.at[j]], rows_v)
            pltpu.sync_copy(rows_v, g1_hbm.at[dst])
            pltpu.sync_copy(ys_hbm.at[p2_v.at[j]], rows_v)
            pltpu.sync_copy(rows_v, g2_hbm.at[dst])

    return run(ys, pos1.reshape(SC_WORKERS, k, SC_CHUNK), pos2.reshape(SC_WORKERS, k, SC_CHUNK))


def _experts_kernel(owner_ref, valid_ref, xs_ref, wg_ref, wu_ref, wd_ref, ys_ref, wg_bf, wu_bf, wd_bf):
    i = pl.program_id(0)
    valid = valid_ref[i]

    @pl.when(jnp.logical_or(i == 0, owner_ref[i] != owner_ref[jnp.maximum(i - 1, 0)]))
    def _():
        wg_bf[...] = wg_ref[...].astype(BF16)
        wu_bf[...] = wu_ref[...].astype(BF16)
        wd_bf[...] = wd_ref[...].astype(BF16)

    @pl.when(valid > 0)
    def _():
        row = lax.broadcasted_iota(jnp.int32, (ROW_TILE, HALF_D), 0)
        packed = jnp.where(row < valid, xs_ref[...], 0)
        hi, lo = _unpack_halves(packed)
        hi = hi.astype(BF16)
        lo = lo.astype(BF16)
        hg = _dot(hi, wg_bf[:HALF_D, :]) + _dot(lo, wg_bf[HALF_D:, :])
        hu = _dot(hi, wu_bf[:HALF_D, :]) + _dot(lo, wu_bf[HALF_D:, :])
        hid = (hg * _sigmoid(hg) * hu).astype(BF16)
        ys_ref[...] = _pack_halves(_dot(hid, wd_bf[...]))

    @pl.when(valid == 0)
    def _():
        ys_ref[...] = jnp.zeros_like(ys_ref)


def _experts(xs, owner, valid, wg, wu, wd):
    n_tiles = owner.shape[0]
    grid_spec = pltpu.PrefetchScalarGridSpec(
        num_scalar_prefetch=2,
        grid=(n_tiles,),
        in_specs=[pl.BlockSpec((ROW_TILE, HALF_D), lambda i, own, val: (i, 0)),
                  pl.BlockSpec((None, D_MODEL, EXPERT_FF), lambda i, own, val: (own[i], 0, 0)),
                  pl.BlockSpec((None, D_MODEL, EXPERT_FF), lambda i, own, val: (own[i], 0, 0)),
                  pl.BlockSpec((None, EXPERT_FF, D_MODEL), lambda i, own, val: (own[i], 0, 0))],
        out_specs=pl.BlockSpec((ROW_TILE, HALF_D), lambda i, own, val: (i, 0)),
        scratch_shapes=[pltpu.VMEM((D_MODEL, EXPERT_FF), BF16), pltpu.VMEM((D_MODEL, EXPERT_FF), BF16),
                        pltpu.VMEM((EXPERT_FF, D_MODEL), BF16)],
    )
    return pl.pallas_call(
        _experts_kernel,
        grid_spec=grid_spec,
        out_shape=jax.ShapeDtypeStruct((n_tiles * ROW_TILE, HALF_D), I32),
        compiler_params=pltpu.CompilerParams(dimension_semantics=("arbitrary",),
                                             vmem_limit_bytes=VMEM_LIMIT),
        name="experts",
    )(owner, valid, xs, wg, wu, wd)


def _ple_kernel(x1_ref, g1_ref, g2_ref, route_ref, p_ref, wp_ref, pg_ref, gg_ref, wgate_ref, o_ref):
    route = route_ref[...]
    lane = lax.broadcasted_iota(jnp.int32, route.shape, 1)
    pick = lambda ln: jnp.sum(jnp.where(lane == ln, route, 0.0), axis=-1, keepdims=True)
    c1 = pick(LANE_C1)
    c2 = pick(LANE_C2)
    a_hi, a_lo = _unpack_halves(g1_ref[...])
    b_hi, b_lo = _unpack_halves(g2_ref[...])
    x = x1_ref[...] + jnp.concatenate([c1 * a_hi + c2 * b_hi, c1 * a_lo + c2 * b_lo], axis=1)
    ple = _rms(_dot(p_ref[...].astype(BF16), wp_ref[...]), pg_ref[...])
    gate = _sigmoid(_dot(_rms(x, gg_ref[...]).astype(BF16), wgate_ref[...]))
    o_ref[...] = x + gate * ple


def _ple(x1, g1, g2, route, p2d, w_ple, ple_g, gate_g, w_gate):
    t = x1.shape[0]
    row = lambda w: pl.BlockSpec((PLE_TM, w), lambda i: (i, 0))
    full = lambda a: pl.BlockSpec(a.shape, lambda i: (0,) * a.ndim)
    return pl.pallas_call(
        _ple_kernel,
        grid=(t // PLE_TM,),
        in_specs=[row(D_MODEL), row(HALF_D), row(HALF_D), row(ROUTE_LANES), row(PLE_DIM),
                  full(w_ple), full(ple_g), full(gate_g), full(w_gate)],
        out_specs=row(D_MODEL),
        out_shape=jax.ShapeDtypeStruct((t, D_MODEL), F32),
        compiler_params=pltpu.CompilerParams(dimension_semantics=("parallel",),
                                             vmem_limit_bytes=VMEM_LIMIT),
        name="ple",
    )(x1, g1, g2, route, p2d, w_ple, ple_g, gate_g, w_gate)


def kernel(x, p, mix_norm_g, w_in, ret_gn_g, ret_gn_b, w_ret_o, q_norm_g, k_norm_g, attn_sinks,
           w_swa_o, w_out, ffn_norm_g, w_router_group, b_router_group, w_router_expert,
           b_router_expert, w_exp_gate, w_exp_up, w_exp_down, ple_gate_norm_g, w_ple_gate,
           w_ple, ple_norm_g):
    batch, seq, d = x.shape
    t = batch * seq
    depth = w_in.shape[0]
    n_tiles = 2 * t // ROW_TILE + N_EXPERTS
    row = lambda a: a.reshape(1, -1)
    x2d = x.reshape(t, d)
    for i in range(depth):
        z_ret, z_sq, z_skv, z_gate = _in_proj(x2d, row(mix_norm_g[i]), w_in[i].astype(BF16))
        y_r = _retention(z_ret, row(ret_gn_g[i]), row(ret_gn_b[i]), batch, seq)
        y_s = _swa(z_sq, z_skv, attn_sinks[i], row(q_norm_g[i]), row(k_norm_g[i]), batch, seq)
        pad = ROUTE_LANES - N_GROUPS - N_EXPERTS
        w_router = jnp.pad(jnp.concatenate([w_router_expert[i], w_router_group[i]], axis=1), ((0, 0), (0, pad)))
        b_router = jnp.pad(jnp.concatenate([b_router_expert[i], b_router_group[i]]), (0, pad)).reshape(1, -1)
        x1, hp, route, cnt = _merge(y_r, y_s, z_gate, x2d, w_ret_o[i].astype(BF16), w_swa_o[i].astype(BF16),
                                    w_out[i].astype(BF16), row(ffn_norm_g[i]), w_router, b_router)
        base, owner, valid = _tile_plan(cnt, n_tiles)
        pos = _route_pos(route, base)
        pos1 = pos[:, 0, :]
        pos2 = pos[:, 1, :]
        xs = _dispatch(hp, pos1, pos2, n_tiles * ROW_TILE)
        ys = _experts(xs, owner, valid, w_exp_gate[i], w_exp_up[i], w_exp_down[i])
        g1, g2 = _combine_gather(ys, pos1, pos2, t)
        x2d = _ple(x1, g1, g2, route, p[i].reshape(t, PLE_DIM), w_ple[i].astype(BF16), row(ple_norm_g[i]),
                   row(ple_gate_norm_g[i]), w_ple_gate[i].astype(BF16))
    return x2d.reshape(batch, seq, d)
```

```python
import functools

import jax
import jax.numpy as jnp
from jax import lax
from jax.experimental import pallas as pl
from jax.experimental.pallas import tpu as pltpu
from jax.experimental.pallas import tpu_sc as plsc

F32 = jnp.float32
BF16 = jnp.bfloat16
I32 = jnp.int32

EPS = 1e-6
D_MODEL = 1024
PLE_DIM = 256
RET_HEADS = 4
RET_DK = 128
RET_DV = 128
RET_CHUNK = 128
SWA_HEADS = 8
SWA_KV_HEADS = 2
SWA_GROUP = SWA_HEADS // SWA_KV_HEADS
SWA_HD = 64
WINDOW = 128
N_GROUPS = 4
EXPERTS_PER_GROUP = 8
N_EXPERTS = N_GROUPS * EXPERTS_PER_GROUP
EXPERT_FF = 256

RET_W = RET_HEADS * RET_DK
SWA_Q = SWA_HEADS * SWA_HD
SWA_KV = SWA_KV_HEADS * SWA_HD
Z_RET_W = 4 * RET_W
Z_SKV_W = 2 * SWA_KV
Z_GATE_W = 2 * D_MODEL
IN_WIDTH = Z_RET_W + SWA_Q + Z_SKV_W + Z_GATE_W

ROUTE_LANES = 128
LANE_GROUP0 = N_EXPERTS
LANE_E1, LANE_E2, LANE_C1, LANE_C2 = 32, 33, 34, 35
NEG_INF = -1e30

HALF_D = D_MODEL // 2
ROW_TILE = 512
SC_CORES = 2
SC_SUBCORES = 16
SC_WORKERS = SC_CORES * SC_SUBCORES
SC_CHUNK = 128

VMEM_LIMIT = 56 * 1024 * 1024

IN_TM = 512
RET_ROWS = 512
SWA_ROWS = 512
MERGE_TM = 512
ROUTE_TM = 512
PLE_TM = 512


def _rms(x, g):
    ms = jnp.mean(x * x, axis=-1, keepdims=True)
    return x * lax.rsqrt(ms + EPS) * g


def _sigmoid(x):
    return 1.0 / (1.0 + jnp.exp(-x))


def _dot(a, b):
    return jnp.dot(a, b, preferred_element_type=F32)


def _dot_nt(a, b):
    return lax.dot_general(a, b, (((1,), (1,)), ((), ())), preferred_element_type=F32)


def _dot_tn(a, b):
    return lax.dot_general(a, b, (((0,), (0,)), ((), ())), preferred_element_type=F32)


def _pack_halves(v):
    hi = lax.bitcast_convert_type(v[:, :HALF_D].astype(BF16).astype(F32), jnp.uint32)
    lo = lax.bitcast_convert_type(v[:, HALF_D:].astype(BF16).astype(F32), jnp.uint32)
    return lax.bitcast_convert_type(hi | (lo >> 16), I32)


def _unpack_halves(p):
    u = lax.bitcast_convert_type(p, jnp.uint32)
    hi = lax.bitcast_convert_type(u & jnp.uint32(0xFFFF0000), F32)
    lo = lax.bitcast_convert_type(u << 16, F32)
    return hi, lo


def _in_proj_kernel(x_ref, g_ref, w_ref, zret_ref, zsq_ref, zskv_ref, zgate_ref):
    h = _rms(x_ref[...], g_ref[...]).astype(BF16)

    def proj(c0, width):
        return _dot(h, w_ref[:, c0:c0 + width]).astype(BF16)

    cw = 512
    for j in range(Z_RET_W // cw):
        zret_ref[:, j * cw:(j + 1) * cw] = proj(j * cw, cw)
    zsq_ref[...] = proj(Z_RET_W, SWA_Q)
    zskv_ref[...] = proj(Z_RET_W + SWA_Q, Z_SKV_W)
    g0 = Z_RET_W + SWA_Q + Z_SKV_W
    for j in range(Z_GATE_W // cw):
        zgate_ref[:, j * cw:(j + 1) * cw] = proj(g0 + j * cw, cw)


def _in_proj(x2d, g, w_bf):
    t = x2d.shape[0]
    row = lambda w: pl.BlockSpec((IN_TM, w), lambda i: (i, 0))
    full = lambda a: pl.BlockSpec(a.shape, lambda i: (0,) * a.ndim)
    return pl.pallas_call(
        _in_proj_kernel,
        grid=(t // IN_TM,),
        in_specs=[row(D_MODEL), full(g), full(w_bf)],
        out_specs=[row(Z_RET_W), row(SWA_Q), row(Z_SKV_W), row(Z_GATE_W)],
        out_shape=[jax.ShapeDtypeStruct((t, w), BF16) for w in (Z_RET_W, SWA_Q, Z_SKV_W, Z_GATE_W)],
        compiler_params=pltpu.CompilerParams(dimension_semantics=("parallel",),
                                             vmem_limit_bytes=VMEM_LIMIT),
        name="in_proj",
    )(x2d, g, w_bf)


def _retention_kernel(z_ref, dmask_ref, zeta_ref, xi_ref, cdec_ref, gng_ref, gnb_ref, o_ref, state_ref):
    @pl.when(pl.program_id(1) == 0)
    def _():
        state_ref[...] = jnp.zeros_like(state_ref)

    def chunk(c, carry):
        r0 = pl.multiple_of(c * RET_CHUNK, RET_CHUNK)
        rows = pl.ds(r0, RET_CHUNK)
        for h in range(RET_HEADS):
            cols = lambda part: slice(part * RET_W + h * RET_DK, part * RET_W + (h + 1) * RET_DK)
            q = z_ref[rows, cols(0)]
            k = z_ref[rows, cols(1)]
            v = z_ref[rows, cols(2)]
            gate = z_ref[rows, cols(3)].astype(F32)
            state = state_ref[h]
            scores = _dot_nt(q, k) * dmask_ref[h]
            inner = _dot(scores.astype(BF16), v)
            cross = _dot(q, state.astype(BF16)) * xi_ref[h]
            y = inner + cross
            kz = (k.astype(F32) * zeta_ref[h]).astype(BF16)
            state_ref[h] = cdec_ref[h] * state + _dot_tn(kz, v)
            mu = jnp.mean(y, axis=-1, keepdims=True)
            d = y - mu
            var = jnp.mean(d * d, axis=-1, keepdims=True)
            hs = slice(h * RET_DV, (h + 1) * RET_DV)
            yn = d * lax.rsqrt(var + EPS) * gng_ref[:, hs] + gnb_ref[:, hs]
            o_ref[rows, hs] = (gate * _sigmoid(gate) * yn).astype(BF16)
        return carry

    lax.fori_loop(0, RET_ROWS // RET_CHUNK, chunk, 0)


def _retention_tables():
    h = RET_HEADS
    c = RET_CHUNK
    log_gamma = jnp.log1p(-jnp.exp2(-5.0 - jnp.arange(h, dtype=F32)))
    pos = jnp.arange(c, dtype=F32)
    diff = pos[:, None] - pos[None, :]
    decay = jnp.where(diff[None] >= 0.0,
                      jnp.exp(jnp.maximum(diff, 0.0)[None] * log_gamma[:, None, None]), 0.0)
    scale = RET_DK ** -0.5
    dmask = decay * scale
    zeta = jnp.exp((c - 1.0 - pos)[None, :] * log_gamma[:, None]) * scale
    xi = jnp.exp((pos + 1.0)[None, :] * log_gamma[:, None])
    cdec = jnp.exp(c * log_gamma)
    bc = lambda a: jnp.broadcast_to(a[:, :, None], (h, c, c))
    return dmask, bc(zeta), bc(xi), jnp.broadcast_to(cdec[:, None, None], (h, c, c))


def _retention(z_ret, gn_g, gn_b, batch, seq):
    z3 = z_ret.reshape(batch, seq, Z_RET_W)
    tables = _retention_tables()
    tab_spec = pl.BlockSpec((RET_HEADS, RET_CHUNK, RET_CHUNK), lambda b, i: (0, 0, 0))
    vec_spec = pl.BlockSpec((1, RET_W), lambda b, i: (0, 0))
    out = pl.pallas_call(
        _retention_kernel,
        grid=(batch, seq // RET_ROWS),
        in_specs=[pl.BlockSpec((None, RET_ROWS, Z_RET_W), lambda b, i: (b, i, 0)),
                  tab_spec, tab_spec, tab_spec, tab_spec, vec_spec, vec_spec],
        out_specs=pl.BlockSpec((None, RET_ROWS, RET_W), lambda b, i: (b, i, 0)),
        out_shape=jax.ShapeDtypeStruct((batch, seq, RET_W), BF16),
        scratch_shapes=[pltpu.VMEM((RET_HEADS, RET_DK, RET_DV), F32)],
        compiler_params=pltpu.CompilerParams(dimension_semantics=("parallel", "arbitrary"),
                                             vmem_limit_bytes=VMEM_LIMIT),
        name="retention",
    )(z3, *tables, gn_g, gn_b)
    return out.reshape(batch * seq, RET_W)


def _swa_kernel(sinks_ref, q_ref, kvp_ref, kvc_ref, qg_ref, kg_ref, bq_ref, bk_ref, bias_ref, o_ref, band_ref):
    first = pl.program_id(1) == 0
    band_ref[:WINDOW, :] = kvp_ref[...]
    band_ref[WINDOW:, :] = kvc_ref[...]
    group_rows = SWA_GROUP * WINDOW
    lo_q = lax.broadcasted_iota(jnp.int32, (WINDOW, 2 * SWA_HD), 1) < SWA_HD
    lo_k = lax.broadcasted_iota(jnp.int32, (2 * WINDOW, 2 * SWA_HD), 1) < SWA_HD
    in_prev = lax.broadcasted_iota(jnp.int32, (group_rows, 2 * WINDOW), 1) < WINDOW
    row_head = lax.broadcasted_iota(jnp.int32, (group_rows, 1), 0) // WINDOW

    def both_heads(x):
        swapped = pltpu.roll(x, SWA_HD, axis=1)
        return jnp.where(lo_k, x, swapped), jnp.where(lo_k, swapped, x)

    def block(n, carry):
        r0 = pl.multiple_of(n * WINDOW, WINDOW)
        rows = pl.ds(r0, WINDOW)
        kv = band_ref[pl.ds(r0, 2 * WINDOW), :]
        kf = kv[:, :SWA_KV].astype(F32)
        kn = kf * lax.rsqrt(_dot((kf * kf).astype(BF16), bk_ref[...]) + EPS) * kg_ref[...]
        keys = [a.astype(BF16) for a in both_heads(kn)]
        vals = [a.astype(BF16) for a in both_heads(kv[:, SWA_KV:].astype(F32))]
        qf = q_ref[rows, :].astype(F32)
        qn = qf * lax.rsqrt(_dot((qf * qf).astype(BF16), bq_ref[...]) + EPS) * qg_ref[...]
        no_prev = jnp.logical_and(first, n == 0)
        for kh in range(SWA_KV_HEADS):
            parts = []
            for j in range(SWA_GROUP // 2):
                c0 = (kh * SWA_GROUP + 2 * j) * SWA_HD
                pair = qn[:, c0:c0 + 2 * SWA_HD]
                parts += [jnp.where(lo_q, pair, 0.0), jnp.where(lo_q, 0.0, pair)]
            qs = jnp.concatenate(parts, axis=0).astype(BF16)
            s = _dot_nt(qs, keys[kh]) + bias_ref[kh]
            s = jnp.where(jnp.logical_and(no_prev, in_prev), NEG_INF, s)
            sink = sinks_ref[kh * SWA_GROUP]
            for g in range(1, SWA_GROUP):
                sink = jnp.where(row_head == g, sinks_ref[kh * SWA_GROUP + g], sink)
            m = jnp.maximum(jnp.max(s, axis=-1, keepdims=True), sink)
            p = jnp.exp(s - m)
            denom = jnp.sum(p, axis=-1, keepdims=True) + jnp.exp(sink - m)
            o = _dot(p.astype(BF16), vals[kh]) * (1.0 / denom)
            for j in range(SWA_GROUP // 2):
                even = o[(2 * j) * WINDOW:(2 * j + 1) * WINDOW]
                odd = o[(2 * j + 1) * WINDOW:(2 * j + 2) * WINDOW]
                c0 = (kh * SWA_GROUP + 2 * j) * SWA_HD
                o_ref[rows, c0:c0 + 2 * SWA_HD] = jnp.where(lo_q, even, odd).astype(BF16)
        return carry

    lax.fori_loop(0, SWA_ROWS // WINDOW, block, 0)


def _swa_tables(q_g, k_g):
    qi = jnp.arange(WINDOW)[:, None]
    sj = jnp.arange(2 * WINDOW)[None, :]
    rel = qi + WINDOW - sj
    ok = (rel >= 0) & (rel < WINDOW)
    slopes = jnp.exp2(-8.0 * jnp.arange(1, SWA_HEADS + 1, dtype=F32) / SWA_HEADS)
    bias = jnp.where(ok[None], -slopes[:, None, None] * rel.astype(F32)[None], NEG_INF)
    bias = bias.reshape(SWA_KV_HEADS, SWA_GROUP * WINDOW, 2 * WINDOW)
    avg = lambda heads: jnp.kron(jnp.eye(heads, dtype=F32), jnp.full((SWA_HD, SWA_HD), 1.0 / SWA_HD, F32)).astype(BF16)
    qg = jnp.tile(q_g.reshape(1, SWA_HD), (1, SWA_HEADS)) * (SWA_HD ** -0.5)
    kg = jnp.tile(k_g.reshape(1, SWA_HD), (1, SWA_KV_HEADS))
    return qg, kg, avg(SWA_HEADS), avg(SWA_KV_HEADS), bias


def _swa(z_sq, z_skv, sinks, q_g, k_g, batch, seq):
    q3 = z_sq.reshape(batch, seq, SWA_Q)
    kv3 = z_skv.reshape(batch, seq, Z_SKV_W)
    per = SWA_ROWS // WINDOW
    qg, kg, bq, bk, bias = _swa_tables(q_g, k_g)
    full = lambda a: pl.BlockSpec(a.shape, lambda b, i: (0,) * a.ndim)
    out = pl.pallas_call(
        _swa_kernel,
        grid=(batch, seq // SWA_ROWS),
        in_specs=[pl.BlockSpec(memory_space=pltpu.SMEM),
                  pl.BlockSpec((None, SWA_ROWS, SWA_Q), lambda b, i: (b, i, 0)),
                  pl.BlockSpec((None, WINDOW, Z_SKV_W), lambda b, i: (b, jnp.maximum(i * per - 1, 0), 0)),
                  pl.BlockSpec((None, SWA_ROWS, Z_SKV_W), lambda b, i: (b, i, 0)),
                  full(qg), full(kg), full(bq), full(bk), full(bias)],
        out_specs=pl.BlockSpec((None, SWA_ROWS, SWA_Q), lambda b, i: (b, i, 0)),
        out_shape=jax.ShapeDtypeStruct((batch, seq, SWA_Q), BF16),
        scratch_shapes=[pltpu.VMEM((SWA_ROWS + WINDOW, Z_SKV_W), BF16)],
        compiler_params=pltpu.CompilerParams(dimension_semantics=("parallel", "arbitrary"),
                                             vmem_limit_bytes=VMEM_LIMIT),
        name="swa",
    )(sinks, q3, kv3, kv3, qg, kg, bq, bk, bias)
    return out.reshape(batch * seq, SWA_Q)


def _merge_kernel(yr_ref, ys_ref, zg_ref, x_ref, wro_ref, wso_ref, wout_ref, fg_ref, wr_ref, br_ref,
                  x1_ref, hp_ref, route_ref, cnt_ref):
    a = _dot(yr_ref[...], wro_ref[...])
    b = _dot(ys_ref[...], wso_ref[...])
    gate_r = zg_ref[:, :D_MODEL].astype(F32)
    gate_s = zg_ref[:, D_MODEL:].astype(F32)
    merged = (_sigmoid(gate_r) * a + _sigmoid(gate_s) * b).astype(BF16)
    x1 = x_ref[...] + _dot(merged, wout_ref[...])
    x1_ref[...] = x1
    h2 = _rms(x1, fg_ref[...])
    hp_ref[...] = _pack_halves(h2)

    h_hi = h2.astype(BF16)
    h_lo = (h2 - h_hi.astype(F32)).astype(BF16)
    parts = _dot(h_hi, wr_ref[...]) + _dot(h_lo, wr_ref[...])
    logits = parts[:, :ROUTE_LANES] + parts[:, ROUTE_LANES:] + br_ref[...]
    lane = lax.broadcasted_iota(jnp.int32, logits.shape, 1)
    big = jnp.int32(ROUTE_LANES)
    is_group = jnp.logical_and(lane >= LANE_GROUP0, lane < LANE_GROUP0 + N_GROUPS)
    gl = jnp.where(is_group, logits, NEG_INF)
    gmax = jnp.max(gl, axis=-1, keepdims=True)
    g_w = 1.0 / jnp.sum(jnp.exp(gl - gmax), axis=-1, keepdims=True)
    g_sel = jnp.min(jnp.where(gl == gmax, lane, big), axis=-1, keepdims=True) - LANE_GROUP0
    in_group = jnp.logical_and(lane < N_EXPERTS, (lane >> 3) == g_sel)
    el = jnp.where(in_group, logits, NEG_INF)
    m1 = jnp.max(el, axis=-1, keepdims=True)
    i1 = jnp.min(jnp.where(el == m1, lane, big), axis=-1, keepdims=True)
    el2 = jnp.where(lane == i1, NEG_INF, el)
    m2 = jnp.max(el2, axis=-1, keepdims=True)
    i2 = jnp.min(jnp.where(el2 == m2, lane, big), axis=-1, keepdims=True)
    e2 = jnp.exp(m2 - m1)
    c1 = g_w / (1.0 + e2)
    c2 = g_w * e2 / (1.0 + e2)
    chosen = jnp.where(jnp.logical_or(lane == i1, lane == i2), 1.0, 0.0)
    route_ref[...] = (chosen
                      + jnp.where(lane == LANE_E1, i1.astype(F32), 0.0)
                      + jnp.where(lane == LANE_E2, i2.astype(F32), 0.0)
                      + jnp.where(lane == LANE_C1, c1, 0.0)
                      + jnp.where(lane == LANE_C2, c2, 0.0))
    cnt_ref[...] = jnp.sum(chosen, axis=0, keepdims=True)


def _merge(y_r, y_s, z_gate, x2d, w_ret_o, w_swa_o, w_out, ffn_g, w_router, b_router):
    t = x2d.shape[0]
    row = lambda w: pl.BlockSpec((MERGE_TM, w), lambda i: (i, 0))
    full = lambda a: pl.BlockSpec(a.shape, lambda i: (0,) * a.ndim)
    return pl.pallas_call(
        _merge_kernel,
        grid=(t // MERGE_TM,),
        in_specs=[row(RET_W), row(SWA_Q), row(Z_GATE_W), row(D_MODEL),
                  full(w_ret_o), full(w_swa_o), full(w_out), full(ffn_g), full(w_router), full(b_router)],
        out_specs=[row(D_MODEL), row(HALF_D), row(ROUTE_LANES),
                   pl.BlockSpec((None, 1, ROUTE_LANES), lambda i: (i, 0, 0))],
        out_shape=[jax.ShapeDtypeStruct((t, D_MODEL), F32),
                   jax.ShapeDtypeStruct((t, HALF_D), I32),
                   jax.ShapeDtypeStruct((t, ROUTE_LANES), F32),
                   jax.ShapeDtypeStruct((t // MERGE_TM, 1, ROUTE_LANES), F32)],
        compiler_params=pltpu.CompilerParams(dimension_semantics=("parallel",),
                                             vmem_limit_bytes=VMEM_LIMIT),
        name="merge_router",
    )(y_r, y_s, z_gate, x2d, w_ret_o, w_swa_o, w_out, ffn_g, w_router, b_router)


def _route_pos_kernel(route_ref, base_ref, pos_ref):
    route = route_ref[...]
    lane = lax.broadcasted_iota(jnp.int32, route.shape, 1)
    lane_f = lane.astype(F32)
    chosen = jnp.where(lane < N_EXPERTS, route, 0.0)
    r = lax.broadcasted_iota(jnp.int32, (ROUTE_TM, ROUTE_TM), 0)
    c = lax.broadcasted_iota(jnp.int32, (ROUTE_TM, ROUTE_TM), 1)
    tri = jnp.where(c <= r, 1.0, 0.0).astype(BF16)
    incl = _dot(tri, chosen.astype(BF16))
    posmat = base_ref[...] + incl - 1.0
    pick = lambda ln: jnp.sum(jnp.where(lane == ln, route, 0.0), axis=-1, keepdims=True)
    pos_of = lambda e: jnp.sum(jnp.where(lane_f == e, posmat, 0.0), axis=-1, keepdims=True)
    p1 = pos_of(pick(LANE_E1))
    p2 = pos_of(pick(LANE_E2))
    rec = jnp.where(lane == 0, p1, 0.0) + jnp.where(lane == 1, p2, 0.0)
    pos_ref[...] = rec.T[:8, :].astype(I32)


def _route_pos(route, base):
    t = route.shape[0]
    n = t // ROUTE_TM
    return pl.pallas_call(
        _route_pos_kernel,
        grid=(n,),
        in_specs=[pl.BlockSpec((ROUTE_TM, ROUTE_LANES), lambda i: (i, 0)),
                  pl.BlockSpec((None, 1, ROUTE_LANES), lambda i: (i, 0, 0))],
        out_specs=pl.BlockSpec((None, 8, ROUTE_TM), lambda i: (i, 0, 0)),
        out_shape=jax.ShapeDtypeStruct((n, 8, ROUTE_TM), I32),
        compiler_params=pltpu.CompilerParams(dimension_semantics=("parallel",),
                                             vmem_limit_bytes=VMEM_LIMIT),
        name="route_pos",
    )(route, base)


def _plan_kernel(cnt_ref, base_ref, owner_ref, valid_ref):
    counts = cnt_ref[...]
    nt = counts.shape[0]
    exact = functools.partial(jnp.dot, precision=lax.Precision.HIGHEST, preferred_element_type=F32)
    total = jnp.sum(counts, axis=0, keepdims=True)
    tiles_e = jnp.ceil(total * (1.0 / ROW_TILE))
    k = lax.broadcasted_iota(jnp.int32, (ROUTE_LANES, ROUTE_LANES), 0)
    lane = lax.broadcasted_iota(jnp.int32, (ROUTE_LANES, ROUTE_LANES), 1)
    tile_end = exact(jnp.broadcast_to(tiles_e, (8, ROUTE_LANES)), jnp.where(k <= lane, 1.0, 0.0))[0:1]
    tile_start = tile_end - tiles_e
    r = lax.broadcasted_iota(jnp.int32, (nt, nt), 0)
    c = lax.broadcasted_iota(jnp.int32, (nt, nt), 1)
    before = exact(jnp.where(c < r, 1.0, 0.0), counts)
    base_ref[...] = tile_start * ROW_TILE + before

    used = jnp.max(tile_end, axis=-1, keepdims=True)
    tile = k.astype(F32)
    clamped = jnp.minimum(tile, jnp.maximum(used - 1.0, 0.0))
    is_expert = lane < N_EXPERTS
    owner = jnp.sum(jnp.where(jnp.logical_and(is_expert, tile_end <= clamped), 1.0, 0.0), axis=-1, keepdims=True)
    owner = jnp.minimum(owner, N_EXPERTS - 1.0)
    mine = lane.astype(F32) == owner
    pick = lambda v: jnp.sum(jnp.where(mine, v, 0.0), axis=-1, keepdims=True)
    t1 = tile[:, 0:1]
    valid = jnp.clip(pick(total) - (t1 - pick(tile_start)) * ROW_TILE, 0.0, float(ROW_TILE))
    valid = jnp.where(t1 < used, valid, 0.0)
    owner_ref[...] = jnp.broadcast_to(owner, owner_ref.shape).astype(I32)
    valid_ref[...] = jnp.broadcast_to(valid, valid_ref.shape).astype(I32)


def _tile_plan(cnt, n_tiles):
    assert n_tiles <= ROUTE_LANES
    nt = cnt.shape[0]
    sq = (ROUTE_LANES, ROUTE_LANES)
    base, owner, valid = pl.pallas_call(
        _plan_kernel,
        out_shape=[jax.ShapeDtypeStruct((nt, ROUTE_LANES), F32), jax.ShapeDtypeStruct(sq, I32),
                   jax.ShapeDtypeStruct(sq, I32)],
        name="tile_plan",
    )(cnt.reshape(nt, ROUTE_LANES))
    return base.reshape(cnt.shape), owner[:n_tiles, 0], valid[:n_tiles, 0]


def _sc_mesh():
    return plsc.VectorSubcoreMesh(core_axis_name="c", subcore_axis_name="s")


def _sc_worker():
    return lax.axis_index("s") * SC_CORES + lax.axis_index("c")


def _dispatch(hp, pos1, pos2, n_rows):
    t = hp.shape[0]
    per_worker = t // SC_WORKERS
    k = per_worker // SC_CHUNK
    idx = pltpu.VMEM((k, SC_CHUNK), I32)

    @functools.partial(pl.kernel, mesh=_sc_mesh(), out_type=jax.ShapeDtypeStruct((n_rows, HALF_D), I32),
                       scratch_types=[idx, idx, pltpu.VMEM((SC_CHUNK, HALF_D), I32)], name="moe_dispatch")
    def run(h_hbm, p1_hbm, p2_hbm, xs_hbm, p1_v, p2_v, rows_v):
        wid = _sc_worker()
        pltpu.sync_copy(p1_hbm.at[wid], p1_v)
        pltpu.sync_copy(p2_hbm.at[wid], p2_v)
        for j in range(k):
            pltpu.sync_copy(h_hbm.at[pl.ds(wid * per_worker + j * SC_CHUNK, SC_CHUNK)], rows_v)
            pltpu.sync_copy(rows_v, xs_hbm.at[p1_v.at[j]])
            pltpu.sync_copy(rows_v, xs_hbm.at[p2_v.at[j]])

    return run(hp, pos1.reshape(SC_WORKERS, k, SC_CHUNK), pos2.reshape(SC_WORKERS, k, SC_CHUNK))


def _combine_gather(ys, pos1, pos2, t):
    per_worker = t // SC_WORKERS
    k = per_worker // SC_CHUNK
    idx = pltpu.VMEM((k, SC_CHUNK), I32)
    out = jax.ShapeDtypeStruct((t, HALF_D), I32)

    @functools.partial(pl.kernel, mesh=_sc_mesh(), out_type=(out, out),
                       scratch_types=[idx, idx, pltpu.VMEM((SC_CHUNK, HALF_D), I32)], name="moe_combine")
    def run(ys_hbm, p1_hbm, p2_hbm, g1_hbm, g2_hbm, p1_v, p2_v, rows_v):
        wid = _sc_worker()
        pltpu.sync_copy(p1_hbm.at[wid], p1_v)
        pltpu.sync_copy(p2_hbm.at[wid], p2_v)
        for j in range(k):
            dst = pl.ds(wid * per_worker + j * SC_CHUNK, SC_CHUNK)
            pltpu.sync_copy(ys_hbm.at[p1_v.at[j]], rows_v)
            pltpu.sync_copy(rows_v, g1_hbm.at[dst])
            pltpu.sync_copy(ys_hbm.at[p2_v.at[j]], rows_v)
            pltpu.sync_copy(rows_v, g2_hbm.at[dst])

    return run(ys, pos1.reshape(SC_WORKERS, k, SC_CHUNK), pos2.reshape(SC_WORKERS, k, SC_CHUNK))


def _experts_kernel(owner_ref, valid_ref, xs_ref, wg_ref, wu_ref, wd_ref, ys_ref, wg_bf, wu_bf, wd_bf):
    i = pl.program_id(0)
    valid = valid_ref[i]

    @pl.when(jnp.logical_or(i == 0, owner_ref[i] != owner_ref[jnp.maximum(i - 1, 0)]))
    def _():
        wg_bf[...] = wg_ref[...].astype(BF16)
        wu_bf[...] = wu_ref[...].astype(BF16)
        wd_bf[...] = wd_ref[...].astype(BF16)

    @pl.when(valid > 0)
    def _():
        row = lax.broadcasted_iota(jnp.int32, (ROW_TILE, HALF_D), 0)
        packed = jnp.where(row < valid, xs_ref[...], 0)
        hi, lo = _unpack_halves(packed)
        hi = hi.astype(BF16)
        lo = lo.astype(BF16)
        hg = _dot(hi, wg_bf[:HALF_D, :]) + _dot(lo, wg_bf[HALF_D:, :])
        hu = _dot(hi, wu_bf[:HALF_D, :]) + _dot(lo, wu_bf[HALF_D:, :])
        hid = (hg * _sigmoid(hg) * hu).astype(BF16)
        ys_ref[...] = _pack_halves(_dot(hid, wd_bf[...]))

    @pl.when(valid == 0)
    def _():
        ys_ref[...] = jnp.zeros_like(ys_ref)


def _experts(xs, owner, valid, wg, wu, wd):
    n_tiles = owner.shape[0]
    grid_spec = pltpu.PrefetchScalarGridSpec(
        num_scalar_prefetch=2,
        grid=(n_tiles,),
        in_specs=[pl.BlockSpec((ROW_TILE, HALF_D), lambda i, own, val: (i, 0)),
                  pl.BlockSpec((None, D_MODEL, EXPERT_FF), lambda i, own, val: (own[i], 0, 0)),
                  pl.BlockSpec((None, D_MODEL, EXPERT_FF), lambda i, own, val: (own[i], 0, 0)),
                  pl.BlockSpec((None, EXPERT_FF, D_MODEL), lambda i, own, val: (own[i], 0, 0))],
        out_specs=pl.BlockSpec((ROW_TILE, HALF_D), lambda i, own, val: (i, 0)),
        scratch_shapes=[pltpu.VMEM((D_MODEL, EXPERT_FF), BF16), pltpu.VMEM((D_MODEL, EXPERT_FF), BF16),
                        pltpu.VMEM((EXPERT_FF, D_MODEL), BF16)],
    )
    return pl.pallas_call(
        _experts_kernel,
        grid_spec=grid_spec,
        out_shape=jax.ShapeDtypeStruct((n_tiles * ROW_TILE, HALF_D), I32),
        compiler_params=pltpu.CompilerParams(dimension_semantics=("arbitrary",),
                                             vmem_limit_bytes=VMEM_LIMIT),
        name="experts",
    )(owner, valid, xs, wg, wu, wd)


def _ple_kernel(x1_ref, g1_ref, g2_ref, route_ref, p_ref, wp_ref, pg_ref, gg_ref, wgate_ref, o_ref):
    route = route_ref[...]
    lane = lax.broadcasted_iota(jnp.int32, route.shape, 1)
    pick = lambda ln: jnp.sum(jnp.where(lane == ln, route, 0.0), axis=-1, keepdims=True)
    c1 = pick(LANE_C1)
    c2 = pick(LANE_C2)
    a_hi, a_lo = _unpack_halves(g1_ref[...])
    b_hi, b_lo = _unpack_halves(g2_ref[...])
    x = x1_ref[...] + jnp.concatenate([c1 * a_hi + c2 * b_hi, c1 * a_lo + c2 * b_lo], axis=1)
    ple = _rms(_dot(p_ref[...].astype(BF16), wp_ref[...]), pg_ref[...])
    gate = _sigmoid(_dot(_rms(x, gg_ref[...]).astype(BF16), wgate_ref[...]))
    o_ref[...] = x + gate * ple


def _ple(x1, g1, g2, route, p2d, w_ple, ple_g, gate_g, w_gate):
    t = x1.shape[0]
    row = lambda w: pl.BlockSpec((PLE_TM, w), lambda i: (i, 0))
    full = lambda a: pl.BlockSpec(a.shape, lambda i: (0,) * a.ndim)
    return pl.pallas_call(
        _ple_kernel,
        grid=(t // PLE_TM,),
        in_specs=[row(D_MODEL), row(HALF_D), row(HALF_D), row(ROUTE_LANES), row(PLE_DIM),
                  full(w_ple), full(ple_g), full(gate_g), full(w_gate)],
        out_specs=row(D_MODEL),
        out_shape=jax.ShapeDtypeStruct((t, D_MODEL), F32),
        compiler_params=pltpu.CompilerParams(dimension_semantics=("parallel",),
                                             vmem_limit_bytes=VMEM_LIMIT),
        name="ple",
    )(x1, g1, g2, route, p2d, w_ple, ple_g, gate_g, w_gate)


def kernel(x, p, mix_norm_g, w_in, ret_gn_g, ret_gn_b, w_ret_o, q_norm_g, k_norm_g, attn_sinks,
           w_swa_o, w_out, ffn_norm_g, w_router_group, b_router_group, w_router_expert,
           b_router_expert, w_exp_gate, w_exp_up, w_exp_down, ple_gate_norm_g, w_ple_gate,
           w_ple, ple_norm_g):
    batch, seq, d = x.shape
    t = batch * seq
    depth = w_in.shape[0]
    n_tiles = 2 * t // ROW_TILE + N_EXPERTS
    row = lambda a: a.reshape(1, -1)
    x2d = x.reshape(t, d)
    for i in range(depth):
        z_ret, z_sq, z_skv, z_gate = _in_proj(x2d, row(mix_norm_g[i]), w_in[i].astype(BF16))
        y_r = _retention(z_ret, row(ret_gn_g[i]), row(ret_gn_b[i]), batch, seq)
        y_s = _swa(z_sq, z_skv, attn_sinks[i], row(q_norm_g[i]), row(k_norm_g[i]), batch, seq)
        pad = ROUTE_LANES - N_GROUPS - N_EXPERTS
        w_router = jnp.pad(jnp.concatenate([w_router_expert[i], w_router_group[i]], axis=1), ((0, 0), (0, pad)))
        b_router = jnp.pad(jnp.concatenate([b_router_expert[i], b_router_group[i]]), (0, pad)).reshape(1, -1)
        w_router_hi = w_router.astype(BF16)
        w_router = jnp.concatenate([w_router_hi, (w_router - w_router_hi.astype(F32)).astype(BF16)], axis=1)
        x1, hp, route, cnt = _merge(y_r, y_s, z_gate, x2d, w_ret_o[i].astype(BF16), w_swa_o[i].astype(BF16),
                                    w_out[i].astype(BF16), row(ffn_norm_g[i]), w_router, b_router)
        base, owner, valid = _tile_plan(cnt, n_tiles)
        pos = _route_pos(route, base)
        pos1 = pos[:, 0, :]
        pos2 = pos[:, 1, :]
        xs = _dispatch(hp, pos1, pos2, n_tiles * ROW_TILE)
        ys = _experts(xs, owner, valid, w_exp_gate[i], w_exp_up[i], w_exp_down[i])
        g1, g2 = _combine_gather(ys, pos1, pos2, t)
        x2d = _ple(x1, g1, g2, route, p[i].reshape(t, PLE_DIM), w_ple[i].astype(BF16), row(ple_norm_g[i]),
                   row(ple_gate_norm_g[i]), w_ple_gate[i].astype(BF16))
    return x2d.reshape(batch, seq, d)
```

```python
import functools

import jax
import jax.numpy as jnp
from jax import lax
from jax.experimental import pallas as pl
from jax.experimental.pallas import tpu as pltpu
from jax.experimental.pallas import tpu_sc as plsc

F32 = jnp.float32
BF16 = jnp.bfloat16
I32 = jnp.int32

EPS = 1e-6
D_MODEL = 1024
PLE_DIM = 256
RET_HEADS = 4
RET_DK = 128
RET_DV = 128
RET_CHUNK = 128
SWA_HEADS = 8
SWA_KV_HEADS = 2
SWA_GROUP = SWA_HEADS // SWA_KV_HEADS
SWA_HD = 64
WINDOW = 128
N_GROUPS = 4
EXPERTS_PER_GROUP = 8
N_EXPERTS = N_GROUPS * EXPERTS_PER_GROUP
EXPERT_FF = 256

RET_W = RET_HEADS * RET_DK
SWA_Q = SWA_HEADS * SWA_HD
SWA_KV = SWA_KV_HEADS * SWA_HD
Z_RET_W = 4 * RET_W
Z_SKV_W = 2 * SWA_KV
Z_GATE_W = 2 * D_MODEL
IN_WIDTH = Z_RET_W + SWA_Q + Z_SKV_W + Z_GATE_W

ROUTE_LANES = 128
LANE_GROUP0 = N_EXPERTS
LANE_E1, LANE_E2, LANE_C1, LANE_C2 = 32, 33, 34, 35
NEG_INF = -1e30

HALF_D = D_MODEL // 2
ROW_TILE = 512
SC_CORES = 2
SC_SUBCORES = 16
SC_WORKERS = SC_CORES * SC_SUBCORES
SC_CHUNK = 128

VMEM_LIMIT = 56 * 1024 * 1024

IN_TM = 512
RET_ROWS = 512
SWA_ROWS = 512
MERGE_TM = 512
MERGE_SUB = 256
EXPERT_SUB = 256
ROUTE_TM = 512
PLE_TM = 512


def _rms(x, g):
    ms = jnp.mean(x * x, axis=-1, keepdims=True)
    return x * lax.rsqrt(ms + EPS) * g


def _sigmoid(x):
    return 1.0 / (1.0 + jnp.exp(-x))


def _dot(a, b):
    return jnp.dot(a, b, preferred_element_type=F32)


def _dot_nt(a, b):
    return lax.dot_general(a, b, (((1,), (1,)), ((), ())), preferred_element_type=F32)


def _dot_tn(a, b):
    return lax.dot_general(a, b, (((0,), (0,)), ((), ())), preferred_element_type=F32)


def _pack_halves(v):
    hi = lax.bitcast_convert_type(v[:, :HALF_D].astype(BF16).astype(F32), jnp.uint32)
    lo = lax.bitcast_convert_type(v[:, HALF_D:].astype(BF16).astype(F32), jnp.uint32)
    return lax.bitcast_convert_type(hi | (lo >> 16), I32)


def _unpack_halves(p):
    u = lax.bitcast_convert_type(p, jnp.uint32)
    hi = lax.bitcast_convert_type(u & jnp.uint32(0xFFFF0000), F32)
    lo = lax.bitcast_convert_type(u << 16, F32)
    return hi, lo


def _in_proj_kernel(x_ref, g_ref, w_ref, zret_ref, zsq_ref, zskv_ref, zgate_ref):
    h = _rms(x_ref[...], g_ref[...]).astype(BF16)

    def proj(c0, width):
        return _dot(h, w_ref[:, c0:c0 + width]).astype(BF16)

    cw = 512
    for j in range(Z_RET_W // cw):
        zret_ref[:, j * cw:(j + 1) * cw] = proj(j * cw, cw)
    zsq_ref[...] = proj(Z_RET_W, SWA_Q)
    zskv_ref[...] = proj(Z_RET_W + SWA_Q, Z_SKV_W)
    g0 = Z_RET_W + SWA_Q + Z_SKV_W
    for j in range(Z_GATE_W // cw):
        zgate_ref[:, j * cw:(j + 1) * cw] = proj(g0 + j * cw, cw)


def _in_proj(x2d, g, w_bf):
    t = x2d.shape[0]
    row = lambda w: pl.BlockSpec((IN_TM, w), lambda i: (i, 0))
    full = lambda a: pl.BlockSpec(a.shape, lambda i: (0,) * a.ndim)
    return pl.pallas_call(
        _in_proj_kernel,
        grid=(t // IN_TM,),
        in_specs=[row(D_MODEL), full(g), full(w_bf)],
        out_specs=[row(Z_RET_W), row(SWA_Q), row(Z_SKV_W), row(Z_GATE_W)],
        out_shape=[jax.ShapeDtypeStruct((t, w), BF16) for w in (Z_RET_W, SWA_Q, Z_SKV_W, Z_GATE_W)],
        compiler_params=pltpu.CompilerParams(dimension_semantics=("parallel",),
                                             vmem_limit_bytes=VMEM_LIMIT),
        name="in_proj",
    )(x2d, g, w_bf)


def _retention_kernel(z_ref, dmask_ref, zeta_ref, xi_ref, cdec_ref, gng_ref, gnb_ref, o_ref, state_ref, prev_ref):
    @pl.when(pl.program_id(1) == 0)
    def _():
        state_ref[...] = jnp.zeros_like(state_ref)

    n_chunks = RET_ROWS // RET_CHUNK
    part = lambda ref, c, p, h: ref[c * RET_CHUNK:(c + 1) * RET_CHUNK,
                                    p * RET_W + h * RET_DK:p * RET_W + (h + 1) * RET_DK]

    for h in range(RET_HEADS):
        state = state_ref[h]
        for c in range(n_chunks):
            prev_ref[c, h] = state.astype(BF16)
            kz = (part(z_ref, c, 1, h).astype(F32) * zeta_ref[h]).astype(BF16)
            state = cdec_ref[h] * state + _dot_tn(kz, part(z_ref, c, 2, h))
        state_ref[h] = state

    out_rows = []
    for c in range(n_chunks):
        heads = []
        for h in range(RET_HEADS):
            q = part(z_ref, c, 0, h)
            scores = _dot_nt(q, part(z_ref, c, 1, h)) * dmask_ref[h]
            inner = _dot(scores.astype(BF16), part(z_ref, c, 2, h))
            cross = _dot(q, prev_ref[c, h]) * xi_ref[h]
            y = inner + cross
            mu = jnp.mean(y, axis=-1, keepdims=True)
            d = y - mu
            var = jnp.mean(d * d, axis=-1, keepdims=True)
            hs = slice(h * RET_DV, (h + 1) * RET_DV)
            yn = d * lax.rsqrt(var + EPS) * gng_ref[:, hs] + gnb_ref[:, hs]
            gate = part(z_ref, c, 3, h).astype(F32)
            heads.append((gate * _sigmoid(gate) * yn).astype(BF16))
        out_rows.append(jnp.concatenate(heads, axis=1))
    o_ref[...] = jnp.concatenate(out_rows, axis=0)


def _retention_tables():
    h = RET_HEADS
    c = RET_CHUNK
    log_gamma = jnp.log1p(-jnp.exp2(-5.0 - jnp.arange(h, dtype=F32)))
    pos = jnp.arange(c, dtype=F32)
    diff = pos[:, None] - pos[None, :]
    decay = jnp.where(diff[None] >= 0.0,
                      jnp.exp(jnp.maximum(diff, 0.0)[None] * log_gamma[:, None, None]), 0.0)
    scale = RET_DK ** -0.5
    dmask = decay * scale
    zeta = jnp.exp((c - 1.0 - pos)[None, :] * log_gamma[:, None]) * scale
    xi = jnp.exp((pos + 1.0)[None, :] * log_gamma[:, None])
    cdec = jnp.exp(c * log_gamma)
    bc = lambda a: jnp.broadcast_to(a[:, :, None], (h, c, c))
    return dmask, bc(zeta), bc(xi), jnp.broadcast_to(cdec[:, None, None], (h, c, c))


def _retention(z_ret, gn_g, gn_b, batch, seq):
    z3 = z_ret.reshape(batch, seq, Z_RET_W)
    tables = _retention_tables()
    tab_spec = pl.BlockSpec((RET_HEADS, RET_CHUNK, RET_CHUNK), lambda b, i: (0, 0, 0))
    vec_spec = pl.BlockSpec((1, RET_W), lambda b, i: (0, 0))
    out = pl.pallas_call(
        _retention_kernel,
        grid=(batch, seq // RET_ROWS),
        in_specs=[pl.BlockSpec((None, RET_ROWS, Z_RET_W), lambda b, i: (b, i, 0)),
                  tab_spec, tab_spec, tab_spec, tab_spec, vec_spec, vec_spec],
        out_specs=pl.BlockSpec((None, RET_ROWS, RET_W), lambda b, i: (b, i, 0)),
        out_shape=jax.ShapeDtypeStruct((batch, seq, RET_W), BF16),
        scratch_shapes=[pltpu.VMEM((RET_HEADS, RET_DK, RET_DV), F32),
                        pltpu.VMEM((RET_ROWS // RET_CHUNK, RET_HEADS, RET_DK, RET_DV), BF16)],
        compiler_params=pltpu.CompilerParams(dimension_semantics=("parallel", "arbitrary"),
                                             vmem_limit_bytes=VMEM_LIMIT),
        name="retention",
    )(z3, *tables, gn_g, gn_b)
    return out.reshape(batch * seq, RET_W)


def _swa_kernel(sinks_ref, q_ref, kvp_ref, kvc_ref, qg_ref, kg_ref, bq_ref, bk_ref, bias_ref, o_ref, band_ref):
    first = pl.program_id(1) == 0
    band_ref[:WINDOW, :] = kvp_ref[...]
    band_ref[WINDOW:, :] = kvc_ref[...]
    group_rows = SWA_GROUP * WINDOW
    lo_q = lax.broadcasted_iota(jnp.int32, (WINDOW, 2 * SWA_HD), 1) < SWA_HD
    lo_k = lax.broadcasted_iota(jnp.int32, (2 * WINDOW, 2 * SWA_HD), 1) < SWA_HD
    in_prev = lax.broadcasted_iota(jnp.int32, (group_rows, 2 * WINDOW), 1) < WINDOW
    row_head = lax.broadcasted_iota(jnp.int32, (group_rows, 1), 0) // WINDOW

    def both_heads(x):
        swapped = pltpu.roll(x, SWA_HD, axis=1)
        return jnp.where(lo_k, x, swapped), jnp.where(lo_k, swapped, x)

    def block(n, carry):
        r0 = pl.multiple_of(n * WINDOW, WINDOW)
        rows = pl.ds(r0, WINDOW)
        kv = band_ref[pl.ds(r0, 2 * WINDOW), :]
        kf = kv[:, :SWA_KV].astype(F32)
        kn = kf * lax.rsqrt(_dot((kf * kf).astype(BF16), bk_ref[...]) + EPS) * kg_ref[...]
        keys = [a.astype(BF16) for a in both_heads(kn)]
        vals = [a.astype(BF16) for a in both_heads(kv[:, SWA_KV:].astype(F32))]
        qf = q_ref[rows, :].astype(F32)
        qn = qf * lax.rsqrt(_dot((qf * qf).astype(BF16), bq_ref[...]) + EPS) * qg_ref[...]
        no_prev = jnp.logical_and(first, n == 0)
        outs = []
        for kh in range(SWA_KV_HEADS):
            parts = []
            for j in range(SWA_GROUP // 2):
                c0 = (kh * SWA_GROUP + 2 * j) * SWA_HD
                pair = qn[:, c0:c0 + 2 * SWA_HD]
                parts += [jnp.where(lo_q, pair, 0.0), jnp.where(lo_q, 0.0, pair)]
            qs = jnp.concatenate(parts, axis=0).astype(BF16)
            s = _dot_nt(qs, keys[kh]) + bias_ref[kh]
            s = jnp.where(jnp.logical_and(no_prev, in_prev), NEG_INF, s)
            sink = sinks_ref[kh * SWA_GROUP]
            for g in range(1, SWA_GROUP):
                sink = jnp.where(row_head == g, sinks_ref[kh * SWA_GROUP + g], sink)
            m = jnp.maximum(jnp.max(s, axis=-1, keepdims=True), sink)
            p = jnp.exp(s - m)
            denom = jnp.sum(p, axis=-1, keepdims=True) + jnp.exp(sink - m)
            o = _dot(p.astype(BF16), vals[kh]) * (1.0 / denom)
            for j in range(SWA_GROUP // 2):
                even = o[(2 * j) * WINDOW:(2 * j + 1) * WINDOW]
                odd = o[(2 * j + 1) * WINDOW:(2 * j + 2) * WINDOW]
                outs.append(jnp.where(lo_q, even, odd).astype(BF16))
        o_ref[rows, :] = jnp.concatenate(outs, axis=1)
        return carry

    lax.fori_loop(0, SWA_ROWS // WINDOW, block, 0)


def _swa_tables(q_g, k_g):
    qi = jnp.arange(WINDOW)[:, None]
    sj = jnp.arange(2 * WINDOW)[None, :]
    rel = qi + WINDOW - sj
    ok = (rel >= 0) & (rel < WINDOW)
    slopes = jnp.exp2(-8.0 * jnp.arange(1, SWA_HEADS + 1, dtype=F32) / SWA_HEADS)
    bias = jnp.where(ok[None], -slopes[:, None, None] * rel.astype(F32)[None], NEG_INF)
    bias = bias.reshape(SWA_KV_HEADS, SWA_GROUP * WINDOW, 2 * WINDOW)
    avg = lambda heads: jnp.kron(jnp.eye(heads, dtype=F32), jnp.full((SWA_HD, SWA_HD), 1.0 / SWA_HD, F32)).astype(BF16)
    qg = jnp.tile(q_g.reshape(1, SWA_HD), (1, SWA_HEADS)) * (SWA_HD ** -0.5)
    kg = jnp.tile(k_g.reshape(1, SWA_HD), (1, SWA_KV_HEADS))
    return qg, kg, avg(SWA_HEADS), avg(SWA_KV_HEADS), bias


def _swa(z_sq, z_skv, sinks, q_g, k_g, batch, seq):
    q3 = z_sq.reshape(batch, seq, SWA_Q)
    kv3 = z_skv.reshape(batch, seq, Z_SKV_W)
    per = SWA_ROWS // WINDOW
    qg, kg, bq, bk, bias = _swa_tables(q_g, k_g)
    full = lambda a: pl.BlockSpec(a.shape, lambda b, i: (0,) * a.ndim)
    out = pl.pallas_call(
        _swa_kernel,
        grid=(batch, seq // SWA_ROWS),
        in_specs=[pl.BlockSpec(memory_space=pltpu.SMEM),
                  pl.BlockSpec((None, SWA_ROWS, SWA_Q), lambda b, i: (b, i, 0)),
                  pl.BlockSpec((None, WINDOW, Z_SKV_W), lambda b, i: (b, jnp.maximum(i * per - 1, 0), 0)),
                  pl.BlockSpec((None, SWA_ROWS, Z_SKV_W), lambda b, i: (b, i, 0)),
                  full(qg), full(kg), full(bq), full(bk), full(bias)],
        out_specs=pl.BlockSpec((None, SWA_ROWS, SWA_Q), lambda b, i: (b, i, 0)),
        out_shape=jax.ShapeDtypeStruct((batch, seq, SWA_Q), BF16),
        scratch_shapes=[pltpu.VMEM((SWA_ROWS + WINDOW, Z_SKV_W), BF16)],
        compiler_params=pltpu.CompilerParams(dimension_semantics=("parallel", "arbitrary"),
                                             vmem_limit_bytes=VMEM_LIMIT),
        name="swa",
    )(sinks, q3, kv3, kv3, qg, kg, bq, bk, bias)
    return out.reshape(batch * seq, SWA_Q)


def _merge_kernel(yr_ref, ys_ref, zg_ref, x_ref, wro_ref, wso_ref, wout_ref, fg_ref, wr_ref, br_ref,
                  x1_ref, hp_ref, route_ref, cnt_ref):
    subs = [_merge_rows(slice(s * MERGE_SUB, (s + 1) * MERGE_SUB), yr_ref, ys_ref, zg_ref, x_ref, wro_ref,
                        wso_ref, wout_ref, fg_ref, wr_ref, br_ref) for s in range(MERGE_TM // MERGE_SUB)]
    x1_ref[...] = jnp.concatenate([s[0] for s in subs], axis=0)
    hp_ref[...] = jnp.concatenate([s[1] for s in subs], axis=0)
    route_ref[...] = jnp.concatenate([s[2] for s in subs], axis=0)
    cnt_ref[...] = sum(s[3] for s in subs)


def _merge_rows(rows, yr_ref, ys_ref, zg_ref, x_ref, wro_ref, wso_ref, wout_ref, fg_ref, wr_ref, br_ref):
    a = _dot(yr_ref[rows, :], wro_ref[...])
    b = _dot(ys_ref[rows, :], wso_ref[...])
    gate_r = zg_ref[rows, :D_MODEL].astype(F32)
    gate_s = zg_ref[rows, D_MODEL:].astype(F32)
    merged = (_sigmoid(gate_r) * a + _sigmoid(gate_s) * b).astype(BF16)
    x1 = x_ref[rows, :] + _dot(merged, wout_ref[...])
    h2 = _rms(x1, fg_ref[...])
    packed = _pack_halves(h2)

    h_hi = h2.astype(BF16)
    h_lo = (h2 - h_hi.astype(F32)).astype(BF16)
    parts = _dot(h_hi, wr_ref[...]) + _dot(h_lo, wr_ref[...])
    logits = parts[:, :ROUTE_LANES] + parts[:, ROUTE_LANES:] + br_ref[...]
    lane = lax.broadcasted_iota(jnp.int32, logits.shape, 1)
    big = jnp.int32(ROUTE_LANES)
    is_group = jnp.logical_and(lane >= LANE_GROUP0, lane < LANE_GROUP0 + N_GROUPS)
    gl = jnp.where(is_group, logits, NEG_INF)
    gmax = jnp.max(gl, axis=-1, keepdims=True)
    g_w = 1.0 / jnp.sum(jnp.exp(gl - gmax), axis=-1, keepdims=True)
    g_sel = jnp.min(jnp.where(gl == gmax, lane, big), axis=-1, keepdims=True) - LANE_GROUP0
    in_group = jnp.logical_and(lane < N_EXPERTS, (lane >> 3) == g_sel)
    el = jnp.where(in_group, logits, NEG_INF)
    m1 = jnp.max(el, axis=-1, keepdims=True)
    i1 = jnp.min(jnp.where(el == m1, lane, big), axis=-1, keepdims=True)
    el2 = jnp.where(lane == i1, NEG_INF, el)
    m2 = jnp.max(el2, axis=-1, keepdims=True)
    i2 = jnp.min(jnp.where(el2 == m2, lane, big), axis=-1, keepdims=True)
    e2 = jnp.exp(m2 - m1)
    c1 = g_w / (1.0 + e2)
    c2 = g_w * e2 / (1.0 + e2)
    chosen = jnp.where(jnp.logical_or(lane == i1, lane == i2), 1.0, 0.0)
    route = (chosen
             + jnp.where(lane == LANE_E1, i1.astype(F32), 0.0)
             + jnp.where(lane == LANE_E2, i2.astype(F32), 0.0)
             + jnp.where(lane == LANE_C1, c1, 0.0)
             + jnp.where(lane == LANE_C2, c2, 0.0))
    return x1, packed, route, jnp.sum(chosen, axis=0, keepdims=True)


def _merge(y_r, y_s, z_gate, x2d, w_ret_o, w_swa_o, w_out, ffn_g, w_router, b_router):
    t = x2d.shape[0]
    row = lambda w: pl.BlockSpec((MERGE_TM, w), lambda i: (i, 0))
    full = lambda a: pl.BlockSpec(a.shape, lambda i: (0,) * a.ndim)
    return pl.pallas_call(
        _merge_kernel,
        grid=(t // MERGE_TM,),
        in_specs=[row(RET_W), row(SWA_Q), row(Z_GATE_W), row(D_MODEL),
                  full(w_ret_o), full(w_swa_o), full(w_out), full(ffn_g), full(w_router), full(b_router)],
        out_specs=[row(D_MODEL), row(HALF_D), row(ROUTE_LANES),
                   pl.BlockSpec((None, 1, ROUTE_LANES), lambda i: (i, 0, 0))],
        out_shape=[jax.ShapeDtypeStruct((t, D_MODEL), F32),
                   jax.ShapeDtypeStruct((t, HALF_D), I32),
                   jax.ShapeDtypeStruct((t, ROUTE_LANES), F32),
                   jax.ShapeDtypeStruct((t // MERGE_TM, 1, ROUTE_LANES), F32)],
        compiler_params=pltpu.CompilerParams(dimension_semantics=("parallel",),
                                             vmem_limit_bytes=VMEM_LIMIT),
        name="merge_router",
    )(y_r, y_s, z_gate, x2d, w_ret_o, w_swa_o, w_out, ffn_g, w_router, b_router)


def _route_pos_kernel(route_ref, base_ref, pos_ref):
    route = route_ref[...]
    lane = lax.broadcasted_iota(jnp.int32, route.shape, 1)
    lane_f = lane.astype(F32)
    chosen = jnp.where(lane < N_EXPERTS, route, 0.0)
    r = lax.broadcasted_iota(jnp.int32, (ROUTE_TM, ROUTE_TM), 0)
    c = lax.broadcasted_iota(jnp.int32, (ROUTE_TM, ROUTE_TM), 1)
    tri = jnp.where(c <= r, 1.0, 0.0).astype(BF16)
    incl = _dot(tri, chosen.astype(BF16))
    posmat = base_ref[...] + incl - 1.0
    pick = lambda ln: jnp.sum(jnp.where(lane == ln, route, 0.0), axis=-1, keepdims=True)
    pos_of = lambda e: jnp.sum(jnp.where(lane_f == e, posmat, 0.0), axis=-1, keepdims=True)
    p1 = pos_of(pick(LANE_E1))
    p2 = pos_of(pick(LANE_E2))
    rec = jnp.where(lane == 0, p1, 0.0) + jnp.where(lane == 1, p2, 0.0)
    pos_ref[...] = rec.T[:8, :].astype(I32)


def _route_pos(route, base):
    t = route.shape[0]
    n = t // ROUTE_TM
    return pl.pallas_call(
        _route_pos_kernel,
        grid=(n,),
        in_specs=[pl.BlockSpec((ROUTE_TM, ROUTE_LANES), lambda i: (i, 0)),
                  pl.BlockSpec((None, 1, ROUTE_LANES), lambda i: (i, 0, 0))],
        out_specs=pl.BlockSpec((None, 8, ROUTE_TM), lambda i: (i, 0, 0)),
        out_shape=jax.ShapeDtypeStruct((n, 8, ROUTE_TM), I32),
        compiler_params=pltpu.CompilerParams(dimension_semantics=("parallel",),
                                             vmem_limit_bytes=VMEM_LIMIT),
        name="route_pos",
    )(route, base)


def _plan_kernel(cnt_ref, base_ref, owner_ref, valid_ref):
    counts = cnt_ref[...]
    nt = counts.shape[0]
    exact = functools.partial(jnp.dot, precision=lax.Precision.HIGHEST, preferred_element_type=F32)
    total = jnp.sum(counts, axis=0, keepdims=True)
    tiles_e = jnp.ceil(total * (1.0 / ROW_TILE))
    k = lax.broadcasted_iota(jnp.int32, (ROUTE_LANES, ROUTE_LANES), 0)
    lane = lax.broadcasted_iota(jnp.int32, (ROUTE_LANES, ROUTE_LANES), 1)
    tile_end = exact(jnp.broadcast_to(tiles_e, (8, ROUTE_LANES)), jnp.where(k <= lane, 1.0, 0.0))[0:1]
    tile_start = tile_end - tiles_e
    r = lax.broadcasted_iota(jnp.int32, (nt, nt), 0)
    c = lax.broadcasted_iota(jnp.int32, (nt, nt), 1)
    before = exact(jnp.where(c < r, 1.0, 0.0), counts)
    base_ref[...] = tile_start * ROW_TILE + before

    used = jnp.max(tile_end, axis=-1, keepdims=True)
    tile = k.astype(F32)
    clamped = jnp.minimum(tile, jnp.maximum(used - 1.0, 0.0))
    is_expert = lane < N_EXPERTS
    owner = jnp.sum(jnp.where(jnp.logical_and(is_expert, tile_end <= clamped), 1.0, 0.0), axis=-1, keepdims=True)
    owner = jnp.minimum(owner, N_EXPERTS - 1.0)
    mine = lane.astype(F32) == owner
    pick = lambda v: jnp.sum(jnp.where(mine, v, 0.0), axis=-1, keepdims=True)
    t1 = tile[:, 0:1]
    valid = jnp.clip(pick(total) - (t1 - pick(tile_start)) * ROW_TILE, 0.0, float(ROW_TILE))
    valid = jnp.where(t1 < used, valid, 0.0)
    owner_ref[...] = jnp.broadcast_to(owner, owner_ref.shape).astype(I32)
    valid_ref[...] = jnp.broadcast_to(valid, valid_ref.shape).astype(I32)


def _tile_plan(cnt, n_tiles):
    assert n_tiles <= ROUTE_LANES
    nt = cnt.shape[0]
    sq = (ROUTE_LANES, ROUTE_LANES)
    base, owner, valid = pl.pallas_call(
        _plan_kernel,
        out_shape=[jax.ShapeDtypeStruct((nt, ROUTE_LANES), F32), jax.ShapeDtypeStruct(sq, I32),
                   jax.ShapeDtypeStruct(sq, I32)],
        name="tile_plan",
    )(cnt.reshape(nt, ROUTE_LANES))
    return base.reshape(cnt.shape), owner[:n_tiles, 0], valid[:n_tiles, 0]


def _sc_mesh():
    return plsc.VectorSubcoreMesh(core_axis_name="c", subcore_axis_name="s")


def _sc_worker():
    return lax.axis_index("s") * SC_CORES + lax.axis_index("c")


def _dispatch(hp, pos1, pos2, n_rows):
    t = hp.shape[0]
    per_worker = t // SC_WORKERS
    k = per_worker // SC_CHUNK
    idx = pltpu.VMEM((k, SC_CHUNK), I32)

    @functools.partial(pl.kernel, mesh=_sc_mesh(), out_type=jax.ShapeDtypeStruct((n_rows, HALF_D), I32),
                       scratch_types=[idx, idx, pltpu.VMEM((SC_CHUNK, HALF_D), I32)], name="moe_dispatch")
    def run(h_hbm, p1_hbm, p2_hbm, xs_hbm, p1_v, p2_v, rows_v):
        wid = _sc_worker()
        pltpu.sync_copy(p1_hbm.at[wid], p1_v)
        pltpu.sync_copy(p2_hbm.at[wid], p2_v)
        for j in range(k):
            pltpu.sync_copy(h_hbm.at[pl.ds(wid * per_worker + j * SC_CHUNK, SC_CHUNK)], rows_v)
            pltpu.sync_copy(rows_v, xs_hbm.at[p1_v.at[j]])
            pltpu.sync_copy(rows_v, xs_hbm.at[p2_v.at[j]])

    return run(hp, pos1.reshape(SC_WORKERS, k, SC_CHUNK), pos2.reshape(SC_WORKERS, k, SC_CHUNK))


def _combine_gather(ys, pos1, pos2, t):
    per_worker = t // SC_WORKERS
    k = per_worker // SC_CHUNK
    idx = pltpu.VMEM((k, SC_CHUNK), I32)
    out = jax.ShapeDtypeStruct((t, HALF_D), I32)

    @functools.partial(pl.kernel, mesh=_sc_mesh(), out_type=(out, out),
                       scratch_types=[idx, idx, pltpu.VMEM((SC_CHUNK, HALF_D), I32)], name="moe_combine")
    def run(ys_hbm, p1_hbm, p2_hbm, g1_hbm, g2_hbm, p1_v, p2_v, rows_v):
        wid = _sc_worker()
        pltpu.sync_copy(p1_hbm.at[wid], p1_v)
        pltpu.sync_copy(p2_hbm.at[wid], p2_v)
        for j in range(k):
            dst = pl.ds(wid * per_worker + j * SC_CHUNK, SC_CHUNK)
            pltpu.sync_copy(ys_hbm.at[p1_v.at[j]], rows_v)
            pltpu.sync_copy(rows_v, g1_hbm.at[dst])
            pltpu.sync_copy(ys_hbm.at[p2_v.at[j]], rows_v)
            pltpu.sync_copy(rows_v, g2_hbm.at[dst])

    return run(ys, pos1.reshape(SC_WORKERS, k, SC_CHUNK), pos2.reshape(SC_WORKERS, k, SC_CHUNK))


def _experts_kernel(owner_ref, valid_ref, xs_ref, wg_ref, wu_ref, wd_ref, ys_ref, wg_bf, wu_bf, wd_bf):
    i = pl.program_id(0)
    valid = valid_ref[i]

    @pl.when(jnp.logical_or(i == 0, owner_ref[i] != owner_ref[jnp.maximum(i - 1, 0)]))
    def _():
        wg_bf[...] = wg_ref[...].astype(BF16)
        wu_bf[...] = wu_ref[...].astype(BF16)
        wd_bf[...] = wd_ref[...].astype(BF16)

    @pl.when(valid > 0)
    def _():
        row = lax.broadcasted_iota(jnp.int32, (EXPERT_SUB, HALF_D), 0)
        outs = []
        for s in range(ROW_TILE // EXPERT_SUB):
            words = xs_ref[s * EXPERT_SUB:(s + 1) * EXPERT_SUB, :]
            packed = jnp.where(row < valid - s * EXPERT_SUB, words, 0)
            hi, lo = _unpack_halves(packed)
            hi = hi.astype(BF16)
            lo = lo.astype(BF16)
            hg = _dot(hi, wg_bf[:HALF_D, :]) + _dot(lo, wg_bf[HALF_D:, :])
            hu = _dot(hi, wu_bf[:HALF_D, :]) + _dot(lo, wu_bf[HALF_D:, :])
            hid = (hg * _sigmoid(hg) * hu).astype(BF16)
            outs.append(_pack_halves(_dot(hid, wd_bf[...])))
        ys_ref[...] = jnp.concatenate(outs, axis=0)

    @pl.when(valid == 0)
    def _():
        ys_ref[...] = jnp.zeros_like(ys_ref)


def _experts(xs, owner, valid, wg, wu, wd):
    n_tiles = owner.shape[0]
    grid_spec = pltpu.PrefetchScalarGridSpec(
        num_scalar_prefetch=2,
        grid=(n_tiles,),
        in_specs=[pl.BlockSpec((ROW_TILE, HALF_D), lambda i, own, val: (i, 0)),
                  pl.BlockSpec((None, D_MODEL, EXPERT_FF), lambda i, own, val: (own[i], 0, 0)),
                  pl.BlockSpec((None, D_MODEL, EXPERT_FF), lambda i, own, val: (own[i], 0, 0)),
                  pl.BlockSpec((None, EXPERT_FF, D_MODEL), lambda i, own, val: (own[i], 0, 0))],
        out_specs=pl.BlockSpec((ROW_TILE, HALF_D), lambda i, own, val: (i, 0)),
        scratch_shapes=[pltpu.VMEM((D_MODEL, EXPERT_FF), BF16), pltpu.VMEM((D_MODEL, EXPERT_FF), BF16),
                        pltpu.VMEM((EXPERT_FF, D_MODEL), BF16)],
    )
    return pl.pallas_call(
        _experts_kernel,
        grid_spec=grid_spec,
        out_shape=jax.ShapeDtypeStruct((n_tiles * ROW_TILE, HALF_D), I32),
        compiler_params=pltpu.CompilerParams(dimension_semantics=("arbitrary",),
                                             vmem_limit_bytes=VMEM_LIMIT),
        name="experts",
    )(owner, valid, xs, wg, wu, wd)


def _ple_kernel(x1_ref, g1_ref, g2_ref, route_ref, p_ref, wp_ref, pg_ref, gg_ref, wgate_ref, o_ref):
    route = route_ref[...]
    lane = lax.broadcasted_iota(jnp.int32, route.shape, 1)
    pick = lambda ln: jnp.sum(jnp.where(lane == ln, route, 0.0), axis=-1, keepdims=True)
    c1 = pick(LANE_C1)
    c2 = pick(LANE_C2)
    a_hi, a_lo = _unpack_halves(g1_ref[...])
    b_hi, b_lo = _unpack_halves(g2_ref[...])
    x = x1_ref[...] + jnp.concatenate([c1 * a_hi + c2 * b_hi, c1 * a_lo + c2 * b_lo], axis=1)
    ple = _rms(_dot(p_ref[...].astype(BF16), wp_ref[...]), pg_ref[...])
    gate = _sigmoid(_dot(_rms(x, gg_ref[...]).astype(BF16), wgate_ref[...]))
    o_ref[...] = x + gate * ple


def _ple(x1, g1, g2, route, p2d, w_ple, ple_g, gate_g, w_gate):
    t = x1.shape[0]
    row = lambda w: pl.BlockSpec((PLE_TM, w), lambda i: (i, 0))
    full = lambda a: pl.BlockSpec(a.shape, lambda i: (0,) * a.ndim)
    return pl.pallas_call(
        _ple_kernel,
        grid=(t // PLE_TM,),
        in_specs=[row(D_MODEL), row(HALF_D), row(HALF_D), row(ROUTE_LANES), row(PLE_DIM),
                  full(w_ple), full(ple_g), full(gate_g), full(w_gate)],
        out_specs=row(D_MODEL),
        out_shape=jax.ShapeDtypeStruct((t, D_MODEL), F32),
        compiler_params=pltpu.CompilerParams(dimension_semantics=("parallel",),
                                             vmem_limit_bytes=VMEM_LIMIT),
        name="ple",
    )(x1, g1, g2, route, p2d, w_ple, ple_g, gate_g, w_gate)


def kernel(x, p, mix_norm_g, w_in, ret_gn_g, ret_gn_b, w_ret_o, q_norm_g, k_norm_g, attn_sinks,
           w_swa_o, w_out, ffn_norm_g, w_router_group, b_router_group, w_router_expert,
           b_router_expert, w_exp_gate, w_exp_up, w_exp_down, ple_gate_norm_g, w_ple_gate,
           w_ple, ple_norm_g):
    batch, seq, d = x.shape
    t = batch * seq
    depth = w_in.shape[0]
    n_tiles = 2 * t // ROW_TILE + N_EXPERTS
    row = lambda a: a.reshape(1, -1)
    x2d = x.reshape(t, d)
    for i in range(depth):
        z_ret, z_sq, z_skv, z_gate = _in_proj(x2d, row(mix_norm_g[i]), w_in[i].astype(BF16))
        y_r = _retention(z_ret, row(ret_gn_g[i]), row(ret_gn_b[i]), batch, seq)
        y_s = _swa(z_sq, z_skv, attn_sinks[i], row(q_norm_g[i]), row(k_norm_g[i]), batch, seq)
        pad = ROUTE_LANES - N_GROUPS - N_EXPERTS
        w_router = jnp.pad(jnp.concatenate([w_router_expert[i], w_router_group[i]], axis=1), ((0, 0), (0, pad)))
        b_router = jnp.pad(jnp.concatenate([b_router_expert[i], b_router_group[i]]), (0, pad)).reshape(1, -1)
        w_router_hi = w_router.astype(BF16)
        w_router = jnp.concatenate([w_router_hi, (w_router - w_router_hi.astype(F32)).astype(BF16)], axis=1)
        x1, hp, route, cnt = _merge(y_r, y_s, z_gate, x2d, w_ret_o[i].astype(BF16), w_swa_o[i].astype(BF16),
                                    w_out[i].astype(BF16), row(ffn_norm_g[i]), w_router, b_router)
        base, owner, valid = _tile_plan(cnt, n_tiles)
        pos = _route_pos(route, base)
        pos1 = pos[:, 0, :]
        pos2 = pos[:, 1, :]
        xs = _dispatch(hp, pos1, pos2, n_tiles * ROW_TILE)
        ys = _experts(xs, owner, valid, w_exp_gate[i], w_exp_up[i], w_exp_down[i])
        g1, g2 = _combine_gather(ys, pos1, pos2, t)
        x2d = _ple(x1, g1, g2, route, p[i].reshape(t, PLE_DIM), w_ple[i].astype(BF16), row(ple_norm_g[i]),
                   row(ple_gate_norm_g[i]), w_ple_gate[i].astype(BF16))
    return x2d.reshape(batch, seq, d)
```

```python
import functools

import jax
import jax.numpy as jnp
from jax import lax
from jax.experimental import pallas as pl
from jax.experimental.pallas import tpu as pltpu
from jax.experimental.pallas import tpu_sc as plsc

F32 = jnp.float32
BF16 = jnp.bfloat16
I32 = jnp.int32

EPS = 1e-6
D_MODEL = 1024
PLE_DIM = 256
RET_HEADS = 4
RET_DK = 128
RET_DV = 128
RET_CHUNK = 128
SWA_HEADS = 8
SWA_KV_HEADS = 2
SWA_GROUP = SWA_HEADS // SWA_KV_HEADS
SWA_HD = 64
WINDOW = 128
N_GROUPS = 4
EXPERTS_PER_GROUP = 8
N_EXPERTS = N_GROUPS * EXPERTS_PER_GROUP
EXPERT_FF = 256

RET_W = RET_HEADS * RET_DK
SWA_Q = SWA_HEADS * SWA_HD
SWA_KV = SWA_KV_HEADS * SWA_HD
Z_RET_W = 4 * RET_W
Z_SKV_W = 2 * SWA_KV
Z_GATE_W = 2 * D_MODEL
IN_WIDTH = Z_RET_W + SWA_Q + Z_SKV_W + Z_GATE_W

ROUTE_LANES = 128
LANE_GROUP0 = N_EXPERTS
LANE_E1, LANE_E2, LANE_C1, LANE_C2 = 32, 33, 34, 35
NEG_INF = -1e30

HALF_D = D_MODEL // 2
ROW_TILE = 512
SC_CORES = 2
SC_SUBCORES = 16
SC_WORKERS = SC_CORES * SC_SUBCORES
SC_CHUNK = 128

VMEM_LIMIT = 56 * 1024 * 1024

IN_TM = 512
RET_ROWS = 512
SWA_ROWS = 512
MERGE_TM = 512
MERGE_SUB = 256
ROUTE_TM = 512
PLE_TM = 512


def _rms(x, g):
    ms = jnp.mean(x * x, axis=-1, keepdims=True)
    return x * lax.rsqrt(ms + EPS) * g


def _sigmoid(x):
    return 1.0 / (1.0 + jnp.exp(-x))


def _dot(a, b):
    return jnp.dot(a, b, preferred_element_type=F32)


def _dot_nt(a, b):
    return lax.dot_general(a, b, (((1,), (1,)), ((), ())), preferred_element_type=F32)


def _dot_tn(a, b):
    return lax.dot_general(a, b, (((0,), (0,)), ((), ())), preferred_element_type=F32)


def _pack_halves(v):
    return pltpu.pack_elementwise([v[:, :HALF_D], v[:, HALF_D:]], packed_dtype=BF16)


def _unpack_halves(p):
    return tuple(pltpu.unpack_elementwise(p, index=k, packed_dtype=BF16, unpacked_dtype=F32) for k in range(2))


def _in_proj_kernel(x_ref, g_ref, w_ref, zret_ref, zsq_ref, zskv_ref, zgate_ref):
    h = _rms(x_ref[...], g_ref[...]).astype(BF16)

    def proj(c0, width):
        return _dot(h, w_ref[:, c0:c0 + width]).astype(BF16)

    cw = 512
    for j in range(Z_RET_W // cw):
        zret_ref[:, j * cw:(j + 1) * cw] = proj(j * cw, cw)
    zsq_ref[...] = proj(Z_RET_W, SWA_Q)
    zskv_ref[...] = proj(Z_RET_W + SWA_Q, Z_SKV_W)
    g0 = Z_RET_W + SWA_Q + Z_SKV_W
    for j in range(Z_GATE_W // cw):
        zgate_ref[:, j * cw:(j + 1) * cw] = proj(g0 + j * cw, cw)


def _in_proj(x2d, g, w_bf):
    t = x2d.shape[0]
    row = lambda w: pl.BlockSpec((IN_TM, w), lambda i: (i, 0))
    full = lambda a: pl.BlockSpec(a.shape, lambda i: (0,) * a.ndim)
    return pl.pallas_call(
        _in_proj_kernel,
        grid=(t // IN_TM,),
        in_specs=[row(D_MODEL), full(g), full(w_bf)],
        out_specs=[row(Z_RET_W), row(SWA_Q), row(Z_SKV_W), row(Z_GATE_W)],
        out_shape=[jax.ShapeDtypeStruct((t, w), BF16) for w in (Z_RET_W, SWA_Q, Z_SKV_W, Z_GATE_W)],
        compiler_params=pltpu.CompilerParams(dimension_semantics=("parallel",),
                                             vmem_limit_bytes=VMEM_LIMIT),
        name="in_proj",
    )(x2d, g, w_bf)


def _retention_kernel(z_ref, dmask_ref, zeta_ref, xi_ref, cdec_ref, gng_ref, gnb_ref, o_ref, state_ref, prev_ref):
    @pl.when(pl.program_id(1) == 0)
    def _():
        state_ref[...] = jnp.zeros_like(state_ref)

    n_chunks = RET_ROWS // RET_CHUNK
    part = lambda ref, c, p, h: ref[c * RET_CHUNK:(c + 1) * RET_CHUNK,
                                    p * RET_W + h * RET_DK:p * RET_W + (h + 1) * RET_DK]

    for h in range(RET_HEADS):
        state = state_ref[h]
        for c in range(n_chunks):
            prev_ref[c, h] = state.astype(BF16)
            kz = (part(z_ref, c, 1, h).astype(F32) * zeta_ref[h]).astype(BF16)
            state = cdec_ref[h] * state + _dot_tn(kz, part(z_ref, c, 2, h))
        state_ref[h] = state

    out_rows = []
    for c in range(n_chunks):
        heads = []
        for h in range(RET_HEADS):
            q = part(z_ref, c, 0, h)
            scores = _dot_nt(q, part(z_ref, c, 1, h)) * dmask_ref[h]
            inner = _dot(scores.astype(BF16), part(z_ref, c, 2, h))
            cross = _dot(q, prev_ref[c, h]) * xi_ref[h]
            y = inner + cross
            mu = jnp.mean(y, axis=-1, keepdims=True)
            d = y - mu
            var = jnp.mean(d * d, axis=-1, keepdims=True)
            hs = slice(h * RET_DV, (h + 1) * RET_DV)
            yn = d * lax.rsqrt(var + EPS) * gng_ref[:, hs] + gnb_ref[:, hs]
            gate = part(z_ref, c, 3, h).astype(F32)
            heads.append((gate * _sigmoid(gate) * yn).astype(BF16))
        out_rows.append(jnp.concatenate(heads, axis=1))
    o_ref[...] = jnp.concatenate(out_rows, axis=0)


def _retention_tables():
    h = RET_HEADS
    c = RET_CHUNK
    log_gamma = jnp.log1p(-jnp.exp2(-5.0 - jnp.arange(h, dtype=F32)))
    pos = jnp.arange(c, dtype=F32)
    diff = pos[:, None] - pos[None, :]
    decay = jnp.where(diff[None] >= 0.0,
                      jnp.exp(jnp.maximum(diff, 0.0)[None] * log_gamma[:, None, None]), 0.0)
    scale = RET_DK ** -0.5
    dmask = decay * scale
    zeta = jnp.exp((c - 1.0 - pos)[None, :] * log_gamma[:, None]) * scale
    xi = jnp.exp((pos + 1.0)[None, :] * log_gamma[:, None])
    cdec = jnp.exp(c * log_gamma)
    bc = lambda a: jnp.broadcast_to(a[:, :, None], (h, c, c))
    return dmask, bc(zeta), bc(xi), jnp.broadcast_to(cdec[:, None, None], (h, c, c))


def _retention(z_ret, gn_g, gn_b, batch, seq):
    z3 = z_ret.reshape(batch, seq, Z_RET_W)
    tables = _retention_tables()
    tab_spec = pl.BlockSpec((RET_HEADS, RET_CHUNK, RET_CHUNK), lambda b, i: (0, 0, 0))
    vec_spec = pl.BlockSpec((1, RET_W), lambda b, i: (0, 0))
    out = pl.pallas_call(
        _retention_kernel,
        grid=(batch, seq // RET_ROWS),
        in_specs=[pl.BlockSpec((None, RET_ROWS, Z_RET_W), lambda b, i: (b, i, 0)),
                  tab_spec, tab_spec, tab_spec, tab_spec, vec_spec, vec_spec],
        out_specs=pl.BlockSpec((None, RET_ROWS, RET_W), lambda b, i: (b, i, 0)),
        out_shape=jax.ShapeDtypeStruct((batch, seq, RET_W), BF16),
        scratch_shapes=[pltpu.VMEM((RET_HEADS, RET_DK, RET_DV), F32),
                        pltpu.VMEM((RET_ROWS // RET_CHUNK, RET_HEADS, RET_DK, RET_DV), BF16)],
        compiler_params=pltpu.CompilerParams(dimension_semantics=("parallel", "arbitrary"),
                                             vmem_limit_bytes=VMEM_LIMIT),
        name="retention",
    )(z3, *tables, gn_g, gn_b)
    return out.reshape(batch * seq, RET_W)


def _swa_kernel(sinks_ref, q_ref, kvp_ref, kvc_ref, qg_ref, kg_ref, bq_ref, bk_ref, bias_ref, o_ref, band_ref):
    first = pl.program_id(1) == 0
    band_ref[:WINDOW, :] = kvp_ref[...]
    band_ref[WINDOW:, :] = kvc_ref[...]
    group_rows = SWA_GROUP * WINDOW
    lo_q = lax.broadcasted_iota(jnp.int32, (WINDOW, 2 * SWA_HD), 1) < SWA_HD
    lo_k = lax.broadcasted_iota(jnp.int32, (2 * WINDOW, 2 * SWA_HD), 1) < SWA_HD
    in_prev = lax.broadcasted_iota(jnp.int32, (group_rows, 2 * WINDOW), 1) < WINDOW
    row_head = lax.broadcasted_iota(jnp.int32, (group_rows, 1), 0) // WINDOW

    def both_heads(x):
        swapped = pltpu.roll(x, SWA_HD, axis=1)
        return jnp.where(lo_k, x, swapped), jnp.where(lo_k, swapped, x)

    def block(n, carry):
        r0 = pl.multiple_of(n * WINDOW, WINDOW)
        rows = pl.ds(r0, WINDOW)
        kv = band_ref[pl.ds(r0, 2 * WINDOW), :]
        kf = kv[:, :SWA_KV].astype(F32)
        kn = kf * lax.rsqrt(_dot((kf * kf).astype(BF16), bk_ref[...]) + EPS) * kg_ref[...]
        keys = [a.astype(BF16) for a in both_heads(kn)]
        vals = [a.astype(BF16) for a in both_heads(kv[:, SWA_KV:].astype(F32))]
        qf = q_ref[rows, :].astype(F32)
        qn = qf * lax.rsqrt(_dot((qf * qf).astype(BF16), bq_ref[...]) + EPS) * qg_ref[...]
        no_prev = jnp.logical_and(first, n == 0)
        outs = []
        for kh in range(SWA_KV_HEADS):
            parts = []
            for j in range(SWA_GROUP // 2):
                c0 = (kh * SWA_GROUP + 2 * j) * SWA_HD
                pair = qn[:, c0:c0 + 2 * SWA_HD]
                parts += [jnp.where(lo_q, pair, 0.0), jnp.where(lo_q, 0.0, pair)]
            qs = jnp.concatenate(parts, axis=0).astype(BF16)
            s = _dot_nt(qs, keys[kh]) + bias_ref[kh]
            s = jnp.where(jnp.logical_and(no_prev, in_prev), NEG_INF, s)
            sink = sinks_ref[kh * SWA_GROUP]
            for g in range(1, SWA_GROUP):
                sink = jnp.where(row_head == g, sinks_ref[kh * SWA_GROUP + g], sink)
            m = jnp.maximum(jnp.max(s, axis=-1, keepdims=True), sink)
            p = jnp.exp(s - m)
            denom = jnp.sum(p, axis=-1, keepdims=True) + jnp.exp(sink - m)
            o = _dot(p.astype(BF16), vals[kh]) * (1.0 / denom)
            for j in range(SWA_GROUP // 2):
                even = o[(2 * j) * WINDOW:(2 * j + 1) * WINDOW]
                odd = o[(2 * j + 1) * WINDOW:(2 * j + 2) * WINDOW]
                outs.append(jnp.where(lo_q, even, odd).astype(BF16))
        o_ref[rows, :] = jnp.concatenate(outs, axis=1)
        return carry

    lax.fori_loop(0, SWA_ROWS // WINDOW, block, 0)


def _swa_tables(q_g, k_g):
    qi = jnp.arange(WINDOW)[:, None]
    sj = jnp.arange(2 * WINDOW)[None, :]
    rel = qi + WINDOW - sj
    ok = (rel >= 0) & (rel < WINDOW)
    slopes = jnp.exp2(-8.0 * jnp.arange(1, SWA_HEADS + 1, dtype=F32) / SWA_HEADS)
    bias = jnp.where(ok[None], -slopes[:, None, None] * rel.astype(F32)[None], NEG_INF)
    bias = bias.reshape(SWA_KV_HEADS, SWA_GROUP * WINDOW, 2 * WINDOW)
    avg = lambda heads: jnp.kron(jnp.eye(heads, dtype=F32), jnp.full((SWA_HD, SWA_HD), 1.0 / SWA_HD, F32)).astype(BF16)
    qg = jnp.tile(q_g.reshape(1, SWA_HD), (1, SWA_HEADS)) * (SWA_HD ** -0.5)
    kg = jnp.tile(k_g.reshape(1, SWA_HD), (1, SWA_KV_HEADS))
    return qg, kg, avg(SWA_HEADS), avg(SWA_KV_HEADS), bias


def _swa(z_sq, z_skv, sinks, q_g, k_g, batch, seq):
    q3 = z_sq.reshape(batch, seq, SWA_Q)
    kv3 = z_skv.reshape(batch, seq, Z_SKV_W)
    per = SWA_ROWS // WINDOW
    qg, kg, bq, bk, bias = _swa_tables(q_g, k_g)
    full = lambda a: pl.BlockSpec(a.shape, lambda b, i: (0,) * a.ndim)
    out = pl.pallas_call(
        _swa_kernel,
        grid=(batch, seq // SWA_ROWS),
        in_specs=[pl.BlockSpec(memory_space=pltpu.SMEM),
                  pl.BlockSpec((None, SWA_ROWS, SWA_Q), lambda b, i: (b, i, 0)),
                  pl.BlockSpec((None, WINDOW, Z_SKV_W), lambda b, i: (b, jnp.maximum(i * per - 1, 0), 0)),
                  pl.BlockSpec((None, SWA_ROWS, Z_SKV_W), lambda b, i: (b, i, 0)),
                  full(qg), full(kg), full(bq), full(bk), full(bias)],
        out_specs=pl.BlockSpec((None, SWA_ROWS, SWA_Q), lambda b, i: (b, i, 0)),
        out_shape=jax.ShapeDtypeStruct((batch, seq, SWA_Q), BF16),
        scratch_shapes=[pltpu.VMEM((SWA_ROWS + WINDOW, Z_SKV_W), BF16)],
        compiler_params=pltpu.CompilerParams(dimension_semantics=("parallel", "arbitrary"),
                                             vmem_limit_bytes=VMEM_LIMIT),
        name="swa",
    )(sinks, q3, kv3, kv3, qg, kg, bq, bk, bias)
    return out.reshape(batch * seq, SWA_Q)


def _merge_kernel(yr_ref, ys_ref, zg_ref, x_ref, wro_ref, wso_ref, wout_ref, fg_ref, wr_ref, br_ref,
                  x1_ref, hp_ref, route_ref, cnt_ref):
    subs = [_merge_rows(slice(s * MERGE_SUB, (s + 1) * MERGE_SUB), yr_ref, ys_ref, zg_ref, x_ref, wro_ref,
                        wso_ref, wout_ref, fg_ref, wr_ref, br_ref) for s in range(MERGE_TM // MERGE_SUB)]
    x1_ref[...] = jnp.concatenate([s[0] for s in subs], axis=0)
    hp_ref[...] = jnp.concatenate([s[1] for s in subs], axis=0)
    route_ref[...] = jnp.concatenate([s[2] for s in subs], axis=0)
    cnt_ref[...] = sum(s[3] for s in subs)


def _merge_rows(rows, yr_ref, ys_ref, zg_ref, x_ref, wro_ref, wso_ref, wout_ref, fg_ref, wr_ref, br_ref):
    a = _dot(yr_ref[rows, :], wro_ref[...])
    b = _dot(ys_ref[rows, :], wso_ref[...])
    gate_r = zg_ref[rows, :D_MODEL].astype(F32)
    gate_s = zg_ref[rows, D_MODEL:].astype(F32)
    merged = (_sigmoid(gate_r) * a + _sigmoid(gate_s) * b).astype(BF16)
    x1 = x_ref[rows, :] + _dot(merged, wout_ref[...])
    h2 = _rms(x1, fg_ref[...])
    packed = _pack_halves(h2)

    h_hi = h2.astype(BF16)
    h_lo = (h2 - h_hi.astype(F32)).astype(BF16)
    parts = _dot(h_hi, wr_ref[...]) + _dot(h_lo, wr_ref[...])
    logits = parts[:, :ROUTE_LANES] + parts[:, ROUTE_LANES:] + br_ref[...]
    lane = lax.broadcasted_iota(jnp.int32, logits.shape, 1)
    big = jnp.int32(ROUTE_LANES)
    is_group = jnp.logical_and(lane >= LANE_GROUP0, lane < LANE_GROUP0 + N_GROUPS)
    gl = jnp.where(is_group, logits, NEG_INF)
    gmax = jnp.max(gl, axis=-1, keepdims=True)
    g_w = 1.0 / jnp.sum(jnp.exp(gl - gmax), axis=-1, keepdims=True)
    g_sel = jnp.min(jnp.where(gl == gmax, lane, big), axis=-1, keepdims=True) - LANE_GROUP0
    in_group = jnp.logical_and(lane < N_EXPERTS, (lane >> 3) == g_sel)
    el = jnp.where(in_group, logits, NEG_INF)
    m1 = jnp.max(el, axis=-1, keepdims=True)
    i1 = jnp.min(jnp.where(el == m1, lane, big), axis=-1, keepdims=True)
    el2 = jnp.where(lane == i1, NEG_INF, el)
    m2 = jnp.max(el2, axis=-1, keepdims=True)
    i2 = jnp.min(jnp.where(el2 == m2, lane, big), axis=-1, keepdims=True)
    e2 = jnp.exp(m2 - m1)
    c1 = g_w / (1.0 + e2)
    c2 = g_w * e2 / (1.0 + e2)
    chosen = jnp.where(jnp.logical_or(lane == i1, lane == i2), 1.0, 0.0)
    route = (chosen
             + jnp.where(lane == LANE_E1, i1.astype(F32), 0.0)
             + jnp.where(lane == LANE_E2, i2.astype(F32), 0.0)
             + jnp.where(lane == LANE_C1, c1, 0.0)
             + jnp.where(lane == LANE_C2, c2, 0.0))
    return x1, packed, route, jnp.sum(chosen, axis=0, keepdims=True)


def _merge(y_r, y_s, z_gate, x2d, w_ret_o, w_swa_o, w_out, ffn_g, w_router, b_router):
    t = x2d.shape[0]
    row = lambda w: pl.BlockSpec((MERGE_TM, w), lambda i: (i, 0))
    full = lambda a: pl.BlockSpec(a.shape, lambda i: (0,) * a.ndim)
    return pl.pallas_call(
        _merge_kernel,
        grid=(t // MERGE_TM,),
        in_specs=[row(RET_W), row(SWA_Q), row(Z_GATE_W), row(D_MODEL),
                  full(w_ret_o), full(w_swa_o), full(w_out), full(ffn_g), full(w_router), full(b_router)],
        out_specs=[row(D_MODEL), row(HALF_D), row(ROUTE_LANES),
                   pl.BlockSpec((None, 1, ROUTE_LANES), lambda i: (i, 0, 0))],
        out_shape=[jax.ShapeDtypeStruct((t, D_MODEL), F32),
                   jax.ShapeDtypeStruct((t, HALF_D), I32),
                   jax.ShapeDtypeStruct((t, ROUTE_LANES), F32),
                   jax.ShapeDtypeStruct((t // MERGE_TM, 1, ROUTE_LANES), F32)],
        compiler_params=pltpu.CompilerParams(dimension_semantics=("parallel",),
                                             vmem_limit_bytes=VMEM_LIMIT),
        name="merge_router",
    )(y_r, y_s, z_gate, x2d, w_ret_o, w_swa_o, w_out, ffn_g, w_router, b_router)


def _route_pos_kernel(route_ref, base_ref, pos_ref):
    route = route_ref[...]
    lane = lax.broadcasted_iota(jnp.int32, route.shape, 1)
    lane_f = lane.astype(F32)
    chosen = jnp.where(lane < N_EXPERTS, route, 0.0)
    r = lax.broadcasted_iota(jnp.int32, (ROUTE_TM, ROUTE_TM), 0)
    c = lax.broadcasted_iota(jnp.int32, (ROUTE_TM, ROUTE_TM), 1)
    tri = jnp.where(c <= r, 1.0, 0.0).astype(BF16)
    incl = _dot(tri, chosen.astype(BF16))
    posmat = base_ref[...] + incl - 1.0
    pick = lambda ln: jnp.sum(jnp.where(lane == ln, route, 0.0), axis=-1, keepdims=True)
    pos_of = lambda e: jnp.sum(jnp.where(lane_f == e, posmat, 0.0), axis=-1, keepdims=True)
    p1 = pos_of(pick(LANE_E1))
    p2 = pos_of(pick(LANE_E2))
    rec = jnp.where(lane == 0, p1, 0.0) + jnp.where(lane == 1, p2, 0.0)
    pos_ref[...] = rec.T[:8, :].astype(I32)


def _route_pos(route, base):
    t = route.shape[0]
    n = t // ROUTE_TM
    return pl.pallas_call(
        _route_pos_kernel,
        grid=(n,),
        in_specs=[pl.BlockSpec((ROUTE_TM, ROUTE_LANES), lambda i: (i, 0)),
                  pl.BlockSpec((None, 1, ROUTE_LANES), lambda i: (i, 0, 0))],
        out_specs=pl.BlockSpec((None, 8, ROUTE_TM), lambda i: (i, 0, 0)),
        out_shape=jax.ShapeDtypeStruct((n, 8, ROUTE_TM), I32),
        compiler_params=pltpu.CompilerParams(dimension_semantics=("parallel",),
                                             vmem_limit_bytes=VMEM_LIMIT),
        name="route_pos",
    )(route, base)


def _plan_kernel(cnt_ref, base_ref, owner_ref, valid_ref, used_ref):
    counts = cnt_ref[...]
    nt = counts.shape[0]
    exact = functools.partial(jnp.dot, precision=lax.Precision.HIGHEST, preferred_element_type=F32)
    total = jnp.sum(counts, axis=0, keepdims=True)
    tiles_e = jnp.ceil(total * (1.0 / ROW_TILE))
    k = lax.broadcasted_iota(jnp.int32, (ROUTE_LANES, ROUTE_LANES), 0)
    lane = lax.broadcasted_iota(jnp.int32, (ROUTE_LANES, ROUTE_LANES), 1)
    tile_end = exact(jnp.broadcast_to(tiles_e, (8, ROUTE_LANES)), jnp.where(k <= lane, 1.0, 0.0))[0:1]
    tile_start = tile_end - tiles_e
    r = lax.broadcasted_iota(jnp.int32, (nt, nt), 0)
    c = lax.broadcasted_iota(jnp.int32, (nt, nt), 1)
    before = exact(jnp.where(c < r, 1.0, 0.0), counts)
    base_ref[...] = tile_start * ROW_TILE + before

    used = jnp.max(tile_end, axis=-1, keepdims=True)
    tile = k.astype(F32)
    clamped = jnp.minimum(tile, jnp.maximum(used - 1.0, 0.0))
    is_expert = lane < N_EXPERTS
    owner = jnp.sum(jnp.where(jnp.logical_and(is_expert, tile_end <= clamped), 1.0, 0.0), axis=-1, keepdims=True)
    owner = jnp.minimum(owner, N_EXPERTS - 1.0)
    mine = lane.astype(F32) == owner
    pick = lambda v: jnp.sum(jnp.where(mine, v, 0.0), axis=-1, keepdims=True)
    t1 = tile[:, 0:1]
    valid = jnp.clip(pick(total) - (t1 - pick(tile_start)) * ROW_TILE, 0.0, float(ROW_TILE))
    valid = jnp.where(t1 < used, valid, 0.0)
    owner_ref[...] = jnp.broadcast_to(owner, owner_ref.shape).astype(I32)
    valid_ref[...] = jnp.broadcast_to(valid, valid_ref.shape).astype(I32)
    used_ref[...] = jnp.broadcast_to(used, used_ref.shape).astype(I32)


def _tile_plan(cnt, n_tiles):
    assert n_tiles <= ROUTE_LANES
    nt = cnt.shape[0]
    sq = (ROUTE_LANES, ROUTE_LANES)
    base, owner, valid, used = pl.pallas_call(
        _plan_kernel,
        out_shape=[jax.ShapeDtypeStruct((nt, ROUTE_LANES), F32), jax.ShapeDtypeStruct(sq, I32),
                   jax.ShapeDtypeStruct(sq, I32), jax.ShapeDtypeStruct((8, ROUTE_LANES), I32)],
        name="tile_plan",
    )(cnt.reshape(nt, ROUTE_LANES))
    return base.reshape(cnt.shape), owner[:n_tiles, 0], valid[:n_tiles, 0], used[0, :1]


def _sc_mesh():
    return plsc.VectorSubcoreMesh(core_axis_name="c", subcore_axis_name="s")


def _sc_worker():
    return lax.axis_index("s") * SC_CORES + lax.axis_index("c")


def _dispatch(hp, pos1, pos2, n_rows):
    t = hp.shape[0]
    per_worker = t // SC_WORKERS
    k = per_worker // SC_CHUNK
    idx = pltpu.VMEM((k, SC_CHUNK), I32)

    @functools.partial(pl.kernel, mesh=_sc_mesh(), out_type=jax.ShapeDtypeStruct((n_rows, HALF_D), I32),
                       scratch_types=[idx, idx, pltpu.VMEM((SC_CHUNK, HALF_D), I32)], name="moe_dispatch")
    def run(h_hbm, p1_hbm, p2_hbm, xs_hbm, p1_v, p2_v, rows_v):
        wid = _sc_worker()
        pltpu.sync_copy(p1_hbm.at[wid], p1_v)
        pltpu.sync_copy(p2_hbm.at[wid], p2_v)
        for j in range(k):
            pltpu.sync_copy(h_hbm.at[pl.ds(wid * per_worker + j * SC_CHUNK, SC_CHUNK)], rows_v)
            pltpu.sync_copy(rows_v, xs_hbm.at[p1_v.at[j]])
            pltpu.sync_copy(rows_v, xs_hbm.at[p2_v.at[j]])

    return run(hp, pos1.reshape(SC_WORKERS, k, SC_CHUNK), pos2.reshape(SC_WORKERS, k, SC_CHUNK))


def _combine_gather(ys, pos1, pos2, t):
    per_worker = t // SC_WORKERS
    k = per_worker // SC_CHUNK
    idx = pltpu.VMEM((k, SC_CHUNK), I32)
    out = jax.ShapeDtypeStruct((t, HALF_D), I32)

    @functools.partial(pl.kernel, mesh=_sc_mesh(), out_type=(out, out),
                       scratch_types=[idx, idx, pltpu.VMEM((SC_CHUNK, HALF_D), I32)], name="moe_combine")
    def run(ys_hbm, p1_hbm, p2_hbm, g1_hbm, g2_hbm, p1_v, p2_v, rows_v):
        wid = _sc_worker()
        pltpu.sync_copy(p1_hbm.at[wid], p1_v)
        pltpu.sync_copy(p2_hbm.at[wid], p2_v)
        for j in range(k):
            dst = pl.ds(wid * per_worker + j * SC_CHUNK, SC_CHUNK)
            pltpu.sync_copy(ys_hbm.at[p1_v.at[j]], rows_v)
            pltpu.sync_copy(rows_v, g1_hbm.at[dst])
            pltpu.sync_copy(ys_hbm.at[p2_v.at[j]], rows_v)
            pltpu.sync_copy(rows_v, g2_hbm.at[dst])

    return run(ys, pos1.reshape(SC_WORKERS, k, SC_CHUNK), pos2.reshape(SC_WORKERS, k, SC_CHUNK))


def _experts_kernel(owner_ref, valid_ref, used_ref, xs_ref, wg_ref, wu_ref, wd_ref, ys_ref, wg_bf, wu_bf, wd_bf):
    i = pl.program_id(0)
    valid = valid_ref[i]

    @pl.when(jnp.logical_or(i == 0, owner_ref[i] != owner_ref[jnp.maximum(i - 1, 0)]))
    def _():
        wg_bf[...] = wg_ref[...].astype(BF16)
        wu_bf[...] = wu_ref[...].astype(BF16)
        wd_bf[...] = wd_ref[...].astype(BF16)

    @pl.when(i < used_ref[0])
    def _():
        row = lax.broadcasted_iota(jnp.int32, (ROW_TILE, HALF_D), 0)
        packed = jnp.where(row < valid, xs_ref[...], 0)
        hi, lo = _unpack_halves(packed)
        hi = hi.astype(BF16)
        lo = lo.astype(BF16)
        hg = _dot(hi, wg_bf[:HALF_D, :]) + _dot(lo, wg_bf[HALF_D:, :])
        hu = _dot(hi, wu_bf[:HALF_D, :]) + _dot(lo, wu_bf[HALF_D:, :])
        hid = (hg * _sigmoid(hg) * hu).astype(BF16)
        ys_ref[...] = _pack_halves(_dot(hid, wd_bf[...]))


def _experts(xs, owner, valid, used, wg, wu, wd):
    n_tiles = owner.shape[0]
    tile = lambda i, own, val, used: (jnp.minimum(i, used[0] - 1), 0)
    grid_spec = pltpu.PrefetchScalarGridSpec(
        num_scalar_prefetch=3,
        grid=(n_tiles,),
        in_specs=[pl.BlockSpec((ROW_TILE, HALF_D), tile),
                  pl.BlockSpec((None, D_MODEL, EXPERT_FF), lambda i, own, val, used: (own[i], 0, 0)),
                  pl.BlockSpec((None, D_MODEL, EXPERT_FF), lambda i, own, val, used: (own[i], 0, 0)),
                  pl.BlockSpec((None, EXPERT_FF, D_MODEL), lambda i, own, val, used: (own[i], 0, 0))],
        out_specs=pl.BlockSpec((ROW_TILE, HALF_D), tile),
        scratch_shapes=[pltpu.VMEM((D_MODEL, EXPERT_FF), BF16), pltpu.VMEM((D_MODEL, EXPERT_FF), BF16),
                        pltpu.VMEM((EXPERT_FF, D_MODEL), BF16)],
    )
    return pl.pallas_call(
        _experts_kernel,
        grid_spec=grid_spec,
        out_shape=jax.ShapeDtypeStruct((n_tiles * ROW_TILE, HALF_D), I32),
        compiler_params=pltpu.CompilerParams(dimension_semantics=("arbitrary",),
                                             vmem_limit_bytes=VMEM_LIMIT),
        name="experts",
    )(owner, valid, used, xs, wg, wu, wd)


def _ple_kernel(x1_ref, g1_ref, g2_ref, route_ref, p_ref, wp_ref, pg_ref, gg_ref, wgate_ref, o_ref):
    route = route_ref[...]
    lane = lax.broadcasted_iota(jnp.int32, route.shape, 1)
    pick = lambda ln: jnp.sum(jnp.where(lane == ln, route, 0.0), axis=-1, keepdims=True)
    c1 = pick(LANE_C1)
    c2 = pick(LANE_C2)
    a_hi, a_lo = _unpack_halves(g1_ref[...])
    b_hi, b_lo = _unpack_halves(g2_ref[...])
    x = x1_ref[...] + jnp.concatenate([c1 * a_hi + c2 * b_hi, c1 * a_lo + c2 * b_lo], axis=1)
    ple = _rms(_dot(p_ref[...].astype(BF16), wp_ref[...]), pg_ref[...])
    gate = _sigmoid(_dot(_rms(x, gg_ref[...]).astype(BF16), wgate_ref[...]))
    o_ref[...] = x + gate * ple


def _ple(x1, g1, g2, route, p2d, w_ple, ple_g, gate_g, w_gate):
    t = x1.shape[0]
    row = lambda w: pl.BlockSpec((PLE_TM, w), lambda i: (i, 0))
    full = lambda a: pl.BlockSpec(a.shape, lambda i: (0,) * a.ndim)
    return pl.pallas_call(
        _ple_kernel,
        grid=(t // PLE_TM,),
        in_specs=[row(D_MODEL), row(HALF_D), row(HALF_D), row(ROUTE_LANES), row(PLE_DIM),
                  full(w_ple), full(ple_g), full(gate_g), full(w_gate)],
        out_specs=row(D_MODEL),
        out_shape=jax.ShapeDtypeStruct((t, D_MODEL), F32),
        compiler_params=pltpu.CompilerParams(dimension_semantics=("parallel",),
                                             vmem_limit_bytes=VMEM_LIMIT),
        name="ple",
    )(x1, g1, g2, route, p2d, w_ple, ple_g, gate_g, w_gate)


def kernel(x, p, mix_norm_g, w_in, ret_gn_g, ret_gn_b, w_ret_o, q_norm_g, k_norm_g, attn_sinks,
           w_swa_o, w_out, ffn_norm_g, w_router_group, b_router_group, w_router_expert,
           b_router_expert, w_exp_gate, w_exp_up, w_exp_down, ple_gate_norm_g, w_ple_gate,
           w_ple, ple_norm_g):
    batch, seq, d = x.shape
    t = batch * seq
    depth = w_in.shape[0]
    n_tiles = 2 * t // ROW_TILE + N_EXPERTS
    row = lambda a: a.reshape(1, -1)
    x2d = x.reshape(t, d)
    for i in range(depth):
        z_ret, z_sq, z_skv, z_gate = _in_proj(x2d, row(mix_norm_g[i]), w_in[i].astype(BF16))
        y_r = _retention(z_ret, row(ret_gn_g[i]), row(ret_gn_b[i]), batch, seq)
        y_s = _swa(z_sq, z_skv, attn_sinks[i], row(q_norm_g[i]), row(k_norm_g[i]), batch, seq)
        pad = ROUTE_LANES - N_GROUPS - N_EXPERTS
        w_router = jnp.pad(jnp.concatenate([w_router_expert[i], w_router_group[i]], axis=1), ((0, 0), (0, pad)))
        b_router = jnp.pad(jnp.concatenate([b_router_expert[i], b_router_group[i]]), (0, pad)).reshape(1, -1)
        w_router_hi = w_router.astype(BF16)
        w_router = jnp.concatenate([w_router_hi, (w_router - w_router_hi.astype(F32)).astype(BF16)], axis=1)
        x1, hp, route, cnt = _merge(y_r, y_s, z_gate, x2d, w_ret_o[i].astype(BF16), w_swa_o[i].astype(BF16),
                                    w_out[i].astype(BF16), row(ffn_norm_g[i]), w_router, b_router)
        base, owner, valid, used = _tile_plan(cnt, n_tiles)
        pos = _route_pos(route, base)
        pos1 = pos[:, 0, :]
        pos2 = pos[:, 1, :]
        xs = _dispatch(hp, pos1, pos2, n_tiles * ROW_TILE)
        ys = _experts(xs, owner, valid, used, w_exp_gate[i], w_exp_up[i], w_exp_down[i])
        g1, g2 = _combine_gather(ys, pos1, pos2, t)
        x2d = _ple(x1, g1, g2, route, p[i].reshape(t, PLE_DIM), w_ple[i].astype(BF16), row(ple_norm_g[i]),
                   row(ple_gate_norm_g[i]), w_ple_gate[i].astype(BF16))
    return x2d.reshape(batch, seq, d)
```

```python
import functools

import jax
import jax.numpy as jnp
from jax import lax
from jax.experimental import pallas as pl
from jax.experimental.pallas import tpu as pltpu
from jax.experimental.pallas import tpu_sc as plsc

F32 = jnp.float32
BF16 = jnp.bfloat16
I32 = jnp.int32

EPS = 1e-6
D_MODEL = 1024
PLE_DIM = 256
RET_HEADS = 4
RET_DK = 128
RET_DV = 128
RET_CHUNK = 128
SWA_HEADS = 8
SWA_KV_HEADS = 2
SWA_GROUP = SWA_HEADS // SWA_KV_HEADS
SWA_HD = 64
WINDOW = 128
N_GROUPS = 4
EXPERTS_PER_GROUP = 8
N_EXPERTS = N_GROUPS * EXPERTS_PER_GROUP
EXPERT_FF = 256

RET_W = RET_HEADS * RET_DK
SWA_Q = SWA_HEADS * SWA_HD
SWA_KV = SWA_KV_HEADS * SWA_HD
Z_RET_W = 4 * RET_W
Z_SKV_W = 2 * SWA_KV
Z_GATE_W = 2 * D_MODEL
IN_WIDTH = Z_RET_W + SWA_Q + Z_SKV_W + Z_GATE_W

ROUTE_LANES = 128
LANE_GROUP0 = N_EXPERTS
LANE_E1, LANE_E2, LANE_C1, LANE_C2 = 32, 33, 34, 35
NEG_INF = -1e30

HALF_D = D_MODEL // 2
ROW_TILE = 512
SC_CORES = 2
SC_SUBCORES = 16
SC_WORKERS = SC_CORES * SC_SUBCORES
SC_CHUNK = 128

VMEM_LIMIT = 56 * 1024 * 1024

IN_TM = 512
MIX_ROWS = 512
MIX_SQ0 = Z_RET_W
MIX_SKV0 = Z_RET_W + SWA_Q
MIX_ZW = Z_RET_W + SWA_Q + Z_SKV_W
RET_ROWS = 512
SWA_ROWS = 512
MERGE_TM = 512
MERGE_SUB = 256
ROUTE_TM = 512
PLE_TM = 512


def _rms(x, g):
    ms = jnp.mean(x * x, axis=-1, keepdims=True)
    return x * lax.rsqrt(ms + EPS) * g


def _sigmoid(x):
    return 1.0 / (1.0 + jnp.exp(-x))


def _dot(a, b):
    return jnp.dot(a, b, preferred_element_type=F32)


def _dot_nt(a, b):
    return lax.dot_general(a, b, (((1,), (1,)), ((), ())), preferred_element_type=F32)


def _dot_tn(a, b):
    return lax.dot_general(a, b, (((0,), (0,)), ((), ())), preferred_element_type=F32)


def _pack_halves(v):
    return pltpu.pack_elementwise([v[:, :HALF_D], v[:, HALF_D:]], packed_dtype=BF16)


def _unpack_halves(p):
    return tuple(pltpu.unpack_elementwise(p, index=k, packed_dtype=BF16, unpacked_dtype=F32) for k in range(2))


def _in_proj_kernel(x_ref, g_ref, w_ref, zret_ref, zsq_ref, zskv_ref, zgate_ref):
    h = _rms(x_ref[...], g_ref[...]).astype(BF16)

    def proj(c0, width):
        return _dot(h, w_ref[:, c0:c0 + width]).astype(BF16)

    cw = 512
    for j in range(Z_RET_W // cw):
        zret_ref[:, j * cw:(j + 1) * cw] = proj(j * cw, cw)
    zsq_ref[...] = proj(Z_RET_W, SWA_Q)
    zskv_ref[...] = proj(Z_RET_W + SWA_Q, Z_SKV_W)
    g0 = Z_RET_W + SWA_Q + Z_SKV_W
    for j in range(Z_GATE_W // cw):
        zgate_ref[:, j * cw:(j + 1) * cw] = proj(g0 + j * cw, cw)


def _in_proj(x2d, g, w_bf):
    t = x2d.shape[0]
    row = lambda w: pl.BlockSpec((IN_TM, w), lambda i: (i, 0))
    full = lambda a: pl.BlockSpec(a.shape, lambda i: (0,) * a.ndim)
    return pl.pallas_call(
        _in_proj_kernel,
        grid=(t // IN_TM,),
        in_specs=[row(D_MODEL), full(g), full(w_bf)],
        out_specs=[row(Z_RET_W), row(SWA_Q), row(Z_SKV_W), row(Z_GATE_W)],
        out_shape=[jax.ShapeDtypeStruct((t, w), BF16) for w in (Z_RET_W, SWA_Q, Z_SKV_W, Z_GATE_W)],
        compiler_params=pltpu.CompilerParams(dimension_semantics=("parallel",),
                                             vmem_limit_bytes=VMEM_LIMIT),
        name="in_proj",
    )(x2d, g, w_bf)


def _retention_kernel(z_ref, dmask_ref, zeta_ref, xi_ref, cdec_ref, gng_ref, gnb_ref, o_ref, state_ref, prev_ref):
    @pl.when(pl.program_id(1) == 0)
    def _():
        state_ref[...] = jnp.zeros_like(state_ref)

    n_chunks = RET_ROWS // RET_CHUNK
    part = lambda ref, c, p, h: ref[c * RET_CHUNK:(c + 1) * RET_CHUNK,
                                    p * RET_W + h * RET_DK:p * RET_W + (h + 1) * RET_DK]

    for h in range(RET_HEADS):
        state = state_ref[h]
        for c in range(n_chunks):
            prev_ref[c, h] = state.astype(BF16)
            kz = (part(z_ref, c, 1, h).astype(F32) * zeta_ref[h]).astype(BF16)
            state = cdec_ref[h] * state + _dot_tn(kz, part(z_ref, c, 2, h))
        state_ref[h] = state

    out_rows = []
    for c in range(n_chunks):
        heads = []
        for h in range(RET_HEADS):
            q = part(z_ref, c, 0, h)
            scores = _dot_nt(q, part(z_ref, c, 1, h)) * dmask_ref[h]
            inner = _dot(scores.astype(BF16), part(z_ref, c, 2, h))
            cross = _dot(q, prev_ref[c, h]) * xi_ref[h]
            y = inner + cross
            mu = jnp.mean(y, axis=-1, keepdims=True)
            d = y - mu
            var = jnp.mean(d * d, axis=-1, keepdims=True)
            hs = slice(h * RET_DV, (h + 1) * RET_DV)
            yn = d * lax.rsqrt(var + EPS) * gng_ref[:, hs] + gnb_ref[:, hs]
            gate = part(z_ref, c, 3, h).astype(F32)
            heads.append((gate * _sigmoid(gate) * yn).astype(BF16))
        out_rows.append(jnp.concatenate(heads, axis=1))
    o_ref[...] = jnp.concatenate(out_rows, axis=0)


def _retention_tables():
    h = RET_HEADS
    c = RET_CHUNK
    log_gamma = jnp.log1p(-jnp.exp2(-5.0 - jnp.arange(h, dtype=F32)))
    pos = jnp.arange(c, dtype=F32)
    diff = pos[:, None] - pos[None, :]
    decay = jnp.where(diff[None] >= 0.0,
                      jnp.exp(jnp.maximum(diff, 0.0)[None] * log_gamma[:, None, None]), 0.0)
    scale = RET_DK ** -0.5
    dmask = decay * scale
    zeta = jnp.exp((c - 1.0 - pos)[None, :] * log_gamma[:, None]) * scale
    xi = jnp.exp((pos + 1.0)[None, :] * log_gamma[:, None])
    cdec = jnp.exp(c * log_gamma)
    bc = lambda a: jnp.broadcast_to(a[:, :, None], (h, c, c))
    return dmask, bc(zeta), bc(xi), jnp.broadcast_to(cdec[:, None, None], (h, c, c))


def _retention(z_ret, gn_g, gn_b, batch, seq):
    z3 = z_ret.reshape(batch, seq, Z_RET_W)
    tables = _retention_tables()
    tab_spec = pl.BlockSpec((RET_HEADS, RET_CHUNK, RET_CHUNK), lambda b, i: (0, 0, 0))
    vec_spec = pl.BlockSpec((1, RET_W), lambda b, i: (0, 0))
    out = pl.pallas_call(
        _retention_kernel,
        grid=(batch, seq // RET_ROWS),
        in_specs=[pl.BlockSpec((None, RET_ROWS, Z_RET_W), lambda b, i: (b, i, 0)),
                  tab_spec, tab_spec, tab_spec, tab_spec, vec_spec, vec_spec],
        out_specs=pl.BlockSpec((None, RET_ROWS, RET_W), lambda b, i: (b, i, 0)),
        out_shape=jax.ShapeDtypeStruct((batch, seq, RET_W), BF16),
        scratch_shapes=[pltpu.VMEM((RET_HEADS, RET_DK, RET_DV), F32),
                        pltpu.VMEM((RET_ROWS // RET_CHUNK, RET_HEADS, RET_DK, RET_DV), BF16)],
        compiler_params=pltpu.CompilerParams(dimension_semantics=("parallel", "arbitrary"),
                                             vmem_limit_bytes=VMEM_LIMIT),
        name="retention",
    )(z3, *tables, gn_g, gn_b)
    return out.reshape(batch * seq, RET_W)


def _swa_kernel(sinks_ref, q_ref, kvp_ref, kvc_ref, qg_ref, kg_ref, bq_ref, bk_ref, bias_ref, o_ref, band_ref):
    first = pl.program_id(1) == 0
    band_ref[:WINDOW, :] = kvp_ref[...]
    band_ref[WINDOW:, :] = kvc_ref[...]
    group_rows = SWA_GROUP * WINDOW
    lo_q = lax.broadcasted_iota(jnp.int32, (WINDOW, 2 * SWA_HD), 1) < SWA_HD
    lo_k = lax.broadcasted_iota(jnp.int32, (2 * WINDOW, 2 * SWA_HD), 1) < SWA_HD
    in_prev = lax.broadcasted_iota(jnp.int32, (group_rows, 2 * WINDOW), 1) < WINDOW
    row_head = lax.broadcasted_iota(jnp.int32, (group_rows, 1), 0) // WINDOW

    def both_heads(x):
        swapped = pltpu.roll(x, SWA_HD, axis=1)
        return jnp.where(lo_k, x, swapped), jnp.where(lo_k, swapped, x)

    def block(n, carry):
        r0 = pl.multiple_of(n * WINDOW, WINDOW)
        rows = pl.ds(r0, WINDOW)
        kv = band_ref[pl.ds(r0, 2 * WINDOW), :]
        kf = kv[:, :SWA_KV].astype(F32)
        kn = kf * lax.rsqrt(_dot((kf * kf).astype(BF16), bk_ref[...]) + EPS) * kg_ref[...]
        keys = [a.astype(BF16) for a in both_heads(kn)]
        vals = [a.astype(BF16) for a in both_heads(kv[:, SWA_KV:].astype(F32))]
        qf = q_ref[rows, :].astype(F32)
        qn = qf * lax.rsqrt(_dot((qf * qf).astype(BF16), bq_ref[...]) + EPS) * qg_ref[...]
        no_prev = jnp.logical_and(first, n == 0)
        outs = []
        for kh in range(SWA_KV_HEADS):
            parts = []
            for j in range(SWA_GROUP // 2):
                c0 = (kh * SWA_GROUP + 2 * j) * SWA_HD
                pair = qn[:, c0:c0 + 2 * SWA_HD]
                parts += [jnp.where(lo_q, pair, 0.0), jnp.where(lo_q, 0.0, pair)]
            qs = jnp.concatenate(parts, axis=0).astype(BF16)
            s = _dot_nt(qs, keys[kh]) + bias_ref[kh]
            s = jnp.where(jnp.logical_and(no_prev, in_prev), NEG_INF, s)
            sink = sinks_ref[kh * SWA_GROUP]
            for g in range(1, SWA_GROUP):
                sink = jnp.where(row_head == g, sinks_ref[kh * SWA_GROUP + g], sink)
            m = jnp.maximum(jnp.max(s, axis=-1, keepdims=True), sink)
            p = jnp.exp(s - m)
            denom = jnp.sum(p, axis=-1, keepdims=True) + jnp.exp(sink - m)
            o = _dot(p.astype(BF16), vals[kh]) * (1.0 / denom)
            for j in range(SWA_GROUP // 2):
                even = o[(2 * j) * WINDOW:(2 * j + 1) * WINDOW]
                odd = o[(2 * j + 1) * WINDOW:(2 * j + 2) * WINDOW]
                outs.append(jnp.where(lo_q, even, odd).astype(BF16))
        o_ref[rows, :] = jnp.concatenate(outs, axis=1)
        return carry

    lax.fori_loop(0, SWA_ROWS // WINDOW, block, 0)


def _swa_tables(q_g, k_g):
    qi = jnp.arange(WINDOW)[:, None]
    sj = jnp.arange(2 * WINDOW)[None, :]
    rel = qi + WINDOW - sj
    ok = (rel >= 0) & (rel < WINDOW)
    slopes = jnp.exp2(-8.0 * jnp.arange(1, SWA_HEADS + 1, dtype=F32) / SWA_HEADS)
    bias = jnp.where(ok[None], -slopes[:, None, None] * rel.astype(F32)[None], NEG_INF)
    bias = bias.reshape(SWA_KV_HEADS, SWA_GROUP * WINDOW, 2 * WINDOW)
    avg = lambda heads: jnp.kron(jnp.eye(heads, dtype=F32), jnp.full((SWA_HD, SWA_HD), 1.0 / SWA_HD, F32)).astype(BF16)
    qg = jnp.tile(q_g.reshape(1, SWA_HD), (1, SWA_HEADS)) * (SWA_HD ** -0.5)
    kg = jnp.tile(k_g.reshape(1, SWA_HD), (1, SWA_KV_HEADS))
    return qg, kg, avg(SWA_HEADS), avg(SWA_KV_HEADS), bias


def _swa(z_sq, z_skv, sinks, q_g, k_g, batch, seq):
    q3 = z_sq.reshape(batch, seq, SWA_Q)
    kv3 = z_skv.reshape(batch, seq, Z_SKV_W)
    per = SWA_ROWS // WINDOW
    qg, kg, bq, bk, bias = _swa_tables(q_g, k_g)
    full = lambda a: pl.BlockSpec(a.shape, lambda b, i: (0,) * a.ndim)
    out = pl.pallas_call(
        _swa_kernel,
        grid=(batch, seq // SWA_ROWS),
        in_specs=[pl.BlockSpec(memory_space=pltpu.SMEM),
                  pl.BlockSpec((None, SWA_ROWS, SWA_Q), lambda b, i: (b, i, 0)),
                  pl.BlockSpec((None, WINDOW, Z_SKV_W), lambda b, i: (b, jnp.maximum(i * per - 1, 0), 0)),
                  pl.BlockSpec((None, SWA_ROWS, Z_SKV_W), lambda b, i: (b, i, 0)),
                  full(qg), full(kg), full(bq), full(bk), full(bias)],
        out_specs=pl.BlockSpec((None, SWA_ROWS, SWA_Q), lambda b, i: (b, i, 0)),
        out_shape=jax.ShapeDtypeStruct((batch, seq, SWA_Q), BF16),
        scratch_shapes=[pltpu.VMEM((SWA_ROWS + WINDOW, Z_SKV_W), BF16)],
        compiler_params=pltpu.CompilerParams(dimension_semantics=("parallel", "arbitrary"),
                                             vmem_limit_bytes=VMEM_LIMIT),
        name="swa",
    )(sinks, q3, kv3, kv3, qg, kg, bq, bk, bias)
    return out.reshape(batch * seq, SWA_Q)


def _mixer_kernel(per_seq, sinks_ref, x_ref, g_ref, w_hbm, dmask_ref, zeta_ref, xi_ref, cdec_ref, gng_ref, gnb_ref,
                  qg_ref, kg_ref, bq_ref, bk_ref, bias_ref,
                  zgate_ref, yr_ref, ys_ref,
                  w_ref, w_sem, zc_ref, zn_ref, state_ref, prev_ref, band_ref, tail_ref):
    g = pl.program_id(0)
    pair = g - 1
    seq_start = lax.rem(pair, per_seq) == 0

    @pl.when(g == 0)
    def _():
        w_copy = pltpu.make_async_copy(w_hbm, w_ref, w_sem)
        w_copy.start()
        w_copy.wait()
        zc_ref[...] = jnp.zeros_like(zc_ref)
        tail_ref[...] = jnp.zeros_like(tail_ref)
        state_ref[...] = jnp.zeros_like(state_ref)

    @pl.when(seq_start)
    def _():
        state_ref[...] = jnp.zeros_like(state_ref)

    band_ref[:WINDOW, :] = tail_ref[...]
    band_ref[WINDOW:, :] = zc_ref[:, MIX_SKV0:MIX_ZW]

    h = _rms(x_ref[...], g_ref[...]).astype(BF16)

    def proj(c0, width):
        return _dot(h, w_ref[:, c0:c0 + width]).astype(BF16)

    cw = 512

    def keep_cols(c0, width):
        zn_ref[:, c0:c0 + width] = proj(c0, width)

    def gate_cols(j):
        zgate_ref[:, j * cw:(j + 1) * cw] = proj(MIX_ZW + j * cw, cw)

    proj_pieces = [functools.partial(keep_cols, j * cw, cw) for j in range(MIX_SKV0 // cw)]
    proj_pieces.append(functools.partial(keep_cols, MIX_SKV0, Z_SKV_W))
    proj_pieces += [functools.partial(gate_cols, j) for j in range(Z_GATE_W // cw)]

    def retention_rows(t):
        r_base = t * RET_ROWS
        n_chunks = RET_ROWS // RET_CHUNK
        part = lambda c, p, hd: zc_ref[r_base + c * RET_CHUNK:r_base + (c + 1) * RET_CHUNK,
                                       p * RET_W + hd * RET_DK:p * RET_W + (hd + 1) * RET_DK]
        for hd in range(RET_HEADS):
            state = state_ref[hd]
            for c in range(n_chunks):
                prev_ref[t * n_chunks + c, hd] = state.astype(BF16)
                kz = (part(c, 1, hd).astype(F32) * zeta_ref[hd]).astype(BF16)
                state = cdec_ref[hd] * state + _dot_tn(kz, part(c, 2, hd))
            state_ref[hd] = state
        out_rows = []
        for c in range(n_chunks):
            heads = []
            for hd in range(RET_HEADS):
                q = part(c, 0, hd)
                scores = _dot_nt(q, part(c, 1, hd)) * dmask_ref[hd]
                y = _dot(scores.astype(BF16), part(c, 2, hd)) + _dot(q, prev_ref[t * n_chunks + c, hd]) * xi_ref[hd]
                mu = jnp.mean(y, axis=-1, keepdims=True)
                d = y - mu
                var = jnp.mean(d * d, axis=-1, keepdims=True)
                hs = slice(hd * RET_DV, (hd + 1) * RET_DV)
                yn = d * lax.rsqrt(var + EPS) * gng_ref[:, hs] + gnb_ref[:, hs]
                gate = part(c, 3, hd).astype(F32)
                heads.append((gate * _sigmoid(gate) * yn).astype(BF16))
            out_rows.append(jnp.concatenate(heads, axis=1))
        yr_ref[r_base:r_base + RET_ROWS, :] = jnp.concatenate(out_rows, axis=0)

    group_rows = SWA_GROUP * WINDOW
    lo_q = lax.broadcasted_iota(jnp.int32, (WINDOW, 2 * SWA_HD), 1) < SWA_HD
    lo_k = lax.broadcasted_iota(jnp.int32, (2 * WINDOW, 2 * SWA_HD), 1) < SWA_HD
    in_prev = lax.broadcasted_iota(jnp.int32, (group_rows, 2 * WINDOW), 1) < WINDOW
    row_head = lax.broadcasted_iota(jnp.int32, (group_rows, 1), 0) // WINDOW

    def both_heads(v):
        swapped = pltpu.roll(v, SWA_HD, axis=1)
        return jnp.where(lo_k, v, swapped), jnp.where(lo_k, swapped, v)

    def swa_block(n):
        rows = slice(n * WINDOW, (n + 1) * WINDOW)
        kv = band_ref[n * WINDOW:(n + 2) * WINDOW, :]
        kf = kv[:, :SWA_KV].astype(F32)
        kn = kf * lax.rsqrt(_dot((kf * kf).astype(BF16), bk_ref[...]) + EPS) * kg_ref[...]
        keys = [a.astype(BF16) for a in both_heads(kn)]
        vals = [a.astype(BF16) for a in both_heads(kv[:, SWA_KV:].astype(F32))]
        qf = zc_ref[rows, MIX_SQ0:MIX_SKV0].astype(F32)
        qn = qf * lax.rsqrt(_dot((qf * qf).astype(BF16), bq_ref[...]) + EPS) * qg_ref[...]
        outs = []
        for kh in range(SWA_KV_HEADS):
            parts = []
            for j in range(SWA_GROUP // 2):
                c0 = (kh * SWA_GROUP + 2 * j) * SWA_HD
                two = qn[:, c0:c0 + 2 * SWA_HD]
                parts += [jnp.where(lo_q, two, 0.0), jnp.where(lo_q, 0.0, two)]
            qs = jnp.concatenate(parts, axis=0).astype(BF16)
            s = _dot_nt(qs, keys[kh]) + bias_ref[kh]
            if n == 0:
                s = jnp.where(jnp.logical_and(seq_start, in_prev), NEG_INF, s)
            sink = sinks_ref[kh * SWA_GROUP]
            for gi in range(1, SWA_GROUP):
                sink = jnp.where(row_head == gi, sinks_ref[kh * SWA_GROUP + gi], sink)
            m = jnp.maximum(jnp.max(s, axis=-1, keepdims=True), sink)
            p = jnp.exp(s - m)
            denom = jnp.sum(p, axis=-1, keepdims=True) + jnp.exp(sink - m)
            o = _dot(p.astype(BF16), vals[kh]) * (1.0 / denom)
            for j in range(SWA_GROUP // 2):
                even = o[(2 * j) * WINDOW:(2 * j + 1) * WINDOW]
                odd = o[(2 * j + 1) * WINDOW:(2 * j + 2) * WINDOW]
                outs.append(jnp.where(lo_q, even, odd).astype(BF16))
        ys_ref[rows, :] = jnp.concatenate(outs, axis=1)

    mix_pieces = [functools.partial(retention_rows, t) for t in range(MIX_ROWS // RET_ROWS)]
    mix_pieces += [functools.partial(swa_block, n) for n in range(MIX_ROWS // WINDOW)]

    for k in range(max(len(proj_pieces), len(mix_pieces))):
        if k < len(proj_pieces):
            proj_pieces[k]()
        if k < len(mix_pieces):
            mix_pieces[k]()

    tail_ref[...] = zc_ref[MIX_ROWS - WINDOW:, MIX_SKV0:MIX_ZW]
    zc_ref[...] = zn_ref[...]


def _mixer(x2d, mix_g, w_bf, gn_g, gn_b, sinks, q_g, k_g, seq):
    t = x2d.shape[0]
    n = t // MIX_ROWS
    assert seq % MIX_ROWS == 0
    tables = _retention_tables()
    qg, kg, bq, bk, bias = _swa_tables(q_g, k_g)
    full = lambda a: pl.BlockSpec(a.shape, lambda g: (0,) * a.ndim)
    cur = lambda w: pl.BlockSpec((MIX_ROWS, w), lambda g: (jnp.minimum(g, n - 1), 0))
    prv = lambda w: pl.BlockSpec((MIX_ROWS, w), lambda g: (jnp.maximum(g - 1, 0), 0))
    chunks = MIX_ROWS // RET_CHUNK
    return pl.pallas_call(
        functools.partial(_mixer_kernel, seq // MIX_ROWS),
        grid=(n + 1,),
        in_specs=[pl.BlockSpec(memory_space=pltpu.SMEM), cur(D_MODEL), full(mix_g),
                  pl.BlockSpec(memory_space=pl.ANY),
                  full(tables[0]), full(tables[1]), full(tables[2]), full(tables[3]), full(gn_g), full(gn_b),
                  full(qg), full(kg), full(bq), full(bk), full(bias)],
        out_specs=[cur(Z_GATE_W), prv(RET_W), prv(SWA_Q)],
        out_shape=[jax.ShapeDtypeStruct((t, Z_GATE_W), BF16), jax.ShapeDtypeStruct((t, RET_W), BF16),
                   jax.ShapeDtypeStruct((t, SWA_Q), BF16)],
        scratch_shapes=[pltpu.VMEM(w_bf.shape, BF16), pltpu.SemaphoreType.DMA,
                        pltpu.VMEM((MIX_ROWS, MIX_ZW), BF16), pltpu.VMEM((MIX_ROWS, MIX_ZW), BF16),
                        pltpu.VMEM((RET_HEADS, RET_DK, RET_DV), F32),
                        pltpu.VMEM((chunks, RET_HEADS, RET_DK, RET_DV), BF16),
                        pltpu.VMEM((MIX_ROWS + WINDOW, Z_SKV_W), BF16), pltpu.VMEM((WINDOW, Z_SKV_W), BF16)],
        compiler_params=pltpu.CompilerParams(dimension_semantics=("arbitrary",),
                                             vmem_limit_bytes=VMEM_LIMIT),
        name="mixer",
    )(sinks, x2d, mix_g, w_bf, *tables, gn_g, gn_b, qg, kg, bq, bk, bias)


def _merge_kernel(yr_ref, ys_ref, zg_ref, x_ref, wro_ref, wso_ref, wout_ref, fg_ref, wr_ref, br_ref,
                  x1_ref, hp_ref, route_ref, cnt_ref):
    subs = [_merge_rows(slice(s * MERGE_SUB, (s + 1) * MERGE_SUB), yr_ref, ys_ref, zg_ref, x_ref, wro_ref,
                        wso_ref, wout_ref, fg_ref, wr_ref, br_ref) for s in range(MERGE_TM // MERGE_SUB)]
    x1_ref[...] = jnp.concatenate([s[0] for s in subs], axis=0)
    hp_ref[...] = jnp.concatenate([s[1] for s in subs], axis=0)
    route_ref[...] = jnp.concatenate([s[2] for s in subs], axis=0)
    cnt_ref[...] = sum(s[3] for s in subs)


def _merge_rows(rows, yr_ref, ys_ref, zg_ref, x_ref, wro_ref, wso_ref, wout_ref, fg_ref, wr_ref, br_ref):
    a = _dot(yr_ref[rows, :], wro_ref[...])
    b = _dot(ys_ref[rows, :], wso_ref[...])
    gate_r = zg_ref[rows, :D_MODEL].astype(F32)
    gate_s = zg_ref[rows, D_MODEL:].astype(F32)
    merged = (_sigmoid(gate_r) * a + _sigmoid(gate_s) * b).astype(BF16)
    x1 = x_ref[rows, :] + _dot(merged, wout_ref[...])
    h2 = _rms(x1, fg_ref[...])
    packed = _pack_halves(h2)

    h_hi = h2.astype(BF16)
    h_lo = (h2 - h_hi.astype(F32)).astype(BF16)
    parts = _dot(h_hi, wr_ref[...]) + _dot(h_lo, wr_ref[...])
    logits = parts[:, :ROUTE_LANES] + parts[:, ROUTE_LANES:] + br_ref[...]
    lane = lax.broadcasted_iota(jnp.int32, logits.shape, 1)
    big = jnp.int32(ROUTE_LANES)
    is_group = jnp.logical_and(lane >= LANE_GROUP0, lane < LANE_GROUP0 + N_GROUPS)
    gl = jnp.where(is_group, logits, NEG_INF)
    gmax = jnp.max(gl, axis=-1, keepdims=True)
    g_w = 1.0 / jnp.sum(jnp.exp(gl - gmax), axis=-1, keepdims=True)
    g_sel = jnp.min(jnp.where(gl == gmax, lane, big), axis=-1, keepdims=True) - LANE_GROUP0
    in_group = jnp.logical_and(lane < N_EXPERTS, (lane >> 3) == g_sel)
    el = jnp.where(in_group, logits, NEG_INF)
    m1 = jnp.max(el, axis=-1, keepdims=True)
    i1 = jnp.min(jnp.where(el == m1, lane, big), axis=-1, keepdims=True)
    el2 = jnp.where(lane == i1, NEG_INF, el)
    m2 = jnp.max(el2, axis=-1, keepdims=True)
    i2 = jnp.min(jnp.where(el2 == m2, lane, big), axis=-1, keepdims=True)
    e2 = jnp.exp(m2 - m1)
    c1 = g_w / (1.0 + e2)
    c2 = g_w * e2 / (1.0 + e2)
    chosen = jnp.where(jnp.logical_or(lane == i1, lane == i2), 1.0, 0.0)
    route = (chosen
             + jnp.where(lane == LANE_E1, i1.astype(F32), 0.0)
             + jnp.where(lane == LANE_E2, i2.astype(F32), 0.0)
             + jnp.where(lane == LANE_C1, c1, 0.0)
             + jnp.where(lane == LANE_C2, c2, 0.0))
    return x1, packed, route, jnp.sum(chosen, axis=0, keepdims=True)


def _merge(y_r, y_s, z_gate, x2d, w_ret_o, w_swa_o, w_out, ffn_g, w_router, b_router):
    t = x2d.shape[0]
    row = lambda w: pl.BlockSpec((MERGE_TM, w), lambda i: (i, 0))
    full = lambda a: pl.BlockSpec(a.shape, lambda i: (0,) * a.ndim)
    return pl.pallas_call(
        _merge_kernel,
        grid=(t // MERGE_TM,),
        in_specs=[row(RET_W), row(SWA_Q), row(Z_GATE_W), row(D_MODEL),
                  full(w_ret_o), full(w_swa_o), full(w_out), full(ffn_g), full(w_router), full(b_router)],
        out_specs=[row(D_MODEL), row(HALF_D), row(ROUTE_LANES),
                   pl.BlockSpec((None, 1, ROUTE_LANES), lambda i: (i, 0, 0))],
        out_shape=[jax.ShapeDtypeStruct((t, D_MODEL), F32),
                   jax.ShapeDtypeStruct((t, HALF_D), I32),
                   jax.ShapeDtypeStruct((t, ROUTE_LANES), F32),
                   jax.ShapeDtypeStruct((t // MERGE_TM, 1, ROUTE_LANES), F32)],
        compiler_params=pltpu.CompilerParams(dimension_semantics=("parallel",),
                                             vmem_limit_bytes=VMEM_LIMIT),
        name="merge_router",
    )(y_r, y_s, z_gate, x2d, w_ret_o, w_swa_o, w_out, ffn_g, w_router, b_router)


def _route_pos_kernel(route_ref, base_ref, pos_ref):
    route = route_ref[...]
    lane = lax.broadcasted_iota(jnp.int32, route.shape, 1)
    lane_f = lane.astype(F32)
    chosen = jnp.where(lane < N_EXPERTS, route, 0.0)
    r = lax.broadcasted_iota(jnp.int32, (ROUTE_TM, ROUTE_TM), 0)
    c = lax.broadcasted_iota(jnp.int32, (ROUTE_TM, ROUTE_TM), 1)
    tri = jnp.where(c <= r, 1.0, 0.0).astype(BF16)
    incl = _dot(tri, chosen.astype(BF16))
    posmat = base_ref[...] + incl - 1.0
    pick = lambda ln: jnp.sum(jnp.where(lane == ln, route, 0.0), axis=-1, keepdims=True)
    pos_of = lambda e: jnp.sum(jnp.where(lane_f == e, posmat, 0.0), axis=-1, keepdims=True)
    p1 = pos_of(pick(LANE_E1))
    p2 = pos_of(pick(LANE_E2))
    rec = jnp.where(lane == 0, p1, 0.0) + jnp.where(lane == 1, p2, 0.0)
    pos_ref[...] = rec.T[:8, :].astype(I32)


def _route_pos(route, base):
    t = route.shape[0]
    n = t // ROUTE_TM
    return pl.pallas_call(
        _route_pos_kernel,
        grid=(n,),
        in_specs=[pl.BlockSpec((ROUTE_TM, ROUTE_LANES), lambda i: (i, 0)),
                  pl.BlockSpec((None, 1, ROUTE_LANES), lambda i: (i, 0, 0))],
        out_specs=pl.BlockSpec((None, 8, ROUTE_TM), lambda i: (i, 0, 0)),
        out_shape=jax.ShapeDtypeStruct((n, 8, ROUTE_TM), I32),
        compiler_params=pltpu.CompilerParams(dimension_semantics=("parallel",),
                                             vmem_limit_bytes=VMEM_LIMIT),
        name="route_pos",
    )(route, base)


def _plan_kernel(cnt_ref, base_ref, owner_ref, valid_ref, used_ref):
    counts = cnt_ref[...]
    nt = counts.shape[0]
    exact = functools.partial(jnp.dot, precision=lax.Precision.HIGHEST, preferred_element_type=F32)
    total = jnp.sum(counts, axis=0, keepdims=True)
    tiles_e = jnp.ceil(total * (1.0 / ROW_TILE))
    k = lax.broadcasted_iota(jnp.int32, (ROUTE_LANES, ROUTE_LANES), 0)
    lane = lax.broadcasted_iota(jnp.int32, (ROUTE_LANES, ROUTE_LANES), 1)
    tile_end = exact(jnp.broadcast_to(tiles_e, (8, ROUTE_LANES)), jnp.where(k <= lane, 1.0, 0.0))[0:1]
    tile_start = tile_end - tiles_e
    r = lax.broadcasted_iota(jnp.int32, (nt, nt), 0)
    c = lax.broadcasted_iota(jnp.int32, (nt, nt), 1)
    before = exact(jnp.where(c < r, 1.0, 0.0), counts)
    base_ref[...] = tile_start * ROW_TILE + before

    used = jnp.max(tile_end, axis=-1, keepdims=True)
    tile = k.astype(F32)
    clamped = jnp.minimum(tile, jnp.maximum(used - 1.0, 0.0))
    is_expert = lane < N_EXPERTS
    owner = jnp.sum(jnp.where(jnp.logical_and(is_expert, tile_end <= clamped), 1.0, 0.0), axis=-1, keepdims=True)
    owner = jnp.minimum(owner, N_EXPERTS - 1.0)
    mine = lane.astype(F32) == owner
    pick = lambda v: jnp.sum(jnp.where(mine, v, 0.0), axis=-1, keepdims=True)
    t1 = tile[:, 0:1]
    valid = jnp.clip(pick(total) - (t1 - pick(tile_start)) * ROW_TILE, 0.0, float(ROW_TILE))
    valid = jnp.where(t1 < used, valid, 0.0)
    owner_ref[...] = jnp.broadcast_to(owner, owner_ref.shape).astype(I32)
    valid_ref[...] = jnp.broadcast_to(valid, valid_ref.shape).astype(I32)
    used_ref[...] = jnp.broadcast_to(used, used_ref.shape).astype(I32)


def _tile_plan(cnt, n_tiles):
    assert n_tiles <= ROUTE_LANES
    nt = cnt.shape[0]
    sq = (ROUTE_LANES, ROUTE_LANES)
    base, owner, valid, used = pl.pallas_call(
        _plan_kernel,
        out_shape=[jax.ShapeDtypeStruct((nt, ROUTE_LANES), F32), jax.ShapeDtypeStruct(sq, I32),
                   jax.ShapeDtypeStruct(sq, I32), jax.ShapeDtypeStruct((8, ROUTE_LANES), I32)],
        name="tile_plan",
    )(cnt.reshape(nt, ROUTE_LANES))
    return base.reshape(cnt.shape), owner[:n_tiles, 0], valid[:n_tiles, 0], used[0, :1]


def _sc_mesh():
    return plsc.VectorSubcoreMesh(core_axis_name="c", subcore_axis_name="s")


def _sc_worker():
    return lax.axis_index("s") * SC_CORES + lax.axis_index("c")


def _dispatch(hp, pos1, pos2, n_rows):
    t = hp.shape[0]
    per_worker = t // SC_WORKERS
    k = per_worker // SC_CHUNK
    idx = pltpu.VMEM((k, SC_CHUNK), I32)

    @functools.partial(pl.kernel, mesh=_sc_mesh(), out_type=jax.ShapeDtypeStruct((n_rows, HALF_D), I32),
                       scratch_types=[idx, idx, pltpu.VMEM((SC_CHUNK, HALF_D), I32)], name="moe_dispatch")
    def run(h_hbm, p1_hbm, p2_hbm, xs_hbm, p1_v, p2_v, rows_v):
        wid = _sc_worker()
        pltpu.sync_copy(p1_hbm.at[wid], p1_v)
        pltpu.sync_copy(p2_hbm.at[wid], p2_v)
        for j in range(k):
            pltpu.sync_copy(h_hbm.at[pl.ds(wid * per_worker + j * SC_CHUNK, SC_CHUNK)], rows_v)
            pltpu.sync_copy(rows_v, xs_hbm.at[p1_v.at[j]])
            pltpu.sync_copy(rows_v, xs_hbm.at[p2_v.at[j]])

    return run(hp, pos1.reshape(SC_WORKERS, k, SC_CHUNK), pos2.reshape(SC_WORKERS, k, SC_CHUNK))


def _combine_gather(ys, pos1, pos2, t):
    per_worker = t // SC_WORKERS
    k = per_worker // SC_CHUNK
    idx = pltpu.VMEM((k, SC_CHUNK), I32)
    out = jax.ShapeDtypeStruct((t, HALF_D), I32)

    @functools.partial(pl.kernel, mesh=_sc_mesh(), out_type=(out, out),
                       scratch_types=[idx, idx, pltpu.VMEM((SC_CHUNK, HALF_D), I32)], name="moe_combine")
    def run(ys_hbm, p1_hbm, p2_hbm, g1_hbm, g2_hbm, p1_v, p2_v, rows_v):
        wid = _sc_worker()
        pltpu.sync_copy(p1_hbm.at[wid], p1_v)
        pltpu.sync_copy(p2_hbm.at[wid], p2_v)
        for j in range(k):
            dst = pl.ds(wid * per_worker + j * SC_CHUNK, SC_CHUNK)
            pltpu.sync_copy(ys_hbm.at[p1_v.at[j]], rows_v)
            pltpu.sync_copy(rows_v, g1_hbm.at[dst])
            pltpu.sync_copy(ys_hbm.at[p2_v.at[j]], rows_v)
            pltpu.sync_copy(rows_v, g2_hbm.at[dst])

    return run(ys, pos1.reshape(SC_WORKERS, k, SC_CHUNK), pos2.reshape(SC_WORKERS, k, SC_CHUNK))


def _experts_kernel(owner_ref, valid_ref, used_ref, xs_ref, wg_ref, wu_ref, wd_ref, ys_ref, wg_bf, wu_bf, wd_bf):
    i = pl.program_id(0)
    valid = valid_ref[i]

    @pl.when(jnp.logical_or(i == 0, owner_ref[i] != owner_ref[jnp.maximum(i - 1, 0)]))
    def _():
        wg_bf[...] = wg_ref[...].astype(BF16)
        wu_bf[...] = wu_ref[...].astype(BF16)
        wd_bf[...] = wd_ref[...].astype(BF16)

    @pl.when(i < used_ref[0])
    def _():
        row = lax.broadcasted_iota(jnp.int32, (ROW_TILE, HALF_D), 0)
        packed = jnp.where(row < valid, xs_ref[...], 0)
        hi, lo = _unpack_halves(packed)
        hi = hi.astype(BF16)
        lo = lo.astype(BF16)
        hg = _dot(hi, wg_bf[:HALF_D, :]) + _dot(lo, wg_bf[HALF_D:, :])
        hu = _dot(hi, wu_bf[:HALF_D, :]) + _dot(lo, wu_bf[HALF_D:, :])
        hid = (hg * _sigmoid(hg) * hu).astype(BF16)
        ys_ref[...] = _pack_halves(_dot(hid, wd_bf[...]))


def _experts(xs, owner, valid, used, wg, wu, wd):
    n_tiles = owner.shape[0]
    tile = lambda i, own, val, used: (jnp.minimum(i, used[0] - 1), 0)
    grid_spec = pltpu.PrefetchScalarGridSpec(
        num_scalar_prefetch=3,
        grid=(n_tiles,),
        in_specs=[pl.BlockSpec((ROW_TILE, HALF_D), tile),
                  pl.BlockSpec((None, D_MODEL, EXPERT_FF), lambda i, own, val, used: (own[i], 0, 0)),
                  pl.BlockSpec((None, D_MODEL, EXPERT_FF), lambda i, own, val, used: (own[i], 0, 0)),
                  pl.BlockSpec((None, EXPERT_FF, D_MODEL), lambda i, own, val, used: (own[i], 0, 0))],
        out_specs=pl.BlockSpec((ROW_TILE, HALF_D), tile),
        scratch_shapes=[pltpu.VMEM((D_MODEL, EXPERT_FF), BF16), pltpu.VMEM((D_MODEL, EXPERT_FF), BF16),
                        pltpu.VMEM((EXPERT_FF, D_MODEL), BF16)],
    )
    return pl.pallas_call(
        _experts_kernel,
        grid_spec=grid_spec,
        out_shape=jax.ShapeDtypeStruct((n_tiles * ROW_TILE, HALF_D), I32),
        compiler_params=pltpu.CompilerParams(dimension_semantics=("arbitrary",),
                                             vmem_limit_bytes=VMEM_LIMIT),
        name="experts",
    )(owner, valid, used, xs, wg, wu, wd)


def _ple_kernel(x1_ref, g1_ref, g2_ref, route_ref, p_ref, wp_ref, pg_ref, gg_ref, wgate_ref, o_ref):
    route = route_ref[...]
    lane = lax.broadcasted_iota(jnp.int32, route.shape, 1)
    pick = lambda ln: jnp.sum(jnp.where(lane == ln, route, 0.0), axis=-1, keepdims=True)
    c1 = pick(LANE_C1)
    c2 = pick(LANE_C2)
    a_hi, a_lo = _unpack_halves(g1_ref[...])
    b_hi, b_lo = _unpack_halves(g2_ref[...])
    x = x1_ref[...] + jnp.concatenate([c1 * a_hi + c2 * b_hi, c1 * a_lo + c2 * b_lo], axis=1)
    ple = _rms(_dot(p_ref[...].astype(BF16), wp_ref[...]), pg_ref[...])
    gate = _sigmoid(_dot(_rms(x, gg_ref[...]).astype(BF16), wgate_ref[...]))
    o_ref[...] = x + gate * ple


def _ple(x1, g1, g2, route, p2d, w_ple, ple_g, gate_g, w_gate):
    t = x1.shape[0]
    row = lambda w: pl.BlockSpec((PLE_TM, w), lambda i: (i, 0))
    full = lambda a: pl.BlockSpec(a.shape, lambda i: (0,) * a.ndim)
    return pl.pallas_call(
        _ple_kernel,
        grid=(t // PLE_TM,),
        in_specs=[row(D_MODEL), row(HALF_D), row(HALF_D), row(ROUTE_LANES), row(PLE_DIM),
                  full(w_ple), full(ple_g), full(gate_g), full(w_gate)],
        out_specs=row(D_MODEL),
        out_shape=jax.ShapeDtypeStruct((t, D_MODEL), F32),
        compiler_params=pltpu.CompilerParams(dimension_semantics=("parallel",),
                                             vmem_limit_bytes=VMEM_LIMIT),
        name="ple",
    )(x1, g1, g2, route, p2d, w_ple, ple_g, gate_g, w_gate)


def kernel(x, p, mix_norm_g, w_in, ret_gn_g, ret_gn_b, w_ret_o, q_norm_g, k_norm_g, attn_sinks,
           w_swa_o, w_out, ffn_norm_g, w_router_group, b_router_group, w_router_expert,
           b_router_expert, w_exp_gate, w_exp_up, w_exp_down, ple_gate_norm_g, w_ple_gate,
           w_ple, ple_norm_g):
    batch, seq, d = x.shape
    t = batch * seq
    depth = w_in.shape[0]
    n_tiles = 2 * t // ROW_TILE + N_EXPERTS
    row = lambda a: a.reshape(1, -1)
    x2d = x.reshape(t, d)
    for i in range(depth):
        z_gate, y_r, y_s = _mixer(x2d, row(mix_norm_g[i]), w_in[i].astype(BF16), row(ret_gn_g[i]),
                                  row(ret_gn_b[i]), attn_sinks[i], q_norm_g[i], k_norm_g[i], seq)
        pad = ROUTE_LANES - N_GROUPS - N_EXPERTS
        w_router = jnp.pad(jnp.concatenate([w_router_expert[i], w_router_group[i]], axis=1), ((0, 0), (0, pad)))
        b_router = jnp.pad(jnp.concatenate([b_router_expert[i], b_router_group[i]]), (0, pad)).reshape(1, -1)
        w_router_hi = w_router.astype(BF16)
        w_router = jnp.concatenate([w_router_hi, (w_router - w_router_hi.astype(F32)).astype(BF16)], axis=1)
        x1, hp, route, cnt = _merge(y_r, y_s, z_gate, x2d, w_ret_o[i].astype(BF16), w_swa_o[i].astype(BF16),
                                    w_out[i].astype(BF16), row(ffn_norm_g[i]), w_router, b_router)
        base, owner, valid, used = _tile_plan(cnt, n_tiles)
        pos = _route_pos(route, base)
        pos1 = pos[:, 0, :]
        pos2 = pos[:, 1, :]
        xs = _dispatch(hp, pos1, pos2, n_tiles * ROW_TILE)
        ys = _experts(xs, owner, valid, used, w_exp_gate[i], w_exp_up[i], w_exp_down[i])
        g1, g2 = _combine_gather(ys, pos1, pos2, t)
        x2d = _ple(x1, g1, g2, route, p[i].reshape(t, PLE_DIM), w_ple[i].astype(BF16), row(ple_norm_g[i]),
                   row(ple_gate_norm_g[i]), w_ple_gate[i].astype(BF16))
    return x2d.reshape(batch, seq, d)
```

```python
import functools

import jax
import jax.numpy as jnp
from jax import lax
from jax.experimental import pallas as pl
from jax.experimental.pallas import tpu as pltpu
from jax.experimental.pallas import tpu_sc as plsc

F32 = jnp.float32
BF16 = jnp.bfloat16
I32 = jnp.int32

EPS = 1e-6
D_MODEL = 1024
PLE_DIM = 256
RET_HEADS = 4
RET_DK = 128
RET_DV = 128
RET_CHUNK = 128
SWA_HEADS = 8
SWA_KV_HEADS = 2
SWA_GROUP = SWA_HEADS // SWA_KV_HEADS
SWA_HD = 64
WINDOW = 128
N_GROUPS = 4
EXPERTS_PER_GROUP = 8
N_EXPERTS = N_GROUPS * EXPERTS_PER_GROUP
EXPERT_FF = 256

RET_W = RET_HEADS * RET_DK
SWA_Q = SWA_HEADS * SWA_HD
SWA_KV = SWA_KV_HEADS * SWA_HD
Z_RET_W = 4 * RET_W
Z_SKV_W = 2 * SWA_KV
Z_GATE_W = 2 * D_MODEL
COL_SQ = Z_RET_W
COL_SKV = COL_SQ + SWA_Q
COL_GATE = COL_SKV + Z_SKV_W
IN_WIDTH = COL_GATE + Z_GATE_W

ROUTE_LANES = 128
LANE_GROUP0 = N_EXPERTS
LANE_E1, LANE_E2, LANE_C1, LANE_C2 = 32, 33, 34, 35
NEG_INF = -1e30

HALF_D = D_MODEL // 2
ROW_TILE = 512
SC_CORES = 2
SC_SUBCORES = 16
SC_WORKERS = SC_CORES * SC_SUBCORES
SC_CHUNK = 128

VMEM_LIMIT = 56 * 1024 * 1024

BLOCK_ROWS = 512
RET_ROWS = 512
MERGE_SUB = 256
ROUTE_TM = 512
PLE_TM = 512


def _rms(x, g):
    ms = jnp.mean(x * x, axis=-1, keepdims=True)
    return x * lax.rsqrt(ms + EPS) * g


def _sigmoid(x):
    return 1.0 / (1.0 + jnp.exp(-x))


def _dot(a, b):
    return jnp.dot(a, b, preferred_element_type=F32)


def _dot_nt(a, b):
    return lax.dot_general(a, b, (((1,), (1,)), ((), ())), preferred_element_type=F32)


def _dot_tn(a, b):
    return lax.dot_general(a, b, (((0,), (0,)), ((), ())), preferred_element_type=F32)


def _pack_halves(v):
    return pltpu.pack_elementwise([v[:, :HALF_D], v[:, HALF_D:]], packed_dtype=BF16)


def _unpack_halves(p):
    return tuple(pltpu.unpack_elementwise(p, index=k, packed_dtype=BF16, unpacked_dtype=F32) for k in range(2))


def _retention_tables():
    h = RET_HEADS
    c = RET_CHUNK
    log_gamma = jnp.log1p(-jnp.exp2(-5.0 - jnp.arange(h, dtype=F32)))
    pos = jnp.arange(c, dtype=F32)
    diff = pos[:, None] - pos[None, :]
    decay = jnp.where(diff[None] >= 0.0,
                      jnp.exp(jnp.maximum(diff, 0.0)[None] * log_gamma[:, None, None]), 0.0)
    scale = RET_DK ** -0.5
    dmask = decay * scale
    zeta = jnp.exp((c - 1.0 - pos)[None, :] * log_gamma[:, None]) * scale
    xi = jnp.exp((pos + 1.0)[None, :] * log_gamma[:, None])
    cdec = jnp.exp(c * log_gamma)
    bc = lambda a: jnp.broadcast_to(a[:, :, None], (h, c, c))
    return dmask, bc(zeta), bc(xi), jnp.broadcast_to(cdec[:, None, None], (h, c, c))


def _swa_tables(q_g, k_g):
    qi = jnp.arange(WINDOW)[:, None]
    sj = jnp.arange(2 * WINDOW)[None, :]
    rel = qi + WINDOW - sj
    ok = (rel >= 0) & (rel < WINDOW)
    slopes = jnp.exp2(-8.0 * jnp.arange(1, SWA_HEADS + 1, dtype=F32) / SWA_HEADS)
    bias = jnp.where(ok[None], -slopes[:, None, None] * rel.astype(F32)[None], NEG_INF)
    bias = bias.reshape(SWA_KV_HEADS, SWA_GROUP * WINDOW, 2 * WINDOW)
    avg = lambda heads: jnp.kron(jnp.eye(heads, dtype=F32), jnp.full((SWA_HD, SWA_HD), 1.0 / SWA_HD, F32)).astype(BF16)
    qg = jnp.tile(q_g.reshape(1, SWA_HD), (1, SWA_HEADS)) * (SWA_HD ** -0.5)
    kg = jnp.tile(k_g.reshape(1, SWA_HD), (1, SWA_KV_HEADS))
    return qg, kg, avg(SWA_HEADS), avg(SWA_KV_HEADS), bias


def _block_kernel(per_seq, sinks_ref, x_ref, xp_ref, g_ref, w_hbm, dmask_ref, zeta_ref, xi_ref, cdec_ref,
                  gng_ref, gnb_ref, qg_ref, kg_ref, bq_ref, bk_ref, bias_ref,
                  wro_ref, wso_ref, wout_ref, fg_ref, wr_ref, br_ref,
                  x1_ref, hp_ref, route_ref, cnt_ref,
                  w_ref, w_sem, zc_ref, zn_ref, yr_ref, ys_ref, state_ref, prev_ref, band_ref, tail_ref):
    g = pl.program_id(0)
    seq_start = lax.rem(g - 1, per_seq) == 0

    @pl.when(g == 0)
    def _():
        w_copy = pltpu.make_async_copy(w_hbm, w_ref, w_sem)
        w_copy.start()
        w_copy.wait()
        zc_ref[...] = jnp.zeros_like(zc_ref)
        tail_ref[...] = jnp.zeros_like(tail_ref)
        state_ref[...] = jnp.zeros_like(state_ref)

    @pl.when(seq_start)
    def _():
        state_ref[...] = jnp.zeros_like(state_ref)

    band_ref[:WINDOW, :] = tail_ref[...]
    band_ref[WINDOW:, :] = zc_ref[:, COL_SKV:COL_GATE]

    h = _rms(x_ref[...], g_ref[...]).astype(BF16)

    def project(c0, width):
        zn_ref[:, c0:c0 + width] = _dot(h, w_ref[:, c0:c0 + width]).astype(BF16)

    cw = 512
    proj_pieces = [functools.partial(project, j * cw, cw) for j in range(COL_SKV // cw)]
    proj_pieces.append(functools.partial(project, COL_SKV, Z_SKV_W))
    proj_pieces += [functools.partial(project, COL_GATE + j * cw, cw) for j in range(Z_GATE_W // cw)]

    def retention_rows(t):
        r_base = t * RET_ROWS
        n_chunks = RET_ROWS // RET_CHUNK
        part = lambda c, p, hd: zc_ref[r_base + c * RET_CHUNK:r_base + (c + 1) * RET_CHUNK,
                                       p * RET_W + hd * RET_DK:p * RET_W + (hd + 1) * RET_DK]
        for hd in range(RET_HEADS):
            state = state_ref[hd]
            for c in range(n_chunks):
                prev_ref[t * n_chunks + c, hd] = state.astype(BF16)
                kz = (part(c, 1, hd).astype(F32) * zeta_ref[hd]).astype(BF16)
                state = cdec_ref[hd] * state + _dot_tn(kz, part(c, 2, hd))
            state_ref[hd] = state
        out_rows = []
        for c in range(n_chunks):
            heads = []
            for hd in range(RET_HEADS):
                q = part(c, 0, hd)
                scores = _dot_nt(q, part(c, 1, hd)) * dmask_ref[hd]
                y = _dot(scores.astype(BF16), part(c, 2, hd)) + _dot(q, prev_ref[t * n_chunks + c, hd]) * xi_ref[hd]
                mu = jnp.mean(y, axis=-1, keepdims=True)
                d = y - mu
                var = jnp.mean(d * d, axis=-1, keepdims=True)
                hs = slice(hd * RET_DV, (hd + 1) * RET_DV)
                yn = d * lax.rsqrt(var + EPS) * gng_ref[:, hs] + gnb_ref[:, hs]
                gate = part(c, 3, hd).astype(F32)
                heads.append((gate * _sigmoid(gate) * yn).astype(BF16))
            out_rows.append(jnp.concatenate(heads, axis=1))
        yr_ref[r_base:r_base + RET_ROWS, :] = jnp.concatenate(out_rows, axis=0)

    group_rows = SWA_GROUP * WINDOW
    lo_q = lax.broadcasted_iota(jnp.int32, (WINDOW, 2 * SWA_HD), 1) < SWA_HD
    lo_k = lax.broadcasted_iota(jnp.int32, (2 * WINDOW, 2 * SWA_HD), 1) < SWA_HD
    in_prev = lax.broadcasted_iota(jnp.int32, (group_rows, 2 * WINDOW), 1) < WINDOW
    row_head = lax.broadcasted_iota(jnp.int32, (group_rows, 1), 0) // WINDOW

    def both_heads(v):
        swapped = pltpu.roll(v, SWA_HD, axis=1)
        return jnp.where(lo_k, v, swapped), jnp.where(lo_k, swapped, v)

    def swa_block(n):
        rows = slice(n * WINDOW, (n + 1) * WINDOW)
        kv = band_ref[n * WINDOW:(n + 2) * WINDOW, :]
        kf = kv[:, :SWA_KV].astype(F32)
        kn = kf * lax.rsqrt(_dot((kf * kf).astype(BF16), bk_ref[...]) + EPS) * kg_ref[...]
        keys = [a.astype(BF16) for a in both_heads(kn)]
        vals = [a.astype(BF16) for a in both_heads(kv[:, SWA_KV:].astype(F32))]
        qf = zc_ref[rows, COL_SQ:COL_SKV].astype(F32)
        qn = qf * lax.rsqrt(_dot((qf * qf).astype(BF16), bq_ref[...]) + EPS) * qg_ref[...]
        outs = []
        for kh in range(SWA_KV_HEADS):
            parts = []
            for j in range(SWA_GROUP // 2):
                c0 = (kh * SWA_GROUP + 2 * j) * SWA_HD
                two = qn[:, c0:c0 + 2 * SWA_HD]
                parts += [jnp.where(lo_q, two, 0.0), jnp.where(lo_q, 0.0, two)]
            qs = jnp.concatenate(parts, axis=0).astype(BF16)
            s = _dot_nt(qs, keys[kh]) + bias_ref[kh]
            if n == 0:
                s = jnp.where(jnp.logical_and(seq_start, in_prev), NEG_INF, s)
            sink = sinks_ref[kh * SWA_GROUP]
            for gi in range(1, SWA_GROUP):
                sink = jnp.where(row_head == gi, sinks_ref[kh * SWA_GROUP + gi], sink)
            m = jnp.maximum(jnp.max(s, axis=-1, keepdims=True), sink)
            p = jnp.exp(s - m)
            denom = jnp.sum(p, axis=-1, keepdims=True) + jnp.exp(sink - m)
            o = _dot(p.astype(BF16), vals[kh]) * (1.0 / denom)
            for j in range(SWA_GROUP // 2):
                even = o[(2 * j) * WINDOW:(2 * j + 1) * WINDOW]
                odd = o[(2 * j + 1) * WINDOW:(2 * j + 2) * WINDOW]
                outs.append(jnp.where(lo_q, even, odd).astype(BF16))
        ys_ref[rows, :] = jnp.concatenate(outs, axis=1)

    mix_pieces = [functools.partial(retention_rows, t) for t in range(BLOCK_ROWS // RET_ROWS)]
    mix_pieces += [functools.partial(swa_block, n) for n in range(BLOCK_ROWS // WINDOW)]

    for k in range(max(len(proj_pieces), len(mix_pieces))):
        if k < len(proj_pieces):
            proj_pieces[k]()
        if k < len(mix_pieces):
            mix_pieces[k]()

    subs = [_merge_rows(slice(s * MERGE_SUB, (s + 1) * MERGE_SUB), yr_ref, ys_ref, zc_ref, xp_ref, wro_ref,
                        wso_ref, wout_ref, fg_ref, wr_ref, br_ref) for s in range(BLOCK_ROWS // MERGE_SUB)]
    x1_ref[...] = jnp.concatenate([s[0] for s in subs], axis=0)
    hp_ref[...] = jnp.concatenate([s[1] for s in subs], axis=0)
    route_ref[...] = jnp.concatenate([s[2] for s in subs], axis=0)
    cnt_ref[...] = sum(s[3] for s in subs)

    tail_ref[...] = zc_ref[BLOCK_ROWS - WINDOW:, COL_SKV:COL_GATE]
    zc_ref[...] = zn_ref[...]


def _merge_rows(rows, yr_ref, ys_ref, z_ref, x_ref, wro_ref, wso_ref, wout_ref, fg_ref, wr_ref, br_ref):
    a = _dot(yr_ref[rows, :], wro_ref[...])
    b = _dot(ys_ref[rows, :], wso_ref[...])
    gate_r = z_ref[rows, COL_GATE:COL_GATE + D_MODEL].astype(F32)
    gate_s = z_ref[rows, COL_GATE + D_MODEL:].astype(F32)
    merged = (_sigmoid(gate_r) * a + _sigmoid(gate_s) * b).astype(BF16)
    x1 = x_ref[rows, :] + _dot(merged, wout_ref[...])
    h2 = _rms(x1, fg_ref[...])
    packed = _pack_halves(h2)

    h_hi = h2.astype(BF16)
    h_lo = (h2 - h_hi.astype(F32)).astype(BF16)
    parts = _dot(h_hi, wr_ref[...]) + _dot(h_lo, wr_ref[...])
    logits = parts[:, :ROUTE_LANES] + parts[:, ROUTE_LANES:] + br_ref[...]
    lane = lax.broadcasted_iota(jnp.int32, logits.shape, 1)
    big = jnp.int32(ROUTE_LANES)
    is_group = jnp.logical_and(lane >= LANE_GROUP0, lane < LANE_GROUP0 + N_GROUPS)
    gl = jnp.where(is_group, logits, NEG_INF)
    gmax = jnp.max(gl, axis=-1, keepdims=True)
    g_w = 1.0 / jnp.sum(jnp.exp(gl - gmax), axis=-1, keepdims=True)
    g_sel = jnp.min(jnp.where(gl == gmax, lane, big), axis=-1, keepdims=True) - LANE_GROUP0
    in_group = jnp.logical_and(lane < N_EXPERTS, (lane >> 3) == g_sel)
    el = jnp.where(in_group, logits, NEG_INF)
    m1 = jnp.max(el, axis=-1, keepdims=True)
    i1 = jnp.min(jnp.where(el == m1, lane, big), axis=-1, keepdims=True)
    el2 = jnp.where(lane == i1, NEG_INF, el)
    m2 = jnp.max(el2, axis=-1, keepdims=True)
    i2 = jnp.min(jnp.where(el2 == m2, lane, big), axis=-1, keepdims=True)
    e2 = jnp.exp(m2 - m1)
    c1 = g_w / (1.0 + e2)
    c2 = g_w * e2 / (1.0 + e2)
    chosen = jnp.where(jnp.logical_or(lane == i1, lane == i2), 1.0, 0.0)
    route = (chosen
             + jnp.where(lane == LANE_E1, i1.astype(F32), 0.0)
             + jnp.where(lane == LANE_E2, i2.astype(F32), 0.0)
             + jnp.where(lane == LANE_C1, c1, 0.0)
             + jnp.where(lane == LANE_C2, c2, 0.0))
    return x1, packed, route, jnp.sum(chosen, axis=0, keepdims=True)


def _block(x2d, mix_g, w_bf, gn_g, gn_b, sinks, q_g, k_g, w_ret_o, w_swa_o, w_out, ffn_g, w_router, b_router,
           seq):
    t = x2d.shape[0]
    n = t // BLOCK_ROWS
    assert seq % BLOCK_ROWS == 0
    tables = _retention_tables()
    swa_tables = _swa_tables(q_g, k_g)
    consts = tables + (gn_g, gn_b) + swa_tables + (w_ret_o, w_swa_o, w_out, ffn_g, w_router, b_router)
    full = lambda a: pl.BlockSpec(a.shape, lambda g: (0,) * a.ndim)
    cur = lambda w: pl.BlockSpec((BLOCK_ROWS, w), lambda g: (jnp.minimum(g, n - 1), 0))
    prv = lambda w: pl.BlockSpec((BLOCK_ROWS, w), lambda g: (jnp.maximum(g - 1, 0), 0))
    chunks = BLOCK_ROWS // RET_CHUNK
    return pl.pallas_call(
        functools.partial(_block_kernel, seq // BLOCK_ROWS),
        grid=(n + 1,),
        in_specs=[pl.BlockSpec(memory_space=pltpu.SMEM), cur(D_MODEL), prv(D_MODEL), full(mix_g),
                  pl.BlockSpec(memory_space=pl.ANY)] + [full(a) for a in consts],
        out_specs=[prv(D_MODEL), prv(HALF_D), prv(ROUTE_LANES),
                   pl.BlockSpec((None, 1, ROUTE_LANES), lambda g: (jnp.maximum(g - 1, 0), 0, 0))],
        out_shape=[jax.ShapeDtypeStruct((t, D_MODEL), F32),
                   jax.ShapeDtypeStruct((t, HALF_D), I32),
                   jax.ShapeDtypeStruct((t, ROUTE_LANES), F32),
                   jax.ShapeDtypeStruct((n, 1, ROUTE_LANES), F32)],
        scratch_shapes=[pltpu.VMEM(w_bf.shape, BF16), pltpu.SemaphoreType.DMA,
                        pltpu.VMEM((BLOCK_ROWS, IN_WIDTH), BF16), pltpu.VMEM((BLOCK_ROWS, IN_WIDTH), BF16),
                        pltpu.VMEM((BLOCK_ROWS, RET_W), BF16), pltpu.VMEM((BLOCK_ROWS, SWA_Q), BF16),
                        pltpu.VMEM((RET_HEADS, RET_DK, RET_DV), F32),
                        pltpu.VMEM((chunks, RET_HEADS, RET_DK, RET_DV), BF16),
                        pltpu.VMEM((BLOCK_ROWS + WINDOW, Z_SKV_W), BF16), pltpu.VMEM((WINDOW, Z_SKV_W), BF16)],
        compiler_params=pltpu.CompilerParams(dimension_semantics=("arbitrary",),
                                             vmem_limit_bytes=VMEM_LIMIT),
        name="block",
    )(sinks, x2d, x2d, mix_g, w_bf, *consts)


def _route_pos_kernel(route_ref, base_ref, pos_ref):
    route = route_ref[...]
    lane = lax.broadcasted_iota(jnp.int32, route.shape, 1)
    lane_f = lane.astype(F32)
    chosen = jnp.where(lane < N_EXPERTS, route, 0.0)
    r = lax.broadcasted_iota(jnp.int32, (ROUTE_TM, ROUTE_TM), 0)
    c = lax.broadcasted_iota(jnp.int32, (ROUTE_TM, ROUTE_TM), 1)
    tri = jnp.where(c <= r, 1.0, 0.0).astype(BF16)
    incl = _dot(tri, chosen.astype(BF16))
    posmat = base_ref[...] + incl - 1.0
    pick = lambda ln: jnp.sum(jnp.where(lane == ln, route, 0.0), axis=-1, keepdims=True)
    pos_of = lambda e: jnp.sum(jnp.where(lane_f == e, posmat, 0.0), axis=-1, keepdims=True)
    p1 = pos_of(pick(LANE_E1))
    p2 = pos_of(pick(LANE_E2))
    rec = jnp.where(lane == 0, p1, 0.0) + jnp.where(lane == 1, p2, 0.0)
    pos_ref[...] = rec.T[:8, :].astype(I32)


def _route_pos(route, base):
    t = route.shape[0]
    n = t // ROUTE_TM
    return pl.pallas_call(
        _route_pos_kernel,
        grid=(n,),
        in_specs=[pl.BlockSpec((ROUTE_TM, ROUTE_LANES), lambda i: (i, 0)),
                  pl.BlockSpec((None, 1, ROUTE_LANES), lambda i: (i, 0, 0))],
        out_specs=pl.BlockSpec((None, 8, ROUTE_TM), lambda i: (i, 0, 0)),
        out_shape=jax.ShapeDtypeStruct((n, 8, ROUTE_TM), I32),
        compiler_params=pltpu.CompilerParams(dimension_semantics=("parallel",),
                                             vmem_limit_bytes=VMEM_LIMIT),
        name="route_pos",
    )(route, base)


def _plan_kernel(cnt_ref, base_ref, owner_ref, valid_ref, used_ref):
    counts = cnt_ref[...]
    nt = counts.shape[0]
    exact = functools.partial(jnp.dot, precision=lax.Precision.HIGHEST, preferred_element_type=F32)
    total = jnp.sum(counts, axis=0, keepdims=True)
    tiles_e = jnp.ceil(total * (1.0 / ROW_TILE))
    k = lax.broadcasted_iota(jnp.int32, (ROUTE_LANES, ROUTE_LANES), 0)
    lane = lax.broadcasted_iota(jnp.int32, (ROUTE_LANES, ROUTE_LANES), 1)
    tile_end = exact(jnp.broadcast_to(tiles_e, (8, ROUTE_LANES)), jnp.where(k <= lane, 1.0, 0.0))[0:1]
    tile_start = tile_end - tiles_e
    r = lax.broadcasted_iota(jnp.int32, (nt, nt), 0)
    c = lax.broadcasted_iota(jnp.int32, (nt, nt), 1)
    before = exact(jnp.where(c < r, 1.0, 0.0), counts)
    base_ref[...] = tile_start * ROW_TILE + before

    used = jnp.max(tile_end, axis=-1, keepdims=True)
    tile = k.astype(F32)
    clamped = jnp.minimum(tile, jnp.maximum(used - 1.0, 0.0))
    is_expert = lane < N_EXPERTS
    owner = jnp.sum(jnp.where(jnp.logical_and(is_expert, tile_end <= clamped), 1.0, 0.0), axis=-1, keepdims=True)
    owner = jnp.minimum(owner, N_EXPERTS - 1.0)
    mine = lane.astype(F32) == owner
    pick = lambda v: jnp.sum(jnp.where(mine, v, 0.0), axis=-1, keepdims=True)
    t1 = tile[:, 0:1]
    valid = jnp.clip(pick(total) - (t1 - pick(tile_start)) * ROW_TILE, 0.0, float(ROW_TILE))
    valid = jnp.where(t1 < used, valid, 0.0)
    owner_ref[...] = jnp.broadcast_to(owner, owner_ref.shape).astype(I32)
    valid_ref[...] = jnp.broadcast_to(valid, valid_ref.shape).astype(I32)
    used_ref[...] = jnp.broadcast_to(used, used_ref.shape).astype(I32)


def _tile_plan(cnt, n_tiles):
    assert n_tiles <= ROUTE_LANES
    nt = cnt.shape[0]
    sq = (ROUTE_LANES, ROUTE_LANES)
    base, owner, valid, used = pl.pallas_call(
        _plan_kernel,
        out_shape=[jax.ShapeDtypeStruct((nt, ROUTE_LANES), F32), jax.ShapeDtypeStruct(sq, I32),
                   jax.ShapeDtypeStruct(sq, I32), jax.ShapeDtypeStruct((8, ROUTE_LANES), I32)],
        name="tile_plan",
    )(cnt.reshape(nt, ROUTE_LANES))
    return base.reshape(cnt.shape), owner[:n_tiles, 0], valid[:n_tiles, 0], used[0, :1]


def _sc_mesh():
    return plsc.VectorSubcoreMesh(core_axis_name="c", subcore_axis_name="s")


def _sc_worker():
    return lax.axis_index("s") * SC_CORES + lax.axis_index("c")


def _dispatch(hp, pos1, pos2, n_rows):
    t = hp.shape[0]
    per_worker = t // SC_WORKERS
    k = per_worker // SC_CHUNK
    idx = pltpu.VMEM((k, SC_CHUNK), I32)

    @functools.partial(pl.kernel, mesh=_sc_mesh(), out_type=jax.ShapeDtypeStruct((n_rows, HALF_D), I32),
                       scratch_types=[idx, idx, pltpu.VMEM((SC_CHUNK, HALF_D), I32)], name="moe_dispatch")
    def run(h_hbm, p1_hbm, p2_hbm, xs_hbm, p1_v, p2_v, rows_v):
        wid = _sc_worker()
        pltpu.sync_copy(p1_hbm.at[wid], p1_v)
        pltpu.sync_copy(p2_hbm.at[wid], p2_v)
        for j in range(k):
            pltpu.sync_copy(h_hbm.at[pl.ds(wid * per_worker + j * SC_CHUNK, SC_CHUNK)], rows_v)
            pltpu.sync_copy(rows_v, xs_hbm.at[p1_v.at[j]])
            pltpu.sync_copy(rows_v, xs_hbm.at[p2_v.at[j]])

    return run(hp, pos1.reshape(SC_WORKERS, k, SC_CHUNK), pos2.reshape(SC_WORKERS, k, SC_CHUNK))


def _combine_gather(ys, pos1, pos2, t):
    per_worker = t // SC_WORKERS
    k = per_worker // SC_CHUNK
    idx = pltpu.VMEM((k, SC_CHUNK), I32)
    out = jax.ShapeDtypeStruct((t, HALF_D), I32)

    @functools.partial(pl.kernel, mesh=_sc_mesh(), out_type=(out, out),
                       scratch_types=[idx, idx, pltpu.VMEM((SC_CHUNK, HALF_D), I32)], name="moe_combine")
    def run(ys_hbm, p1_hbm, p2_hbm, g1_hbm, g2_hbm, p1_v, p2_v, rows_v):
        wid = _sc_worker()
        pltpu.sync_copy(p1_hbm.at[wid], p1_v)
        pltpu.sync_copy(p2_hbm.at[wid], p2_v)
        for j in range(k):
            dst = pl.ds(wid * per_worker + j * SC_CHUNK, SC_CHUNK)
            pltpu.sync_copy(ys_hbm.at[p1_v.at[j]], rows_v)
            pltpu.sync_copy(rows_v, g1_hbm.at[dst])
            pltpu.sync_copy(ys_hbm.at[p2_v.at[j]], rows_v)
            pltpu.sync_copy(rows_v, g2_hbm.at[dst])

    return run(ys, pos1.reshape(SC_WORKERS, k, SC_CHUNK), pos2.reshape(SC_WORKERS, k, SC_CHUNK))


def _experts_kernel(owner_ref, valid_ref, used_ref, xs_ref, wg_ref, wu_ref, wd_ref, ys_ref, wg_bf, wu_bf, wd_bf):
    i = pl.program_id(0)
    valid = valid_ref[i]

    @pl.when(jnp.logical_or(i == 0, owner_ref[i] != owner_ref[jnp.maximum(i - 1, 0)]))
    def _():
        wg_bf[...] = wg_ref[...].astype(BF16)
        wu_bf[...] = wu_ref[...].astype(BF16)
        wd_bf[...] = wd_ref[...].astype(BF16)

    @pl.when(i < used_ref[0])
    def _():
        row = lax.broadcasted_iota(jnp.int32, (ROW_TILE, HALF_D), 0)
        packed = jnp.where(row < valid, xs_ref[...], 0)
        hi, lo = _unpack_halves(packed)
        hi = hi.astype(BF16)
        lo = lo.astype(BF16)
        hg = _dot(hi, wg_bf[:HALF_D, :]) + _dot(lo, wg_bf[HALF_D:, :])
        hu = _dot(hi, wu_bf[:HALF_D, :]) + _dot(lo, wu_bf[HALF_D:, :])
        hid = (hg * _sigmoid(hg) * hu).astype(BF16)
        ys_ref[...] = _pack_halves(_dot(hid, wd_bf[...]))


def _experts(xs, owner, valid, used, wg, wu, wd):
    n_tiles = owner.shape[0]
    tile = lambda i, own, val, used: (jnp.minimum(i, used[0] - 1), 0)
    grid_spec = pltpu.PrefetchScalarGridSpec(
        num_scalar_prefetch=3,
        grid=(n_tiles,),
        in_specs=[pl.BlockSpec((ROW_TILE, HALF_D), tile),
                  pl.BlockSpec((None, D_MODEL, EXPERT_FF), lambda i, own, val, used: (own[i], 0, 0)),
                  pl.BlockSpec((None, D_MODEL, EXPERT_FF), lambda i, own, val, used: (own[i], 0, 0)),
                  pl.BlockSpec((None, EXPERT_FF, D_MODEL), lambda i, own, val, used: (own[i], 0, 0))],
        out_specs=pl.BlockSpec((ROW_TILE, HALF_D), tile),
        scratch_shapes=[pltpu.VMEM((D_MODEL, EXPERT_FF), BF16), pltpu.VMEM((D_MODEL, EXPERT_FF), BF16),
                        pltpu.VMEM((EXPERT_FF, D_MODEL), BF16)],
    )
    return pl.pallas_call(
        _experts_kernel,
        grid_spec=grid_spec,
        out_shape=jax.ShapeDtypeStruct((n_tiles * ROW_TILE, HALF_D), I32),
        compiler_params=pltpu.CompilerParams(dimension_semantics=("arbitrary",),
                                             vmem_limit_bytes=VMEM_LIMIT),
        name="experts",
    )(owner, valid, used, xs, wg, wu, wd)


def _ple_kernel(x1_ref, g1_ref, g2_ref, route_ref, p_ref, wp_ref, pg_ref, gg_ref, wgate_ref, o_ref):
    route = route_ref[...]
    lane = lax.broadcasted_iota(jnp.int32, route.shape, 1)
    pick = lambda ln: jnp.sum(jnp.where(lane == ln, route, 0.0), axis=-1, keepdims=True)
    c1 = pick(LANE_C1)
    c2 = pick(LANE_C2)
    a_hi, a_lo = _unpack_halves(g1_ref[...])
    b_hi, b_lo = _unpack_halves(g2_ref[...])
    x = x1_ref[...] + jnp.concatenate([c1 * a_hi + c2 * b_hi, c1 * a_lo + c2 * b_lo], axis=1)
    ple = _rms(_dot(p_ref[...].astype(BF16), wp_ref[...]), pg_ref[...])
    gate = _sigmoid(_dot(_rms(x, gg_ref[...]).astype(BF16), wgate_ref[...]))
    o_ref[...] = x + gate * ple


def _ple(x1, g1, g2, route, p2d, w_ple, ple_g, gate_g, w_gate):
    t = x1.shape[0]
    row = lambda w: pl.BlockSpec((PLE_TM, w), lambda i: (i, 0))
    full = lambda a: pl.BlockSpec(a.shape, lambda i: (0,) * a.ndim)
    return pl.pallas_call(
        _ple_kernel,
        grid=(t // PLE_TM,),
        in_specs=[row(D_MODEL), row(HALF_D), row(HALF_D), row(ROUTE_LANES), row(PLE_DIM),
                  full(w_ple), full(ple_g), full(gate_g), full(w_gate)],
        out_specs=row(D_MODEL),
        out_shape=jax.ShapeDtypeStruct((t, D_MODEL), F32),
        compiler_params=pltpu.CompilerParams(dimension_semantics=("parallel",),
                                             vmem_limit_bytes=VMEM_LIMIT),
        name="ple",
    )(x1, g1, g2, route, p2d, w_ple, ple_g, gate_g, w_gate)


def kernel(x, p, mix_norm_g, w_in, ret_gn_g, ret_gn_b, w_ret_o, q_norm_g, k_norm_g, attn_sinks,
           w_swa_o, w_out, ffn_norm_g, w_router_group, b_router_group, w_router_expert,
           b_router_expert, w_exp_gate, w_exp_up, w_exp_down, ple_gate_norm_g, w_ple_gate,
           w_ple, ple_norm_g):
    batch, seq, d = x.shape
    t = batch * seq
    depth = w_in.shape[0]
    n_tiles = 2 * t // ROW_TILE + N_EXPERTS
    row = lambda a: a.reshape(1, -1)
    x2d = x.reshape(t, d)
    for i in range(depth):
        pad = ROUTE_LANES - N_GROUPS - N_EXPERTS
        w_router = jnp.pad(jnp.concatenate([w_router_expert[i], w_router_group[i]], axis=1), ((0, 0), (0, pad)))
        b_router = jnp.pad(jnp.concatenate([b_router_expert[i], b_router_group[i]]), (0, pad)).reshape(1, -1)
        w_router_hi = w_router.astype(BF16)
        w_router = jnp.concatenate([w_router_hi, (w_router - w_router_hi.astype(F32)).astype(BF16)], axis=1)
        x1, hp, route, cnt = _block(x2d, row(mix_norm_g[i]), w_in[i].astype(BF16), row(ret_gn_g[i]),
                                    row(ret_gn_b[i]), attn_sinks[i], q_norm_g[i], k_norm_g[i],
                                    w_ret_o[i].astype(BF16), w_swa_o[i].astype(BF16), w_out[i].astype(BF16),
                                    row(ffn_norm_g[i]), w_router, b_router, seq)
        base, owner, valid, used = _tile_plan(cnt, n_tiles)
        pos = _route_pos(route, base)
        pos1 = pos[:, 0, :]
        pos2 = pos[:, 1, :]
        xs = _dispatch(hp, pos1, pos2, n_tiles * ROW_TILE)
        ys = _experts(xs, owner, valid, used, w_exp_gate[i], w_exp_up[i], w_exp_down[i])
        g1, g2 = _combine_gather(ys, pos1, pos2, t)
        x2d = _ple(x1, g1, g2, route, p[i].reshape(t, PLE_DIM), w_ple[i].astype(BF16), row(ple_norm_g[i]),
                   row(ple_gate_norm_g[i]), w_ple_gate[i].astype(BF16))
    return x2d.reshape(batch, seq, d)
```

```python
import functools

import jax
import jax.numpy as jnp
import numpy as np
from jax import lax
from jax.experimental import pallas as pl
from jax.experimental.pallas import tpu as pltpu
from jax.experimental.pallas import tpu_sc as plsc

F32 = jnp.float32
BF16 = jnp.bfloat16
I32 = jnp.int32

EPS = 1e-6
D_MODEL = 1024
PLE_DIM = 256
RET_HEADS = 4
RET_DK = 128
RET_DV = 128
RET_CHUNK = 128
SWA_HEADS = 8
SWA_KV_HEADS = 2
SWA_GROUP = SWA_HEADS // SWA_KV_HEADS
SWA_HD = 64
WINDOW = 128
N_GROUPS = 4
EXPERTS_PER_GROUP = 8
N_EXPERTS = N_GROUPS * EXPERTS_PER_GROUP
EXPERT_FF = 256

RET_W = RET_HEADS * RET_DK
SWA_Q = SWA_HEADS * SWA_HD
SWA_KV = SWA_KV_HEADS * SWA_HD
Z_RET_W = 4 * RET_W
Z_SKV_W = 2 * SWA_KV
Z_GATE_W = 2 * D_MODEL
COL_SQ = Z_RET_W
COL_SKV = COL_SQ + SWA_Q
COL_GATE = COL_SKV + Z_SKV_W
IN_WIDTH = COL_GATE + Z_GATE_W

ROUTE_LANES = 128
LANE_GROUP0 = N_EXPERTS
LANE_E1, LANE_E2, LANE_C1, LANE_C2 = 32, 33, 34, 35
NEG_INF = -1e30

HALF_D = D_MODEL // 2
ROW_TILE = 512
SC_CORES = 2
SC_SUBCORES = 16
SC_WORKERS = SC_CORES * SC_SUBCORES
SC_CHUNK = 128

VMEM_LIMIT = 56 * 1024 * 1024

BLOCK_ROWS = 512
W_SLAB = 256
RET_ROWS = 512
MERGE_SUB = 256
ROUTE_TM = 512
PLE_TM = 512


def _rms(x, g):
    ms = jnp.mean(x * x, axis=-1, keepdims=True)
    return x * lax.rsqrt(ms + EPS) * g


def _sigmoid(x):
    return 1.0 / (1.0 + jnp.exp(-x))


def _dot(a, b):
    return jnp.dot(a, b, preferred_element_type=F32)


def _dot_nt(a, b):
    return lax.dot_general(a, b, (((1,), (1,)), ((), ())), preferred_element_type=F32)


def _dot_tn(a, b):
    return lax.dot_general(a, b, (((0,), (0,)), ((), ())), preferred_element_type=F32)


def _pack_halves(v):
    return pltpu.pack_elementwise([v[:, :HALF_D], v[:, HALF_D:]], packed_dtype=BF16)


def _unpack_halves(p):
    return tuple(pltpu.unpack_elementwise(p, index=k, packed_dtype=BF16, unpacked_dtype=F32) for k in range(2))


def _retention_tables():
    f32 = np.float32
    h = RET_HEADS
    c = RET_CHUNK
    log_gamma = np.log1p(-np.exp2(f32(-5.0) - np.arange(h, dtype=f32))).astype(f32)
    pos = np.arange(c, dtype=f32)
    diff = pos[:, None] - pos[None, :]
    decay = np.where(diff[None] >= 0.0, np.exp(np.maximum(diff, f32(0.0))[None] * log_gamma[:, None, None]), f32(0.0))
    scale = f32(RET_DK ** -0.5)
    dmask = decay * scale
    zeta = np.exp((f32(c - 1.0) - pos)[None, :] * log_gamma[:, None]) * scale
    xi = np.exp((pos + f32(1.0))[None, :] * log_gamma[:, None])
    cdec = np.exp(f32(c) * log_gamma)
    bc = lambda a: np.ascontiguousarray(np.broadcast_to(a[:, :, None], (h, c, c)), dtype=f32)
    return (dmask.astype(f32), bc(zeta), bc(xi),
            np.ascontiguousarray(np.broadcast_to(cdec[:, None, None], (h, c, c)), dtype=f32))


def _swa_tables(q_g, k_g):
    f32 = np.float32
    qi = np.arange(WINDOW)[:, None]
    sj = np.arange(2 * WINDOW)[None, :]
    rel = qi + WINDOW - sj
    ok = (rel >= 0) & (rel < WINDOW)
    slopes = np.exp2(f32(-8.0) * np.arange(1, SWA_HEADS + 1, dtype=f32) / f32(SWA_HEADS)).astype(f32)
    bias = np.where(ok[None], -slopes[:, None, None] * rel.astype(f32)[None], f32(NEG_INF)).astype(f32)
    bias = bias.reshape(SWA_KV_HEADS, SWA_GROUP * WINDOW, 2 * WINDOW)
    avg = lambda heads: jnp.asarray(np.kron(np.eye(heads, dtype=f32), np.full((SWA_HD, SWA_HD), 1.0 / SWA_HD, f32)),
                                    dtype=BF16)
    qg = jnp.tile(q_g.reshape(1, SWA_HD), (1, SWA_HEADS)) * (SWA_HD ** -0.5)
    kg = jnp.tile(k_g.reshape(1, SWA_HD), (1, SWA_KV_HEADS))
    return qg, kg, avg(SWA_HEADS), avg(SWA_KV_HEADS), jnp.asarray(bias)


def _block_kernel(per_seq, sinks_ref, x_ref, xp_ref, g_ref, w_hbm, dmask_ref, zeta_ref, xi_ref, cdec_ref,
                  gng_ref, gnb_ref, qg_ref, kg_ref, bq_ref, bk_ref, bias_ref,
                  wro_ref, wso_ref, wout_ref, fg_ref, wr_ref, br_ref,
                  x1_ref, hp_ref, route_ref, cnt_ref,
                  w_ref, wtmp_ref, w_sem, zc_ref, zn_ref, yr_ref, ys_ref, state_ref, prev_ref, band_ref, tail_ref):
    g = pl.program_id(0)
    seq_start = lax.rem(g - 1, per_seq) == 0

    @pl.when(g == 0)
    def _():
        n_slabs = IN_WIDTH // W_SLAB
        slab = lambda j: pltpu.make_async_copy(w_hbm.at[:, pl.ds(j * W_SLAB, W_SLAB)], wtmp_ref.at[j % 2],
                                               w_sem.at[j % 2])
        slab(0).start()
        for j in range(n_slabs):
            if j + 1 < n_slabs:
                slab(j + 1).start()
            slab(j).wait()
            w_ref[:, j * W_SLAB:(j + 1) * W_SLAB] = wtmp_ref[j % 2].astype(BF16)
        zc_ref[...] = jnp.zeros_like(zc_ref)
        tail_ref[...] = jnp.zeros_like(tail_ref)
        state_ref[...] = jnp.zeros_like(state_ref)

    @pl.when(seq_start)
    def _():
        state_ref[...] = jnp.zeros_like(state_ref)

    band_ref[:WINDOW, :] = tail_ref[...]
    band_ref[WINDOW:, :] = zc_ref[:, COL_SKV:COL_GATE]

    h = _rms(x_ref[...], g_ref[...]).astype(BF16)

    def project(c0, width):
        zn_ref[:, c0:c0 + width] = _dot(h, w_ref[:, c0:c0 + width]).astype(BF16)

    cw = 512
    proj_pieces = [functools.partial(project, j * cw, cw) for j in range(COL_SKV // cw)]
    proj_pieces.append(functools.partial(project, COL_SKV, Z_SKV_W))
    proj_pieces += [functools.partial(project, COL_GATE + j * cw, cw) for j in range(Z_GATE_W // cw)]

    def retention_rows(t):
        r_base = t * RET_ROWS
        n_chunks = RET_ROWS // RET_CHUNK
        part = lambda c, p, hd: zc_ref[r_base + c * RET_CHUNK:r_base + (c + 1) * RET_CHUNK,
                                       p * RET_W + hd * RET_DK:p * RET_W + (hd + 1) * RET_DK]
        for hd in range(RET_HEADS):
            state = state_ref[hd]
            for c in range(n_chunks):
                prev_ref[t * n_chunks + c, hd] = state.astype(BF16)
                kz = (part(c, 1, hd).astype(F32) * zeta_ref[hd]).astype(BF16)
                state = cdec_ref[hd] * state + _dot_tn(kz, part(c, 2, hd))
            state_ref[hd] = state
        out_rows = []
        for c in range(n_chunks):
            heads = []
            for hd in range(RET_HEADS):
                q = part(c, 0, hd)
                scores = _dot_nt(q, part(c, 1, hd)) * dmask_ref[hd]
                y = _dot(scores.astype(BF16), part(c, 2, hd)) + _dot(q, prev_ref[t * n_chunks + c, hd]) * xi_ref[hd]
                mu = jnp.mean(y, axis=-1, keepdims=True)
                d = y - mu
                var = jnp.mean(d * d, axis=-1, keepdims=True)
                hs = slice(hd * RET_DV, (hd + 1) * RET_DV)
                yn = d * lax.rsqrt(var + EPS) * gng_ref[:, hs] + gnb_ref[:, hs]
                gate = part(c, 3, hd).astype(F32)
                heads.append((gate * _sigmoid(gate) * yn).astype(BF16))
            out_rows.append(jnp.concatenate(heads, axis=1))
        yr_ref[r_base:r_base + RET_ROWS, :] = jnp.concatenate(out_rows, axis=0)

    group_rows = SWA_GROUP * WINDOW
    lo_q = lax.broadcasted_iota(jnp.int32, (WINDOW, 2 * SWA_HD), 1) < SWA_HD
    lo_k = lax.broadcasted_iota(jnp.int32, (2 * WINDOW, 2 * SWA_HD), 1) < SWA_HD
    in_prev = lax.broadcasted_iota(jnp.int32, (group_rows, 2 * WINDOW), 1) < WINDOW
    row_head = lax.broadcasted_iota(jnp.int32, (group_rows, 1), 0) // WINDOW

    def both_heads(v):
        swapped = pltpu.roll(v, SWA_HD, axis=1)
        return jnp.where(lo_k, v, swapped), jnp.where(lo_k, swapped, v)

    def swa_block(n):
        rows = slice(n * WINDOW, (n + 1) * WINDOW)
        kv = band_ref[n * WINDOW:(n + 2) * WINDOW, :]
        kf = kv[:, :SWA_KV].astype(F32)
        kn = kf * lax.rsqrt(_dot((kf * kf).astype(BF16), bk_ref[...]) + EPS) * kg_ref[...]
        keys = [a.astype(BF16) for a in both_heads(kn)]
        vals = [a.astype(BF16) for a in both_heads(kv[:, SWA_KV:].astype(F32))]
        qf = zc_ref[rows, COL_SQ:COL_SKV].astype(F32)
        qn = qf * lax.rsqrt(_dot((qf * qf).astype(BF16), bq_ref[...]) + EPS) * qg_ref[...]
        outs = []
        for kh in range(SWA_KV_HEADS):
            parts = []
            for j in range(SWA_GROUP // 2):
                c0 = (kh * SWA_GROUP + 2 * j) * SWA_HD
                two = qn[:, c0:c0 + 2 * SWA_HD]
                parts += [jnp.where(lo_q, two, 0.0), jnp.where(lo_q, 0.0, two)]
            qs = jnp.concatenate(parts, axis=0).astype(BF16)
            s = _dot_nt(qs, keys[kh]) + bias_ref[kh]
            if n == 0:
                s = jnp.where(jnp.logical_and(seq_start, in_prev), NEG_INF, s)
            sink = sinks_ref[kh * SWA_GROUP]
            for gi in range(1, SWA_GROUP):
                sink = jnp.where(row_head == gi, sinks_ref[kh * SWA_GROUP + gi], sink)
            m = jnp.maximum(jnp.max(s, axis=-1, keepdims=True), sink)
            p = jnp.exp(s - m)
            denom = jnp.sum(p, axis=-1, keepdims=True) + jnp.exp(sink - m)
            o = _dot(p.astype(BF16), vals[kh]) * (1.0 / denom)
            for j in range(SWA_GROUP // 2):
                even = o[(2 * j) * WINDOW:(2 * j + 1) * WINDOW]
                odd = o[(2 * j + 1) * WINDOW:(2 * j + 2) * WINDOW]
                outs.append(jnp.where(lo_q, even, odd).astype(BF16))
        ys_ref[rows, :] = jnp.concatenate(outs, axis=1)

    mix_pieces = [functools.partial(retention_rows, t) for t in range(BLOCK_ROWS // RET_ROWS)]
    mix_pieces += [functools.partial(swa_block, n) for n in range(BLOCK_ROWS // WINDOW)]

    for k in range(max(len(proj_pieces), len(mix_pieces))):
        if k < len(proj_pieces):
            proj_pieces[k]()
        if k < len(mix_pieces):
            mix_pieces[k]()

    subs = [_merge_rows(slice(s * MERGE_SUB, (s + 1) * MERGE_SUB), yr_ref, ys_ref, zc_ref, xp_ref, wro_ref,
                        wso_ref, wout_ref, fg_ref, wr_ref, br_ref) for s in range(BLOCK_ROWS // MERGE_SUB)]
    x1_ref[...] = jnp.concatenate([s[0] for s in subs], axis=0)
    hp_ref[...] = jnp.concatenate([s[1] for s in subs], axis=0)
    route_ref[...] = jnp.concatenate([s[2] for s in subs], axis=0)
    cnt_ref[...] = sum(s[3] for s in subs)

    tail_ref[...] = zc_ref[BLOCK_ROWS - WINDOW:, COL_SKV:COL_GATE]
    zc_ref[...] = zn_ref[...]


def _merge_rows(rows, yr_ref, ys_ref, z_ref, x_ref, wro_ref, wso_ref, wout_ref, fg_ref, wr_ref, br_ref):
    a = _dot(yr_ref[rows, :], wro_ref[...])
    b = _dot(ys_ref[rows, :], wso_ref[...])
    gate_r = z_ref[rows, COL_GATE:COL_GATE + D_MODEL].astype(F32)
    gate_s = z_ref[rows, COL_GATE + D_MODEL:].astype(F32)
    merged = (_sigmoid(gate_r) * a + _sigmoid(gate_s) * b).astype(BF16)
    x1 = x_ref[rows, :] + _dot(merged, wout_ref[...])
    h2 = _rms(x1, fg_ref[...])
    packed = _pack_halves(h2)

    h_hi = h2.astype(BF16)
    h_lo = (h2 - h_hi.astype(F32)).astype(BF16)
    parts = _dot(h_hi, wr_ref[...]) + _dot(h_lo, wr_ref[...])
    logits = parts[:, :ROUTE_LANES] + parts[:, ROUTE_LANES:] + br_ref[...]
    lane = lax.broadcasted_iota(jnp.int32, logits.shape, 1)
    big = jnp.int32(ROUTE_LANES)
    is_group = jnp.logical_and(lane >= LANE_GROUP0, lane < LANE_GROUP0 + N_GROUPS)
    gl = jnp.where(is_group, logits, NEG_INF)
    gmax = jnp.max(gl, axis=-1, keepdims=True)
    g_w = 1.0 / jnp.sum(jnp.exp(gl - gmax), axis=-1, keepdims=True)
    g_sel = jnp.min(jnp.where(gl == gmax, lane, big), axis=-1, keepdims=True) - LANE_GROUP0
    in_group = jnp.logical_and(lane < N_EXPERTS, (lane >> 3) == g_sel)
    el = jnp.where(in_group, logits, NEG_INF)
    m1 = jnp.max(el, axis=-1, keepdims=True)
    i1 = jnp.min(jnp.where(el == m1, lane, big), axis=-1, keepdims=True)
    el2 = jnp.where(lane == i1, NEG_INF, el)
    m2 = jnp.max(el2, axis=-1, keepdims=True)
    i2 = jnp.min(jnp.where(el2 == m2, lane, big), axis=-1, keepdims=True)
    e2 = jnp.exp(m2 - m1)
    c1 = g_w / (1.0 + e2)
    c2 = g_w * e2 / (1.0 + e2)
    chosen = jnp.where(jnp.logical_or(lane == i1, lane == i2), 1.0, 0.0)
    route = (chosen
             + jnp.where(lane == LANE_E1, i1.astype(F32), 0.0)
             + jnp.where(lane == LANE_E2, i2.astype(F32), 0.0)
             + jnp.where(lane == LANE_C1, c1, 0.0)
             + jnp.where(lane == LANE_C2, c2, 0.0))
    return x1, packed, route, jnp.sum(chosen, axis=0, keepdims=True)


def _block(x2d, mix_g, w_in, gn_g, gn_b, sinks, q_g, k_g, w_ret_o, w_swa_o, w_out, ffn_g, w_router, b_router,
           seq):
    t = x2d.shape[0]
    n = t // BLOCK_ROWS
    assert seq % BLOCK_ROWS == 0
    tables = _retention_tables()
    swa_tables = _swa_tables(q_g, k_g)
    consts = tables + (gn_g, gn_b) + swa_tables + (w_ret_o, w_swa_o, w_out, ffn_g, w_router, b_router)
    full = lambda a: pl.BlockSpec(a.shape, lambda g: (0,) * a.ndim)
    cur = lambda w: pl.BlockSpec((BLOCK_ROWS, w), lambda g: (jnp.minimum(g, n - 1), 0))
    prv = lambda w: pl.BlockSpec((BLOCK_ROWS, w), lambda g: (jnp.maximum(g - 1, 0), 0))
    chunks = BLOCK_ROWS // RET_CHUNK
    return pl.pallas_call(
        functools.partial(_block_kernel, seq // BLOCK_ROWS),
        grid=(n + 1,),
        in_specs=[pl.BlockSpec(memory_space=pltpu.SMEM), cur(D_MODEL), prv(D_MODEL), full(mix_g),
                  pl.BlockSpec(memory_space=pl.ANY)] + [full(a) for a in consts],
        out_specs=[prv(D_MODEL), prv(HALF_D), prv(ROUTE_LANES),
                   pl.BlockSpec((None, 1, ROUTE_LANES), lambda g: (jnp.maximum(g - 1, 0), 0, 0))],
        out_shape=[jax.ShapeDtypeStruct((t, D_MODEL), F32),
                   jax.ShapeDtypeStruct((t, HALF_D), I32),
                   jax.ShapeDtypeStruct((t, ROUTE_LANES), F32),
                   jax.ShapeDtypeStruct((n, 1, ROUTE_LANES), F32)],
        scratch_shapes=[pltpu.VMEM(w_in.shape, BF16), pltpu.VMEM((2, D_MODEL, W_SLAB), F32),
                        pltpu.SemaphoreType.DMA((2,)),
                        pltpu.VMEM((BLOCK_ROWS, IN_WIDTH), BF16), pltpu.VMEM((BLOCK_ROWS, IN_WIDTH), BF16),
                        pltpu.VMEM((BLOCK_ROWS, RET_W), BF16), pltpu.VMEM((BLOCK_ROWS, SWA_Q), BF16),
                        pltpu.VMEM((RET_HEADS, RET_DK, RET_DV), F32),
                        pltpu.VMEM((chunks, RET_HEADS, RET_DK, RET_DV), BF16),
                        pltpu.VMEM((BLOCK_ROWS + WINDOW, Z_SKV_W), BF16), pltpu.VMEM((WINDOW, Z_SKV_W), BF16)],
        compiler_params=pltpu.CompilerParams(dimension_semantics=("arbitrary",),
                                             vmem_limit_bytes=VMEM_LIMIT),
        name="block",
    )(sinks, x2d, x2d, mix_g, w_in, *consts)


def _route_pos_kernel(route_ref, base_ref, pos_ref):
    route = route_ref[...]
    lane = lax.broadcasted_iota(jnp.int32, route.shape, 1)
    lane_f = lane.astype(F32)
    chosen = jnp.where(lane < N_EXPERTS, route, 0.0)
    r = lax.broadcasted_iota(jnp.int32, (ROUTE_TM, ROUTE_TM), 0)
    c = lax.broadcasted_iota(jnp.int32, (ROUTE_TM, ROUTE_TM), 1)
    tri = jnp.where(c <= r, 1.0, 0.0).astype(BF16)
    incl = _dot(tri, chosen.astype(BF16))
    posmat = base_ref[...] + incl - 1.0
    pick = lambda ln: jnp.sum(jnp.where(lane == ln, route, 0.0), axis=-1, keepdims=True)
    pos_of = lambda e: jnp.sum(jnp.where(lane_f == e, posmat, 0.0), axis=-1, keepdims=True)
    p1 = pos_of(pick(LANE_E1))
    p2 = pos_of(pick(LANE_E2))
    rec = jnp.where(lane == 0, p1, 0.0) + jnp.where(lane == 1, p2, 0.0)
    pos_ref[...] = rec.T[:8, :].astype(I32)


def _route_pos(route, base):
    t = route.shape[0]
    n = t // ROUTE_TM
    return pl.pallas_call(
        _route_pos_kernel,
        grid=(n,),
        in_specs=[pl.BlockSpec((ROUTE_TM, ROUTE_LANES), lambda i: (i, 0)),
                  pl.BlockSpec((None, 1, ROUTE_LANES), lambda i: (i, 0, 0))],
        out_specs=pl.BlockSpec((None, 8, ROUTE_TM), lambda i: (i, 0, 0)),
        out_shape=jax.ShapeDtypeStruct((n, 8, ROUTE_TM), I32),
        compiler_params=pltpu.CompilerParams(dimension_semantics=("parallel",),
                                             vmem_limit_bytes=VMEM_LIMIT),
        name="route_pos",
    )(route, base)


def _plan_kernel(cnt_ref, base_ref, owner_ref, valid_ref, used_ref):
    counts = cnt_ref[...]
    nt = counts.shape[0]
    exact = functools.partial(jnp.dot, precision=lax.Precision.HIGHEST, preferred_element_type=F32)
    total = jnp.sum(counts, axis=0, keepdims=True)
    tiles_e = jnp.ceil(total * (1.0 / ROW_TILE))
    k = lax.broadcasted_iota(jnp.int32, (ROUTE_LANES, ROUTE_LANES), 0)
    lane = lax.broadcasted_iota(jnp.int32, (ROUTE_LANES, ROUTE_LANES), 1)
    tile_end = exact(jnp.broadcast_to(tiles_e, (8, ROUTE_LANES)), jnp.where(k <= lane, 1.0, 0.0))[0:1]
    tile_start = tile_end - tiles_e
    r = lax.broadcasted_iota(jnp.int32, (nt, nt), 0)
    c = lax.broadcasted_iota(jnp.int32, (nt, nt), 1)
    before = exact(jnp.where(c < r, 1.0, 0.0), counts)
    base_ref[...] = tile_start * ROW_TILE + before

    used = jnp.max(tile_end, axis=-1, keepdims=True)
    tile = k.astype(F32)
    clamped = jnp.minimum(tile, jnp.maximum(used - 1.0, 0.0))
    is_expert = lane < N_EXPERTS
    owner = jnp.sum(jnp.where(jnp.logical_and(is_expert, tile_end <= clamped), 1.0, 0.0), axis=-1, keepdims=True)
    owner = jnp.minimum(owner, N_EXPERTS - 1.0)
    mine = lane.astype(F32) == owner
    pick = lambda v: jnp.sum(jnp.where(mine, v, 0.0), axis=-1, keepdims=True)
    t1 = tile[:, 0:1]
    valid = jnp.clip(pick(total) - (t1 - pick(tile_start)) * ROW_TILE, 0.0, float(ROW_TILE))
    valid = jnp.where(t1 < used, valid, 0.0)
    owner_ref[...] = jnp.broadcast_to(owner, owner_ref.shape).astype(I32)
    valid_ref[...] = jnp.broadcast_to(valid, valid_ref.shape).astype(I32)
    used_ref[...] = jnp.broadcast_to(used, used_ref.shape).astype(I32)


def _tile_plan(cnt, n_tiles):
    assert n_tiles <= ROUTE_LANES
    nt = cnt.shape[0]
    sq = (ROUTE_LANES, ROUTE_LANES)
    base, owner, valid, used = pl.pallas_call(
        _plan_kernel,
        out_shape=[jax.ShapeDtypeStruct((nt, ROUTE_LANES), F32), jax.ShapeDtypeStruct(sq, I32),
                   jax.ShapeDtypeStruct(sq, I32), jax.ShapeDtypeStruct((8, ROUTE_LANES), I32)],
        name="tile_plan",
    )(cnt.reshape(nt, ROUTE_LANES))
    return base.reshape(cnt.shape), owner[:n_tiles, 0], valid[:n_tiles, 0], used[0, :1]


def _sc_mesh():
    return plsc.VectorSubcoreMesh(core_axis_name="c", subcore_axis_name="s")


def _sc_worker():
    return lax.axis_index("s") * SC_CORES + lax.axis_index("c")


def _dispatch(hp, pos1, pos2, n_rows):
    t = hp.shape[0]
    per_worker = t // SC_WORKERS
    k = per_worker // SC_CHUNK
    idx = pltpu.VMEM((k, SC_CHUNK), I32)

    @functools.partial(pl.kernel, mesh=_sc_mesh(), out_type=jax.ShapeDtypeStruct((n_rows, HALF_D), I32),
                       scratch_types=[idx, idx, pltpu.VMEM((SC_CHUNK, HALF_D), I32)], name="moe_dispatch")
    def run(h_hbm, p1_hbm, p2_hbm, xs_hbm, p1_v, p2_v, rows_v):
        wid = _sc_worker()
        pltpu.sync_copy(p1_hbm.at[wid], p1_v)
        pltpu.sync_copy(p2_hbm.at[wid], p2_v)
        for j in range(k):
            pltpu.sync_copy(h_hbm.at[pl.ds(wid * per_worker + j * SC_CHUNK, SC_CHUNK)], rows_v)
            pltpu.sync_copy(rows_v, xs_hbm.at[p1_v.at[j]])
            pltpu.sync_copy(rows_v, xs_hbm.at[p2_v.at[j]])

    return run(hp, pos1.reshape(SC_WORKERS, k, SC_CHUNK), pos2.reshape(SC_WORKERS, k, SC_CHUNK))


def _combine_gather(ys, pos1, pos2, t):
    per_worker = t // SC_WORKERS
    k = per_worker // SC_CHUNK
    idx = pltpu.VMEM((k, SC_CHUNK), I32)
    out = jax.ShapeDtypeStruct((t, HALF_D), I32)

    @functools.partial(pl.kernel, mesh=_sc_mesh(), out_type=(out, out),
                       scratch_types=[idx, idx, pltpu.VMEM((SC_CHUNK, HALF_D), I32)], name="moe_combine")
    def run(ys_hbm, p1_hbm, p2_hbm, g1_hbm, g2_hbm, p1_v, p2_v, rows_v):
        wid = _sc_worker()
        pltpu.sync_copy(p1_hbm.at[wid], p1_v)
        pltpu.sync_copy(p2_hbm.at[wid], p2_v)
        for j in range(k):
            dst = pl.ds(wid * per_worker + j * SC_CHUNK, SC_CHUNK)
            pltpu.sync_copy(ys_hbm.at[p1_v.at[j]], rows_v)
            pltpu.sync_copy(rows_v, g1_hbm.at[dst])
            pltpu.sync_copy(ys_hbm.at[p2_v.at[j]], rows_v)
            pltpu.sync_copy(rows_v, g2_hbm.at[dst])

    return run(ys, pos1.reshape(SC_WORKERS, k, SC_CHUNK), pos2.reshape(SC_WORKERS, k, SC_CHUNK))


def _experts_kernel(owner_ref, valid_ref, used_ref, xs_ref, wg_ref, wu_ref, wd_ref, ys_ref, wg_bf, wu_bf, wd_bf):
    i = pl.program_id(0)
    valid = valid_ref[i]

    @pl.when(jnp.logical_or(i == 0, owner_ref[i] != owner_ref[jnp.maximum(i - 1, 0)]))
    def _():
        wg_bf[...] = wg_ref[...].astype(BF16)
        wu_bf[...] = wu_ref[...].astype(BF16)
        wd_bf[...] = wd_ref[...].astype(BF16)

    @pl.when(i < used_ref[0])
    def _():
        row = lax.broadcasted_iota(jnp.int32, (ROW_TILE, HALF_D), 0)
        packed = jnp.where(row < valid, xs_ref[...], 0)
        hi, lo = _unpack_halves(packed)
        hi = hi.astype(BF16)
        lo = lo.astype(BF16)
        hg = _dot(hi, wg_bf[:HALF_D, :]) + _dot(lo, wg_bf[HALF_D:, :])
        hu = _dot(hi, wu_bf[:HALF_D, :]) + _dot(lo, wu_bf[HALF_D:, :])
        hid = (hg * _sigmoid(hg) * hu).astype(BF16)
        ys_ref[...] = _pack_halves(_dot(hid, wd_bf[...]))


def _experts(xs, owner, valid, used, wg, wu, wd):
    n_tiles = owner.shape[0]
    tile = lambda i, own, val, used: (jnp.minimum(i, used[0] - 1), 0)
    grid_spec = pltpu.PrefetchScalarGridSpec(
        num_scalar_prefetch=3,
        grid=(n_tiles,),
        in_specs=[pl.BlockSpec((ROW_TILE, HALF_D), tile),
                  pl.BlockSpec((None, D_MODEL, EXPERT_FF), lambda i, own, val, used: (own[i], 0, 0)),
                  pl.BlockSpec((None, D_MODEL, EXPERT_FF), lambda i, own, val, used: (own[i], 0, 0)),
                  pl.BlockSpec((None, EXPERT_FF, D_MODEL), lambda i, own, val, used: (own[i], 0, 0))],
        out_specs=pl.BlockSpec((ROW_TILE, HALF_D), tile),
        scratch_shapes=[pltpu.VMEM((D_MODEL, EXPERT_FF), BF16), pltpu.VMEM((D_MODEL, EXPERT_FF), BF16),
                        pltpu.VMEM((EXPERT_FF, D_MODEL), BF16)],
    )
    return pl.pallas_call(
        _experts_kernel,
        grid_spec=grid_spec,
        out_shape=jax.ShapeDtypeStruct((n_tiles * ROW_TILE, HALF_D), I32),
        compiler_params=pltpu.CompilerParams(dimension_semantics=("arbitrary",),
                                             vmem_limit_bytes=VMEM_LIMIT),
        name="experts",
    )(owner, valid, used, xs, wg, wu, wd)


def _ple_kernel(x1_ref, g1_ref, g2_ref, route_ref, p_ref, wp_ref, pg_ref, gg_ref, wgate_ref, o_ref):
    route = route_ref[...]
    lane = lax.broadcasted_iota(jnp.int32, route.shape, 1)
    pick = lambda ln: jnp.sum(jnp.where(lane == ln, route, 0.0), axis=-1, keepdims=True)
    c1 = pick(LANE_C1)
    c2 = pick(LANE_C2)
    a_hi, a_lo = _unpack_halves(g1_ref[...])
    b_hi, b_lo = _unpack_halves(g2_ref[...])
    x = x1_ref[...] + jnp.concatenate([c1 * a_hi + c2 * b_hi, c1 * a_lo + c2 * b_lo], axis=1)
    ple = _rms(_dot(p_ref[...].astype(BF16), wp_ref[...]), pg_ref[...])
    gate = _sigmoid(_dot(_rms(x, gg_ref[...]).astype(BF16), wgate_ref[...]))
    o_ref[...] = x + gate * ple


def _ple(x1, g1, g2, route, p2d, w_ple, ple_g, gate_g, w_gate):
    t = x1.shape[0]
    row = lambda w: pl.BlockSpec((PLE_TM, w), lambda i: (i, 0))
    full = lambda a: pl.BlockSpec(a.shape, lambda i: (0,) * a.ndim)
    return pl.pallas_call(
        _ple_kernel,
        grid=(t // PLE_TM,),
        in_specs=[row(D_MODEL), row(HALF_D), row(HALF_D), row(ROUTE_LANES), row(PLE_DIM),
                  full(w_ple), full(ple_g), full(gate_g), full(w_gate)],
        out_specs=row(D_MODEL),
        out_shape=jax.ShapeDtypeStruct((t, D_MODEL), F32),
        compiler_params=pltpu.CompilerParams(dimension_semantics=("parallel",),
                                             vmem_limit_bytes=VMEM_LIMIT),
        name="ple",
    )(x1, g1, g2, route, p2d, w_ple, ple_g, gate_g, w_gate)


def kernel(x, p, mix_norm_g, w_in, ret_gn_g, ret_gn_b, w_ret_o, q_norm_g, k_norm_g, attn_sinks,
           w_swa_o, w_out, ffn_norm_g, w_router_group, b_router_group, w_router_expert,
           b_router_expert, w_exp_gate, w_exp_up, w_exp_down, ple_gate_norm_g, w_ple_gate,
           w_ple, ple_norm_g):
    batch, seq, d = x.shape
    t = batch * seq
    depth = w_in.shape[0]
    n_tiles = 2 * t // ROW_TILE + N_EXPERTS
    row = lambda a: a.reshape(1, -1)
    x2d = x.reshape(t, d)
    for i in range(depth):
        pad = ROUTE_LANES - N_GROUPS - N_EXPERTS
        w_router = jnp.pad(jnp.concatenate([w_router_expert[i], w_router_group[i]], axis=1), ((0, 0), (0, pad)))
        b_router = jnp.pad(jnp.concatenate([b_router_expert[i], b_router_group[i]]), (0, pad)).reshape(1, -1)
        w_router_hi = w_router.astype(BF16)
        w_router = jnp.concatenate([w_router_hi, (w_router - w_router_hi.astype(F32)).astype(BF16)], axis=1)
        x1, hp, route, cnt = _block(x2d, row(mix_norm_g[i]), w_in[i], row(ret_gn_g[i]),
                                    row(ret_gn_b[i]), attn_sinks[i], q_norm_g[i], k_norm_g[i],
                                    w_ret_o[i].astype(BF16), w_swa_o[i].astype(BF16), w_out[i].astype(BF16),
                                    row(ffn_norm_g[i]), w_router, b_router, seq)
        base, owner, valid, used = _tile_plan(cnt, n_tiles)
        pos = _route_pos(route, base)
        pos1 = pos[:, 0, :]
        pos2 = pos[:, 1, :]
        xs = _dispatch(hp, pos1, pos2, n_tiles * ROW_TILE)
        ys = _experts(xs, owner, valid, used, w_exp_gate[i], w_exp_up[i], w_exp_down[i])
        g1, g2 = _combine_gather(ys, pos1, pos2, t)
        x2d = _ple(x1, g1, g2, route, p[i].reshape(t, PLE_DIM), w_ple[i].astype(BF16), row(ple_norm_g[i]),
                   row(ple_gate_norm_g[i]), w_ple_gate[i].astype(BF16))
    return x2d.reshape(batch, seq, d)
```

```python
import functools

import jax
import jax.numpy as jnp
import numpy as np
from jax import lax
from jax.experimental import pallas as pl
from jax.experimental.pallas import tpu as pltpu
from jax.experimental.pallas import tpu_sc as plsc

F32 = jnp.float32
BF16 = jnp.bfloat16
I32 = jnp.int32

EPS = 1e-6
D_MODEL = 1024
PLE_DIM = 256
RET_HEADS = 4
RET_DK = 128
RET_DV = 128
RET_CHUNK = 128
SWA_HEADS = 8
SWA_KV_HEADS = 2
SWA_GROUP = SWA_HEADS // SWA_KV_HEADS
SWA_HD = 64
WINDOW = 128
N_GROUPS = 4
EXPERTS_PER_GROUP = 8
N_EXPERTS = N_GROUPS * EXPERTS_PER_GROUP
EXPERT_FF = 256

RET_W = RET_HEADS * RET_DK
SWA_Q = SWA_HEADS * SWA_HD
SWA_KV = SWA_KV_HEADS * SWA_HD
Z_RET_W = 4 * RET_W
Z_SKV_W = 2 * SWA_KV
Z_GATE_W = 2 * D_MODEL
COL_SQ = Z_RET_W
COL_SKV = COL_SQ + SWA_Q
COL_GATE = COL_SKV + Z_SKV_W
IN_WIDTH = COL_GATE + Z_GATE_W

ROUTE_LANES = 128
LANE_GROUP0 = N_EXPERTS
LANE_E1, LANE_E2, LANE_C1, LANE_C2 = 32, 33, 34, 35
NEG_INF = -1e30

HALF_D = D_MODEL // 2
ROW_TILE = 512
SC_CORES = 2
SC_SUBCORES = 16
SC_WORKERS = SC_CORES * SC_SUBCORES
SC_CHUNK = 128
COMBINE_PARTS = 2

VMEM_LIMIT = 56 * 1024 * 1024

BLOCK_ROWS = 512
W_SLAB = 256
RET_ROWS = 512
MERGE_SUB = 256
ROUTE_TM = 512
PLE_TM = 512


def _rms(x, g):
    ms = jnp.mean(x * x, axis=-1, keepdims=True)
    return x * lax.rsqrt(ms + EPS) * g


def _sigmoid(x):
    return 1.0 / (1.0 + jnp.exp(-x))


def _dot(a, b):
    return jnp.dot(a, b, preferred_element_type=F32)


def _dot_nt(a, b):
    return lax.dot_general(a, b, (((1,), (1,)), ((), ())), preferred_element_type=F32)


def _dot_tn(a, b):
    return lax.dot_general(a, b, (((0,), (0,)), ((), ())), preferred_element_type=F32)


def _pack_halves(v):
    return pltpu.pack_elementwise([v[:, :HALF_D], v[:, HALF_D:]], packed_dtype=BF16)


def _unpack_halves(p):
    return tuple(pltpu.unpack_elementwise(p, index=k, packed_dtype=BF16, unpacked_dtype=F32) for k in range(2))


def _retention_tables():
    f32 = np.float32
    h = RET_HEADS
    c = RET_CHUNK
    log_gamma = np.log1p(-np.exp2(f32(-5.0) - np.arange(h, dtype=f32))).astype(f32)
    pos = np.arange(c, dtype=f32)
    diff = pos[:, None] - pos[None, :]
    decay = np.where(diff[None] >= 0.0, np.exp(np.maximum(diff, f32(0.0))[None] * log_gamma[:, None, None]), f32(0.0))
    scale = f32(RET_DK ** -0.5)
    dmask = decay * scale
    zeta = np.exp((f32(c - 1.0) - pos)[None, :] * log_gamma[:, None]) * scale
    xi = np.exp((pos + f32(1.0))[None, :] * log_gamma[:, None])
    cdec = np.exp(f32(c) * log_gamma)
    bc = lambda a: np.ascontiguousarray(np.broadcast_to(a[:, :, None], (h, c, c)), dtype=f32)
    return (dmask.astype(f32), bc(zeta), bc(xi),
            np.ascontiguousarray(np.broadcast_to(cdec[:, None, None], (h, c, c)), dtype=f32))


def _swa_tables(q_g, k_g):
    f32 = np.float32
    qi = np.arange(WINDOW)[:, None]
    sj = np.arange(2 * WINDOW)[None, :]
    rel = qi + WINDOW - sj
    ok = (rel >= 0) & (rel < WINDOW)
    slopes = np.exp2(f32(-8.0) * np.arange(1, SWA_HEADS + 1, dtype=f32) / f32(SWA_HEADS)).astype(f32)
    bias = np.where(ok[None], -slopes[:, None, None] * rel.astype(f32)[None], f32(NEG_INF)).astype(f32)
    bias = bias.reshape(SWA_KV_HEADS, SWA_GROUP * WINDOW, 2 * WINDOW)
    avg = lambda heads: jnp.asarray(np.kron(np.eye(heads, dtype=f32), np.full((SWA_HD, SWA_HD), 1.0 / SWA_HD, f32)),
                                    dtype=BF16)
    qg = jnp.tile(q_g.reshape(1, SWA_HD), (1, SWA_HEADS)) * (SWA_HD ** -0.5)
    kg = jnp.tile(k_g.reshape(1, SWA_HD), (1, SWA_KV_HEADS))
    return qg, kg, avg(SWA_HEADS), avg(SWA_KV_HEADS), jnp.asarray(bias)


def _block_kernel(per_seq, sinks_ref, x_ref, xp_ref, g_ref, w_hbm, dmask_ref, zeta_ref, xi_ref, cdec_ref,
                  gng_ref, gnb_ref, qg_ref, kg_ref, bq_ref, bk_ref, bias_ref,
                  wro_ref, wso_ref, wout_ref, fg_ref, wr_ref, br_ref,
                  x1_ref, hp_ref, route_ref, cnt_ref,
                  w_ref, wtmp_ref, w_sem, zc_ref, zn_ref, yr_ref, ys_ref, state_ref, prev_ref, band_ref, tail_ref):
    g = pl.program_id(0)
    seq_start = lax.rem(g - 1, per_seq) == 0

    @pl.when(g == 0)
    def _():
        n_slabs = IN_WIDTH // W_SLAB
        slab = lambda j: pltpu.make_async_copy(w_hbm.at[:, pl.ds(j * W_SLAB, W_SLAB)], wtmp_ref.at[j % 2],
                                               w_sem.at[j % 2])
        slab(0).start()
        for j in range(n_slabs):
            if j + 1 < n_slabs:
                slab(j + 1).start()
            slab(j).wait()
            w_ref[:, j * W_SLAB:(j + 1) * W_SLAB] = wtmp_ref[j % 2].astype(BF16)
        zc_ref[...] = jnp.zeros_like(zc_ref)
        tail_ref[...] = jnp.zeros_like(tail_ref)
        state_ref[...] = jnp.zeros_like(state_ref)

    @pl.when(seq_start)
    def _():
        state_ref[...] = jnp.zeros_like(state_ref)

    band_ref[:WINDOW, :] = tail_ref[...]
    band_ref[WINDOW:, :] = zc_ref[:, COL_SKV:COL_GATE]

    h = _rms(x_ref[...], g_ref[...]).astype(BF16)

    def project(c0, width):
        zn_ref[:, c0:c0 + width] = _dot(h, w_ref[:, c0:c0 + width]).astype(BF16)

    cw = 512
    proj_pieces = [functools.partial(project, j * cw, cw) for j in range(COL_SKV // cw)]
    proj_pieces.append(functools.partial(project, COL_SKV, Z_SKV_W))
    proj_pieces += [functools.partial(project, COL_GATE + j * cw, cw) for j in range(Z_GATE_W // cw)]

    def retention_rows(t):
        r_base = t * RET_ROWS
        n_chunks = RET_ROWS // RET_CHUNK
        part = lambda c, p, hd: zc_ref[r_base + c * RET_CHUNK:r_base + (c + 1) * RET_CHUNK,
                                       p * RET_W + hd * RET_DK:p * RET_W + (hd + 1) * RET_DK]
        for hd in range(RET_HEADS):
            state = state_ref[hd]
            for c in range(n_chunks):
                prev_ref[t * n_chunks + c, hd] = state.astype(BF16)
                kz = (part(c, 1, hd).astype(F32) * zeta_ref[hd]).astype(BF16)
                state = cdec_ref[hd] * state + _dot_tn(kz, part(c, 2, hd))
            state_ref[hd] = state
        out_rows = []
        for c in range(n_chunks):
            heads = []
            for hd in range(RET_HEADS):
                q = part(c, 0, hd)
                scores = _dot_nt(q, part(c, 1, hd)) * dmask_ref[hd]
                y = _dot(scores.astype(BF16), part(c, 2, hd)) + _dot(q, prev_ref[t * n_chunks + c, hd]) * xi_ref[hd]
                mu = jnp.mean(y, axis=-1, keepdims=True)
                d = y - mu
                var = jnp.mean(d * d, axis=-1, keepdims=True)
                hs = slice(hd * RET_DV, (hd + 1) * RET_DV)
                yn = d * lax.rsqrt(var + EPS) * gng_ref[:, hs] + gnb_ref[:, hs]
                gate = part(c, 3, hd).astype(F32)
                heads.append((gate * _sigmoid(gate) * yn).astype(BF16))
            out_rows.append(jnp.concatenate(heads, axis=1))
        yr_ref[r_base:r_base + RET_ROWS, :] = jnp.concatenate(out_rows, axis=0)

    group_rows = SWA_GROUP * WINDOW
    lo_q = lax.broadcasted_iota(jnp.int32, (WINDOW, 2 * SWA_HD), 1) < SWA_HD
    lo_k = lax.broadcasted_iota(jnp.int32, (2 * WINDOW, 2 * SWA_HD), 1) < SWA_HD
    in_prev = lax.broadcasted_iota(jnp.int32, (group_rows, 2 * WINDOW), 1) < WINDOW
    row_head = lax.broadcasted_iota(jnp.int32, (group_rows, 1), 0) // WINDOW

    def both_heads(v):
        swapped = pltpu.roll(v, SWA_HD, axis=1)
        return jnp.where(lo_k, v, swapped), jnp.where(lo_k, swapped, v)

    def swa_block(n):
        rows = slice(n * WINDOW, (n + 1) * WINDOW)
        kv = band_ref[n * WINDOW:(n + 2) * WINDOW, :]
        kf = kv[:, :SWA_KV].astype(F32)
        kn = kf * lax.rsqrt(_dot((kf * kf).astype(BF16), bk_ref[...]) + EPS) * kg_ref[...]
        keys = [a.astype(BF16) for a in both_heads(kn)]
        vals = [a.astype(BF16) for a in both_heads(kv[:, SWA_KV:].astype(F32))]
        qf = zc_ref[rows, COL_SQ:COL_SKV].astype(F32)
        qn = qf * lax.rsqrt(_dot((qf * qf).astype(BF16), bq_ref[...]) + EPS) * qg_ref[...]
        outs = []
        for kh in range(SWA_KV_HEADS):
            parts = []
            for j in range(SWA_GROUP // 2):
                c0 = (kh * SWA_GROUP + 2 * j) * SWA_HD
                two = qn[:, c0:c0 + 2 * SWA_HD]
                parts += [jnp.where(lo_q, two, 0.0), jnp.where(lo_q, 0.0, two)]
            qs = jnp.concatenate(parts, axis=0).astype(BF16)
            s = _dot_nt(qs, keys[kh]) + bias_ref[kh]
            if n == 0:
                s = jnp.where(jnp.logical_and(seq_start, in_prev), NEG_INF, s)
            sink = sinks_ref[kh * SWA_GROUP]
            for gi in range(1, SWA_GROUP):
                sink = jnp.where(row_head == gi, sinks_ref[kh * SWA_GROUP + gi], sink)
            m = jnp.maximum(jnp.max(s, axis=-1, keepdims=True), sink)
            p = jnp.exp(s - m)
            denom = jnp.sum(p, axis=-1, keepdims=True) + jnp.exp(sink - m)
            o = _dot(p.astype(BF16), vals[kh]) * (1.0 / denom)
            for j in range(SWA_GROUP // 2):
                even = o[(2 * j) * WINDOW:(2 * j + 1) * WINDOW]
                odd = o[(2 * j + 1) * WINDOW:(2 * j + 2) * WINDOW]
                outs.append(jnp.where(lo_q, even, odd).astype(BF16))
        ys_ref[rows, :] = jnp.concatenate(outs, axis=1)

    mix_pieces = [functools.partial(retention_rows, t) for t in range(BLOCK_ROWS // RET_ROWS)]
    mix_pieces += [functools.partial(swa_block, n) for n in range(BLOCK_ROWS // WINDOW)]

    subs = []

    def merge_sub(s):
        subs.append(_merge_rows(slice(s * MERGE_SUB, (s + 1) * MERGE_SUB), yr_ref, ys_ref, zc_ref, xp_ref, wro_ref,
                                wso_ref, wout_ref, fg_ref, wr_ref, br_ref))

    other_pieces = mix_pieces + [functools.partial(merge_sub, s) for s in range(BLOCK_ROWS // MERGE_SUB)]

    n_other = len(other_pieces)
    for k, piece in enumerate(other_pieces):
        for proj_piece in proj_pieces[k * len(proj_pieces) // n_other:(k + 1) * len(proj_pieces) // n_other]:
            proj_piece()
        piece()
    x1_ref[...] = jnp.concatenate([s[0] for s in subs], axis=0)
    hp_ref[...] = jnp.concatenate([s[1] for s in subs], axis=0)
    route_ref[...] = jnp.concatenate([s[2] for s in subs], axis=0)
    cnt_ref[...] = sum(s[3] for s in subs)

    tail_ref[...] = zc_ref[BLOCK_ROWS - WINDOW:, COL_SKV:COL_GATE]
    zc_ref[...] = zn_ref[...]


def _merge_rows(rows, yr_ref, ys_ref, z_ref, x_ref, wro_ref, wso_ref, wout_ref, fg_ref, wr_ref, br_ref):
    a = _dot(yr_ref[rows, :], wro_ref[...])
    b = _dot(ys_ref[rows, :], wso_ref[...])
    gate_r = z_ref[rows, COL_GATE:COL_GATE + D_MODEL].astype(F32)
    gate_s = z_ref[rows, COL_GATE + D_MODEL:].astype(F32)
    merged = (_sigmoid(gate_r) * a + _sigmoid(gate_s) * b).astype(BF16)
    x1 = x_ref[rows, :] + _dot(merged, wout_ref[...])
    h2 = _rms(x1, fg_ref[...])
    packed = _pack_halves(h2)

    h_hi = h2.astype(BF16)
    h_lo = (h2 - h_hi.astype(F32)).astype(BF16)
    parts = _dot(h_hi, wr_ref[...]) + _dot(h_lo, wr_ref[...])
    logits = parts[:, :ROUTE_LANES] + parts[:, ROUTE_LANES:] + br_ref[...]
    lane = lax.broadcasted_iota(jnp.int32, logits.shape, 1)
    big = jnp.int32(ROUTE_LANES)
    is_group = jnp.logical_and(lane >= LANE_GROUP0, lane < LANE_GROUP0 + N_GROUPS)
    gl = jnp.where(is_group, logits, NEG_INF)
    gmax = jnp.max(gl, axis=-1, keepdims=True)
    g_w = 1.0 / jnp.sum(jnp.exp(gl - gmax), axis=-1, keepdims=True)
    g_sel = jnp.min(jnp.where(gl == gmax, lane, big), axis=-1, keepdims=True) - LANE_GROUP0
    in_group = jnp.logical_and(lane < N_EXPERTS, (lane >> 3) == g_sel)
    el = jnp.where(in_group, logits, NEG_INF)
    m1 = jnp.max(el, axis=-1, keepdims=True)
    i1 = jnp.min(jnp.where(el == m1, lane, big), axis=-1, keepdims=True)
    el2 = jnp.where(lane == i1, NEG_INF, el)
    m2 = jnp.max(el2, axis=-1, keepdims=True)
    i2 = jnp.min(jnp.where(el2 == m2, lane, big), axis=-1, keepdims=True)
    e2 = jnp.exp(m2 - m1)
    c1 = g_w / (1.0 + e2)
    c2 = g_w * e2 / (1.0 + e2)
    chosen = jnp.where(jnp.logical_or(lane == i1, lane == i2), 1.0, 0.0)
    route = (chosen
             + jnp.where(lane == LANE_E1, i1.astype(F32), 0.0)
             + jnp.where(lane == LANE_E2, i2.astype(F32), 0.0)
             + jnp.where(lane == LANE_C1, c1, 0.0)
             + jnp.where(lane == LANE_C2, c2, 0.0))
    return x1, packed, route, jnp.sum(chosen, axis=0, keepdims=True)


def _block(x2d, mix_g, w_in, gn_g, gn_b, sinks, q_g, k_g, w_ret_o, w_swa_o, w_out, ffn_g, w_router, b_router,
           seq):
    t = x2d.shape[0]
    n = t // BLOCK_ROWS
    assert seq % BLOCK_ROWS == 0
    tables = _retention_tables()
    swa_tables = _swa_tables(q_g, k_g)
    consts = tables + (gn_g, gn_b) + swa_tables + (w_ret_o, w_swa_o, w_out, ffn_g, w_router, b_router)
    full = lambda a: pl.BlockSpec(a.shape, lambda g: (0,) * a.ndim)
    cur = lambda w: pl.BlockSpec((BLOCK_ROWS, w), lambda g: (jnp.minimum(g, n - 1), 0))
    prv = lambda w: pl.BlockSpec((BLOCK_ROWS, w), lambda g: (jnp.maximum(g - 1, 0), 0))
    chunks = BLOCK_ROWS // RET_CHUNK
    return pl.pallas_call(
        functools.partial(_block_kernel, seq // BLOCK_ROWS),
        grid=(n + 1,),
        in_specs=[pl.BlockSpec(memory_space=pltpu.SMEM), cur(D_MODEL), prv(D_MODEL), full(mix_g),
                  pl.BlockSpec(memory_space=pl.ANY)] + [full(a) for a in consts],
        out_specs=[prv(D_MODEL), prv(HALF_D), prv(ROUTE_LANES),
                   pl.BlockSpec((None, 1, ROUTE_LANES), lambda g: (jnp.maximum(g - 1, 0), 0, 0))],
        out_shape=[jax.ShapeDtypeStruct((t, D_MODEL), F32),
                   jax.ShapeDtypeStruct((t, HALF_D), I32),
                   jax.ShapeDtypeStruct((t, ROUTE_LANES), F32),
                   jax.ShapeDtypeStruct((n, 1, ROUTE_LANES), F32)],
        scratch_shapes=[pltpu.VMEM(w_in.shape, BF16), pltpu.VMEM((2, D_MODEL, W_SLAB), F32),
                        pltpu.SemaphoreType.DMA((2,)),
                        pltpu.VMEM((BLOCK_ROWS, IN_WIDTH), BF16), pltpu.VMEM((BLOCK_ROWS, IN_WIDTH), BF16),
                        pltpu.VMEM((BLOCK_ROWS, RET_W), BF16), pltpu.VMEM((BLOCK_ROWS, SWA_Q), BF16),
                        pltpu.VMEM((RET_HEADS, RET_DK, RET_DV), F32),
                        pltpu.VMEM((chunks, RET_HEADS, RET_DK, RET_DV), BF16),
                        pltpu.VMEM((BLOCK_ROWS + WINDOW, Z_SKV_W), BF16), pltpu.VMEM((WINDOW, Z_SKV_W), BF16)],
        compiler_params=pltpu.CompilerParams(dimension_semantics=("arbitrary",),
                                             vmem_limit_bytes=VMEM_LIMIT),
        name="block",
    )(sinks, x2d, x2d, mix_g, w_in, *consts)


def _route_pos_kernel(route_ref, base_ref, pos_ref):
    route = route_ref[...]
    lane = lax.broadcasted_iota(jnp.int32, route.shape, 1)
    lane_f = lane.astype(F32)
    chosen = jnp.where(lane < N_EXPERTS, route, 0.0)
    r = lax.broadcasted_iota(jnp.int32, (ROUTE_TM, ROUTE_TM), 0)
    c = lax.broadcasted_iota(jnp.int32, (ROUTE_TM, ROUTE_TM), 1)
    tri = jnp.where(c <= r, 1.0, 0.0).astype(BF16)
    incl = _dot(tri, chosen.astype(BF16))
    posmat = base_ref[...] + incl - 1.0
    pick = lambda ln: jnp.sum(jnp.where(lane == ln, route, 0.0), axis=-1, keepdims=True)
    pos_of = lambda e: jnp.sum(jnp.where(lane_f == e, posmat, 0.0), axis=-1, keepdims=True)
    p1 = pos_of(pick(LANE_E1))
    p2 = pos_of(pick(LANE_E2))
    rec = jnp.where(lane == 0, p1, 0.0) + jnp.where(lane == 1, p2, 0.0)
    pos_ref[...] = rec.T[:8, :].astype(I32)


def _route_pos(route, base):
    t = route.shape[0]
    n = t // ROUTE_TM
    return pl.pallas_call(
        _route_pos_kernel,
        grid=(n,),
        in_specs=[pl.BlockSpec((ROUTE_TM, ROUTE_LANES), lambda i: (i, 0)),
                  pl.BlockSpec((None, 1, ROUTE_LANES), lambda i: (i, 0, 0))],
        out_specs=pl.BlockSpec((None, 8, ROUTE_TM), lambda i: (i, 0, 0)),
        out_shape=jax.ShapeDtypeStruct((n, 8, ROUTE_TM), I32),
        compiler_params=pltpu.CompilerParams(dimension_semantics=("parallel",),
                                             vmem_limit_bytes=VMEM_LIMIT),
        name="route_pos",
    )(route, base)


def _plan_kernel(cnt_ref, base_ref, owner_ref, valid_ref, used_ref):
    counts = cnt_ref[...]
    nt = counts.shape[0]
    exact = functools.partial(jnp.dot, precision=lax.Precision.HIGHEST, preferred_element_type=F32)
    total = jnp.sum(counts, axis=0, keepdims=True)
    tiles_e = jnp.ceil(total * (1.0 / ROW_TILE))
    k = lax.broadcasted_iota(jnp.int32, (ROUTE_LANES, ROUTE_LANES), 0)
    lane = lax.broadcasted_iota(jnp.int32, (ROUTE_LANES, ROUTE_LANES), 1)
    tile_end = exact(jnp.broadcast_to(tiles_e, (8, ROUTE_LANES)), jnp.where(k <= lane, 1.0, 0.0))[0:1]
    tile_start = tile_end - tiles_e
    r = lax.broadcasted_iota(jnp.int32, (nt, nt), 0)
    c = lax.broadcasted_iota(jnp.int32, (nt, nt), 1)
    before = exact(jnp.where(c < r, 1.0, 0.0), counts)
    base_ref[...] = tile_start * ROW_TILE + before

    used = jnp.max(tile_end, axis=-1, keepdims=True)
    tile = k.astype(F32)
    clamped = jnp.minimum(tile, jnp.maximum(used - 1.0, 0.0))
    is_expert = lane < N_EXPERTS
    owner = jnp.sum(jnp.where(jnp.logical_and(is_expert, tile_end <= clamped), 1.0, 0.0), axis=-1, keepdims=True)
    owner = jnp.minimum(owner, N_EXPERTS - 1.0)
    mine = lane.astype(F32) == owner
    pick = lambda v: jnp.sum(jnp.where(mine, v, 0.0), axis=-1, keepdims=True)
    t1 = tile[:, 0:1]
    valid = jnp.clip(pick(total) - (t1 - pick(tile_start)) * ROW_TILE, 0.0, float(ROW_TILE))
    valid = jnp.where(t1 < used, valid, 0.0)
    owner_ref[...] = jnp.broadcast_to(owner, owner_ref.shape).astype(I32)
    valid_ref[...] = jnp.broadcast_to(valid, valid_ref.shape).astype(I32)
    used_ref[...] = jnp.broadcast_to(used, used_ref.shape).astype(I32)


def _tile_plan(cnt, n_tiles):
    assert n_tiles <= ROUTE_LANES
    nt = cnt.shape[0]
    sq = (ROUTE_LANES, ROUTE_LANES)
    base, owner, valid, used = pl.pallas_call(
        _plan_kernel,
        out_shape=[jax.ShapeDtypeStruct((nt, ROUTE_LANES), F32), jax.ShapeDtypeStruct(sq, I32),
                   jax.ShapeDtypeStruct(sq, I32), jax.ShapeDtypeStruct((8, ROUTE_LANES), I32)],
        name="tile_plan",
    )(cnt.reshape(nt, ROUTE_LANES))
    return base.reshape(cnt.shape), owner[:n_tiles, 0], valid[:n_tiles, 0], used[0, :1]


def _sc_mesh():
    return plsc.VectorSubcoreMesh(core_axis_name="c", subcore_axis_name="s")


def _sc_worker():
    return lax.axis_index("s") * SC_CORES + lax.axis_index("c")


def _dispatch(hp, pos1, pos2, n_rows):
    t = hp.shape[0]
    per_worker = t // SC_WORKERS
    k = per_worker // SC_CHUNK
    idx = pltpu.VMEM((k, SC_CHUNK), I32)

    @functools.partial(pl.kernel, mesh=_sc_mesh(), out_type=jax.ShapeDtypeStruct((n_rows, HALF_D), I32),
                       scratch_types=[idx, idx, pltpu.VMEM((SC_CHUNK, HALF_D), I32)], name="moe_dispatch")
    def run(h_hbm, p1_hbm, p2_hbm, xs_hbm, p1_v, p2_v, rows_v):
        wid = _sc_worker()
        pltpu.sync_copy(p1_hbm.at[wid], p1_v)
        pltpu.sync_copy(p2_hbm.at[wid], p2_v)
        for j in range(k):
            pltpu.sync_copy(h_hbm.at[pl.ds(wid * per_worker + j * SC_CHUNK, SC_CHUNK)], rows_v)
            pltpu.sync_copy(rows_v, xs_hbm.at[p1_v.at[j]])
            pltpu.sync_copy(rows_v, xs_hbm.at[p2_v.at[j]])

    return run(hp, pos1.reshape(SC_WORKERS, k, SC_CHUNK), pos2.reshape(SC_WORKERS, k, SC_CHUNK))


def _combine_gather(ys, pos1, pos2):
    t = pos1.size
    per_worker = t // SC_WORKERS
    k = per_worker // SC_CHUNK
    idx = pltpu.VMEM((k, SC_CHUNK), I32)
    out = jax.ShapeDtypeStruct((t, HALF_D), I32)

    @functools.partial(pl.kernel, mesh=_sc_mesh(), out_type=(out, out),
                       scratch_types=[idx, idx, pltpu.VMEM((SC_CHUNK, HALF_D), I32)], name="moe_combine")
    def run(ys_hbm, p1_hbm, p2_hbm, g1_hbm, g2_hbm, p1_v, p2_v, rows_v):
        wid = _sc_worker()
        pltpu.sync_copy(p1_hbm.at[wid], p1_v)
        pltpu.sync_copy(p2_hbm.at[wid], p2_v)
        for j in range(k):
            dst = pl.ds(wid * per_worker + j * SC_CHUNK, SC_CHUNK)
            pltpu.sync_copy(ys_hbm.at[p1_v.at[j]], rows_v)
            pltpu.sync_copy(rows_v, g1_hbm.at[dst])
            pltpu.sync_copy(ys_hbm.at[p2_v.at[j]], rows_v)
            pltpu.sync_copy(rows_v, g2_hbm.at[dst])

    return run(ys, pos1.reshape(SC_WORKERS, k, SC_CHUNK), pos2.reshape(SC_WORKERS, k, SC_CHUNK))


def _experts_kernel(owner_ref, valid_ref, used_ref, xs_ref, wg_ref, wu_ref, wd_ref, ys_ref, wg_bf, wu_bf, wd_bf):
    i = pl.program_id(0)
    valid = valid_ref[i]

    @pl.when(jnp.logical_or(i == 0, owner_ref[i] != owner_ref[jnp.maximum(i - 1, 0)]))
    def _():
        wg_bf[...] = wg_ref[...].astype(BF16)
        wu_bf[...] = wu_ref[...].astype(BF16)
        wd_bf[...] = wd_ref[...].astype(BF16)

    @pl.when(i < used_ref[0])
    def _():
        row = lax.broadcasted_iota(jnp.int32, (ROW_TILE, HALF_D), 0)
        packed = jnp.where(row < valid, xs_ref[...], 0)
        hi, lo = _unpack_halves(packed)
        hi = hi.astype(BF16)
        lo = lo.astype(BF16)
        hg = _dot(hi, wg_bf[:HALF_D, :]) + _dot(lo, wg_bf[HALF_D:, :])
        hu = _dot(hi, wu_bf[:HALF_D, :]) + _dot(lo, wu_bf[HALF_D:, :])
        hid = (hg * _sigmoid(hg) * hu).astype(BF16)
        ys_ref[...] = _pack_halves(_dot(hid, wd_bf[...]))


def _experts(xs, owner, valid, used, wg, wu, wd):
    n_tiles = owner.shape[0]
    tile = lambda i, own, val, used: (jnp.minimum(i, used[0] - 1), 0)
    grid_spec = pltpu.PrefetchScalarGridSpec(
        num_scalar_prefetch=3,
        grid=(n_tiles,),
        in_specs=[pl.BlockSpec((ROW_TILE, HALF_D), tile),
                  pl.BlockSpec((None, D_MODEL, EXPERT_FF), lambda i, own, val, used: (own[i], 0, 0)),
                  pl.BlockSpec((None, D_MODEL, EXPERT_FF), lambda i, own, val, used: (own[i], 0, 0)),
                  pl.BlockSpec((None, EXPERT_FF, D_MODEL), lambda i, own, val, used: (own[i], 0, 0))],
        out_specs=pl.BlockSpec((ROW_TILE, HALF_D), tile),
        scratch_shapes=[pltpu.VMEM((D_MODEL, EXPERT_FF), BF16), pltpu.VMEM((D_MODEL, EXPERT_FF), BF16),
                        pltpu.VMEM((EXPERT_FF, D_MODEL), BF16)],
    )
    return pl.pallas_call(
        _experts_kernel,
        grid_spec=grid_spec,
        out_shape=jax.ShapeDtypeStruct((n_tiles * ROW_TILE, HALF_D), I32),
        compiler_params=pltpu.CompilerParams(dimension_semantics=("arbitrary",),
                                             vmem_limit_bytes=VMEM_LIMIT),
        name="experts",
    )(owner, valid, used, xs, wg, wu, wd)


def _ple_kernel(x1_ref, g1_ref, g2_ref, route_ref, p_ref, wp_ref, pg_ref, gg_ref, wgate_ref, *rest):
    o_ref = rest[-1]
    route = route_ref[...]
    lane = lax.broadcasted_iota(jnp.int32, route.shape, 1)
    pick = lambda ln: jnp.sum(jnp.where(lane == ln, route, 0.0), axis=-1, keepdims=True)
    c1 = pick(LANE_C1)
    c2 = pick(LANE_C2)
    a_hi, a_lo = _unpack_halves(g1_ref[...])
    b_hi, b_lo = _unpack_halves(g2_ref[...])
    x = x1_ref[...] + jnp.concatenate([c1 * a_hi + c2 * b_hi, c1 * a_lo + c2 * b_lo], axis=1)
    ple = _rms(_dot(p_ref[...].astype(BF16), wp_ref[...]), pg_ref[...])
    gate = _sigmoid(_dot(_rms(x, gg_ref[...]).astype(BF16), wgate_ref[...]))
    o_ref[...] = x + gate * ple


def _ple(x1, g1, g2, route, p2d, w_ple, ple_g, gate_g, w_gate, part, out_prev):
    t = x1.shape[0]
    steps = g1.shape[0] // PLE_TM
    off = part * steps
    whole = lambda w: pl.BlockSpec((PLE_TM, w), lambda i: (i + off, 0))
    local = lambda w: pl.BlockSpec((PLE_TM, w), lambda i: (i, 0))
    full = lambda a: pl.BlockSpec(a.shape, lambda i: (0,) * a.ndim)
    in_specs = [whole(D_MODEL), local(HALF_D), local(HALF_D), whole(ROUTE_LANES), whole(PLE_DIM),
                full(w_ple), full(ple_g), full(gate_g), full(w_gate)]
    args = [x1, g1, g2, route, p2d, w_ple, ple_g, gate_g, w_gate]
    aliases = {}
    if out_prev is not None:
        in_specs.append(pl.BlockSpec(memory_space=pl.ANY))
        aliases = {len(args): 0}
        args.append(out_prev)
    return pl.pallas_call(
        _ple_kernel,
        grid=(steps,),
        in_specs=in_specs,
        out_specs=whole(D_MODEL),
        out_shape=jax.ShapeDtypeStruct((t, D_MODEL), F32),
        input_output_aliases=aliases,
        compiler_params=pltpu.CompilerParams(dimension_semantics=("parallel",),
                                             vmem_limit_bytes=VMEM_LIMIT),
        name="ple",
    )(*args)


def kernel(x, p, mix_norm_g, w_in, ret_gn_g, ret_gn_b, w_ret_o, q_norm_g, k_norm_g, attn_sinks,
           w_swa_o, w_out, ffn_norm_g, w_router_group, b_router_group, w_router_expert,
           b_router_expert, w_exp_gate, w_exp_up, w_exp_down, ple_gate_norm_g, w_ple_gate,
           w_ple, ple_norm_g):
    batch, seq, d = x.shape
    t = batch * seq
    depth = w_in.shape[0]
    n_tiles = 2 * t // ROW_TILE + N_EXPERTS
    row = lambda a: a.reshape(1, -1)
    x2d = x.reshape(t, d)
    for i in range(depth):
        pad = ROUTE_LANES - N_GROUPS - N_EXPERTS
        w_router = jnp.pad(jnp.concatenate([w_router_expert[i], w_router_group[i]], axis=1), ((0, 0), (0, pad)))
        b_router = jnp.pad(jnp.concatenate([b_router_expert[i], b_router_group[i]]), (0, pad)).reshape(1, -1)
        w_router_hi = w_router.astype(BF16)
        w_router = jnp.concatenate([w_router_hi, (w_router - w_router_hi.astype(F32)).astype(BF16)], axis=1)
        x1, hp, route, cnt = _block(x2d, row(mix_norm_g[i]), w_in[i], row(ret_gn_g[i]),
                                    row(ret_gn_b[i]), attn_sinks[i], q_norm_g[i], k_norm_g[i],
                                    w_ret_o[i].astype(BF16), w_swa_o[i].astype(BF16), w_out[i].astype(BF16),
                                    row(ffn_norm_g[i]), w_router, b_router, seq)
        base, owner, valid, used = _tile_plan(cnt, n_tiles)
        pos = _route_pos(route, base)
        pos1 = pos[:, 0, :]
        pos2 = pos[:, 1, :]
        xs = _dispatch(hp, pos1, pos2, n_tiles * ROW_TILE)
        ys = _experts(xs, owner, valid, used, w_exp_gate[i], w_exp_up[i], w_exp_down[i])
        out = None
        blocks = pos.shape[0] // COMBINE_PARTS
        for part in range(COMBINE_PARTS):
            sel = slice(part * blocks, (part + 1) * blocks)
            g1, g2 = _combine_gather(ys, pos1[sel], pos2[sel])
            out = _ple(x1, g1, g2, route, p[i].reshape(t, PLE_DIM), w_ple[i].astype(BF16), row(ple_norm_g[i]),
                       row(ple_gate_norm_g[i]), w_ple_gate[i].astype(BF16), part, out)
        x2d = out
    return x2d.reshape(batch, seq, d)
```

```python
import functools

import jax
import jax.numpy as jnp
import numpy as np
from jax import lax
from jax.experimental import pallas as pl
from jax.experimental.pallas import tpu as pltpu
from jax.experimental.pallas import tpu_sc as plsc

F32 = jnp.float32
BF16 = jnp.bfloat16
I32 = jnp.int32

EPS = 1e-6
D_MODEL = 1024
PLE_DIM = 256
RET_HEADS = 4
RET_DK = 128
RET_DV = 128
RET_CHUNK = 128
SWA_HEADS = 8
SWA_KV_HEADS = 2
SWA_GROUP = SWA_HEADS // SWA_KV_HEADS
SWA_HD = 64
WINDOW = 128
N_GROUPS = 4
EXPERTS_PER_GROUP = 8
N_EXPERTS = N_GROUPS * EXPERTS_PER_GROUP
EXPERT_FF = 256

RET_W = RET_HEADS * RET_DK
SWA_Q = SWA_HEADS * SWA_HD
SWA_KV = SWA_KV_HEADS * SWA_HD
Z_RET_W = 4 * RET_W
Z_SKV_W = 2 * SWA_KV
Z_GATE_W = 2 * D_MODEL
COL_SQ = Z_RET_W
COL_SKV = COL_SQ + SWA_Q
COL_GATE = COL_SKV + Z_SKV_W
IN_WIDTH = COL_GATE + Z_GATE_W

ROUTE_LANES = 128
LANE_GROUP0 = N_EXPERTS
LANE_E1, LANE_E2, LANE_C1, LANE_C2 = 32, 33, 34, 35
NEG_INF = -1e30

HALF_D = D_MODEL // 2
ROW_TILE = 512
SC_CORES = 2
SC_SUBCORES = 16
SC_WORKERS = SC_CORES * SC_SUBCORES
SC_CHUNK = 128
COMBINE_PARTS = 1

VMEM_LIMIT = 56 * 1024 * 1024

BLOCK_ROWS = 512
W_SLAB = 256
RET_ROWS = 512
MERGE_SUB = 256
ROUTE_TM = 512
PLE_TM = 512


def _rms(x, g):
    ms = jnp.mean(x * x, axis=-1, keepdims=True)
    return x * lax.rsqrt(ms + EPS) * g


def _sigmoid(x):
    return 1.0 / (1.0 + jnp.exp(-x))


def _dot(a, b):
    return jnp.dot(a, b, preferred_element_type=F32)


def _dot_nt(a, b):
    return lax.dot_general(a, b, (((1,), (1,)), ((), ())), preferred_element_type=F32)


def _dot_tn(a, b):
    return lax.dot_general(a, b, (((0,), (0,)), ((), ())), preferred_element_type=F32)


def _pack_halves(v):
    return pltpu.pack_elementwise([v[:, :HALF_D], v[:, HALF_D:]], packed_dtype=BF16)


def _unpack_halves(p):
    words = lax.bitcast_convert_type(p, jnp.uint32)
    return tuple(pltpu.unpack_elementwise(words, index=k, packed_dtype=BF16, unpacked_dtype=F32) for k in range(2))


def _retention_tables():
    f32 = np.float32
    h = RET_HEADS
    c = RET_CHUNK
    log_gamma = np.log1p(-np.exp2(f32(-5.0) - np.arange(h, dtype=f32))).astype(f32)
    pos = np.arange(c, dtype=f32)
    diff = pos[:, None] - pos[None, :]
    decay = np.where(diff[None] >= 0.0, np.exp(np.maximum(diff, f32(0.0))[None] * log_gamma[:, None, None]), f32(0.0))
    scale = f32(RET_DK ** -0.5)
    dmask = decay * scale
    zeta = np.exp((f32(c - 1.0) - pos)[None, :] * log_gamma[:, None]) * scale
    xi = np.exp((pos + f32(1.0))[None, :] * log_gamma[:, None])
    cdec = np.exp(f32(c) * log_gamma)
    bc = lambda a: np.ascontiguousarray(np.broadcast_to(a[:, :, None], (h, c, c)), dtype=f32)
    return (dmask.astype(f32), bc(zeta), bc(xi),
            np.ascontiguousarray(np.broadcast_to(cdec[:, None, None], (h, c, c)), dtype=f32))


def _swa_tables(q_g, k_g):
    f32 = np.float32
    qi = np.arange(WINDOW)[:, None]
    sj = np.arange(2 * WINDOW)[None, :]
    rel = qi + WINDOW - sj
    ok = (rel >= 0) & (rel < WINDOW)
    slopes = np.exp2(f32(-8.0) * np.arange(1, SWA_HEADS + 1, dtype=f32) / f32(SWA_HEADS)).astype(f32)
    bias = np.where(ok[None], -slopes[:, None, None] * rel.astype(f32)[None], f32(NEG_INF)).astype(f32)
    bias = bias.reshape(SWA_KV_HEADS, SWA_GROUP * WINDOW, 2 * WINDOW)
    avg = lambda heads: jnp.asarray(np.kron(np.eye(heads, dtype=f32), np.full((SWA_HD, SWA_HD), 1.0 / SWA_HD, f32)),
                                    dtype=BF16)
    qg = jnp.tile(q_g.reshape(1, SWA_HD), (1, SWA_HEADS)) * (SWA_HD ** -0.5)
    kg = jnp.tile(k_g.reshape(1, SWA_HD), (1, SWA_KV_HEADS))
    return qg, kg, avg(SWA_HEADS), avg(SWA_KV_HEADS), jnp.asarray(bias)


def _block_kernel(per_seq, sinks_ref, x_ref, xp_ref, g_ref, w_hbm, dmask_ref, zeta_ref, xi_ref, cdec_ref,
                  gng_ref, gnb_ref, qg_ref, kg_ref, bq_ref, bk_ref, bias_ref,
                  wro_ref, wso_ref, wout_ref, fg_ref, wr_ref, br_ref,
                  x1_ref, hp_ref, route_ref, cnt_ref,
                  w_ref, wtmp_ref, w_sem, zc_ref, zn_ref, yr_ref, ys_ref, state_ref, prev_ref, band_ref, tail_ref):
    g = pl.program_id(0)
    seq_start = lax.rem(g - 1, per_seq) == 0

    @pl.when(g == 0)
    def _():
        n_slabs = IN_WIDTH // W_SLAB
        slab = lambda j: pltpu.make_async_copy(w_hbm.at[:, pl.ds(j * W_SLAB, W_SLAB)], wtmp_ref.at[j % 2],
                                               w_sem.at[j % 2])
        slab(0).start()
        for j in range(n_slabs):
            if j + 1 < n_slabs:
                slab(j + 1).start()
            slab(j).wait()
            w_ref[:, j * W_SLAB:(j + 1) * W_SLAB] = wtmp_ref[j % 2].astype(BF16)
        zc_ref[...] = jnp.zeros_like(zc_ref)
        tail_ref[...] = jnp.zeros_like(tail_ref)
        state_ref[...] = jnp.zeros_like(state_ref)

    @pl.when(seq_start)
    def _():
        state_ref[...] = jnp.zeros_like(state_ref)

    band_ref[:WINDOW, :] = tail_ref[...]
    band_ref[WINDOW:, :] = zc_ref[:, COL_SKV:COL_GATE]

    h = _rms(x_ref[...], g_ref[...]).astype(BF16)

    def project(c0, width):
        zn_ref[:, c0:c0 + width] = _dot(h, w_ref[:, c0:c0 + width]).astype(BF16)

    cw = 512
    proj_pieces = [functools.partial(project, j * cw, cw) for j in range(COL_SKV // cw)]
    proj_pieces.append(functools.partial(project, COL_SKV, Z_SKV_W))
    proj_pieces += [functools.partial(project, COL_GATE + j * cw, cw) for j in range(Z_GATE_W // cw)]

    def retention_rows(t):
        r_base = t * RET_ROWS
        n_chunks = RET_ROWS // RET_CHUNK
        part = lambda c, p, hd: zc_ref[r_base + c * RET_CHUNK:r_base + (c + 1) * RET_CHUNK,
                                       p * RET_W + hd * RET_DK:p * RET_W + (hd + 1) * RET_DK]
        for hd in range(RET_HEADS):
            state = state_ref[hd]
            for c in range(n_chunks):
                prev_ref[t * n_chunks + c, hd] = state.astype(BF16)
                kz = (part(c, 1, hd).astype(F32) * zeta_ref[hd]).astype(BF16)
                state = cdec_ref[hd] * state + _dot_tn(kz, part(c, 2, hd))
            state_ref[hd] = state
        out_rows = []
        for c in range(n_chunks):
            heads = []
            for hd in range(RET_HEADS):
                q = part(c, 0, hd)
                scores = _dot_nt(q, part(c, 1, hd)) * dmask_ref[hd]
                y = _dot(scores.astype(BF16), part(c, 2, hd)) + _dot(q, prev_ref[t * n_chunks + c, hd]) * xi_ref[hd]
                mu = jnp.mean(y, axis=-1, keepdims=True)
                d = y - mu
                var = jnp.mean(d * d, axis=-1, keepdims=True)
                hs = slice(hd * RET_DV, (hd + 1) * RET_DV)
                yn = d * lax.rsqrt(var + EPS) * gng_ref[:, hs] + gnb_ref[:, hs]
                gate = part(c, 3, hd).astype(F32)
                heads.append((gate * _sigmoid(gate) * yn).astype(BF16))
            out_rows.append(jnp.concatenate(heads, axis=1))
        yr_ref[r_base:r_base + RET_ROWS, :] = jnp.concatenate(out_rows, axis=0)

    group_rows = SWA_GROUP * WINDOW
    lo_q = lax.broadcasted_iota(jnp.int32, (WINDOW, 2 * SWA_HD), 1) < SWA_HD
    lo_k = lax.broadcasted_iota(jnp.int32, (2 * WINDOW, 2 * SWA_HD), 1) < SWA_HD
    in_prev = lax.broadcasted_iota(jnp.int32, (group_rows, 2 * WINDOW), 1) < WINDOW
    row_head = lax.broadcasted_iota(jnp.int32, (group_rows, 1), 0) // WINDOW

    def both_heads(v):
        swapped = pltpu.roll(v, SWA_HD, axis=1)
        return jnp.where(lo_k, v, swapped), jnp.where(lo_k, swapped, v)

    def swa_block(n):
        rows = slice(n * WINDOW, (n + 1) * WINDOW)
        kv = band_ref[n * WINDOW:(n + 2) * WINDOW, :]
        kf = kv[:, :SWA_KV].astype(F32)
        kn = kf * lax.rsqrt(_dot((kf * kf).astype(BF16), bk_ref[...]) + EPS) * kg_ref[...]
        keys = [a.astype(BF16) for a in both_heads(kn)]
        vals = [a.astype(BF16) for a in both_heads(kv[:, SWA_KV:].astype(F32))]
        qf = zc_ref[rows, COL_SQ:COL_SKV].astype(F32)
        qn = qf * lax.rsqrt(_dot((qf * qf).astype(BF16), bq_ref[...]) + EPS) * qg_ref[...]
        outs = []
        for kh in range(SWA_KV_HEADS):
            parts = []
            for j in range(SWA_GROUP // 2):
                c0 = (kh * SWA_GROUP + 2 * j) * SWA_HD
                two = qn[:, c0:c0 + 2 * SWA_HD]
                parts += [jnp.where(lo_q, two, 0.0), jnp.where(lo_q, 0.0, two)]
            qs = jnp.concatenate(parts, axis=0).astype(BF16)
            s = _dot_nt(qs, keys[kh]) + bias_ref[kh]
            if n == 0:
                s = jnp.where(jnp.logical_and(seq_start, in_prev), NEG_INF, s)
            sink = sinks_ref[kh * SWA_GROUP]
            for gi in range(1, SWA_GROUP):
                sink = jnp.where(row_head == gi, sinks_ref[kh * SWA_GROUP + gi], sink)
            m = jnp.maximum(jnp.max(s, axis=-1, keepdims=True), sink)
            p = jnp.exp(s - m)
            denom = jnp.sum(p, axis=-1, keepdims=True) + jnp.exp(sink - m)
            o = _dot(p.astype(BF16), vals[kh]) * (1.0 / denom)
            for j in range(SWA_GROUP // 2):
                even = o[(2 * j) * WINDOW:(2 * j + 1) * WINDOW]
                odd = o[(2 * j + 1) * WINDOW:(2 * j + 2) * WINDOW]
                outs.append(jnp.where(lo_q, even, odd).astype(BF16))
        ys_ref[rows, :] = jnp.concatenate(outs, axis=1)

    mix_pieces = [functools.partial(retention_rows, t) for t in range(BLOCK_ROWS // RET_ROWS)]
    mix_pieces += [functools.partial(swa_block, n) for n in range(BLOCK_ROWS // WINDOW)]

    subs = []

    def merge_sub(s):
        subs.append(_merge_rows(slice(s * MERGE_SUB, (s + 1) * MERGE_SUB), yr_ref, ys_ref, zc_ref, xp_ref, wro_ref,
                                wso_ref, wout_ref, fg_ref, wr_ref, br_ref))

    other_pieces = mix_pieces + [functools.partial(merge_sub, s) for s in range(BLOCK_ROWS // MERGE_SUB)]

    n_other = len(other_pieces)
    for k, piece in enumerate(other_pieces):
        for proj_piece in proj_pieces[k * len(proj_pieces) // n_other:(k + 1) * len(proj_pieces) // n_other]:
            proj_piece()
        piece()
    x1_ref[...] = jnp.concatenate([s[0] for s in subs], axis=0)
    hp_ref[...] = jnp.concatenate([s[1] for s in subs], axis=0)
    route_ref[...] = jnp.concatenate([s[2] for s in subs], axis=0)
    cnt_ref[...] = sum(s[3] for s in subs)

    tail_ref[...] = zc_ref[BLOCK_ROWS - WINDOW:, COL_SKV:COL_GATE]
    zc_ref[...] = zn_ref[...]


def _merge_rows(rows, yr_ref, ys_ref, z_ref, x_ref, wro_ref, wso_ref, wout_ref, fg_ref, wr_ref, br_ref):
    a = _dot(yr_ref[rows, :], wro_ref[...])
    b = _dot(ys_ref[rows, :], wso_ref[...])
    gate_r = z_ref[rows, COL_GATE:COL_GATE + D_MODEL].astype(F32)
    gate_s = z_ref[rows, COL_GATE + D_MODEL:].astype(F32)
    merged = (_sigmoid(gate_r) * a + _sigmoid(gate_s) * b).astype(BF16)
    x1 = x_ref[rows, :] + _dot(merged, wout_ref[...])
    h2 = _rms(x1, fg_ref[...])
    packed = _pack_halves(h2)

    h_hi = h2.astype(BF16)
    h_lo = (h2 - h_hi.astype(F32)).astype(BF16)
    parts = _dot(h_hi, wr_ref[...]) + _dot(h_lo, wr_ref[...])
    logits = parts[:, :ROUTE_LANES] + parts[:, ROUTE_LANES:] + br_ref[...]
    lane = lax.broadcasted_iota(jnp.int32, logits.shape, 1)
    big = jnp.int32(ROUTE_LANES)
    is_group = jnp.logical_and(lane >= LANE_GROUP0, lane < LANE_GROUP0 + N_GROUPS)
    gl = jnp.where(is_group, logits, NEG_INF)
    gmax = jnp.max(gl, axis=-1, keepdims=True)
    g_w = 1.0 / jnp.sum(jnp.exp(gl - gmax), axis=-1, keepdims=True)
    g_sel = jnp.min(jnp.where(gl == gmax, lane, big), axis=-1, keepdims=True) - LANE_GROUP0
    in_group = jnp.logical_and(lane < N_EXPERTS, (lane >> 3) == g_sel)
    el = jnp.where(in_group, logits, NEG_INF)
    m1 = jnp.max(el, axis=-1, keepdims=True)
    i1 = jnp.min(jnp.where(el == m1, lane, big), axis=-1, keepdims=True)
    el2 = jnp.where(lane == i1, NEG_INF, el)
    m2 = jnp.max(el2, axis=-1, keepdims=True)
    i2 = jnp.min(jnp.where(el2 == m2, lane, big), axis=-1, keepdims=True)
    e2 = jnp.exp(m2 - m1)
    c1 = g_w / (1.0 + e2)
    c2 = g_w * e2 / (1.0 + e2)
    chosen = jnp.where(jnp.logical_or(lane == i1, lane == i2), 1.0, 0.0)
    route = (chosen
             + jnp.where(lane == LANE_E1, i1.astype(F32), 0.0)
             + jnp.where(lane == LANE_E2, i2.astype(F32), 0.0)
             + jnp.where(lane == LANE_C1, c1, 0.0)
             + jnp.where(lane == LANE_C2, c2, 0.0))
    return x1, packed, route, jnp.sum(chosen, axis=0, keepdims=True)


def _block(x2d, mix_g, w_in, gn_g, gn_b, sinks, q_g, k_g, w_ret_o, w_swa_o, w_out, ffn_g, w_router, b_router,
           seq):
    t = x2d.shape[0]
    n = t // BLOCK_ROWS
    assert seq % BLOCK_ROWS == 0
    tables = _retention_tables()
    swa_tables = _swa_tables(q_g, k_g)
    consts = tables + (gn_g, gn_b) + swa_tables + (w_ret_o, w_swa_o, w_out, ffn_g, w_router, b_router)
    full = lambda a: pl.BlockSpec(a.shape, lambda g: (0,) * a.ndim)
    cur = lambda w: pl.BlockSpec((BLOCK_ROWS, w), lambda g: (jnp.minimum(g, n - 1), 0))
    prv = lambda w: pl.BlockSpec((BLOCK_ROWS, w), lambda g: (jnp.maximum(g - 1, 0), 0))
    chunks = BLOCK_ROWS // RET_CHUNK
    return pl.pallas_call(
        functools.partial(_block_kernel, seq // BLOCK_ROWS),
        grid=(n + 1,),
        in_specs=[pl.BlockSpec(memory_space=pltpu.SMEM), cur(D_MODEL), prv(D_MODEL), full(mix_g),
                  pl.BlockSpec(memory_space=pl.ANY)] + [full(a) for a in consts],
        out_specs=[prv(D_MODEL), prv(HALF_D), prv(ROUTE_LANES),
                   pl.BlockSpec((None, 1, ROUTE_LANES), lambda g: (jnp.maximum(g - 1, 0), 0, 0))],
        out_shape=[jax.ShapeDtypeStruct((t, D_MODEL), F32),
                   jax.ShapeDtypeStruct((t, HALF_D), I32),
                   jax.ShapeDtypeStruct((t, ROUTE_LANES), F32),
                   jax.ShapeDtypeStruct((n, 1, ROUTE_LANES), F32)],
        scratch_shapes=[pltpu.VMEM(w_in.shape, BF16), pltpu.VMEM((2, D_MODEL, W_SLAB), F32),
                        pltpu.SemaphoreType.DMA((2,)),
                        pltpu.VMEM((BLOCK_ROWS, IN_WIDTH), BF16), pltpu.VMEM((BLOCK_ROWS, IN_WIDTH), BF16),
                        pltpu.VMEM((BLOCK_ROWS, RET_W), BF16), pltpu.VMEM((BLOCK_ROWS, SWA_Q), BF16),
                        pltpu.VMEM((RET_HEADS, RET_DK, RET_DV), F32),
                        pltpu.VMEM((chunks, RET_HEADS, RET_DK, RET_DV), BF16),
                        pltpu.VMEM((BLOCK_ROWS + WINDOW, Z_SKV_W), BF16), pltpu.VMEM((WINDOW, Z_SKV_W), BF16)],
        compiler_params=pltpu.CompilerParams(dimension_semantics=("arbitrary",),
                                             vmem_limit_bytes=VMEM_LIMIT),
        name="block",
    )(sinks, x2d, x2d, mix_g, w_in, *consts)


def _route_pos_kernel(route_ref, base_ref, pos_ref):
    route = route_ref[...]
    lane = lax.broadcasted_iota(jnp.int32, route.shape, 1)
    lane_f = lane.astype(F32)
    chosen = jnp.where(lane < N_EXPERTS, route, 0.0)
    r = lax.broadcasted_iota(jnp.int32, (ROUTE_TM, ROUTE_TM), 0)
    c = lax.broadcasted_iota(jnp.int32, (ROUTE_TM, ROUTE_TM), 1)
    tri = jnp.where(c <= r, 1.0, 0.0).astype(BF16)
    incl = _dot(tri, chosen.astype(BF16))
    posmat = base_ref[...] + incl - 1.0
    pick = lambda ln: jnp.sum(jnp.where(lane == ln, route, 0.0), axis=-1, keepdims=True)
    pos_of = lambda e: jnp.sum(jnp.where(lane_f == e, posmat, 0.0), axis=-1, keepdims=True)
    p1 = pos_of(pick(LANE_E1))
    p2 = pos_of(pick(LANE_E2))
    rec = jnp.where(lane == 0, p1, 0.0) + jnp.where(lane == 1, p2, 0.0)
    pos_ref[...] = rec.T[:8, :].astype(I32)


def _route_pos(route, base):
    t = route.shape[0]
    n = t // ROUTE_TM
    return pl.pallas_call(
        _route_pos_kernel,
        grid=(n,),
        in_specs=[pl.BlockSpec((ROUTE_TM, ROUTE_LANES), lambda i: (i, 0)),
                  pl.BlockSpec((None, 1, ROUTE_LANES), lambda i: (i, 0, 0))],
        out_specs=pl.BlockSpec((None, 8, ROUTE_TM), lambda i: (i, 0, 0)),
        out_shape=jax.ShapeDtypeStruct((n, 8, ROUTE_TM), I32),
        compiler_params=pltpu.CompilerParams(dimension_semantics=("parallel",),
                                             vmem_limit_bytes=VMEM_LIMIT),
        name="route_pos",
    )(route, base)


def _plan_kernel(cnt_ref, base_ref, owner_ref, valid_ref, used_ref, first_ref, next_ref, slot_ref):
    counts = cnt_ref[...]
    nt = counts.shape[0]
    exact = functools.partial(jnp.dot, precision=lax.Precision.HIGHEST, preferred_element_type=F32)
    total = jnp.sum(counts, axis=0, keepdims=True)
    tiles_e = jnp.ceil(total * (1.0 / ROW_TILE))
    k = lax.broadcasted_iota(jnp.int32, (ROUTE_LANES, ROUTE_LANES), 0)
    lane = lax.broadcasted_iota(jnp.int32, (ROUTE_LANES, ROUTE_LANES), 1)
    tile_end = exact(jnp.broadcast_to(tiles_e, (8, ROUTE_LANES)), jnp.where(k <= lane, 1.0, 0.0))[0:1]
    tile_start = tile_end - tiles_e
    r = lax.broadcasted_iota(jnp.int32, (nt, nt), 0)
    c = lax.broadcasted_iota(jnp.int32, (nt, nt), 1)
    before = exact(jnp.where(c < r, 1.0, 0.0), counts)
    base_ref[...] = tile_start * ROW_TILE + before

    used = jnp.max(tile_end, axis=-1, keepdims=True)
    tile = k.astype(F32)
    clamped = jnp.minimum(tile, jnp.maximum(used - 1.0, 0.0))
    is_expert = lane < N_EXPERTS
    owner = jnp.sum(jnp.where(jnp.logical_and(is_expert, tile_end <= clamped), 1.0, 0.0), axis=-1, keepdims=True)
    owner = jnp.minimum(owner, N_EXPERTS - 1.0)
    mine = lane.astype(F32) == owner
    pick = lambda v: jnp.sum(jnp.where(mine, v, 0.0), axis=-1, keepdims=True)
    t1 = tile[:, 0:1]
    valid = jnp.clip(pick(total) - (t1 - pick(tile_start)) * ROW_TILE, 0.0, float(ROW_TILE))
    valid = jnp.where(t1 < used, valid, 0.0)
    owner_ref[...] = jnp.broadcast_to(owner, owner_ref.shape).astype(I32)
    valid_ref[...] = jnp.broadcast_to(valid, valid_ref.shape).astype(I32)
    used_ref[...] = jnp.broadcast_to(used, used_ref.shape).astype(I32)

    nonempty = jnp.where(jnp.logical_and(lane[0:1] < N_EXPERTS, tiles_e > 0.0), 1.0, 0.0)
    rank = exact(jnp.broadcast_to(nonempty, (8, ROUTE_LANES)), jnp.where(k < lane, 1.0, 0.0))[0:1]
    nonempty_col = jnp.sum(jnp.where(k == lane, jnp.broadcast_to(nonempty, k.shape), 0.0), axis=-1, keepdims=True)
    later = jnp.where(jnp.logical_and(k > lane, nonempty_col > 0.0), tile, float(ROUTE_LANES))
    nxt_e = jnp.min(later, axis=0, keepdims=True)
    nxt_e = jnp.where(nxt_e < float(ROUTE_LANES), nxt_e, -1.0)
    first = jnp.where(jnp.logical_and(t1 == pick(tile_start), t1 < used), 1.0, 0.0)
    my_rank = pick(rank)
    slot = my_rank - 2.0 * jnp.floor(my_rank * 0.5)
    first_ref[...] = jnp.broadcast_to(first, first_ref.shape).astype(I32)
    next_ref[...] = jnp.broadcast_to(pick(nxt_e), next_ref.shape).astype(I32)
    slot_ref[...] = jnp.broadcast_to(slot, slot_ref.shape).astype(I32)


def _tile_plan(cnt, n_tiles):
    assert n_tiles <= ROUTE_LANES
    nt = cnt.shape[0]
    sq = jax.ShapeDtypeStruct((ROUTE_LANES, ROUTE_LANES), I32)
    base, owner, valid, used, first, nxt, slot = pl.pallas_call(
        _plan_kernel,
        out_shape=[jax.ShapeDtypeStruct((nt, ROUTE_LANES), F32), sq, sq,
                   jax.ShapeDtypeStruct((8, ROUTE_LANES), I32), sq, sq, sq],
        name="tile_plan",
    )(cnt.reshape(nt, ROUTE_LANES))
    per_tile = [a[:n_tiles, 0] for a in (owner, valid, first, nxt, slot)]
    return base.reshape(cnt.shape), per_tile, used[0, :1]


def _sc_mesh():
    return plsc.VectorSubcoreMesh(core_axis_name="c", subcore_axis_name="s")


def _sc_worker():
    return lax.axis_index("s") * SC_CORES + lax.axis_index("c")


def _dispatch(hp, pos1, pos2, n_rows):
    t = hp.shape[0]
    per_worker = t // SC_WORKERS
    k = per_worker // SC_CHUNK
    idx = pltpu.VMEM((k, SC_CHUNK), I32)

    @functools.partial(pl.kernel, mesh=_sc_mesh(), out_type=jax.ShapeDtypeStruct((n_rows, HALF_D), I32),
                       scratch_types=[idx, idx, pltpu.VMEM((SC_CHUNK, HALF_D), I32)], name="moe_dispatch")
    def run(h_hbm, p1_hbm, p2_hbm, xs_hbm, p1_v, p2_v, rows_v):
        wid = _sc_worker()
        pltpu.sync_copy(p1_hbm.at[wid], p1_v)
        pltpu.sync_copy(p2_hbm.at[wid], p2_v)
        for j in range(k):
            pltpu.sync_copy(h_hbm.at[pl.ds(wid * per_worker + j * SC_CHUNK, SC_CHUNK)], rows_v)
            pltpu.sync_copy(rows_v, xs_hbm.at[p1_v.at[j]])
            pltpu.sync_copy(rows_v, xs_hbm.at[p2_v.at[j]])

    return run(hp, pos1.reshape(SC_WORKERS, k, SC_CHUNK), pos2.reshape(SC_WORKERS, k, SC_CHUNK))


def _combine_gather(ys, pos1, pos2):
    t = pos1.size
    per_worker = t // SC_WORKERS
    k = per_worker // SC_CHUNK
    idx = pltpu.VMEM((k, SC_CHUNK), I32)
    out = jax.ShapeDtypeStruct((t, HALF_D), I32)

    @functools.partial(pl.kernel, mesh=_sc_mesh(), out_type=(out, out),
                       scratch_types=[idx, idx, pltpu.VMEM((SC_CHUNK, HALF_D), I32)], name="moe_combine")
    def run(ys_hbm, p1_hbm, p2_hbm, g1_hbm, g2_hbm, p1_v, p2_v, rows_v):
        wid = _sc_worker()
        pltpu.sync_copy(p1_hbm.at[wid], p1_v)
        pltpu.sync_copy(p2_hbm.at[wid], p2_v)
        for j in range(k):
            dst = pl.ds(wid * per_worker + j * SC_CHUNK, SC_CHUNK)
            pltpu.sync_copy(ys_hbm.at[p1_v.at[j]], rows_v)
            pltpu.sync_copy(rows_v, g1_hbm.at[dst])
            pltpu.sync_copy(ys_hbm.at[p2_v.at[j]], rows_v)
            pltpu.sync_copy(rows_v, g2_hbm.at[dst])

    return run(ys, pos1.reshape(SC_WORKERS, k, SC_CHUNK), pos2.reshape(SC_WORKERS, k, SC_CHUNK))


def _experts_kernel(owner_ref, valid_ref, first_ref, next_ref, slot_ref, used_ref,
                    xs_ref, wg_hbm, wu_hbm, wd_hbm, ys_ref,
                    wg_f, wu_f, wd_f, wg_bf, wu_bf, wd_bf, w_sem):
    i = pl.program_id(0)
    valid = valid_ref[i]

    def fetch(expert, slot):
        pairs = ((wg_hbm, wg_f), (wu_hbm, wu_f), (wd_hbm, wd_f))
        return [pltpu.make_async_copy(w.at[expert], buf.at[slot], w_sem.at[slot, n])
                for n, (w, buf) in enumerate(pairs)]

    @pl.when(i == 0)
    def _():
        for copy in fetch(owner_ref[0], 0):
            copy.start()

    @pl.when(first_ref[i] == 1)
    def _():
        slot = slot_ref[i]
        for copy in fetch(owner_ref[i], slot):
            copy.wait()
        wg_bf[...] = wg_f[slot].astype(BF16)
        wu_bf[...] = wu_f[slot].astype(BF16)
        wd_bf[...] = wd_f[slot].astype(BF16)

        @pl.when(next_ref[i] >= 0)
        def _():
            for copy in fetch(next_ref[i], 1 - slot):
                copy.start()

    @pl.when(i < used_ref[0])
    def _():
        row = lax.broadcasted_iota(jnp.int32, (ROW_TILE, HALF_D), 0)
        packed = jnp.where(row < valid, xs_ref[...], 0)
        hi, lo = _unpack_halves(packed)
        hi = hi.astype(BF16)
        lo = lo.astype(BF16)
        hg = _dot(hi, wg_bf[:HALF_D, :]) + _dot(lo, wg_bf[HALF_D:, :])
        hu = _dot(hi, wu_bf[:HALF_D, :]) + _dot(lo, wu_bf[HALF_D:, :])
        hid = (hg * _sigmoid(hg) * hu).astype(BF16)
        ys_ref[...] = _pack_halves(_dot(hid, wd_bf[...]))


def _experts(xs, per_tile, used, wg, wu, wd):
    n_tiles = per_tile[0].shape[0]
    tile = lambda i, *prefetch: (jnp.minimum(i, prefetch[-1][0] - 1), 0)
    hbm = pl.BlockSpec(memory_space=pl.ANY)
    up_shape, down_shape = (D_MODEL, EXPERT_FF), (EXPERT_FF, D_MODEL)
    grid_spec = pltpu.PrefetchScalarGridSpec(
        num_scalar_prefetch=len(per_tile) + 1,
        grid=(n_tiles,),
        in_specs=[pl.BlockSpec((ROW_TILE, HALF_D), tile), hbm, hbm, hbm],
        out_specs=pl.BlockSpec((ROW_TILE, HALF_D), tile),
        scratch_shapes=[pltpu.VMEM((2,) + up_shape, F32), pltpu.VMEM((2,) + up_shape, F32),
                        pltpu.VMEM((2,) + down_shape, F32),
                        pltpu.VMEM(up_shape, BF16), pltpu.VMEM(up_shape, BF16), pltpu.VMEM(down_shape, BF16),
                        pltpu.SemaphoreType.DMA((2, 3))],
    )
    return pl.pallas_call(
        _experts_kernel,
        grid_spec=grid_spec,
        out_shape=jax.ShapeDtypeStruct((n_tiles * ROW_TILE, HALF_D), I32),
        compiler_params=pltpu.CompilerParams(dimension_semantics=("arbitrary",),
                                             vmem_limit_bytes=VMEM_LIMIT),
        name="experts",
    )(*per_tile, used, xs, wg, wu, wd)


def _ple_kernel(x1_ref, g1_ref, g2_ref, route_ref, p_ref, wp_ref, pg_ref, gg_ref, wgate_ref, *rest):
    o_ref = rest[-1]
    route = route_ref[...]
    lane = lax.broadcasted_iota(jnp.int32, route.shape, 1)
    pick = lambda ln: jnp.sum(jnp.where(lane == ln, route, 0.0), axis=-1, keepdims=True)
    c1 = pick(LANE_C1)
    c2 = pick(LANE_C2)
    a_hi, a_lo = _unpack_halves(g1_ref[...])
    b_hi, b_lo = _unpack_halves(g2_ref[...])
    x = x1_ref[...] + jnp.concatenate([c1 * a_hi + c2 * b_hi, c1 * a_lo + c2 * b_lo], axis=1)
    ple = _rms(_dot(p_ref[...].astype(BF16), wp_ref[...]), pg_ref[...])
    gate = _sigmoid(_dot(_rms(x, gg_ref[...]).astype(BF16), wgate_ref[...]))
    o_ref[...] = x + gate * ple


def _ple(x1, g1, g2, route, p2d, w_ple, ple_g, gate_g, w_gate, part, out_prev):
    t = x1.shape[0]
    steps = g1.shape[0] // PLE_TM
    off = part * steps
    whole = lambda w: pl.BlockSpec((PLE_TM, w), lambda i: (i + off, 0))
    local = lambda w: pl.BlockSpec((PLE_TM, w), lambda i: (i, 0))
    full = lambda a: pl.BlockSpec(a.shape, lambda i: (0,) * a.ndim)
    in_specs = [whole(D_MODEL), local(HALF_D), local(HALF_D), whole(ROUTE_LANES), whole(PLE_DIM),
                full(w_ple), full(ple_g), full(gate_g), full(w_gate)]
    args = [x1, g1, g2, route, p2d, w_ple, ple_g, gate_g, w_gate]
    aliases = {}
    if out_prev is not None:
        in_specs.append(pl.BlockSpec(memory_space=pl.ANY))
        aliases = {len(args): 0}
        args.append(out_prev)
    return pl.pallas_call(
        _ple_kernel,
        grid=(steps,),
        in_specs=in_specs,
        out_specs=whole(D_MODEL),
        out_shape=jax.ShapeDtypeStruct((t, D_MODEL), F32),
        input_output_aliases=aliases,
        compiler_params=pltpu.CompilerParams(dimension_semantics=("parallel",),
                                             vmem_limit_bytes=VMEM_LIMIT),
        name="ple",
    )(*args)


def kernel(x, p, mix_norm_g, w_in, ret_gn_g, ret_gn_b, w_ret_o, q_norm_g, k_norm_g, attn_sinks,
           w_swa_o, w_out, ffn_norm_g, w_router_group, b_router_group, w_router_expert,
           b_router_expert, w_exp_gate, w_exp_up, w_exp_down, ple_gate_norm_g, w_ple_gate,
           w_ple, ple_norm_g):
    batch, seq, d = x.shape
    t = batch * seq
    depth = w_in.shape[0]
    n_tiles = 2 * t // ROW_TILE + N_EXPERTS
    row = lambda a: a.reshape(1, -1)
    x2d = x.reshape(t, d)
    for i in range(depth):
        pad = ROUTE_LANES - N_GROUPS - N_EXPERTS
        w_router = jnp.pad(jnp.concatenate([w_router_expert[i], w_router_group[i]], axis=1), ((0, 0), (0, pad)))
        b_router = jnp.pad(jnp.concatenate([b_router_expert[i], b_router_group[i]]), (0, pad)).reshape(1, -1)
        w_router_hi = w_router.astype(BF16)
        w_router = jnp.concatenate([w_router_hi, (w_router - w_router_hi.astype(F32)).astype(BF16)], axis=1)
        x1, hp, route, cnt = _block(x2d, row(mix_norm_g[i]), w_in[i], row(ret_gn_g[i]),
                                    row(ret_gn_b[i]), attn_sinks[i], q_norm_g[i], k_norm_g[i],
                                    w_ret_o[i].astype(BF16), w_swa_o[i].astype(BF16), w_out[i].astype(BF16),
                                    row(ffn_norm_g[i]), w_router, b_router, seq)
        base, per_tile, used = _tile_plan(cnt, n_tiles)
        pos = _route_pos(route, base)
        pos1 = pos[:, 0, :]
        pos2 = pos[:, 1, :]
        xs = _dispatch(hp, pos1, pos2, n_tiles * ROW_TILE)
        ys = _experts(xs, per_tile, used, w_exp_gate[i], w_exp_up[i], w_exp_down[i])
        out = None
        blocks = pos.shape[0] // COMBINE_PARTS
        for part in range(COMBINE_PARTS):
            sel = slice(part * blocks, (part + 1) * blocks)
            g1, g2 = _combine_gather(ys, pos1[sel], pos2[sel])
            out = _ple(x1, g1, g2, route, p[i].reshape(t, PLE_DIM), w_ple[i].astype(BF16), row(ple_norm_g[i]),
                       row(ple_gate_norm_g[i]), w_ple_gate[i].astype(BF16), part, out)
        x2d = out
    return x2d.reshape(batch, seq, d)
```

```python
import functools

import jax
import jax.numpy as jnp
import numpy as np
from jax import lax
from jax.experimental import pallas as pl
from jax.experimental.pallas import tpu as pltpu
from jax.experimental.pallas import tpu_sc as plsc

F32 = jnp.float32
BF16 = jnp.bfloat16
I32 = jnp.int32

EPS = 1e-6
D_MODEL = 1024
PLE_DIM = 256
RET_HEADS = 4
RET_DK = 128
RET_DV = 128
RET_CHUNK = 128
SWA_HEADS = 8
SWA_KV_HEADS = 2
SWA_GROUP = SWA_HEADS // SWA_KV_HEADS
SWA_HD = 64
WINDOW = 128
N_GROUPS = 4
EXPERTS_PER_GROUP = 8
N_EXPERTS = N_GROUPS * EXPERTS_PER_GROUP
EXPERT_FF = 256

RET_W = RET_HEADS * RET_DK
SWA_Q = SWA_HEADS * SWA_HD
SWA_KV = SWA_KV_HEADS * SWA_HD
Z_RET_W = 4 * RET_W
Z_SKV_W = 2 * SWA_KV
Z_GATE_W = 2 * D_MODEL
COL_SQ = Z_RET_W
COL_SKV = COL_SQ + SWA_Q
COL_GATE = COL_SKV + Z_SKV_W
IN_WIDTH = COL_GATE + Z_GATE_W

ROUTE_LANES = 128
LANE_GROUP0 = N_EXPERTS
LANE_E1, LANE_E2, LANE_C1, LANE_C2 = 32, 33, 34, 35
NEG_INF = -1e30

HALF_D = D_MODEL // 2
ROW_TILE = 512
SC_CORES = 2
SC_SUBCORES = 16
SC_WORKERS = SC_CORES * SC_SUBCORES
SC_CHUNK = 128
COMBINE_PARTS = 1

VMEM_LIMIT = 56 * 1024 * 1024

BLOCK_ROWS = 512
W_SLAB = 256
RET_ROWS = 512
MERGE_SUB = 256
ROUTE_TM = 512
PLE_TM = 512


def _rms(x, g):
    ms = jnp.mean(x * x, axis=-1, keepdims=True)
    return x * lax.rsqrt(ms + EPS) * g


def _sigmoid(x):
    return 1.0 / (1.0 + jnp.exp(-x))


def _dot(a, b):
    return jnp.dot(a, b, preferred_element_type=F32)


def _dot_nt(a, b):
    return lax.dot_general(a, b, (((1,), (1,)), ((), ())), preferred_element_type=F32)


def _dot_tn(a, b):
    return lax.dot_general(a, b, (((0,), (0,)), ((), ())), preferred_element_type=F32)


def _pack_halves(v):
    return pltpu.pack_elementwise([v[:, :HALF_D], v[:, HALF_D:]], packed_dtype=BF16)


def _unpack_halves(p):
    words = lax.bitcast_convert_type(p, jnp.uint32)
    return tuple(pltpu.unpack_elementwise(words, index=k, packed_dtype=BF16, unpacked_dtype=F32) for k in range(2))


def _retention_tables():
    f32 = np.float32
    h = RET_HEADS
    c = RET_CHUNK
    log_gamma = np.log1p(-np.exp2(f32(-5.0) - np.arange(h, dtype=f32))).astype(f32)
    pos = np.arange(c, dtype=f32)
    diff = pos[:, None] - pos[None, :]
    decay = np.where(diff[None] >= 0.0, np.exp(np.maximum(diff, f32(0.0))[None] * log_gamma[:, None, None]), f32(0.0))
    scale = f32(RET_DK ** -0.5)
    dmask = decay * scale
    zeta = np.exp((f32(c - 1.0) - pos)[None, :] * log_gamma[:, None]) * scale
    xi = np.exp((pos + f32(1.0))[None, :] * log_gamma[:, None])
    cdec = np.exp(f32(c) * log_gamma)
    bc = lambda a: np.ascontiguousarray(np.broadcast_to(a[:, :, None], (h, c, c)), dtype=f32)
    return (dmask.astype(f32), bc(zeta), bc(xi),
            np.ascontiguousarray(np.broadcast_to(cdec[:, None, None], (h, c, c)), dtype=f32))


def _swa_tables(q_g, k_g):
    f32 = np.float32
    qi = np.arange(WINDOW)[:, None]
    sj = np.arange(2 * WINDOW)[None, :]
    rel = qi + WINDOW - sj
    ok = (rel >= 0) & (rel < WINDOW)
    slopes = np.exp2(f32(-8.0) * np.arange(1, SWA_HEADS + 1, dtype=f32) / f32(SWA_HEADS)).astype(f32)
    bias = np.where(ok[None], -slopes[:, None, None] * rel.astype(f32)[None], f32(NEG_INF)).astype(f32)
    bias = bias.reshape(SWA_KV_HEADS, SWA_GROUP * WINDOW, 2 * WINDOW)
    avg = lambda heads: jnp.asarray(np.kron(np.eye(heads, dtype=f32), np.full((SWA_HD, SWA_HD), 1.0 / SWA_HD, f32)),
                                    dtype=BF16)
    qg = jnp.tile(q_g.reshape(1, SWA_HD), (1, SWA_HEADS)) * (SWA_HD ** -0.5)
    kg = jnp.tile(k_g.reshape(1, SWA_HD), (1, SWA_KV_HEADS))
    return qg, kg, avg(SWA_HEADS), avg(SWA_KV_HEADS), jnp.asarray(bias)


def _block_kernel(per_seq, sinks_ref, x_ref, xp_ref, g_ref, w_hbm, dmask_ref, zeta_ref, xi_ref, cdec_ref,
                  gng_ref, gnb_ref, qg_ref, kg_ref, bq_ref, bk_ref, bias_ref,
                  wro_ref, wso_ref, wout_ref, fg_ref, wr_ref, br_ref,
                  x1_ref, hp_ref, route_ref, cnt_ref,
                  w_ref, wtmp_ref, w_sem, zc_ref, zn_ref, yr_ref, ys_ref, state_ref, prev_ref, band_ref, tail_ref):
    g = pl.program_id(0)
    seq_start = lax.rem(g - 1, per_seq) == 0

    @pl.when(g == 0)
    def _():
        n_slabs = IN_WIDTH // W_SLAB
        slab = lambda j: pltpu.make_async_copy(w_hbm.at[:, pl.ds(j * W_SLAB, W_SLAB)], wtmp_ref.at[j % 2],
                                               w_sem.at[j % 2])
        slab(0).start()
        for j in range(n_slabs):
            if j + 1 < n_slabs:
                slab(j + 1).start()
            slab(j).wait()
            w_ref[:, j * W_SLAB:(j + 1) * W_SLAB] = wtmp_ref[j % 2].astype(BF16)
        zc_ref[...] = jnp.zeros_like(zc_ref)
        tail_ref[...] = jnp.zeros_like(tail_ref)
        state_ref[...] = jnp.zeros_like(state_ref)

    @pl.when(seq_start)
    def _():
        state_ref[...] = jnp.zeros_like(state_ref)

    band_ref[:WINDOW, :] = tail_ref[...]
    band_ref[WINDOW:, :] = zc_ref[:, COL_SKV:COL_GATE]

    h = _rms(x_ref[...], g_ref[...]).astype(BF16)

    def project(c0, width):
        zn_ref[:, c0:c0 + width] = _dot(h, w_ref[:, c0:c0 + width]).astype(BF16)

    cw = 512
    proj_pieces = [functools.partial(project, j * cw, cw) for j in range(COL_SKV // cw)]
    proj_pieces.append(functools.partial(project, COL_SKV, Z_SKV_W))
    proj_pieces += [functools.partial(project, COL_GATE + j * cw, cw) for j in range(Z_GATE_W // cw)]

    def retention_rows(t):
        r_base = t * RET_ROWS
        n_chunks = RET_ROWS // RET_CHUNK
        part = lambda c, p, hd: zc_ref[r_base + c * RET_CHUNK:r_base + (c + 1) * RET_CHUNK,
                                       p * RET_W + hd * RET_DK:p * RET_W + (hd + 1) * RET_DK]
        for hd in range(RET_HEADS):
            state = state_ref[hd]
            for c in range(n_chunks):
                prev_ref[t * n_chunks + c, hd] = state.astype(BF16)
                kz = (part(c, 1, hd).astype(F32) * zeta_ref[hd]).astype(BF16)
                state = cdec_ref[hd] * state + _dot_tn(kz, part(c, 2, hd))
            state_ref[hd] = state
        out_rows = []
        for c in range(n_chunks):
            heads = []
            for hd in range(RET_HEADS):
                q = part(c, 0, hd)
                scores = _dot_nt(q, part(c, 1, hd)) * dmask_ref[hd]
                y = _dot(scores.astype(BF16), part(c, 2, hd)) + _dot(q, prev_ref[t * n_chunks + c, hd]) * xi_ref[hd]
                mu = jnp.mean(y, axis=-1, keepdims=True)
                d = y - mu
                var = jnp.mean(d * d, axis=-1, keepdims=True)
                hs = slice(hd * RET_DV, (hd + 1) * RET_DV)
                yn = d * lax.rsqrt(var + EPS) * gng_ref[:, hs] + gnb_ref[:, hs]
                gate = part(c, 3, hd).astype(F32)
                heads.append((gate * _sigmoid(gate) * yn).astype(BF16))
            out_rows.append(jnp.concatenate(heads, axis=1))
        yr_ref[r_base:r_base + RET_ROWS, :] = jnp.concatenate(out_rows, axis=0)

    group_rows = SWA_GROUP * WINDOW
    lo_q = lax.broadcasted_iota(jnp.int32, (WINDOW, 2 * SWA_HD), 1) < SWA_HD
    lo_k = lax.broadcasted_iota(jnp.int32, (2 * WINDOW, 2 * SWA_HD), 1) < SWA_HD
    in_prev = lax.broadcasted_iota(jnp.int32, (group_rows, 2 * WINDOW), 1) < WINDOW
    row_head = lax.broadcasted_iota(jnp.int32, (group_rows, 1), 0) // WINDOW

    def both_heads(v):
        swapped = pltpu.roll(v, SWA_HD, axis=1)
        return jnp.where(lo_k, v, swapped), jnp.where(lo_k, swapped, v)

    def swa_block(n):
        rows = slice(n * WINDOW, (n + 1) * WINDOW)
        kv = band_ref[n * WINDOW:(n + 2) * WINDOW, :]
        kf = kv[:, :SWA_KV].astype(F32)
        kn = kf * lax.rsqrt(_dot((kf * kf).astype(BF16), bk_ref[...]) + EPS) * kg_ref[...]
        keys = [a.astype(BF16) for a in both_heads(kn)]
        vals = [a.astype(BF16) for a in both_heads(kv[:, SWA_KV:].astype(F32))]
        qf = zc_ref[rows, COL_SQ:COL_SKV].astype(F32)
        qn = qf * lax.rsqrt(_dot((qf * qf).astype(BF16), bq_ref[...]) + EPS) * qg_ref[...]
        outs = []
        for kh in range(SWA_KV_HEADS):
            parts = []
            for j in range(SWA_GROUP // 2):
                c0 = (kh * SWA_GROUP + 2 * j) * SWA_HD
                two = qn[:, c0:c0 + 2 * SWA_HD]
                parts += [jnp.where(lo_q, two, 0.0), jnp.where(lo_q, 0.0, two)]
            qs = jnp.concatenate(parts, axis=0).astype(BF16)
            s = _dot_nt(qs, keys[kh]) + bias_ref[kh]
            if n == 0:
                s = jnp.where(jnp.logical_and(seq_start, in_prev), NEG_INF, s)
            sink = sinks_ref[kh * SWA_GROUP]
            for gi in range(1, SWA_GROUP):
                sink = jnp.where(row_head == gi, sinks_ref[kh * SWA_GROUP + gi], sink)
            m = jnp.maximum(jnp.max(s, axis=-1, keepdims=True), sink)
            p = jnp.exp(s - m)
            denom = jnp.sum(p, axis=-1, keepdims=True) + jnp.exp(sink - m)
            o = _dot(p.astype(BF16), vals[kh]) * (1.0 / denom)
            for j in range(SWA_GROUP // 2):
                even = o[(2 * j) * WINDOW:(2 * j + 1) * WINDOW]
                odd = o[(2 * j + 1) * WINDOW:(2 * j + 2) * WINDOW]
                outs.append(jnp.where(lo_q, even, odd).astype(BF16))
        ys_ref[rows, :] = jnp.concatenate(outs, axis=1)

    mix_pieces = [functools.partial(retention_rows, t) for t in range(BLOCK_ROWS // RET_ROWS)]
    mix_pieces += [functools.partial(swa_block, n) for n in range(BLOCK_ROWS // WINDOW)]

    subs = []

    def merge_sub(s):
        subs.append(_merge_rows(slice(s * MERGE_SUB, (s + 1) * MERGE_SUB), yr_ref, ys_ref, zc_ref, xp_ref, wro_ref,
                                wso_ref, wout_ref, fg_ref, wr_ref, br_ref))

    other_pieces = mix_pieces + [functools.partial(merge_sub, s) for s in range(BLOCK_ROWS // MERGE_SUB)]

    n_other = len(other_pieces)
    for k, piece in enumerate(other_pieces):
        for proj_piece in proj_pieces[k * len(proj_pieces) // n_other:(k + 1) * len(proj_pieces) // n_other]:
            proj_piece()
        piece()
    x1_ref[...] = jnp.concatenate([s[0] for s in subs], axis=0)
    hp_ref[...] = jnp.concatenate([s[1] for s in subs], axis=0)
    route_ref[...] = jnp.concatenate([s[2] for s in subs], axis=0)
    cnt_ref[...] = sum(s[3] for s in subs)

    tail_ref[...] = zc_ref[BLOCK_ROWS - WINDOW:, COL_SKV:COL_GATE]
    zc_ref[...] = zn_ref[...]


def _merge_rows(rows, yr_ref, ys_ref, z_ref, x_ref, wro_ref, wso_ref, wout_ref, fg_ref, wr_ref, br_ref):
    a = _dot(yr_ref[rows, :], wro_ref[...])
    b = _dot(ys_ref[rows, :], wso_ref[...])
    gate_r = z_ref[rows, COL_GATE:COL_GATE + D_MODEL].astype(F32)
    gate_s = z_ref[rows, COL_GATE + D_MODEL:].astype(F32)
    merged = (_sigmoid(gate_r) * a + _sigmoid(gate_s) * b).astype(BF16)
    x1 = x_ref[rows, :] + _dot(merged, wout_ref[...])
    h2 = _rms(x1, fg_ref[...])
    packed = _pack_halves(h2)

    logits = _dot(h2.astype(BF16), wr_ref[...]) + br_ref[...]
    lane = lax.broadcasted_iota(jnp.int32, logits.shape, 1)
    big = jnp.int32(ROUTE_LANES)
    is_group = jnp.logical_and(lane >= LANE_GROUP0, lane < LANE_GROUP0 + N_GROUPS)
    gl = jnp.where(is_group, logits, NEG_INF)
    gmax = jnp.max(gl, axis=-1, keepdims=True)
    g_w = 1.0 / jnp.sum(jnp.exp(gl - gmax), axis=-1, keepdims=True)
    g_sel = jnp.min(jnp.where(gl == gmax, lane, big), axis=-1, keepdims=True) - LANE_GROUP0
    in_group = jnp.logical_and(lane < N_EXPERTS, (lane >> 3) == g_sel)
    el = jnp.where(in_group, logits, NEG_INF)
    m1 = jnp.max(el, axis=-1, keepdims=True)
    i1 = jnp.min(jnp.where(el == m1, lane, big), axis=-1, keepdims=True)
    el2 = jnp.where(lane == i1, NEG_INF, el)
    m2 = jnp.max(el2, axis=-1, keepdims=True)
    i2 = jnp.min(jnp.where(el2 == m2, lane, big), axis=-1, keepdims=True)
    e2 = jnp.exp(m2 - m1)
    c1 = g_w / (1.0 + e2)
    c2 = g_w * e2 / (1.0 + e2)
    chosen = jnp.where(jnp.logical_or(lane == i1, lane == i2), 1.0, 0.0)
    route = (chosen
             + jnp.where(lane == LANE_E1, i1.astype(F32), 0.0)
             + jnp.where(lane == LANE_E2, i2.astype(F32), 0.0)
             + jnp.where(lane == LANE_C1, c1, 0.0)
             + jnp.where(lane == LANE_C2, c2, 0.0))
    return x1, packed, route, jnp.sum(chosen, axis=0, keepdims=True)


def _block(x2d, mix_g, w_in, gn_g, gn_b, sinks, q_g, k_g, w_ret_o, w_swa_o, w_out, ffn_g, w_router, b_router,
           seq):
    t = x2d.shape[0]
    n = t // BLOCK_ROWS
    assert seq % BLOCK_ROWS == 0
    tables = _retention_tables()
    swa_tables = _swa_tables(q_g, k_g)
    consts = tables + (gn_g, gn_b) + swa_tables + (w_ret_o, w_swa_o, w_out, ffn_g, w_router, b_router)
    full = lambda a: pl.BlockSpec(a.shape, lambda g: (0,) * a.ndim)
    cur = lambda w: pl.BlockSpec((BLOCK_ROWS, w), lambda g: (jnp.minimum(g, n - 1), 0))
    prv = lambda w: pl.BlockSpec((BLOCK_ROWS, w), lambda g: (jnp.maximum(g - 1, 0), 0))
    chunks = BLOCK_ROWS // RET_CHUNK
    return pl.pallas_call(
        functools.partial(_block_kernel, seq // BLOCK_ROWS),
        grid=(n + 1,),
        in_specs=[pl.BlockSpec(memory_space=pltpu.SMEM), cur(D_MODEL), prv(D_MODEL), full(mix_g),
                  pl.BlockSpec(memory_space=pl.ANY)] + [full(a) for a in consts],
        out_specs=[prv(D_MODEL), prv(HALF_D), prv(ROUTE_LANES),
                   pl.BlockSpec((None, 1, ROUTE_LANES), lambda g: (jnp.maximum(g - 1, 0), 0, 0))],
        out_shape=[jax.ShapeDtypeStruct((t, D_MODEL), F32),
                   jax.ShapeDtypeStruct((t, HALF_D), I32),
                   jax.ShapeDtypeStruct((t, ROUTE_LANES), F32),
                   jax.ShapeDtypeStruct((n, 1, ROUTE_LANES), F32)],
        scratch_shapes=[pltpu.VMEM(w_in.shape, BF16), pltpu.VMEM((2, D_MODEL, W_SLAB), F32),
                        pltpu.SemaphoreType.DMA((2,)),
                        pltpu.VMEM((BLOCK_ROWS, IN_WIDTH), BF16), pltpu.VMEM((BLOCK_ROWS, IN_WIDTH), BF16),
                        pltpu.VMEM((BLOCK_ROWS, RET_W), BF16), pltpu.VMEM((BLOCK_ROWS, SWA_Q), BF16),
                        pltpu.VMEM((RET_HEADS, RET_DK, RET_DV), F32),
                        pltpu.VMEM((chunks, RET_HEADS, RET_DK, RET_DV), BF16),
                        pltpu.VMEM((BLOCK_ROWS + WINDOW, Z_SKV_W), BF16), pltpu.VMEM((WINDOW, Z_SKV_W), BF16)],
        compiler_params=pltpu.CompilerParams(dimension_semantics=("arbitrary",),
                                             vmem_limit_bytes=VMEM_LIMIT),
        name="block",
    )(sinks, x2d, x2d, mix_g, w_in, *consts)


def _route_pos_kernel(route_ref, base_ref, pos_ref):
    route = route_ref[...]
    lane = lax.broadcasted_iota(jnp.int32, route.shape, 1)
    lane_f = lane.astype(F32)
    chosen = jnp.where(lane < N_EXPERTS, route, 0.0)
    r = lax.broadcasted_iota(jnp.int32, (ROUTE_TM, ROUTE_TM), 0)
    c = lax.broadcasted_iota(jnp.int32, (ROUTE_TM, ROUTE_TM), 1)
    tri = jnp.where(c <= r, 1.0, 0.0).astype(BF16)
    incl = _dot(tri, chosen.astype(BF16))
    posmat = base_ref[...] + incl - 1.0
    pick = lambda ln: jnp.sum(jnp.where(lane == ln, route, 0.0), axis=-1, keepdims=True)
    pos_of = lambda e: jnp.sum(jnp.where(lane_f == e, posmat, 0.0), axis=-1, keepdims=True)
    p1 = pos_of(pick(LANE_E1))
    p2 = pos_of(pick(LANE_E2))
    rec = jnp.where(lane == 0, p1, 0.0) + jnp.where(lane == 1, p2, 0.0)
    pos_ref[...] = rec.T[:8, :].astype(I32)


def _route_pos(route, base):
    t = route.shape[0]
    n = t // ROUTE_TM
    return pl.pallas_call(
        _route_pos_kernel,
        grid=(n,),
        in_specs=[pl.BlockSpec((ROUTE_TM, ROUTE_LANES), lambda i: (i, 0)),
                  pl.BlockSpec((None, 1, ROUTE_LANES), lambda i: (i, 0, 0))],
        out_specs=pl.BlockSpec((None, 8, ROUTE_TM), lambda i: (i, 0, 0)),
        out_shape=jax.ShapeDtypeStruct((n, 8, ROUTE_TM), I32),
        compiler_params=pltpu.CompilerParams(dimension_semantics=("parallel",),
                                             vmem_limit_bytes=VMEM_LIMIT),
        name="route_pos",
    )(route, base)


def _plan_kernel(cnt_ref, base_ref, owner_ref, valid_ref, used_ref, first_ref, next_ref, slot_ref):
    counts = cnt_ref[...]
    nt = counts.shape[0]
    exact = functools.partial(jnp.dot, precision=lax.Precision.HIGHEST, preferred_element_type=F32)
    total = jnp.sum(counts, axis=0, keepdims=True)
    tiles_e = jnp.ceil(total * (1.0 / ROW_TILE))
    k = lax.broadcasted_iota(jnp.int32, (ROUTE_LANES, ROUTE_LANES), 0)
    lane = lax.broadcasted_iota(jnp.int32, (ROUTE_LANES, ROUTE_LANES), 1)
    tile_end = exact(jnp.broadcast_to(tiles_e, (8, ROUTE_LANES)), jnp.where(k <= lane, 1.0, 0.0))[0:1]
    tile_start = tile_end - tiles_e
    r = lax.broadcasted_iota(jnp.int32, (nt, nt), 0)
    c = lax.broadcasted_iota(jnp.int32, (nt, nt), 1)
    before = exact(jnp.where(c < r, 1.0, 0.0), counts)
    base_ref[...] = tile_start * ROW_TILE + before

    used = jnp.max(tile_end, axis=-1, keepdims=True)
    tile = k.astype(F32)
    clamped = jnp.minimum(tile, jnp.maximum(used - 1.0, 0.0))
    is_expert = lane < N_EXPERTS
    owner = jnp.sum(jnp.where(jnp.logical_and(is_expert, tile_end <= clamped), 1.0, 0.0), axis=-1, keepdims=True)
    owner = jnp.minimum(owner, N_EXPERTS - 1.0)
    mine = lane.astype(F32) == owner
    pick = lambda v: jnp.sum(jnp.where(mine, v, 0.0), axis=-1, keepdims=True)
    t1 = tile[:, 0:1]
    valid = jnp.clip(pick(total) - (t1 - pick(tile_start)) * ROW_TILE, 0.0, float(ROW_TILE))
    valid = jnp.where(t1 < used, valid, 0.0)
    owner_ref[...] = jnp.broadcast_to(owner, owner_ref.shape).astype(I32)
    valid_ref[...] = jnp.broadcast_to(valid, valid_ref.shape).astype(I32)
    used_ref[...] = jnp.broadcast_to(used, used_ref.shape).astype(I32)

    nonempty = jnp.where(jnp.logical_and(lane[0:1] < N_EXPERTS, tiles_e > 0.0), 1.0, 0.0)
    rank = exact(jnp.broadcast_to(nonempty, (8, ROUTE_LANES)), jnp.where(k < lane, 1.0, 0.0))[0:1]
    nonempty_col = jnp.sum(jnp.where(k == lane, jnp.broadcast_to(nonempty, k.shape), 0.0), axis=-1, keepdims=True)
    later = jnp.where(jnp.logical_and(k > lane, nonempty_col > 0.0), tile, float(ROUTE_LANES))
    nxt_e = jnp.min(later, axis=0, keepdims=True)
    nxt_e = jnp.where(nxt_e < float(ROUTE_LANES), nxt_e, -1.0)
    first = jnp.where(jnp.logical_and(t1 == pick(tile_start), t1 < used), 1.0, 0.0)
    my_rank = pick(rank)
    slot = my_rank - 2.0 * jnp.floor(my_rank * 0.5)
    first_ref[...] = jnp.broadcast_to(first, first_ref.shape).astype(I32)
    next_ref[...] = jnp.broadcast_to(pick(nxt_e), next_ref.shape).astype(I32)
    slot_ref[...] = jnp.broadcast_to(slot, slot_ref.shape).astype(I32)


def _tile_plan(cnt, n_tiles):
    assert n_tiles <= ROUTE_LANES
    nt = cnt.shape[0]
    sq = jax.ShapeDtypeStruct((ROUTE_LANES, ROUTE_LANES), I32)
    base, owner, valid, used, first, nxt, slot = pl.pallas_call(
        _plan_kernel,
        out_shape=[jax.ShapeDtypeStruct((nt, ROUTE_LANES), F32), sq, sq,
                   jax.ShapeDtypeStruct((8, ROUTE_LANES), I32), sq, sq, sq],
        name="tile_plan",
    )(cnt.reshape(nt, ROUTE_LANES))
    per_tile = [a[:n_tiles, 0] for a in (owner, valid, first, nxt, slot)]
    return base.reshape(cnt.shape), per_tile, used[0, :1]


def _sc_mesh():
    return plsc.VectorSubcoreMesh(core_axis_name="c", subcore_axis_name="s")


def _sc_worker():
    return lax.axis_index("s") * SC_CORES + lax.axis_index("c")


def _dispatch(hp, pos1, pos2, n_rows):
    t = hp.shape[0]
    per_worker = t // SC_WORKERS
    k = per_worker // SC_CHUNK
    idx = pltpu.VMEM((k, SC_CHUNK), I32)

    @functools.partial(pl.kernel, mesh=_sc_mesh(), out_type=jax.ShapeDtypeStruct((n_rows, HALF_D), I32),
                       scratch_types=[idx, idx, pltpu.VMEM((SC_CHUNK, HALF_D), I32)], name="moe_dispatch")
    def run(h_hbm, p1_hbm, p2_hbm, xs_hbm, p1_v, p2_v, rows_v):
        wid = _sc_worker()
        pltpu.sync_copy(p1_hbm.at[wid], p1_v)
        pltpu.sync_copy(p2_hbm.at[wid], p2_v)
        for j in range(k):
            pltpu.sync_copy(h_hbm.at[pl.ds(wid * per_worker + j * SC_CHUNK, SC_CHUNK)], rows_v)
            pltpu.sync_copy(rows_v, xs_hbm.at[p1_v.at[j]])
            pltpu.sync_copy(rows_v, xs_hbm.at[p2_v.at[j]])

    return run(hp, pos1.reshape(SC_WORKERS, k, SC_CHUNK), pos2.reshape(SC_WORKERS, k, SC_CHUNK))


def _combine_gather(ys, pos1, pos2):
    t = pos1.size
    per_worker = t // SC_WORKERS
    k = per_worker // SC_CHUNK
    idx = pltpu.VMEM((k, SC_CHUNK), I32)
    out = jax.ShapeDtypeStruct((t, HALF_D), I32)

    @functools.partial(pl.kernel, mesh=_sc_mesh(), out_type=(out, out),
                       scratch_types=[idx, idx, pltpu.VMEM((SC_CHUNK, HALF_D), I32)], name="moe_combine")
    def run(ys_hbm, p1_hbm, p2_hbm, g1_hbm, g2_hbm, p1_v, p2_v, rows_v):
        wid = _sc_worker()
        pltpu.sync_copy(p1_hbm.at[wid], p1_v)
        pltpu.sync_copy(p2_hbm.at[wid], p2_v)
        for j in range(k):
            dst = pl.ds(wid * per_worker + j * SC_CHUNK, SC_CHUNK)
            pltpu.sync_copy(ys_hbm.at[p1_v.at[j]], rows_v)
            pltpu.sync_copy(rows_v, g1_hbm.at[dst])
            pltpu.sync_copy(ys_hbm.at[p2_v.at[j]], rows_v)
            pltpu.sync_copy(rows_v, g2_hbm.at[dst])

    return run(ys, pos1.reshape(SC_WORKERS, k, SC_CHUNK), pos2.reshape(SC_WORKERS, k, SC_CHUNK))


def _experts_kernel(owner_ref, valid_ref, first_ref, next_ref, slot_ref, used_ref,
                    xs_ref, wg_hbm, wu_hbm, wd_hbm, ys_ref,
                    wg_f, wu_f, wd_f, wg_bf, wu_bf, wd_bf, w_sem):
    i = pl.program_id(0)
    valid = valid_ref[i]

    def fetch(expert, slot):
        pairs = ((wg_hbm, wg_f), (wu_hbm, wu_f), (wd_hbm, wd_f))
        return [pltpu.make_async_copy(w.at[expert], buf.at[slot], w_sem.at[slot, n])
                for n, (w, buf) in enumerate(pairs)]

    @pl.when(i == 0)
    def _():
        for copy in fetch(owner_ref[0], 0):
            copy.start()

    @pl.when(first_ref[i] == 1)
    def _():
        slot = slot_ref[i]
        for copy in fetch(owner_ref[i], slot):
            copy.wait()
        wg_bf[...] = wg_f[slot].astype(BF16)
        wu_bf[...] = wu_f[slot].astype(BF16)
        wd_bf[...] = wd_f[slot].astype(BF16)

        @pl.when(next_ref[i] >= 0)
        def _():
            for copy in fetch(next_ref[i], 1 - slot):
                copy.start()

    @pl.when(i < used_ref[0])
    def _():
        row = lax.broadcasted_iota(jnp.int32, (ROW_TILE, HALF_D), 0)
        packed = jnp.where(row < valid, xs_ref[...], 0)
        hi, lo = _unpack_halves(packed)
        hi = hi.astype(BF16)
        lo = lo.astype(BF16)
        hg = _dot(hi, wg_bf[:HALF_D, :]) + _dot(lo, wg_bf[HALF_D:, :])
        hu = _dot(hi, wu_bf[:HALF_D, :]) + _dot(lo, wu_bf[HALF_D:, :])
        hid = (hg * _sigmoid(hg) * hu).astype(BF16)
        ys_ref[...] = _pack_halves(_dot(hid, wd_bf[...]))


def _experts(xs, per_tile, used, wg, wu, wd):
    n_tiles = per_tile[0].shape[0]
    tile = lambda i, *prefetch: (jnp.minimum(i, prefetch[-1][0] - 1), 0)
    hbm = pl.BlockSpec(memory_space=pl.ANY)
    up_shape, down_shape = (D_MODEL, EXPERT_FF), (EXPERT_FF, D_MODEL)
    grid_spec = pltpu.PrefetchScalarGridSpec(
        num_scalar_prefetch=len(per_tile) + 1,
        grid=(n_tiles,),
        in_specs=[pl.BlockSpec((ROW_TILE, HALF_D), tile), hbm, hbm, hbm],
        out_specs=pl.BlockSpec((ROW_TILE, HALF_D), tile),
        scratch_shapes=[pltpu.VMEM((2,) + up_shape, F32), pltpu.VMEM((2,) + up_shape, F32),
                        pltpu.VMEM((2,) + down_shape, F32),
                        pltpu.VMEM(up_shape, BF16), pltpu.VMEM(up_shape, BF16), pltpu.VMEM(down_shape, BF16),
                        pltpu.SemaphoreType.DMA((2, 3))],
    )
    return pl.pallas_call(
        _experts_kernel,
        grid_spec=grid_spec,
        out_shape=jax.ShapeDtypeStruct((n_tiles * ROW_TILE, HALF_D), I32),
        compiler_params=pltpu.CompilerParams(dimension_semantics=("arbitrary",),
                                             vmem_limit_bytes=VMEM_LIMIT),
        name="experts",
    )(*per_tile, used, xs, wg, wu, wd)


def _ple_kernel(x1_ref, g1_ref, g2_ref, route_ref, p_ref, wp_ref, pg_ref, gg_ref, wgate_ref, *rest):
    o_ref = rest[-1]
    route = route_ref[...]
    lane = lax.broadcasted_iota(jnp.int32, route.shape, 1)
    pick = lambda ln: jnp.sum(jnp.where(lane == ln, route, 0.0), axis=-1, keepdims=True)
    c1 = pick(LANE_C1)
    c2 = pick(LANE_C2)
    a_hi, a_lo = _unpack_halves(g1_ref[...])
    b_hi, b_lo = _unpack_halves(g2_ref[...])
    x = x1_ref[...] + jnp.concatenate([c1 * a_hi + c2 * b_hi, c1 * a_lo + c2 * b_lo], axis=1)
    ple = _rms(_dot(p_ref[...].astype(BF16), wp_ref[...]), pg_ref[...])
    gate = _sigmoid(_dot(_rms(x, gg_ref[...]).astype(BF16), wgate_ref[...]))
    o_ref[...] = x + gate * ple


def _ple(x1, g1, g2, route, p2d, w_ple, ple_g, gate_g, w_gate, part, out_prev):
    t = x1.shape[0]
    steps = g1.shape[0] // PLE_TM
    off = part * steps
    whole = lambda w: pl.BlockSpec((PLE_TM, w), lambda i: (i + off, 0))
    local = lambda w: pl.BlockSpec((PLE_TM, w), lambda i: (i, 0))
    full = lambda a: pl.BlockSpec(a.shape, lambda i: (0,) * a.ndim)
    in_specs = [whole(D_MODEL), local(HALF_D), local(HALF_D), whole(ROUTE_LANES), whole(PLE_DIM),
                full(w_ple), full(ple_g), full(gate_g), full(w_gate)]
    args = [x1, g1, g2, route, p2d, w_ple, ple_g, gate_g, w_gate]
    aliases = {}
    if out_prev is not None:
        in_specs.append(pl.BlockSpec(memory_space=pl.ANY))
        aliases = {len(args): 0}
        args.append(out_prev)
    return pl.pallas_call(
        _ple_kernel,
        grid=(steps,),
        in_specs=in_specs,
        out_specs=whole(D_MODEL),
        out_shape=jax.ShapeDtypeStruct((t, D_MODEL), F32),
        input_output_aliases=aliases,
        compiler_params=pltpu.CompilerParams(dimension_semantics=("parallel",),
                                             vmem_limit_bytes=VMEM_LIMIT),
        name="ple",
    )(*args)


def kernel(x, p, mix_norm_g, w_in, ret_gn_g, ret_gn_b, w_ret_o, q_norm_g, k_norm_g, attn_sinks,
           w_swa_o, w_out, ffn_norm_g, w_router_group, b_router_group, w_router_expert,
           b_router_expert, w_exp_gate, w_exp_up, w_exp_down, ple_gate_norm_g, w_ple_gate,
           w_ple, ple_norm_g):
    batch, seq, d = x.shape
    t = batch * seq
    depth = w_in.shape[0]
    n_tiles = 2 * t // ROW_TILE + N_EXPERTS
    row = lambda a: a.reshape(1, -1)
    x2d = x.reshape(t, d)
    for i in range(depth):
        pad = ROUTE_LANES - N_GROUPS - N_EXPERTS
        w_router = jnp.pad(jnp.concatenate([w_router_expert[i], w_router_group[i]], axis=1), ((0, 0), (0, pad)))
        b_router = jnp.pad(jnp.concatenate([b_router_expert[i], b_router_group[i]]), (0, pad)).reshape(1, -1)
        w_router = w_router.astype(BF16)
        x1, hp, route, cnt = _block(x2d, row(mix_norm_g[i]), w_in[i], row(ret_gn_g[i]),
                                    row(ret_gn_b[i]), attn_sinks[i], q_norm_g[i], k_norm_g[i],
                                    w_ret_o[i].astype(BF16), w_swa_o[i].astype(BF16), w_out[i].astype(BF16),
                                    row(ffn_norm_g[i]), w_router, b_router, seq)
        base, per_tile, used = _tile_plan(cnt, n_tiles)
        pos = _route_pos(route, base)
        pos1 = pos[:, 0, :]
        pos2 = pos[:, 1, :]
        xs = _dispatch(hp, pos1, pos2, n_tiles * ROW_TILE)
        ys = _experts(xs, per_tile, used, w_exp_gate[i], w_exp_up[i], w_exp_down[i])
        out = None
        blocks = pos.shape[0] // COMBINE_PARTS
        for part in range(COMBINE_PARTS):
            sel = slice(part * blocks, (part + 1) * blocks)
            g1, g2 = _combine_gather(ys, pos1[sel], pos2[sel])
            out = _ple(x1, g1, g2, route, p[i].reshape(t, PLE_DIM), w_ple[i].astype(BF16), row(ple_norm_g[i]),
                       row(ple_gate_norm_g[i]), w_ple_gate[i].astype(BF16), part, out)
        x2d = out
    return x2d.reshape(batch, seq, d)
```

```python
import functools

import jax
import jax.numpy as jnp
import numpy as np
from jax import lax
from jax.experimental import pallas as pl
from jax.experimental.pallas import tpu as pltpu
from jax.experimental.pallas import tpu_sc as plsc

F32 = jnp.float32
BF16 = jnp.bfloat16
I32 = jnp.int32

EPS = 1e-6
D_MODEL = 1024
PLE_DIM = 256
RET_HEADS = 4
RET_DK = 128
RET_DV = 128
RET_CHUNK = 128
SWA_HEADS = 8
SWA_KV_HEADS = 2
SWA_GROUP = SWA_HEADS // SWA_KV_HEADS
SWA_HD = 64
WINDOW = 128
N_GROUPS = 4
EXPERTS_PER_GROUP = 8
N_EXPERTS = N_GROUPS * EXPERTS_PER_GROUP
EXPERT_FF = 256

RET_W = RET_HEADS * RET_DK
SWA_Q = SWA_HEADS * SWA_HD
SWA_KV = SWA_KV_HEADS * SWA_HD
Z_RET_W = 4 * RET_W
Z_SKV_W = 2 * SWA_KV
Z_GATE_W = 2 * D_MODEL
COL_SQ = Z_RET_W
COL_SKV = COL_SQ + SWA_Q
COL_GATE = COL_SKV + Z_SKV_W
IN_WIDTH = COL_GATE + Z_GATE_W

ROUTE_LANES = 128
LANE_GROUP0 = N_EXPERTS
LANE_E1, LANE_E2, LANE_C1, LANE_C2 = 32, 33, 34, 35
NEG_INF = -1e30

HALF_D = D_MODEL // 2
ROW_TILE = 512
SC_CORES = 2
SC_SUBCORES = 16
SC_WORKERS = SC_CORES * SC_SUBCORES
SC_CHUNK = 128
COMBINE_PARTS = 1

VMEM_LIMIT = 56 * 1024 * 1024

BLOCK_ROWS = 512
W_SLAB = 256
RET_ROWS = 512
MERGE_SUB = 256
ROUTE_TM = 512
PLE_TM = 512


def _rms(x, g):
    ms = jnp.mean(x * x, axis=-1, keepdims=True)
    return x * lax.rsqrt(ms + EPS) * g


def _sigmoid(x):
    return 1.0 / (1.0 + jnp.exp(-x))


def _dot(a, b):
    return jnp.dot(a, b, preferred_element_type=F32)


def _dot_nt(a, b):
    return lax.dot_general(a, b, (((1,), (1,)), ((), ())), preferred_element_type=F32)


def _dot_tn(a, b):
    return lax.dot_general(a, b, (((0,), (0,)), ((), ())), preferred_element_type=F32)


def _pack_halves(v):
    return pltpu.pack_elementwise([v[:, :HALF_D], v[:, HALF_D:]], packed_dtype=BF16)


def _unpack_halves(p):
    words = lax.bitcast_convert_type(p, jnp.uint32)
    return tuple(pltpu.unpack_elementwise(words, index=k, packed_dtype=BF16, unpacked_dtype=F32) for k in range(2))


def _retention_tables():
    f32 = np.float32
    h = RET_HEADS
    c = RET_CHUNK
    log_gamma = np.log1p(-np.exp2(f32(-5.0) - np.arange(h, dtype=f32))).astype(f32)
    pos = np.arange(c, dtype=f32)
    diff = pos[:, None] - pos[None, :]
    decay = np.where(diff[None] >= 0.0, np.exp(np.maximum(diff, f32(0.0))[None] * log_gamma[:, None, None]), f32(0.0))
    scale = f32(RET_DK ** -0.5)
    dmask = decay * scale
    zeta = np.exp((f32(c - 1.0) - pos)[None, :] * log_gamma[:, None]) * scale
    xi = np.exp((pos + f32(1.0))[None, :] * log_gamma[:, None])
    cdec = np.exp(f32(c) * log_gamma)
    bc = lambda a: np.ascontiguousarray(np.broadcast_to(a[:, :, None], (h, c, c)), dtype=f32)
    return (dmask.astype(f32), bc(zeta), bc(xi),
            np.ascontiguousarray(np.broadcast_to(cdec[:, None, None], (h, c, c)), dtype=f32))


def _swa_tables(q_g, k_g):
    f32 = np.float32
    qi = np.arange(WINDOW)[:, None]
    sj = np.arange(2 * WINDOW)[None, :]
    rel = qi + WINDOW - sj
    ok = (rel >= 0) & (rel < WINDOW)
    slopes = np.exp2(f32(-8.0) * np.arange(1, SWA_HEADS + 1, dtype=f32) / f32(SWA_HEADS)).astype(f32)
    bias = np.where(ok[None], -slopes[:, None, None] * rel.astype(f32)[None], f32(NEG_INF)).astype(f32)
    bias = bias.reshape(SWA_KV_HEADS, SWA_GROUP * WINDOW, 2 * WINDOW)
    avg = lambda heads: jnp.asarray(np.kron(np.eye(heads, dtype=f32), np.full((SWA_HD, SWA_HD), 1.0 / SWA_HD, f32)),
                                    dtype=BF16)
    qg = jnp.tile(q_g.reshape(1, SWA_HD), (1, SWA_HEADS)) * (SWA_HD ** -0.5)
    kg = jnp.tile(k_g.reshape(1, SWA_HD), (1, SWA_KV_HEADS))
    return qg, kg, avg(SWA_HEADS), avg(SWA_KV_HEADS), jnp.asarray(bias)


def _block_kernel(per_seq, sinks_ref, x_ref, xp_ref, g_ref, w_hbm, dmask_ref, zeta_ref, xi_ref, cdec_ref,
                  gng_ref, gnb_ref, qg_ref, kg_ref, bq_ref, bk_ref, bias_ref,
                  wro_ref, wso_ref, wout_ref, fg_ref, wr_ref, br_ref,
                  x1_ref, hp_ref, route_ref, cnt_ref,
                  w_ref, wtmp_ref, w_sem, zc_ref, zn_ref, yr_ref, ys_ref, state_ref, prev_ref, band_ref, tail_ref):
    g = pl.program_id(0)
    seq_start = lax.rem(g - 1, per_seq) == 0

    @pl.when(g == 0)
    def _():
        n_slabs = IN_WIDTH // W_SLAB
        slab = lambda j: pltpu.make_async_copy(w_hbm.at[:, pl.ds(j * W_SLAB, W_SLAB)], wtmp_ref.at[j % 2],
                                               w_sem.at[j % 2])
        slab(0).start()
        for j in range(n_slabs):
            if j + 1 < n_slabs:
                slab(j + 1).start()
            slab(j).wait()
            w_ref[:, j * W_SLAB:(j + 1) * W_SLAB] = wtmp_ref[j % 2].astype(BF16)
        tail_ref[...] = jnp.zeros_like(tail_ref)

    @pl.when(seq_start)
    def _():
        state_ref[...] = jnp.zeros_like(state_ref)

    normed = []

    def project(c0, width):
        zn_ref[:, c0:c0 + width] = _dot(normed[0], w_ref[:, c0:c0 + width]).astype(BF16)

    cw = 512
    proj_pieces = [functools.partial(project, j * cw, cw) for j in range(COL_SKV // cw)]
    proj_pieces.append(functools.partial(project, COL_SKV, Z_SKV_W))
    proj_pieces += [functools.partial(project, COL_GATE + j * cw, cw) for j in range(Z_GATE_W // cw)]

    def retention_rows(t):
        r_base = t * RET_ROWS
        n_chunks = RET_ROWS // RET_CHUNK
        part = lambda c, p, hd: zc_ref[r_base + c * RET_CHUNK:r_base + (c + 1) * RET_CHUNK,
                                       p * RET_W + hd * RET_DK:p * RET_W + (hd + 1) * RET_DK]
        for hd in range(RET_HEADS):
            state = state_ref[hd]
            for c in range(n_chunks):
                prev_ref[t * n_chunks + c, hd] = state.astype(BF16)
                kz = (part(c, 1, hd).astype(F32) * zeta_ref[hd]).astype(BF16)
                state = cdec_ref[hd] * state + _dot_tn(kz, part(c, 2, hd))
            state_ref[hd] = state
        out_rows = []
        for c in range(n_chunks):
            heads = []
            for hd in range(RET_HEADS):
                q = part(c, 0, hd)
                scores = _dot_nt(q, part(c, 1, hd)) * dmask_ref[hd]
                y = _dot(scores.astype(BF16), part(c, 2, hd)) + _dot(q, prev_ref[t * n_chunks + c, hd]) * xi_ref[hd]
                mu = jnp.mean(y, axis=-1, keepdims=True)
                d = y - mu
                var = jnp.mean(d * d, axis=-1, keepdims=True)
                hs = slice(hd * RET_DV, (hd + 1) * RET_DV)
                yn = d * lax.rsqrt(var + EPS) * gng_ref[:, hs] + gnb_ref[:, hs]
                gate = part(c, 3, hd).astype(F32)
                heads.append((gate * _sigmoid(gate) * yn).astype(BF16))
            out_rows.append(jnp.concatenate(heads, axis=1))
        yr_ref[r_base:r_base + RET_ROWS, :] = jnp.concatenate(out_rows, axis=0)

    group_rows = SWA_GROUP * WINDOW
    lo_q = lax.broadcasted_iota(jnp.int32, (WINDOW, 2 * SWA_HD), 1) < SWA_HD
    lo_k = lax.broadcasted_iota(jnp.int32, (2 * WINDOW, 2 * SWA_HD), 1) < SWA_HD
    in_prev = lax.broadcasted_iota(jnp.int32, (group_rows, 2 * WINDOW), 1) < WINDOW
    row_head = lax.broadcasted_iota(jnp.int32, (group_rows, 1), 0) // WINDOW

    def both_heads(v):
        swapped = pltpu.roll(v, SWA_HD, axis=1)
        return jnp.where(lo_k, v, swapped), jnp.where(lo_k, swapped, v)

    def swa_block(n):
        rows = slice(n * WINDOW, (n + 1) * WINDOW)
        kv = band_ref[n * WINDOW:(n + 2) * WINDOW, :]
        kf = kv[:, :SWA_KV].astype(F32)
        kn = kf * lax.rsqrt(_dot((kf * kf).astype(BF16), bk_ref[...]) + EPS) * kg_ref[...]
        keys = [a.astype(BF16) for a in both_heads(kn)]
        vals = [a.astype(BF16) for a in both_heads(kv[:, SWA_KV:].astype(F32))]
        qf = zc_ref[rows, COL_SQ:COL_SKV].astype(F32)
        qn = qf * lax.rsqrt(_dot((qf * qf).astype(BF16), bq_ref[...]) + EPS) * qg_ref[...]
        outs = []
        for kh in range(SWA_KV_HEADS):
            parts = []
            for j in range(SWA_GROUP // 2):
                c0 = (kh * SWA_GROUP + 2 * j) * SWA_HD
                two = qn[:, c0:c0 + 2 * SWA_HD]
                parts += [jnp.where(lo_q, two, 0.0), jnp.where(lo_q, 0.0, two)]
            qs = jnp.concatenate(parts, axis=0).astype(BF16)
            s = _dot_nt(qs, keys[kh]) + bias_ref[kh]
            if n == 0:
                s = jnp.where(jnp.logical_and(seq_start, in_prev), NEG_INF, s)
            sink = sinks_ref[kh * SWA_GROUP]
            for gi in range(1, SWA_GROUP):
                sink = jnp.where(row_head == gi, sinks_ref[kh * SWA_GROUP + gi], sink)
            m = jnp.maximum(jnp.max(s, axis=-1, keepdims=True), sink)
            p = jnp.exp(s - m)
            denom = jnp.sum(p, axis=-1, keepdims=True) + jnp.exp(sink - m)
            o = _dot(p.astype(BF16), vals[kh]) * (1.0 / denom)
            for j in range(SWA_GROUP // 2):
                even = o[(2 * j) * WINDOW:(2 * j + 1) * WINDOW]
                odd = o[(2 * j + 1) * WINDOW:(2 * j + 2) * WINDOW]
                outs.append(jnp.where(lo_q, even, odd).astype(BF16))
        ys_ref[rows, :] = jnp.concatenate(outs, axis=1)

    mix_pieces = [functools.partial(retention_rows, t) for t in range(BLOCK_ROWS // RET_ROWS)]
    mix_pieces += [functools.partial(swa_block, n) for n in range(BLOCK_ROWS // WINDOW)]

    subs = []

    def merge_sub(s):
        subs.append(_merge_rows(slice(s * MERGE_SUB, (s + 1) * MERGE_SUB), yr_ref, ys_ref, zc_ref, xp_ref, wro_ref,
                                wso_ref, wout_ref, fg_ref, wr_ref, br_ref))

    other_pieces = mix_pieces + [functools.partial(merge_sub, s) for s in range(BLOCK_ROWS // MERGE_SUB)]

    def emit(with_proj, with_mix):
        del normed[:], subs[:]
        if with_proj:
            normed.append(_rms(x_ref[...], g_ref[...]).astype(BF16))
        if with_mix:
            band_ref[:WINDOW, :] = tail_ref[...]
            band_ref[WINDOW:, :] = zc_ref[:, COL_SKV:COL_GATE]
        mine = other_pieces if with_mix else []
        theirs = proj_pieces if with_proj else []
        for k, piece in enumerate(mine):
            for proj_piece in theirs[k * len(theirs) // len(mine):(k + 1) * len(theirs) // len(mine)]:
                proj_piece()
            piece()
        if not with_mix:
            for proj_piece in theirs:
                proj_piece()
        if with_mix:
            x1_ref[...] = jnp.concatenate([s[0] for s in subs], axis=0)
            hp_ref[...] = jnp.concatenate([s[1] for s in subs], axis=0)
            route_ref[...] = jnp.concatenate([s[2] for s in subs], axis=0)
            cnt_ref[...] = sum(s[3] for s in subs)
            tail_ref[...] = zc_ref[BLOCK_ROWS - WINDOW:, COL_SKV:COL_GATE]
        if with_proj:
            zc_ref[...] = zn_ref[...]

    last = pl.num_programs(0) - 1
    pl.when(g == 0)(functools.partial(emit, True, False))
    pl.when(jnp.logical_and(g > 0, g < last))(functools.partial(emit, True, True))
    pl.when(g == last)(functools.partial(emit, False, True))


def _merge_rows(rows, yr_ref, ys_ref, z_ref, x_ref, wro_ref, wso_ref, wout_ref, fg_ref, wr_ref, br_ref):
    a = _dot(yr_ref[rows, :], wro_ref[...])
    b = _dot(ys_ref[rows, :], wso_ref[...])
    gate_r = z_ref[rows, COL_GATE:COL_GATE + D_MODEL].astype(F32)
    gate_s = z_ref[rows, COL_GATE + D_MODEL:].astype(F32)
    merged = (_sigmoid(gate_r) * a + _sigmoid(gate_s) * b).astype(BF16)
    x1 = x_ref[rows, :] + _dot(merged, wout_ref[...])
    h2 = _rms(x1, fg_ref[...])
    packed = _pack_halves(h2)

    logits = _dot(h2.astype(BF16), wr_ref[...]) + br_ref[...]
    lane = lax.broadcasted_iota(jnp.int32, logits.shape, 1)
    big = jnp.int32(ROUTE_LANES)
    is_group = jnp.logical_and(lane >= LANE_GROUP0, lane < LANE_GROUP0 + N_GROUPS)
    gl = jnp.where(is_group, logits, NEG_INF)
    gmax = jnp.max(gl, axis=-1, keepdims=True)
    g_w = 1.0 / jnp.sum(jnp.exp(gl - gmax), axis=-1, keepdims=True)
    g_sel = jnp.min(jnp.where(gl == gmax, lane, big), axis=-1, keepdims=True) - LANE_GROUP0
    in_group = jnp.logical_and(lane < N_EXPERTS, (lane >> 3) == g_sel)
    el = jnp.where(in_group, logits, NEG_INF)
    m1 = jnp.max(el, axis=-1, keepdims=True)
    i1 = jnp.min(jnp.where(el == m1, lane, big), axis=-1, keepdims=True)
    el2 = jnp.where(lane == i1, NEG_INF, el)
    m2 = jnp.max(el2, axis=-1, keepdims=True)
    i2 = jnp.min(jnp.where(el2 == m2, lane, big), axis=-1, keepdims=True)
    e2 = jnp.exp(m2 - m1)
    c1 = g_w / (1.0 + e2)
    c2 = g_w * e2 / (1.0 + e2)
    chosen = jnp.where(jnp.logical_or(lane == i1, lane == i2), 1.0, 0.0)
    route = (chosen
             + jnp.where(lane == LANE_E1, i1.astype(F32), 0.0)
             + jnp.where(lane == LANE_E2, i2.astype(F32), 0.0)
             + jnp.where(lane == LANE_C1, c1, 0.0)
             + jnp.where(lane == LANE_C2, c2, 0.0))
    return x1, packed, route, jnp.sum(chosen, axis=0, keepdims=True)


def _block(x2d, mix_g, w_in, gn_g, gn_b, sinks, q_g, k_g, w_ret_o, w_swa_o, w_out, ffn_g, w_router, b_router,
           seq):
    t = x2d.shape[0]
    n = t // BLOCK_ROWS
    assert seq % BLOCK_ROWS == 0
    tables = _retention_tables()
    swa_tables = _swa_tables(q_g, k_g)
    consts = tables + (gn_g, gn_b) + swa_tables + (w_ret_o, w_swa_o, w_out, ffn_g, w_router, b_router)
    full = lambda a: pl.BlockSpec(a.shape, lambda g: (0,) * a.ndim)
    cur = lambda w: pl.BlockSpec((BLOCK_ROWS, w), lambda g: (jnp.minimum(g, n - 1), 0))
    prv = lambda w: pl.BlockSpec((BLOCK_ROWS, w), lambda g: (jnp.maximum(g - 1, 0), 0))
    chunks = BLOCK_ROWS // RET_CHUNK
    return pl.pallas_call(
        functools.partial(_block_kernel, seq // BLOCK_ROWS),
        grid=(n + 1,),
        in_specs=[pl.BlockSpec(memory_space=pltpu.SMEM), cur(D_MODEL), prv(D_MODEL), full(mix_g),
                  pl.BlockSpec(memory_space=pl.ANY)] + [full(a) for a in consts],
        out_specs=[prv(D_MODEL), prv(HALF_D), prv(ROUTE_LANES),
                   pl.BlockSpec((None, 1, ROUTE_LANES), lambda g: (jnp.maximum(g - 1, 0), 0, 0))],
        out_shape=[jax.ShapeDtypeStruct((t, D_MODEL), F32),
                   jax.ShapeDtypeStruct((t, HALF_D), I32),
                   jax.ShapeDtypeStruct((t, ROUTE_LANES), F32),
                   jax.ShapeDtypeStruct((n, 1, ROUTE_LANES), F32)],
        scratch_shapes=[pltpu.VMEM(w_in.shape, BF16), pltpu.VMEM((2, D_MODEL, W_SLAB), F32),
                        pltpu.SemaphoreType.DMA((2,)),
                        pltpu.VMEM((BLOCK_ROWS, IN_WIDTH), BF16), pltpu.VMEM((BLOCK_ROWS, IN_WIDTH), BF16),
                        pltpu.VMEM((BLOCK_ROWS, RET_W), BF16), pltpu.VMEM((BLOCK_ROWS, SWA_Q), BF16),
                        pltpu.VMEM((RET_HEADS, RET_DK, RET_DV), F32),
                        pltpu.VMEM((chunks, RET_HEADS, RET_DK, RET_DV), BF16),
                        pltpu.VMEM((BLOCK_ROWS + WINDOW, Z_SKV_W), BF16), pltpu.VMEM((WINDOW, Z_SKV_W), BF16)],
        compiler_params=pltpu.CompilerParams(dimension_semantics=("arbitrary",),
                                             vmem_limit_bytes=VMEM_LIMIT),
        name="block",
    )(sinks, x2d, x2d, mix_g, w_in, *consts)


def _route_pos_kernel(route_ref, base_ref, pos_ref):
    route = route_ref[...]
    lane = lax.broadcasted_iota(jnp.int32, route.shape, 1)
    lane_f = lane.astype(F32)
    chosen = jnp.where(lane < N_EXPERTS, route, 0.0)
    r = lax.broadcasted_iota(jnp.int32, (ROUTE_TM, ROUTE_TM), 0)
    c = lax.broadcasted_iota(jnp.int32, (ROUTE_TM, ROUTE_TM), 1)
    tri = jnp.where(c <= r, 1.0, 0.0).astype(BF16)
    incl = _dot(tri, chosen.astype(BF16))
    posmat = base_ref[...] + incl - 1.0
    pick = lambda ln: jnp.sum(jnp.where(lane == ln, route, 0.0), axis=-1, keepdims=True)
    pos_of = lambda e: jnp.sum(jnp.where(lane_f == e, posmat, 0.0), axis=-1, keepdims=True)
    p1 = pos_of(pick(LANE_E1))
    p2 = pos_of(pick(LANE_E2))
    rec = jnp.where(lane == 0, p1, 0.0) + jnp.where(lane == 1, p2, 0.0)
    pos_ref[...] = rec.T[:8, :].astype(I32)


def _route_pos(route, base):
    t = route.shape[0]
    n = t // ROUTE_TM
    return pl.pallas_call(
        _route_pos_kernel,
        grid=(n,),
        in_specs=[pl.BlockSpec((ROUTE_TM, ROUTE_LANES), lambda i: (i, 0)),
                  pl.BlockSpec((None, 1, ROUTE_LANES), lambda i: (i, 0, 0))],
        out_specs=pl.BlockSpec((None, 8, ROUTE_TM), lambda i: (i, 0, 0)),
        out_shape=jax.ShapeDtypeStruct((n, 8, ROUTE_TM), I32),
        compiler_params=pltpu.CompilerParams(dimension_semantics=("parallel",),
                                             vmem_limit_bytes=VMEM_LIMIT),
        name="route_pos",
    )(route, base)


def _plan_kernel(cnt_ref, base_ref, owner_ref, valid_ref, used_ref, first_ref, next_ref, slot_ref):
    counts = cnt_ref[...]
    nt = counts.shape[0]
    exact = functools.partial(jnp.dot, precision=lax.Precision.HIGHEST, preferred_element_type=F32)
    total = jnp.sum(counts, axis=0, keepdims=True)
    tiles_e = jnp.ceil(total * (1.0 / ROW_TILE))
    k = lax.broadcasted_iota(jnp.int32, (ROUTE_LANES, ROUTE_LANES), 0)
    lane = lax.broadcasted_iota(jnp.int32, (ROUTE_LANES, ROUTE_LANES), 1)
    tile_end = exact(jnp.broadcast_to(tiles_e, (8, ROUTE_LANES)), jnp.where(k <= lane, 1.0, 0.0))[0:1]
    tile_start = tile_end - tiles_e
    r = lax.broadcasted_iota(jnp.int32, (nt, nt), 0)
    c = lax.broadcasted_iota(jnp.int32, (nt, nt), 1)
    before = exact(jnp.where(c < r, 1.0, 0.0), counts)
    base_ref[...] = tile_start * ROW_TILE + before

    used = jnp.max(tile_end, axis=-1, keepdims=True)
    tile = k.astype(F32)
    clamped = jnp.minimum(tile, jnp.maximum(used - 1.0, 0.0))
    is_expert = lane < N_EXPERTS
    owner = jnp.sum(jnp.where(jnp.logical_and(is_expert, tile_end <= clamped), 1.0, 0.0), axis=-1, keepdims=True)
    owner = jnp.minimum(owner, N_EXPERTS - 1.0)
    mine = lane.astype(F32) == owner
    pick = lambda v: jnp.sum(jnp.where(mine, v, 0.0), axis=-1, keepdims=True)
    t1 = tile[:, 0:1]
    valid = jnp.clip(pick(total) - (t1 - pick(tile_start)) * ROW_TILE, 0.0, float(ROW_TILE))
    valid = jnp.where(t1 < used, valid, 0.0)
    owner_ref[...] = jnp.broadcast_to(owner, owner_ref.shape).astype(I32)
    valid_ref[...] = jnp.broadcast_to(valid, valid_ref.shape).astype(I32)
    used_ref[...] = jnp.broadcast_to(used, used_ref.shape).astype(I32)

    nonempty = jnp.where(jnp.logical_and(lane[0:1] < N_EXPERTS, tiles_e > 0.0), 1.0, 0.0)
    rank = exact(jnp.broadcast_to(nonempty, (8, ROUTE_LANES)), jnp.where(k < lane, 1.0, 0.0))[0:1]
    nonempty_col = jnp.sum(jnp.where(k == lane, jnp.broadcast_to(nonempty, k.shape), 0.0), axis=-1, keepdims=True)
    later = jnp.where(jnp.logical_and(k > lane, nonempty_col > 0.0), tile, float(ROUTE_LANES))
    nxt_e = jnp.min(later, axis=0, keepdims=True)
    nxt_e = jnp.where(nxt_e < float(ROUTE_LANES), nxt_e, -1.0)
    first = jnp.where(jnp.logical_and(t1 == pick(tile_start), t1 < used), 1.0, 0.0)
    my_rank = pick(rank)
    slot = my_rank - 2.0 * jnp.floor(my_rank * 0.5)
    first_ref[...] = jnp.broadcast_to(first, first_ref.shape).astype(I32)
    next_ref[...] = jnp.broadcast_to(pick(nxt_e), next_ref.shape).astype(I32)
    slot_ref[...] = jnp.broadcast_to(slot, slot_ref.shape).astype(I32)


def _tile_plan(cnt, n_tiles):
    assert n_tiles <= ROUTE_LANES
    nt = cnt.shape[0]
    sq = jax.ShapeDtypeStruct((ROUTE_LANES, ROUTE_LANES), I32)
    base, owner, valid, used, first, nxt, slot = pl.pallas_call(
        _plan_kernel,
        out_shape=[jax.ShapeDtypeStruct((nt, ROUTE_LANES), F32), sq, sq,
                   jax.ShapeDtypeStruct((8, ROUTE_LANES), I32), sq, sq, sq],
        name="tile_plan",
    )(cnt.reshape(nt, ROUTE_LANES))
    per_tile = [a[:n_tiles, 0] for a in (owner, valid, first, nxt, slot)]
    return base.reshape(cnt.shape), per_tile, used[0, :1]


def _sc_mesh():
    return plsc.VectorSubcoreMesh(core_axis_name="c", subcore_axis_name="s")


def _sc_worker():
    return lax.axis_index("s") * SC_CORES + lax.axis_index("c")


def _dispatch(hp, pos1, pos2, n_rows):
    t = hp.shape[0]
    per_worker = t // SC_WORKERS
    k = per_worker // SC_CHUNK
    idx = pltpu.VMEM((k, SC_CHUNK), I32)

    @functools.partial(pl.kernel, mesh=_sc_mesh(), out_type=jax.ShapeDtypeStruct((n_rows, HALF_D), I32),
                       scratch_types=[idx, idx, pltpu.VMEM((SC_CHUNK, HALF_D), I32)], name="moe_dispatch")
    def run(h_hbm, p1_hbm, p2_hbm, xs_hbm, p1_v, p2_v, rows_v):
        wid = _sc_worker()
        pltpu.sync_copy(p1_hbm.at[wid], p1_v)
        pltpu.sync_copy(p2_hbm.at[wid], p2_v)
        for j in range(k):
            pltpu.sync_copy(h_hbm.at[pl.ds(wid * per_worker + j * SC_CHUNK, SC_CHUNK)], rows_v)
            pltpu.sync_copy(rows_v, xs_hbm.at[p1_v.at[j]])
            pltpu.sync_copy(rows_v, xs_hbm.at[p2_v.at[j]])

    return run(hp, pos1.reshape(SC_WORKERS, k, SC_CHUNK), pos2.reshape(SC_WORKERS, k, SC_CHUNK))


def _combine_gather(ys, pos1, pos2):
    t = pos1.size
    per_worker = t // SC_WORKERS
    k = per_worker // SC_CHUNK
    idx = pltpu.VMEM((k, SC_CHUNK), I32)
    out = jax.ShapeDtypeStruct((t, HALF_D), I32)

    @functools.partial(pl.kernel, mesh=_sc_mesh(), out_type=(out, out),
                       scratch_types=[idx, idx, pltpu.VMEM((SC_CHUNK, HALF_D), I32)], name="moe_combine")
    def run(ys_hbm, p1_hbm, p2_hbm, g1_hbm, g2_hbm, p1_v, p2_v, rows_v):
        wid = _sc_worker()
        pltpu.sync_copy(p1_hbm.at[wid], p1_v)
        pltpu.sync_copy(p2_hbm.at[wid], p2_v)
        for j in range(k):
            dst = pl.ds(wid * per_worker + j * SC_CHUNK, SC_CHUNK)
            pltpu.sync_copy(ys_hbm.at[p1_v.at[j]], rows_v)
            pltpu.sync_copy(rows_v, g1_hbm.at[dst])
            pltpu.sync_copy(ys_hbm.at[p2_v.at[j]], rows_v)
            pltpu.sync_copy(rows_v, g2_hbm.at[dst])

    return run(ys, pos1.reshape(SC_WORKERS, k, SC_CHUNK), pos2.reshape(SC_WORKERS, k, SC_CHUNK))


def _experts_kernel(owner_ref, valid_ref, first_ref, next_ref, slot_ref, used_ref,
                    xs_ref, wg_hbm, wu_hbm, wd_hbm, ys_ref,
                    wg_f, wu_f, wd_f, wg_bf, wu_bf, wd_bf, w_sem):
    i = pl.program_id(0)
    valid = valid_ref[i]

    def fetch(expert, slot):
        pairs = ((wg_hbm, wg_f), (wu_hbm, wu_f), (wd_hbm, wd_f))
        return [pltpu.make_async_copy(w.at[expert], buf.at[slot], w_sem.at[slot, n])
                for n, (w, buf) in enumerate(pairs)]

    @pl.when(i == 0)
    def _():
        for copy in fetch(owner_ref[0], 0):
            copy.start()

    @pl.when(first_ref[i] == 1)
    def _():
        slot = slot_ref[i]
        for copy in fetch(owner_ref[i], slot):
            copy.wait()
        wg_bf[...] = wg_f[slot].astype(BF16)
        wu_bf[...] = wu_f[slot].astype(BF16)
        wd_bf[...] = wd_f[slot].astype(BF16)

        @pl.when(next_ref[i] >= 0)
        def _():
            for copy in fetch(next_ref[i], 1 - slot):
                copy.start()

    @pl.when(i < used_ref[0])
    def _():
        row = lax.broadcasted_iota(jnp.int32, (ROW_TILE, HALF_D), 0)
        packed = jnp.where(row < valid, xs_ref[...], 0)
        hi, lo = _unpack_halves(packed)
        hi = hi.astype(BF16)
        lo = lo.astype(BF16)
        hg = _dot(hi, wg_bf[:HALF_D, :]) + _dot(lo, wg_bf[HALF_D:, :])
        hu = _dot(hi, wu_bf[:HALF_D, :]) + _dot(lo, wu_bf[HALF_D:, :])
        hid = (hg * _sigmoid(hg) * hu).astype(BF16)
        ys_ref[...] = _pack_halves(_dot(hid, wd_bf[...]))


def _experts(xs, per_tile, used, wg, wu, wd):
    n_tiles = per_tile[0].shape[0]
    tile = lambda i, *prefetch: (jnp.minimum(i, prefetch[-1][0] - 1), 0)
    hbm = pl.BlockSpec(memory_space=pl.ANY)
    up_shape, down_shape = (D_MODEL, EXPERT_FF), (EXPERT_FF, D_MODEL)
    grid_spec = pltpu.PrefetchScalarGridSpec(
        num_scalar_prefetch=len(per_tile) + 1,
        grid=(n_tiles,),
        in_specs=[pl.BlockSpec((ROW_TILE, HALF_D), tile), hbm, hbm, hbm],
        out_specs=pl.BlockSpec((ROW_TILE, HALF_D), tile),
        scratch_shapes=[pltpu.VMEM((2,) + up_shape, F32), pltpu.VMEM((2,) + up_shape, F32),
                        pltpu.VMEM((2,) + down_shape, F32),
                        pltpu.VMEM(up_shape, BF16), pltpu.VMEM(up_shape, BF16), pltpu.VMEM(down_shape, BF16),
                        pltpu.SemaphoreType.DMA((2, 3))],
    )
    return pl.pallas_call(
        _experts_kernel,
        grid_spec=grid_spec,
        out_shape=jax.ShapeDtypeStruct((n_tiles * ROW_TILE, HALF_D), I32),
        compiler_params=pltpu.CompilerParams(dimension_semantics=("arbitrary",),
                                             vmem_limit_bytes=VMEM_LIMIT),
        name="experts",
    )(*per_tile, used, xs, wg, wu, wd)


def _ple_kernel(x1_ref, g1_ref, g2_ref, route_ref, p_ref, wp_ref, pg_ref, gg_ref, wgate_ref, *rest):
    o_ref = rest[-1]
    route = route_ref[...]
    lane = lax.broadcasted_iota(jnp.int32, route.shape, 1)
    pick = lambda ln: jnp.sum(jnp.where(lane == ln, route, 0.0), axis=-1, keepdims=True)
    c1 = pick(LANE_C1)
    c2 = pick(LANE_C2)
    a_hi, a_lo = _unpack_halves(g1_ref[...])
    b_hi, b_lo = _unpack_halves(g2_ref[...])
    x = x1_ref[...] + jnp.concatenate([c1 * a_hi + c2 * b_hi, c1 * a_lo + c2 * b_lo], axis=1)
    ple = _rms(_dot(p_ref[...].astype(BF16), wp_ref[...]), pg_ref[...])
    gate = _sigmoid(_dot(_rms(x, gg_ref[...]).astype(BF16), wgate_ref[...]))
    o_ref[...] = x + gate * ple


def _ple(x1, g1, g2, route, p2d, w_ple, ple_g, gate_g, w_gate, part, out_prev):
    t = x1.shape[0]
    steps = g1.shape[0] // PLE_TM
    off = part * steps
    whole = lambda w: pl.BlockSpec((PLE_TM, w), lambda i: (i + off, 0))
    local = lambda w: pl.BlockSpec((PLE_TM, w), lambda i: (i, 0))
    full = lambda a: pl.BlockSpec(a.shape, lambda i: (0,) * a.ndim)
    in_specs = [whole(D_MODEL), local(HALF_D), local(HALF_D), whole(ROUTE_LANES), whole(PLE_DIM),
                full(w_ple), full(ple_g), full(gate_g), full(w_gate)]
    args = [x1, g1, g2, route, p2d, w_ple, ple_g, gate_g, w_gate]
    aliases = {}
    if out_prev is not None:
        in_specs.append(pl.BlockSpec(memory_space=pl.ANY))
        aliases = {len(args): 0}
        args.append(out_prev)
    return pl.pallas_call(
        _ple_kernel,
        grid=(steps,),
        in_specs=in_specs,
        out_specs=whole(D_MODEL),
        out_shape=jax.ShapeDtypeStruct((t, D_MODEL), F32),
        input_output_aliases=aliases,
        compiler_params=pltpu.CompilerParams(dimension_semantics=("parallel",),
                                             vmem_limit_bytes=VMEM_LIMIT),
        name="ple",
    )(*args)


def kernel(x, p, mix_norm_g, w_in, ret_gn_g, ret_gn_b, w_ret_o, q_norm_g, k_norm_g, attn_sinks,
           w_swa_o, w_out, ffn_norm_g, w_router_group, b_router_group, w_router_expert,
           b_router_expert, w_exp_gate, w_exp_up, w_exp_down, ple_gate_norm_g, w_ple_gate,
           w_ple, ple_norm_g):
    batch, seq, d = x.shape
    t = batch * seq
    depth = w_in.shape[0]
    n_tiles = 2 * t // ROW_TILE + N_EXPERTS
    row = lambda a: a.reshape(1, -1)
    x2d = x.reshape(t, d)
    for i in range(depth):
        pad = ROUTE_LANES - N_GROUPS - N_EXPERTS
        w_router = jnp.pad(jnp.concatenate([w_router_expert[i], w_router_group[i]], axis=1), ((0, 0), (0, pad)))
        b_router = jnp.pad(jnp.concatenate([b_router_expert[i], b_router_group[i]]), (0, pad)).reshape(1, -1)
        w_router = w_router.astype(BF16)
        x1, hp, route, cnt = _block(x2d, row(mix_norm_g[i]), w_in[i], row(ret_gn_g[i]),
                                    row(ret_gn_b[i]), attn_sinks[i], q_norm_g[i], k_norm_g[i],
                                    w_ret_o[i].astype(BF16), w_swa_o[i].astype(BF16), w_out[i].astype(BF16),
                                    row(ffn_norm_g[i]), w_router, b_router, seq)
        base, per_tile, used = _tile_plan(cnt, n_tiles)
        pos = _route_pos(route, base)
        pos1 = pos[:, 0, :]
        pos2 = pos[:, 1, :]
        xs = _dispatch(hp, pos1, pos2, n_tiles * ROW_TILE)
        ys = _experts(xs, per_tile, used, w_exp_gate[i], w_exp_up[i], w_exp_down[i])
        out = None
        blocks = pos.shape[0] // COMBINE_PARTS
        for part in range(COMBINE_PARTS):
            sel = slice(part * blocks, (part + 1) * blocks)
            g1, g2 = _combine_gather(ys, pos1[sel], pos2[sel])
            out = _ple(x1, g1, g2, route, p[i].reshape(t, PLE_DIM), w_ple[i].astype(BF16), row(ple_norm_g[i]),
                       row(ple_gate_norm_g[i]), w_ple_gate[i].astype(BF16), part, out)
        x2d = out
    return x2d.reshape(batch, seq, d)
```

```python
import functools

import jax
import jax.numpy as jnp
import numpy as np
from jax import lax
from jax.experimental import pallas as pl
from jax.experimental.pallas import tpu as pltpu
from jax.experimental.pallas import tpu_sc as plsc

F32 = jnp.float32
BF16 = jnp.bfloat16
I32 = jnp.int32

EPS = 1e-6
D_MODEL = 1024
PLE_DIM = 256
RET_HEADS = 4
RET_DK = 128
RET_DV = 128
RET_CHUNK = 128
SWA_HEADS = 8
SWA_KV_HEADS = 2
SWA_GROUP = SWA_HEADS // SWA_KV_HEADS
SWA_HD = 64
WINDOW = 128
N_GROUPS = 4
EXPERTS_PER_GROUP = 8
N_EXPERTS = N_GROUPS * EXPERTS_PER_GROUP
EXPERT_FF = 256

RET_W = RET_HEADS * RET_DK
SWA_Q = SWA_HEADS * SWA_HD
SWA_KV = SWA_KV_HEADS * SWA_HD
Z_RET_W = 4 * RET_W
Z_SKV_W = 2 * SWA_KV
Z_GATE_W = 2 * D_MODEL
COL_SQ = Z_RET_W
COL_SKV = COL_SQ + SWA_Q
COL_GATE = COL_SKV + Z_SKV_W
IN_WIDTH = COL_GATE + Z_GATE_W

ROUTE_LANES = 128
LANE_GROUP0 = N_EXPERTS
LANE_E1, LANE_E2, LANE_C1, LANE_C2 = 32, 33, 34, 35
LANE_R1, LANE_R2 = 36, 37
LANE_SECOND0 = 64
NEG_INF = -1e30

HALF_D = D_MODEL // 2
ROW_TILE = 512
SC_CORES = 2
SC_SUBCORES = 16
SC_WORKERS = SC_CORES * SC_SUBCORES
SC_CHUNK = 128

VMEM_LIMIT = 56 * 1024 * 1024

BLOCK_ROWS = 512
W_SLAB = 256
RET_ROWS = 512
MERGE_SUB = 256
PLE_TM = 512


def _rms(x, g):
    ms = jnp.mean(x * x, axis=-1, keepdims=True)
    return x * lax.rsqrt(ms + EPS) * g


def _sigmoid(x):
    return 1.0 / (1.0 + jnp.exp(-x))


def _dot(a, b):
    return jnp.dot(a, b, preferred_element_type=F32)


def _dot_nt(a, b):
    return lax.dot_general(a, b, (((1,), (1,)), ((), ())), preferred_element_type=F32)


def _dot_tn(a, b):
    return lax.dot_general(a, b, (((0,), (0,)), ((), ())), preferred_element_type=F32)


def _pack_halves(v):
    return pltpu.pack_elementwise([v[:, :HALF_D], v[:, HALF_D:]], packed_dtype=BF16)


def _unpack_halves(p):
    words = lax.bitcast_convert_type(p, jnp.uint32)
    return tuple(pltpu.unpack_elementwise(words, index=k, packed_dtype=BF16, unpacked_dtype=F32) for k in range(2))


def _retention_tables():
    f32 = np.float32
    h = RET_HEADS
    c = RET_CHUNK
    log_gamma = np.log1p(-np.exp2(f32(-5.0) - np.arange(h, dtype=f32))).astype(f32)
    pos = np.arange(c, dtype=f32)
    diff = pos[:, None] - pos[None, :]
    decay = np.where(diff[None] >= 0.0, np.exp(np.maximum(diff, f32(0.0))[None] * log_gamma[:, None, None]), f32(0.0))
    scale = f32(RET_DK ** -0.5)
    dmask = decay * scale
    zeta = np.exp((f32(c - 1.0) - pos)[None, :] * log_gamma[:, None]) * scale
    xi = np.exp((pos + f32(1.0))[None, :] * log_gamma[:, None])
    cdec = np.exp(f32(c) * log_gamma)
    bc = lambda a: np.ascontiguousarray(np.broadcast_to(a[:, :, None], (h, c, c)), dtype=f32)
    return (dmask.astype(f32), bc(zeta), bc(xi),
            np.ascontiguousarray(np.broadcast_to(cdec[:, None, None], (h, c, c)), dtype=f32))


def _swa_tables(q_g, k_g):
    f32 = np.float32
    qi = np.arange(WINDOW)[:, None]
    sj = np.arange(2 * WINDOW)[None, :]
    rel = qi + WINDOW - sj
    ok = (rel >= 0) & (rel < WINDOW)
    slopes = np.exp2(f32(-8.0) * np.arange(1, SWA_HEADS + 1, dtype=f32) / f32(SWA_HEADS)).astype(f32)
    bias = np.where(ok[None], -slopes[:, None, None] * rel.astype(f32)[None], f32(NEG_INF)).astype(f32)
    bias = bias.reshape(SWA_KV_HEADS, SWA_GROUP * WINDOW, 2 * WINDOW)
    avg = lambda heads: jnp.asarray(np.kron(np.eye(heads, dtype=f32), np.full((SWA_HD, SWA_HD), 1.0 / SWA_HD, f32)),
                                    dtype=BF16)
    qg = jnp.tile(q_g.reshape(1, SWA_HD), (1, SWA_HEADS)) * (SWA_HD ** -0.5)
    kg = jnp.tile(k_g.reshape(1, SWA_HD), (1, SWA_KV_HEADS))
    return qg, kg, avg(SWA_HEADS), avg(SWA_KV_HEADS), jnp.asarray(bias)


def _block_kernel(per_seq, sinks_ref, x_ref, xp_ref, g_ref, w_hbm, dmask_ref, zeta_ref, xi_ref, cdec_ref,
                  gng_ref, gnb_ref, qg_ref, kg_ref, bq_ref, bk_ref, bias_ref,
                  wro_ref, wso_ref, wout_ref, fg_ref, wr_ref, br_ref,
                  x1_ref, hp_ref, route_ref, cnt_ref,
                  w_ref, wtmp_ref, w_sem, zc_ref, zn_ref, yr_ref, ys_ref, state_ref, prev_ref, band_ref, tail_ref):
    g = pl.program_id(0)
    seq_start = lax.rem(g - 1, per_seq) == 0

    @pl.when(g == 0)
    def _():
        n_slabs = IN_WIDTH // W_SLAB
        slab = lambda j: pltpu.make_async_copy(w_hbm.at[:, pl.ds(j * W_SLAB, W_SLAB)], wtmp_ref.at[j % 2],
                                               w_sem.at[j % 2])
        slab(0).start()
        for j in range(n_slabs):
            if j + 1 < n_slabs:
                slab(j + 1).start()
            slab(j).wait()
            w_ref[:, j * W_SLAB:(j + 1) * W_SLAB] = wtmp_ref[j % 2].astype(BF16)
        zc_ref[...] = jnp.zeros_like(zc_ref)
        tail_ref[...] = jnp.zeros_like(tail_ref)
        state_ref[...] = jnp.zeros_like(state_ref)

    @pl.when(seq_start)
    def _():
        state_ref[...] = jnp.zeros_like(state_ref)

    band_ref[:WINDOW, :] = tail_ref[...]
    band_ref[WINDOW:, :] = zc_ref[:, COL_SKV:COL_GATE]

    h = _rms(x_ref[...], g_ref[...]).astype(BF16)

    def project(c0, width):
        zn_ref[:, c0:c0 + width] = _dot(h, w_ref[:, c0:c0 + width]).astype(BF16)

    cw = 512
    proj_pieces = [functools.partial(project, j * cw, cw) for j in range(COL_SKV // cw)]
    proj_pieces.append(functools.partial(project, COL_SKV, Z_SKV_W))
    proj_pieces += [functools.partial(project, COL_GATE + j * cw, cw) for j in range(Z_GATE_W // cw)]

    def retention_rows(t):
        r_base = t * RET_ROWS
        n_chunks = RET_ROWS // RET_CHUNK
        part = lambda c, p, hd: zc_ref[r_base + c * RET_CHUNK:r_base + (c + 1) * RET_CHUNK,
                                       p * RET_W + hd * RET_DK:p * RET_W + (hd + 1) * RET_DK]
        for hd in range(RET_HEADS):
            state = state_ref[hd]
            for c in range(n_chunks):
                prev_ref[t * n_chunks + c, hd] = state.astype(BF16)
                kz = (part(c, 1, hd).astype(F32) * zeta_ref[hd]).astype(BF16)
                state = cdec_ref[hd] * state + _dot_tn(kz, part(c, 2, hd))
            state_ref[hd] = state
        out_rows = []
        for c in range(n_chunks):
            heads = []
            for hd in range(RET_HEADS):
                q = part(c, 0, hd)
                scores = _dot_nt(q, part(c, 1, hd)) * dmask_ref[hd]
                y = _dot(scores.astype(BF16), part(c, 2, hd)) + _dot(q, prev_ref[t * n_chunks + c, hd]) * xi_ref[hd]
                mu = jnp.mean(y, axis=-1, keepdims=True)
                d = y - mu
                var = jnp.mean(d * d, axis=-1, keepdims=True)
                hs = slice(hd * RET_DV, (hd + 1) * RET_DV)
                yn = d * lax.rsqrt(var + EPS) * gng_ref[:, hs] + gnb_ref[:, hs]
                gate = part(c, 3, hd).astype(F32)
                heads.append((gate * _sigmoid(gate) * yn).astype(BF16))
            out_rows.append(jnp.concatenate(heads, axis=1))
        yr_ref[r_base:r_base + RET_ROWS, :] = jnp.concatenate(out_rows, axis=0)

    group_rows = SWA_GROUP * WINDOW
    lo_q = lax.broadcasted_iota(jnp.int32, (WINDOW, 2 * SWA_HD), 1) < SWA_HD
    lo_k = lax.broadcasted_iota(jnp.int32, (2 * WINDOW, 2 * SWA_HD), 1) < SWA_HD
    in_prev = lax.broadcasted_iota(jnp.int32, (group_rows, 2 * WINDOW), 1) < WINDOW
    row_head = lax.broadcasted_iota(jnp.int32, (group_rows, 1), 0) // WINDOW

    def both_heads(v):
        swapped = pltpu.roll(v, SWA_HD, axis=1)
        return jnp.where(lo_k, v, swapped), jnp.where(lo_k, swapped, v)

    def swa_block(n):
        rows = slice(n * WINDOW, (n + 1) * WINDOW)
        kv = band_ref[n * WINDOW:(n + 2) * WINDOW, :]
        kf = kv[:, :SWA_KV].astype(F32)
        kn = kf * lax.rsqrt(_dot((kf * kf).astype(BF16), bk_ref[...]) + EPS) * kg_ref[...]
        keys = [a.astype(BF16) for a in both_heads(kn)]
        vals = [a.astype(BF16) for a in both_heads(kv[:, SWA_KV:].astype(F32))]
        qf = zc_ref[rows, COL_SQ:COL_SKV].astype(F32)
        qn = qf * lax.rsqrt(_dot((qf * qf).astype(BF16), bq_ref[...]) + EPS) * qg_ref[...]
        outs = []
        for kh in range(SWA_KV_HEADS):
            parts = []
            for j in range(SWA_GROUP // 2):
                c0 = (kh * SWA_GROUP + 2 * j) * SWA_HD
                two = qn[:, c0:c0 + 2 * SWA_HD]
                parts += [jnp.where(lo_q, two, 0.0), jnp.where(lo_q, 0.0, two)]
            qs = jnp.concatenate(parts, axis=0).astype(BF16)
            s = _dot_nt(qs, keys[kh]) + bias_ref[kh]
            if n == 0:
                s = jnp.where(jnp.logical_and(seq_start, in_prev), NEG_INF, s)
            sink = sinks_ref[kh * SWA_GROUP]
            for gi in range(1, SWA_GROUP):
                sink = jnp.where(row_head == gi, sinks_ref[kh * SWA_GROUP + gi], sink)
            m = jnp.maximum(jnp.max(s, axis=-1, keepdims=True), sink)
            p = jnp.exp(s - m)
            denom = jnp.sum(p, axis=-1, keepdims=True) + jnp.exp(sink - m)
            o = _dot(p.astype(BF16), vals[kh]) * (1.0 / denom)
            for j in range(SWA_GROUP // 2):
                even = o[(2 * j) * WINDOW:(2 * j + 1) * WINDOW]
                odd = o[(2 * j + 1) * WINDOW:(2 * j + 2) * WINDOW]
                outs.append(jnp.where(lo_q, even, odd).astype(BF16))
        ys_ref[rows, :] = jnp.concatenate(outs, axis=1)

    mix_pieces = [functools.partial(retention_rows, t) for t in range(BLOCK_ROWS // RET_ROWS)]
    mix_pieces += [functools.partial(swa_block, n) for n in range(BLOCK_ROWS // WINDOW)]

    subs = []

    def merge_sub(s):
        subs.append(_merge_rows(slice(s * MERGE_SUB, (s + 1) * MERGE_SUB), yr_ref, ys_ref, zc_ref, xp_ref, wro_ref,
                                wso_ref, wout_ref, fg_ref, wr_ref, br_ref))

    other_pieces = mix_pieces + [functools.partial(merge_sub, s) for s in range(BLOCK_ROWS // MERGE_SUB)]

    n_other = len(other_pieces)
    for k, piece in enumerate(other_pieces):
        for proj_piece in proj_pieces[k * len(proj_pieces) // n_other:(k + 1) * len(proj_pieces) // n_other]:
            proj_piece()
        piece()
    x1_ref[...] = jnp.concatenate([s[0] for s in subs], axis=0)
    hp_ref[...] = jnp.concatenate([s[1] for s in subs], axis=0)
    route_ref[...] = jnp.concatenate([s[2] for s in subs], axis=0)
    for s, sub in enumerate(subs):
        cnt_ref[s] = sub[3]

    tail_ref[...] = zc_ref[BLOCK_ROWS - WINDOW:, COL_SKV:COL_GATE]
    zc_ref[...] = zn_ref[...]


def _merge_rows(rows, yr_ref, ys_ref, z_ref, x_ref, wro_ref, wso_ref, wout_ref, fg_ref, wr_ref, br_ref):
    a = _dot(yr_ref[rows, :], wro_ref[...])
    b = _dot(ys_ref[rows, :], wso_ref[...])
    gate_r = z_ref[rows, COL_GATE:COL_GATE + D_MODEL].astype(F32)
    gate_s = z_ref[rows, COL_GATE + D_MODEL:].astype(F32)
    merged = (_sigmoid(gate_r) * a + _sigmoid(gate_s) * b).astype(BF16)
    x1 = x_ref[rows, :] + _dot(merged, wout_ref[...])
    h2 = _rms(x1, fg_ref[...])
    packed = _pack_halves(h2)

    logits = _dot(h2.astype(BF16), wr_ref[...]) + br_ref[...]
    lane = lax.broadcasted_iota(jnp.int32, logits.shape, 1)
    big = jnp.int32(ROUTE_LANES)
    is_group = jnp.logical_and(lane >= LANE_GROUP0, lane < LANE_GROUP0 + N_GROUPS)
    gl = jnp.where(is_group, logits, NEG_INF)
    gmax = jnp.max(gl, axis=-1, keepdims=True)
    g_w = 1.0 / jnp.sum(jnp.exp(gl - gmax), axis=-1, keepdims=True)
    g_sel = jnp.min(jnp.where(gl == gmax, lane, big), axis=-1, keepdims=True) - LANE_GROUP0
    in_group = jnp.logical_and(lane < N_EXPERTS, (lane >> 3) == g_sel)
    el = jnp.where(in_group, logits, NEG_INF)
    m1 = jnp.max(el, axis=-1, keepdims=True)
    i1 = jnp.min(jnp.where(el == m1, lane, big), axis=-1, keepdims=True)
    el2 = jnp.where(lane == i1, NEG_INF, el)
    m2 = jnp.max(el2, axis=-1, keepdims=True)
    i2 = jnp.min(jnp.where(el2 == m2, lane, big), axis=-1, keepdims=True)
    e2 = jnp.exp(m2 - m1)
    c1 = g_w / (1.0 + e2)
    c2 = g_w * e2 / (1.0 + e2)
    chosen = jnp.where(jnp.logical_or(lane == i1, lane == i2), 1.0, 0.0)
    n_rows = logits.shape[0]
    tri = jnp.where(lax.broadcasted_iota(jnp.int32, (n_rows, n_rows), 1)
                    <= lax.broadcasted_iota(jnp.int32, (n_rows, n_rows), 0), 1.0, 0.0).astype(BF16)
    incl = _dot(tri, chosen.astype(BF16))
    rank = lambda idx: jnp.sum(jnp.where(lane == idx, incl, 0.0), axis=-1, keepdims=True) - 1.0
    route = (chosen
             + jnp.where(lane == LANE_E1, i1.astype(F32), 0.0)
             + jnp.where(lane == LANE_E2, i2.astype(F32), 0.0)
             + jnp.where(lane == LANE_C1, c1, 0.0)
             + jnp.where(lane == LANE_C2, c2, 0.0)
             + jnp.where(lane == LANE_R1, rank(i1), 0.0)
             + jnp.where(lane == LANE_R2, rank(i2), 0.0)
             + jnp.where(lane == i2 + LANE_SECOND0, 1.0, 0.0))
    return x1, packed, route, jnp.sum(chosen, axis=0, keepdims=True)


def _block(x2d, mix_g, w_in, gn_g, gn_b, sinks, q_g, k_g, w_ret_o, w_swa_o, w_out, ffn_g, w_router, b_router,
           seq):
    t = x2d.shape[0]
    n = t // BLOCK_ROWS
    assert seq % BLOCK_ROWS == 0
    tables = _retention_tables()
    swa_tables = _swa_tables(q_g, k_g)
    consts = tables + (gn_g, gn_b) + swa_tables + (w_ret_o, w_swa_o, w_out, ffn_g, w_router, b_router)
    full = lambda a: pl.BlockSpec(a.shape, lambda g: (0,) * a.ndim)
    cur = lambda w: pl.BlockSpec((BLOCK_ROWS, w), lambda g: (jnp.minimum(g, n - 1), 0))
    prv = lambda w: pl.BlockSpec((BLOCK_ROWS, w), lambda g: (jnp.maximum(g - 1, 0), 0))
    chunks = BLOCK_ROWS // RET_CHUNK
    subs_per_block = BLOCK_ROWS // MERGE_SUB
    return pl.pallas_call(
        functools.partial(_block_kernel, seq // BLOCK_ROWS),
        grid=(n + 1,),
        in_specs=[pl.BlockSpec(memory_space=pltpu.SMEM), cur(D_MODEL), prv(D_MODEL), full(mix_g),
                  pl.BlockSpec(memory_space=pl.ANY)] + [full(a) for a in consts],
        out_specs=[prv(D_MODEL), prv(HALF_D), prv(ROUTE_LANES),
                   pl.BlockSpec((subs_per_block, 1, ROUTE_LANES), lambda g: (jnp.maximum(g - 1, 0), 0, 0))],
        out_shape=[jax.ShapeDtypeStruct((t, D_MODEL), F32),
                   jax.ShapeDtypeStruct((t, HALF_D), I32),
                   jax.ShapeDtypeStruct((t, ROUTE_LANES), F32),
                   jax.ShapeDtypeStruct((n * subs_per_block, 1, ROUTE_LANES), F32)],
        scratch_shapes=[pltpu.VMEM(w_in.shape, BF16), pltpu.VMEM((2, D_MODEL, W_SLAB), F32),
                        pltpu.SemaphoreType.DMA((2,)),
                        pltpu.VMEM((BLOCK_ROWS, IN_WIDTH), BF16), pltpu.VMEM((BLOCK_ROWS, IN_WIDTH), BF16),
                        pltpu.VMEM((BLOCK_ROWS, RET_W), BF16), pltpu.VMEM((BLOCK_ROWS, SWA_Q), BF16),
                        pltpu.VMEM((RET_HEADS, RET_DK, RET_DV), F32),
                        pltpu.VMEM((chunks, RET_HEADS, RET_DK, RET_DV), BF16),
                        pltpu.VMEM((BLOCK_ROWS + WINDOW, Z_SKV_W), BF16), pltpu.VMEM((WINDOW, Z_SKV_W), BF16)],
        compiler_params=pltpu.CompilerParams(dimension_semantics=("arbitrary",),
                                             vmem_limit_bytes=VMEM_LIMIT),
        name="block",
    )(sinks, x2d, x2d, mix_g, w_in, *consts)


def _plan_kernel(cnt_ref, route_ref, base_ref, owner_ref, valid_ref, used_ref, first_ref, next_ref, slot_ref,
                 pos_ref):
    counts = cnt_ref[...]
    nt = counts.shape[0]
    exact = functools.partial(jnp.dot, precision=lax.Precision.HIGHEST, preferred_element_type=F32)
    total = jnp.sum(counts, axis=0, keepdims=True)
    tiles_e = jnp.ceil(total * (1.0 / ROW_TILE))
    k = lax.broadcasted_iota(jnp.int32, (ROUTE_LANES, ROUTE_LANES), 0)
    lane = lax.broadcasted_iota(jnp.int32, (ROUTE_LANES, ROUTE_LANES), 1)
    tile_end = exact(jnp.broadcast_to(tiles_e, (8, ROUTE_LANES)), jnp.where(k <= lane, 1.0, 0.0))[0:1]
    tile_start = tile_end - tiles_e
    r = lax.broadcasted_iota(jnp.int32, (nt, nt), 0)
    c = lax.broadcasted_iota(jnp.int32, (nt, nt), 1)
    before = exact(jnp.where(c < r, 1.0, 0.0), counts)
    base_ref[...] = tile_start * ROW_TILE + before

    used = jnp.max(tile_end, axis=-1, keepdims=True)
    tile = k.astype(F32)
    clamped = jnp.minimum(tile, jnp.maximum(used - 1.0, 0.0))
    is_expert = lane < N_EXPERTS
    owner = jnp.sum(jnp.where(jnp.logical_and(is_expert, tile_end <= clamped), 1.0, 0.0), axis=-1, keepdims=True)
    owner = jnp.minimum(owner, N_EXPERTS - 1.0)
    mine = lane.astype(F32) == owner
    pick = lambda v: jnp.sum(jnp.where(mine, v, 0.0), axis=-1, keepdims=True)
    t1 = tile[:, 0:1]
    valid = jnp.clip(pick(total) - (t1 - pick(tile_start)) * ROW_TILE, 0.0, float(ROW_TILE))
    valid = jnp.where(t1 < used, valid, 0.0)
    owner_ref[...] = jnp.broadcast_to(owner, owner_ref.shape).astype(I32)
    valid_ref[...] = jnp.broadcast_to(valid, valid_ref.shape).astype(I32)
    used_ref[...] = jnp.broadcast_to(used, used_ref.shape).astype(I32)

    nonempty = jnp.where(jnp.logical_and(lane[0:1] < N_EXPERTS, tiles_e > 0.0), 1.0, 0.0)
    rank = exact(jnp.broadcast_to(nonempty, (8, ROUTE_LANES)), jnp.where(k < lane, 1.0, 0.0))[0:1]
    nonempty_col = jnp.sum(jnp.where(k == lane, jnp.broadcast_to(nonempty, k.shape), 0.0), axis=-1, keepdims=True)
    later = jnp.where(jnp.logical_and(k > lane, nonempty_col > 0.0), tile, float(ROUTE_LANES))
    nxt_e = jnp.min(later, axis=0, keepdims=True)
    nxt_e = jnp.where(nxt_e < float(ROUTE_LANES), nxt_e, -1.0)
    first = jnp.where(jnp.logical_and(t1 == pick(tile_start), t1 < used), 1.0, 0.0)
    my_rank = pick(rank)
    slot = my_rank - 2.0 * jnp.floor(my_rank * 0.5)
    first_ref[...] = jnp.broadcast_to(first, first_ref.shape).astype(I32)
    next_ref[...] = jnp.broadcast_to(pick(nxt_e), next_ref.shape).astype(I32)
    slot_ref[...] = jnp.broadcast_to(slot, slot_ref.shape).astype(I32)

    sel_row = lax.broadcasted_iota(jnp.int32, (8, ROUTE_LANES), 0)
    sel_lane = lax.broadcasted_iota(jnp.int32, (8, ROUTE_LANES), 1)
    pick_ranks = (jnp.where(jnp.logical_and(sel_row == 4, sel_lane == LANE_R1), 1.0, 0.0)
                  + jnp.where(jnp.logical_and(sel_row == 5, sel_lane == LANE_R2), 1.0, 0.0))

    def sub_positions(s, carry):
        route = route_ref[pl.ds(pl.multiple_of(s * MERGE_SUB, MERGE_SUB), MERGE_SUB), :].astype(BF16)
        base_row = jnp.where(sel_lane < N_EXPERTS, jnp.broadcast_to(base_ref[pl.ds(s, 1), :], (8, ROUTE_LANES)), 0.0)
        hi = jnp.floor(base_row * (1.0 / 256.0))
        lo = base_row - 256.0 * hi
        hi2 = pltpu.roll(hi, LANE_SECOND0, axis=1)
        lo2 = pltpu.roll(lo, LANE_SECOND0, axis=1)
        table = (jnp.where(sel_row == 0, hi, 0.0) + jnp.where(sel_row == 1, lo, 0.0)
                 + jnp.where(sel_row == 2, hi2, 0.0) + jnp.where(sel_row == 3, lo2, 0.0) + pick_ranks)
        r = _dot_nt(table.astype(BF16), route)
        both = 256.0 * r[0:1] + r[1:2]
        second = 256.0 * r[2:3] + r[3:4]
        rows = lax.broadcasted_iota(jnp.int32, r.shape, 0)
        out = jnp.where(rows == 0, both - second + r[4:5], jnp.where(rows == 1, second + r[5:6], 0.0))
        pos_ref[s] = out.astype(I32)
        return carry

    lax.fori_loop(0, nt, sub_positions, 0)


def _tile_plan(cnt, route, n_tiles):
    assert n_tiles <= ROUTE_LANES
    nt = cnt.shape[0]
    sq = jax.ShapeDtypeStruct((ROUTE_LANES, ROUTE_LANES), I32)
    _, owner, valid, used, first, nxt, slot, pos = pl.pallas_call(
        _plan_kernel,
        out_shape=[jax.ShapeDtypeStruct((nt, ROUTE_LANES), F32), sq, sq,
                   jax.ShapeDtypeStruct((8, ROUTE_LANES), I32), sq, sq, sq,
                   jax.ShapeDtypeStruct((nt, 8, MERGE_SUB), I32)],
        compiler_params=pltpu.CompilerParams(vmem_limit_bytes=VMEM_LIMIT),
        name="tile_plan",
    )(cnt.reshape(nt, ROUTE_LANES), route)
    per_tile = [a[:n_tiles, 0] for a in (owner, valid, first, nxt, slot)]
    return per_tile, used[0, :1], pos[:, 0, :], pos[:, 1, :]


def _sc_mesh():
    return plsc.VectorSubcoreMesh(core_axis_name="c", subcore_axis_name="s")


def _sc_worker():
    return lax.axis_index("s") * SC_CORES + lax.axis_index("c")


def _dispatch(hp, pos1, pos2, n_rows):
    t = hp.shape[0]
    per_worker = t // SC_WORKERS
    k = per_worker // SC_CHUNK
    idx = pltpu.VMEM((k, SC_CHUNK), I32)

    @functools.partial(pl.kernel, mesh=_sc_mesh(), out_type=jax.ShapeDtypeStruct((n_rows, HALF_D), I32),
                       scratch_types=[idx, idx, pltpu.VMEM((SC_CHUNK, HALF_D), I32)], name="moe_dispatch")
    def run(h_hbm, p1_hbm, p2_hbm, xs_hbm, p1_v, p2_v, rows_v):
        wid = _sc_worker()
        pltpu.sync_copy(p1_hbm.at[wid], p1_v)
        pltpu.sync_copy(p2_hbm.at[wid], p2_v)
        for j in range(k):
            pltpu.sync_copy(h_hbm.at[pl.ds(wid * per_worker + j * SC_CHUNK, SC_CHUNK)], rows_v)
            pltpu.sync_copy(rows_v, xs_hbm.at[p1_v.at[j]])
            pltpu.sync_copy(rows_v, xs_hbm.at[p2_v.at[j]])

    return run(hp, pos1.reshape(SC_WORKERS, k, SC_CHUNK), pos2.reshape(SC_WORKERS, k, SC_CHUNK))


def _combine_gather(ys, pos1, pos2):
    t = pos1.size
    per_worker = t // SC_WORKERS
    k = per_worker // SC_CHUNK
    idx = pltpu.VMEM((k, SC_CHUNK), I32)
    out = jax.ShapeDtypeStruct((t, HALF_D), I32)

    @functools.partial(pl.kernel, mesh=_sc_mesh(), out_type=(out, out),
                       scratch_types=[idx, idx, pltpu.VMEM((SC_CHUNK, HALF_D), I32)], name="moe_combine")
    def run(ys_hbm, p1_hbm, p2_hbm, g1_hbm, g2_hbm, p1_v, p2_v, rows_v):
        wid = _sc_worker()
        pltpu.sync_copy(p1_hbm.at[wid], p1_v)
        pltpu.sync_copy(p2_hbm.at[wid], p2_v)
        for j in range(k):
            dst = pl.ds(wid * per_worker + j * SC_CHUNK, SC_CHUNK)
            pltpu.sync_copy(ys_hbm.at[p1_v.at[j]], rows_v)
            pltpu.sync_copy(rows_v, g1_hbm.at[dst])
            pltpu.sync_copy(ys_hbm.at[p2_v.at[j]], rows_v)
            pltpu.sync_copy(rows_v, g2_hbm.at[dst])

    return run(ys, pos1.reshape(SC_WORKERS, k, SC_CHUNK), pos2.reshape(SC_WORKERS, k, SC_CHUNK))


def _experts_kernel(owner_ref, valid_ref, first_ref, next_ref, slot_ref, used_ref,
                    xs_ref, wg_hbm, wu_hbm, wd_hbm, ys_ref,
                    wg_f, wu_f, wd_f, wg_bf, wu_bf, wd_bf, w_sem):
    i = pl.program_id(0)
    valid = valid_ref[i]

    def fetch(expert, slot):
        pairs = ((wg_hbm, wg_f), (wu_hbm, wu_f), (wd_hbm, wd_f))
        return [pltpu.make_async_copy(w.at[expert], buf.at[slot], w_sem.at[slot, n])
                for n, (w, buf) in enumerate(pairs)]

    @pl.when(i == 0)
    def _():
        for copy in fetch(owner_ref[0], 0):
            copy.start()

    @pl.when(first_ref[i] == 1)
    def _():
        slot = slot_ref[i]
        for copy in fetch(owner_ref[i], slot):
            copy.wait()
        wg_bf[...] = wg_f[slot].astype(BF16)
        wu_bf[...] = wu_f[slot].astype(BF16)
        wd_bf[...] = wd_f[slot].astype(BF16)

        @pl.when(next_ref[i] >= 0)
        def _():
            for copy in fetch(next_ref[i], 1 - slot):
                copy.start()

    @pl.when(i < used_ref[0])
    def _():
        row = lax.broadcasted_iota(jnp.int32, (ROW_TILE, HALF_D), 0)
        packed = jnp.where(row < valid, xs_ref[...], 0)
        hi, lo = _unpack_halves(packed)
        hi = hi.astype(BF16)
        lo = lo.astype(BF16)
        hg = _dot(hi, wg_bf[:HALF_D, :]) + _dot(lo, wg_bf[HALF_D:, :])
        hu = _dot(hi, wu_bf[:HALF_D, :]) + _dot(lo, wu_bf[HALF_D:, :])
        hid = (hg * _sigmoid(hg) * hu).astype(BF16)
        ys_ref[...] = _pack_halves(_dot(hid, wd_bf[...]))


def _experts(xs, per_tile, used, wg, wu, wd):
    n_tiles = per_tile[0].shape[0]
    tile = lambda i, *prefetch: (jnp.minimum(i, prefetch[-1][0] - 1), 0)
    hbm = pl.BlockSpec(memory_space=pl.ANY)
    up_shape, down_shape = (D_MODEL, EXPERT_FF), (EXPERT_FF, D_MODEL)
    grid_spec = pltpu.PrefetchScalarGridSpec(
        num_scalar_prefetch=len(per_tile) + 1,
        grid=(n_tiles,),
        in_specs=[pl.BlockSpec((ROW_TILE, HALF_D), tile), hbm, hbm, hbm],
        out_specs=pl.BlockSpec((ROW_TILE, HALF_D), tile),
        scratch_shapes=[pltpu.VMEM((2,) + up_shape, F32), pltpu.VMEM((2,) + up_shape, F32),
                        pltpu.VMEM((2,) + down_shape, F32),
                        pltpu.VMEM(up_shape, BF16), pltpu.VMEM(up_shape, BF16), pltpu.VMEM(down_shape, BF16),
                        pltpu.SemaphoreType.DMA((2, 3))],
    )
    return pl.pallas_call(
        _experts_kernel,
        grid_spec=grid_spec,
        out_shape=jax.ShapeDtypeStruct((n_tiles * ROW_TILE, HALF_D), I32),
        compiler_params=pltpu.CompilerParams(dimension_semantics=("arbitrary",),
                                             vmem_limit_bytes=VMEM_LIMIT),
        name="experts",
    )(*per_tile, used, xs, wg, wu, wd)


def _ple_kernel(x1_ref, g1_ref, g2_ref, route_ref, p_ref, wp_ref, pg_ref, gg_ref, wgate_ref, o_ref):
    route = route_ref[...]
    lane = lax.broadcasted_iota(jnp.int32, route.shape, 1)
    pick = lambda ln: jnp.sum(jnp.where(lane == ln, route, 0.0), axis=-1, keepdims=True)
    c1 = pick(LANE_C1)
    c2 = pick(LANE_C2)
    a_hi, a_lo = _unpack_halves(g1_ref[...])
    b_hi, b_lo = _unpack_halves(g2_ref[...])
    x = x1_ref[...] + jnp.concatenate([c1 * a_hi + c2 * b_hi, c1 * a_lo + c2 * b_lo], axis=1)
    ple = _rms(_dot(p_ref[...].astype(BF16), wp_ref[...]), pg_ref[...])
    gate = _sigmoid(_dot(_rms(x, gg_ref[...]).astype(BF16), wgate_ref[...]))
    o_ref[...] = x + gate * ple


def _ple(x1, g1, g2, route, p2d, w_ple, ple_g, gate_g, w_gate):
    t = x1.shape[0]
    row = lambda w: pl.BlockSpec((PLE_TM, w), lambda i: (i, 0))
    full = lambda a: pl.BlockSpec(a.shape, lambda i: (0,) * a.ndim)
    return pl.pallas_call(
        _ple_kernel,
        grid=(t // PLE_TM,),
        in_specs=[row(D_MODEL), row(HALF_D), row(HALF_D), row(ROUTE_LANES), row(PLE_DIM),
                  full(w_ple), full(ple_g), full(gate_g), full(w_gate)],
        out_specs=row(D_MODEL),
        out_shape=jax.ShapeDtypeStruct((t, D_MODEL), F32),
        compiler_params=pltpu.CompilerParams(dimension_semantics=("parallel",),
                                             vmem_limit_bytes=VMEM_LIMIT),
        name="ple",
    )(x1, g1, g2, route, p2d, w_ple, ple_g, gate_g, w_gate)


def kernel(x, p, mix_norm_g, w_in, ret_gn_g, ret_gn_b, w_ret_o, q_norm_g, k_norm_g, attn_sinks,
           w_swa_o, w_out, ffn_norm_g, w_router_group, b_router_group, w_router_expert,
           b_router_expert, w_exp_gate, w_exp_up, w_exp_down, ple_gate_norm_g, w_ple_gate,
           w_ple, ple_norm_g):
    batch, seq, d = x.shape
    t = batch * seq
    depth = w_in.shape[0]
    n_tiles = 2 * t // ROW_TILE + N_EXPERTS
    row = lambda a: a.reshape(1, -1)
    x2d = x.reshape(t, d)
    for i in range(depth):
        pad = ROUTE_LANES - N_GROUPS - N_EXPERTS
        w_router = jnp.pad(jnp.concatenate([w_router_expert[i], w_router_group[i]], axis=1), ((0, 0), (0, pad)))
        b_router = jnp.pad(jnp.concatenate([b_router_expert[i], b_router_group[i]]), (0, pad)).reshape(1, -1)
        w_router = w_router.astype(BF16)
        x1, hp, route, cnt = _block(x2d, row(mix_norm_g[i]), w_in[i], row(ret_gn_g[i]),
                                    row(ret_gn_b[i]), attn_sinks[i], q_norm_g[i], k_norm_g[i],
                                    w_ret_o[i].astype(BF16), w_swa_o[i].astype(BF16), w_out[i].astype(BF16),
                                    row(ffn_norm_g[i]), w_router, b_router, seq)
        per_tile, used, pos1, pos2 = _tile_plan(cnt, route, n_tiles)
        xs = _dispatch(hp, pos1, pos2, n_tiles * ROW_TILE)
        ys = _experts(xs, per_tile, used, w_exp_gate[i], w_exp_up[i], w_exp_down[i])
        g1, g2 = _combine_gather(ys, pos1, pos2)
        x2d = _ple(x1, g1, g2, route, p[i].reshape(t, PLE_DIM), w_ple[i].astype(BF16), row(ple_norm_g[i]),
                   row(ple_gate_norm_g[i]), w_ple_gate[i].astype(BF16))
    return x2d.reshape(batch, seq, d)
```

```python
import functools

import jax
import jax.numpy as jnp
import numpy as np
from jax import lax
from jax.experimental import pallas as pl
from jax.experimental.pallas import tpu as pltpu
from jax.experimental.pallas import tpu_sc as plsc

F32 = jnp.float32
BF16 = jnp.bfloat16
I32 = jnp.int32

EPS = 1e-6
D_MODEL = 1024
PLE_DIM = 256
RET_HEADS = 4
RET_DK = 128
RET_DV = 128
RET_CHUNK = 128
SWA_HEADS = 8
SWA_KV_HEADS = 2
SWA_GROUP = SWA_HEADS // SWA_KV_HEADS
SWA_HD = 64
WINDOW = 128
N_GROUPS = 4
EXPERTS_PER_GROUP = 8
N_EXPERTS = N_GROUPS * EXPERTS_PER_GROUP
EXPERT_FF = 256

RET_W = RET_HEADS * RET_DK
SWA_Q = SWA_HEADS * SWA_HD
SWA_KV = SWA_KV_HEADS * SWA_HD
Z_RET_W = 4 * RET_W
Z_SKV_W = 2 * SWA_KV
Z_GATE_W = 2 * D_MODEL
COL_SQ = Z_RET_W
COL_SKV = COL_SQ + SWA_Q
COL_GATE = COL_SKV + Z_SKV_W
IN_WIDTH = COL_GATE + Z_GATE_W

ROUTE_LANES = 128
LANE_GROUP0 = N_EXPERTS
LANE_E1, LANE_E2, LANE_C1, LANE_C2 = 32, 33, 34, 35
LANE_SECOND0 = 64
NEG_INF = -1e30

HALF_D = D_MODEL // 2
ROW_TILE = 512
SC_CORES = 2
SC_SUBCORES = 16
SC_WORKERS = SC_CORES * SC_SUBCORES
SC_CHUNK = 128

VMEM_LIMIT = 56 * 1024 * 1024

BLOCK_ROWS = 512
W_SLAB = 256
RET_ROWS = 512
MERGE_SUB = 256
PLE_TM = 512


def _rms(x, g):
    ms = jnp.mean(x * x, axis=-1, keepdims=True)
    return x * lax.rsqrt(ms + EPS) * g


def _sigmoid(x):
    return 1.0 / (1.0 + jnp.exp(-x))


def _dot(a, b):
    return jnp.dot(a, b, preferred_element_type=F32)


def _dot_nt(a, b):
    return lax.dot_general(a, b, (((1,), (1,)), ((), ())), preferred_element_type=F32)


def _dot_tn(a, b):
    return lax.dot_general(a, b, (((0,), (0,)), ((), ())), preferred_element_type=F32)


def _pack_halves(v):
    return pltpu.pack_elementwise([v[:, :HALF_D], v[:, HALF_D:]], packed_dtype=BF16)


def _unpack_halves(p):
    words = lax.bitcast_convert_type(p, jnp.uint32)
    return tuple(pltpu.unpack_elementwise(words, index=k, packed_dtype=BF16, unpacked_dtype=F32) for k in range(2))


def _retention_tables():
    f32 = np.float32
    h = RET_HEADS
    c = RET_CHUNK
    log_gamma = np.log1p(-np.exp2(f32(-5.0) - np.arange(h, dtype=f32))).astype(f32)
    pos = np.arange(c, dtype=f32)
    diff = pos[:, None] - pos[None, :]
    decay = np.where(diff[None] >= 0.0, np.exp(np.maximum(diff, f32(0.0))[None] * log_gamma[:, None, None]), f32(0.0))
    scale = f32(RET_DK ** -0.5)
    dmask = decay * scale
    zeta = np.exp((f32(c - 1.0) - pos)[None, :] * log_gamma[:, None]) * scale
    xi = np.exp((pos + f32(1.0))[None, :] * log_gamma[:, None])
    cdec = np.exp(f32(c) * log_gamma)
    bc = lambda a: np.ascontiguousarray(np.broadcast_to(a[:, :, None], (h, c, c)), dtype=f32)
    return (dmask.astype(f32), bc(zeta), bc(xi),
            np.ascontiguousarray(np.broadcast_to(cdec[:, None, None], (h, c, c)), dtype=f32))


def _swa_tables(q_g, k_g):
    f32 = np.float32
    qi = np.arange(WINDOW)[:, None]
    sj = np.arange(2 * WINDOW)[None, :]
    rel = qi + WINDOW - sj
    ok = (rel >= 0) & (rel < WINDOW)
    slopes = np.exp2(f32(-8.0) * np.arange(1, SWA_HEADS + 1, dtype=f32) / f32(SWA_HEADS)).astype(f32)
    bias = np.where(ok[None], -slopes[:, None, None] * rel.astype(f32)[None], f32(NEG_INF)).astype(f32)
    bias = bias.reshape(SWA_KV_HEADS, SWA_GROUP * WINDOW, 2 * WINDOW)
    avg = lambda heads: jnp.asarray(np.kron(np.eye(heads, dtype=f32), np.full((SWA_HD, SWA_HD), 1.0 / SWA_HD, f32)),
                                    dtype=BF16)
    qg = jnp.tile(q_g.reshape(1, SWA_HD), (1, SWA_HEADS)) * (SWA_HD ** -0.5)
    kg = jnp.tile(k_g.reshape(1, SWA_HD), (1, SWA_KV_HEADS))
    return qg, kg, avg(SWA_HEADS), avg(SWA_KV_HEADS), jnp.asarray(bias)


def _block_kernel(per_seq, sinks_ref, x_ref, xp_ref, g_ref, w_hbm, dmask_ref, zeta_ref, xi_ref, cdec_ref,
                  gng_ref, gnb_ref, qg_ref, kg_ref, bq_ref, bk_ref, bias_ref,
                  wro_ref, wso_ref, wout_ref, fg_ref, wr_ref, br_ref,
                  x1_ref, hp_ref, route_ref, cnt_ref,
                  w_ref, wtmp_ref, w_sem, zc_ref, zn_ref, yr_ref, ys_ref, state_ref, prev_ref, band_ref, tail_ref):
    g = pl.program_id(0)
    seq_start = lax.rem(g - 1, per_seq) == 0

    @pl.when(g == 0)
    def _():
        n_slabs = IN_WIDTH // W_SLAB
        slab = lambda j: pltpu.make_async_copy(w_hbm.at[:, pl.ds(j * W_SLAB, W_SLAB)], wtmp_ref.at[j % 2],
                                               w_sem.at[j % 2])
        slab(0).start()
        for j in range(n_slabs):
            if j + 1 < n_slabs:
                slab(j + 1).start()
            slab(j).wait()
            w_ref[:, j * W_SLAB:(j + 1) * W_SLAB] = wtmp_ref[j % 2].astype(BF16)
        zc_ref[...] = jnp.zeros_like(zc_ref)
        tail_ref[...] = jnp.zeros_like(tail_ref)
        state_ref[...] = jnp.zeros_like(state_ref)

    @pl.when(seq_start)
    def _():
        state_ref[...] = jnp.zeros_like(state_ref)

    band_ref[:WINDOW, :] = tail_ref[...]
    band_ref[WINDOW:, :] = zc_ref[:, COL_SKV:COL_GATE]

    h = _rms(x_ref[...], g_ref[...]).astype(BF16)

    def project(c0, width):
        zn_ref[:, c0:c0 + width] = _dot(h, w_ref[:, c0:c0 + width]).astype(BF16)

    cw = 512
    proj_pieces = [functools.partial(project, j * cw, cw) for j in range(COL_SKV // cw)]
    proj_pieces.append(functools.partial(project, COL_SKV, Z_SKV_W))
    proj_pieces += [functools.partial(project, COL_GATE + j * cw, cw) for j in range(Z_GATE_W // cw)]

    def retention_rows(t):
        r_base = t * RET_ROWS
        n_chunks = RET_ROWS // RET_CHUNK
        part = lambda c, p, hd: zc_ref[r_base + c * RET_CHUNK:r_base + (c + 1) * RET_CHUNK,
                                       p * RET_W + hd * RET_DK:p * RET_W + (hd + 1) * RET_DK]
        for hd in range(RET_HEADS):
            state = state_ref[hd]
            for c in range(n_chunks):
                prev_ref[t * n_chunks + c, hd] = state.astype(BF16)
                kz = (part(c, 1, hd).astype(F32) * zeta_ref[hd]).astype(BF16)
                state = cdec_ref[hd] * state + _dot_tn(kz, part(c, 2, hd))
            state_ref[hd] = state
        out_rows = []
        for c in range(n_chunks):
            heads = []
            for hd in range(RET_HEADS):
                q = part(c, 0, hd)
                scores = _dot_nt(q, part(c, 1, hd)) * dmask_ref[hd]
                y = _dot(scores.astype(BF16), part(c, 2, hd)) + _dot(q, prev_ref[t * n_chunks + c, hd]) * xi_ref[hd]
                mu = jnp.mean(y, axis=-1, keepdims=True)
                d = y - mu
                var = jnp.mean(d * d, axis=-1, keepdims=True)
                hs = slice(hd * RET_DV, (hd + 1) * RET_DV)
                yn = d * lax.rsqrt(var + EPS) * gng_ref[:, hs] + gnb_ref[:, hs]
                gate = part(c, 3, hd).astype(F32)
                heads.append((gate * _sigmoid(gate) * yn).astype(BF16))
            out_rows.append(jnp.concatenate(heads, axis=1))
        yr_ref[r_base:r_base + RET_ROWS, :] = jnp.concatenate(out_rows, axis=0)

    group_rows = SWA_GROUP * WINDOW
    lo_q = lax.broadcasted_iota(jnp.int32, (WINDOW, 2 * SWA_HD), 1) < SWA_HD
    lo_k = lax.broadcasted_iota(jnp.int32, (2 * WINDOW, 2 * SWA_HD), 1) < SWA_HD
    in_prev = lax.broadcasted_iota(jnp.int32, (group_rows, 2 * WINDOW), 1) < WINDOW
    row_head = lax.broadcasted_iota(jnp.int32, (group_rows, 1), 0) // WINDOW

    def both_heads(v):
        swapped = pltpu.roll(v, SWA_HD, axis=1)
        return jnp.where(lo_k, v, swapped), jnp.where(lo_k, swapped, v)

    def swa_block(n):
        rows = slice(n * WINDOW, (n + 1) * WINDOW)
        kv = band_ref[n * WINDOW:(n + 2) * WINDOW, :]
        kf = kv[:, :SWA_KV].astype(F32)
        kn = kf * lax.rsqrt(_dot((kf * kf).astype(BF16), bk_ref[...]) + EPS) * kg_ref[...]
        keys = [a.astype(BF16) for a in both_heads(kn)]
        vals = [a.astype(BF16) for a in both_heads(kv[:, SWA_KV:].astype(F32))]
        qf = zc_ref[rows, COL_SQ:COL_SKV].astype(F32)
        qn = qf * lax.rsqrt(_dot((qf * qf).astype(BF16), bq_ref[...]) + EPS) * qg_ref[...]
        outs = []
        for kh in range(SWA_KV_HEADS):
            parts = []
            for j in range(SWA_GROUP // 2):
                c0 = (kh * SWA_GROUP + 2 * j) * SWA_HD
                two = qn[:, c0:c0 + 2 * SWA_HD]
                parts += [jnp.where(lo_q, two, 0.0), jnp.where(lo_q, 0.0, two)]
            qs = jnp.concatenate(parts, axis=0).astype(BF16)
            s = _dot_nt(qs, keys[kh]) + bias_ref[kh]
            if n == 0:
                s = jnp.where(jnp.logical_and(seq_start, in_prev), NEG_INF, s)
            sink = sinks_ref[kh * SWA_GROUP]
            for gi in range(1, SWA_GROUP):
                sink = jnp.where(row_head == gi, sinks_ref[kh * SWA_GROUP + gi], sink)
            m = jnp.maximum(jnp.max(s, axis=-1, keepdims=True), sink)
            p = jnp.exp(s - m)
            denom = jnp.sum(p, axis=-1, keepdims=True) + jnp.exp(sink - m)
            o = _dot(p.astype(BF16), vals[kh]) * (1.0 / denom)
            for j in range(SWA_GROUP // 2):
                even = o[(2 * j) * WINDOW:(2 * j + 1) * WINDOW]
                odd = o[(2 * j + 1) * WINDOW:(2 * j + 2) * WINDOW]
                outs.append(jnp.where(lo_q, even, odd).astype(BF16))
        ys_ref[rows, :] = jnp.concatenate(outs, axis=1)

    mix_pieces = [functools.partial(retention_rows, t) for t in range(BLOCK_ROWS // RET_ROWS)]
    mix_pieces += [functools.partial(swa_block, n) for n in range(BLOCK_ROWS // WINDOW)]

    subs = []

    def merge_sub(s):
        subs.append(_merge_rows(slice(s * MERGE_SUB, (s + 1) * MERGE_SUB), yr_ref, ys_ref, zc_ref, xp_ref, wro_ref,
                                wso_ref, wout_ref, fg_ref, wr_ref, br_ref))

    other_pieces = mix_pieces + [functools.partial(merge_sub, s) for s in range(BLOCK_ROWS // MERGE_SUB)]

    n_other = len(other_pieces)
    for k, piece in enumerate(other_pieces):
        for proj_piece in proj_pieces[k * len(proj_pieces) // n_other:(k + 1) * len(proj_pieces) // n_other]:
            proj_piece()
        piece()
    x1_ref[...] = jnp.concatenate([s[0] for s in subs], axis=0)
    hp_ref[...] = jnp.concatenate([s[1] for s in subs], axis=0)
    route_ref[...] = jnp.concatenate([s[2] for s in subs], axis=0)
    for s, sub in enumerate(subs):
        cnt_ref[s] = sub[3]

    tail_ref[...] = zc_ref[BLOCK_ROWS - WINDOW:, COL_SKV:COL_GATE]
    zc_ref[...] = zn_ref[...]


def _merge_rows(rows, yr_ref, ys_ref, z_ref, x_ref, wro_ref, wso_ref, wout_ref, fg_ref, wr_ref, br_ref):
    a = _dot(yr_ref[rows, :], wro_ref[...])
    b = _dot(ys_ref[rows, :], wso_ref[...])
    gate_r = z_ref[rows, COL_GATE:COL_GATE + D_MODEL].astype(F32)
    gate_s = z_ref[rows, COL_GATE + D_MODEL:].astype(F32)
    merged = (_sigmoid(gate_r) * a + _sigmoid(gate_s) * b).astype(BF16)
    x1 = x_ref[rows, :] + _dot(merged, wout_ref[...])
    h2 = _rms(x1, fg_ref[...])
    packed = _pack_halves(h2)

    logits = _dot(h2.astype(BF16), wr_ref[...]) + br_ref[...]
    lane = lax.broadcasted_iota(jnp.int32, logits.shape, 1)
    big = jnp.int32(ROUTE_LANES)
    is_group = jnp.logical_and(lane >= LANE_GROUP0, lane < LANE_GROUP0 + N_GROUPS)
    gl = jnp.where(is_group, logits, NEG_INF)
    gmax = jnp.max(gl, axis=-1, keepdims=True)
    g_w = 1.0 / jnp.sum(jnp.exp(gl - gmax), axis=-1, keepdims=True)
    g_sel = jnp.min(jnp.where(gl == gmax, lane, big), axis=-1, keepdims=True) - LANE_GROUP0
    in_group = jnp.logical_and(lane < N_EXPERTS, (lane >> 3) == g_sel)
    el = jnp.where(in_group, logits, NEG_INF)
    m1 = jnp.max(el, axis=-1, keepdims=True)
    i1 = jnp.min(jnp.where(el == m1, lane, big), axis=-1, keepdims=True)
    el2 = jnp.where(lane == i1, NEG_INF, el)
    m2 = jnp.max(el2, axis=-1, keepdims=True)
    i2 = jnp.min(jnp.where(el2 == m2, lane, big), axis=-1, keepdims=True)
    e2 = jnp.exp(m2 - m1)
    c1 = g_w / (1.0 + e2)
    c2 = g_w * e2 / (1.0 + e2)
    chosen = jnp.where(jnp.logical_or(lane == i1, lane == i2), 1.0, 0.0)
    route = (chosen
             + jnp.where(lane == LANE_E1, i1.astype(F32), 0.0)
             + jnp.where(lane == LANE_E2, i2.astype(F32), 0.0)
             + jnp.where(lane == LANE_C1, c1, 0.0)
             + jnp.where(lane == LANE_C2, c2, 0.0)
             + jnp.where(lane == i2 + LANE_SECOND0, 1.0, 0.0))
    return x1, packed, route, jnp.sum(chosen, axis=0, keepdims=True)


def _block(x2d, mix_g, w_in, gn_g, gn_b, sinks, q_g, k_g, w_ret_o, w_swa_o, w_out, ffn_g, w_router, b_router,
           seq):
    t = x2d.shape[0]
    n = t // BLOCK_ROWS
    assert seq % BLOCK_ROWS == 0
    tables = _retention_tables()
    swa_tables = _swa_tables(q_g, k_g)
    consts = tables + (gn_g, gn_b) + swa_tables + (w_ret_o, w_swa_o, w_out, ffn_g, w_router, b_router)
    full = lambda a: pl.BlockSpec(a.shape, lambda g: (0,) * a.ndim)
    cur = lambda w: pl.BlockSpec((BLOCK_ROWS, w), lambda g: (jnp.minimum(g, n - 1), 0))
    prv = lambda w: pl.BlockSpec((BLOCK_ROWS, w), lambda g: (jnp.maximum(g - 1, 0), 0))
    chunks = BLOCK_ROWS // RET_CHUNK
    subs_per_block = BLOCK_ROWS // MERGE_SUB
    return pl.pallas_call(
        functools.partial(_block_kernel, seq // BLOCK_ROWS),
        grid=(n + 1,),
        in_specs=[pl.BlockSpec(memory_space=pltpu.SMEM), cur(D_MODEL), prv(D_MODEL), full(mix_g),
                  pl.BlockSpec(memory_space=pl.ANY)] + [full(a) for a in consts],
        out_specs=[prv(D_MODEL), prv(HALF_D), prv(ROUTE_LANES),
                   pl.BlockSpec((subs_per_block, 1, ROUTE_LANES), lambda g: (jnp.maximum(g - 1, 0), 0, 0))],
        out_shape=[jax.ShapeDtypeStruct((t, D_MODEL), F32),
                   jax.ShapeDtypeStruct((t, HALF_D), I32),
                   jax.ShapeDtypeStruct((t, ROUTE_LANES), F32),
                   jax.ShapeDtypeStruct((n * subs_per_block, 1, ROUTE_LANES), F32)],
        scratch_shapes=[pltpu.VMEM(w_in.shape, BF16), pltpu.VMEM((2, D_MODEL, W_SLAB), F32),
                        pltpu.SemaphoreType.DMA((2,)),
                        pltpu.VMEM((BLOCK_ROWS, IN_WIDTH), BF16), pltpu.VMEM((BLOCK_ROWS, IN_WIDTH), BF16),
                        pltpu.VMEM((BLOCK_ROWS, RET_W), BF16), pltpu.VMEM((BLOCK_ROWS, SWA_Q), BF16),
                        pltpu.VMEM((RET_HEADS, RET_DK, RET_DV), F32),
                        pltpu.VMEM((chunks, RET_HEADS, RET_DK, RET_DV), BF16),
                        pltpu.VMEM((BLOCK_ROWS + WINDOW, Z_SKV_W), BF16), pltpu.VMEM((WINDOW, Z_SKV_W), BF16)],
        compiler_params=pltpu.CompilerParams(dimension_semantics=("arbitrary",),
                                             vmem_limit_bytes=VMEM_LIMIT),
        name="block",
    )(sinks, x2d, x2d, mix_g, w_in, *consts)


def _plan_kernel(cnt_ref, route_ref, base_ref, owner_ref, valid_ref, used_ref, first_ref, next_ref, slot_ref,
                 pos_ref):
    counts = cnt_ref[...]
    nt = counts.shape[0]
    exact = functools.partial(jnp.dot, precision=lax.Precision.HIGHEST, preferred_element_type=F32)
    total = jnp.sum(counts, axis=0, keepdims=True)
    tiles_e = jnp.ceil(total * (1.0 / ROW_TILE))
    k = lax.broadcasted_iota(jnp.int32, (ROUTE_LANES, ROUTE_LANES), 0)
    lane = lax.broadcasted_iota(jnp.int32, (ROUTE_LANES, ROUTE_LANES), 1)
    tile_end = exact(jnp.broadcast_to(tiles_e, (8, ROUTE_LANES)), jnp.where(k <= lane, 1.0, 0.0))[0:1]
    tile_start = tile_end - tiles_e
    r = lax.broadcasted_iota(jnp.int32, (nt, nt), 0)
    c = lax.broadcasted_iota(jnp.int32, (nt, nt), 1)
    before = exact(jnp.where(c < r, 1.0, 0.0), counts)
    base_ref[...] = tile_start * ROW_TILE + before

    used = jnp.max(tile_end, axis=-1, keepdims=True)
    tile = k.astype(F32)
    clamped = jnp.minimum(tile, jnp.maximum(used - 1.0, 0.0))
    is_expert = lane < N_EXPERTS
    owner = jnp.sum(jnp.where(jnp.logical_and(is_expert, tile_end <= clamped), 1.0, 0.0), axis=-1, keepdims=True)
    owner = jnp.minimum(owner, N_EXPERTS - 1.0)
    mine = lane.astype(F32) == owner
    pick = lambda v: jnp.sum(jnp.where(mine, v, 0.0), axis=-1, keepdims=True)
    t1 = tile[:, 0:1]
    valid = jnp.clip(pick(total) - (t1 - pick(tile_start)) * ROW_TILE, 0.0, float(ROW_TILE))
    valid = jnp.where(t1 < used, valid, 0.0)
    owner_ref[...] = jnp.broadcast_to(owner, owner_ref.shape).astype(I32)
    valid_ref[...] = jnp.broadcast_to(valid, valid_ref.shape).astype(I32)
    used_ref[...] = jnp.broadcast_to(used, used_ref.shape).astype(I32)

    nonempty = jnp.where(jnp.logical_and(lane[0:1] < N_EXPERTS, tiles_e > 0.0), 1.0, 0.0)
    rank = exact(jnp.broadcast_to(nonempty, (8, ROUTE_LANES)), jnp.where(k < lane, 1.0, 0.0))[0:1]
    nonempty_col = jnp.sum(jnp.where(k == lane, jnp.broadcast_to(nonempty, k.shape), 0.0), axis=-1, keepdims=True)
    later = jnp.where(jnp.logical_and(k > lane, nonempty_col > 0.0), tile, float(ROUTE_LANES))
    nxt_e = jnp.min(later, axis=0, keepdims=True)
    nxt_e = jnp.where(nxt_e < float(ROUTE_LANES), nxt_e, -1.0)
    first = jnp.where(jnp.logical_and(t1 == pick(tile_start), t1 < used), 1.0, 0.0)
    my_rank = pick(rank)
    slot = my_rank - 2.0 * jnp.floor(my_rank * 0.5)
    first_ref[...] = jnp.broadcast_to(first, first_ref.shape).astype(I32)
    next_ref[...] = jnp.broadcast_to(pick(nxt_e), next_ref.shape).astype(I32)
    slot_ref[...] = jnp.broadcast_to(slot, slot_ref.shape).astype(I32)

    sel_row = lax.broadcasted_iota(jnp.int32, (8, ROUTE_LANES), 0)
    sel_lane = lax.broadcasted_iota(jnp.int32, (8, ROUTE_LANES), 1)
    tok_r = lax.broadcasted_iota(jnp.int32, (MERGE_SUB, MERGE_SUB), 0)
    tok_c = lax.broadcasted_iota(jnp.int32, (MERGE_SUB, MERGE_SUB), 1)
    eye = jnp.where(tok_r == tok_c, 1.0, 0.0).astype(BF16)
    upto = jnp.where(tok_r <= tok_c, 1.0, 0.0).astype(BF16)
    out_row = lax.broadcasted_iota(jnp.int32, (8, MERGE_SUB), 0)
    sum_experts = lambda a: jnp.sum(a, axis=0, keepdims=True)

    def sub_positions(s, carry):
        route = route_ref[pl.ds(pl.multiple_of(s * MERGE_SUB, MERGE_SUB), MERGE_SUB), :].astype(BF16)
        record_t = _dot_tn(route, eye)
        incl_t = _dot_tn(route, upto)[:N_EXPERTS]
        both_t = record_t[:N_EXPERTS]
        second_t = record_t[LANE_SECOND0:LANE_SECOND0 + N_EXPERTS]
        rank2 = sum_experts(second_t * incl_t) - 1.0
        rank1 = sum_experts(both_t * incl_t) - rank2 - 2.0
        base_row = jnp.where(sel_lane < N_EXPERTS, jnp.broadcast_to(base_ref[pl.ds(s, 1), :], (8, ROUTE_LANES)), 0.0)
        hi = jnp.floor(base_row * (1.0 / 256.0))
        lo = base_row - 256.0 * hi
        hi2 = pltpu.roll(hi, LANE_SECOND0, axis=1)
        lo2 = pltpu.roll(lo, LANE_SECOND0, axis=1)
        table = (jnp.where(sel_row == 0, hi, 0.0) + jnp.where(sel_row == 1, lo, 0.0)
                 + jnp.where(sel_row == 2, hi2, 0.0) + jnp.where(sel_row == 3, lo2, 0.0))
        r = _dot_nt(table.astype(BF16), route)
        base_both = 256.0 * r[0:1] + r[1:2]
        base2 = 256.0 * r[2:3] + r[3:4]
        out = jnp.where(out_row == 0, base_both - base2 + rank1, jnp.where(out_row == 1, base2 + rank2, 0.0))
        pos_ref[s] = out.astype(I32)
        return carry

    lax.fori_loop(0, nt, sub_positions, 0)


def _tile_plan(cnt, route, n_tiles):
    assert n_tiles <= ROUTE_LANES
    nt = cnt.shape[0]
    sq = jax.ShapeDtypeStruct((ROUTE_LANES, ROUTE_LANES), I32)
    _, owner, valid, used, first, nxt, slot, pos = pl.pallas_call(
        _plan_kernel,
        out_shape=[jax.ShapeDtypeStruct((nt, ROUTE_LANES), F32), sq, sq,
                   jax.ShapeDtypeStruct((8, ROUTE_LANES), I32), sq, sq, sq,
                   jax.ShapeDtypeStruct((nt, 8, MERGE_SUB), I32)],
        compiler_params=pltpu.CompilerParams(vmem_limit_bytes=VMEM_LIMIT),
        name="tile_plan",
    )(cnt.reshape(nt, ROUTE_LANES), route)
    per_tile = [a[:n_tiles, 0] for a in (owner, valid, first, nxt, slot)]
    return per_tile, used[0, :1], pos[:, 0, :], pos[:, 1, :]


def _sc_mesh():
    return plsc.VectorSubcoreMesh(core_axis_name="c", subcore_axis_name="s")


def _sc_worker():
    return lax.axis_index("s") * SC_CORES + lax.axis_index("c")


def _dispatch(hp, pos1, pos2, n_rows):
    t = hp.shape[0]
    per_worker = t // SC_WORKERS
    k = per_worker // SC_CHUNK
    idx = pltpu.VMEM((k, SC_CHUNK), I32)

    @functools.partial(pl.kernel, mesh=_sc_mesh(), out_type=jax.ShapeDtypeStruct((n_rows, HALF_D), I32),
                       scratch_types=[idx, idx, pltpu.VMEM((SC_CHUNK, HALF_D), I32)], name="moe_dispatch")
    def run(h_hbm, p1_hbm, p2_hbm, xs_hbm, p1_v, p2_v, rows_v):
        wid = _sc_worker()
        pltpu.sync_copy(p1_hbm.at[wid], p1_v)
        pltpu.sync_copy(p2_hbm.at[wid], p2_v)
        for j in range(k):
            pltpu.sync_copy(h_hbm.at[pl.ds(wid * per_worker + j * SC_CHUNK, SC_CHUNK)], rows_v)
            pltpu.sync_copy(rows_v, xs_hbm.at[p1_v.at[j]])
            pltpu.sync_copy(rows_v, xs_hbm.at[p2_v.at[j]])

    return run(hp, pos1.reshape(SC_WORKERS, k, SC_CHUNK), pos2.reshape(SC_WORKERS, k, SC_CHUNK))


def _combine_gather(ys, pos1, pos2):
    t = pos1.size
    per_worker = t // SC_WORKERS
    k = per_worker // SC_CHUNK
    idx = pltpu.VMEM((k, SC_CHUNK), I32)
    out = jax.ShapeDtypeStruct((t, HALF_D), I32)

    @functools.partial(pl.kernel, mesh=_sc_mesh(), out_type=(out, out),
                       scratch_types=[idx, idx, pltpu.VMEM((SC_CHUNK, HALF_D), I32)], name="moe_combine")
    def run(ys_hbm, p1_hbm, p2_hbm, g1_hbm, g2_hbm, p1_v, p2_v, rows_v):
        wid = _sc_worker()
        pltpu.sync_copy(p1_hbm.at[wid], p1_v)
        pltpu.sync_copy(p2_hbm.at[wid], p2_v)
        for j in range(k):
            dst = pl.ds(wid * per_worker + j * SC_CHUNK, SC_CHUNK)
            pltpu.sync_copy(ys_hbm.at[p1_v.at[j]], rows_v)
            pltpu.sync_copy(rows_v, g1_hbm.at[dst])
            pltpu.sync_copy(ys_hbm.at[p2_v.at[j]], rows_v)
            pltpu.sync_copy(rows_v, g2_hbm.at[dst])

    return run(ys, pos1.reshape(SC_WORKERS, k, SC_CHUNK), pos2.reshape(SC_WORKERS, k, SC_CHUNK))


def _experts_kernel(owner_ref, valid_ref, first_ref, next_ref, slot_ref, used_ref,
                    xs_ref, wg_hbm, wu_hbm, wd_hbm, ys_ref,
                    wg_f, wu_f, wd_f, wg_bf, wu_bf, wd_bf, w_sem):
    i = pl.program_id(0)
    valid = valid_ref[i]

    def fetch(expert, slot):
        pairs = ((wg_hbm, wg_f), (wu_hbm, wu_f), (wd_hbm, wd_f))
        return [pltpu.make_async_copy(w.at[expert], buf.at[slot], w_sem.at[slot, n])
                for n, (w, buf) in enumerate(pairs)]

    @pl.when(i == 0)
    def _():
        for copy in fetch(owner_ref[0], 0):
            copy.start()

    @pl.when(first_ref[i] == 1)
    def _():
        slot = slot_ref[i]
        for copy in fetch(owner_ref[i], slot):
            copy.wait()
        wg_bf[...] = wg_f[slot].astype(BF16)
        wu_bf[...] = wu_f[slot].astype(BF16)
        wd_bf[...] = wd_f[slot].astype(BF16)

        @pl.when(next_ref[i] >= 0)
        def _():
            for copy in fetch(next_ref[i], 1 - slot):
                copy.start()

    @pl.when(i < used_ref[0])
    def _():
        row = lax.broadcasted_iota(jnp.int32, (ROW_TILE, HALF_D), 0)
        packed = jnp.where(row < valid, xs_ref[...], 0)
        hi, lo = _unpack_halves(packed)
        hi = hi.astype(BF16)
        lo = lo.astype(BF16)
        hg = _dot(hi, wg_bf[:HALF_D, :]) + _dot(lo, wg_bf[HALF_D:, :])
        hu = _dot(hi, wu_bf[:HALF_D, :]) + _dot(lo, wu_bf[HALF_D:, :])
        hid = (hg * _sigmoid(hg) * hu).astype(BF16)
        ys_ref[...] = _pack_halves(_dot(hid, wd_bf[...]))


def _experts(xs, per_tile, used, wg, wu, wd):
    n_tiles = per_tile[0].shape[0]
    tile = lambda i, *prefetch: (jnp.minimum(i, prefetch[-1][0] - 1), 0)
    hbm = pl.BlockSpec(memory_space=pl.ANY)
    up_shape, down_shape = (D_MODEL, EXPERT_FF), (EXPERT_FF, D_MODEL)
    grid_spec = pltpu.PrefetchScalarGridSpec(
        num_scalar_prefetch=len(per_tile) + 1,
        grid=(n_tiles,),
        in_specs=[pl.BlockSpec((ROW_TILE, HALF_D), tile), hbm, hbm, hbm],
        out_specs=pl.BlockSpec((ROW_TILE, HALF_D), tile),
        scratch_shapes=[pltpu.VMEM((2,) + up_shape, F32), pltpu.VMEM((2,) + up_shape, F32),
                        pltpu.VMEM((2,) + down_shape, F32),
                        pltpu.VMEM(up_shape, BF16), pltpu.VMEM(up_shape, BF16), pltpu.VMEM(down_shape, BF16),
                        pltpu.SemaphoreType.DMA((2, 3))],
    )
    return pl.pallas_call(
        _experts_kernel,
        grid_spec=grid_spec,
        out_shape=jax.ShapeDtypeStruct((n_tiles * ROW_TILE, HALF_D), I32),
        compiler_params=pltpu.CompilerParams(dimension_semantics=("arbitrary",),
                                             vmem_limit_bytes=VMEM_LIMIT),
        name="experts",
    )(*per_tile, used, xs, wg, wu, wd)


def _ple_kernel(x1_ref, g1_ref, g2_ref, route_ref, p_ref, wp_ref, pg_ref, gg_ref, wgate_ref, o_ref):
    route = route_ref[...]
    lane = lax.broadcasted_iota(jnp.int32, route.shape, 1)
    pick = lambda ln: jnp.sum(jnp.where(lane == ln, route, 0.0), axis=-1, keepdims=True)
    c1 = pick(LANE_C1)
    c2 = pick(LANE_C2)
    a_hi, a_lo = _unpack_halves(g1_ref[...])
    b_hi, b_lo = _unpack_halves(g2_ref[...])
    x = x1_ref[...] + jnp.concatenate([c1 * a_hi + c2 * b_hi, c1 * a_lo + c2 * b_lo], axis=1)
    ple = _rms(_dot(p_ref[...].astype(BF16), wp_ref[...]), pg_ref[...])
    gate = _sigmoid(_dot(_rms(x, gg_ref[...]).astype(BF16), wgate_ref[...]))
    o_ref[...] = x + gate * ple


def _ple(x1, g1, g2, route, p2d, w_ple, ple_g, gate_g, w_gate):
    t = x1.shape[0]
    row = lambda w: pl.BlockSpec((PLE_TM, w), lambda i: (i, 0))
    full = lambda a: pl.BlockSpec(a.shape, lambda i: (0,) * a.ndim)
    return pl.pallas_call(
        _ple_kernel,
        grid=(t // PLE_TM,),
        in_specs=[row(D_MODEL), row(HALF_D), row(HALF_D), row(ROUTE_LANES), row(PLE_DIM),
                  full(w_ple), full(ple_g), full(gate_g), full(w_gate)],
        out_specs=row(D_MODEL),
        out_shape=jax.ShapeDtypeStruct((t, D_MODEL), F32),
        compiler_params=pltpu.CompilerParams(dimension_semantics=("parallel",),
                                             vmem_limit_bytes=VMEM_LIMIT),
        name="ple",
    )(x1, g1, g2, route, p2d, w_ple, ple_g, gate_g, w_gate)


def kernel(x, p, mix_norm_g, w_in, ret_gn_g, ret_gn_b, w_ret_o, q_norm_g, k_norm_g, attn_sinks,
           w_swa_o, w_out, ffn_norm_g, w_router_group, b_router_group, w_router_expert,
           b_router_expert, w_exp_gate, w_exp_up, w_exp_down, ple_gate_norm_g, w_ple_gate,
           w_ple, ple_norm_g):
    batch, seq, d = x.shape
    t = batch * seq
    depth = w_in.shape[0]
    n_tiles = 2 * t // ROW_TILE + N_EXPERTS
    row = lambda a: a.reshape(1, -1)
    x2d = x.reshape(t, d)
    for i in range(depth):
        pad = ROUTE_LANES - N_GROUPS - N_EXPERTS
        w_router = jnp.pad(jnp.concatenate([w_router_expert[i], w_router_group[i]], axis=1), ((0, 0), (0, pad)))
        b_router = jnp.pad(jnp.concatenate([b_router_expert[i], b_router_group[i]]), (0, pad)).reshape(1, -1)
        w_router = w_router.astype(BF16)
        x1, hp, route, cnt = _block(x2d, row(mix_norm_g[i]), w_in[i], row(ret_gn_g[i]),
                                    row(ret_gn_b[i]), attn_sinks[i], q_norm_g[i], k_norm_g[i],
                                    w_ret_o[i].astype(BF16), w_swa_o[i].astype(BF16), w_out[i].astype(BF16),
                                    row(ffn_norm_g[i]), w_router, b_router, seq)
        per_tile, used, pos1, pos2 = _tile_plan(cnt, route, n_tiles)
        xs = _dispatch(hp, pos1, pos2, n_tiles * ROW_TILE)
        ys = _experts(xs, per_tile, used, w_exp_gate[i], w_exp_up[i], w_exp_down[i])
        g1, g2 = _combine_gather(ys, pos1, pos2)
        x2d = _ple(x1, g1, g2, route, p[i].reshape(t, PLE_DIM), w_ple[i].astype(BF16), row(ple_norm_g[i]),
                   row(ple_gate_norm_g[i]), w_ple_gate[i].astype(BF16))
    return x2d.reshape(batch, seq, d)
```

```python
import functools

import jax
import jax.numpy as jnp
import numpy as np
from jax import lax
from jax.experimental import pallas as pl
from jax.experimental.pallas import tpu as pltpu
from jax.experimental.pallas import tpu_sc as plsc

F32 = jnp.float32
BF16 = jnp.bfloat16
I32 = jnp.int32

EPS = 1e-6
D_MODEL = 1024
PLE_DIM = 256
RET_HEADS = 4
RET_DK = 128
RET_DV = 128
RET_CHUNK = 128
SWA_HEADS = 8
SWA_KV_HEADS = 2
SWA_GROUP = SWA_HEADS // SWA_KV_HEADS
SWA_HD = 64
WINDOW = 128
N_GROUPS = 4
EXPERTS_PER_GROUP = 8
N_EXPERTS = N_GROUPS * EXPERTS_PER_GROUP
EXPERT_FF = 256

RET_W = RET_HEADS * RET_DK
SWA_Q = SWA_HEADS * SWA_HD
SWA_KV = SWA_KV_HEADS * SWA_HD
Z_RET_W = 4 * RET_W
Z_SKV_W = 2 * SWA_KV
Z_GATE_W = 2 * D_MODEL
COL_SQ = Z_RET_W
COL_SKV = COL_SQ + SWA_Q
COL_GATE = COL_SKV + Z_SKV_W
IN_WIDTH = COL_GATE + Z_GATE_W

ROUTE_LANES = 128
LANE_GROUP0 = N_EXPERTS
LANE_E1, LANE_E2, LANE_C1, LANE_C2 = 32, 33, 34, 35
LANE_SECOND0 = 64
NEG_INF = -1e30

HALF_D = D_MODEL // 2
ROW_TILE = 512
SC_CORES = 2
SC_SUBCORES = 16
SC_WORKERS = SC_CORES * SC_SUBCORES
SC_CHUNK = 128

VMEM_LIMIT = 56 * 1024 * 1024

BLOCK_ROWS = 512
W_SLAB = 256
RET_ROWS = 512
MERGE_SUB = 256
PLE_TM = 512


def _rms(x, g):
    ms = jnp.mean(x * x, axis=-1, keepdims=True)
    return x * lax.rsqrt(ms + EPS) * g


def _sigmoid(x):
    return 1.0 / (1.0 + jnp.exp(-x))


def _dot(a, b):
    return jnp.dot(a, b, preferred_element_type=F32)


def _dot_nt(a, b):
    return lax.dot_general(a, b, (((1,), (1,)), ((), ())), preferred_element_type=F32)


def _dot_tn(a, b):
    return lax.dot_general(a, b, (((0,), (0,)), ((), ())), preferred_element_type=F32)


def _pack_halves(v):
    return pltpu.pack_elementwise([v[:, :HALF_D], v[:, HALF_D:]], packed_dtype=BF16)


def _unpack_halves(p):
    words = lax.bitcast_convert_type(p, jnp.uint32)
    return tuple(pltpu.unpack_elementwise(words, index=k, packed_dtype=BF16, unpacked_dtype=F32) for k in range(2))


def _retention_tables():
    f32 = np.float32
    h = RET_HEADS
    c = RET_CHUNK
    log_gamma = np.log1p(-np.exp2(f32(-5.0) - np.arange(h, dtype=f32))).astype(f32)
    pos = np.arange(c, dtype=f32)
    diff = pos[:, None] - pos[None, :]
    decay = np.where(diff[None] >= 0.0, np.exp(np.maximum(diff, f32(0.0))[None] * log_gamma[:, None, None]), f32(0.0))
    scale = f32(RET_DK ** -0.5)
    dmask = decay * scale
    zeta = np.exp((f32(c - 1.0) - pos)[None, :] * log_gamma[:, None]) * scale
    xi = np.exp((pos + f32(1.0))[None, :] * log_gamma[:, None])
    cdec = np.exp(f32(c) * log_gamma)
    bc = lambda a: np.ascontiguousarray(np.broadcast_to(a[:, :, None], (h, c, c)), dtype=f32)
    return (dmask.astype(f32), bc(zeta), bc(xi),
            np.ascontiguousarray(np.broadcast_to(cdec[:, None, None], (h, c, c)), dtype=f32))


def _swa_tables(q_g, k_g):
    f32 = np.float32
    qi = np.arange(WINDOW)[:, None]
    sj = np.arange(2 * WINDOW)[None, :]
    rel = qi + WINDOW - sj
    ok = (rel >= 0) & (rel < WINDOW)
    slopes = np.exp2(f32(-8.0) * np.arange(1, SWA_HEADS + 1, dtype=f32) / f32(SWA_HEADS)).astype(f32)
    bias = np.where(ok[None], -slopes[:, None, None] * rel.astype(f32)[None], f32(NEG_INF)).astype(f32)
    bias = bias.reshape(SWA_KV_HEADS, SWA_GROUP * WINDOW, 2 * WINDOW)
    avg = lambda heads: jnp.asarray(np.kron(np.eye(heads, dtype=f32), np.full((SWA_HD, SWA_HD), 1.0 / SWA_HD, f32)),
                                    dtype=BF16)
    qg = jnp.tile(q_g.reshape(1, SWA_HD), (1, SWA_HEADS)) * (SWA_HD ** -0.5)
    kg = jnp.tile(k_g.reshape(1, SWA_HD), (1, SWA_KV_HEADS))
    return qg, kg, avg(SWA_HEADS), avg(SWA_KV_HEADS), jnp.asarray(bias)


def _block_kernel(per_seq, sinks_ref, x_ref, xp_ref, g_ref, w_hbm, dmask_ref, zeta_ref, xi_ref, cdec_ref,
                  gng_ref, gnb_ref, qg_ref, kg_ref, bq_ref, bk_ref, bias_ref,
                  wro_hbm, wso_hbm, wout_hbm, fg_ref, wr_ref, br_ref,
                  x1_ref, hp_ref, route_ref, cnt_ref,
                  w_ref, wro_ref, wso_ref, wout_ref, wtmp_ref, w_sem,
                  zc_ref, zn_ref, yr_ref, ys_ref, state_ref, prev_ref, band_ref, tail_ref):
    g = pl.program_id(0)
    seq_start = lax.rem(g - 1, per_seq) == 0

    @pl.when(g == 0)
    def _():
        slabs = [(src, dst, c0) for src, dst in ((w_hbm, w_ref), (wro_hbm, wro_ref), (wso_hbm, wso_ref),
                                                 (wout_hbm, wout_ref))
                 for c0 in range(0, dst.shape[1], W_SLAB)]

        def slab(k):
            src, dst, c0 = slabs[k]
            return pltpu.make_async_copy(src.at[:, pl.ds(c0, W_SLAB)], wtmp_ref.at[k % 2, pl.ds(0, dst.shape[0])],
                                         w_sem.at[k % 2])

        slab(0).start()
        for k, (_, dst, c0) in enumerate(slabs):
            if k + 1 < len(slabs):
                slab(k + 1).start()
            slab(k).wait()
            dst[:, c0:c0 + W_SLAB] = wtmp_ref[k % 2, :dst.shape[0]].astype(BF16)
        zc_ref[...] = jnp.zeros_like(zc_ref)
        tail_ref[...] = jnp.zeros_like(tail_ref)
        state_ref[...] = jnp.zeros_like(state_ref)

    @pl.when(seq_start)
    def _():
        state_ref[...] = jnp.zeros_like(state_ref)

    band_ref[:WINDOW, :] = tail_ref[...]
    band_ref[WINDOW:, :] = zc_ref[:, COL_SKV:COL_GATE]

    h = _rms(x_ref[...], g_ref[...]).astype(BF16)

    def project(c0, width):
        zn_ref[:, c0:c0 + width] = _dot(h, w_ref[:, c0:c0 + width]).astype(BF16)

    cw = 512
    proj_pieces = [functools.partial(project, j * cw, cw) for j in range(COL_SKV // cw)]
    proj_pieces.append(functools.partial(project, COL_SKV, Z_SKV_W))
    proj_pieces += [functools.partial(project, COL_GATE + j * cw, cw) for j in range(Z_GATE_W // cw)]

    def retention_rows(t):
        r_base = t * RET_ROWS
        n_chunks = RET_ROWS // RET_CHUNK
        part = lambda c, p, hd: zc_ref[r_base + c * RET_CHUNK:r_base + (c + 1) * RET_CHUNK,
                                       p * RET_W + hd * RET_DK:p * RET_W + (hd + 1) * RET_DK]
        for hd in range(RET_HEADS):
            state = state_ref[hd]
            for c in range(n_chunks):
                prev_ref[t * n_chunks + c, hd] = state.astype(BF16)
                kz = (part(c, 1, hd).astype(F32) * zeta_ref[hd]).astype(BF16)
                state = cdec_ref[hd] * state + _dot_tn(kz, part(c, 2, hd))
            state_ref[hd] = state
        out_rows = []
        for c in range(n_chunks):
            heads = []
            for hd in range(RET_HEADS):
                q = part(c, 0, hd)
                scores = _dot_nt(q, part(c, 1, hd)) * dmask_ref[hd]
                y = _dot(scores.astype(BF16), part(c, 2, hd)) + _dot(q, prev_ref[t * n_chunks + c, hd]) * xi_ref[hd]
                mu = jnp.mean(y, axis=-1, keepdims=True)
                d = y - mu
                var = jnp.mean(d * d, axis=-1, keepdims=True)
                hs = slice(hd * RET_DV, (hd + 1) * RET_DV)
                yn = d * lax.rsqrt(var + EPS) * gng_ref[:, hs] + gnb_ref[:, hs]
                gate = part(c, 3, hd).astype(F32)
                heads.append((gate * _sigmoid(gate) * yn).astype(BF16))
            out_rows.append(jnp.concatenate(heads, axis=1))
        yr_ref[r_base:r_base + RET_ROWS, :] = jnp.concatenate(out_rows, axis=0)

    group_rows = SWA_GROUP * WINDOW
    lo_q = lax.broadcasted_iota(jnp.int32, (WINDOW, 2 * SWA_HD), 1) < SWA_HD
    lo_k = lax.broadcasted_iota(jnp.int32, (2 * WINDOW, 2 * SWA_HD), 1) < SWA_HD
    in_prev = lax.broadcasted_iota(jnp.int32, (group_rows, 2 * WINDOW), 1) < WINDOW
    row_head = lax.broadcasted_iota(jnp.int32, (group_rows, 1), 0) // WINDOW

    def both_heads(v):
        swapped = pltpu.roll(v, SWA_HD, axis=1)
        return jnp.where(lo_k, v, swapped), jnp.where(lo_k, swapped, v)

    def swa_block(n):
        rows = slice(n * WINDOW, (n + 1) * WINDOW)
        kv = band_ref[n * WINDOW:(n + 2) * WINDOW, :]
        kf = kv[:, :SWA_KV].astype(F32)
        kn = kf * lax.rsqrt(_dot((kf * kf).astype(BF16), bk_ref[...]) + EPS) * kg_ref[...]
        keys = [a.astype(BF16) for a in both_heads(kn)]
        vals = [a.astype(BF16) for a in both_heads(kv[:, SWA_KV:].astype(F32))]
        qf = zc_ref[rows, COL_SQ:COL_SKV].astype(F32)
        qn = qf * lax.rsqrt(_dot((qf * qf).astype(BF16), bq_ref[...]) + EPS) * qg_ref[...]
        outs = []
        for kh in range(SWA_KV_HEADS):
            parts = []
            for j in range(SWA_GROUP // 2):
                c0 = (kh * SWA_GROUP + 2 * j) * SWA_HD
                two = qn[:, c0:c0 + 2 * SWA_HD]
                parts += [jnp.where(lo_q, two, 0.0), jnp.where(lo_q, 0.0, two)]
            qs = jnp.concatenate(parts, axis=0).astype(BF16)
            s = _dot_nt(qs, keys[kh]) + bias_ref[kh]
            if n == 0:
                s = jnp.where(jnp.logical_and(seq_start, in_prev), NEG_INF, s)
            sink = sinks_ref[kh * SWA_GROUP]
            for gi in range(1, SWA_GROUP):
                sink = jnp.where(row_head == gi, sinks_ref[kh * SWA_GROUP + gi], sink)
            m = jnp.maximum(jnp.max(s, axis=-1, keepdims=True), sink)
            p = jnp.exp(s - m)
            denom = jnp.sum(p, axis=-1, keepdims=True) + jnp.exp(sink - m)
            o = _dot(p.astype(BF16), vals[kh]) * (1.0 / denom)
            for j in range(SWA_GROUP // 2):
                even = o[(2 * j) * WINDOW:(2 * j + 1) * WINDOW]
                odd = o[(2 * j + 1) * WINDOW:(2 * j + 2) * WINDOW]
                outs.append(jnp.where(lo_q, even, odd).astype(BF16))
        ys_ref[rows, :] = jnp.concatenate(outs, axis=1)

    mix_pieces = [functools.partial(retention_rows, t) for t in range(BLOCK_ROWS // RET_ROWS)]
    mix_pieces += [functools.partial(swa_block, n) for n in range(BLOCK_ROWS // WINDOW)]

    subs = []

    def merge_sub(s):
        subs.append(_merge_rows(slice(s * MERGE_SUB, (s + 1) * MERGE_SUB), yr_ref, ys_ref, zc_ref, xp_ref, wro_ref,
                                wso_ref, wout_ref, fg_ref, wr_ref, br_ref))

    other_pieces = mix_pieces + [functools.partial(merge_sub, s) for s in range(BLOCK_ROWS // MERGE_SUB)]

    n_other = len(other_pieces)
    for k, piece in enumerate(other_pieces):
        for proj_piece in proj_pieces[k * len(proj_pieces) // n_other:(k + 1) * len(proj_pieces) // n_other]:
            proj_piece()
        piece()
    x1_ref[...] = jnp.concatenate([s[0] for s in subs], axis=0)
    hp_ref[...] = jnp.concatenate([s[1] for s in subs], axis=0)
    route_ref[...] = jnp.concatenate([s[2] for s in subs], axis=0)
    for s, sub in enumerate(subs):
        cnt_ref[s] = sub[3]

    tail_ref[...] = zc_ref[BLOCK_ROWS - WINDOW:, COL_SKV:COL_GATE]
    zc_ref[...] = zn_ref[...]


def _merge_rows(rows, yr_ref, ys_ref, z_ref, x_ref, wro_ref, wso_ref, wout_ref, fg_ref, wr_ref, br_ref):
    a = _dot(yr_ref[rows, :], wro_ref[...])
    b = _dot(ys_ref[rows, :], wso_ref[...])
    gate_r = z_ref[rows, COL_GATE:COL_GATE + D_MODEL].astype(F32)
    gate_s = z_ref[rows, COL_GATE + D_MODEL:].astype(F32)
    merged = (_sigmoid(gate_r) * a + _sigmoid(gate_s) * b).astype(BF16)
    x1 = x_ref[rows, :] + _dot(merged, wout_ref[...])
    h2 = _rms(x1, fg_ref[...])
    packed = _pack_halves(h2)

    logits = _dot(h2.astype(BF16), wr_ref[...]) + br_ref[...]
    lane = lax.broadcasted_iota(jnp.int32, logits.shape, 1)
    big = jnp.int32(ROUTE_LANES)
    is_group = jnp.logical_and(lane >= LANE_GROUP0, lane < LANE_GROUP0 + N_GROUPS)
    gl = jnp.where(is_group, logits, NEG_INF)
    gmax = jnp.max(gl, axis=-1, keepdims=True)
    g_w = 1.0 / jnp.sum(jnp.exp(gl - gmax), axis=-1, keepdims=True)
    g_sel = jnp.min(jnp.where(gl == gmax, lane, big), axis=-1, keepdims=True) - LANE_GROUP0
    in_group = jnp.logical_and(lane < N_EXPERTS, (lane >> 3) == g_sel)
    el = jnp.where(in_group, logits, NEG_INF)
    m1 = jnp.max(el, axis=-1, keepdims=True)
    i1 = jnp.min(jnp.where(el == m1, lane, big), axis=-1, keepdims=True)
    el2 = jnp.where(lane == i1, NEG_INF, el)
    m2 = jnp.max(el2, axis=-1, keepdims=True)
    i2 = jnp.min(jnp.where(el2 == m2, lane, big), axis=-1, keepdims=True)
    e2 = jnp.exp(m2 - m1)
    c1 = g_w / (1.0 + e2)
    c2 = g_w * e2 / (1.0 + e2)
    chosen = jnp.where(jnp.logical_or(lane == i1, lane == i2), 1.0, 0.0)
    route = (chosen
             + jnp.where(lane == LANE_E1, i1.astype(F32), 0.0)
             + jnp.where(lane == LANE_E2, i2.astype(F32), 0.0)
             + jnp.where(lane == LANE_C1, c1, 0.0)
             + jnp.where(lane == LANE_C2, c2, 0.0)
             + jnp.where(lane == i2 + LANE_SECOND0, 1.0, 0.0))
    return x1, packed, route, jnp.sum(chosen, axis=0, keepdims=True)


def _block(x2d, mix_g, w_in, gn_g, gn_b, sinks, q_g, k_g, w_ret_o, w_swa_o, w_out, ffn_g, w_router, b_router,
           seq):
    t = x2d.shape[0]
    n = t // BLOCK_ROWS
    assert seq % BLOCK_ROWS == 0
    tables = _retention_tables()
    swa_tables = _swa_tables(q_g, k_g)
    consts = tables + (gn_g, gn_b) + swa_tables
    merge_weights = (w_ret_o, w_swa_o, w_out)
    small = (ffn_g, w_router, b_router)
    hbm = pl.BlockSpec(memory_space=pl.ANY)
    full = lambda a: pl.BlockSpec(a.shape, lambda g: (0,) * a.ndim)
    cur = lambda w: pl.BlockSpec((BLOCK_ROWS, w), lambda g: (jnp.minimum(g, n - 1), 0))
    prv = lambda w: pl.BlockSpec((BLOCK_ROWS, w), lambda g: (jnp.maximum(g - 1, 0), 0))
    chunks = BLOCK_ROWS // RET_CHUNK
    subs_per_block = BLOCK_ROWS // MERGE_SUB
    return pl.pallas_call(
        functools.partial(_block_kernel, seq // BLOCK_ROWS),
        grid=(n + 1,),
        in_specs=([pl.BlockSpec(memory_space=pltpu.SMEM), cur(D_MODEL), prv(D_MODEL), full(mix_g), hbm]
                  + [full(a) for a in consts] + [hbm] * len(merge_weights) + [full(a) for a in small]),
        out_specs=[prv(D_MODEL), prv(HALF_D), prv(ROUTE_LANES),
                   pl.BlockSpec((subs_per_block, 1, ROUTE_LANES), lambda g: (jnp.maximum(g - 1, 0), 0, 0))],
        out_shape=[jax.ShapeDtypeStruct((t, D_MODEL), F32),
                   jax.ShapeDtypeStruct((t, HALF_D), I32),
                   jax.ShapeDtypeStruct((t, ROUTE_LANES), F32),
                   jax.ShapeDtypeStruct((n * subs_per_block, 1, ROUTE_LANES), F32)],
        scratch_shapes=[pltpu.VMEM(w_in.shape, BF16)] + [pltpu.VMEM(w.shape, BF16) for w in merge_weights]
                       + [pltpu.VMEM((2, D_MODEL, W_SLAB), F32), pltpu.SemaphoreType.DMA((2,)),
                        pltpu.VMEM((BLOCK_ROWS, IN_WIDTH), BF16), pltpu.VMEM((BLOCK_ROWS, IN_WIDTH), BF16),
                        pltpu.VMEM((BLOCK_ROWS, RET_W), BF16), pltpu.VMEM((BLOCK_ROWS, SWA_Q), BF16),
                        pltpu.VMEM((RET_HEADS, RET_DK, RET_DV), F32),
                        pltpu.VMEM((chunks, RET_HEADS, RET_DK, RET_DV), BF16),
                        pltpu.VMEM((BLOCK_ROWS + WINDOW, Z_SKV_W), BF16), pltpu.VMEM((WINDOW, Z_SKV_W), BF16)],
        compiler_params=pltpu.CompilerParams(dimension_semantics=("arbitrary",),
                                             vmem_limit_bytes=VMEM_LIMIT),
        name="block",
    )(sinks, x2d, x2d, mix_g, w_in, *consts, *merge_weights, *small)


def _plan_kernel(cnt_ref, route_ref, base_ref, owner_ref, valid_ref, used_ref, first_ref, next_ref, slot_ref,
                 pos_ref):
    counts = cnt_ref[...]
    nt = counts.shape[0]
    exact = functools.partial(jnp.dot, precision=lax.Precision.HIGHEST, preferred_element_type=F32)
    total = jnp.sum(counts, axis=0, keepdims=True)
    tiles_e = jnp.ceil(total * (1.0 / ROW_TILE))
    k = lax.broadcasted_iota(jnp.int32, (ROUTE_LANES, ROUTE_LANES), 0)
    lane = lax.broadcasted_iota(jnp.int32, (ROUTE_LANES, ROUTE_LANES), 1)
    tile_end = exact(jnp.broadcast_to(tiles_e, (8, ROUTE_LANES)), jnp.where(k <= lane, 1.0, 0.0))[0:1]
    tile_start = tile_end - tiles_e
    r = lax.broadcasted_iota(jnp.int32, (nt, nt), 0)
    c = lax.broadcasted_iota(jnp.int32, (nt, nt), 1)
    before = exact(jnp.where(c < r, 1.0, 0.0), counts)
    base_ref[...] = tile_start * ROW_TILE + before

    used = jnp.max(tile_end, axis=-1, keepdims=True)
    tile = k.astype(F32)
    clamped = jnp.minimum(tile, jnp.maximum(used - 1.0, 0.0))
    is_expert = lane < N_EXPERTS
    owner = jnp.sum(jnp.where(jnp.logical_and(is_expert, tile_end <= clamped), 1.0, 0.0), axis=-1, keepdims=True)
    owner = jnp.minimum(owner, N_EXPERTS - 1.0)
    mine = lane.astype(F32) == owner
    pick = lambda v: jnp.sum(jnp.where(mine, v, 0.0), axis=-1, keepdims=True)
    t1 = tile[:, 0:1]
    valid = jnp.clip(pick(total) - (t1 - pick(tile_start)) * ROW_TILE, 0.0, float(ROW_TILE))
    valid = jnp.where(t1 < used, valid, 0.0)
    owner_ref[...] = jnp.broadcast_to(owner, owner_ref.shape).astype(I32)
    valid_ref[...] = jnp.broadcast_to(valid, valid_ref.shape).astype(I32)
    used_ref[...] = jnp.broadcast_to(used, used_ref.shape).astype(I32)

    nonempty = jnp.where(jnp.logical_and(lane[0:1] < N_EXPERTS, tiles_e > 0.0), 1.0, 0.0)
    rank = exact(jnp.broadcast_to(nonempty, (8, ROUTE_LANES)), jnp.where(k < lane, 1.0, 0.0))[0:1]
    nonempty_col = jnp.sum(jnp.where(k == lane, jnp.broadcast_to(nonempty, k.shape), 0.0), axis=-1, keepdims=True)
    later = jnp.where(jnp.logical_and(k > lane, nonempty_col > 0.0), tile, float(ROUTE_LANES))
    nxt_e = jnp.min(later, axis=0, keepdims=True)
    nxt_e = jnp.where(nxt_e < float(ROUTE_LANES), nxt_e, -1.0)
    first = jnp.where(jnp.logical_and(t1 == pick(tile_start), t1 < used), 1.0, 0.0)
    my_rank = pick(rank)
    slot = my_rank - 2.0 * jnp.floor(my_rank * 0.5)
    first_ref[...] = jnp.broadcast_to(first, first_ref.shape).astype(I32)
    next_ref[...] = jnp.broadcast_to(pick(nxt_e), next_ref.shape).astype(I32)
    slot_ref[...] = jnp.broadcast_to(slot, slot_ref.shape).astype(I32)

    sel_row = lax.broadcasted_iota(jnp.int32, (8, ROUTE_LANES), 0)
    sel_lane = lax.broadcasted_iota(jnp.int32, (8, ROUTE_LANES), 1)
    tok_r = lax.broadcasted_iota(jnp.int32, (MERGE_SUB, MERGE_SUB), 0)
    tok_c = lax.broadcasted_iota(jnp.int32, (MERGE_SUB, MERGE_SUB), 1)
    eye = jnp.where(tok_r == tok_c, 1.0, 0.0).astype(BF16)
    upto = jnp.where(tok_r <= tok_c, 1.0, 0.0).astype(BF16)
    out_row = lax.broadcasted_iota(jnp.int32, (8, MERGE_SUB), 0)
    sum_experts = lambda a: jnp.sum(a, axis=0, keepdims=True)

    def sub_positions(s, carry):
        route = route_ref[pl.ds(pl.multiple_of(s * MERGE_SUB, MERGE_SUB), MERGE_SUB), :].astype(BF16)
        record_t = _dot_tn(route, eye)
        incl_t = _dot_tn(route, upto)[:N_EXPERTS]
        both_t = record_t[:N_EXPERTS]
        second_t = record_t[LANE_SECOND0:LANE_SECOND0 + N_EXPERTS]
        rank2 = sum_experts(second_t * incl_t) - 1.0
        rank1 = sum_experts(both_t * incl_t) - rank2 - 2.0
        base_row = jnp.where(sel_lane < N_EXPERTS, jnp.broadcast_to(base_ref[pl.ds(s, 1), :], (8, ROUTE_LANES)), 0.0)
        hi = jnp.floor(base_row * (1.0 / 256.0))
        lo = base_row - 256.0 * hi
        hi2 = pltpu.roll(hi, LANE_SECOND0, axis=1)
        lo2 = pltpu.roll(lo, LANE_SECOND0, axis=1)
        table = (jnp.where(sel_row == 0, hi, 0.0) + jnp.where(sel_row == 1, lo, 0.0)
                 + jnp.where(sel_row == 2, hi2, 0.0) + jnp.where(sel_row == 3, lo2, 0.0))
        r = _dot_nt(table.astype(BF16), route)
        base_both = 256.0 * r[0:1] + r[1:2]
        base2 = 256.0 * r[2:3] + r[3:4]
        out = jnp.where(out_row == 0, base_both - base2 + rank1, jnp.where(out_row == 1, base2 + rank2, 0.0))
        pos_ref[s] = out.astype(I32)
        return carry

    lax.fori_loop(0, nt, sub_positions, 0, unroll=4)


def _tile_plan(cnt, route, n_tiles):
    assert n_tiles <= ROUTE_LANES
    nt = cnt.shape[0]
    sq = jax.ShapeDtypeStruct((ROUTE_LANES, ROUTE_LANES), I32)
    _, owner, valid, used, first, nxt, slot, pos = pl.pallas_call(
        _plan_kernel,
        out_shape=[jax.ShapeDtypeStruct((nt, ROUTE_LANES), F32), sq, sq,
                   jax.ShapeDtypeStruct((8, ROUTE_LANES), I32), sq, sq, sq,
                   jax.ShapeDtypeStruct((nt, 8, MERGE_SUB), I32)],
        compiler_params=pltpu.CompilerParams(vmem_limit_bytes=VMEM_LIMIT),
        name="tile_plan",
    )(cnt.reshape(nt, ROUTE_LANES), route)
    per_tile = [a[:n_tiles, 0] for a in (owner, valid, first, nxt, slot)]
    return per_tile, used[0, :1], pos[:, 0, :], pos[:, 1, :]


def _sc_mesh():
    return plsc.VectorSubcoreMesh(core_axis_name="c", subcore_axis_name="s")


def _sc_worker():
    return lax.axis_index("s") * SC_CORES + lax.axis_index("c")


def _dispatch(hp, pos1, pos2, n_rows):
    t = hp.shape[0]
    per_worker = t // SC_WORKERS
    k = per_worker // SC_CHUNK
    idx = pltpu.VMEM((k, SC_CHUNK), I32)

    @functools.partial(pl.kernel, mesh=_sc_mesh(), out_type=jax.ShapeDtypeStruct((n_rows, HALF_D), I32),
                       scratch_types=[idx, idx, pltpu.VMEM((SC_CHUNK, HALF_D), I32)], name="moe_dispatch")
    def run(h_hbm, p1_hbm, p2_hbm, xs_hbm, p1_v, p2_v, rows_v):
        wid = _sc_worker()
        pltpu.sync_copy(p1_hbm.at[wid], p1_v)
        pltpu.sync_copy(p2_hbm.at[wid], p2_v)
        for j in range(k):
            pltpu.sync_copy(h_hbm.at[pl.ds(wid * per_worker + j * SC_CHUNK, SC_CHUNK)], rows_v)
            pltpu.sync_copy(rows_v, xs_hbm.at[p1_v.at[j]])
            pltpu.sync_copy(rows_v, xs_hbm.at[p2_v.at[j]])

    return run(hp, pos1.reshape(SC_WORKERS, k, SC_CHUNK), pos2.reshape(SC_WORKERS, k, SC_CHUNK))


def _combine_gather(ys, pos1, pos2):
    t = pos1.size
    per_worker = t // SC_WORKERS
    k = per_worker // SC_CHUNK
    idx = pltpu.VMEM((k, SC_CHUNK), I32)
    out = jax.ShapeDtypeStruct((t, HALF_D), I32)

    @functools.partial(pl.kernel, mesh=_sc_mesh(), out_type=(out, out),
                       scratch_types=[idx, idx, pltpu.VMEM((SC_CHUNK, HALF_D), I32)], name="moe_combine")
    def run(ys_hbm, p1_hbm, p2_hbm, g1_hbm, g2_hbm, p1_v, p2_v, rows_v):
        wid = _sc_worker()
        pltpu.sync_copy(p1_hbm.at[wid], p1_v)
        pltpu.sync_copy(p2_hbm.at[wid], p2_v)
        for j in range(k):
            dst = pl.ds(wid * per_worker + j * SC_CHUNK, SC_CHUNK)
            pltpu.sync_copy(ys_hbm.at[p1_v.at[j]], rows_v)
            pltpu.sync_copy(rows_v, g1_hbm.at[dst])
            pltpu.sync_copy(ys_hbm.at[p2_v.at[j]], rows_v)
            pltpu.sync_copy(rows_v, g2_hbm.at[dst])

    return run(ys, pos1.reshape(SC_WORKERS, k, SC_CHUNK), pos2.reshape(SC_WORKERS, k, SC_CHUNK))


def _experts_kernel(owner_ref, valid_ref, first_ref, next_ref, slot_ref, used_ref,
                    xs_ref, wg_hbm, wu_hbm, wd_hbm, ys_ref,
                    wg_f, wu_f, wd_f, wg_bf, wu_bf, wd_bf, w_sem):
    i = pl.program_id(0)
    valid = valid_ref[i]

    def fetch(expert, slot):
        pairs = ((wg_hbm, wg_f), (wu_hbm, wu_f), (wd_hbm, wd_f))
        return [pltpu.make_async_copy(w.at[expert], buf.at[slot], w_sem.at[slot, n])
                for n, (w, buf) in enumerate(pairs)]

    @pl.when(i == 0)
    def _():
        for copy in fetch(owner_ref[0], 0):
            copy.start()

    @pl.when(first_ref[i] == 1)
    def _():
        slot = slot_ref[i]
        for copy in fetch(owner_ref[i], slot):
            copy.wait()
        wg_bf[...] = wg_f[slot].astype(BF16)
        wu_bf[...] = wu_f[slot].astype(BF16)
        wd_bf[...] = wd_f[slot].astype(BF16)

        @pl.when(next_ref[i] >= 0)
        def _():
            for copy in fetch(next_ref[i], 1 - slot):
                copy.start()

    @pl.when(i < used_ref[0])
    def _():
        row = lax.broadcasted_iota(jnp.int32, (ROW_TILE, HALF_D), 0)
        packed = jnp.where(row < valid, xs_ref[...], 0)
        hi, lo = _unpack_halves(packed)
        hi = hi.astype(BF16)
        lo = lo.astype(BF16)
        hg = _dot(hi, wg_bf[:HALF_D, :]) + _dot(lo, wg_bf[HALF_D:, :])
        hu = _dot(hi, wu_bf[:HALF_D, :]) + _dot(lo, wu_bf[HALF_D:, :])
        hid = (hg * _sigmoid(hg) * hu).astype(BF16)
        ys_ref[...] = _pack_halves(_dot(hid, wd_bf[...]))


def _experts(xs, per_tile, used, wg, wu, wd):
    n_tiles = per_tile[0].shape[0]
    tile = lambda i, *prefetch: (jnp.minimum(i, prefetch[-1][0] - 1), 0)
    hbm = pl.BlockSpec(memory_space=pl.ANY)
    up_shape, down_shape = (D_MODEL, EXPERT_FF), (EXPERT_FF, D_MODEL)
    grid_spec = pltpu.PrefetchScalarGridSpec(
        num_scalar_prefetch=len(per_tile) + 1,
        grid=(n_tiles,),
        in_specs=[pl.BlockSpec((ROW_TILE, HALF_D), tile), hbm, hbm, hbm],
        out_specs=pl.BlockSpec((ROW_TILE, HALF_D), tile),
        scratch_shapes=[pltpu.VMEM((2,) + up_shape, F32), pltpu.VMEM((2,) + up_shape, F32),
                        pltpu.VMEM((2,) + down_shape, F32),
                        pltpu.VMEM(up_shape, BF16), pltpu.VMEM(up_shape, BF16), pltpu.VMEM(down_shape, BF16),
                        pltpu.SemaphoreType.DMA((2, 3))],
    )
    return pl.pallas_call(
        _experts_kernel,
        grid_spec=grid_spec,
        out_shape=jax.ShapeDtypeStruct((n_tiles * ROW_TILE, HALF_D), I32),
        compiler_params=pltpu.CompilerParams(dimension_semantics=("arbitrary",),
                                             vmem_limit_bytes=VMEM_LIMIT),
        name="experts",
    )(*per_tile, used, xs, wg, wu, wd)


def _ple_kernel(x1_ref, g1_ref, g2_ref, route_ref, p_ref, wp_f32, pg_ref, gg_ref, wgate_f32, o_ref,
                wp_ref, wgate_ref):
    @pl.when(pl.program_id(0) == 0)
    def _():
        wp_ref[...] = wp_f32[...].astype(BF16)
        wgate_ref[...] = wgate_f32[...].astype(BF16)

    route = route_ref[...]
    lane = lax.broadcasted_iota(jnp.int32, route.shape, 1)
    pick = lambda ln: jnp.sum(jnp.where(lane == ln, route, 0.0), axis=-1, keepdims=True)
    c1 = pick(LANE_C1)
    c2 = pick(LANE_C2)
    a_hi, a_lo = _unpack_halves(g1_ref[...])
    b_hi, b_lo = _unpack_halves(g2_ref[...])
    x = x1_ref[...] + jnp.concatenate([c1 * a_hi + c2 * b_hi, c1 * a_lo + c2 * b_lo], axis=1)
    ple = _rms(_dot(p_ref[...].astype(BF16), wp_ref[...]), pg_ref[...])
    gate = _sigmoid(_dot(_rms(x, gg_ref[...]).astype(BF16), wgate_ref[...]))
    o_ref[...] = x + gate * ple


def _ple(x1, g1, g2, route, p2d, w_ple, ple_g, gate_g, w_gate):
    t = x1.shape[0]
    row = lambda w: pl.BlockSpec((PLE_TM, w), lambda i: (i, 0))
    full = lambda a: pl.BlockSpec(a.shape, lambda i: (0,) * a.ndim)
    return pl.pallas_call(
        _ple_kernel,
        grid=(t // PLE_TM,),
        in_specs=[row(D_MODEL), row(HALF_D), row(HALF_D), row(ROUTE_LANES), row(PLE_DIM),
                  full(w_ple), full(ple_g), full(gate_g), full(w_gate)],
        out_specs=row(D_MODEL),
        out_shape=jax.ShapeDtypeStruct((t, D_MODEL), F32),
        scratch_shapes=[pltpu.VMEM(w_ple.shape, BF16), pltpu.VMEM(w_gate.shape, BF16)],
        compiler_params=pltpu.CompilerParams(dimension_semantics=("arbitrary",),
                                             vmem_limit_bytes=VMEM_LIMIT),
        name="ple",
    )(x1, g1, g2, route, p2d, w_ple, ple_g, gate_g, w_gate)


def kernel(x, p, mix_norm_g, w_in, ret_gn_g, ret_gn_b, w_ret_o, q_norm_g, k_norm_g, attn_sinks,
           w_swa_o, w_out, ffn_norm_g, w_router_group, b_router_group, w_router_expert,
           b_router_expert, w_exp_gate, w_exp_up, w_exp_down, ple_gate_norm_g, w_ple_gate,
           w_ple, ple_norm_g):
    batch, seq, d = x.shape
    t = batch * seq
    depth = w_in.shape[0]
    n_tiles = 2 * t // ROW_TILE + N_EXPERTS
    row = lambda a: a.reshape(1, -1)
    x2d = x.reshape(t, d)
    for i in range(depth):
        pad = ROUTE_LANES - N_GROUPS - N_EXPERTS
        w_router = jnp.pad(jnp.concatenate([w_router_expert[i], w_router_group[i]], axis=1), ((0, 0), (0, pad)))
        b_router = jnp.pad(jnp.concatenate([b_router_expert[i], b_router_group[i]]), (0, pad)).reshape(1, -1)
        w_router = w_router.astype(BF16)
        x1, hp, route, cnt = _block(x2d, row(mix_norm_g[i]), w_in[i], row(ret_gn_g[i]),
                                    row(ret_gn_b[i]), attn_sinks[i], q_norm_g[i], k_norm_g[i],
                                    w_ret_o[i], w_swa_o[i], w_out[i],
                                    row(ffn_norm_g[i]), w_router, b_router, seq)
        per_tile, used, pos1, pos2 = _tile_plan(cnt, route, n_tiles)
        xs = _dispatch(hp, pos1, pos2, n_tiles * ROW_TILE)
        ys = _experts(xs, per_tile, used, w_exp_gate[i], w_exp_up[i], w_exp_down[i])
        g1, g2 = _combine_gather(ys, pos1, pos2)
        x2d = _ple(x1, g1, g2, route, p[i].reshape(t, PLE_DIM), w_ple[i], row(ple_norm_g[i]),
                   row(ple_gate_norm_g[i]), w_ple_gate[i])
    return x2d.reshape(batch, seq, d)
```

```python
import functools

import jax
import jax.numpy as jnp
import numpy as np
from jax import lax
from jax.experimental import pallas as pl
from jax.experimental.pallas import tpu as pltpu
from jax.experimental.pallas import tpu_sc as plsc

F32 = jnp.float32
BF16 = jnp.bfloat16
I32 = jnp.int32

EPS = 1e-6
D_MODEL = 1024
PLE_DIM = 256
RET_HEADS = 4
RET_DK = 128
RET_DV = 128
RET_CHUNK = 128
SWA_HEADS = 8
SWA_KV_HEADS = 2
SWA_GROUP = SWA_HEADS // SWA_KV_HEADS
SWA_HD = 64
WINDOW = 128
N_GROUPS = 4
EXPERTS_PER_GROUP = 8
N_EXPERTS = N_GROUPS * EXPERTS_PER_GROUP
EXPERT_FF = 256

RET_W = RET_HEADS * RET_DK
SWA_Q = SWA_HEADS * SWA_HD
SWA_KV = SWA_KV_HEADS * SWA_HD
Z_RET_W = 4 * RET_W
Z_SKV_W = 2 * SWA_KV
Z_GATE_W = 2 * D_MODEL
COL_SQ = Z_RET_W
COL_SKV = COL_SQ + SWA_Q
COL_GATE = COL_SKV + Z_SKV_W
IN_WIDTH = COL_GATE + Z_GATE_W

ROUTE_LANES = 128
LANE_GROUP0 = N_EXPERTS
LANE_E1, LANE_E2, LANE_C1, LANE_C2 = 32, 33, 34, 35
LANE_SECOND0 = 64
NEG_INF = -1e30

HALF_D = D_MODEL // 2
ROW_TILE = 512
SC_CORES = 2
SC_SUBCORES = 16
SC_WORKERS = SC_CORES * SC_SUBCORES
SC_CHUNK = 128

VMEM_LIMIT = 56 * 1024 * 1024

BLOCK_ROWS = 512
W_SLAB = 256
RET_ROWS = 512
MERGE_SUB = 256
PLE_TM = 1024


def _rms(x, g):
    ms = jnp.mean(x * x, axis=-1, keepdims=True)
    return x * lax.rsqrt(ms + EPS) * g


def _sigmoid(x):
    return 1.0 / (1.0 + jnp.exp(-x))


def _dot(a, b):
    return jnp.dot(a, b, preferred_element_type=F32)


def _dot_nt(a, b):
    return lax.dot_general(a, b, (((1,), (1,)), ((), ())), preferred_element_type=F32)


def _dot_tn(a, b):
    return lax.dot_general(a, b, (((0,), (0,)), ((), ())), preferred_element_type=F32)


def _pack_halves(v):
    return pltpu.pack_elementwise([v[:, :HALF_D], v[:, HALF_D:]], packed_dtype=BF16)


def _unpack_halves(p):
    words = lax.bitcast_convert_type(p, jnp.uint32)
    return tuple(pltpu.unpack_elementwise(words, index=k, packed_dtype=BF16, unpacked_dtype=F32) for k in range(2))


def _retention_tables():
    f32 = np.float32
    h = RET_HEADS
    c = RET_CHUNK
    log_gamma = np.log1p(-np.exp2(f32(-5.0) - np.arange(h, dtype=f32))).astype(f32)
    pos = np.arange(c, dtype=f32)
    diff = pos[:, None] - pos[None, :]
    decay = np.where(diff[None] >= 0.0, np.exp(np.maximum(diff, f32(0.0))[None] * log_gamma[:, None, None]), f32(0.0))
    scale = f32(RET_DK ** -0.5)
    dmask = decay * scale
    zeta = np.exp((f32(c - 1.0) - pos)[None, :] * log_gamma[:, None]) * scale
    xi = np.exp((pos + f32(1.0))[None, :] * log_gamma[:, None])
    cdec = np.exp(f32(c) * log_gamma)
    bc = lambda a: np.ascontiguousarray(np.broadcast_to(a[:, :, None], (h, c, c)), dtype=f32)
    return (dmask.astype(f32), bc(zeta), bc(xi),
            np.ascontiguousarray(np.broadcast_to(cdec[:, None, None], (h, c, c)), dtype=f32))


def _swa_tables(q_g, k_g):
    f32 = np.float32
    qi = np.arange(WINDOW)[:, None]
    sj = np.arange(2 * WINDOW)[None, :]
    rel = qi + WINDOW - sj
    ok = (rel >= 0) & (rel < WINDOW)
    slopes = np.exp2(f32(-8.0) * np.arange(1, SWA_HEADS + 1, dtype=f32) / f32(SWA_HEADS)).astype(f32)
    bias = np.where(ok[None], -slopes[:, None, None] * rel.astype(f32)[None], f32(NEG_INF)).astype(f32)
    bias = bias.reshape(SWA_KV_HEADS, SWA_GROUP * WINDOW, 2 * WINDOW)
    avg = lambda heads: jnp.asarray(np.kron(np.eye(heads, dtype=f32), np.full((SWA_HD, SWA_HD), 1.0 / SWA_HD, f32)),
                                    dtype=BF16)
    qg = jnp.tile(q_g.reshape(1, SWA_HD), (1, SWA_HEADS)) * (SWA_HD ** -0.5)
    kg = jnp.tile(k_g.reshape(1, SWA_HD), (1, SWA_KV_HEADS))
    return qg, kg, avg(SWA_HEADS), avg(SWA_KV_HEADS), jnp.asarray(bias)


def _block_kernel(per_seq, sinks_ref, x_ref, xp_ref, g_ref, w_hbm, dmask_ref, zeta_ref, xi_ref, cdec_ref,
                  gng_ref, gnb_ref, qg_ref, kg_ref, bq_ref, bk_ref, bias_ref,
                  wro_hbm, wso_hbm, wout_hbm, fg_ref, wr_ref, br_ref,
                  x1_ref, hp_ref, route_ref, cnt_ref,
                  w_ref, wro_ref, wso_ref, wout_ref, wtmp_ref, w_sem,
                  zc_ref, zn_ref, yr_ref, ys_ref, state_ref, prev_ref, band_ref, tail_ref):
    g = pl.program_id(0)
    seq_start = lax.rem(g - 1, per_seq) == 0

    @pl.when(g == 0)
    def _():
        slabs = [(src, dst, c0) for src, dst in ((w_hbm, w_ref), (wro_hbm, wro_ref), (wso_hbm, wso_ref),
                                                 (wout_hbm, wout_ref))
                 for c0 in range(0, dst.shape[1], W_SLAB)]

        def slab(k):
            src, dst, c0 = slabs[k]
            return pltpu.make_async_copy(src.at[:, pl.ds(c0, W_SLAB)], wtmp_ref.at[k % 2, pl.ds(0, dst.shape[0])],
                                         w_sem.at[k % 2])

        slab(0).start()
        for k, (_, dst, c0) in enumerate(slabs):
            if k + 1 < len(slabs):
                slab(k + 1).start()
            slab(k).wait()
            dst[:, c0:c0 + W_SLAB] = wtmp_ref[k % 2, :dst.shape[0]].astype(BF16)
        zc_ref[...] = jnp.zeros_like(zc_ref)
        tail_ref[...] = jnp.zeros_like(tail_ref)
        state_ref[...] = jnp.zeros_like(state_ref)

    @pl.when(seq_start)
    def _():
        state_ref[...] = jnp.zeros_like(state_ref)

    band_ref[:WINDOW, :] = tail_ref[...]
    band_ref[WINDOW:, :] = zc_ref[:, COL_SKV:COL_GATE]

    h = _rms(x_ref[...], g_ref[...]).astype(BF16)

    def project(c0, width):
        zn_ref[:, c0:c0 + width] = _dot(h, w_ref[:, c0:c0 + width]).astype(BF16)

    cw = 512
    proj_pieces = [functools.partial(project, j * cw, cw) for j in range(COL_SKV // cw)]
    proj_pieces.append(functools.partial(project, COL_SKV, Z_SKV_W))
    proj_pieces += [functools.partial(project, COL_GATE + j * cw, cw) for j in range(Z_GATE_W // cw)]

    def retention_rows(t):
        r_base = t * RET_ROWS
        n_chunks = RET_ROWS // RET_CHUNK
        part = lambda c, p, hd: zc_ref[r_base + c * RET_CHUNK:r_base + (c + 1) * RET_CHUNK,
                                       p * RET_W + hd * RET_DK:p * RET_W + (hd + 1) * RET_DK]
        for hd in range(RET_HEADS):
            state = state_ref[hd]
            for c in range(n_chunks):
                prev_ref[t * n_chunks + c, hd] = state.astype(BF16)
                kz = (part(c, 1, hd).astype(F32) * zeta_ref[hd]).astype(BF16)
                state = cdec_ref[hd] * state + _dot_tn(kz, part(c, 2, hd))
            state_ref[hd] = state
        out_rows = []
        for c in range(n_chunks):
            heads = []
            for hd in range(RET_HEADS):
                q = part(c, 0, hd)
                scores = _dot_nt(q, part(c, 1, hd)) * dmask_ref[hd]
                y = _dot(scores.astype(BF16), part(c, 2, hd)) + _dot(q, prev_ref[t * n_chunks + c, hd]) * xi_ref[hd]
                mu = jnp.mean(y, axis=-1, keepdims=True)
                d = y - mu
                var = jnp.mean(d * d, axis=-1, keepdims=True)
                hs = slice(hd * RET_DV, (hd + 1) * RET_DV)
                yn = d * lax.rsqrt(var + EPS) * gng_ref[:, hs] + gnb_ref[:, hs]
                gate = part(c, 3, hd).astype(F32)
                heads.append((gate * _sigmoid(gate) * yn).astype(BF16))
            out_rows.append(jnp.concatenate(heads, axis=1))
        yr_ref[r_base:r_base + RET_ROWS, :] = jnp.concatenate(out_rows, axis=0)

    group_rows = SWA_GROUP * WINDOW
    lo_q = lax.broadcasted_iota(jnp.int32, (WINDOW, 2 * SWA_HD), 1) < SWA_HD
    lo_k = lax.broadcasted_iota(jnp.int32, (2 * WINDOW, 2 * SWA_HD), 1) < SWA_HD
    in_prev = lax.broadcasted_iota(jnp.int32, (group_rows, 2 * WINDOW), 1) < WINDOW
    row_head = lax.broadcasted_iota(jnp.int32, (group_rows, 1), 0) // WINDOW

    def both_heads(v):
        swapped = pltpu.roll(v, SWA_HD, axis=1)
        return jnp.where(lo_k, v, swapped), jnp.where(lo_k, swapped, v)

    def swa_block(n):
        rows = slice(n * WINDOW, (n + 1) * WINDOW)
        kv = band_ref[n * WINDOW:(n + 2) * WINDOW, :]
        kf = kv[:, :SWA_KV].astype(F32)
        kn = kf * lax.rsqrt(_dot((kf * kf).astype(BF16), bk_ref[...]) + EPS) * kg_ref[...]
        keys = [a.astype(BF16) for a in both_heads(kn)]
        vals = [a.astype(BF16) for a in both_heads(kv[:, SWA_KV:].astype(F32))]
        qf = zc_ref[rows, COL_SQ:COL_SKV].astype(F32)
        qn = qf * lax.rsqrt(_dot((qf * qf).astype(BF16), bq_ref[...]) + EPS) * qg_ref[...]
        outs = []
        for kh in range(SWA_KV_HEADS):
            parts = []
            for j in range(SWA_GROUP // 2):
                c0 = (kh * SWA_GROUP + 2 * j) * SWA_HD
                two = qn[:, c0:c0 + 2 * SWA_HD]
                parts += [jnp.where(lo_q, two, 0.0), jnp.where(lo_q, 0.0, two)]
            qs = jnp.concatenate(parts, axis=0).astype(BF16)
            s = _dot_nt(qs, keys[kh]) + bias_ref[kh]
            if n == 0:
                s = jnp.where(jnp.logical_and(seq_start, in_prev), NEG_INF, s)
            sink = sinks_ref[kh * SWA_GROUP]
            for gi in range(1, SWA_GROUP):
                sink = jnp.where(row_head == gi, sinks_ref[kh * SWA_GROUP + gi], sink)
            m = jnp.maximum(jnp.max(s, axis=-1, keepdims=True), sink)
            p = jnp.exp(s - m)
            denom = jnp.sum(p, axis=-1, keepdims=True) + jnp.exp(sink - m)
            o = _dot(p.astype(BF16), vals[kh]) * (1.0 / denom)
            for j in range(SWA_GROUP // 2):
                even = o[(2 * j) * WINDOW:(2 * j + 1) * WINDOW]
                odd = o[(2 * j + 1) * WINDOW:(2 * j + 2) * WINDOW]
                outs.append(jnp.where(lo_q, even, odd).astype(BF16))
        ys_ref[rows, :] = jnp.concatenate(outs, axis=1)

    mix_pieces = [functools.partial(retention_rows, t) for t in range(BLOCK_ROWS // RET_ROWS)]
    mix_pieces += [functools.partial(swa_block, n) for n in range(BLOCK_ROWS // WINDOW)]

    subs = []

    def merge_sub(s):
        subs.append(_merge_rows(slice(s * MERGE_SUB, (s + 1) * MERGE_SUB), yr_ref, ys_ref, zc_ref, xp_ref, wro_ref,
                                wso_ref, wout_ref, fg_ref, wr_ref, br_ref))

    other_pieces = mix_pieces + [functools.partial(merge_sub, s) for s in range(BLOCK_ROWS // MERGE_SUB)]

    n_other = len(other_pieces)
    for k, piece in enumerate(other_pieces):
        for proj_piece in proj_pieces[k * len(proj_pieces) // n_other:(k + 1) * len(proj_pieces) // n_other]:
            proj_piece()
        piece()
    x1_ref[...] = jnp.concatenate([s[0] for s in subs], axis=0)
    hp_ref[...] = jnp.concatenate([s[1] for s in subs], axis=0)
    route_ref[...] = jnp.concatenate([s[2] for s in subs], axis=0)
    for s, sub in enumerate(subs):
        cnt_ref[s] = sub[3]

    tail_ref[...] = zc_ref[BLOCK_ROWS - WINDOW:, COL_SKV:COL_GATE]
    zc_ref[...] = zn_ref[...]


def _merge_rows(rows, yr_ref, ys_ref, z_ref, x_ref, wro_ref, wso_ref, wout_ref, fg_ref, wr_ref, br_ref):
    a = _dot(yr_ref[rows, :], wro_ref[...])
    b = _dot(ys_ref[rows, :], wso_ref[...])
    gate_r = z_ref[rows, COL_GATE:COL_GATE + D_MODEL].astype(F32)
    gate_s = z_ref[rows, COL_GATE + D_MODEL:].astype(F32)
    merged = (_sigmoid(gate_r) * a + _sigmoid(gate_s) * b).astype(BF16)
    x1 = x_ref[rows, :] + _dot(merged, wout_ref[...])
    h2 = _rms(x1, fg_ref[...])
    packed = _pack_halves(h2)

    logits = _dot(h2.astype(BF16), wr_ref[...]) + br_ref[...]
    lane = lax.broadcasted_iota(jnp.int32, logits.shape, 1)
    big = jnp.int32(ROUTE_LANES)
    is_group = jnp.logical_and(lane >= LANE_GROUP0, lane < LANE_GROUP0 + N_GROUPS)
    gl = jnp.where(is_group, logits, NEG_INF)
    gmax = jnp.max(gl, axis=-1, keepdims=True)
    g_w = 1.0 / jnp.sum(jnp.exp(gl - gmax), axis=-1, keepdims=True)
    g_sel = jnp.min(jnp.where(gl == gmax, lane, big), axis=-1, keepdims=True) - LANE_GROUP0
    in_group = jnp.logical_and(lane < N_EXPERTS, (lane >> 3) == g_sel)
    el = jnp.where(in_group, logits, NEG_INF)
    m1 = jnp.max(el, axis=-1, keepdims=True)
    i1 = jnp.min(jnp.where(el == m1, lane, big), axis=-1, keepdims=True)
    el2 = jnp.where(lane == i1, NEG_INF, el)
    m2 = jnp.max(el2, axis=-1, keepdims=True)
    i2 = jnp.min(jnp.where(el2 == m2, lane, big), axis=-1, keepdims=True)
    e2 = jnp.exp(m2 - m1)
    c1 = g_w / (1.0 + e2)
    c2 = g_w * e2 / (1.0 + e2)
    chosen = jnp.where(jnp.logical_or(lane == i1, lane == i2), 1.0, 0.0)
    route = (chosen
             + jnp.where(lane == LANE_E1, i1.astype(F32), 0.0)
             + jnp.where(lane == LANE_E2, i2.astype(F32), 0.0)
             + jnp.where(lane == LANE_C1, c1, 0.0)
             + jnp.where(lane == LANE_C2, c2, 0.0)
             + jnp.where(lane == i2 + LANE_SECOND0, 1.0, 0.0))
    return x1, packed, route, jnp.sum(chosen, axis=0, keepdims=True)


def _block(x2d, mix_g, w_in, gn_g, gn_b, sinks, q_g, k_g, w_ret_o, w_swa_o, w_out, ffn_g, w_router, b_router,
           seq):
    t = x2d.shape[0]
    n = t // BLOCK_ROWS
    assert seq % BLOCK_ROWS == 0
    tables = _retention_tables()
    swa_tables = _swa_tables(q_g, k_g)
    consts = tables + (gn_g, gn_b) + swa_tables
    merge_weights = (w_ret_o, w_swa_o, w_out)
    small = (ffn_g, w_router, b_router)
    hbm = pl.BlockSpec(memory_space=pl.ANY)
    full = lambda a: pl.BlockSpec(a.shape, lambda g: (0,) * a.ndim)
    cur = lambda w: pl.BlockSpec((BLOCK_ROWS, w), lambda g: (jnp.minimum(g, n - 1), 0))
    prv = lambda w: pl.BlockSpec((BLOCK_ROWS, w), lambda g: (jnp.maximum(g - 1, 0), 0))
    chunks = BLOCK_ROWS // RET_CHUNK
    subs_per_block = BLOCK_ROWS // MERGE_SUB
    return pl.pallas_call(
        functools.partial(_block_kernel, seq // BLOCK_ROWS),
        grid=(n + 1,),
        in_specs=([pl.BlockSpec(memory_space=pltpu.SMEM), cur(D_MODEL), prv(D_MODEL), full(mix_g), hbm]
                  + [full(a) for a in consts] + [hbm] * len(merge_weights) + [full(a) for a in small]),
        out_specs=[prv(D_MODEL), prv(HALF_D), prv(ROUTE_LANES),
                   pl.BlockSpec((subs_per_block, 1, ROUTE_LANES), lambda g: (jnp.maximum(g - 1, 0), 0, 0))],
        out_shape=[jax.ShapeDtypeStruct((t, D_MODEL), F32),
                   jax.ShapeDtypeStruct((t, HALF_D), I32),
                   jax.ShapeDtypeStruct((t, ROUTE_LANES), F32),
                   jax.ShapeDtypeStruct((n * subs_per_block, 1, ROUTE_LANES), F32)],
        scratch_shapes=[pltpu.VMEM(w_in.shape, BF16)] + [pltpu.VMEM(w.shape, BF16) for w in merge_weights]
                       + [pltpu.VMEM((2, D_MODEL, W_SLAB), F32), pltpu.SemaphoreType.DMA((2,)),
                        pltpu.VMEM((BLOCK_ROWS, IN_WIDTH), BF16), pltpu.VMEM((BLOCK_ROWS, IN_WIDTH), BF16),
                        pltpu.VMEM((BLOCK_ROWS, RET_W), BF16), pltpu.VMEM((BLOCK_ROWS, SWA_Q), BF16),
                        pltpu.VMEM((RET_HEADS, RET_DK, RET_DV), F32),
                        pltpu.VMEM((chunks, RET_HEADS, RET_DK, RET_DV), BF16),
                        pltpu.VMEM((BLOCK_ROWS + WINDOW, Z_SKV_W), BF16), pltpu.VMEM((WINDOW, Z_SKV_W), BF16)],
        compiler_params=pltpu.CompilerParams(dimension_semantics=("arbitrary",),
                                             vmem_limit_bytes=VMEM_LIMIT),
        name="block",
    )(sinks, x2d, x2d, mix_g, w_in, *consts, *merge_weights, *small)


def _plan_kernel(cnt_ref, route_ref, base_ref, meta_ref, pos_ref):
    counts = cnt_ref[...]
    nt = counts.shape[0]
    exact = functools.partial(jnp.dot, precision=lax.Precision.HIGHEST, preferred_element_type=F32)
    total = jnp.sum(counts, axis=0, keepdims=True)
    tiles_e = jnp.ceil(total * (1.0 / ROW_TILE))
    k = lax.broadcasted_iota(jnp.int32, (ROUTE_LANES, ROUTE_LANES), 0)
    lane = lax.broadcasted_iota(jnp.int32, (ROUTE_LANES, ROUTE_LANES), 1)
    tile_end = exact(jnp.broadcast_to(tiles_e, (8, ROUTE_LANES)), jnp.where(k <= lane, 1.0, 0.0))[0:1]
    tile_start = tile_end - tiles_e
    r = lax.broadcasted_iota(jnp.int32, (nt, nt), 0)
    c = lax.broadcasted_iota(jnp.int32, (nt, nt), 1)
    before = exact(jnp.where(c < r, 1.0, 0.0), counts)
    base_ref[...] = tile_start * ROW_TILE + before

    used = jnp.max(tile_end, axis=-1, keepdims=True)
    tile = k.astype(F32)
    clamped = jnp.minimum(tile, jnp.maximum(used - 1.0, 0.0))
    is_expert = lane < N_EXPERTS
    owner = jnp.sum(jnp.where(jnp.logical_and(is_expert, tile_end <= clamped), 1.0, 0.0), axis=-1, keepdims=True)
    owner = jnp.minimum(owner, N_EXPERTS - 1.0)
    mine = lane.astype(F32) == owner
    pick = lambda v: jnp.sum(jnp.where(mine, v, 0.0), axis=-1, keepdims=True)
    t1 = tile[:, 0:1]
    valid = jnp.clip(pick(total) - (t1 - pick(tile_start)) * ROW_TILE, 0.0, float(ROW_TILE))
    valid = jnp.where(t1 < used, valid, 0.0)

    nonempty = jnp.where(jnp.logical_and(lane[0:1] < N_EXPERTS, tiles_e > 0.0), 1.0, 0.0)
    rank = exact(jnp.broadcast_to(nonempty, (8, ROUTE_LANES)), jnp.where(k < lane, 1.0, 0.0))[0:1]
    nonempty_col = jnp.sum(jnp.where(k == lane, jnp.broadcast_to(nonempty, k.shape), 0.0), axis=-1, keepdims=True)
    later = jnp.where(jnp.logical_and(k > lane, nonempty_col > 0.0), tile, float(ROUTE_LANES))
    nxt_e = jnp.min(later, axis=0, keepdims=True)
    nxt_e = jnp.where(nxt_e < float(ROUTE_LANES), nxt_e, -1.0)
    first = jnp.where(jnp.logical_and(t1 == pick(tile_start), t1 < used), 1.0, 0.0)
    my_rank = pick(rank)
    slot = my_rank - 2.0 * jnp.floor(my_rank * 0.5)
    fields = (owner, valid, first, pick(nxt_e), slot, jnp.broadcast_to(used, owner.shape))
    by_tile = sum(jnp.where(lane == n, f, 0.0) for n, f in enumerate(fields))
    meta_ref[...] = by_tile.T[:8, :].astype(I32)

    sel_row = lax.broadcasted_iota(jnp.int32, (8, ROUTE_LANES), 0)
    sel_lane = lax.broadcasted_iota(jnp.int32, (8, ROUTE_LANES), 1)
    tok_r = lax.broadcasted_iota(jnp.int32, (MERGE_SUB, MERGE_SUB), 0)
    tok_c = lax.broadcasted_iota(jnp.int32, (MERGE_SUB, MERGE_SUB), 1)
    eye = jnp.where(tok_r == tok_c, 1.0, 0.0).astype(BF16)
    upto = jnp.where(tok_r <= tok_c, 1.0, 0.0).astype(BF16)
    out_row = lax.broadcasted_iota(jnp.int32, (8, MERGE_SUB), 0)
    sum_experts = lambda a: jnp.sum(a, axis=0, keepdims=True)

    def sub_positions(s, carry):
        route = route_ref[pl.ds(pl.multiple_of(s * MERGE_SUB, MERGE_SUB), MERGE_SUB), :].astype(BF16)
        record_t = _dot_tn(route, eye)
        incl_t = _dot_tn(route, upto)[:N_EXPERTS]
        both_t = record_t[:N_EXPERTS]
        second_t = record_t[LANE_SECOND0:LANE_SECOND0 + N_EXPERTS]
        rank2 = sum_experts(second_t * incl_t) - 1.0
        rank1 = sum_experts(both_t * incl_t) - rank2 - 2.0
        base_row = jnp.where(sel_lane < N_EXPERTS, jnp.broadcast_to(base_ref[pl.ds(s, 1), :], (8, ROUTE_LANES)), 0.0)
        hi = jnp.floor(base_row * (1.0 / 256.0))
        lo = base_row - 256.0 * hi
        hi2 = pltpu.roll(hi, LANE_SECOND0, axis=1)
        lo2 = pltpu.roll(lo, LANE_SECOND0, axis=1)
        table = (jnp.where(sel_row == 0, hi, 0.0) + jnp.where(sel_row == 1, lo, 0.0)
                 + jnp.where(sel_row == 2, hi2, 0.0) + jnp.where(sel_row == 3, lo2, 0.0))
        r = _dot_nt(table.astype(BF16), route)
        base_both = 256.0 * r[0:1] + r[1:2]
        base2 = 256.0 * r[2:3] + r[3:4]
        out = jnp.where(out_row == 0, base_both - base2 + rank1, jnp.where(out_row == 1, base2 + rank2, 0.0))
        pos_ref[s] = out.astype(I32)
        return carry

    lax.fori_loop(0, nt, sub_positions, 0, unroll=4)


def _tile_plan(cnt, route, n_tiles):
    assert n_tiles <= ROUTE_LANES
    nt = cnt.shape[0]
    _, meta, pos = pl.pallas_call(
        _plan_kernel,
        out_shape=[jax.ShapeDtypeStruct((nt, ROUTE_LANES), F32), jax.ShapeDtypeStruct((8, ROUTE_LANES), I32),
                   jax.ShapeDtypeStruct((nt, 8, MERGE_SUB), I32)],
        compiler_params=pltpu.CompilerParams(vmem_limit_bytes=VMEM_LIMIT),
        name="tile_plan",
    )(cnt.reshape(nt, ROUTE_LANES), route)
    return meta, pos[:, 0, :], pos[:, 1, :]


def _sc_mesh():
    return plsc.VectorSubcoreMesh(core_axis_name="c", subcore_axis_name="s")


def _sc_worker():
    return lax.axis_index("s") * SC_CORES + lax.axis_index("c")


def _dispatch(hp, pos1, pos2, n_rows):
    t = hp.shape[0]
    per_worker = t // SC_WORKERS
    k = per_worker // SC_CHUNK
    idx = pltpu.VMEM((k, SC_CHUNK), I32)

    @functools.partial(pl.kernel, mesh=_sc_mesh(), out_type=jax.ShapeDtypeStruct((n_rows, HALF_D), I32),
                       scratch_types=[idx, idx, pltpu.VMEM((SC_CHUNK, HALF_D), I32)], name="moe_dispatch")
    def run(h_hbm, p1_hbm, p2_hbm, xs_hbm, p1_v, p2_v, rows_v):
        wid = _sc_worker()
        pltpu.sync_copy(p1_hbm.at[wid], p1_v)
        pltpu.sync_copy(p2_hbm.at[wid], p2_v)
        for j in range(k):
            pltpu.sync_copy(h_hbm.at[pl.ds(wid * per_worker + j * SC_CHUNK, SC_CHUNK)], rows_v)
            pltpu.sync_copy(rows_v, xs_hbm.at[p1_v.at[j]])
            pltpu.sync_copy(rows_v, xs_hbm.at[p2_v.at[j]])

    return run(hp, pos1.reshape(SC_WORKERS, k, SC_CHUNK), pos2.reshape(SC_WORKERS, k, SC_CHUNK))


def _combine_gather(ys, pos1, pos2):
    t = pos1.size
    per_worker = t // SC_WORKERS
    k = per_worker // SC_CHUNK
    idx = pltpu.VMEM((k, SC_CHUNK), I32)
    out = jax.ShapeDtypeStruct((t, HALF_D), I32)

    @functools.partial(pl.kernel, mesh=_sc_mesh(), out_type=(out, out),
                       scratch_types=[idx, idx, pltpu.VMEM((SC_CHUNK, HALF_D), I32)], name="moe_combine")
    def run(ys_hbm, p1_hbm, p2_hbm, g1_hbm, g2_hbm, p1_v, p2_v, rows_v):
        wid = _sc_worker()
        pltpu.sync_copy(p1_hbm.at[wid], p1_v)
        pltpu.sync_copy(p2_hbm.at[wid], p2_v)
        for j in range(k):
            dst = pl.ds(wid * per_worker + j * SC_CHUNK, SC_CHUNK)
            pltpu.sync_copy(ys_hbm.at[p1_v.at[j]], rows_v)
            pltpu.sync_copy(rows_v, g1_hbm.at[dst])
            pltpu.sync_copy(ys_hbm.at[p2_v.at[j]], rows_v)
            pltpu.sync_copy(rows_v, g2_hbm.at[dst])

    return run(ys, pos1.reshape(SC_WORKERS, k, SC_CHUNK), pos2.reshape(SC_WORKERS, k, SC_CHUNK))


META_OWNER, META_VALID, META_FIRST, META_NEXT, META_SLOT, META_USED = range(6)


def _experts_kernel(meta_ref, xs_ref, wg_hbm, wu_hbm, wd_hbm, ys_ref,
                    wg_f, wu_f, wd_f, wg_bf, wu_bf, wd_bf, w_sem):
    i = pl.program_id(0)
    valid = meta_ref[META_VALID, i]
    owner = meta_ref[META_OWNER, i]
    nxt = meta_ref[META_NEXT, i]

    def fetch(expert, slot):
        pairs = ((wg_hbm, wg_f), (wu_hbm, wu_f), (wd_hbm, wd_f))
        return [pltpu.make_async_copy(w.at[expert], buf.at[slot], w_sem.at[slot, n])
                for n, (w, buf) in enumerate(pairs)]

    @pl.when(i == 0)
    def _():
        for copy in fetch(owner, 0):
            copy.start()

    @pl.when(meta_ref[META_FIRST, i] == 1)
    def _():
        slot = meta_ref[META_SLOT, i]
        for copy in fetch(owner, slot):
            copy.wait()
        wg_bf[...] = wg_f[slot].astype(BF16)
        wu_bf[...] = wu_f[slot].astype(BF16)
        wd_bf[...] = wd_f[slot].astype(BF16)

        @pl.when(nxt >= 0)
        def _():
            for copy in fetch(nxt, 1 - slot):
                copy.start()

    @pl.when(i < meta_ref[META_USED, 0])
    def _():
        row = lax.broadcasted_iota(jnp.int32, (ROW_TILE, HALF_D), 0)
        packed = jnp.where(row < valid, xs_ref[...], 0)
        hi, lo = _unpack_halves(packed)
        hi = hi.astype(BF16)
        lo = lo.astype(BF16)
        hg = _dot(hi, wg_bf[:HALF_D, :]) + _dot(lo, wg_bf[HALF_D:, :])
        hu = _dot(hi, wu_bf[:HALF_D, :]) + _dot(lo, wu_bf[HALF_D:, :])
        hid = (hg * _sigmoid(hg) * hu).astype(BF16)
        ys_ref[...] = _pack_halves(_dot(hid, wd_bf[...]))


def _experts(xs, meta, n_tiles, wg, wu, wd):
    tile = lambda i, meta: (jnp.minimum(i, meta[META_USED, 0] - 1), 0)
    hbm = pl.BlockSpec(memory_space=pl.ANY)
    up_shape, down_shape = (D_MODEL, EXPERT_FF), (EXPERT_FF, D_MODEL)
    grid_spec = pltpu.PrefetchScalarGridSpec(
        num_scalar_prefetch=1,
        grid=(n_tiles,),
        in_specs=[pl.BlockSpec((ROW_TILE, HALF_D), tile), hbm, hbm, hbm],
        out_specs=pl.BlockSpec((ROW_TILE, HALF_D), tile),
        scratch_shapes=[pltpu.VMEM((2,) + up_shape, F32), pltpu.VMEM((2,) + up_shape, F32),
                        pltpu.VMEM((2,) + down_shape, F32),
                        pltpu.VMEM(up_shape, BF16), pltpu.VMEM(up_shape, BF16), pltpu.VMEM(down_shape, BF16),
                        pltpu.SemaphoreType.DMA((2, 3))],
    )
    return pl.pallas_call(
        _experts_kernel,
        grid_spec=grid_spec,
        out_shape=jax.ShapeDtypeStruct((n_tiles * ROW_TILE, HALF_D), I32),
        compiler_params=pltpu.CompilerParams(dimension_semantics=("arbitrary",),
                                             vmem_limit_bytes=VMEM_LIMIT),
        name="experts",
    )(meta, xs, wg, wu, wd)


def _ple_kernel(x1_ref, g1_ref, g2_ref, route_ref, p_ref, wp_f32, pg_ref, gg_ref, wgate_f32, o_ref,
                wp_ref, wgate_ref):
    @pl.when(pl.program_id(0) == 0)
    def _():
        wp_ref[...] = wp_f32[...].astype(BF16)
        wgate_ref[...] = wgate_f32[...].astype(BF16)

    route = route_ref[...]
    lane = lax.broadcasted_iota(jnp.int32, route.shape, 1)
    pick = lambda ln: jnp.sum(jnp.where(lane == ln, route, 0.0), axis=-1, keepdims=True)
    c1 = pick(LANE_C1)
    c2 = pick(LANE_C2)
    a_hi, a_lo = _unpack_halves(g1_ref[...])
    b_hi, b_lo = _unpack_halves(g2_ref[...])
    x = x1_ref[...] + jnp.concatenate([c1 * a_hi + c2 * b_hi, c1 * a_lo + c2 * b_lo], axis=1)
    ple = _rms(_dot(p_ref[...].astype(BF16), wp_ref[...]), pg_ref[...])
    gate = _sigmoid(_dot(_rms(x, gg_ref[...]).astype(BF16), wgate_ref[...]))
    o_ref[...] = x + gate * ple


def _ple(x1, g1, g2, route, p2d, w_ple, ple_g, gate_g, w_gate):
    t = x1.shape[0]
    row = lambda w: pl.BlockSpec((PLE_TM, w), lambda i: (i, 0))
    full = lambda a: pl.BlockSpec(a.shape, lambda i: (0,) * a.ndim)
    return pl.pallas_call(
        _ple_kernel,
        grid=(t // PLE_TM,),
        in_specs=[row(D_MODEL), row(HALF_D), row(HALF_D), row(ROUTE_LANES), row(PLE_DIM),
                  full(w_ple), full(ple_g), full(gate_g), full(w_gate)],
        out_specs=row(D_MODEL),
        out_shape=jax.ShapeDtypeStruct((t, D_MODEL), F32),
        scratch_shapes=[pltpu.VMEM(w_ple.shape, BF16), pltpu.VMEM(w_gate.shape, BF16)],
        compiler_params=pltpu.CompilerParams(dimension_semantics=("arbitrary",),
                                             vmem_limit_bytes=VMEM_LIMIT),
        name="ple",
    )(x1, g1, g2, route, p2d, w_ple, ple_g, gate_g, w_gate)


def kernel(x, p, mix_norm_g, w_in, ret_gn_g, ret_gn_b, w_ret_o, q_norm_g, k_norm_g, attn_sinks,
           w_swa_o, w_out, ffn_norm_g, w_router_group, b_router_group, w_router_expert,
           b_router_expert, w_exp_gate, w_exp_up, w_exp_down, ple_gate_norm_g, w_ple_gate,
           w_ple, ple_norm_g):
    batch, seq, d = x.shape
    t = batch * seq
    depth = w_in.shape[0]
    n_tiles = 2 * t // ROW_TILE + N_EXPERTS
    row = lambda a: a.reshape(1, -1)
    x2d = x.reshape(t, d)
    for i in range(depth):
        pad = ROUTE_LANES - N_GROUPS - N_EXPERTS
        w_router = jnp.pad(jnp.concatenate([w_router_expert[i], w_router_group[i]], axis=1), ((0, 0), (0, pad)))
        b_router = jnp.pad(jnp.concatenate([b_router_expert[i], b_router_group[i]]), (0, pad)).reshape(1, -1)
        w_router = w_router.astype(BF16)
        x1, hp, route, cnt = _block(x2d, row(mix_norm_g[i]), w_in[i], row(ret_gn_g[i]),
                                    row(ret_gn_b[i]), attn_sinks[i], q_norm_g[i], k_norm_g[i],
                                    w_ret_o[i], w_swa_o[i], w_out[i],
                                    row(ffn_norm_g[i]), w_router, b_router, seq)
        meta, pos1, pos2 = _tile_plan(cnt, route, n_tiles)
        xs = _dispatch(hp, pos1, pos2, n_tiles * ROW_TILE)
        ys = _experts(xs, meta, n_tiles, w_exp_gate[i], w_exp_up[i], w_exp_down[i])
        g1, g2 = _combine_gather(ys, pos1, pos2)
        x2d = _ple(x1, g1, g2, route, p[i].reshape(t, PLE_DIM), w_ple[i], row(ple_norm_g[i]),
                   row(ple_gate_norm_g[i]), w_ple_gate[i])
    return x2d.reshape(batch, seq, d)
```

```python
import functools

import jax
import jax.numpy as jnp
import numpy as np
from jax import lax
from jax.experimental import pallas as pl
from jax.experimental.pallas import tpu as pltpu
from jax.experimental.pallas import tpu_sc as plsc

F32 = jnp.float32
BF16 = jnp.bfloat16
I32 = jnp.int32

EPS = 1e-6
D_MODEL = 1024
PLE_DIM = 256
RET_HEADS = 4
RET_DK = 128
RET_DV = 128
RET_CHUNK = 128
SWA_HEADS = 8
SWA_KV_HEADS = 2
SWA_GROUP = SWA_HEADS // SWA_KV_HEADS
SWA_HD = 64
WINDOW = 128
N_GROUPS = 4
EXPERTS_PER_GROUP = 8
N_EXPERTS = N_GROUPS * EXPERTS_PER_GROUP
EXPERT_FF = 256

RET_W = RET_HEADS * RET_DK
SWA_Q = SWA_HEADS * SWA_HD
SWA_KV = SWA_KV_HEADS * SWA_HD
Z_RET_W = 4 * RET_W
Z_SKV_W = 2 * SWA_KV
Z_GATE_W = 2 * D_MODEL
COL_SQ = Z_RET_W
COL_SKV = COL_SQ + SWA_Q
COL_GATE = COL_SKV + Z_SKV_W
IN_WIDTH = COL_GATE + Z_GATE_W

ROUTE_LANES = 128
LANE_GROUP0 = N_EXPERTS
LANE_E1, LANE_E2, LANE_C1, LANE_C2 = 32, 33, 34, 35
LANE_SECOND0 = 64
NEG_INF = -1e30

HALF_D = D_MODEL // 2
ROW_TILE = 512
SC_CORES = 2
SC_SUBCORES = 16
SC_WORKERS = SC_CORES * SC_SUBCORES
SC_CHUNK = 128

VMEM_LIMIT = 56 * 1024 * 1024

BLOCK_ROWS = 512
W_SLAB = 256
RET_ROWS = 512
MERGE_SUB = 256
PLE_TM = 1024


def _rms(x, g):
    ms = jnp.mean(x * x, axis=-1, keepdims=True)
    return x * lax.rsqrt(ms + EPS) * g


def _sigmoid(x):
    return 1.0 / (1.0 + jnp.exp(-x))


def _dot(a, b):
    return jnp.dot(a, b, preferred_element_type=F32)


def _dot_nt(a, b):
    return lax.dot_general(a, b, (((1,), (1,)), ((), ())), preferred_element_type=F32)


def _dot_tn(a, b):
    return lax.dot_general(a, b, (((0,), (0,)), ((), ())), preferred_element_type=F32)


def _pack_halves(v):
    return pltpu.pack_elementwise([v[:, :HALF_D], v[:, HALF_D:]], packed_dtype=BF16)


def _unpack_halves(p):
    words = lax.bitcast_convert_type(p, jnp.uint32)
    return tuple(pltpu.unpack_elementwise(words, index=k, packed_dtype=BF16, unpacked_dtype=F32) for k in range(2))


def _retention_tables():
    f32 = np.float32
    h = RET_HEADS
    c = RET_CHUNK
    log_gamma = np.log1p(-np.exp2(f32(-5.0) - np.arange(h, dtype=f32))).astype(f32)
    pos = np.arange(c, dtype=f32)
    diff = pos[:, None] - pos[None, :]
    decay = np.where(diff[None] >= 0.0, np.exp(np.maximum(diff, f32(0.0))[None] * log_gamma[:, None, None]), f32(0.0))
    scale = f32(RET_DK ** -0.5)
    dmask = decay * scale
    zeta = np.exp((f32(c - 1.0) - pos)[None, :] * log_gamma[:, None]) * scale
    xi = np.exp((pos + f32(1.0))[None, :] * log_gamma[:, None])
    cdec = np.exp(f32(c) * log_gamma)
    bc = lambda a: np.ascontiguousarray(np.broadcast_to(a[:, :, None], (h, c, c)), dtype=f32)
    return (dmask.astype(f32), bc(zeta), bc(xi),
            np.ascontiguousarray(np.broadcast_to(cdec[:, None, None], (h, c, c)), dtype=f32))


def _swa_tables(q_g, k_g):
    f32 = np.float32
    qi = np.arange(WINDOW)[:, None]
    sj = np.arange(2 * WINDOW)[None, :]
    rel = qi + WINDOW - sj
    ok = (rel >= 0) & (rel < WINDOW)
    slopes = np.exp2(f32(-8.0) * np.arange(1, SWA_HEADS + 1, dtype=f32) / f32(SWA_HEADS)).astype(f32)
    bias = np.where(ok[None], -slopes[:, None, None] * rel.astype(f32)[None], f32(NEG_INF)).astype(f32)
    bias = bias.reshape(SWA_KV_HEADS, SWA_GROUP * WINDOW, 2 * WINDOW)
    avg = lambda heads: jnp.asarray(np.kron(np.eye(heads, dtype=f32), np.full((SWA_HD, SWA_HD), 1.0 / SWA_HD, f32)),
                                    dtype=BF16)
    qg = jnp.tile(q_g.reshape(1, SWA_HD), (1, SWA_HEADS)) * (SWA_HD ** -0.5)
    kg = jnp.tile(k_g.reshape(1, SWA_HD), (1, SWA_KV_HEADS))
    return qg, kg, avg(SWA_HEADS), avg(SWA_KV_HEADS), jnp.asarray(bias)


def _block_kernel(per_seq, sinks_ref, x_ref, xp_ref, g_ref, w_hbm, dmask_ref, zeta_ref, xi_ref, cdec_ref,
                  gng_ref, gnb_ref, qg_ref, kg_ref, bq_ref, bk_ref, bias_ref,
                  wro_hbm, wso_hbm, wout_hbm, fg_ref, wr_ref, br_ref,
                  x1_ref, hp_ref, route_ref, cnt_ref,
                  w_ref, wro_ref, wso_ref, wout_ref, wtmp_ref, w_sem,
                  zc_ref, zn_ref, yr_ref, ys_ref, state_ref, prev_ref, band_ref, tail_ref):
    g = pl.program_id(0)
    seq_start = lax.rem(g - 1, per_seq) == 0

    slabs = [(src, dst, c0) for src, dst in ((wro_hbm, wro_ref), (wso_hbm, wso_ref), (wout_hbm, wout_ref),
                                             (w_hbm, w_ref))
             for c0 in range(0, dst.shape[1], W_SLAB)]
    first_proj_slab = len(slabs) - IN_WIDTH // W_SLAB

    def slab(k):
        src, dst, c0 = slabs[k]
        return pltpu.make_async_copy(src.at[:, pl.ds(c0, W_SLAB)], wtmp_ref.at[k % 2, pl.ds(0, dst.shape[0])],
                                     w_sem.at[k % 2])

    def land_slabs(k0, k1):
        for k in range(k0, k1):
            if k + 1 < len(slabs):
                slab(k + 1).start()
            slab(k).wait()
            _, dst, c0 = slabs[k]
            dst[:, c0:c0 + W_SLAB] = wtmp_ref[k % 2, :dst.shape[0]].astype(BF16)

    @pl.when(g == 0)
    def _():
        slab(0).start()
        land_slabs(0, first_proj_slab)
        zc_ref[...] = jnp.zeros_like(zc_ref)
        tail_ref[...] = jnp.zeros_like(tail_ref)
        state_ref[...] = jnp.zeros_like(state_ref)

    @pl.when(seq_start)
    def _():
        state_ref[...] = jnp.zeros_like(state_ref)

    band_ref[:WINDOW, :] = tail_ref[...]
    band_ref[WINDOW:, :] = zc_ref[:, COL_SKV:COL_GATE]

    h = _rms(x_ref[...], g_ref[...]).astype(BF16)

    def project(c0, width):
        k0 = first_proj_slab + c0 // W_SLAB
        pl.when(g == 0)(functools.partial(land_slabs, k0, k0 + width // W_SLAB))
        zn_ref[:, c0:c0 + width] = _dot(h, w_ref[:, c0:c0 + width]).astype(BF16)

    cw = 512
    proj_pieces = [functools.partial(project, j * cw, cw) for j in range(COL_SKV // cw)]
    proj_pieces.append(functools.partial(project, COL_SKV, Z_SKV_W))
    proj_pieces += [functools.partial(project, COL_GATE + j * cw, cw) for j in range(Z_GATE_W // cw)]

    def retention_rows(t):
        r_base = t * RET_ROWS
        n_chunks = RET_ROWS // RET_CHUNK
        part = lambda c, p, hd: zc_ref[r_base + c * RET_CHUNK:r_base + (c + 1) * RET_CHUNK,
                                       p * RET_W + hd * RET_DK:p * RET_W + (hd + 1) * RET_DK]
        for hd in range(RET_HEADS):
            state = state_ref[hd]
            for c in range(n_chunks):
                prev_ref[t * n_chunks + c, hd] = state.astype(BF16)
                kz = (part(c, 1, hd).astype(F32) * zeta_ref[hd]).astype(BF16)
                state = cdec_ref[hd] * state + _dot_tn(kz, part(c, 2, hd))
            state_ref[hd] = state
        out_rows = []
        for c in range(n_chunks):
            heads = []
            for hd in range(RET_HEADS):
                q = part(c, 0, hd)
                scores = _dot_nt(q, part(c, 1, hd)) * dmask_ref[hd]
                y = _dot(scores.astype(BF16), part(c, 2, hd)) + _dot(q, prev_ref[t * n_chunks + c, hd]) * xi_ref[hd]
                mu = jnp.mean(y, axis=-1, keepdims=True)
                d = y - mu
                var = jnp.mean(d * d, axis=-1, keepdims=True)
                hs = slice(hd * RET_DV, (hd + 1) * RET_DV)
                yn = d * lax.rsqrt(var + EPS) * gng_ref[:, hs] + gnb_ref[:, hs]
                gate = part(c, 3, hd).astype(F32)
                heads.append((gate * _sigmoid(gate) * yn).astype(BF16))
            out_rows.append(jnp.concatenate(heads, axis=1))
        yr_ref[r_base:r_base + RET_ROWS, :] = jnp.concatenate(out_rows, axis=0)

    group_rows = SWA_GROUP * WINDOW
    lo_q = lax.broadcasted_iota(jnp.int32, (WINDOW, 2 * SWA_HD), 1) < SWA_HD
    lo_k = lax.broadcasted_iota(jnp.int32, (2 * WINDOW, 2 * SWA_HD), 1) < SWA_HD
    in_prev = lax.broadcasted_iota(jnp.int32, (group_rows, 2 * WINDOW), 1) < WINDOW
    row_head = lax.broadcasted_iota(jnp.int32, (group_rows, 1), 0) // WINDOW

    def both_heads(v):
        swapped = pltpu.roll(v, SWA_HD, axis=1)
        return jnp.where(lo_k, v, swapped), jnp.where(lo_k, swapped, v)

    def swa_block(n):
        rows = slice(n * WINDOW, (n + 1) * WINDOW)
        kv = band_ref[n * WINDOW:(n + 2) * WINDOW, :]
        kf = kv[:, :SWA_KV].astype(F32)
        kn = kf * lax.rsqrt(_dot((kf * kf).astype(BF16), bk_ref[...]) + EPS) * kg_ref[...]
        keys = [a.astype(BF16) for a in both_heads(kn)]
        vals = [a.astype(BF16) for a in both_heads(kv[:, SWA_KV:].astype(F32))]
        qf = zc_ref[rows, COL_SQ:COL_SKV].astype(F32)
        qn = qf * lax.rsqrt(_dot((qf * qf).astype(BF16), bq_ref[...]) + EPS) * qg_ref[...]
        outs = []
        for kh in range(SWA_KV_HEADS):
            parts = []
            for j in range(SWA_GROUP // 2):
                c0 = (kh * SWA_GROUP + 2 * j) * SWA_HD
                two = qn[:, c0:c0 + 2 * SWA_HD]
                parts += [jnp.where(lo_q, two, 0.0), jnp.where(lo_q, 0.0, two)]
            qs = jnp.concatenate(parts, axis=0).astype(BF16)
            s = _dot_nt(qs, keys[kh]) + bias_ref[kh]
            if n == 0:
                s = jnp.where(jnp.logical_and(seq_start, in_prev), NEG_INF, s)
            sink = sinks_ref[kh * SWA_GROUP]
            for gi in range(1, SWA_GROUP):
                sink = jnp.where(row_head == gi, sinks_ref[kh * SWA_GROUP + gi], sink)
            m = jnp.maximum(jnp.max(s, axis=-1, keepdims=True), sink)
            p = jnp.exp(s - m)
            denom = jnp.sum(p, axis=-1, keepdims=True) + jnp.exp(sink - m)
            o = _dot(p.astype(BF16), vals[kh]) * (1.0 / denom)
            for j in range(SWA_GROUP // 2):
                even = o[(2 * j) * WINDOW:(2 * j + 1) * WINDOW]
                odd = o[(2 * j + 1) * WINDOW:(2 * j + 2) * WINDOW]
                outs.append(jnp.where(lo_q, even, odd).astype(BF16))
        ys_ref[rows, :] = jnp.concatenate(outs, axis=1)

    mix_pieces = [functools.partial(retention_rows, t) for t in range(BLOCK_ROWS // RET_ROWS)]
    mix_pieces += [functools.partial(swa_block, n) for n in range(BLOCK_ROWS // WINDOW)]

    subs = []

    def merge_sub(s):
        subs.append(_merge_rows(slice(s * MERGE_SUB, (s + 1) * MERGE_SUB), yr_ref, ys_ref, zc_ref, xp_ref, wro_ref,
                                wso_ref, wout_ref, fg_ref, wr_ref, br_ref))

    other_pieces = mix_pieces + [functools.partial(merge_sub, s) for s in range(BLOCK_ROWS // MERGE_SUB)]

    n_other = len(other_pieces)
    for k, piece in enumerate(other_pieces):
        for proj_piece in proj_pieces[k * len(proj_pieces) // n_other:(k + 1) * len(proj_pieces) // n_other]:
            proj_piece()
        piece()
    x1_ref[...] = jnp.concatenate([s[0] for s in subs], axis=0)
    hp_ref[...] = jnp.concatenate([s[1] for s in subs], axis=0)
    route_ref[...] = jnp.concatenate([s[2] for s in subs], axis=0)
    for s, sub in enumerate(subs):
        cnt_ref[s] = sub[3]

    tail_ref[...] = zc_ref[BLOCK_ROWS - WINDOW:, COL_SKV:COL_GATE]
    zc_ref[...] = zn_ref[...]


def _merge_rows(rows, yr_ref, ys_ref, z_ref, x_ref, wro_ref, wso_ref, wout_ref, fg_ref, wr_ref, br_ref):
    a = _dot(yr_ref[rows, :], wro_ref[...])
    b = _dot(ys_ref[rows, :], wso_ref[...])
    gate_r = z_ref[rows, COL_GATE:COL_GATE + D_MODEL].astype(F32)
    gate_s = z_ref[rows, COL_GATE + D_MODEL:].astype(F32)
    merged = (_sigmoid(gate_r) * a + _sigmoid(gate_s) * b).astype(BF16)
    x1 = x_ref[rows, :] + _dot(merged, wout_ref[...])
    h2 = _rms(x1, fg_ref[...])
    packed = _pack_halves(h2)

    logits = _dot(h2.astype(BF16), wr_ref[...]) + br_ref[...]
    lane = lax.broadcasted_iota(jnp.int32, logits.shape, 1)
    big = jnp.int32(ROUTE_LANES)
    is_group = jnp.logical_and(lane >= LANE_GROUP0, lane < LANE_GROUP0 + N_GROUPS)
    gl = jnp.where(is_group, logits, NEG_INF)
    gmax = jnp.max(gl, axis=-1, keepdims=True)
    g_w = 1.0 / jnp.sum(jnp.exp(gl - gmax), axis=-1, keepdims=True)
    g_sel = jnp.min(jnp.where(gl == gmax, lane, big), axis=-1, keepdims=True) - LANE_GROUP0
    in_group = jnp.logical_and(lane < N_EXPERTS, (lane >> 3) == g_sel)
    el = jnp.where(in_group, logits, NEG_INF)
    m1 = jnp.max(el, axis=-1, keepdims=True)
    i1 = jnp.min(jnp.where(el == m1, lane, big), axis=-1, keepdims=True)
    el2 = jnp.where(lane == i1, NEG_INF, el)
    m2 = jnp.max(el2, axis=-1, keepdims=True)
    i2 = jnp.min(jnp.where(el2 == m2, lane, big), axis=-1, keepdims=True)
    e2 = jnp.exp(m2 - m1)
    c1 = g_w / (1.0 + e2)
    c2 = g_w * e2 / (1.0 + e2)
    chosen = jnp.where(jnp.logical_or(lane == i1, lane == i2), 1.0, 0.0)
    route = (chosen
             + jnp.where(lane == LANE_E1, i1.astype(F32), 0.0)
             + jnp.where(lane == LANE_E2, i2.astype(F32), 0.0)
             + jnp.where(lane == LANE_C1, c1, 0.0)
             + jnp.where(lane == LANE_C2, c2, 0.0)
             + jnp.where(lane == i2 + LANE_SECOND0, 1.0, 0.0))
    return x1, packed, route, jnp.sum(chosen, axis=0, keepdims=True)


def _block(x2d, mix_g, w_in, gn_g, gn_b, sinks, q_g, k_g, w_ret_o, w_swa_o, w_out, ffn_g, w_router, b_router,
           seq):
    t = x2d.shape[0]
    n = t // BLOCK_ROWS
    assert seq % BLOCK_ROWS == 0
    tables = _retention_tables()
    swa_tables = _swa_tables(q_g, k_g)
    consts = tables + (gn_g, gn_b) + swa_tables
    merge_weights = (w_ret_o, w_swa_o, w_out)
    small = (ffn_g, w_router, b_router)
    hbm = pl.BlockSpec(memory_space=pl.ANY)
    full = lambda a: pl.BlockSpec(a.shape, lambda g: (0,) * a.ndim)
    cur = lambda w: pl.BlockSpec((BLOCK_ROWS, w), lambda g: (jnp.minimum(g, n - 1), 0))
    prv = lambda w: pl.BlockSpec((BLOCK_ROWS, w), lambda g: (jnp.maximum(g - 1, 0), 0))
    chunks = BLOCK_ROWS // RET_CHUNK
    subs_per_block = BLOCK_ROWS // MERGE_SUB
    return pl.pallas_call(
        functools.partial(_block_kernel, seq // BLOCK_ROWS),
        grid=(n + 1,),
        in_specs=([pl.BlockSpec(memory_space=pltpu.SMEM), cur(D_MODEL), prv(D_MODEL), full(mix_g), hbm]
                  + [full(a) for a in consts] + [hbm] * len(merge_weights) + [full(a) for a in small]),
        out_specs=[prv(D_MODEL), prv(HALF_D), prv(ROUTE_LANES),
                   pl.BlockSpec((subs_per_block, 1, ROUTE_LANES), lambda g: (jnp.maximum(g - 1, 0), 0, 0))],
        out_shape=[jax.ShapeDtypeStruct((t, D_MODEL), F32),
                   jax.ShapeDtypeStruct((t, HALF_D), I32),
                   jax.ShapeDtypeStruct((t, ROUTE_LANES), F32),
                   jax.ShapeDtypeStruct((n * subs_per_block, 1, ROUTE_LANES), F32)],
        scratch_shapes=[pltpu.VMEM(w_in.shape, BF16)] + [pltpu.VMEM(w.shape, BF16) for w in merge_weights]
                       + [pltpu.VMEM((2, D_MODEL, W_SLAB), F32), pltpu.SemaphoreType.DMA((2,)),
                        pltpu.VMEM((BLOCK_ROWS, IN_WIDTH), BF16), pltpu.VMEM((BLOCK_ROWS, IN_WIDTH), BF16),
                        pltpu.VMEM((BLOCK_ROWS, RET_W), BF16), pltpu.VMEM((BLOCK_ROWS, SWA_Q), BF16),
                        pltpu.VMEM((RET_HEADS, RET_DK, RET_DV), F32),
                        pltpu.VMEM((chunks, RET_HEADS, RET_DK, RET_DV), BF16),
                        pltpu.VMEM((BLOCK_ROWS + WINDOW, Z_SKV_W), BF16), pltpu.VMEM((WINDOW, Z_SKV_W), BF16)],
        compiler_params=pltpu.CompilerParams(dimension_semantics=("arbitrary",),
                                             vmem_limit_bytes=VMEM_LIMIT),
        name="block",
    )(sinks, x2d, x2d, mix_g, w_in, *consts, *merge_weights, *small)


def _plan_kernel(cnt_ref, route_ref, base_ref, meta_ref, pos_ref):
    counts = cnt_ref[...]
    nt = counts.shape[0]
    exact = functools.partial(jnp.dot, precision=lax.Precision.HIGHEST, preferred_element_type=F32)
    total = jnp.sum(counts, axis=0, keepdims=True)
    tiles_e = jnp.ceil(total * (1.0 / ROW_TILE))
    k = lax.broadcasted_iota(jnp.int32, (ROUTE_LANES, ROUTE_LANES), 0)
    lane = lax.broadcasted_iota(jnp.int32, (ROUTE_LANES, ROUTE_LANES), 1)
    tile_end = exact(jnp.broadcast_to(tiles_e, (8, ROUTE_LANES)), jnp.where(k <= lane, 1.0, 0.0))[0:1]
    tile_start = tile_end - tiles_e
    r = lax.broadcasted_iota(jnp.int32, (nt, nt), 0)
    c = lax.broadcasted_iota(jnp.int32, (nt, nt), 1)
    before = exact(jnp.where(c < r, 1.0, 0.0), counts)
    base_ref[...] = tile_start * ROW_TILE + before

    used = jnp.max(tile_end, axis=-1, keepdims=True)
    tile = k.astype(F32)
    clamped = jnp.minimum(tile, jnp.maximum(used - 1.0, 0.0))
    is_expert = lane < N_EXPERTS
    owner = jnp.sum(jnp.where(jnp.logical_and(is_expert, tile_end <= clamped), 1.0, 0.0), axis=-1, keepdims=True)
    owner = jnp.minimum(owner, N_EXPERTS - 1.0)
    mine = lane.astype(F32) == owner
    pick = lambda v: jnp.sum(jnp.where(mine, v, 0.0), axis=-1, keepdims=True)
    t1 = tile[:, 0:1]
    valid = jnp.clip(pick(total) - (t1 - pick(tile_start)) * ROW_TILE, 0.0, float(ROW_TILE))
    valid = jnp.where(t1 < used, valid, 0.0)

    nonempty = jnp.where(jnp.logical_and(lane[0:1] < N_EXPERTS, tiles_e > 0.0), 1.0, 0.0)
    rank = exact(jnp.broadcast_to(nonempty, (8, ROUTE_LANES)), jnp.where(k < lane, 1.0, 0.0))[0:1]
    nonempty_col = jnp.sum(jnp.where(k == lane, jnp.broadcast_to(nonempty, k.shape), 0.0), axis=-1, keepdims=True)
    later = jnp.where(jnp.logical_and(k > lane, nonempty_col > 0.0), tile, float(ROUTE_LANES))
    nxt_e = jnp.min(later, axis=0, keepdims=True)
    nxt_e = jnp.where(nxt_e < float(ROUTE_LANES), nxt_e, -1.0)
    first = jnp.where(jnp.logical_and(t1 == pick(tile_start), t1 < used), 1.0, 0.0)
    my_rank = pick(rank)
    slot = my_rank - 2.0 * jnp.floor(my_rank * 0.5)
    fields = (owner, valid, first, pick(nxt_e), slot, jnp.broadcast_to(used, owner.shape))
    by_tile = sum(jnp.where(lane == n, f, 0.0) for n, f in enumerate(fields))
    meta_ref[...] = by_tile.T[:8, :].astype(I32)

    sel_row = lax.broadcasted_iota(jnp.int32, (8, ROUTE_LANES), 0)
    sel_lane = lax.broadcasted_iota(jnp.int32, (8, ROUTE_LANES), 1)
    tok_r = lax.broadcasted_iota(jnp.int32, (MERGE_SUB, MERGE_SUB), 0)
    tok_c = lax.broadcasted_iota(jnp.int32, (MERGE_SUB, MERGE_SUB), 1)
    eye = jnp.where(tok_r == tok_c, 1.0, 0.0).astype(BF16)
    upto = jnp.where(tok_r <= tok_c, 1.0, 0.0).astype(BF16)
    out_row = lax.broadcasted_iota(jnp.int32, (8, MERGE_SUB), 0)
    sum_experts = lambda a: jnp.sum(a, axis=0, keepdims=True)

    def sub_positions(s, carry):
        route = route_ref[pl.ds(pl.multiple_of(s * MERGE_SUB, MERGE_SUB), MERGE_SUB), :].astype(BF16)
        record_t = _dot_tn(route, eye)
        incl_t = _dot_tn(route, upto)[:N_EXPERTS]
        both_t = record_t[:N_EXPERTS]
        second_t = record_t[LANE_SECOND0:LANE_SECOND0 + N_EXPERTS]
        rank2 = sum_experts(second_t * incl_t) - 1.0
        rank1 = sum_experts(both_t * incl_t) - rank2 - 2.0
        base_row = jnp.where(sel_lane < N_EXPERTS, jnp.broadcast_to(base_ref[pl.ds(s, 1), :], (8, ROUTE_LANES)), 0.0)
        hi = jnp.floor(base_row * (1.0 / 256.0))
        lo = base_row - 256.0 * hi
        hi2 = pltpu.roll(hi, LANE_SECOND0, axis=1)
        lo2 = pltpu.roll(lo, LANE_SECOND0, axis=1)
        table = (jnp.where(sel_row == 0, hi, 0.0) + jnp.where(sel_row == 1, lo, 0.0)
                 + jnp.where(sel_row == 2, hi2, 0.0) + jnp.where(sel_row == 3, lo2, 0.0))
        r = _dot_nt(table.astype(BF16), route)
        base_both = 256.0 * r[0:1] + r[1:2]
        base2 = 256.0 * r[2:3] + r[3:4]
        out = jnp.where(out_row == 0, base_both - base2 + rank1, jnp.where(out_row == 1, base2 + rank2, 0.0))
        pos_ref[s] = out.astype(I32)
        return carry

    lax.fori_loop(0, nt, sub_positions, 0, unroll=4)


def _tile_plan(cnt, route, n_tiles):
    assert n_tiles <= ROUTE_LANES
    nt = cnt.shape[0]
    _, meta, pos = pl.pallas_call(
        _plan_kernel,
        out_shape=[jax.ShapeDtypeStruct((nt, ROUTE_LANES), F32), jax.ShapeDtypeStruct((8, ROUTE_LANES), I32),
                   jax.ShapeDtypeStruct((nt, 8, MERGE_SUB), I32)],
        compiler_params=pltpu.CompilerParams(vmem_limit_bytes=VMEM_LIMIT),
        name="tile_plan",
    )(cnt.reshape(nt, ROUTE_LANES), route)
    return meta, pos[:, 0, :], pos[:, 1, :]


def _sc_mesh():
    return plsc.VectorSubcoreMesh(core_axis_name="c", subcore_axis_name="s")


def _sc_worker():
    return lax.axis_index("s") * SC_CORES + lax.axis_index("c")


def _dispatch(hp, pos1, pos2, n_rows):
    t = hp.shape[0]
    per_worker = t // SC_WORKERS
    k = per_worker // SC_CHUNK
    idx = pltpu.VMEM((k, SC_CHUNK), I32)

    @functools.partial(pl.kernel, mesh=_sc_mesh(), out_type=jax.ShapeDtypeStruct((n_rows, HALF_D), I32),
                       scratch_types=[idx, idx, pltpu.VMEM((SC_CHUNK, HALF_D), I32)], name="moe_dispatch")
    def run(h_hbm, p1_hbm, p2_hbm, xs_hbm, p1_v, p2_v, rows_v):
        wid = _sc_worker()
        pltpu.sync_copy(p1_hbm.at[wid], p1_v)
        pltpu.sync_copy(p2_hbm.at[wid], p2_v)
        for j in range(k):
            pltpu.sync_copy(h_hbm.at[pl.ds(wid * per_worker + j * SC_CHUNK, SC_CHUNK)], rows_v)
            pltpu.sync_copy(rows_v, xs_hbm.at[p1_v.at[j]])
            pltpu.sync_copy(rows_v, xs_hbm.at[p2_v.at[j]])

    return run(hp, pos1.reshape(SC_WORKERS, k, SC_CHUNK), pos2.reshape(SC_WORKERS, k, SC_CHUNK))


def _combine_gather(ys, pos1, pos2):
    t = pos1.size
    per_worker = t // SC_WORKERS
    k = per_worker // SC_CHUNK
    idx = pltpu.VMEM((k, SC_CHUNK), I32)
    out = jax.ShapeDtypeStruct((t, HALF_D), I32)

    @functools.partial(pl.kernel, mesh=_sc_mesh(), out_type=(out, out),
                       scratch_types=[idx, idx, pltpu.VMEM((SC_CHUNK, HALF_D), I32)], name="moe_combine")
    def run(ys_hbm, p1_hbm, p2_hbm, g1_hbm, g2_hbm, p1_v, p2_v, rows_v):
        wid = _sc_worker()
        pltpu.sync_copy(p1_hbm.at[wid], p1_v)
        pltpu.sync_copy(p2_hbm.at[wid], p2_v)
        for j in range(k):
            dst = pl.ds(wid * per_worker + j * SC_CHUNK, SC_CHUNK)
            pltpu.sync_copy(ys_hbm.at[p1_v.at[j]], rows_v)
            pltpu.sync_copy(rows_v, g1_hbm.at[dst])
            pltpu.sync_copy(ys_hbm.at[p2_v.at[j]], rows_v)
            pltpu.sync_copy(rows_v, g2_hbm.at[dst])

    return run(ys, pos1.reshape(SC_WORKERS, k, SC_CHUNK), pos2.reshape(SC_WORKERS, k, SC_CHUNK))


META_OWNER, META_VALID, META_FIRST, META_NEXT, META_SLOT, META_USED = range(6)


def _experts_kernel(meta_ref, xs_ref, wg_hbm, wu_hbm, wd_hbm, ys_ref,
                    wg_f, wu_f, wd_f, wg_bf, wu_bf, wd_bf, w_sem):
    i = pl.program_id(0)
    valid = meta_ref[META_VALID, i]
    owner = meta_ref[META_OWNER, i]
    nxt = meta_ref[META_NEXT, i]

    def fetch(expert, slot):
        pairs = ((wg_hbm, wg_f), (wu_hbm, wu_f), (wd_hbm, wd_f))
        return [pltpu.make_async_copy(w.at[expert], buf.at[slot], w_sem.at[slot, n])
                for n, (w, buf) in enumerate(pairs)]

    @pl.when(i == 0)
    def _():
        for copy in fetch(owner, 0):
            copy.start()

    @pl.when(meta_ref[META_FIRST, i] == 1)
    def _():
        slot = meta_ref[META_SLOT, i]
        for copy in fetch(owner, slot):
            copy.wait()
        wg_bf[...] = wg_f[slot].astype(BF16)
        wu_bf[...] = wu_f[slot].astype(BF16)
        wd_bf[...] = wd_f[slot].astype(BF16)

        @pl.when(nxt >= 0)
        def _():
            for copy in fetch(nxt, 1 - slot):
                copy.start()

    @pl.when(i < meta_ref[META_USED, 0])
    def _():
        row = lax.broadcasted_iota(jnp.int32, (ROW_TILE, HALF_D), 0)
        packed = jnp.where(row < valid, xs_ref[...], 0)
        hi, lo = _unpack_halves(packed)
        hi = hi.astype(BF16)
        lo = lo.astype(BF16)
        hg = _dot(hi, wg_bf[:HALF_D, :]) + _dot(lo, wg_bf[HALF_D:, :])
        hu = _dot(hi, wu_bf[:HALF_D, :]) + _dot(lo, wu_bf[HALF_D:, :])
        hid = (hg * _sigmoid(hg) * hu).astype(BF16)
        ys_ref[...] = _pack_halves(_dot(hid, wd_bf[...]))


def _experts(xs, meta, n_tiles, wg, wu, wd):
    tile = lambda i, meta: (jnp.minimum(i, meta[META_USED, 0] - 1), 0)
    hbm = pl.BlockSpec(memory_space=pl.ANY)
    up_shape, down_shape = (D_MODEL, EXPERT_FF), (EXPERT_FF, D_MODEL)
    grid_spec = pltpu.PrefetchScalarGridSpec(
        num_scalar_prefetch=1,
        grid=(n_tiles,),
        in_specs=[pl.BlockSpec((ROW_TILE, HALF_D), tile), hbm, hbm, hbm],
        out_specs=pl.BlockSpec((ROW_TILE, HALF_D), tile),
        scratch_shapes=[pltpu.VMEM((2,) + up_shape, F32), pltpu.VMEM((2,) + up_shape, F32),
                        pltpu.VMEM((2,) + down_shape, F32),
                        pltpu.VMEM(up_shape, BF16), pltpu.VMEM(up_shape, BF16), pltpu.VMEM(down_shape, BF16),
                        pltpu.SemaphoreType.DMA((2, 3))],
    )
    return pl.pallas_call(
        _experts_kernel,
        grid_spec=grid_spec,
        out_shape=jax.ShapeDtypeStruct((n_tiles * ROW_TILE, HALF_D), I32),
        compiler_params=pltpu.CompilerParams(dimension_semantics=("arbitrary",),
                                             vmem_limit_bytes=VMEM_LIMIT),
        name="experts",
    )(meta, xs, wg, wu, wd)


def _ple_kernel(x1_ref, g1_ref, g2_ref, route_ref, p_ref, wp_f32, pg_ref, gg_ref, wgate_f32, o_ref,
                wp_ref, wgate_ref):
    @pl.when(pl.program_id(0) == 0)
    def _():
        wp_ref[...] = wp_f32[...].astype(BF16)
        wgate_ref[...] = wgate_f32[...].astype(BF16)

    route = route_ref[...]
    lane = lax.broadcasted_iota(jnp.int32, route.shape, 1)
    pick = lambda ln: jnp.sum(jnp.where(lane == ln, route, 0.0), axis=-1, keepdims=True)
    c1 = pick(LANE_C1)
    c2 = pick(LANE_C2)
    a_hi, a_lo = _unpack_halves(g1_ref[...])
    b_hi, b_lo = _unpack_halves(g2_ref[...])
    x = x1_ref[...] + jnp.concatenate([c1 * a_hi + c2 * b_hi, c1 * a_lo + c2 * b_lo], axis=1)
    ple = _rms(_dot(p_ref[...].astype(BF16), wp_ref[...]), pg_ref[...])
    gate = _sigmoid(_dot(_rms(x, gg_ref[...]).astype(BF16), wgate_ref[...]))
    o_ref[...] = x + gate * ple


def _ple(x1, g1, g2, route, p2d, w_ple, ple_g, gate_g, w_gate):
    t = x1.shape[0]
    row = lambda w: pl.BlockSpec((PLE_TM, w), lambda i: (i, 0))
    full = lambda a: pl.BlockSpec(a.shape, lambda i: (0,) * a.ndim)
    return pl.pallas_call(
        _ple_kernel,
        grid=(t // PLE_TM,),
        in_specs=[row(D_MODEL), row(HALF_D), row(HALF_D), row(ROUTE_LANES), row(PLE_DIM),
                  full(w_ple), full(ple_g), full(gate_g), full(w_gate)],
        out_specs=row(D_MODEL),
        out_shape=jax.ShapeDtypeStruct((t, D_MODEL), F32),
        scratch_shapes=[pltpu.VMEM(w_ple.shape, BF16), pltpu.VMEM(w_gate.shape, BF16)],
        compiler_params=pltpu.CompilerParams(dimension_semantics=("arbitrary",),
                                             vmem_limit_bytes=VMEM_LIMIT),
        name="ple",
    )(x1, g1, g2, route, p2d, w_ple, ple_g, gate_g, w_gate)


def kernel(x, p, mix_norm_g, w_in, ret_gn_g, ret_gn_b, w_ret_o, q_norm_g, k_norm_g, attn_sinks,
           w_swa_o, w_out, ffn_norm_g, w_router_group, b_router_group, w_router_expert,
           b_router_expert, w_exp_gate, w_exp_up, w_exp_down, ple_gate_norm_g, w_ple_gate,
           w_ple, ple_norm_g):
    batch, seq, d = x.shape
    t = batch * seq
    depth = w_in.shape[0]
    n_tiles = 2 * t // ROW_TILE + N_EXPERTS
    row = lambda a: a.reshape(1, -1)
    x2d = x.reshape(t, d)
    for i in range(depth):
        pad = ROUTE_LANES - N_GROUPS - N_EXPERTS
        w_router = jnp.pad(jnp.concatenate([w_router_expert[i], w_router_group[i]], axis=1), ((0, 0), (0, pad)))
        b_router = jnp.pad(jnp.concatenate([b_router_expert[i], b_router_group[i]]), (0, pad)).reshape(1, -1)
        w_router = w_router.astype(BF16)
        x1, hp, route, cnt = _block(x2d, row(mix_norm_g[i]), w_in[i], row(ret_gn_g[i]),
                                    row(ret_gn_b[i]), attn_sinks[i], q_norm_g[i], k_norm_g[i],
                                    w_ret_o[i], w_swa_o[i], w_out[i],
                                    row(ffn_norm_g[i]), w_router, b_router, seq)
        meta, pos1, pos2 = _tile_plan(cnt, route, n_tiles)
        xs = _dispatch(hp, pos1, pos2, n_tiles * ROW_TILE)
        ys = _experts(xs, meta, n_tiles, w_exp_gate[i], w_exp_up[i], w_exp_down[i])
        g1, g2 = _combine_gather(ys, pos1, pos2)
        x2d = _ple(x1, g1, g2, route, p[i].reshape(t, PLE_DIM), w_ple[i], row(ple_norm_g[i]),
                   row(ple_gate_norm_g[i]), w_ple_gate[i])
    return x2d.reshape(batch, seq, d)
```

```python
import functools

import jax
import jax.numpy as jnp
import numpy as np
from jax import lax
from jax.experimental import pallas as pl
from jax.experimental.pallas import tpu as pltpu
from jax.experimental.pallas import tpu_sc as plsc

F32 = jnp.float32
BF16 = jnp.bfloat16
I32 = jnp.int32

EPS = 1e-6
D_MODEL = 1024
PLE_DIM = 256
RET_HEADS = 4
RET_DK = 128
RET_DV = 128
RET_CHUNK = 128
SWA_HEADS = 8
SWA_KV_HEADS = 2
SWA_GROUP = SWA_HEADS // SWA_KV_HEADS
SWA_HD = 64
WINDOW = 128
N_GROUPS = 4
EXPERTS_PER_GROUP = 8
N_EXPERTS = N_GROUPS * EXPERTS_PER_GROUP
EXPERT_FF = 256

RET_W = RET_HEADS * RET_DK
SWA_Q = SWA_HEADS * SWA_HD
SWA_KV = SWA_KV_HEADS * SWA_HD
Z_RET_W = 4 * RET_W
Z_SKV_W = 2 * SWA_KV
Z_GATE_W = 2 * D_MODEL
COL_SQ = Z_RET_W
COL_SKV = COL_SQ + SWA_Q
COL_GATE = COL_SKV + Z_SKV_W
IN_WIDTH = COL_GATE + Z_GATE_W

ROUTE_LANES = 128
LANE_GROUP0 = N_EXPERTS
LANE_E1, LANE_E2, LANE_C1, LANE_C2 = 32, 33, 34, 35
LANE_SECOND0 = 64
NEG_INF = -1e30

HALF_D = D_MODEL // 2
ROW_TILE = 1024
SC_CORES = 2
SC_SUBCORES = 16
SC_WORKERS = SC_CORES * SC_SUBCORES
SC_CHUNK = 128

VMEM_LIMIT = 56 * 1024 * 1024

BLOCK_ROWS = 512
W_SLAB = 256
RET_ROWS = 512
MERGE_SUB = 256
PLE_TM = 1024


def _rms(x, g):
    ms = jnp.mean(x * x, axis=-1, keepdims=True)
    return x * lax.rsqrt(ms + EPS) * g


def _sigmoid(x):
    return 1.0 / (1.0 + jnp.exp(-x))


def _dot(a, b):
    return jnp.dot(a, b, preferred_element_type=F32)


def _dot_nt(a, b):
    return lax.dot_general(a, b, (((1,), (1,)), ((), ())), preferred_element_type=F32)


def _dot_tn(a, b):
    return lax.dot_general(a, b, (((0,), (0,)), ((), ())), preferred_element_type=F32)


def _pack_halves(v):
    return pltpu.pack_elementwise([v[:, :HALF_D], v[:, HALF_D:]], packed_dtype=BF16)


def _unpack_halves(p):
    words = lax.bitcast_convert_type(p, jnp.uint32)
    return tuple(pltpu.unpack_elementwise(words, index=k, packed_dtype=BF16, unpacked_dtype=F32) for k in range(2))


def _retention_tables():
    f32 = np.float32
    h = RET_HEADS
    c = RET_CHUNK
    log_gamma = np.log1p(-np.exp2(f32(-5.0) - np.arange(h, dtype=f32))).astype(f32)
    pos = np.arange(c, dtype=f32)
    diff = pos[:, None] - pos[None, :]
    decay = np.where(diff[None] >= 0.0, np.exp(np.maximum(diff, f32(0.0))[None] * log_gamma[:, None, None]), f32(0.0))
    scale = f32(RET_DK ** -0.5)
    dmask = decay * scale
    zeta = np.exp((f32(c - 1.0) - pos)[None, :] * log_gamma[:, None]) * scale
    xi = np.exp((pos + f32(1.0))[None, :] * log_gamma[:, None])
    cdec = np.exp(f32(c) * log_gamma)
    bc = lambda a: np.ascontiguousarray(np.broadcast_to(a[:, :, None], (h, c, c)), dtype=f32)
    return (dmask.astype(f32), bc(zeta), bc(xi),
            np.ascontiguousarray(np.broadcast_to(cdec[:, None, None], (h, c, c)), dtype=f32))


def _swa_tables(q_g, k_g):
    f32 = np.float32
    qi = np.arange(WINDOW)[:, None]
    sj = np.arange(2 * WINDOW)[None, :]
    rel = qi + WINDOW - sj
    ok = (rel >= 0) & (rel < WINDOW)
    slopes = np.exp2(f32(-8.0) * np.arange(1, SWA_HEADS + 1, dtype=f32) / f32(SWA_HEADS)).astype(f32)
    bias = np.where(ok[None], -slopes[:, None, None] * rel.astype(f32)[None], f32(NEG_INF)).astype(f32)
    bias = bias.reshape(SWA_KV_HEADS, SWA_GROUP * WINDOW, 2 * WINDOW)
    avg = lambda heads: jnp.asarray(np.kron(np.eye(heads, dtype=f32), np.full((SWA_HD, SWA_HD), 1.0 / SWA_HD, f32)),
                                    dtype=BF16)
    qg = jnp.tile(q_g.reshape(1, SWA_HD), (1, SWA_HEADS)) * (SWA_HD ** -0.5)
    kg = jnp.tile(k_g.reshape(1, SWA_HD), (1, SWA_KV_HEADS))
    return qg, kg, avg(SWA_HEADS), avg(SWA_KV_HEADS), jnp.asarray(bias)


def _block_kernel(per_seq, sinks_ref, x_ref, xp_ref, g_ref, w_hbm, dmask_ref, zeta_ref, xi_ref, cdec_ref,
                  gng_ref, gnb_ref, qg_ref, kg_ref, bq_ref, bk_ref, bias_ref,
                  wro_hbm, wso_hbm, wout_hbm, fg_ref, wr_ref, br_ref,
                  x1_ref, hp_ref, route_ref, cnt_ref,
                  w_ref, wro_ref, wso_ref, wout_ref, wtmp_ref, w_sem,
                  zc_ref, zn_ref, yr_ref, ys_ref, state_ref, prev_ref, band_ref, tail_ref):
    g = pl.program_id(0)
    seq_start = lax.rem(g - 1, per_seq) == 0

    @pl.when(g == 0)
    def _():
        slabs = [(src, dst, c0) for src, dst in ((w_hbm, w_ref), (wro_hbm, wro_ref), (wso_hbm, wso_ref),
                                                 (wout_hbm, wout_ref))
                 for c0 in range(0, dst.shape[1], W_SLAB)]

        def slab(k):
            src, dst, c0 = slabs[k]
            return pltpu.make_async_copy(src.at[:, pl.ds(c0, W_SLAB)], wtmp_ref.at[k % 2, pl.ds(0, dst.shape[0])],
                                         w_sem.at[k % 2])

        slab(0).start()
        for k, (_, dst, c0) in enumerate(slabs):
            if k + 1 < len(slabs):
                slab(k + 1).start()
            slab(k).wait()
            dst[:, c0:c0 + W_SLAB] = wtmp_ref[k % 2, :dst.shape[0]].astype(BF16)
        zc_ref[...] = jnp.zeros_like(zc_ref)
        tail_ref[...] = jnp.zeros_like(tail_ref)
        state_ref[...] = jnp.zeros_like(state_ref)

    @pl.when(seq_start)
    def _():
        state_ref[...] = jnp.zeros_like(state_ref)

    band_ref[:WINDOW, :] = tail_ref[...]
    band_ref[WINDOW:, :] = zc_ref[:, COL_SKV:COL_GATE]

    h = _rms(x_ref[...], g_ref[...]).astype(BF16)

    def project(c0, width):
        zn_ref[:, c0:c0 + width] = _dot(h, w_ref[:, c0:c0 + width]).astype(BF16)

    cw = 512
    proj_pieces = [functools.partial(project, j * cw, cw) for j in range(COL_SKV // cw)]
    proj_pieces.append(functools.partial(project, COL_SKV, Z_SKV_W))
    proj_pieces += [functools.partial(project, COL_GATE + j * cw, cw) for j in range(Z_GATE_W // cw)]

    def retention_rows(t):
        r_base = t * RET_ROWS
        n_chunks = RET_ROWS // RET_CHUNK
        part = lambda c, p, hd: zc_ref[r_base + c * RET_CHUNK:r_base + (c + 1) * RET_CHUNK,
                                       p * RET_W + hd * RET_DK:p * RET_W + (hd + 1) * RET_DK]
        for hd in range(RET_HEADS):
            state = state_ref[hd]
            for c in range(n_chunks):
                prev_ref[t * n_chunks + c, hd] = state.astype(BF16)
                kz = (part(c, 1, hd).astype(F32) * zeta_ref[hd]).astype(BF16)
                state = cdec_ref[hd] * state + _dot_tn(kz, part(c, 2, hd))
            state_ref[hd] = state
        out_rows = []
        for c in range(n_chunks):
            heads = []
            for hd in range(RET_HEADS):
                q = part(c, 0, hd)
                scores = _dot_nt(q, part(c, 1, hd)) * dmask_ref[hd]
                y = _dot(scores.astype(BF16), part(c, 2, hd)) + _dot(q, prev_ref[t * n_chunks + c, hd]) * xi_ref[hd]
                mu = jnp.mean(y, axis=-1, keepdims=True)
                d = y - mu
                var = jnp.mean(d * d, axis=-1, keepdims=True)
                hs = slice(hd * RET_DV, (hd + 1) * RET_DV)
                yn = d * lax.rsqrt(var + EPS) * gng_ref[:, hs] + gnb_ref[:, hs]
                gate = part(c, 3, hd).astype(F32)
                heads.append((gate * _sigmoid(gate) * yn).astype(BF16))
            out_rows.append(jnp.concatenate(heads, axis=1))
        yr_ref[r_base:r_base + RET_ROWS, :] = jnp.concatenate(out_rows, axis=0)

    group_rows = SWA_GROUP * WINDOW
    lo_q = lax.broadcasted_iota(jnp.int32, (WINDOW, 2 * SWA_HD), 1) < SWA_HD
    lo_k = lax.broadcasted_iota(jnp.int32, (2 * WINDOW, 2 * SWA_HD), 1) < SWA_HD
    in_prev = lax.broadcasted_iota(jnp.int32, (group_rows, 2 * WINDOW), 1) < WINDOW
    row_head = lax.broadcasted_iota(jnp.int32, (group_rows, 1), 0) // WINDOW

    def both_heads(v):
        swapped = pltpu.roll(v, SWA_HD, axis=1)
        return jnp.where(lo_k, v, swapped), jnp.where(lo_k, swapped, v)

    def swa_block(n):
        rows = slice(n * WINDOW, (n + 1) * WINDOW)
        kv = band_ref[n * WINDOW:(n + 2) * WINDOW, :]
        kf = kv[:, :SWA_KV].astype(F32)
        kn = kf * lax.rsqrt(_dot((kf * kf).astype(BF16), bk_ref[...]) + EPS) * kg_ref[...]
        keys = [a.astype(BF16) for a in both_heads(kn)]
        vals = [a.astype(BF16) for a in both_heads(kv[:, SWA_KV:].astype(F32))]
        qf = zc_ref[rows, COL_SQ:COL_SKV].astype(F32)
        qn = qf * lax.rsqrt(_dot((qf * qf).astype(BF16), bq_ref[...]) + EPS) * qg_ref[...]
        outs = []
        for kh in range(SWA_KV_HEADS):
            parts = []
            for j in range(SWA_GROUP // 2):
                c0 = (kh * SWA_GROUP + 2 * j) * SWA_HD
                two = qn[:, c0:c0 + 2 * SWA_HD]
                parts += [jnp.where(lo_q, two, 0.0), jnp.where(lo_q, 0.0, two)]
            qs = jnp.concatenate(parts, axis=0).astype(BF16)
            s = _dot_nt(qs, keys[kh]) + bias_ref[kh]
            if n == 0:
                s = jnp.where(jnp.logical_and(seq_start, in_prev), NEG_INF, s)
            sink = sinks_ref[kh * SWA_GROUP]
            for gi in range(1, SWA_GROUP):
                sink = jnp.where(row_head == gi, sinks_ref[kh * SWA_GROUP + gi], sink)
            m = jnp.maximum(jnp.max(s, axis=-1, keepdims=True), sink)
            p = jnp.exp(s - m)
            denom = jnp.sum(p, axis=-1, keepdims=True) + jnp.exp(sink - m)
            o = _dot(p.astype(BF16), vals[kh]) * (1.0 / denom)
            for j in range(SWA_GROUP // 2):
                even = o[(2 * j) * WINDOW:(2 * j + 1) * WINDOW]
                odd = o[(2 * j + 1) * WINDOW:(2 * j + 2) * WINDOW]
                outs.append(jnp.where(lo_q, even, odd).astype(BF16))
        ys_ref[rows, :] = jnp.concatenate(outs, axis=1)

    mix_pieces = [functools.partial(retention_rows, t) for t in range(BLOCK_ROWS // RET_ROWS)]
    mix_pieces += [functools.partial(swa_block, n) for n in range(BLOCK_ROWS // WINDOW)]

    subs = []

    def merge_sub(s):
        subs.append(_merge_rows(slice(s * MERGE_SUB, (s + 1) * MERGE_SUB), yr_ref, ys_ref, zc_ref, xp_ref, wro_ref,
                                wso_ref, wout_ref, fg_ref, wr_ref, br_ref))

    other_pieces = mix_pieces + [functools.partial(merge_sub, s) for s in range(BLOCK_ROWS // MERGE_SUB)]

    n_other = len(other_pieces)
    for k, piece in enumerate(other_pieces):
        for proj_piece in proj_pieces[k * len(proj_pieces) // n_other:(k + 1) * len(proj_pieces) // n_other]:
            proj_piece()
        piece()
    x1_ref[...] = jnp.concatenate([s[0] for s in subs], axis=0)
    hp_ref[...] = jnp.concatenate([s[1] for s in subs], axis=0)
    route_ref[...] = jnp.concatenate([s[2] for s in subs], axis=0)
    for s, sub in enumerate(subs):
        cnt_ref[s] = sub[3]

    tail_ref[...] = zc_ref[BLOCK_ROWS - WINDOW:, COL_SKV:COL_GATE]
    zc_ref[...] = zn_ref[...]


def _merge_rows(rows, yr_ref, ys_ref, z_ref, x_ref, wro_ref, wso_ref, wout_ref, fg_ref, wr_ref, br_ref):
    a = _dot(yr_ref[rows, :], wro_ref[...])
    b = _dot(ys_ref[rows, :], wso_ref[...])
    gate_r = z_ref[rows, COL_GATE:COL_GATE + D_MODEL].astype(F32)
    gate_s = z_ref[rows, COL_GATE + D_MODEL:].astype(F32)
    merged = (_sigmoid(gate_r) * a + _sigmoid(gate_s) * b).astype(BF16)
    x1 = x_ref[rows, :] + _dot(merged, wout_ref[...])
    h2 = _rms(x1, fg_ref[...])
    packed = _pack_halves(h2)

    logits = _dot(h2.astype(BF16), wr_ref[...]) + br_ref[...]
    lane = lax.broadcasted_iota(jnp.int32, logits.shape, 1)
    big = jnp.int32(ROUTE_LANES)
    is_group = jnp.logical_and(lane >= LANE_GROUP0, lane < LANE_GROUP0 + N_GROUPS)
    gl = jnp.where(is_group, logits, NEG_INF)
    gmax = jnp.max(gl, axis=-1, keepdims=True)
    g_w = 1.0 / jnp.sum(jnp.exp(gl - gmax), axis=-1, keepdims=True)
    g_sel = jnp.min(jnp.where(gl == gmax, lane, big), axis=-1, keepdims=True) - LANE_GROUP0
    in_group = jnp.logical_and(lane < N_EXPERTS, (lane >> 3) == g_sel)
    el = jnp.where(in_group, logits, NEG_INF)
    m1 = jnp.max(el, axis=-1, keepdims=True)
    i1 = jnp.min(jnp.where(el == m1, lane, big), axis=-1, keepdims=True)
    el2 = jnp.where(lane == i1, NEG_INF, el)
    m2 = jnp.max(el2, axis=-1, keepdims=True)
    i2 = jnp.min(jnp.where(el2 == m2, lane, big), axis=-1, keepdims=True)
    e2 = jnp.exp(m2 - m1)
    c1 = g_w / (1.0 + e2)
    c2 = g_w * e2 / (1.0 + e2)
    chosen = jnp.where(jnp.logical_or(lane == i1, lane == i2), 1.0, 0.0)
    route = (chosen
             + jnp.where(lane == LANE_E1, i1.astype(F32), 0.0)
             + jnp.where(lane == LANE_E2, i2.astype(F32), 0.0)
             + jnp.where(lane == LANE_C1, c1, 0.0)
             + jnp.where(lane == LANE_C2, c2, 0.0)
             + jnp.where(lane == i2 + LANE_SECOND0, 1.0, 0.0))
    return x1, packed, route, jnp.sum(chosen, axis=0, keepdims=True)


def _block(x2d, mix_g, w_in, gn_g, gn_b, sinks, q_g, k_g, w_ret_o, w_swa_o, w_out, ffn_g, w_router, b_router,
           seq):
    t = x2d.shape[0]
    n = t // BLOCK_ROWS
    assert seq % BLOCK_ROWS == 0
    tables = _retention_tables()
    swa_tables = _swa_tables(q_g, k_g)
    consts = tables + (gn_g, gn_b) + swa_tables
    merge_weights = (w_ret_o, w_swa_o, w_out)
    small = (ffn_g, w_router, b_router)
    hbm = pl.BlockSpec(memory_space=pl.ANY)
    full = lambda a: pl.BlockSpec(a.shape, lambda g: (0,) * a.ndim)
    cur = lambda w: pl.BlockSpec((BLOCK_ROWS, w), lambda g: (jnp.minimum(g, n - 1), 0))
    prv = lambda w: pl.BlockSpec((BLOCK_ROWS, w), lambda g: (jnp.maximum(g - 1, 0), 0))
    chunks = BLOCK_ROWS // RET_CHUNK
    subs_per_block = BLOCK_ROWS // MERGE_SUB
    return pl.pallas_call(
        functools.partial(_block_kernel, seq // BLOCK_ROWS),
        grid=(n + 1,),
        in_specs=([pl.BlockSpec(memory_space=pltpu.SMEM), cur(D_MODEL), prv(D_MODEL), full(mix_g), hbm]
                  + [full(a) for a in consts] + [hbm] * len(merge_weights) + [full(a) for a in small]),
        out_specs=[prv(D_MODEL), prv(HALF_D), prv(ROUTE_LANES),
                   pl.BlockSpec((subs_per_block, 1, ROUTE_LANES), lambda g: (jnp.maximum(g - 1, 0), 0, 0))],
        out_shape=[jax.ShapeDtypeStruct((t, D_MODEL), F32),
                   jax.ShapeDtypeStruct((t, HALF_D), I32),
                   jax.ShapeDtypeStruct((t, ROUTE_LANES), F32),
                   jax.ShapeDtypeStruct((n * subs_per_block, 1, ROUTE_LANES), F32)],
        scratch_shapes=[pltpu.VMEM(w_in.shape, BF16)] + [pltpu.VMEM(w.shape, BF16) for w in merge_weights]
                       + [pltpu.VMEM((2, D_MODEL, W_SLAB), F32), pltpu.SemaphoreType.DMA((2,)),
                        pltpu.VMEM((BLOCK_ROWS, IN_WIDTH), BF16), pltpu.VMEM((BLOCK_ROWS, IN_WIDTH), BF16),
                        pltpu.VMEM((BLOCK_ROWS, RET_W), BF16), pltpu.VMEM((BLOCK_ROWS, SWA_Q), BF16),
                        pltpu.VMEM((RET_HEADS, RET_DK, RET_DV), F32),
                        pltpu.VMEM((chunks, RET_HEADS, RET_DK, RET_DV), BF16),
                        pltpu.VMEM((BLOCK_ROWS + WINDOW, Z_SKV_W), BF16), pltpu.VMEM((WINDOW, Z_SKV_W), BF16)],
        compiler_params=pltpu.CompilerParams(dimension_semantics=("arbitrary",),
                                             vmem_limit_bytes=VMEM_LIMIT),
        name="block",
    )(sinks, x2d, x2d, mix_g, w_in, *consts, *merge_weights, *small)


def _plan_kernel(cnt_ref, route_ref, base_ref, meta_ref, pos_ref):
    counts = cnt_ref[...]
    nt = counts.shape[0]
    exact = functools.partial(jnp.dot, precision=lax.Precision.HIGHEST, preferred_element_type=F32)
    total = jnp.sum(counts, axis=0, keepdims=True)
    tiles_e = jnp.ceil(total * (1.0 / ROW_TILE))
    k = lax.broadcasted_iota(jnp.int32, (ROUTE_LANES, ROUTE_LANES), 0)
    lane = lax.broadcasted_iota(jnp.int32, (ROUTE_LANES, ROUTE_LANES), 1)
    tile_end = exact(jnp.broadcast_to(tiles_e, (8, ROUTE_LANES)), jnp.where(k <= lane, 1.0, 0.0))[0:1]
    tile_start = tile_end - tiles_e
    r = lax.broadcasted_iota(jnp.int32, (nt, nt), 0)
    c = lax.broadcasted_iota(jnp.int32, (nt, nt), 1)
    before = exact(jnp.where(c < r, 1.0, 0.0), counts)
    base_ref[...] = tile_start * ROW_TILE + before

    used = jnp.max(tile_end, axis=-1, keepdims=True)
    tile = k.astype(F32)
    clamped = jnp.minimum(tile, jnp.maximum(used - 1.0, 0.0))
    is_expert = lane < N_EXPERTS
    owner = jnp.sum(jnp.where(jnp.logical_and(is_expert, tile_end <= clamped), 1.0, 0.0), axis=-1, keepdims=True)
    owner = jnp.minimum(owner, N_EXPERTS - 1.0)
    mine = lane.astype(F32) == owner
    pick = lambda v: jnp.sum(jnp.where(mine, v, 0.0), axis=-1, keepdims=True)
    t1 = tile[:, 0:1]
    valid = jnp.clip(pick(total) - (t1 - pick(tile_start)) * ROW_TILE, 0.0, float(ROW_TILE))
    valid = jnp.where(t1 < used, valid, 0.0)

    nonempty = jnp.where(jnp.logical_and(lane[0:1] < N_EXPERTS, tiles_e > 0.0), 1.0, 0.0)
    rank = exact(jnp.broadcast_to(nonempty, (8, ROUTE_LANES)), jnp.where(k < lane, 1.0, 0.0))[0:1]
    nonempty_col = jnp.sum(jnp.where(k == lane, jnp.broadcast_to(nonempty, k.shape), 0.0), axis=-1, keepdims=True)
    later = jnp.where(jnp.logical_and(k > lane, nonempty_col > 0.0), tile, float(ROUTE_LANES))
    nxt_e = jnp.min(later, axis=0, keepdims=True)
    nxt_e = jnp.where(nxt_e < float(ROUTE_LANES), nxt_e, -1.0)
    first = jnp.where(jnp.logical_and(t1 == pick(tile_start), t1 < used), 1.0, 0.0)
    my_rank = pick(rank)
    slot = my_rank - 2.0 * jnp.floor(my_rank * 0.5)
    fields = (owner, valid, first, pick(nxt_e), slot, jnp.broadcast_to(used, owner.shape))
    by_tile = sum(jnp.where(lane == n, f, 0.0) for n, f in enumerate(fields))
    meta_ref[...] = by_tile.T[:8, :].astype(I32)

    sel_row = lax.broadcasted_iota(jnp.int32, (8, ROUTE_LANES), 0)
    sel_lane = lax.broadcasted_iota(jnp.int32, (8, ROUTE_LANES), 1)
    tok_r = lax.broadcasted_iota(jnp.int32, (MERGE_SUB, MERGE_SUB), 0)
    tok_c = lax.broadcasted_iota(jnp.int32, (MERGE_SUB, MERGE_SUB), 1)
    eye = jnp.where(tok_r == tok_c, 1.0, 0.0).astype(BF16)
    upto = jnp.where(tok_r <= tok_c, 1.0, 0.0).astype(BF16)
    out_row = lax.broadcasted_iota(jnp.int32, (8, MERGE_SUB), 0)
    sum_experts = lambda a: jnp.sum(a, axis=0, keepdims=True)

    def sub_positions(s, carry):
        route = route_ref[pl.ds(pl.multiple_of(s * MERGE_SUB, MERGE_SUB), MERGE_SUB), :].astype(BF16)
        record_t = _dot_tn(route, eye)
        incl_t = _dot_tn(route, upto)[:N_EXPERTS]
        both_t = record_t[:N_EXPERTS]
        second_t = record_t[LANE_SECOND0:LANE_SECOND0 + N_EXPERTS]
        rank2 = sum_experts(second_t * incl_t) - 1.0
        rank1 = sum_experts(both_t * incl_t) - rank2 - 2.0
        base_row = jnp.where(sel_lane < N_EXPERTS, jnp.broadcast_to(base_ref[pl.ds(s, 1), :], (8, ROUTE_LANES)), 0.0)
        hi = jnp.floor(base_row * (1.0 / 256.0))
        lo = base_row - 256.0 * hi
        hi2 = pltpu.roll(hi, LANE_SECOND0, axis=1)
        lo2 = pltpu.roll(lo, LANE_SECOND0, axis=1)
        table = (jnp.where(sel_row == 0, hi, 0.0) + jnp.where(sel_row == 1, lo, 0.0)
                 + jnp.where(sel_row == 2, hi2, 0.0) + jnp.where(sel_row == 3, lo2, 0.0))
        r = _dot_nt(table.astype(BF16), route)
        base_both = 256.0 * r[0:1] + r[1:2]
        base2 = 256.0 * r[2:3] + r[3:4]
        out = jnp.where(out_row == 0, base_both - base2 + rank1, jnp.where(out_row == 1, base2 + rank2, 0.0))
        pos_ref[s] = out.astype(I32)
        return carry

    lax.fori_loop(0, nt, sub_positions, 0, unroll=4)


def _tile_plan(cnt, route, n_tiles):
    assert n_tiles <= ROUTE_LANES
    nt = cnt.shape[0]
    _, meta, pos = pl.pallas_call(
        _plan_kernel,
        out_shape=[jax.ShapeDtypeStruct((nt, ROUTE_LANES), F32), jax.ShapeDtypeStruct((8, ROUTE_LANES), I32),
                   jax.ShapeDtypeStruct((nt, 8, MERGE_SUB), I32)],
        compiler_params=pltpu.CompilerParams(vmem_limit_bytes=VMEM_LIMIT),
        name="tile_plan",
    )(cnt.reshape(nt, ROUTE_LANES), route)
    return meta, pos[:, 0, :], pos[:, 1, :]


def _sc_mesh():
    return plsc.VectorSubcoreMesh(core_axis_name="c", subcore_axis_name="s")


def _sc_worker():
    return lax.axis_index("s") * SC_CORES + lax.axis_index("c")


def _dispatch(hp, pos1, pos2, n_rows):
    t = hp.shape[0]
    per_worker = t // SC_WORKERS
    k = per_worker // SC_CHUNK
    idx = pltpu.VMEM((k, SC_CHUNK), I32)

    @functools.partial(pl.kernel, mesh=_sc_mesh(), out_type=jax.ShapeDtypeStruct((n_rows, HALF_D), I32),
                       scratch_types=[idx, idx, pltpu.VMEM((SC_CHUNK, HALF_D), I32)], name="moe_dispatch")
    def run(h_hbm, p1_hbm, p2_hbm, xs_hbm, p1_v, p2_v, rows_v):
        wid = _sc_worker()
        pltpu.sync_copy(p1_hbm.at[wid], p1_v)
        pltpu.sync_copy(p2_hbm.at[wid], p2_v)
        for j in range(k):
            pltpu.sync_copy(h_hbm.at[pl.ds(wid * per_worker + j * SC_CHUNK, SC_CHUNK)], rows_v)
            pltpu.sync_copy(rows_v, xs_hbm.at[p1_v.at[j]])
            pltpu.sync_copy(rows_v, xs_hbm.at[p2_v.at[j]])

    return run(hp, pos1.reshape(SC_WORKERS, k, SC_CHUNK), pos2.reshape(SC_WORKERS, k, SC_CHUNK))


def _combine_gather(ys, pos1, pos2):
    t = pos1.size
    per_worker = t // SC_WORKERS
    k = per_worker // SC_CHUNK
    idx = pltpu.VMEM((k, SC_CHUNK), I32)
    out = jax.ShapeDtypeStruct((t, HALF_D), I32)

    @functools.partial(pl.kernel, mesh=_sc_mesh(), out_type=(out, out),
                       scratch_types=[idx, idx, pltpu.VMEM((SC_CHUNK, HALF_D), I32)], name="moe_combine")
    def run(ys_hbm, p1_hbm, p2_hbm, g1_hbm, g2_hbm, p1_v, p2_v, rows_v):
        wid = _sc_worker()
        pltpu.sync_copy(p1_hbm.at[wid], p1_v)
        pltpu.sync_copy(p2_hbm.at[wid], p2_v)
        for j in range(k):
            dst = pl.ds(wid * per_worker + j * SC_CHUNK, SC_CHUNK)
            pltpu.sync_copy(ys_hbm.at[p1_v.at[j]], rows_v)
            pltpu.sync_copy(rows_v, g1_hbm.at[dst])
            pltpu.sync_copy(ys_hbm.at[p2_v.at[j]], rows_v)
            pltpu.sync_copy(rows_v, g2_hbm.at[dst])

    return run(ys, pos1.reshape(SC_WORKERS, k, SC_CHUNK), pos2.reshape(SC_WORKERS, k, SC_CHUNK))


META_OWNER, META_VALID, META_FIRST, META_NEXT, META_SLOT, META_USED = range(6)


def _experts_kernel(meta_ref, xs_ref, wg_hbm, wu_hbm, wd_hbm, ys_ref,
                    wg_f, wu_f, wd_f, wg_bf, wu_bf, wd_bf, w_sem):
    i = pl.program_id(0)
    valid = meta_ref[META_VALID, i]
    owner = meta_ref[META_OWNER, i]
    nxt = meta_ref[META_NEXT, i]

    def fetch(expert, slot):
        pairs = ((wg_hbm, wg_f), (wu_hbm, wu_f), (wd_hbm, wd_f))
        return [pltpu.make_async_copy(w.at[expert], buf.at[slot], w_sem.at[slot, n])
                for n, (w, buf) in enumerate(pairs)]

    @pl.when(i == 0)
    def _():
        for copy in fetch(owner, 0):
            copy.start()

    @pl.when(meta_ref[META_FIRST, i] == 1)
    def _():
        slot = meta_ref[META_SLOT, i]
        for copy in fetch(owner, slot):
            copy.wait()
        wg_bf[...] = wg_f[slot].astype(BF16)
        wu_bf[...] = wu_f[slot].astype(BF16)
        wd_bf[...] = wd_f[slot].astype(BF16)

        @pl.when(nxt >= 0)
        def _():
            for copy in fetch(nxt, 1 - slot):
                copy.start()

    @pl.when(i < meta_ref[META_USED, 0])
    def _():
        row = lax.broadcasted_iota(jnp.int32, (ROW_TILE, HALF_D), 0)
        packed = jnp.where(row < valid, xs_ref[...], 0)
        hi, lo = _unpack_halves(packed)
        hi = hi.astype(BF16)
        lo = lo.astype(BF16)
        hg = _dot(hi, wg_bf[:HALF_D, :]) + _dot(lo, wg_bf[HALF_D:, :])
        hu = _dot(hi, wu_bf[:HALF_D, :]) + _dot(lo, wu_bf[HALF_D:, :])
        hid = (hg * _sigmoid(hg) * hu).astype(BF16)
        ys_ref[...] = _pack_halves(_dot(hid, wd_bf[...]))


def _experts(xs, meta, n_tiles, wg, wu, wd):
    tile = lambda i, meta: (jnp.minimum(i, meta[META_USED, 0] - 1), 0)
    hbm = pl.BlockSpec(memory_space=pl.ANY)
    up_shape, down_shape = (D_MODEL, EXPERT_FF), (EXPERT_FF, D_MODEL)
    grid_spec = pltpu.PrefetchScalarGridSpec(
        num_scalar_prefetch=1,
        grid=(n_tiles,),
        in_specs=[pl.BlockSpec((ROW_TILE, HALF_D), tile), hbm, hbm, hbm],
        out_specs=pl.BlockSpec((ROW_TILE, HALF_D), tile),
        scratch_shapes=[pltpu.VMEM((2,) + up_shape, F32), pltpu.VMEM((2,) + up_shape, F32),
                        pltpu.VMEM((2,) + down_shape, F32),
                        pltpu.VMEM(up_shape, BF16), pltpu.VMEM(up_shape, BF16), pltpu.VMEM(down_shape, BF16),
                        pltpu.SemaphoreType.DMA((2, 3))],
    )
    return pl.pallas_call(
        _experts_kernel,
        grid_spec=grid_spec,
        out_shape=jax.ShapeDtypeStruct((n_tiles * ROW_TILE, HALF_D), I32),
        compiler_params=pltpu.CompilerParams(dimension_semantics=("arbitrary",),
                                             vmem_limit_bytes=VMEM_LIMIT),
        name="experts",
    )(meta, xs, wg, wu, wd)


def _ple_kernel(x1_ref, g1_ref, g2_ref, route_ref, p_ref, wp_f32, pg_ref, gg_ref, wgate_f32, o_ref,
                wp_ref, wgate_ref):
    @pl.when(pl.program_id(0) == 0)
    def _():
        wp_ref[...] = wp_f32[...].astype(BF16)
        wgate_ref[...] = wgate_f32[...].astype(BF16)

    route = route_ref[...]
    lane = lax.broadcasted_iota(jnp.int32, route.shape, 1)
    pick = lambda ln: jnp.sum(jnp.where(lane == ln, route, 0.0), axis=-1, keepdims=True)
    c1 = pick(LANE_C1)
    c2 = pick(LANE_C2)
    a_hi, a_lo = _unpack_halves(g1_ref[...])
    b_hi, b_lo = _unpack_halves(g2_ref[...])
    x = x1_ref[...] + jnp.concatenate([c1 * a_hi + c2 * b_hi, c1 * a_lo + c2 * b_lo], axis=1)
    ple = _rms(_dot(p_ref[...].astype(BF16), wp_ref[...]), pg_ref[...])
    gate = _sigmoid(_dot(_rms(x, gg_ref[...]).astype(BF16), wgate_ref[...]))
    o_ref[...] = x + gate * ple


def _ple(x1, g1, g2, route, p2d, w_ple, ple_g, gate_g, w_gate):
    t = x1.shape[0]
    row = lambda w: pl.BlockSpec((PLE_TM, w), lambda i: (i, 0))
    full = lambda a: pl.BlockSpec(a.shape, lambda i: (0,) * a.ndim)
    return pl.pallas_call(
        _ple_kernel,
        grid=(t // PLE_TM,),
        in_specs=[row(D_MODEL), row(HALF_D), row(HALF_D), row(ROUTE_LANES), row(PLE_DIM),
                  full(w_ple), full(ple_g), full(gate_g), full(w_gate)],
        out_specs=row(D_MODEL),
        out_shape=jax.ShapeDtypeStruct((t, D_MODEL), F32),
        scratch_shapes=[pltpu.VMEM(w_ple.shape, BF16), pltpu.VMEM(w_gate.shape, BF16)],
        compiler_params=pltpu.CompilerParams(dimension_semantics=("arbitrary",),
                                             vmem_limit_bytes=VMEM_LIMIT),
        name="ple",
    )(x1, g1, g2, route, p2d, w_ple, ple_g, gate_g, w_gate)


def kernel(x, p, mix_norm_g, w_in, ret_gn_g, ret_gn_b, w_ret_o, q_norm_g, k_norm_g, attn_sinks,
           w_swa_o, w_out, ffn_norm_g, w_router_group, b_router_group, w_router_expert,
           b_router_expert, w_exp_gate, w_exp_up, w_exp_down, ple_gate_norm_g, w_ple_gate,
           w_ple, ple_norm_g):
    batch, seq, d = x.shape
    t = batch * seq
    depth = w_in.shape[0]
    n_tiles = 2 * t // ROW_TILE + N_EXPERTS
    row = lambda a: a.reshape(1, -1)
    x2d = x.reshape(t, d)
    for i in range(depth):
        pad = ROUTE_LANES - N_GROUPS - N_EXPERTS
        w_router = jnp.pad(jnp.concatenate([w_router_expert[i], w_router_group[i]], axis=1), ((0, 0), (0, pad)))
        b_router = jnp.pad(jnp.concatenate([b_router_expert[i], b_router_group[i]]), (0, pad)).reshape(1, -1)
        w_router = w_router.astype(BF16)
        x1, hp, route, cnt = _block(x2d, row(mix_norm_g[i]), w_in[i], row(ret_gn_g[i]),
                                    row(ret_gn_b[i]), attn_sinks[i], q_norm_g[i], k_norm_g[i],
                                    w_ret_o[i], w_swa_o[i], w_out[i],
                                    row(ffn_norm_g[i]), w_router, b_router, seq)
        meta, pos1, pos2 = _tile_plan(cnt, route, n_tiles)
        xs = _dispatch(hp, pos1, pos2, n_tiles * ROW_TILE)
        ys = _experts(xs, meta, n_tiles, w_exp_gate[i], w_exp_up[i], w_exp_down[i])
        g1, g2 = _combine_gather(ys, pos1, pos2)
        x2d = _ple(x1, g1, g2, route, p[i].reshape(t, PLE_DIM), w_ple[i], row(ple_norm_g[i]),
                   row(ple_gate_norm_g[i]), w_ple_gate[i])
    return x2d.reshape(batch, seq, d)
```

```python
import functools

import jax
import jax.numpy as jnp
import numpy as np
from jax import lax
from jax.experimental import pallas as pl
from jax.experimental.pallas import tpu as pltpu
from jax.experimental.pallas import tpu_sc as plsc

F32 = jnp.float32
BF16 = jnp.bfloat16
I32 = jnp.int32

EPS = 1e-6
D_MODEL = 1024
PLE_DIM = 256
RET_HEADS = 4
RET_DK = 128
RET_DV = 128
RET_CHUNK = 128
SWA_HEADS = 8
SWA_KV_HEADS = 2
SWA_GROUP = SWA_HEADS // SWA_KV_HEADS
SWA_HD = 64
WINDOW = 128
N_GROUPS = 4
EXPERTS_PER_GROUP = 8
N_EXPERTS = N_GROUPS * EXPERTS_PER_GROUP
EXPERT_FF = 256

RET_W = RET_HEADS * RET_DK
SWA_Q = SWA_HEADS * SWA_HD
SWA_KV = SWA_KV_HEADS * SWA_HD
Z_RET_W = 4 * RET_W
Z_SKV_W = 2 * SWA_KV
Z_GATE_W = 2 * D_MODEL
COL_SQ = Z_RET_W
COL_SKV = COL_SQ + SWA_Q
COL_GATE = COL_SKV + Z_SKV_W
IN_WIDTH = COL_GATE + Z_GATE_W

ROUTE_LANES = 128
LANE_GROUP0 = N_EXPERTS
LANE_E1, LANE_E2, LANE_C1, LANE_C2 = 32, 33, 34, 35
LANE_SECOND0 = 64
NEG_INF = -1e30

HALF_D = D_MODEL // 2
ROW_TILE = 512
XS_SLOTS = 3
SC_CORES = 2
SC_SUBCORES = 16
SC_WORKERS = SC_CORES * SC_SUBCORES
SC_CHUNK = 128

VMEM_LIMIT = 56 * 1024 * 1024

BLOCK_ROWS = 512
W_SLAB = 256
RET_ROWS = 512
MERGE_SUB = 256
PLE_TM = 1024


def _rms(x, g):
    ms = jnp.mean(x * x, axis=-1, keepdims=True)
    return x * lax.rsqrt(ms + EPS) * g


def _sigmoid(x):
    return 1.0 / (1.0 + jnp.exp(-x))


def _dot(a, b):
    return jnp.dot(a, b, preferred_element_type=F32)


def _dot_nt(a, b):
    return lax.dot_general(a, b, (((1,), (1,)), ((), ())), preferred_element_type=F32)


def _dot_tn(a, b):
    return lax.dot_general(a, b, (((0,), (0,)), ((), ())), preferred_element_type=F32)


def _pack_halves(v):
    return pltpu.pack_elementwise([v[:, :HALF_D], v[:, HALF_D:]], packed_dtype=BF16)


def _unpack_halves(p):
    words = lax.bitcast_convert_type(p, jnp.uint32)
    return tuple(pltpu.unpack_elementwise(words, index=k, packed_dtype=BF16, unpacked_dtype=F32) for k in range(2))


def _retention_tables():
    f32 = np.float32
    h = RET_HEADS
    c = RET_CHUNK
    log_gamma = np.log1p(-np.exp2(f32(-5.0) - np.arange(h, dtype=f32))).astype(f32)
    pos = np.arange(c, dtype=f32)
    diff = pos[:, None] - pos[None, :]
    decay = np.where(diff[None] >= 0.0, np.exp(np.maximum(diff, f32(0.0))[None] * log_gamma[:, None, None]), f32(0.0))
    scale = f32(RET_DK ** -0.5)
    dmask = decay * scale
    zeta = np.exp((f32(c - 1.0) - pos)[None, :] * log_gamma[:, None]) * scale
    xi = np.exp((pos + f32(1.0))[None, :] * log_gamma[:, None])
    cdec = np.exp(f32(c) * log_gamma)
    bc = lambda a: np.ascontiguousarray(np.broadcast_to(a[:, :, None], (h, c, c)), dtype=f32)
    return (dmask.astype(f32), bc(zeta), bc(xi),
            np.ascontiguousarray(np.broadcast_to(cdec[:, None, None], (h, c, c)), dtype=f32))


def _swa_tables(q_g, k_g):
    f32 = np.float32
    qi = np.arange(WINDOW)[:, None]
    sj = np.arange(2 * WINDOW)[None, :]
    rel = qi + WINDOW - sj
    ok = (rel >= 0) & (rel < WINDOW)
    slopes = np.exp2(f32(-8.0) * np.arange(1, SWA_HEADS + 1, dtype=f32) / f32(SWA_HEADS)).astype(f32)
    bias = np.where(ok[None], -slopes[:, None, None] * rel.astype(f32)[None], f32(NEG_INF)).astype(f32)
    bias = bias.reshape(SWA_KV_HEADS, SWA_GROUP * WINDOW, 2 * WINDOW)
    avg = lambda heads: jnp.asarray(np.kron(np.eye(heads, dtype=f32), np.full((SWA_HD, SWA_HD), 1.0 / SWA_HD, f32)),
                                    dtype=BF16)
    qg = jnp.tile(q_g.reshape(1, SWA_HD), (1, SWA_HEADS)) * (SWA_HD ** -0.5)
    kg = jnp.tile(k_g.reshape(1, SWA_HD), (1, SWA_KV_HEADS))
    return qg, kg, avg(SWA_HEADS), avg(SWA_KV_HEADS), jnp.asarray(bias)


def _block_kernel(per_seq, sinks_ref, x_ref, xp_ref, g_ref, w_hbm, dmask_ref, zeta_ref, xi_ref, cdec_ref,
                  gng_ref, gnb_ref, qg_ref, kg_ref, bq_ref, bk_ref, bias_ref,
                  wro_hbm, wso_hbm, wout_hbm, fg_ref, wr_ref, br_ref,
                  x1_ref, hp_ref, route_ref, cnt_ref,
                  w_ref, wro_ref, wso_ref, wout_ref, wtmp_ref, w_sem,
                  zc_ref, zn_ref, yr_ref, ys_ref, state_ref, prev_ref, band_ref, tail_ref):
    g = pl.program_id(0)
    seq_start = lax.rem(g - 1, per_seq) == 0

    @pl.when(g == 0)
    def _():
        slabs = [(src, dst, c0) for src, dst in ((w_hbm, w_ref), (wro_hbm, wro_ref), (wso_hbm, wso_ref),
                                                 (wout_hbm, wout_ref))
                 for c0 in range(0, dst.shape[1], W_SLAB)]

        def slab(k):
            src, dst, c0 = slabs[k]
            return pltpu.make_async_copy(src.at[:, pl.ds(c0, W_SLAB)], wtmp_ref.at[k % 2, pl.ds(0, dst.shape[0])],
                                         w_sem.at[k % 2])

        slab(0).start()
        for k, (_, dst, c0) in enumerate(slabs):
            if k + 1 < len(slabs):
                slab(k + 1).start()
            slab(k).wait()
            dst[:, c0:c0 + W_SLAB] = wtmp_ref[k % 2, :dst.shape[0]].astype(BF16)
        zc_ref[...] = jnp.zeros_like(zc_ref)
        tail_ref[...] = jnp.zeros_like(tail_ref)
        state_ref[...] = jnp.zeros_like(state_ref)

    @pl.when(seq_start)
    def _():
        state_ref[...] = jnp.zeros_like(state_ref)

    band_ref[:WINDOW, :] = tail_ref[...]
    band_ref[WINDOW:, :] = zc_ref[:, COL_SKV:COL_GATE]

    h = _rms(x_ref[...], g_ref[...]).astype(BF16)

    def project(c0, width):
        zn_ref[:, c0:c0 + width] = _dot(h, w_ref[:, c0:c0 + width]).astype(BF16)

    cw = 512
    proj_pieces = [functools.partial(project, j * cw, cw) for j in range(COL_SKV // cw)]
    proj_pieces.append(functools.partial(project, COL_SKV, Z_SKV_W))
    proj_pieces += [functools.partial(project, COL_GATE + j * cw, cw) for j in range(Z_GATE_W // cw)]

    def retention_rows(t):
        r_base = t * RET_ROWS
        n_chunks = RET_ROWS // RET_CHUNK
        part = lambda c, p, hd: zc_ref[r_base + c * RET_CHUNK:r_base + (c + 1) * RET_CHUNK,
                                       p * RET_W + hd * RET_DK:p * RET_W + (hd + 1) * RET_DK]
        for hd in range(RET_HEADS):
            state = state_ref[hd]
            for c in range(n_chunks):
                prev_ref[t * n_chunks + c, hd] = state.astype(BF16)
                kz = (part(c, 1, hd).astype(F32) * zeta_ref[hd]).astype(BF16)
                state = cdec_ref[hd] * state + _dot_tn(kz, part(c, 2, hd))
            state_ref[hd] = state
        out_rows = []
        for c in range(n_chunks):
            heads = []
            for hd in range(RET_HEADS):
                q = part(c, 0, hd)
                scores = _dot_nt(q, part(c, 1, hd)) * dmask_ref[hd]
                y = _dot(scores.astype(BF16), part(c, 2, hd)) + _dot(q, prev_ref[t * n_chunks + c, hd]) * xi_ref[hd]
                mu = jnp.mean(y, axis=-1, keepdims=True)
                d = y - mu
                var = jnp.mean(d * d, axis=-1, keepdims=True)
                hs = slice(hd * RET_DV, (hd + 1) * RET_DV)
                yn = d * lax.rsqrt(var + EPS) * gng_ref[:, hs] + gnb_ref[:, hs]
                gate = part(c, 3, hd).astype(F32)
                heads.append((gate * _sigmoid(gate) * yn).astype(BF16))
            out_rows.append(jnp.concatenate(heads, axis=1))
        yr_ref[r_base:r_base + RET_ROWS, :] = jnp.concatenate(out_rows, axis=0)

    group_rows = SWA_GROUP * WINDOW
    lo_q = lax.broadcasted_iota(jnp.int32, (WINDOW, 2 * SWA_HD), 1) < SWA_HD
    lo_k = lax.broadcasted_iota(jnp.int32, (2 * WINDOW, 2 * SWA_HD), 1) < SWA_HD
    in_prev = lax.broadcasted_iota(jnp.int32, (group_rows, 2 * WINDOW), 1) < WINDOW
    row_head = lax.broadcasted_iota(jnp.int32, (group_rows, 1), 0) // WINDOW

    def both_heads(v):
        swapped = pltpu.roll(v, SWA_HD, axis=1)
        return jnp.where(lo_k, v, swapped), jnp.where(lo_k, swapped, v)

    def swa_block(n):
        rows = slice(n * WINDOW, (n + 1) * WINDOW)
        kv = band_ref[n * WINDOW:(n + 2) * WINDOW, :]
        kf = kv[:, :SWA_KV].astype(F32)
        kn = kf * lax.rsqrt(_dot((kf * kf).astype(BF16), bk_ref[...]) + EPS) * kg_ref[...]
        keys = [a.astype(BF16) for a in both_heads(kn)]
        vals = [a.astype(BF16) for a in both_heads(kv[:, SWA_KV:].astype(F32))]
        qf = zc_ref[rows, COL_SQ:COL_SKV].astype(F32)
        qn = qf * lax.rsqrt(_dot((qf * qf).astype(BF16), bq_ref[...]) + EPS) * qg_ref[...]
        outs = []
        for kh in range(SWA_KV_HEADS):
            parts = []
            for j in range(SWA_GROUP // 2):
                c0 = (kh * SWA_GROUP + 2 * j) * SWA_HD
                two = qn[:, c0:c0 + 2 * SWA_HD]
                parts += [jnp.where(lo_q, two, 0.0), jnp.where(lo_q, 0.0, two)]
            qs = jnp.concatenate(parts, axis=0).astype(BF16)
            s = _dot_nt(qs, keys[kh]) + bias_ref[kh]
            if n == 0:
                s = jnp.where(jnp.logical_and(seq_start, in_prev), NEG_INF, s)
            sink = sinks_ref[kh * SWA_GROUP]
            for gi in range(1, SWA_GROUP):
                sink = jnp.where(row_head == gi, sinks_ref[kh * SWA_GROUP + gi], sink)
            m = jnp.maximum(jnp.max(s, axis=-1, keepdims=True), sink)
            p = jnp.exp(s - m)
            denom = jnp.sum(p, axis=-1, keepdims=True) + jnp.exp(sink - m)
            o = _dot(p.astype(BF16), vals[kh]) * (1.0 / denom)
            for j in range(SWA_GROUP // 2):
                even = o[(2 * j) * WINDOW:(2 * j + 1) * WINDOW]
                odd = o[(2 * j + 1) * WINDOW:(2 * j + 2) * WINDOW]
                outs.append(jnp.where(lo_q, even, odd).astype(BF16))
        ys_ref[rows, :] = jnp.concatenate(outs, axis=1)

    mix_pieces = [functools.partial(retention_rows, t) for t in range(BLOCK_ROWS // RET_ROWS)]
    mix_pieces += [functools.partial(swa_block, n) for n in range(BLOCK_ROWS // WINDOW)]

    subs = []

    def merge_sub(s):
        subs.append(_merge_rows(slice(s * MERGE_SUB, (s + 1) * MERGE_SUB), yr_ref, ys_ref, zc_ref, xp_ref, wro_ref,
                                wso_ref, wout_ref, fg_ref, wr_ref, br_ref))

    other_pieces = mix_pieces + [functools.partial(merge_sub, s) for s in range(BLOCK_ROWS // MERGE_SUB)]

    n_other = len(other_pieces)
    for k, piece in enumerate(other_pieces):
        for proj_piece in proj_pieces[k * len(proj_pieces) // n_other:(k + 1) * len(proj_pieces) // n_other]:
            proj_piece()
        piece()
    x1_ref[...] = jnp.concatenate([s[0] for s in subs], axis=0)
    hp_ref[...] = jnp.concatenate([s[1] for s in subs], axis=0)
    route_ref[...] = jnp.concatenate([s[2] for s in subs], axis=0)
    for s, sub in enumerate(subs):
        cnt_ref[s] = sub[3]

    tail_ref[...] = zc_ref[BLOCK_ROWS - WINDOW:, COL_SKV:COL_GATE]
    zc_ref[...] = zn_ref[...]


def _merge_rows(rows, yr_ref, ys_ref, z_ref, x_ref, wro_ref, wso_ref, wout_ref, fg_ref, wr_ref, br_ref):
    a = _dot(yr_ref[rows, :], wro_ref[...])
    b = _dot(ys_ref[rows, :], wso_ref[...])
    gate_r = z_ref[rows, COL_GATE:COL_GATE + D_MODEL].astype(F32)
    gate_s = z_ref[rows, COL_GATE + D_MODEL:].astype(F32)
    merged = (_sigmoid(gate_r) * a + _sigmoid(gate_s) * b).astype(BF16)
    x1 = x_ref[rows, :] + _dot(merged, wout_ref[...])
    h2 = _rms(x1, fg_ref[...])
    packed = _pack_halves(h2)

    logits = _dot(h2.astype(BF16), wr_ref[...]) + br_ref[...]
    lane = lax.broadcasted_iota(jnp.int32, logits.shape, 1)
    big = jnp.int32(ROUTE_LANES)
    is_group = jnp.logical_and(lane >= LANE_GROUP0, lane < LANE_GROUP0 + N_GROUPS)
    gl = jnp.where(is_group, logits, NEG_INF)
    gmax = jnp.max(gl, axis=-1, keepdims=True)
    g_w = 1.0 / jnp.sum(jnp.exp(gl - gmax), axis=-1, keepdims=True)
    g_sel = jnp.min(jnp.where(gl == gmax, lane, big), axis=-1, keepdims=True) - LANE_GROUP0
    in_group = jnp.logical_and(lane < N_EXPERTS, (lane >> 3) == g_sel)
    el = jnp.where(in_group, logits, NEG_INF)
    m1 = jnp.max(el, axis=-1, keepdims=True)
    i1 = jnp.min(jnp.where(el == m1, lane, big), axis=-1, keepdims=True)
    el2 = jnp.where(lane == i1, NEG_INF, el)
    m2 = jnp.max(el2, axis=-1, keepdims=True)
    i2 = jnp.min(jnp.where(el2 == m2, lane, big), axis=-1, keepdims=True)
    e2 = jnp.exp(m2 - m1)
    c1 = g_w / (1.0 + e2)
    c2 = g_w * e2 / (1.0 + e2)
    chosen = jnp.where(jnp.logical_or(lane == i1, lane == i2), 1.0, 0.0)
    route = (chosen
             + jnp.where(lane == LANE_E1, i1.astype(F32), 0.0)
             + jnp.where(lane == LANE_E2, i2.astype(F32), 0.0)
             + jnp.where(lane == LANE_C1, c1, 0.0)
             + jnp.where(lane == LANE_C2, c2, 0.0)
             + jnp.where(lane == i2 + LANE_SECOND0, 1.0, 0.0))
    return x1, packed, route, jnp.sum(chosen, axis=0, keepdims=True)


def _block(x2d, mix_g, w_in, gn_g, gn_b, sinks, q_g, k_g, w_ret_o, w_swa_o, w_out, ffn_g, w_router, b_router,
           seq):
    t = x2d.shape[0]
    n = t // BLOCK_ROWS
    assert seq % BLOCK_ROWS == 0
    tables = _retention_tables()
    swa_tables = _swa_tables(q_g, k_g)
    consts = tables + (gn_g, gn_b) + swa_tables
    merge_weights = (w_ret_o, w_swa_o, w_out)
    small = (ffn_g, w_router, b_router)
    hbm = pl.BlockSpec(memory_space=pl.ANY)
    full = lambda a: pl.BlockSpec(a.shape, lambda g: (0,) * a.ndim)
    cur = lambda w: pl.BlockSpec((BLOCK_ROWS, w), lambda g: (jnp.minimum(g, n - 1), 0))
    prv = lambda w: pl.BlockSpec((BLOCK_ROWS, w), lambda g: (jnp.maximum(g - 1, 0), 0))
    chunks = BLOCK_ROWS // RET_CHUNK
    subs_per_block = BLOCK_ROWS // MERGE_SUB
    return pl.pallas_call(
        functools.partial(_block_kernel, seq // BLOCK_ROWS),
        grid=(n + 1,),
        in_specs=([pl.BlockSpec(memory_space=pltpu.SMEM), cur(D_MODEL), prv(D_MODEL), full(mix_g), hbm]
                  + [full(a) for a in consts] + [hbm] * len(merge_weights) + [full(a) for a in small]),
        out_specs=[prv(D_MODEL), prv(HALF_D), prv(ROUTE_LANES),
                   pl.BlockSpec((subs_per_block, 1, ROUTE_LANES), lambda g: (jnp.maximum(g - 1, 0), 0, 0))],
        out_shape=[jax.ShapeDtypeStruct((t, D_MODEL), F32),
                   jax.ShapeDtypeStruct((t, HALF_D), I32),
                   jax.ShapeDtypeStruct((t, ROUTE_LANES), F32),
                   jax.ShapeDtypeStruct((n * subs_per_block, 1, ROUTE_LANES), F32)],
        scratch_shapes=[pltpu.VMEM(w_in.shape, BF16)] + [pltpu.VMEM(w.shape, BF16) for w in merge_weights]
                       + [pltpu.VMEM((2, D_MODEL, W_SLAB), F32), pltpu.SemaphoreType.DMA((2,)),
                        pltpu.VMEM((BLOCK_ROWS, IN_WIDTH), BF16), pltpu.VMEM((BLOCK_ROWS, IN_WIDTH), BF16),
                        pltpu.VMEM((BLOCK_ROWS, RET_W), BF16), pltpu.VMEM((BLOCK_ROWS, SWA_Q), BF16),
                        pltpu.VMEM((RET_HEADS, RET_DK, RET_DV), F32),
                        pltpu.VMEM((chunks, RET_HEADS, RET_DK, RET_DV), BF16),
                        pltpu.VMEM((BLOCK_ROWS + WINDOW, Z_SKV_W), BF16), pltpu.VMEM((WINDOW, Z_SKV_W), BF16)],
        compiler_params=pltpu.CompilerParams(dimension_semantics=("arbitrary",),
                                             vmem_limit_bytes=VMEM_LIMIT),
        name="block",
    )(sinks, x2d, x2d, mix_g, w_in, *consts, *merge_weights, *small)


def _plan_kernel(cnt_ref, route_ref, base_ref, meta_ref, pos_ref):
    counts = cnt_ref[...]
    nt = counts.shape[0]
    exact = functools.partial(jnp.dot, precision=lax.Precision.HIGHEST, preferred_element_type=F32)
    total = jnp.sum(counts, axis=0, keepdims=True)
    tiles_e = jnp.ceil(total * (1.0 / ROW_TILE))
    k = lax.broadcasted_iota(jnp.int32, (ROUTE_LANES, ROUTE_LANES), 0)
    lane = lax.broadcasted_iota(jnp.int32, (ROUTE_LANES, ROUTE_LANES), 1)
    tile_end = exact(jnp.broadcast_to(tiles_e, (8, ROUTE_LANES)), jnp.where(k <= lane, 1.0, 0.0))[0:1]
    tile_start = tile_end - tiles_e
    r = lax.broadcasted_iota(jnp.int32, (nt, nt), 0)
    c = lax.broadcasted_iota(jnp.int32, (nt, nt), 1)
    before = exact(jnp.where(c < r, 1.0, 0.0), counts)
    base_ref[...] = tile_start * ROW_TILE + before

    used = jnp.max(tile_end, axis=-1, keepdims=True)
    tile = k.astype(F32)
    clamped = jnp.minimum(tile, jnp.maximum(used - 1.0, 0.0))
    is_expert = lane < N_EXPERTS
    owner = jnp.sum(jnp.where(jnp.logical_and(is_expert, tile_end <= clamped), 1.0, 0.0), axis=-1, keepdims=True)
    owner = jnp.minimum(owner, N_EXPERTS - 1.0)
    mine = lane.astype(F32) == owner
    pick = lambda v: jnp.sum(jnp.where(mine, v, 0.0), axis=-1, keepdims=True)
    t1 = tile[:, 0:1]
    valid = jnp.clip(pick(total) - (t1 - pick(tile_start)) * ROW_TILE, 0.0, float(ROW_TILE))
    valid = jnp.where(t1 < used, valid, 0.0)

    nonempty = jnp.where(jnp.logical_and(lane[0:1] < N_EXPERTS, tiles_e > 0.0), 1.0, 0.0)
    rank = exact(jnp.broadcast_to(nonempty, (8, ROUTE_LANES)), jnp.where(k < lane, 1.0, 0.0))[0:1]
    nonempty_col = jnp.sum(jnp.where(k == lane, jnp.broadcast_to(nonempty, k.shape), 0.0), axis=-1, keepdims=True)
    later = jnp.where(jnp.logical_and(k > lane, nonempty_col > 0.0), tile, float(ROUTE_LANES))
    nxt_e = jnp.min(later, axis=0, keepdims=True)
    nxt_e = jnp.where(nxt_e < float(ROUTE_LANES), nxt_e, -1.0)
    first = jnp.where(jnp.logical_and(t1 == pick(tile_start), t1 < used), 1.0, 0.0)
    my_rank = pick(rank)
    slot = my_rank - 2.0 * jnp.floor(my_rank * 0.5)
    fields = (owner, valid, first, pick(nxt_e), slot, jnp.broadcast_to(used, owner.shape))
    by_tile = sum(jnp.where(lane == n, f, 0.0) for n, f in enumerate(fields))
    meta_ref[...] = by_tile.T[:8, :].astype(I32)

    sel_row = lax.broadcasted_iota(jnp.int32, (8, ROUTE_LANES), 0)
    sel_lane = lax.broadcasted_iota(jnp.int32, (8, ROUTE_LANES), 1)
    tok_r = lax.broadcasted_iota(jnp.int32, (MERGE_SUB, MERGE_SUB), 0)
    tok_c = lax.broadcasted_iota(jnp.int32, (MERGE_SUB, MERGE_SUB), 1)
    eye = jnp.where(tok_r == tok_c, 1.0, 0.0).astype(BF16)
    upto = jnp.where(tok_r <= tok_c, 1.0, 0.0).astype(BF16)
    out_row = lax.broadcasted_iota(jnp.int32, (8, MERGE_SUB), 0)
    sum_experts = lambda a: jnp.sum(a, axis=0, keepdims=True)

    def sub_positions(s, carry):
        route = route_ref[pl.ds(pl.multiple_of(s * MERGE_SUB, MERGE_SUB), MERGE_SUB), :].astype(BF16)
        record_t = _dot_tn(route, eye)
        incl_t = _dot_tn(route, upto)[:N_EXPERTS]
        both_t = record_t[:N_EXPERTS]
        second_t = record_t[LANE_SECOND0:LANE_SECOND0 + N_EXPERTS]
        rank2 = sum_experts(second_t * incl_t) - 1.0
        rank1 = sum_experts(both_t * incl_t) - rank2 - 2.0
        base_row = jnp.where(sel_lane < N_EXPERTS, jnp.broadcast_to(base_ref[pl.ds(s, 1), :], (8, ROUTE_LANES)), 0.0)
        hi = jnp.floor(base_row * (1.0 / 256.0))
        lo = base_row - 256.0 * hi
        hi2 = pltpu.roll(hi, LANE_SECOND0, axis=1)
        lo2 = pltpu.roll(lo, LANE_SECOND0, axis=1)
        table = (jnp.where(sel_row == 0, hi, 0.0) + jnp.where(sel_row == 1, lo, 0.0)
                 + jnp.where(sel_row == 2, hi2, 0.0) + jnp.where(sel_row == 3, lo2, 0.0))
        r = _dot_nt(table.astype(BF16), route)
        base_both = 256.0 * r[0:1] + r[1:2]
        base2 = 256.0 * r[2:3] + r[3:4]
        out = jnp.where(out_row == 0, base_both - base2 + rank1, jnp.where(out_row == 1, base2 + rank2, 0.0))
        pos_ref[s] = out.astype(I32)
        return carry

    lax.fori_loop(0, nt, sub_positions, 0, unroll=4)


def _tile_plan(cnt, route, n_tiles):
    assert n_tiles <= ROUTE_LANES
    nt = cnt.shape[0]
    _, meta, pos = pl.pallas_call(
        _plan_kernel,
        out_shape=[jax.ShapeDtypeStruct((nt, ROUTE_LANES), F32), jax.ShapeDtypeStruct((8, ROUTE_LANES), I32),
                   jax.ShapeDtypeStruct((nt, 8, MERGE_SUB), I32)],
        compiler_params=pltpu.CompilerParams(vmem_limit_bytes=VMEM_LIMIT),
        name="tile_plan",
    )(cnt.reshape(nt, ROUTE_LANES), route)
    return meta, pos[:, 0, :], pos[:, 1, :]


def _sc_mesh():
    return plsc.VectorSubcoreMesh(core_axis_name="c", subcore_axis_name="s")


def _sc_worker():
    return lax.axis_index("s") * SC_CORES + lax.axis_index("c")


def _dispatch(hp, pos1, pos2, n_rows):
    t = hp.shape[0]
    per_worker = t // SC_WORKERS
    k = per_worker // SC_CHUNK
    idx = pltpu.VMEM((k, SC_CHUNK), I32)

    @functools.partial(pl.kernel, mesh=_sc_mesh(), out_type=jax.ShapeDtypeStruct((n_rows, HALF_D), I32),
                       scratch_types=[idx, idx, pltpu.VMEM((SC_CHUNK, HALF_D), I32)], name="moe_dispatch")
    def run(h_hbm, p1_hbm, p2_hbm, xs_hbm, p1_v, p2_v, rows_v):
        wid = _sc_worker()
        pltpu.sync_copy(p1_hbm.at[wid], p1_v)
        pltpu.sync_copy(p2_hbm.at[wid], p2_v)
        for j in range(k):
            pltpu.sync_copy(h_hbm.at[pl.ds(wid * per_worker + j * SC_CHUNK, SC_CHUNK)], rows_v)
            pltpu.sync_copy(rows_v, xs_hbm.at[p1_v.at[j]])
            pltpu.sync_copy(rows_v, xs_hbm.at[p2_v.at[j]])

    return run(hp, pos1.reshape(SC_WORKERS, k, SC_CHUNK), pos2.reshape(SC_WORKERS, k, SC_CHUNK))


def _combine_gather(ys, pos1, pos2):
    t = pos1.size
    per_worker = t // SC_WORKERS
    k = per_worker // SC_CHUNK
    idx = pltpu.VMEM((k, SC_CHUNK), I32)
    out = jax.ShapeDtypeStruct((t, HALF_D), I32)

    @functools.partial(pl.kernel, mesh=_sc_mesh(), out_type=(out, out),
                       scratch_types=[idx, idx, pltpu.VMEM((SC_CHUNK, HALF_D), I32)], name="moe_combine")
    def run(ys_hbm, p1_hbm, p2_hbm, g1_hbm, g2_hbm, p1_v, p2_v, rows_v):
        wid = _sc_worker()
        pltpu.sync_copy(p1_hbm.at[wid], p1_v)
        pltpu.sync_copy(p2_hbm.at[wid], p2_v)
        for j in range(k):
            dst = pl.ds(wid * per_worker + j * SC_CHUNK, SC_CHUNK)
            pltpu.sync_copy(ys_hbm.at[p1_v.at[j]], rows_v)
            pltpu.sync_copy(rows_v, g1_hbm.at[dst])
            pltpu.sync_copy(ys_hbm.at[p2_v.at[j]], rows_v)
            pltpu.sync_copy(rows_v, g2_hbm.at[dst])

    return run(ys, pos1.reshape(SC_WORKERS, k, SC_CHUNK), pos2.reshape(SC_WORKERS, k, SC_CHUNK))


META_OWNER, META_VALID, META_FIRST, META_NEXT, META_SLOT, META_USED = range(6)


def _experts_kernel(meta_ref, xs_hbm, wg_hbm, wu_hbm, wd_hbm, ys_ref,
                    xs_buf, xs_sem, wg_f, wu_f, wd_f, wg_bf, wu_bf, wd_bf, w_sem):
    i = pl.program_id(0)
    used = meta_ref[META_USED, 0]
    valid = meta_ref[META_VALID, i]
    owner = meta_ref[META_OWNER, i]
    nxt = meta_ref[META_NEXT, i]

    def rows_in(tile):
        start = pl.multiple_of(tile * ROW_TILE, ROW_TILE)
        slot = lax.rem(tile, XS_SLOTS)
        return pltpu.make_async_copy(xs_hbm.at[pl.ds(start, ROW_TILE)], xs_buf.at[slot], xs_sem.at[slot])

    @pl.when(i == 0)
    def _():
        for ahead in range(XS_SLOTS - 1):
            pl.when(ahead < used)(rows_in(ahead).start)

    @pl.when(i + (XS_SLOTS - 1) < used)
    def _():
        rows_in(i + (XS_SLOTS - 1)).start()

    def fetch(expert, slot):
        pairs = ((wg_hbm, wg_f), (wu_hbm, wu_f), (wd_hbm, wd_f))
        return [pltpu.make_async_copy(w.at[expert], buf.at[slot], w_sem.at[slot, n])
                for n, (w, buf) in enumerate(pairs)]

    @pl.when(i == 0)
    def _():
        for copy in fetch(owner, 0):
            copy.start()

    @pl.when(meta_ref[META_FIRST, i] == 1)
    def _():
        slot = meta_ref[META_SLOT, i]
        for copy in fetch(owner, slot):
            copy.wait()
        wg_bf[...] = wg_f[slot].astype(BF16)
        wu_bf[...] = wu_f[slot].astype(BF16)
        wd_bf[...] = wd_f[slot].astype(BF16)

        @pl.when(nxt >= 0)
        def _():
            for copy in fetch(nxt, 1 - slot):
                copy.start()

    @pl.when(i < used)
    def _():
        rows_in(i).wait()
        row = lax.broadcasted_iota(jnp.int32, (ROW_TILE, HALF_D), 0)
        words = xs_buf[lax.rem(i, XS_SLOTS)]
        packed = jnp.where(row < valid, words, 0)
        hi, lo = _unpack_halves(packed)
        hi = hi.astype(BF16)
        lo = lo.astype(BF16)
        hg = _dot(hi, wg_bf[:HALF_D, :]) + _dot(lo, wg_bf[HALF_D:, :])
        hu = _dot(hi, wu_bf[:HALF_D, :]) + _dot(lo, wu_bf[HALF_D:, :])
        hid = (hg * _sigmoid(hg) * hu).astype(BF16)
        ys_ref[...] = _pack_halves(_dot(hid, wd_bf[...]))


def _experts(xs, meta, n_tiles, wg, wu, wd):
    tile = lambda i, meta: (jnp.minimum(i, meta[META_USED, 0] - 1), 0)
    hbm = pl.BlockSpec(memory_space=pl.ANY)
    up_shape, down_shape = (D_MODEL, EXPERT_FF), (EXPERT_FF, D_MODEL)
    grid_spec = pltpu.PrefetchScalarGridSpec(
        num_scalar_prefetch=1,
        grid=(n_tiles,),
        in_specs=[hbm, hbm, hbm, hbm],
        out_specs=pl.BlockSpec((ROW_TILE, HALF_D), tile),
        scratch_shapes=[pltpu.VMEM((XS_SLOTS, ROW_TILE, HALF_D), I32), pltpu.SemaphoreType.DMA((XS_SLOTS,)),
                        pltpu.VMEM((2,) + up_shape, F32), pltpu.VMEM((2,) + up_shape, F32),
                        pltpu.VMEM((2,) + down_shape, F32),
                        pltpu.VMEM(up_shape, BF16), pltpu.VMEM(up_shape, BF16), pltpu.VMEM(down_shape, BF16),
                        pltpu.SemaphoreType.DMA((2, 3))],
    )
    return pl.pallas_call(
        _experts_kernel,
        grid_spec=grid_spec,
        out_shape=jax.ShapeDtypeStruct((n_tiles * ROW_TILE, HALF_D), I32),
        compiler_params=pltpu.CompilerParams(dimension_semantics=("arbitrary",),
                                             vmem_limit_bytes=VMEM_LIMIT),
        name="experts",
    )(meta, xs, wg, wu, wd)


def _ple_kernel(x1_ref, g1_ref, g2_ref, route_ref, p_ref, wp_f32, pg_ref, gg_ref, wgate_f32, o_ref,
                wp_ref, wgate_ref):
    @pl.when(pl.program_id(0) == 0)
    def _():
        wp_ref[...] = wp_f32[...].astype(BF16)
        wgate_ref[...] = wgate_f32[...].astype(BF16)

    route = route_ref[...]
    lane = lax.broadcasted_iota(jnp.int32, route.shape, 1)
    pick = lambda ln: jnp.sum(jnp.where(lane == ln, route, 0.0), axis=-1, keepdims=True)
    c1 = pick(LANE_C1)
    c2 = pick(LANE_C2)
    a_hi, a_lo = _unpack_halves(g1_ref[...])
    b_hi, b_lo = _unpack_halves(g2_ref[...])
    x = x1_ref[...] + jnp.concatenate([c1 * a_hi + c2 * b_hi, c1 * a_lo + c2 * b_lo], axis=1)
    ple = _rms(_dot(p_ref[...].astype(BF16), wp_ref[...]), pg_ref[...])
    gate = _sigmoid(_dot(_rms(x, gg_ref[...]).astype(BF16), wgate_ref[...]))
    o_ref[...] = x + gate * ple


def _ple(x1, g1, g2, route, p2d, w_ple, ple_g, gate_g, w_gate):
    t = x1.shape[0]
    row = lambda w: pl.BlockSpec((PLE_TM, w), lambda i: (i, 0))
    full = lambda a: pl.BlockSpec(a.shape, lambda i: (0,) * a.ndim)
    return pl.pallas_call(
        _ple_kernel,
        grid=(t // PLE_TM,),
        in_specs=[row(D_MODEL), row(HALF_D), row(HALF_D), row(ROUTE_LANES), row(PLE_DIM),
                  full(w_ple), full(ple_g), full(gate_g), full(w_gate)],
        out_specs=row(D_MODEL),
        out_shape=jax.ShapeDtypeStruct((t, D_MODEL), F32),
        scratch_shapes=[pltpu.VMEM(w_ple.shape, BF16), pltpu.VMEM(w_gate.shape, BF16)],
        compiler_params=pltpu.CompilerParams(dimension_semantics=("arbitrary",),
                                             vmem_limit_bytes=VMEM_LIMIT),
        name="ple",
    )(x1, g1, g2, route, p2d, w_ple, ple_g, gate_g, w_gate)


def kernel(x, p, mix_norm_g, w_in, ret_gn_g, ret_gn_b, w_ret_o, q_norm_g, k_norm_g, attn_sinks,
           w_swa_o, w_out, ffn_norm_g, w_router_group, b_router_group, w_router_expert,
           b_router_expert, w_exp_gate, w_exp_up, w_exp_down, ple_gate_norm_g, w_ple_gate,
           w_ple, ple_norm_g):
    batch, seq, d = x.shape
    t = batch * seq
    depth = w_in.shape[0]
    n_tiles = 2 * t // ROW_TILE + N_EXPERTS
    row = lambda a: a.reshape(1, -1)
    x2d = x.reshape(t, d)
    for i in range(depth):
        pad = ROUTE_LANES - N_GROUPS - N_EXPERTS
        w_router = jnp.pad(jnp.concatenate([w_router_expert[i], w_router_group[i]], axis=1), ((0, 0), (0, pad)))
        b_router = jnp.pad(jnp.concatenate([b_router_expert[i], b_router_group[i]]), (0, pad)).reshape(1, -1)
        w_router = w_router.astype(BF16)
        x1, hp, route, cnt = _block(x2d, row(mix_norm_g[i]), w_in[i], row(ret_gn_g[i]),
                                    row(ret_gn_b[i]), attn_sinks[i], q_norm_g[i], k_norm_g[i],
                                    w_ret_o[i], w_swa_o[i], w_out[i],
                                    row(ffn_norm_g[i]), w_router, b_router, seq)
        meta, pos1, pos2 = _tile_plan(cnt, route, n_tiles)
        xs = _dispatch(hp, pos1, pos2, n_tiles * ROW_TILE)
        ys = _experts(xs, meta, n_tiles, w_exp_gate[i], w_exp_up[i], w_exp_down[i])
        g1, g2 = _combine_gather(ys, pos1, pos2)
        x2d = _ple(x1, g1, g2, route, p[i].reshape(t, PLE_DIM), w_ple[i], row(ple_norm_g[i]),
                   row(ple_gate_norm_g[i]), w_ple_gate[i])
    return x2d.reshape(batch, seq, d)
```

```python
import functools

import jax
import jax.numpy as jnp
import numpy as np
from jax import lax
from jax.experimental import pallas as pl
from jax.experimental.pallas import tpu as pltpu
from jax.experimental.pallas import tpu_sc as plsc

F32 = jnp.float32
BF16 = jnp.bfloat16
I32 = jnp.int32

EPS = 1e-6
D_MODEL = 1024
PLE_DIM = 256
RET_HEADS = 4
RET_DK = 128
RET_DV = 128
RET_CHUNK = 128
SWA_HEADS = 8
SWA_KV_HEADS = 2
SWA_GROUP = SWA_HEADS // SWA_KV_HEADS
SWA_HD = 64
WINDOW = 128
N_GROUPS = 4
EXPERTS_PER_GROUP = 8
N_EXPERTS = N_GROUPS * EXPERTS_PER_GROUP
EXPERT_FF = 256

RET_W = RET_HEADS * RET_DK
SWA_Q = SWA_HEADS * SWA_HD
SWA_KV = SWA_KV_HEADS * SWA_HD
Z_RET_W = 4 * RET_W
Z_SKV_W = 2 * SWA_KV
Z_GATE_W = 2 * D_MODEL
COL_SQ = Z_RET_W
COL_SKV = COL_SQ + SWA_Q
COL_GATE = COL_SKV + Z_SKV_W
IN_WIDTH = COL_GATE + Z_GATE_W

ROUTE_LANES = 128
LANE_GROUP0 = N_EXPERTS
LANE_E1, LANE_E2, LANE_C1, LANE_C2 = 32, 33, 34, 35
LANE_SECOND0 = 64
NEG_INF = -1e30

HALF_D = D_MODEL // 2
ROW_TILE = 512
XS_SLOTS = 3
SC_CORES = 2
SC_SUBCORES = 16
SC_WORKERS = SC_CORES * SC_SUBCORES
SC_CHUNK = 128

VMEM_LIMIT = 56 * 1024 * 1024

BLOCK_ROWS = 512
W_SLAB = 256
RET_ROWS = 512
MERGE_SUB = 256
PLE_TM = 1024
PLE_SLOTS = 3


def _rms(x, g):
    ms = jnp.mean(x * x, axis=-1, keepdims=True)
    return x * lax.rsqrt(ms + EPS) * g


def _sigmoid(x):
    return 1.0 / (1.0 + jnp.exp(-x))


def _dot(a, b):
    return jnp.dot(a, b, preferred_element_type=F32)


def _dot_nt(a, b):
    return lax.dot_general(a, b, (((1,), (1,)), ((), ())), preferred_element_type=F32)


def _dot_tn(a, b):
    return lax.dot_general(a, b, (((0,), (0,)), ((), ())), preferred_element_type=F32)


def _pack_halves(v):
    return pltpu.pack_elementwise([v[:, :HALF_D], v[:, HALF_D:]], packed_dtype=BF16)


def _unpack_halves(p):
    words = lax.bitcast_convert_type(p, jnp.uint32)
    return tuple(pltpu.unpack_elementwise(words, index=k, packed_dtype=BF16, unpacked_dtype=F32) for k in range(2))


def _retention_tables():
    f32 = np.float32
    h = RET_HEADS
    c = RET_CHUNK
    log_gamma = np.log1p(-np.exp2(f32(-5.0) - np.arange(h, dtype=f32))).astype(f32)
    pos = np.arange(c, dtype=f32)
    diff = pos[:, None] - pos[None, :]
    decay = np.where(diff[None] >= 0.0, np.exp(np.maximum(diff, f32(0.0))[None] * log_gamma[:, None, None]), f32(0.0))
    scale = f32(RET_DK ** -0.5)
    dmask = decay * scale
    zeta = np.exp((f32(c - 1.0) - pos)[None, :] * log_gamma[:, None]) * scale
    xi = np.exp((pos + f32(1.0))[None, :] * log_gamma[:, None])
    cdec = np.exp(f32(c) * log_gamma)
    bc = lambda a: np.ascontiguousarray(np.broadcast_to(a[:, :, None], (h, c, c)), dtype=f32)
    return (dmask.astype(f32), bc(zeta), bc(xi),
            np.ascontiguousarray(np.broadcast_to(cdec[:, None, None], (h, c, c)), dtype=f32))


def _swa_tables(q_g, k_g):
    f32 = np.float32
    qi = np.arange(WINDOW)[:, None]
    sj = np.arange(2 * WINDOW)[None, :]
    rel = qi + WINDOW - sj
    ok = (rel >= 0) & (rel < WINDOW)
    slopes = np.exp2(f32(-8.0) * np.arange(1, SWA_HEADS + 1, dtype=f32) / f32(SWA_HEADS)).astype(f32)
    bias = np.where(ok[None], -slopes[:, None, None] * rel.astype(f32)[None], f32(NEG_INF)).astype(f32)
    bias = bias.reshape(SWA_KV_HEADS, SWA_GROUP * WINDOW, 2 * WINDOW)
    avg = lambda heads: jnp.asarray(np.kron(np.eye(heads, dtype=f32), np.full((SWA_HD, SWA_HD), 1.0 / SWA_HD, f32)),
                                    dtype=BF16)
    qg = jnp.tile(q_g.reshape(1, SWA_HD), (1, SWA_HEADS)) * (SWA_HD ** -0.5)
    kg = jnp.tile(k_g.reshape(1, SWA_HD), (1, SWA_KV_HEADS))
    return qg, kg, avg(SWA_HEADS), avg(SWA_KV_HEADS), jnp.asarray(bias)


def _block_kernel(per_seq, sinks_ref, x_ref, xp_ref, g_ref, w_hbm, dmask_ref, zeta_ref, xi_ref, cdec_ref,
                  gng_ref, gnb_ref, qg_ref, kg_ref, bq_ref, bk_ref, bias_ref,
                  wro_hbm, wso_hbm, wout_hbm, fg_ref, wr_ref, br_ref,
                  x1_ref, hp_ref, route_ref, cnt_ref,
                  w_ref, wro_ref, wso_ref, wout_ref, wtmp_ref, w_sem,
                  zc_ref, zn_ref, yr_ref, ys_ref, state_ref, prev_ref, band_ref, tail_ref):
    g = pl.program_id(0)
    seq_start = lax.rem(g - 1, per_seq) == 0

    @pl.when(g == 0)
    def _():
        slabs = [(src, dst, c0) for src, dst in ((w_hbm, w_ref), (wro_hbm, wro_ref), (wso_hbm, wso_ref),
                                                 (wout_hbm, wout_ref))
                 for c0 in range(0, dst.shape[1], W_SLAB)]

        def slab(k):
            src, dst, c0 = slabs[k]
            return pltpu.make_async_copy(src.at[:, pl.ds(c0, W_SLAB)], wtmp_ref.at[k % 2, pl.ds(0, dst.shape[0])],
                                         w_sem.at[k % 2])

        slab(0).start()
        for k, (_, dst, c0) in enumerate(slabs):
            if k + 1 < len(slabs):
                slab(k + 1).start()
            slab(k).wait()
            dst[:, c0:c0 + W_SLAB] = wtmp_ref[k % 2, :dst.shape[0]].astype(BF16)
        zc_ref[...] = jnp.zeros_like(zc_ref)
        tail_ref[...] = jnp.zeros_like(tail_ref)
        state_ref[...] = jnp.zeros_like(state_ref)

    @pl.when(seq_start)
    def _():
        state_ref[...] = jnp.zeros_like(state_ref)

    band_ref[:WINDOW, :] = tail_ref[...]
    band_ref[WINDOW:, :] = zc_ref[:, COL_SKV:COL_GATE]

    h = _rms(x_ref[...], g_ref[...]).astype(BF16)

    def project(c0, width):
        zn_ref[:, c0:c0 + width] = _dot(h, w_ref[:, c0:c0 + width]).astype(BF16)

    cw = 512
    proj_pieces = [functools.partial(project, j * cw, cw) for j in range(COL_SKV // cw)]
    proj_pieces.append(functools.partial(project, COL_SKV, Z_SKV_W))
    proj_pieces += [functools.partial(project, COL_GATE + j * cw, cw) for j in range(Z_GATE_W // cw)]

    def retention_rows(t):
        r_base = t * RET_ROWS
        n_chunks = RET_ROWS // RET_CHUNK
        part = lambda c, p, hd: zc_ref[r_base + c * RET_CHUNK:r_base + (c + 1) * RET_CHUNK,
                                       p * RET_W + hd * RET_DK:p * RET_W + (hd + 1) * RET_DK]
        for hd in range(RET_HEADS):
            state = state_ref[hd]
            for c in range(n_chunks):
                prev_ref[t * n_chunks + c, hd] = state.astype(BF16)
                kz = (part(c, 1, hd).astype(F32) * zeta_ref[hd]).astype(BF16)
                state = cdec_ref[hd] * state + _dot_tn(kz, part(c, 2, hd))
            state_ref[hd] = state
        out_rows = []
        for c in range(n_chunks):
            heads = []
            for hd in range(RET_HEADS):
                q = part(c, 0, hd)
                scores = _dot_nt(q, part(c, 1, hd)) * dmask_ref[hd]
                y = _dot(scores.astype(BF16), part(c, 2, hd)) + _dot(q, prev_ref[t * n_chunks + c, hd]) * xi_ref[hd]
                mu = jnp.mean(y, axis=-1, keepdims=True)
                d = y - mu
                var = jnp.mean(d * d, axis=-1, keepdims=True)
                hs = slice(hd * RET_DV, (hd + 1) * RET_DV)
                yn = d * lax.rsqrt(var + EPS) * gng_ref[:, hs] + gnb_ref[:, hs]
                gate = part(c, 3, hd).astype(F32)
                heads.append((gate * _sigmoid(gate) * yn).astype(BF16))
            out_rows.append(jnp.concatenate(heads, axis=1))
        yr_ref[r_base:r_base + RET_ROWS, :] = jnp.concatenate(out_rows, axis=0)

    group_rows = SWA_GROUP * WINDOW
    lo_q = lax.broadcasted_iota(jnp.int32, (WINDOW, 2 * SWA_HD), 1) < SWA_HD
    lo_k = lax.broadcasted_iota(jnp.int32, (2 * WINDOW, 2 * SWA_HD), 1) < SWA_HD
    in_prev = lax.broadcasted_iota(jnp.int32, (group_rows, 2 * WINDOW), 1) < WINDOW
    row_head = lax.broadcasted_iota(jnp.int32, (group_rows, 1), 0) // WINDOW

    def both_heads(v):
        swapped = pltpu.roll(v, SWA_HD, axis=1)
        return jnp.where(lo_k, v, swapped), jnp.where(lo_k, swapped, v)

    def swa_block(n):
        rows = slice(n * WINDOW, (n + 1) * WINDOW)
        kv = band_ref[n * WINDOW:(n + 2) * WINDOW, :]
        kf = kv[:, :SWA_KV].astype(F32)
        kn = kf * lax.rsqrt(_dot((kf * kf).astype(BF16), bk_ref[...]) + EPS) * kg_ref[...]
        keys = [a.astype(BF16) for a in both_heads(kn)]
        vals = [a.astype(BF16) for a in both_heads(kv[:, SWA_KV:].astype(F32))]
        qf = zc_ref[rows, COL_SQ:COL_SKV].astype(F32)
        qn = qf * lax.rsqrt(_dot((qf * qf).astype(BF16), bq_ref[...]) + EPS) * qg_ref[...]
        outs = []
        for kh in range(SWA_KV_HEADS):
            parts = []
            for j in range(SWA_GROUP // 2):
                c0 = (kh * SWA_GROUP + 2 * j) * SWA_HD
                two = qn[:, c0:c0 + 2 * SWA_HD]
                parts += [jnp.where(lo_q, two, 0.0), jnp.where(lo_q, 0.0, two)]
            qs = jnp.concatenate(parts, axis=0).astype(BF16)
            s = _dot_nt(qs, keys[kh]) + bias_ref[kh]
            if n == 0:
                s = jnp.where(jnp.logical_and(seq_start, in_prev), NEG_INF, s)
            sink = sinks_ref[kh * SWA_GROUP]
            for gi in range(1, SWA_GROUP):
                sink = jnp.where(row_head == gi, sinks_ref[kh * SWA_GROUP + gi], sink)
            m = jnp.maximum(jnp.max(s, axis=-1, keepdims=True), sink)
            p = jnp.exp(s - m)
            denom = jnp.sum(p, axis=-1, keepdims=True) + jnp.exp(sink - m)
            o = _dot(p.astype(BF16), vals[kh]) * (1.0 / denom)
            for j in range(SWA_GROUP // 2):
                even = o[(2 * j) * WINDOW:(2 * j + 1) * WINDOW]
                odd = o[(2 * j + 1) * WINDOW:(2 * j + 2) * WINDOW]
                outs.append(jnp.where(lo_q, even, odd).astype(BF16))
        ys_ref[rows, :] = jnp.concatenate(outs, axis=1)

    mix_pieces = [functools.partial(retention_rows, t) for t in range(BLOCK_ROWS // RET_ROWS)]
    mix_pieces += [functools.partial(swa_block, n) for n in range(BLOCK_ROWS // WINDOW)]

    subs = []

    def merge_sub(s):
        subs.append(_merge_rows(slice(s * MERGE_SUB, (s + 1) * MERGE_SUB), yr_ref, ys_ref, zc_ref, xp_ref, wro_ref,
                                wso_ref, wout_ref, fg_ref, wr_ref, br_ref))

    other_pieces = mix_pieces + [functools.partial(merge_sub, s) for s in range(BLOCK_ROWS // MERGE_SUB)]

    n_other = len(other_pieces)
    for k, piece in enumerate(other_pieces):
        for proj_piece in proj_pieces[k * len(proj_pieces) // n_other:(k + 1) * len(proj_pieces) // n_other]:
            proj_piece()
        piece()
    x1_ref[...] = jnp.concatenate([s[0] for s in subs], axis=0)
    hp_ref[...] = jnp.concatenate([s[1] for s in subs], axis=0)
    route_ref[...] = jnp.concatenate([s[2] for s in subs], axis=0)
    for s, sub in enumerate(subs):
        cnt_ref[s] = sub[3]

    tail_ref[...] = zc_ref[BLOCK_ROWS - WINDOW:, COL_SKV:COL_GATE]
    zc_ref[...] = zn_ref[...]


def _merge_rows(rows, yr_ref, ys_ref, z_ref, x_ref, wro_ref, wso_ref, wout_ref, fg_ref, wr_ref, br_ref):
    a = _dot(yr_ref[rows, :], wro_ref[...])
    b = _dot(ys_ref[rows, :], wso_ref[...])
    gate_r = z_ref[rows, COL_GATE:COL_GATE + D_MODEL].astype(F32)
    gate_s = z_ref[rows, COL_GATE + D_MODEL:].astype(F32)
    merged = (_sigmoid(gate_r) * a + _sigmoid(gate_s) * b).astype(BF16)
    x1 = x_ref[rows, :] + _dot(merged, wout_ref[...])
    h2 = _rms(x1, fg_ref[...])
    packed = _pack_halves(h2)

    logits = _dot(h2.astype(BF16), wr_ref[...]) + br_ref[...]
    lane = lax.broadcasted_iota(jnp.int32, logits.shape, 1)
    big = jnp.int32(ROUTE_LANES)
    is_group = jnp.logical_and(lane >= LANE_GROUP0, lane < LANE_GROUP0 + N_GROUPS)
    gl = jnp.where(is_group, logits, NEG_INF)
    gmax = jnp.max(gl, axis=-1, keepdims=True)
    g_w = 1.0 / jnp.sum(jnp.exp(gl - gmax), axis=-1, keepdims=True)
    g_sel = jnp.min(jnp.where(gl == gmax, lane, big), axis=-1, keepdims=True) - LANE_GROUP0
    in_group = jnp.logical_and(lane < N_EXPERTS, (lane >> 3) == g_sel)
    el = jnp.where(in_group, logits, NEG_INF)
    m1 = jnp.max(el, axis=-1, keepdims=True)
    i1 = jnp.min(jnp.where(el == m1, lane, big), axis=-1, keepdims=True)
    el2 = jnp.where(lane == i1, NEG_INF, el)
    m2 = jnp.max(el2, axis=-1, keepdims=True)
    i2 = jnp.min(jnp.where(el2 == m2, lane, big), axis=-1, keepdims=True)
    e2 = jnp.exp(m2 - m1)
    c1 = g_w / (1.0 + e2)
    c2 = g_w * e2 / (1.0 + e2)
    chosen = jnp.where(jnp.logical_or(lane == i1, lane == i2), 1.0, 0.0)
    route = (chosen
             + jnp.where(lane == LANE_E1, i1.astype(F32), 0.0)
             + jnp.where(lane == LANE_E2, i2.astype(F32), 0.0)
             + jnp.where(lane == LANE_C1, c1, 0.0)
             + jnp.where(lane == LANE_C2, c2, 0.0)
             + jnp.where(lane == i2 + LANE_SECOND0, 1.0, 0.0))
    return x1, packed, route, jnp.sum(chosen, axis=0, keepdims=True)


def _block(x2d, mix_g, w_in, gn_g, gn_b, sinks, q_g, k_g, w_ret_o, w_swa_o, w_out, ffn_g, w_router, b_router,
           seq):
    t = x2d.shape[0]
    n = t // BLOCK_ROWS
    assert seq % BLOCK_ROWS == 0
    tables = _retention_tables()
    swa_tables = _swa_tables(q_g, k_g)
    consts = tables + (gn_g, gn_b) + swa_tables
    merge_weights = (w_ret_o, w_swa_o, w_out)
    small = (ffn_g, w_router, b_router)
    hbm = pl.BlockSpec(memory_space=pl.ANY)
    full = lambda a: pl.BlockSpec(a.shape, lambda g: (0,) * a.ndim)
    cur = lambda w: pl.BlockSpec((BLOCK_ROWS, w), lambda g: (jnp.minimum(g, n - 1), 0))
    prv = lambda w: pl.BlockSpec((BLOCK_ROWS, w), lambda g: (jnp.maximum(g - 1, 0), 0))
    chunks = BLOCK_ROWS // RET_CHUNK
    subs_per_block = BLOCK_ROWS // MERGE_SUB
    return pl.pallas_call(
        functools.partial(_block_kernel, seq // BLOCK_ROWS),
        grid=(n + 1,),
        in_specs=([pl.BlockSpec(memory_space=pltpu.SMEM), cur(D_MODEL), prv(D_MODEL), full(mix_g), hbm]
                  + [full(a) for a in consts] + [hbm] * len(merge_weights) + [full(a) for a in small]),
        out_specs=[prv(D_MODEL), prv(HALF_D), prv(ROUTE_LANES),
                   pl.BlockSpec((subs_per_block, 1, ROUTE_LANES), lambda g: (jnp.maximum(g - 1, 0), 0, 0))],
        out_shape=[jax.ShapeDtypeStruct((t, D_MODEL), F32),
                   jax.ShapeDtypeStruct((t, HALF_D), I32),
                   jax.ShapeDtypeStruct((t, ROUTE_LANES), F32),
                   jax.ShapeDtypeStruct((n * subs_per_block, 1, ROUTE_LANES), F32)],
        scratch_shapes=[pltpu.VMEM(w_in.shape, BF16)] + [pltpu.VMEM(w.shape, BF16) for w in merge_weights]
                       + [pltpu.VMEM((2, D_MODEL, W_SLAB), F32), pltpu.SemaphoreType.DMA((2,)),
                        pltpu.VMEM((BLOCK_ROWS, IN_WIDTH), BF16), pltpu.VMEM((BLOCK_ROWS, IN_WIDTH), BF16),
                        pltpu.VMEM((BLOCK_ROWS, RET_W), BF16), pltpu.VMEM((BLOCK_ROWS, SWA_Q), BF16),
                        pltpu.VMEM((RET_HEADS, RET_DK, RET_DV), F32),
                        pltpu.VMEM((chunks, RET_HEADS, RET_DK, RET_DV), BF16),
                        pltpu.VMEM((BLOCK_ROWS + WINDOW, Z_SKV_W), BF16), pltpu.VMEM((WINDOW, Z_SKV_W), BF16)],
        compiler_params=pltpu.CompilerParams(dimension_semantics=("arbitrary",),
                                             vmem_limit_bytes=VMEM_LIMIT),
        name="block",
    )(sinks, x2d, x2d, mix_g, w_in, *consts, *merge_weights, *small)


def _plan_kernel(cnt_ref, route_ref, base_ref, meta_ref, pos_ref):
    counts = cnt_ref[...]
    nt = counts.shape[0]
    exact = functools.partial(jnp.dot, precision=lax.Precision.HIGHEST, preferred_element_type=F32)
    total = jnp.sum(counts, axis=0, keepdims=True)
    tiles_e = jnp.ceil(total * (1.0 / ROW_TILE))
    k = lax.broadcasted_iota(jnp.int32, (ROUTE_LANES, ROUTE_LANES), 0)
    lane = lax.broadcasted_iota(jnp.int32, (ROUTE_LANES, ROUTE_LANES), 1)
    tile_end = exact(jnp.broadcast_to(tiles_e, (8, ROUTE_LANES)), jnp.where(k <= lane, 1.0, 0.0))[0:1]
    tile_start = tile_end - tiles_e
    r = lax.broadcasted_iota(jnp.int32, (nt, nt), 0)
    c = lax.broadcasted_iota(jnp.int32, (nt, nt), 1)
    before = exact(jnp.where(c < r, 1.0, 0.0), counts)
    base_ref[...] = tile_start * ROW_TILE + before

    used = jnp.max(tile_end, axis=-1, keepdims=True)
    tile = k.astype(F32)
    clamped = jnp.minimum(tile, jnp.maximum(used - 1.0, 0.0))
    is_expert = lane < N_EXPERTS
    owner = jnp.sum(jnp.where(jnp.logical_and(is_expert, tile_end <= clamped), 1.0, 0.0), axis=-1, keepdims=True)
    owner = jnp.minimum(owner, N_EXPERTS - 1.0)
    mine = lane.astype(F32) == owner
    pick = lambda v: jnp.sum(jnp.where(mine, v, 0.0), axis=-1, keepdims=True)
    t1 = tile[:, 0:1]
    valid = jnp.clip(pick(total) - (t1 - pick(tile_start)) * ROW_TILE, 0.0, float(ROW_TILE))
    valid = jnp.where(t1 < used, valid, 0.0)

    nonempty = jnp.where(jnp.logical_and(lane[0:1] < N_EXPERTS, tiles_e > 0.0), 1.0, 0.0)
    rank = exact(jnp.broadcast_to(nonempty, (8, ROUTE_LANES)), jnp.where(k < lane, 1.0, 0.0))[0:1]
    nonempty_col = jnp.sum(jnp.where(k == lane, jnp.broadcast_to(nonempty, k.shape), 0.0), axis=-1, keepdims=True)
    later = jnp.where(jnp.logical_and(k > lane, nonempty_col > 0.0), tile, float(ROUTE_LANES))
    nxt_e = jnp.min(later, axis=0, keepdims=True)
    nxt_e = jnp.where(nxt_e < float(ROUTE_LANES), nxt_e, -1.0)
    first = jnp.where(jnp.logical_and(t1 == pick(tile_start), t1 < used), 1.0, 0.0)
    my_rank = pick(rank)
    slot = my_rank - 2.0 * jnp.floor(my_rank * 0.5)
    fields = (owner, valid, first, pick(nxt_e), slot, jnp.broadcast_to(used, owner.shape))
    by_tile = sum(jnp.where(lane == n, f, 0.0) for n, f in enumerate(fields))
    meta_ref[...] = by_tile.T[:8, :].astype(I32)

    sel_row = lax.broadcasted_iota(jnp.int32, (8, ROUTE_LANES), 0)
    sel_lane = lax.broadcasted_iota(jnp.int32, (8, ROUTE_LANES), 1)
    tok_r = lax.broadcasted_iota(jnp.int32, (MERGE_SUB, MERGE_SUB), 0)
    tok_c = lax.broadcasted_iota(jnp.int32, (MERGE_SUB, MERGE_SUB), 1)
    eye = jnp.where(tok_r == tok_c, 1.0, 0.0).astype(BF16)
    upto = jnp.where(tok_r <= tok_c, 1.0, 0.0).astype(BF16)
    out_row = lax.broadcasted_iota(jnp.int32, (8, MERGE_SUB), 0)
    sum_experts = lambda a: jnp.sum(a, axis=0, keepdims=True)

    def sub_positions(s, carry):
        route = route_ref[pl.ds(pl.multiple_of(s * MERGE_SUB, MERGE_SUB), MERGE_SUB), :].astype(BF16)
        record_t = _dot_tn(route, eye)
        incl_t = _dot_tn(route, upto)[:N_EXPERTS]
        both_t = record_t[:N_EXPERTS]
        second_t = record_t[LANE_SECOND0:LANE_SECOND0 + N_EXPERTS]
        rank2 = sum_experts(second_t * incl_t) - 1.0
        rank1 = sum_experts(both_t * incl_t) - rank2 - 2.0
        base_row = jnp.where(sel_lane < N_EXPERTS, jnp.broadcast_to(base_ref[pl.ds(s, 1), :], (8, ROUTE_LANES)), 0.0)
        hi = jnp.floor(base_row * (1.0 / 256.0))
        lo = base_row - 256.0 * hi
        hi2 = pltpu.roll(hi, LANE_SECOND0, axis=1)
        lo2 = pltpu.roll(lo, LANE_SECOND0, axis=1)
        table = (jnp.where(sel_row == 0, hi, 0.0) + jnp.where(sel_row == 1, lo, 0.0)
                 + jnp.where(sel_row == 2, hi2, 0.0) + jnp.where(sel_row == 3, lo2, 0.0))
        r = _dot_nt(table.astype(BF16), route)
        base_both = 256.0 * r[0:1] + r[1:2]
        base2 = 256.0 * r[2:3] + r[3:4]
        out = jnp.where(out_row == 0, base_both - base2 + rank1, jnp.where(out_row == 1, base2 + rank2, 0.0))
        pos_ref[s] = out.astype(I32)
        return carry

    lax.fori_loop(0, nt, sub_positions, 0, unroll=4)


def _tile_plan(cnt, route, n_tiles):
    assert n_tiles <= ROUTE_LANES
    nt = cnt.shape[0]
    _, meta, pos = pl.pallas_call(
        _plan_kernel,
        out_shape=[jax.ShapeDtypeStruct((nt, ROUTE_LANES), F32), jax.ShapeDtypeStruct((8, ROUTE_LANES), I32),
                   jax.ShapeDtypeStruct((nt, 8, MERGE_SUB), I32)],
        compiler_params=pltpu.CompilerParams(vmem_limit_bytes=VMEM_LIMIT),
        name="tile_plan",
    )(cnt.reshape(nt, ROUTE_LANES), route)
    return meta, pos[:, 0, :], pos[:, 1, :]


def _sc_mesh():
    return plsc.VectorSubcoreMesh(core_axis_name="c", subcore_axis_name="s")


def _sc_worker():
    return lax.axis_index("s") * SC_CORES + lax.axis_index("c")


def _dispatch(hp, pos1, pos2, n_rows):
    t = hp.shape[0]
    per_worker = t // SC_WORKERS
    k = per_worker // SC_CHUNK
    idx = pltpu.VMEM((k, SC_CHUNK), I32)

    @functools.partial(pl.kernel, mesh=_sc_mesh(), out_type=jax.ShapeDtypeStruct((n_rows, HALF_D), I32),
                       scratch_types=[idx, idx, pltpu.VMEM((SC_CHUNK, HALF_D), I32)], name="moe_dispatch")
    def run(h_hbm, p1_hbm, p2_hbm, xs_hbm, p1_v, p2_v, rows_v):
        wid = _sc_worker()
        pltpu.sync_copy(p1_hbm.at[wid], p1_v)
        pltpu.sync_copy(p2_hbm.at[wid], p2_v)
        for j in range(k):
            pltpu.sync_copy(h_hbm.at[pl.ds(wid * per_worker + j * SC_CHUNK, SC_CHUNK)], rows_v)
            pltpu.sync_copy(rows_v, xs_hbm.at[p1_v.at[j]])
            pltpu.sync_copy(rows_v, xs_hbm.at[p2_v.at[j]])

    return run(hp, pos1.reshape(SC_WORKERS, k, SC_CHUNK), pos2.reshape(SC_WORKERS, k, SC_CHUNK))


def _combine_gather(ys, pos1, pos2):
    t = pos1.size
    per_worker = t // SC_WORKERS
    k = per_worker // SC_CHUNK
    idx = pltpu.VMEM((k, SC_CHUNK), I32)
    out = jax.ShapeDtypeStruct((t, HALF_D), I32)

    @functools.partial(pl.kernel, mesh=_sc_mesh(), out_type=(out, out),
                       scratch_types=[idx, idx, pltpu.VMEM((SC_CHUNK, HALF_D), I32)], name="moe_combine")
    def run(ys_hbm, p1_hbm, p2_hbm, g1_hbm, g2_hbm, p1_v, p2_v, rows_v):
        wid = _sc_worker()
        pltpu.sync_copy(p1_hbm.at[wid], p1_v)
        pltpu.sync_copy(p2_hbm.at[wid], p2_v)
        for j in range(k):
            dst = pl.ds(wid * per_worker + j * SC_CHUNK, SC_CHUNK)
            pltpu.sync_copy(ys_hbm.at[p1_v.at[j]], rows_v)
            pltpu.sync_copy(rows_v, g1_hbm.at[dst])
            pltpu.sync_copy(ys_hbm.at[p2_v.at[j]], rows_v)
            pltpu.sync_copy(rows_v, g2_hbm.at[dst])

    return run(ys, pos1.reshape(SC_WORKERS, k, SC_CHUNK), pos2.reshape(SC_WORKERS, k, SC_CHUNK))


META_OWNER, META_VALID, META_FIRST, META_NEXT, META_SLOT, META_USED = range(6)


def _experts_kernel(meta_ref, xs_hbm, wg_hbm, wu_hbm, wd_hbm, ys_ref,
                    xs_buf, xs_sem, wg_f, wu_f, wd_f, wg_bf, wu_bf, wd_bf, w_sem):
    i = pl.program_id(0)
    used = meta_ref[META_USED, 0]
    valid = meta_ref[META_VALID, i]
    owner = meta_ref[META_OWNER, i]
    nxt = meta_ref[META_NEXT, i]

    def rows_in(tile):
        start = pl.multiple_of(tile * ROW_TILE, ROW_TILE)
        slot = lax.rem(tile, XS_SLOTS)
        return pltpu.make_async_copy(xs_hbm.at[pl.ds(start, ROW_TILE)], xs_buf.at[slot], xs_sem.at[slot])

    @pl.when(i == 0)
    def _():
        for ahead in range(XS_SLOTS - 1):
            pl.when(ahead < used)(rows_in(ahead).start)

    @pl.when(i + (XS_SLOTS - 1) < used)
    def _():
        rows_in(i + (XS_SLOTS - 1)).start()

    def fetch(expert, slot):
        pairs = ((wg_hbm, wg_f), (wu_hbm, wu_f), (wd_hbm, wd_f))
        return [pltpu.make_async_copy(w.at[expert], buf.at[slot], w_sem.at[slot, n])
                for n, (w, buf) in enumerate(pairs)]

    @pl.when(i == 0)
    def _():
        for copy in fetch(owner, 0):
            copy.start()

    @pl.when(meta_ref[META_FIRST, i] == 1)
    def _():
        slot = meta_ref[META_SLOT, i]
        for copy in fetch(owner, slot):
            copy.wait()
        wg_bf[...] = wg_f[slot].astype(BF16)
        wu_bf[...] = wu_f[slot].astype(BF16)
        wd_bf[...] = wd_f[slot].astype(BF16)

        @pl.when(nxt >= 0)
        def _():
            for copy in fetch(nxt, 1 - slot):
                copy.start()

    @pl.when(i < used)
    def _():
        rows_in(i).wait()
        row = lax.broadcasted_iota(jnp.int32, (ROW_TILE, HALF_D), 0)
        words = xs_buf[lax.rem(i, XS_SLOTS)]
        packed = jnp.where(row < valid, words, 0)
        hi, lo = _unpack_halves(packed)
        hi = hi.astype(BF16)
        lo = lo.astype(BF16)
        hg = _dot(hi, wg_bf[:HALF_D, :]) + _dot(lo, wg_bf[HALF_D:, :])
        hu = _dot(hi, wu_bf[:HALF_D, :]) + _dot(lo, wu_bf[HALF_D:, :])
        hid = (hg * _sigmoid(hg) * hu).astype(BF16)
        ys_ref[...] = _pack_halves(_dot(hid, wd_bf[...]))


def _experts(xs, meta, n_tiles, wg, wu, wd):
    tile = lambda i, meta: (jnp.minimum(i, meta[META_USED, 0] - 1), 0)
    hbm = pl.BlockSpec(memory_space=pl.ANY)
    up_shape, down_shape = (D_MODEL, EXPERT_FF), (EXPERT_FF, D_MODEL)
    grid_spec = pltpu.PrefetchScalarGridSpec(
        num_scalar_prefetch=1,
        grid=(n_tiles,),
        in_specs=[hbm, hbm, hbm, hbm],
        out_specs=pl.BlockSpec((ROW_TILE, HALF_D), tile),
        scratch_shapes=[pltpu.VMEM((XS_SLOTS, ROW_TILE, HALF_D), I32), pltpu.SemaphoreType.DMA((XS_SLOTS,)),
                        pltpu.VMEM((2,) + up_shape, F32), pltpu.VMEM((2,) + up_shape, F32),
                        pltpu.VMEM((2,) + down_shape, F32),
                        pltpu.VMEM(up_shape, BF16), pltpu.VMEM(up_shape, BF16), pltpu.VMEM(down_shape, BF16),
                        pltpu.SemaphoreType.DMA((2, 3))],
    )
    return pl.pallas_call(
        _experts_kernel,
        grid_spec=grid_spec,
        out_shape=jax.ShapeDtypeStruct((n_tiles * ROW_TILE, HALF_D), I32),
        compiler_params=pltpu.CompilerParams(dimension_semantics=("arbitrary",),
                                             vmem_limit_bytes=VMEM_LIMIT),
        name="experts",
    )(meta, xs, wg, wu, wd)


def _ple_kernel(x1_hbm, g1_ref, g2_ref, route_ref, p_ref, wp_f32, pg_ref, gg_ref, wgate_f32, o_ref,
                wp_ref, wgate_ref, x1_buf, x1_sem):
    i = pl.program_id(0)
    steps = pl.num_programs(0)

    def x1_in(step):
        start = pl.multiple_of(step * PLE_TM, PLE_TM)
        slot = lax.rem(step, PLE_SLOTS)
        return pltpu.make_async_copy(x1_hbm.at[pl.ds(start, PLE_TM)], x1_buf.at[slot], x1_sem.at[slot])

    @pl.when(i == 0)
    def _():
        for ahead in range(PLE_SLOTS - 1):
            pl.when(ahead < steps)(x1_in(ahead).start)
        wp_ref[...] = wp_f32[...].astype(BF16)
        wgate_ref[...] = wgate_f32[...].astype(BF16)

    @pl.when(i + (PLE_SLOTS - 1) < steps)
    def _():
        x1_in(i + (PLE_SLOTS - 1)).start()

    x1_in(i).wait()
    x1_ref = x1_buf.at[lax.rem(i, PLE_SLOTS)]

    route = route_ref[...]
    lane = lax.broadcasted_iota(jnp.int32, route.shape, 1)
    pick = lambda ln: jnp.sum(jnp.where(lane == ln, route, 0.0), axis=-1, keepdims=True)
    c1 = pick(LANE_C1)
    c2 = pick(LANE_C2)
    a_hi, a_lo = _unpack_halves(g1_ref[...])
    b_hi, b_lo = _unpack_halves(g2_ref[...])
    x = x1_ref[...] + jnp.concatenate([c1 * a_hi + c2 * b_hi, c1 * a_lo + c2 * b_lo], axis=1)
    ple = _rms(_dot(p_ref[...].astype(BF16), wp_ref[...]), pg_ref[...])
    gate = _sigmoid(_dot(_rms(x, gg_ref[...]).astype(BF16), wgate_ref[...]))
    o_ref[...] = x + gate * ple


def _ple(x1, g1, g2, route, p2d, w_ple, ple_g, gate_g, w_gate):
    t = x1.shape[0]
    row = lambda w: pl.BlockSpec((PLE_TM, w), lambda i: (i, 0))
    full = lambda a: pl.BlockSpec(a.shape, lambda i: (0,) * a.ndim)
    return pl.pallas_call(
        _ple_kernel,
        grid=(t // PLE_TM,),
        in_specs=[pl.BlockSpec(memory_space=pl.ANY), row(HALF_D), row(HALF_D), row(ROUTE_LANES), row(PLE_DIM),
                  full(w_ple), full(ple_g), full(gate_g), full(w_gate)],
        out_specs=row(D_MODEL),
        out_shape=jax.ShapeDtypeStruct((t, D_MODEL), F32),
        scratch_shapes=[pltpu.VMEM(w_ple.shape, BF16), pltpu.VMEM(w_gate.shape, BF16),
                        pltpu.VMEM((PLE_SLOTS, PLE_TM, D_MODEL), F32), pltpu.SemaphoreType.DMA((PLE_SLOTS,))],
        compiler_params=pltpu.CompilerParams(dimension_semantics=("arbitrary",),
                                             vmem_limit_bytes=VMEM_LIMIT),
        name="ple",
    )(x1, g1, g2, route, p2d, w_ple, ple_g, gate_g, w_gate)


def kernel(x, p, mix_norm_g, w_in, ret_gn_g, ret_gn_b, w_ret_o, q_norm_g, k_norm_g, attn_sinks,
           w_swa_o, w_out, ffn_norm_g, w_router_group, b_router_group, w_router_expert,
           b_router_expert, w_exp_gate, w_exp_up, w_exp_down, ple_gate_norm_g, w_ple_gate,
           w_ple, ple_norm_g):
    batch, seq, d = x.shape
    t = batch * seq
    depth = w_in.shape[0]
    n_tiles = 2 * t // ROW_TILE + N_EXPERTS
    row = lambda a: a.reshape(1, -1)
    x2d = x.reshape(t, d)
    for i in range(depth):
        pad = ROUTE_LANES - N_GROUPS - N_EXPERTS
        w_router = jnp.pad(jnp.concatenate([w_router_expert[i], w_router_group[i]], axis=1), ((0, 0), (0, pad)))
        b_router = jnp.pad(jnp.concatenate([b_router_expert[i], b_router_group[i]]), (0, pad)).reshape(1, -1)
        w_router = w_router.astype(BF16)
        x1, hp, route, cnt = _block(x2d, row(mix_norm_g[i]), w_in[i], row(ret_gn_g[i]),
                                    row(ret_gn_b[i]), attn_sinks[i], q_norm_g[i], k_norm_g[i],
                                    w_ret_o[i], w_swa_o[i], w_out[i],
                                    row(ffn_norm_g[i]), w_router, b_router, seq)
        meta, pos1, pos2 = _tile_plan(cnt, route, n_tiles)
        xs = _dispatch(hp, pos1, pos2, n_tiles * ROW_TILE)
        ys = _experts(xs, meta, n_tiles, w_exp_gate[i], w_exp_up[i], w_exp_down[i])
        g1, g2 = _combine_gather(ys, pos1, pos2)
        x2d = _ple(x1, g1, g2, route, p[i].reshape(t, PLE_DIM), w_ple[i], row(ple_norm_g[i]),
                   row(ple_gate_norm_g[i]), w_ple_gate[i])
    return x2d.reshape(batch, seq, d)
```

```python
import functools

import jax
import jax.numpy as jnp
import numpy as np
from jax import lax
from jax.experimental import pallas as pl
from jax.experimental.pallas import tpu as pltpu
from jax.experimental.pallas import tpu_sc as plsc

F32 = jnp.float32
BF16 = jnp.bfloat16
I32 = jnp.int32

EPS = 1e-6
D_MODEL = 1024
PLE_DIM = 256
RET_HEADS = 4
RET_DK = 128
RET_DV = 128
RET_CHUNK = 128
SWA_HEADS = 8
SWA_KV_HEADS = 2
SWA_GROUP = SWA_HEADS // SWA_KV_HEADS
SWA_HD = 64
WINDOW = 128
N_GROUPS = 4
EXPERTS_PER_GROUP = 8
N_EXPERTS = N_GROUPS * EXPERTS_PER_GROUP
EXPERT_FF = 256

RET_W = RET_HEADS * RET_DK
SWA_Q = SWA_HEADS * SWA_HD
SWA_KV = SWA_KV_HEADS * SWA_HD
Z_RET_W = 4 * RET_W
Z_SKV_W = 2 * SWA_KV
Z_GATE_W = 2 * D_MODEL
COL_SQ = Z_RET_W
COL_SKV = COL_SQ + SWA_Q
COL_GATE = COL_SKV + Z_SKV_W
IN_WIDTH = COL_GATE + Z_GATE_W

ROUTE_LANES = 128
LANE_GROUP0 = N_EXPERTS
LANE_E1, LANE_E2, LANE_C1, LANE_C2 = 32, 33, 34, 35
LANE_SECOND0 = 64
NEG_INF = -1e30

HALF_D = D_MODEL // 2
ROW_TILE = 512
XS_SLOTS = 3
SC_CORES = 2
SC_SUBCORES = 16
SC_WORKERS = SC_CORES * SC_SUBCORES
SC_CHUNK = 128

VMEM_LIMIT = 56 * 1024 * 1024

BLOCK_ROWS = 512
W_SLAB = 256
RET_ROWS = 512
MERGE_SUB = 256
PLE_TM = 1024
PLE_SLOTS = 3


def _rms(x, g):
    ms = jnp.mean(x * x, axis=-1, keepdims=True)
    return x * lax.rsqrt(ms + EPS) * g


def _sigmoid(x):
    return 1.0 / (1.0 + jnp.exp(-x))


def _dot(a, b):
    return jnp.dot(a, b, preferred_element_type=F32)


def _dot_nt(a, b):
    return lax.dot_general(a, b, (((1,), (1,)), ((), ())), preferred_element_type=F32)


def _dot_tn(a, b):
    return lax.dot_general(a, b, (((0,), (0,)), ((), ())), preferred_element_type=F32)


def _pack_halves(v):
    return pltpu.pack_elementwise([v[:, :HALF_D], v[:, HALF_D:]], packed_dtype=BF16)


def _unpack_halves(p):
    words = lax.bitcast_convert_type(p, jnp.uint32)
    return tuple(pltpu.unpack_elementwise(words, index=k, packed_dtype=BF16, unpacked_dtype=F32) for k in range(2))


def _retention_tables():
    f32 = np.float32
    h = RET_HEADS
    c = RET_CHUNK
    log_gamma = np.log1p(-np.exp2(f32(-5.0) - np.arange(h, dtype=f32))).astype(f32)
    pos = np.arange(c, dtype=f32)
    diff = pos[:, None] - pos[None, :]
    decay = np.where(diff[None] >= 0.0, np.exp(np.maximum(diff, f32(0.0))[None] * log_gamma[:, None, None]), f32(0.0))
    scale = f32(RET_DK ** -0.5)
    dmask = decay * scale
    zeta = np.exp((f32(c - 1.0) - pos)[None, :] * log_gamma[:, None]) * scale
    xi = np.exp((pos + f32(1.0))[None, :] * log_gamma[:, None])
    cdec = np.exp(f32(c) * log_gamma)
    bc = lambda a: np.ascontiguousarray(np.broadcast_to(a[:, :, None], (h, c, c)), dtype=f32)
    return (dmask.astype(f32), bc(zeta), bc(xi),
            np.ascontiguousarray(np.broadcast_to(cdec[:, None, None], (h, c, c)), dtype=f32))


def _swa_tables(q_g, k_g):
    f32 = np.float32
    qi = np.arange(WINDOW)[:, None]
    sj = np.arange(2 * WINDOW)[None, :]
    rel = qi + WINDOW - sj
    ok = (rel >= 0) & (rel < WINDOW)
    slopes = np.exp2(f32(-8.0) * np.arange(1, SWA_HEADS + 1, dtype=f32) / f32(SWA_HEADS)).astype(f32)
    bias = np.where(ok[None], -slopes[:, None, None] * rel.astype(f32)[None], f32(NEG_INF)).astype(f32)
    bias = bias.reshape(SWA_KV_HEADS, SWA_GROUP * WINDOW, 2 * WINDOW)
    avg = lambda heads: jnp.asarray(np.kron(np.eye(heads, dtype=f32), np.full((SWA_HD, SWA_HD), 1.0 / SWA_HD, f32)),
                                    dtype=BF16)
    qg = jnp.tile(q_g.reshape(1, SWA_HD), (1, SWA_HEADS)) * (SWA_HD ** -0.5)
    kg = jnp.tile(k_g.reshape(1, SWA_HD), (1, SWA_KV_HEADS))
    return qg, kg, avg(SWA_HEADS), avg(SWA_KV_HEADS), jnp.asarray(bias)


def _block_kernel(per_seq, sinks_ref, x_ref, xp_ref, g_ref, w_hbm, dmask_ref, zeta_ref, xi_ref, cdec_ref,
                  gng_ref, gnb_ref, qg_ref, kg_ref, bq_ref, bk_ref, bias_ref,
                  wro_hbm, wso_hbm, wout_hbm, fg_ref, wr_ref, br_ref,
                  x1_ref, hp_ref, route_ref, cnt_ref,
                  w_ref, wro_ref, wso_ref, wout_ref, wtmp_ref, w_sem,
                  zc_ref, zn_ref, yr_ref, ys_ref, state_ref, prev_ref, band_ref, tail_ref):
    g = pl.program_id(0)
    seq_start = lax.rem(g - 1, per_seq) == 0

    @pl.when(g == 0)
    def _():
        slabs = [(src, dst, c0) for src, dst in ((w_hbm, w_ref), (wro_hbm, wro_ref), (wso_hbm, wso_ref),
                                                 (wout_hbm, wout_ref))
                 for c0 in range(0, dst.shape[1], W_SLAB)]

        def slab(k):
            src, dst, c0 = slabs[k]
            return pltpu.make_async_copy(src.at[:, pl.ds(c0, W_SLAB)], wtmp_ref.at[k % 2, pl.ds(0, dst.shape[0])],
                                         w_sem.at[k % 2])

        slab(0).start()
        for k, (_, dst, c0) in enumerate(slabs):
            if k + 1 < len(slabs):
                slab(k + 1).start()
            slab(k).wait()
            dst[:, c0:c0 + W_SLAB] = wtmp_ref[k % 2, :dst.shape[0]].astype(BF16)
        zc_ref[...] = jnp.zeros_like(zc_ref)
        tail_ref[...] = jnp.zeros_like(tail_ref)
        state_ref[...] = jnp.zeros_like(state_ref)

    @pl.when(seq_start)
    def _():
        state_ref[...] = jnp.zeros_like(state_ref)

    band_ref[:WINDOW, :] = tail_ref[...]
    band_ref[WINDOW:, :] = zc_ref[:, COL_SKV:COL_GATE]

    h = _rms(x_ref[...], g_ref[...]).astype(BF16)

    def project(c0, width):
        zn_ref[:, c0:c0 + width] = _dot(h, w_ref[:, c0:c0 + width]).astype(BF16)

    cw = 512
    proj_pieces = [functools.partial(project, j * cw, cw) for j in range(COL_SKV // cw)]
    proj_pieces.append(functools.partial(project, COL_SKV, Z_SKV_W))
    proj_pieces += [functools.partial(project, COL_GATE + j * cw, cw) for j in range(Z_GATE_W // cw)]

    def retention_rows(t):
        r_base = t * RET_ROWS
        n_chunks = RET_ROWS // RET_CHUNK
        part = lambda c, p, hd: zc_ref[r_base + c * RET_CHUNK:r_base + (c + 1) * RET_CHUNK,
                                       p * RET_W + hd * RET_DK:p * RET_W + (hd + 1) * RET_DK]
        for hd in range(RET_HEADS):
            state = state_ref[hd]
            for c in range(n_chunks):
                prev_ref[t * n_chunks + c, hd] = state.astype(BF16)
                kz = (part(c, 1, hd).astype(F32) * zeta_ref[hd]).astype(BF16)
                state = cdec_ref[hd] * state + _dot_tn(kz, part(c, 2, hd))
            state_ref[hd] = state
        out_rows = []
        for c in range(n_chunks):
            heads = []
            for hd in range(RET_HEADS):
                q = part(c, 0, hd)
                scores = _dot_nt(q, part(c, 1, hd)) * dmask_ref[hd]
                y = _dot(scores.astype(BF16), part(c, 2, hd)) + _dot(q, prev_ref[t * n_chunks + c, hd]) * xi_ref[hd]
                mu = jnp.mean(y, axis=-1, keepdims=True)
                d = y - mu
                var = jnp.mean(d * d, axis=-1, keepdims=True)
                hs = slice(hd * RET_DV, (hd + 1) * RET_DV)
                yn = d * lax.rsqrt(var + EPS) * gng_ref[:, hs] + gnb_ref[:, hs]
                gate = part(c, 3, hd).astype(F32)
                heads.append((gate * _sigmoid(gate) * yn).astype(BF16))
            out_rows.append(jnp.concatenate(heads, axis=1))
        yr_ref[r_base:r_base + RET_ROWS, :] = jnp.concatenate(out_rows, axis=0)

    group_rows = SWA_GROUP * WINDOW
    lo_q = lax.broadcasted_iota(jnp.int32, (WINDOW, 2 * SWA_HD), 1) < SWA_HD
    lo_k = lax.broadcasted_iota(jnp.int32, (2 * WINDOW, 2 * SWA_HD), 1) < SWA_HD
    in_prev = lax.broadcasted_iota(jnp.int32, (group_rows, 2 * WINDOW), 1) < WINDOW
    row_head = lax.broadcasted_iota(jnp.int32, (group_rows, 1), 0) // WINDOW

    def both_heads(v):
        swapped = pltpu.roll(v, SWA_HD, axis=1)
        return jnp.where(lo_k, v, swapped), jnp.where(lo_k, swapped, v)

    def swa_block(n):
        rows = slice(n * WINDOW, (n + 1) * WINDOW)
        kv = band_ref[n * WINDOW:(n + 2) * WINDOW, :]
        kf = kv[:, :SWA_KV].astype(F32)
        kn = kf * lax.rsqrt(_dot((kf * kf).astype(BF16), bk_ref[...]) + EPS) * kg_ref[...]
        keys = [a.astype(BF16) for a in both_heads(kn)]
        vals = [a.astype(BF16) for a in both_heads(kv[:, SWA_KV:].astype(F32))]
        qf = zc_ref[rows, COL_SQ:COL_SKV].astype(F32)
        qn = qf * lax.rsqrt(_dot((qf * qf).astype(BF16), bq_ref[...]) + EPS) * qg_ref[...]
        outs = []
        for kh in range(SWA_KV_HEADS):
            parts = []
            for j in range(SWA_GROUP // 2):
                c0 = (kh * SWA_GROUP + 2 * j) * SWA_HD
                two = qn[:, c0:c0 + 2 * SWA_HD]
                parts += [jnp.where(lo_q, two, 0.0), jnp.where(lo_q, 0.0, two)]
            qs = jnp.concatenate(parts, axis=0).astype(BF16)
            s = _dot_nt(qs, keys[kh]) + bias_ref[kh]
            if n == 0:
                s = jnp.where(jnp.logical_and(seq_start, in_prev), NEG_INF, s)
            sink = sinks_ref[kh * SWA_GROUP]
            for gi in range(1, SWA_GROUP):
                sink = jnp.where(row_head == gi, sinks_ref[kh * SWA_GROUP + gi], sink)
            m = jnp.maximum(jnp.max(s, axis=-1, keepdims=True), sink)
            p = jnp.exp(s - m)
            denom = jnp.sum(p, axis=-1, keepdims=True) + jnp.exp(sink - m)
            o = _dot(p.astype(BF16), vals[kh]) * (1.0 / denom)
            for j in range(SWA_GROUP // 2):
                even = o[(2 * j) * WINDOW:(2 * j + 1) * WINDOW]
                odd = o[(2 * j + 1) * WINDOW:(2 * j + 2) * WINDOW]
                outs.append(jnp.where(lo_q, even, odd).astype(BF16))
        ys_ref[rows, :] = jnp.concatenate(outs, axis=1)

    mix_pieces = [functools.partial(retention_rows, t) for t in range(BLOCK_ROWS // RET_ROWS)]
    mix_pieces += [functools.partial(swa_block, n) for n in range(BLOCK_ROWS // WINDOW)]

    subs = []

    def merge_sub(s):
        subs.append(_merge_rows(slice(s * MERGE_SUB, (s + 1) * MERGE_SUB), yr_ref, ys_ref, zc_ref, xp_ref, wro_ref,
                                wso_ref, wout_ref, fg_ref, wr_ref, br_ref))

    other_pieces = mix_pieces + [functools.partial(merge_sub, s) for s in range(BLOCK_ROWS // MERGE_SUB)]

    n_other = len(other_pieces)
    for k, piece in enumerate(other_pieces):
        for proj_piece in proj_pieces[k * len(proj_pieces) // n_other:(k + 1) * len(proj_pieces) // n_other]:
            proj_piece()
        piece()
    x1_ref[...] = jnp.concatenate([s[0] for s in subs], axis=0)
    hp_ref[...] = jnp.concatenate([s[1] for s in subs], axis=0)
    route_ref[...] = jnp.concatenate([s[2] for s in subs], axis=0)
    for s, sub in enumerate(subs):
        cnt_ref[s] = sub[3]

    tail_ref[...] = zc_ref[BLOCK_ROWS - WINDOW:, COL_SKV:COL_GATE]
    zc_ref[...] = zn_ref[...]


def _merge_rows(rows, yr_ref, ys_ref, z_ref, x_ref, wro_ref, wso_ref, wout_ref, fg_ref, wr_ref, br_ref):
    a = _dot(yr_ref[rows, :], wro_ref[...])
    b = _dot(ys_ref[rows, :], wso_ref[...])
    gate_r = z_ref[rows, COL_GATE:COL_GATE + D_MODEL].astype(F32)
    gate_s = z_ref[rows, COL_GATE + D_MODEL:].astype(F32)
    merged = (_sigmoid(gate_r) * a + _sigmoid(gate_s) * b).astype(BF16)
    x1 = x_ref[rows, :] + _dot(merged, wout_ref[...])
    h2 = _rms(x1, fg_ref[...])
    packed = _pack_halves(h2)

    logits = _dot(h2.astype(BF16), wr_ref[...]) + br_ref[...]
    lane = lax.broadcasted_iota(jnp.int32, logits.shape, 1)
    big = jnp.int32(ROUTE_LANES)
    is_group = jnp.logical_and(lane >= LANE_GROUP0, lane < LANE_GROUP0 + N_GROUPS)
    gl = jnp.where(is_group, logits, NEG_INF)
    gmax = jnp.max(gl, axis=-1, keepdims=True)
    g_w = 1.0 / jnp.sum(jnp.exp(gl - gmax), axis=-1, keepdims=True)
    g_sel = jnp.min(jnp.where(gl == gmax, lane, big), axis=-1, keepdims=True) - LANE_GROUP0
    in_group = jnp.logical_and(lane < N_EXPERTS, (lane >> 3) == g_sel)
    el = jnp.where(in_group, logits, NEG_INF)
    m1 = jnp.max(el, axis=-1, keepdims=True)
    i1 = jnp.min(jnp.where(el == m1, lane, big), axis=-1, keepdims=True)
    el2 = jnp.where(lane == i1, NEG_INF, el)
    m2 = jnp.max(el2, axis=-1, keepdims=True)
    i2 = jnp.min(jnp.where(el2 == m2, lane, big), axis=-1, keepdims=True)
    e2 = jnp.exp(m2 - m1)
    c1 = g_w / (1.0 + e2)
    c2 = g_w * e2 / (1.0 + e2)
    chosen = jnp.where(jnp.logical_or(lane == i1, lane == i2), 1.0, 0.0)
    route = (chosen
             + jnp.where(lane == LANE_E1, i1.astype(F32), 0.0)
             + jnp.where(lane == LANE_E2, i2.astype(F32), 0.0)
             + jnp.where(lane == LANE_C1, c1, 0.0)
             + jnp.where(lane == LANE_C2, c2, 0.0)
             + jnp.where(lane == i2 + LANE_SECOND0, 1.0, 0.0))
    return x1, packed, route, jnp.sum(chosen, axis=0, keepdims=True)


def _block(x2d, mix_g, w_in, gn_g, gn_b, sinks, q_g, k_g, w_ret_o, w_swa_o, w_out, ffn_g, w_router, b_router,
           seq):
    t = x2d.shape[0]
    n = t // BLOCK_ROWS
    assert seq % BLOCK_ROWS == 0
    tables = _retention_tables()
    swa_tables = _swa_tables(q_g, k_g)
    consts = tables + (gn_g, gn_b) + swa_tables
    merge_weights = (w_ret_o, w_swa_o, w_out)
    small = (ffn_g, w_router, b_router)
    hbm = pl.BlockSpec(memory_space=pl.ANY)
    full = lambda a: pl.BlockSpec(a.shape, lambda g: (0,) * a.ndim)
    cur = lambda w: pl.BlockSpec((BLOCK_ROWS, w), lambda g: (jnp.minimum(g, n - 1), 0))
    prv = lambda w: pl.BlockSpec((BLOCK_ROWS, w), lambda g: (jnp.maximum(g - 1, 0), 0))
    chunks = BLOCK_ROWS // RET_CHUNK
    subs_per_block = BLOCK_ROWS // MERGE_SUB
    return pl.pallas_call(
        functools.partial(_block_kernel, seq // BLOCK_ROWS),
        grid=(n + 1,),
        in_specs=([pl.BlockSpec(memory_space=pltpu.SMEM), cur(D_MODEL), prv(D_MODEL), full(mix_g), hbm]
                  + [full(a) for a in consts] + [hbm] * len(merge_weights) + [full(a) for a in small]),
        out_specs=[prv(D_MODEL), prv(HALF_D), prv(ROUTE_LANES),
                   pl.BlockSpec((subs_per_block, 1, ROUTE_LANES), lambda g: (jnp.maximum(g - 1, 0), 0, 0))],
        out_shape=[jax.ShapeDtypeStruct((t, D_MODEL), F32),
                   jax.ShapeDtypeStruct((t, HALF_D), I32),
                   jax.ShapeDtypeStruct((t, ROUTE_LANES), F32),
                   jax.ShapeDtypeStruct((n * subs_per_block, 1, ROUTE_LANES), F32)],
        scratch_shapes=[pltpu.VMEM(w_in.shape, BF16)] + [pltpu.VMEM(w.shape, BF16) for w in merge_weights]
                       + [pltpu.VMEM((2, D_MODEL, W_SLAB), F32), pltpu.SemaphoreType.DMA((2,)),
                        pltpu.VMEM((BLOCK_ROWS, IN_WIDTH), BF16), pltpu.VMEM((BLOCK_ROWS, IN_WIDTH), BF16),
                        pltpu.VMEM((BLOCK_ROWS, RET_W), BF16), pltpu.VMEM((BLOCK_ROWS, SWA_Q), BF16),
                        pltpu.VMEM((RET_HEADS, RET_DK, RET_DV), F32),
                        pltpu.VMEM((chunks, RET_HEADS, RET_DK, RET_DV), BF16),
                        pltpu.VMEM((BLOCK_ROWS + WINDOW, Z_SKV_W), BF16), pltpu.VMEM((WINDOW, Z_SKV_W), BF16)],
        compiler_params=pltpu.CompilerParams(dimension_semantics=("arbitrary",),
                                             vmem_limit_bytes=VMEM_LIMIT),
        name="block",
    )(sinks, x2d, x2d, mix_g, w_in, *consts, *merge_weights, *small)


def _plan_kernel(cnt_ref, route_ref, base_ref, meta_ref, pos_ref):
    counts = cnt_ref[...]
    nt = counts.shape[0]
    exact = functools.partial(jnp.dot, precision=lax.Precision.HIGHEST, preferred_element_type=F32)
    total = jnp.sum(counts, axis=0, keepdims=True)
    tiles_e = jnp.ceil(total * (1.0 / ROW_TILE))
    k = lax.broadcasted_iota(jnp.int32, (ROUTE_LANES, ROUTE_LANES), 0)
    lane = lax.broadcasted_iota(jnp.int32, (ROUTE_LANES, ROUTE_LANES), 1)
    tile_end = exact(jnp.broadcast_to(tiles_e, (8, ROUTE_LANES)), jnp.where(k <= lane, 1.0, 0.0))[0:1]
    tile_start = tile_end - tiles_e
    r = lax.broadcasted_iota(jnp.int32, (nt, nt), 0)
    c = lax.broadcasted_iota(jnp.int32, (nt, nt), 1)
    before = exact(jnp.where(c < r, 1.0, 0.0), counts)
    base_ref[...] = tile_start * ROW_TILE + before

    used = jnp.max(tile_end, axis=-1, keepdims=True)
    tile = k.astype(F32)
    clamped = jnp.minimum(tile, jnp.maximum(used - 1.0, 0.0))
    is_expert = lane < N_EXPERTS
    owner = jnp.sum(jnp.where(jnp.logical_and(is_expert, tile_end <= clamped), 1.0, 0.0), axis=-1, keepdims=True)
    owner = jnp.minimum(owner, N_EXPERTS - 1.0)
    mine = lane.astype(F32) == owner
    pick = lambda v: jnp.sum(jnp.where(mine, v, 0.0), axis=-1, keepdims=True)
    t1 = tile[:, 0:1]
    valid = jnp.clip(pick(total) - (t1 - pick(tile_start)) * ROW_TILE, 0.0, float(ROW_TILE))
    valid = jnp.where(t1 < used, valid, 0.0)

    nonempty = jnp.where(jnp.logical_and(lane[0:1] < N_EXPERTS, tiles_e > 0.0), 1.0, 0.0)
    rank = exact(jnp.broadcast_to(nonempty, (8, ROUTE_LANES)), jnp.where(k < lane, 1.0, 0.0))[0:1]
    nonempty_col = jnp.sum(jnp.where(k == lane, jnp.broadcast_to(nonempty, k.shape), 0.0), axis=-1, keepdims=True)
    later = jnp.where(jnp.logical_and(k > lane, nonempty_col > 0.0), tile, float(ROUTE_LANES))
    nxt_e = jnp.min(later, axis=0, keepdims=True)
    nxt_e = jnp.where(nxt_e < float(ROUTE_LANES), nxt_e, -1.0)
    first = jnp.where(jnp.logical_and(t1 == pick(tile_start), t1 < used), 1.0, 0.0)
    my_rank = pick(rank)
    slot = my_rank - 2.0 * jnp.floor(my_rank * 0.5)
    fields = (owner, valid, first, pick(nxt_e), slot, jnp.broadcast_to(used, owner.shape))
    by_tile = sum(jnp.where(lane == n, f, 0.0) for n, f in enumerate(fields))
    meta_ref[...] = by_tile.T[:8, :].astype(I32)

    sel_row = lax.broadcasted_iota(jnp.int32, (8, ROUTE_LANES), 0)
    sel_lane = lax.broadcasted_iota(jnp.int32, (8, ROUTE_LANES), 1)
    tok_r = lax.broadcasted_iota(jnp.int32, (MERGE_SUB, MERGE_SUB), 0)
    tok_c = lax.broadcasted_iota(jnp.int32, (MERGE_SUB, MERGE_SUB), 1)
    eye = jnp.where(tok_r == tok_c, 1.0, 0.0).astype(BF16)
    upto = jnp.where(tok_r <= tok_c, 1.0, 0.0).astype(BF16)
    out_row = lax.broadcasted_iota(jnp.int32, (8, MERGE_SUB), 0)
    sum_experts = lambda a: jnp.sum(a, axis=0, keepdims=True)

    def sub_positions(s, carry):
        route = route_ref[pl.ds(pl.multiple_of(s * MERGE_SUB, MERGE_SUB), MERGE_SUB), :].astype(BF16)
        record_t = _dot_tn(route, eye)
        incl_t = _dot_tn(route, upto)[:N_EXPERTS]
        both_t = record_t[:N_EXPERTS]
        second_t = record_t[LANE_SECOND0:LANE_SECOND0 + N_EXPERTS]
        rank2 = sum_experts(second_t * incl_t) - 1.0
        rank1 = sum_experts(both_t * incl_t) - rank2 - 2.0
        base_row = jnp.where(sel_lane < N_EXPERTS, jnp.broadcast_to(base_ref[pl.ds(s, 1), :], (8, ROUTE_LANES)), 0.0)
        hi = jnp.floor(base_row * (1.0 / 256.0))
        lo = base_row - 256.0 * hi
        hi2 = pltpu.roll(hi, LANE_SECOND0, axis=1)
        lo2 = pltpu.roll(lo, LANE_SECOND0, axis=1)
        table = (jnp.where(sel_row == 0, hi, 0.0) + jnp.where(sel_row == 1, lo, 0.0)
                 + jnp.where(sel_row == 2, hi2, 0.0) + jnp.where(sel_row == 3, lo2, 0.0))
        r = _dot_nt(table.astype(BF16), route)
        base_both = 256.0 * r[0:1] + r[1:2]
        base2 = 256.0 * r[2:3] + r[3:4]
        out = jnp.where(out_row == 0, base_both - base2 + rank1, jnp.where(out_row == 1, base2 + rank2, 0.0))
        pos_ref[s] = out.astype(I32)
        return carry

    lax.fori_loop(0, nt, sub_positions, 0, unroll=4)


def _tile_plan(cnt, route, n_tiles):
    assert n_tiles <= ROUTE_LANES
    nt = cnt.shape[0]
    _, meta, pos = pl.pallas_call(
        _plan_kernel,
        out_shape=[jax.ShapeDtypeStruct((nt, ROUTE_LANES), F32), jax.ShapeDtypeStruct((8, ROUTE_LANES), I32),
                   jax.ShapeDtypeStruct((nt, 8, MERGE_SUB), I32)],
        compiler_params=pltpu.CompilerParams(vmem_limit_bytes=VMEM_LIMIT),
        name="tile_plan",
    )(cnt.reshape(nt, ROUTE_LANES), route)
    return meta, pos[:, 0, :], pos[:, 1, :]


def _sc_mesh():
    return plsc.VectorSubcoreMesh(core_axis_name="c", subcore_axis_name="s")


def _sc_worker():
    return lax.axis_index("s") * SC_CORES + lax.axis_index("c")


def _dispatch(hp, pos1, pos2, n_rows):
    t = hp.shape[0]
    per_worker = t // SC_WORKERS
    k = per_worker // SC_CHUNK
    idx = pltpu.VMEM((k, SC_CHUNK), I32)

    @functools.partial(pl.kernel, mesh=_sc_mesh(), out_type=jax.ShapeDtypeStruct((n_rows, HALF_D), I32),
                       scratch_types=[idx, idx, pltpu.VMEM((SC_CHUNK, HALF_D), I32)], name="moe_dispatch")
    def run(h_hbm, p1_hbm, p2_hbm, xs_hbm, p1_v, p2_v, rows_v):
        wid = _sc_worker()
        pltpu.sync_copy(p1_hbm.at[wid], p1_v)
        pltpu.sync_copy(p2_hbm.at[wid], p2_v)
        for j in range(k):
            pltpu.sync_copy(h_hbm.at[pl.ds(wid * per_worker + j * SC_CHUNK, SC_CHUNK)], rows_v)
            pltpu.sync_copy(rows_v, xs_hbm.at[p1_v.at[j]])
            pltpu.sync_copy(rows_v, xs_hbm.at[p2_v.at[j]])

    return run(hp, pos1.reshape(SC_WORKERS, k, SC_CHUNK), pos2.reshape(SC_WORKERS, k, SC_CHUNK))


def _combine_gather(ys, pos1, pos2):
    t = pos1.size
    per_worker = t // SC_WORKERS
    k = per_worker // SC_CHUNK
    idx = pltpu.VMEM((k, SC_CHUNK), I32)
    out = jax.ShapeDtypeStruct((t, HALF_D), I32)

    @functools.partial(pl.kernel, mesh=_sc_mesh(), out_type=(out, out),
                       scratch_types=[idx, idx, pltpu.VMEM((SC_CHUNK, HALF_D), I32)], name="moe_combine")
    def run(ys_hbm, p1_hbm, p2_hbm, g1_hbm, g2_hbm, p1_v, p2_v, rows_v):
        wid = _sc_worker()
        pltpu.sync_copy(p1_hbm.at[wid], p1_v)
        pltpu.sync_copy(p2_hbm.at[wid], p2_v)
        for j in range(k):
            dst = pl.ds(wid * per_worker + j * SC_CHUNK, SC_CHUNK)
            pltpu.sync_copy(ys_hbm.at[p1_v.at[j]], rows_v)
            pltpu.sync_copy(rows_v, g1_hbm.at[dst])
            pltpu.sync_copy(ys_hbm.at[p2_v.at[j]], rows_v)
            pltpu.sync_copy(rows_v, g2_hbm.at[dst])

    return run(ys, pos1.reshape(SC_WORKERS, k, SC_CHUNK), pos2.reshape(SC_WORKERS, k, SC_CHUNK))


META_OWNER, META_VALID, META_FIRST, META_NEXT, META_SLOT, META_USED = range(6)


def _experts_kernel(meta_ref, xs_hbm, wg_hbm, wu_hbm, wd_hbm, ys_ref,
                    xs_buf, xs_sem, wg_f, wu_f, wd_f, wg_bf, wu_bf, wd_bf, w_sem):
    i = pl.program_id(0)
    used = meta_ref[META_USED, 0]
    valid = meta_ref[META_VALID, i]
    owner = meta_ref[META_OWNER, i]
    nxt = meta_ref[META_NEXT, i]

    def rows_in(tile):
        start = pl.multiple_of(tile * ROW_TILE, ROW_TILE)
        slot = lax.rem(tile, XS_SLOTS)
        return pltpu.make_async_copy(xs_hbm.at[pl.ds(start, ROW_TILE)], xs_buf.at[slot], xs_sem.at[slot])

    @pl.when(i == 0)
    def _():
        for ahead in range(XS_SLOTS - 1):
            pl.when(ahead < used)(rows_in(ahead).start)

    @pl.when(i + (XS_SLOTS - 1) < used)
    def _():
        rows_in(i + (XS_SLOTS - 1)).start()

    def fetch(expert, slot):
        pairs = ((wg_hbm, wg_f), (wu_hbm, wu_f), (wd_hbm, wd_f))
        return [pltpu.make_async_copy(w.at[expert], buf.at[slot], w_sem.at[slot, n])
                for n, (w, buf) in enumerate(pairs)]

    @pl.when(i == 0)
    def _():
        for copy in fetch(owner, 0):
            copy.start()

    @pl.when(meta_ref[META_FIRST, i] == 1)
    def _():
        slot = meta_ref[META_SLOT, i]
        for copy in fetch(owner, slot):
            copy.wait()
        wg_bf[...] = wg_f[slot].astype(BF16)
        wu_bf[...] = wu_f[slot].astype(BF16)
        wd_bf[...] = wd_f[slot].astype(BF16)

        @pl.when(nxt >= 0)
        def _():
            for copy in fetch(nxt, 1 - slot):
                copy.start()

    @pl.when(i < used)
    def _():
        rows_in(i).wait()
        row = lax.broadcasted_iota(jnp.int32, (ROW_TILE, HALF_D), 0)
        words = xs_buf[lax.rem(i, XS_SLOTS)]
        packed = jnp.where(row < valid, words, 0)
        hi, lo = _unpack_halves(packed)
        hi = hi.astype(BF16)
        lo = lo.astype(BF16)
        hg = _dot(hi, wg_bf[:HALF_D, :]) + _dot(lo, wg_bf[HALF_D:, :])
        hu = _dot(hi, wu_bf[:HALF_D, :]) + _dot(lo, wu_bf[HALF_D:, :])
        hid = (hg * _sigmoid(hg) * hu).astype(BF16)
        ys_ref[...] = _pack_halves(_dot(hid, wd_bf[...]))


def _experts(xs, meta, n_tiles, wg, wu, wd):
    tile = lambda i, meta: (jnp.minimum(i, meta[META_USED, 0] - 1), 0)
    hbm = pl.BlockSpec(memory_space=pl.ANY)
    up_shape, down_shape = (D_MODEL, EXPERT_FF), (EXPERT_FF, D_MODEL)
    grid_spec = pltpu.PrefetchScalarGridSpec(
        num_scalar_prefetch=1,
        grid=(n_tiles,),
        in_specs=[hbm, hbm, hbm, hbm],
        out_specs=pl.BlockSpec((ROW_TILE, HALF_D), tile),
        scratch_shapes=[pltpu.VMEM((XS_SLOTS, ROW_TILE, HALF_D), I32), pltpu.SemaphoreType.DMA((XS_SLOTS,)),
                        pltpu.VMEM((2,) + up_shape, F32), pltpu.VMEM((2,) + up_shape, F32),
                        pltpu.VMEM((2,) + down_shape, F32),
                        pltpu.VMEM(up_shape, BF16), pltpu.VMEM(up_shape, BF16), pltpu.VMEM(down_shape, BF16),
                        pltpu.SemaphoreType.DMA((2, 3))],
    )
    return pl.pallas_call(
        _experts_kernel,
        grid_spec=grid_spec,
        out_shape=jax.ShapeDtypeStruct((n_tiles * ROW_TILE, HALF_D), I32),
        compiler_params=pltpu.CompilerParams(dimension_semantics=("arbitrary",),
                                             vmem_limit_bytes=VMEM_LIMIT),
        name="experts",
    )(meta, xs, wg, wu, wd)


def _ple_kernel(x1_hbm, g1_hbm, g2_hbm, route_ref, p_ref, wp_f32, pg_ref, gg_ref, wgate_f32, o_ref,
                wp_ref, wgate_ref, x1_buf, g1_buf, g2_buf, ring_sem):
    i = pl.program_id(0)
    steps = pl.num_programs(0)
    streams = ((x1_hbm, x1_buf), (g1_hbm, g1_buf), (g2_hbm, g2_buf))

    def rows_in(step):
        start = pl.multiple_of(step * PLE_TM, PLE_TM)
        slot = lax.rem(step, PLE_SLOTS)
        return [pltpu.make_async_copy(src.at[pl.ds(start, PLE_TM)], buf.at[slot], ring_sem.at[n, slot])
                for n, (src, buf) in enumerate(streams)]

    def start_rows(step):
        for copy in rows_in(step):
            copy.start()

    @pl.when(i == 0)
    def _():
        for ahead in range(PLE_SLOTS - 1):
            pl.when(ahead < steps)(functools.partial(start_rows, ahead))
        wp_ref[...] = wp_f32[...].astype(BF16)
        wgate_ref[...] = wgate_f32[...].astype(BF16)

    pl.when(i + (PLE_SLOTS - 1) < steps)(functools.partial(start_rows, i + (PLE_SLOTS - 1)))

    for copy in rows_in(i):
        copy.wait()
    slot = lax.rem(i, PLE_SLOTS)
    x1_ref, g1_ref, g2_ref = x1_buf.at[slot], g1_buf.at[slot], g2_buf.at[slot]

    route = route_ref[...]
    lane = lax.broadcasted_iota(jnp.int32, route.shape, 1)
    pick = lambda ln: jnp.sum(jnp.where(lane == ln, route, 0.0), axis=-1, keepdims=True)
    c1 = pick(LANE_C1)
    c2 = pick(LANE_C2)
    a_hi, a_lo = _unpack_halves(g1_ref[...])
    b_hi, b_lo = _unpack_halves(g2_ref[...])
    x = x1_ref[...] + jnp.concatenate([c1 * a_hi + c2 * b_hi, c1 * a_lo + c2 * b_lo], axis=1)
    ple = _rms(_dot(p_ref[...].astype(BF16), wp_ref[...]), pg_ref[...])
    gate = _sigmoid(_dot(_rms(x, gg_ref[...]).astype(BF16), wgate_ref[...]))
    o_ref[...] = x + gate * ple


def _ple(x1, g1, g2, route, p2d, w_ple, ple_g, gate_g, w_gate):
    t = x1.shape[0]
    row = lambda w: pl.BlockSpec((PLE_TM, w), lambda i: (i, 0))
    full = lambda a: pl.BlockSpec(a.shape, lambda i: (0,) * a.ndim)
    hbm = pl.BlockSpec(memory_space=pl.ANY)
    return pl.pallas_call(
        _ple_kernel,
        grid=(t // PLE_TM,),
        in_specs=[hbm, hbm, hbm, row(ROUTE_LANES), row(PLE_DIM),
                  full(w_ple), full(ple_g), full(gate_g), full(w_gate)],
        out_specs=row(D_MODEL),
        out_shape=jax.ShapeDtypeStruct((t, D_MODEL), F32),
        scratch_shapes=[pltpu.VMEM(w_ple.shape, BF16), pltpu.VMEM(w_gate.shape, BF16),
                        pltpu.VMEM((PLE_SLOTS, PLE_TM, D_MODEL), F32),
                        pltpu.VMEM((PLE_SLOTS, PLE_TM, HALF_D), I32), pltpu.VMEM((PLE_SLOTS, PLE_TM, HALF_D), I32),
                        pltpu.SemaphoreType.DMA((3, PLE_SLOTS))],
        compiler_params=pltpu.CompilerParams(dimension_semantics=("arbitrary",),
                                             vmem_limit_bytes=VMEM_LIMIT),
        name="ple",
    )(x1, g1, g2, route, p2d, w_ple, ple_g, gate_g, w_gate)


def kernel(x, p, mix_norm_g, w_in, ret_gn_g, ret_gn_b, w_ret_o, q_norm_g, k_norm_g, attn_sinks,
           w_swa_o, w_out, ffn_norm_g, w_router_group, b_router_group, w_router_expert,
           b_router_expert, w_exp_gate, w_exp_up, w_exp_down, ple_gate_norm_g, w_ple_gate,
           w_ple, ple_norm_g):
    batch, seq, d = x.shape
    t = batch * seq
    depth = w_in.shape[0]
    n_tiles = 2 * t // ROW_TILE + N_EXPERTS
    row = lambda a: a.reshape(1, -1)
    x2d = x.reshape(t, d)
    for i in range(depth):
        pad = ROUTE_LANES - N_GROUPS - N_EXPERTS
        w_router = jnp.pad(jnp.concatenate([w_router_expert[i], w_router_group[i]], axis=1), ((0, 0), (0, pad)))
        b_router = jnp.pad(jnp.concatenate([b_router_expert[i], b_router_group[i]]), (0, pad)).reshape(1, -1)
        w_router = w_router.astype(BF16)
        x1, hp, route, cnt = _block(x2d, row(mix_norm_g[i]), w_in[i], row(ret_gn_g[i]),
                                    row(ret_gn_b[i]), attn_sinks[i], q_norm_g[i], k_norm_g[i],
                                    w_ret_o[i], w_swa_o[i], w_out[i],
                                    row(ffn_norm_g[i]), w_router, b_router, seq)
        meta, pos1, pos2 = _tile_plan(cnt, route, n_tiles)
        xs = _dispatch(hp, pos1, pos2, n_tiles * ROW_TILE)
        ys = _experts(xs, meta, n_tiles, w_exp_gate[i], w_exp_up[i], w_exp_down[i])
        g1, g2 = _combine_gather(ys, pos1, pos2)
        x2d = _ple(x1, g1, g2, route, p[i].reshape(t, PLE_DIM), w_ple[i], row(ple_norm_g[i]),
                   row(ple_gate_norm_g[i]), w_ple_gate[i])
    return x2d.reshape(batch, seq, d)
```

```python
import functools

import jax
import jax.numpy as jnp
import numpy as np
from jax import lax
from jax.experimental import pallas as pl
from jax.experimental.pallas import tpu as pltpu
from jax.experimental.pallas import tpu_sc as plsc

F32 = jnp.float32
BF16 = jnp.bfloat16
I32 = jnp.int32

EPS = 1e-6
D_MODEL = 1024
PLE_DIM = 256
RET_HEADS = 4
RET_DK = 128
RET_DV = 128
RET_CHUNK = 128
SWA_HEADS = 8
SWA_KV_HEADS = 2
SWA_GROUP = SWA_HEADS // SWA_KV_HEADS
SWA_HD = 64
WINDOW = 128
N_GROUPS = 4
EXPERTS_PER_GROUP = 8
N_EXPERTS = N_GROUPS * EXPERTS_PER_GROUP
EXPERT_FF = 256

RET_W = RET_HEADS * RET_DK
SWA_Q = SWA_HEADS * SWA_HD
SWA_KV = SWA_KV_HEADS * SWA_HD
Z_RET_W = 4 * RET_W
Z_SKV_W = 2 * SWA_KV
Z_GATE_W = 2 * D_MODEL
COL_SQ = Z_RET_W
COL_SKV = COL_SQ + SWA_Q
COL_GATE = COL_SKV + Z_SKV_W
IN_WIDTH = COL_GATE + Z_GATE_W

ROUTE_LANES = 128
LANE_GROUP0 = N_EXPERTS
LANE_C1, LANE_C2 = 34, 35
LANE_SECOND0 = 64
NEG_INF = -1e30

HALF_D = D_MODEL // 2
ROW_TILE = 512
XS_SLOTS = 3
SC_CORES = 2
SC_SUBCORES = 16
SC_WORKERS = SC_CORES * SC_SUBCORES
SC_CHUNK = 128
SC_IDX_ROWS = 8

VMEM_LIMIT = 56 * 1024 * 1024

BLOCK_ROWS = 512
W_SLAB = 256
RET_ROWS = 512
MERGE_SUB = 256
PLE_TM = 1024
PLE_SLOTS = 3


def _rms(x, g):
    ms = jnp.mean(x * x, axis=-1, keepdims=True)
    return x * lax.rsqrt(ms + EPS) * g


def _sigmoid(x):
    return 1.0 / (1.0 + jnp.exp(-x))


def _dot(a, b):
    return jnp.dot(a, b, preferred_element_type=F32)


def _dot_nt(a, b):
    return lax.dot_general(a, b, (((1,), (1,)), ((), ())), preferred_element_type=F32)


def _dot_tn(a, b):
    return lax.dot_general(a, b, (((0,), (0,)), ((), ())), preferred_element_type=F32)


def _pack_halves(v):
    return pltpu.pack_elementwise([v[:, :HALF_D], v[:, HALF_D:]], packed_dtype=BF16)


def _unpack_halves(p):
    words = lax.bitcast_convert_type(p, jnp.uint32)
    return tuple(pltpu.unpack_elementwise(words, index=k, packed_dtype=BF16, unpacked_dtype=F32) for k in range(2))


def _retention_tables():
    f32 = np.float32
    h = RET_HEADS
    c = RET_CHUNK
    log_gamma = np.log1p(-np.exp2(f32(-5.0) - np.arange(h, dtype=f32))).astype(f32)
    pos = np.arange(c, dtype=f32)
    diff = pos[:, None] - pos[None, :]
    decay = np.where(diff[None] >= 0.0, np.exp(np.maximum(diff, f32(0.0))[None] * log_gamma[:, None, None]), f32(0.0))
    scale = f32(RET_DK ** -0.5)
    dmask = decay * scale
    zeta = np.exp((f32(c - 1.0) - pos)[None, :] * log_gamma[:, None]) * scale
    xi = np.exp((pos + f32(1.0))[None, :] * log_gamma[:, None])
    cdec = np.exp(f32(c) * log_gamma)
    bc = lambda a: np.ascontiguousarray(np.broadcast_to(a[:, :, None], (h, c, c)), dtype=f32)
    return (dmask.astype(f32), bc(zeta), bc(xi),
            np.ascontiguousarray(np.broadcast_to(cdec[:, None, None], (h, c, c)), dtype=f32))


def _swa_tables(q_g, k_g):
    f32 = np.float32
    qi = np.arange(WINDOW)[:, None]
    sj = np.arange(2 * WINDOW)[None, :]
    rel = qi + WINDOW - sj
    ok = (rel >= 0) & (rel < WINDOW)
    slopes = np.exp2(f32(-8.0) * np.arange(1, SWA_HEADS + 1, dtype=f32) / f32(SWA_HEADS)).astype(f32)
    bias = np.where(ok[None], -slopes[:, None, None] * rel.astype(f32)[None], f32(NEG_INF)).astype(f32)
    bias = bias.reshape(SWA_KV_HEADS, SWA_GROUP * WINDOW, 2 * WINDOW)
    avg = lambda heads: jnp.asarray(np.kron(np.eye(heads, dtype=f32), np.full((SWA_HD, SWA_HD), 1.0 / SWA_HD, f32)),
                                    dtype=BF16)
    qg = jnp.tile(q_g.reshape(1, SWA_HD), (1, SWA_HEADS)) * (SWA_HD ** -0.5)
    kg = jnp.tile(k_g.reshape(1, SWA_HD), (1, SWA_KV_HEADS))
    return qg, kg, avg(SWA_HEADS), avg(SWA_KV_HEADS), jnp.asarray(bias)


def _block_kernel(per_seq, sinks_ref, x_ref, xp_ref, g_ref, w_hbm, dmask_ref, zeta_ref, xi_ref, cdec_ref,
                  gng_ref, gnb_ref, qg_ref, kg_ref, bq_ref, bk_ref, bias_ref,
                  wro_hbm, wso_hbm, wout_hbm, fg_ref, wr_ref, br_ref,
                  x1_ref, hp_ref, route_ref, cnt_ref,
                  w_ref, wro_ref, wso_ref, wout_ref, wtmp_ref, w_sem,
                  zc_ref, zn_ref, yr_ref, ys_ref, state_ref, prev_ref, band_ref, tail_ref):
    g = pl.program_id(0)
    seq_start = lax.rem(g - 1, per_seq) == 0

    @pl.when(g == 0)
    def _():
        slabs = [(src, dst, c0) for src, dst in ((w_hbm, w_ref), (wro_hbm, wro_ref), (wso_hbm, wso_ref),
                                                 (wout_hbm, wout_ref))
                 for c0 in range(0, dst.shape[1], W_SLAB)]

        def slab(k):
            src, dst, c0 = slabs[k]
            return pltpu.make_async_copy(src.at[:, pl.ds(c0, W_SLAB)], wtmp_ref.at[k % 2, pl.ds(0, dst.shape[0])],
                                         w_sem.at[k % 2])

        slab(0).start()
        for k, (_, dst, c0) in enumerate(slabs):
            if k + 1 < len(slabs):
                slab(k + 1).start()
            slab(k).wait()
            dst[:, c0:c0 + W_SLAB] = wtmp_ref[k % 2, :dst.shape[0]].astype(BF16)
        zc_ref[...] = jnp.zeros_like(zc_ref)
        tail_ref[...] = jnp.zeros_like(tail_ref)
        state_ref[...] = jnp.zeros_like(state_ref)

    @pl.when(seq_start)
    def _():
        state_ref[...] = jnp.zeros_like(state_ref)

    band_ref[:WINDOW, :] = tail_ref[...]
    band_ref[WINDOW:, :] = zc_ref[:, COL_SKV:COL_GATE]

    h = _rms(x_ref[...], g_ref[...]).astype(BF16)

    def project(c0, width):
        zn_ref[:, c0:c0 + width] = _dot(h, w_ref[:, c0:c0 + width]).astype(BF16)

    cw = 512
    proj_pieces = [functools.partial(project, j * cw, cw) for j in range(COL_SKV // cw)]
    proj_pieces.append(functools.partial(project, COL_SKV, Z_SKV_W))
    proj_pieces += [functools.partial(project, COL_GATE + j * cw, cw) for j in range(Z_GATE_W // cw)]

    def retention_rows(t):
        r_base = t * RET_ROWS
        n_chunks = RET_ROWS // RET_CHUNK
        part = lambda c, p, hd: zc_ref[r_base + c * RET_CHUNK:r_base + (c + 1) * RET_CHUNK,
                                       p * RET_W + hd * RET_DK:p * RET_W + (hd + 1) * RET_DK]
        for hd in range(RET_HEADS):
            state = state_ref[hd]
            for c in range(n_chunks):
                prev_ref[t * n_chunks + c, hd] = state.astype(BF16)
                kz = (part(c, 1, hd).astype(F32) * zeta_ref[hd]).astype(BF16)
                state = cdec_ref[hd] * state + _dot_tn(kz, part(c, 2, hd))
            state_ref[hd] = state
        out_rows = []
        for c in range(n_chunks):
            heads = []
            for hd in range(RET_HEADS):
                q = part(c, 0, hd)
                scores = _dot_nt(q, part(c, 1, hd)) * dmask_ref[hd]
                y = _dot(scores.astype(BF16), part(c, 2, hd)) + _dot(q, prev_ref[t * n_chunks + c, hd]) * xi_ref[hd]
                mu = jnp.mean(y, axis=-1, keepdims=True)
                d = y - mu
                var = jnp.mean(d * d, axis=-1, keepdims=True)
                hs = slice(hd * RET_DV, (hd + 1) * RET_DV)
                yn = d * lax.rsqrt(var + EPS) * gng_ref[:, hs] + gnb_ref[:, hs]
                gate = part(c, 3, hd).astype(F32)
                heads.append((gate * _sigmoid(gate) * yn).astype(BF16))
            out_rows.append(jnp.concatenate(heads, axis=1))
        yr_ref[r_base:r_base + RET_ROWS, :] = jnp.concatenate(out_rows, axis=0)

    group_rows = SWA_GROUP * WINDOW
    lo_q = lax.broadcasted_iota(jnp.int32, (WINDOW, 2 * SWA_HD), 1) < SWA_HD
    lo_k = lax.broadcasted_iota(jnp.int32, (2 * WINDOW, 2 * SWA_HD), 1) < SWA_HD
    in_prev = lax.broadcasted_iota(jnp.int32, (group_rows, 2 * WINDOW), 1) < WINDOW
    row_head = lax.broadcasted_iota(jnp.int32, (group_rows, 1), 0) // WINDOW

    def both_heads(v):
        swapped = pltpu.roll(v, SWA_HD, axis=1)
        return jnp.where(lo_k, v, swapped), jnp.where(lo_k, swapped, v)

    def swa_block(n):
        rows = slice(n * WINDOW, (n + 1) * WINDOW)
        kv = band_ref[n * WINDOW:(n + 2) * WINDOW, :]
        kf = kv[:, :SWA_KV].astype(F32)
        kn = kf * lax.rsqrt(_dot((kf * kf).astype(BF16), bk_ref[...]) + EPS) * kg_ref[...]
        keys = [a.astype(BF16) for a in both_heads(kn)]
        vals = [a.astype(BF16) for a in both_heads(kv[:, SWA_KV:].astype(F32))]
        qf = zc_ref[rows, COL_SQ:COL_SKV].astype(F32)
        qn = qf * lax.rsqrt(_dot((qf * qf).astype(BF16), bq_ref[...]) + EPS) * qg_ref[...]
        outs = []
        for kh in range(SWA_KV_HEADS):
            parts = []
            for j in range(SWA_GROUP // 2):
                c0 = (kh * SWA_GROUP + 2 * j) * SWA_HD
                two = qn[:, c0:c0 + 2 * SWA_HD]
                parts += [jnp.where(lo_q, two, 0.0), jnp.where(lo_q, 0.0, two)]
            qs = jnp.concatenate(parts, axis=0).astype(BF16)
            s = _dot_nt(qs, keys[kh]) + bias_ref[kh]
            if n == 0:
                s = jnp.where(jnp.logical_and(seq_start, in_prev), NEG_INF, s)
            sink = sinks_ref[kh * SWA_GROUP]
            for gi in range(1, SWA_GROUP):
                sink = jnp.where(row_head == gi, sinks_ref[kh * SWA_GROUP + gi], sink)
            m = jnp.maximum(jnp.max(s, axis=-1, keepdims=True), sink)
            p = jnp.exp(s - m)
            denom = jnp.sum(p, axis=-1, keepdims=True) + jnp.exp(sink - m)
            o = _dot(p.astype(BF16), vals[kh]) * (1.0 / denom)
            for j in range(SWA_GROUP // 2):
                even = o[(2 * j) * WINDOW:(2 * j + 1) * WINDOW]
                odd = o[(2 * j + 1) * WINDOW:(2 * j + 2) * WINDOW]
                outs.append(jnp.where(lo_q, even, odd).astype(BF16))
        ys_ref[rows, :] = jnp.concatenate(outs, axis=1)

    mix_pieces = [functools.partial(retention_rows, t) for t in range(BLOCK_ROWS // RET_ROWS)]
    mix_pieces += [functools.partial(swa_block, n) for n in range(BLOCK_ROWS // WINDOW)]

    subs = []

    def merge_sub(s):
        subs.append(_merge_rows(slice(s * MERGE_SUB, (s + 1) * MERGE_SUB), yr_ref, ys_ref, zc_ref, xp_ref, wro_ref,
                                wso_ref, wout_ref, fg_ref, wr_ref, br_ref))

    other_pieces = mix_pieces + [functools.partial(merge_sub, s) for s in range(BLOCK_ROWS // MERGE_SUB)]

    n_other = len(other_pieces)
    for k, piece in enumerate(other_pieces):
        for proj_piece in proj_pieces[k * len(proj_pieces) // n_other:(k + 1) * len(proj_pieces) // n_other]:
            proj_piece()
        piece()
    x1_ref[...] = jnp.concatenate([s[0] for s in subs], axis=0)
    hp_ref[...] = jnp.concatenate([s[1] for s in subs], axis=0)
    route_ref[...] = jnp.concatenate([s[2] for s in subs], axis=0)
    for s, sub in enumerate(subs):
        cnt_ref[s] = sub[3]

    tail_ref[...] = zc_ref[BLOCK_ROWS - WINDOW:, COL_SKV:COL_GATE]
    zc_ref[...] = zn_ref[...]


def _merge_rows(rows, yr_ref, ys_ref, z_ref, x_ref, wro_ref, wso_ref, wout_ref, fg_ref, wr_ref, br_ref):
    a = _dot(yr_ref[rows, :], wro_ref[...])
    b = _dot(ys_ref[rows, :], wso_ref[...])
    gate_r = z_ref[rows, COL_GATE:COL_GATE + D_MODEL].astype(F32)
    gate_s = z_ref[rows, COL_GATE + D_MODEL:].astype(F32)
    merged = (_sigmoid(gate_r) * a + _sigmoid(gate_s) * b).astype(BF16)
    x1 = x_ref[rows, :] + _dot(merged, wout_ref[...])
    h2 = _rms(x1, fg_ref[...])
    packed = _pack_halves(h2)

    logits = _dot(h2.astype(BF16), wr_ref[...]) + br_ref[...]
    lane = lax.broadcasted_iota(jnp.int32, logits.shape, 1)
    big = jnp.int32(ROUTE_LANES)
    is_group = jnp.logical_and(lane >= LANE_GROUP0, lane < LANE_GROUP0 + N_GROUPS)
    gl = jnp.where(is_group, logits, NEG_INF)
    gmax = jnp.max(gl, axis=-1, keepdims=True)
    g_w = 1.0 / jnp.sum(jnp.exp(gl - gmax), axis=-1, keepdims=True)
    g_sel = jnp.min(jnp.where(gl == gmax, lane, big), axis=-1, keepdims=True) - LANE_GROUP0
    in_group = jnp.logical_and(lane < N_EXPERTS, (lane >> 3) == g_sel)
    el = jnp.where(in_group, logits, NEG_INF)
    m1 = jnp.max(el, axis=-1, keepdims=True)
    i1 = jnp.min(jnp.where(el == m1, lane, big), axis=-1, keepdims=True)
    el2 = jnp.where(lane == i1, NEG_INF, el)
    m2 = jnp.max(el2, axis=-1, keepdims=True)
    i2 = jnp.min(jnp.where(el2 == m2, lane, big), axis=-1, keepdims=True)
    e2 = jnp.exp(m2 - m1)
    c1 = g_w / (1.0 + e2)
    c2 = g_w * e2 / (1.0 + e2)
    chosen = jnp.where(jnp.logical_or(lane == i1, lane == i2), 1.0, 0.0)
    route = (chosen
             + jnp.where(lane == LANE_C1, c1, 0.0)
             + jnp.where(lane == LANE_C2, c2, 0.0)
             + jnp.where(lane == i2 + LANE_SECOND0, 1.0, 0.0))
    return x1, packed, route, jnp.sum(chosen, axis=0, keepdims=True)


def _block(x2d, mix_g, w_in, gn_g, gn_b, sinks, q_g, k_g, w_ret_o, w_swa_o, w_out, ffn_g, w_router, b_router,
           seq):
    t = x2d.shape[0]
    n = t // BLOCK_ROWS
    assert seq % BLOCK_ROWS == 0
    tables = _retention_tables()
    swa_tables = _swa_tables(q_g, k_g)
    consts = tables + (gn_g, gn_b) + swa_tables
    merge_weights = (w_ret_o, w_swa_o, w_out)
    small = (ffn_g, w_router, b_router)
    hbm = pl.BlockSpec(memory_space=pl.ANY)
    full = lambda a: pl.BlockSpec(a.shape, lambda g: (0,) * a.ndim)
    cur = lambda w: pl.BlockSpec((BLOCK_ROWS, w), lambda g: (jnp.minimum(g, n - 1), 0))
    prv = lambda w: pl.BlockSpec((BLOCK_ROWS, w), lambda g: (jnp.maximum(g - 1, 0), 0))
    chunks = BLOCK_ROWS // RET_CHUNK
    subs_per_block = BLOCK_ROWS // MERGE_SUB
    return pl.pallas_call(
        functools.partial(_block_kernel, seq // BLOCK_ROWS),
        grid=(n + 1,),
        in_specs=([pl.BlockSpec(memory_space=pltpu.SMEM), cur(D_MODEL), prv(D_MODEL), full(mix_g), hbm]
                  + [full(a) for a in consts] + [hbm] * len(merge_weights) + [full(a) for a in small]),
        out_specs=[prv(D_MODEL), prv(HALF_D), prv(ROUTE_LANES),
                   pl.BlockSpec((subs_per_block, 1, ROUTE_LANES), lambda g: (jnp.maximum(g - 1, 0), 0, 0))],
        out_shape=[jax.ShapeDtypeStruct((t, D_MODEL), F32),
                   jax.ShapeDtypeStruct((t, HALF_D), I32),
                   jax.ShapeDtypeStruct((t, ROUTE_LANES), F32),
                   jax.ShapeDtypeStruct((n * subs_per_block, 1, ROUTE_LANES), F32)],
        scratch_shapes=[pltpu.VMEM(w_in.shape, BF16)] + [pltpu.VMEM(w.shape, BF16) for w in merge_weights]
                       + [pltpu.VMEM((2, D_MODEL, W_SLAB), F32), pltpu.SemaphoreType.DMA((2,)),
                        pltpu.VMEM((BLOCK_ROWS, IN_WIDTH), BF16), pltpu.VMEM((BLOCK_ROWS, IN_WIDTH), BF16),
                        pltpu.VMEM((BLOCK_ROWS, RET_W), BF16), pltpu.VMEM((BLOCK_ROWS, SWA_Q), BF16),
                        pltpu.VMEM((RET_HEADS, RET_DK, RET_DV), F32),
                        pltpu.VMEM((chunks, RET_HEADS, RET_DK, RET_DV), BF16),
                        pltpu.VMEM((BLOCK_ROWS + WINDOW, Z_SKV_W), BF16), pltpu.VMEM((WINDOW, Z_SKV_W), BF16)],
        compiler_params=pltpu.CompilerParams(dimension_semantics=("arbitrary",),
                                             vmem_limit_bytes=VMEM_LIMIT),
        name="block",
    )(sinks, x2d, x2d, mix_g, w_in, *consts, *merge_weights, *small)


def _plan_kernel(cnt_ref, route_ref, base_ref, meta_ref, pos1_ref, pos2_ref):
    counts = cnt_ref[...]
    nt = counts.shape[0]
    exact = functools.partial(jnp.dot, precision=lax.Precision.HIGHEST, preferred_element_type=F32)
    total = jnp.sum(counts, axis=0, keepdims=True)
    tiles_e = jnp.ceil(total * (1.0 / ROW_TILE))
    k = lax.broadcasted_iota(jnp.int32, (ROUTE_LANES, ROUTE_LANES), 0)
    lane = lax.broadcasted_iota(jnp.int32, (ROUTE_LANES, ROUTE_LANES), 1)
    tile_end = exact(jnp.broadcast_to(tiles_e, (8, ROUTE_LANES)), jnp.where(k <= lane, 1.0, 0.0))[0:1]
    tile_start = tile_end - tiles_e
    r = lax.broadcasted_iota(jnp.int32, (nt, nt), 0)
    c = lax.broadcasted_iota(jnp.int32, (nt, nt), 1)
    before = exact(jnp.where(c < r, 1.0, 0.0), counts)
    base_ref[...] = tile_start * ROW_TILE + before

    used = jnp.max(tile_end, axis=-1, keepdims=True)
    tile = k.astype(F32)
    clamped = jnp.minimum(tile, jnp.maximum(used - 1.0, 0.0))
    is_expert = lane < N_EXPERTS
    owner = jnp.sum(jnp.where(jnp.logical_and(is_expert, tile_end <= clamped), 1.0, 0.0), axis=-1, keepdims=True)
    owner = jnp.minimum(owner, N_EXPERTS - 1.0)
    mine = lane.astype(F32) == owner
    pick = lambda v: jnp.sum(jnp.where(mine, v, 0.0), axis=-1, keepdims=True)
    t1 = tile[:, 0:1]
    valid = jnp.clip(pick(total) - (t1 - pick(tile_start)) * ROW_TILE, 0.0, float(ROW_TILE))
    valid = jnp.where(t1 < used, valid, 0.0)

    nonempty = jnp.where(jnp.logical_and(lane[0:1] < N_EXPERTS, tiles_e > 0.0), 1.0, 0.0)
    rank = exact(jnp.broadcast_to(nonempty, (8, ROUTE_LANES)), jnp.where(k < lane, 1.0, 0.0))[0:1]
    nonempty_col = jnp.sum(jnp.where(k == lane, jnp.broadcast_to(nonempty, k.shape), 0.0), axis=-1, keepdims=True)
    later = jnp.where(jnp.logical_and(k > lane, nonempty_col > 0.0), tile, float(ROUTE_LANES))
    nxt_e = jnp.min(later, axis=0, keepdims=True)
    nxt_e = jnp.where(nxt_e < float(ROUTE_LANES), nxt_e, -1.0)
    first = jnp.where(jnp.logical_and(t1 == pick(tile_start), t1 < used), 1.0, 0.0)
    my_rank = pick(rank)
    slot = my_rank - 2.0 * jnp.floor(my_rank * 0.5)
    fields = (owner, valid, first, pick(nxt_e), slot, jnp.broadcast_to(used, owner.shape))
    by_tile = sum(jnp.where(lane == n, f, 0.0) for n, f in enumerate(fields))
    meta_ref[...] = by_tile.T[:8, :].astype(I32)

    sel_row = lax.broadcasted_iota(jnp.int32, (8, ROUTE_LANES), 0)
    sel_lane = lax.broadcasted_iota(jnp.int32, (8, ROUTE_LANES), 1)
    tok_r = lax.broadcasted_iota(jnp.int32, (MERGE_SUB, MERGE_SUB), 0)
    tok_c = lax.broadcasted_iota(jnp.int32, (MERGE_SUB, MERGE_SUB), 1)
    eye = jnp.where(tok_r == tok_c, 1.0, 0.0).astype(BF16)
    upto = jnp.where(tok_r <= tok_c, 1.0, 0.0).astype(BF16)
    sum_experts = lambda a: jnp.sum(a, axis=0, keepdims=True)

    def sub_positions(s, carry):
        route = route_ref[pl.ds(pl.multiple_of(s * MERGE_SUB, MERGE_SUB), MERGE_SUB), :].astype(BF16)
        record_t = _dot_tn(route, eye)
        incl_t = _dot_tn(route, upto)[:N_EXPERTS]
        both_t = record_t[:N_EXPERTS]
        second_t = record_t[LANE_SECOND0:LANE_SECOND0 + N_EXPERTS]
        rank2 = sum_experts(second_t * incl_t) - 1.0
        rank1 = sum_experts(both_t * incl_t) - rank2 - 2.0
        base_row = jnp.where(sel_lane < N_EXPERTS, jnp.broadcast_to(base_ref[pl.ds(s, 1), :], (8, ROUTE_LANES)), 0.0)
        hi = jnp.floor(base_row * (1.0 / 256.0))
        lo = base_row - 256.0 * hi
        hi2 = pltpu.roll(hi, LANE_SECOND0, axis=1)
        lo2 = pltpu.roll(lo, LANE_SECOND0, axis=1)
        table = (jnp.where(sel_row == 0, hi, 0.0) + jnp.where(sel_row == 1, lo, 0.0)
                 + jnp.where(sel_row == 2, hi2, 0.0) + jnp.where(sel_row == 3, lo2, 0.0))
        r = _dot_nt(table.astype(BF16), route)
        base_both = 256.0 * r[0:1] + r[1:2]
        base2 = 256.0 * r[2:3] + r[3:4]
        worker = s // subs_per_worker
        chunk0 = lax.rem(s, subs_per_worker) * chunks_per_sub
        for ref, row in ((pos1_ref, base_both - base2 + rank1), (pos2_ref, base2 + rank2)):
            for c in range(chunks_per_sub):
                ref[worker, pl.ds(chunk0 + c, 1), :] = row[:, c * SC_CHUNK:(c + 1) * SC_CHUNK].astype(I32)
        return carry

    chunks_per_sub = MERGE_SUB // SC_CHUNK
    subs_per_worker = nt // SC_WORKERS
    pos1_ref[...] = jnp.zeros_like(pos1_ref)
    pos2_ref[...] = jnp.zeros_like(pos2_ref)
    lax.fori_loop(0, nt, sub_positions, 0, unroll=4)


def _tile_plan(cnt, route, n_tiles):
    assert n_tiles <= ROUTE_LANES
    nt = cnt.shape[0]
    assert nt % SC_WORKERS == 0 and MERGE_SUB % SC_CHUNK == 0 and nt * MERGE_SUB <= SC_WORKERS * SC_IDX_ROWS * SC_CHUNK
    pos = jax.ShapeDtypeStruct((SC_WORKERS, SC_IDX_ROWS, SC_CHUNK), I32)
    _, meta, pos1, pos2 = pl.pallas_call(
        _plan_kernel,
        out_shape=[jax.ShapeDtypeStruct((nt, ROUTE_LANES), F32), jax.ShapeDtypeStruct((8, ROUTE_LANES), I32), pos, pos],
        compiler_params=pltpu.CompilerParams(vmem_limit_bytes=VMEM_LIMIT),
        name="tile_plan",
    )(cnt.reshape(nt, ROUTE_LANES), route)
    return meta, pos1, pos2


def _sc_mesh():
    return plsc.VectorSubcoreMesh(core_axis_name="c", subcore_axis_name="s")


def _sc_worker():
    return lax.axis_index("s") * SC_CORES + lax.axis_index("c")


def _dispatch(hp, pos1, pos2, n_rows):
    t = hp.shape[0]
    per_worker = t // SC_WORKERS
    k = per_worker // SC_CHUNK
    idx = pltpu.VMEM((SC_IDX_ROWS, SC_CHUNK), I32)

    @functools.partial(pl.kernel, mesh=_sc_mesh(), out_type=jax.ShapeDtypeStruct((n_rows, HALF_D), I32),
                       scratch_types=[idx, idx, pltpu.VMEM((SC_CHUNK, HALF_D), I32)], name="moe_dispatch")
    def run(h_hbm, p1_hbm, p2_hbm, xs_hbm, p1_v, p2_v, rows_v):
        wid = _sc_worker()
        pltpu.sync_copy(p1_hbm.at[wid], p1_v)
        pltpu.sync_copy(p2_hbm.at[wid], p2_v)
        for j in range(k):
            pltpu.sync_copy(h_hbm.at[pl.ds(wid * per_worker + j * SC_CHUNK, SC_CHUNK)], rows_v)
            pltpu.sync_copy(rows_v, xs_hbm.at[p1_v.at[j]])
            pltpu.sync_copy(rows_v, xs_hbm.at[p2_v.at[j]])

    return run(hp, pos1, pos2)


def _combine_gather(ys, pos1, pos2, t):
    per_worker = t // SC_WORKERS
    k = per_worker // SC_CHUNK
    idx = pltpu.VMEM((SC_IDX_ROWS, SC_CHUNK), I32)
    out = jax.ShapeDtypeStruct((t, HALF_D), I32)

    @functools.partial(pl.kernel, mesh=_sc_mesh(), out_type=(out, out),
                       scratch_types=[idx, idx, pltpu.VMEM((SC_CHUNK, HALF_D), I32)], name="moe_combine")
    def run(ys_hbm, p1_hbm, p2_hbm, g1_hbm, g2_hbm, p1_v, p2_v, rows_v):
        wid = _sc_worker()
        pltpu.sync_copy(p1_hbm.at[wid], p1_v)
        pltpu.sync_copy(p2_hbm.at[wid], p2_v)
        for j in range(k):
            dst = pl.ds(wid * per_worker + j * SC_CHUNK, SC_CHUNK)
            pltpu.sync_copy(ys_hbm.at[p1_v.at[j]], rows_v)
            pltpu.sync_copy(rows_v, g1_hbm.at[dst])
            pltpu.sync_copy(ys_hbm.at[p2_v.at[j]], rows_v)
            pltpu.sync_copy(rows_v, g2_hbm.at[dst])

    return run(ys, pos1, pos2)


META_OWNER, META_VALID, META_FIRST, META_NEXT, META_SLOT, META_USED = range(6)


def _experts_kernel(meta_ref, xs_hbm, wg_hbm, wu_hbm, wd_hbm, ys_ref,
                    xs_buf, xs_sem, wg_f, wu_f, wd_f, wg_bf, wu_bf, wd_bf, w_sem):
    i = pl.program_id(0)
    used = meta_ref[META_USED, 0]
    valid = meta_ref[META_VALID, i]
    owner = meta_ref[META_OWNER, i]
    nxt = meta_ref[META_NEXT, i]

    def rows_in(tile):
        start = pl.multiple_of(tile * ROW_TILE, ROW_TILE)
        slot = lax.rem(tile, XS_SLOTS)
        return pltpu.make_async_copy(xs_hbm.at[pl.ds(start, ROW_TILE)], xs_buf.at[slot], xs_sem.at[slot])

    @pl.when(i == 0)
    def _():
        for ahead in range(XS_SLOTS - 1):
            pl.when(ahead < used)(rows_in(ahead).start)

    @pl.when(i + (XS_SLOTS - 1) < used)
    def _():
        rows_in(i + (XS_SLOTS - 1)).start()

    def fetch(expert, slot):
        pairs = ((wg_hbm, wg_f), (wu_hbm, wu_f), (wd_hbm, wd_f))
        return [pltpu.make_async_copy(w.at[expert], buf.at[slot], w_sem.at[slot, n])
                for n, (w, buf) in enumerate(pairs)]

    @pl.when(i == 0)
    def _():
        for copy in fetch(owner, 0):
            copy.start()

    @pl.when(meta_ref[META_FIRST, i] == 1)
    def _():
        slot = meta_ref[META_SLOT, i]
        for copy in fetch(owner, slot):
            copy.wait()
        wg_bf[...] = wg_f[slot].astype(BF16)
        wu_bf[...] = wu_f[slot].astype(BF16)
        wd_bf[...] = wd_f[slot].astype(BF16)

        @pl.when(nxt >= 0)
        def _():
            for copy in fetch(nxt, 1 - slot):
                copy.start()

    @pl.when(i < used)
    def _():
        rows_in(i).wait()
        row = lax.broadcasted_iota(jnp.int32, (ROW_TILE, HALF_D), 0)
        words = xs_buf[lax.rem(i, XS_SLOTS)]
        packed = jnp.where(row < valid, words, 0)
        hi, lo = _unpack_halves(packed)
        hi = hi.astype(BF16)
        lo = lo.astype(BF16)
        hg = _dot(hi, wg_bf[:HALF_D, :]) + _dot(lo, wg_bf[HALF_D:, :])
        hu = _dot(hi, wu_bf[:HALF_D, :]) + _dot(lo, wu_bf[HALF_D:, :])
        hid = (hg * _sigmoid(hg) * hu).astype(BF16)
        ys_ref[...] = _pack_halves(_dot(hid, wd_bf[...]))


def _experts(xs, meta, n_tiles, wg, wu, wd):
    tile = lambda i, meta: (jnp.minimum(i, meta[META_USED, 0] - 1), 0)
    hbm = pl.BlockSpec(memory_space=pl.ANY)
    up_shape, down_shape = (D_MODEL, EXPERT_FF), (EXPERT_FF, D_MODEL)
    grid_spec = pltpu.PrefetchScalarGridSpec(
        num_scalar_prefetch=1,
        grid=(n_tiles,),
        in_specs=[hbm, hbm, hbm, hbm],
        out_specs=pl.BlockSpec((ROW_TILE, HALF_D), tile),
        scratch_shapes=[pltpu.VMEM((XS_SLOTS, ROW_TILE, HALF_D), I32), pltpu.SemaphoreType.DMA((XS_SLOTS,)),
                        pltpu.VMEM((2,) + up_shape, F32), pltpu.VMEM((2,) + up_shape, F32),
                        pltpu.VMEM((2,) + down_shape, F32),
                        pltpu.VMEM(up_shape, BF16), pltpu.VMEM(up_shape, BF16), pltpu.VMEM(down_shape, BF16),
                        pltpu.SemaphoreType.DMA((2, 3))],
    )
    return pl.pallas_call(
        _experts_kernel,
        grid_spec=grid_spec,
        out_shape=jax.ShapeDtypeStruct((n_tiles * ROW_TILE, HALF_D), I32),
        compiler_params=pltpu.CompilerParams(dimension_semantics=("arbitrary",),
                                             vmem_limit_bytes=VMEM_LIMIT),
        name="experts",
    )(meta, xs, wg, wu, wd)


def _ple_kernel(x1_hbm, g1_hbm, g2_hbm, route_ref, p_ref, wp_f32, pg_ref, gg_ref, wgate_f32, o_ref,
                wp_ref, wgate_ref, x1_buf, g1_buf, g2_buf, ring_sem):
    i = pl.program_id(0)
    steps = pl.num_programs(0)
    streams = ((x1_hbm, x1_buf), (g1_hbm, g1_buf), (g2_hbm, g2_buf))

    def rows_in(step):
        start = pl.multiple_of(step * PLE_TM, PLE_TM)
        slot = lax.rem(step, PLE_SLOTS)
        return [pltpu.make_async_copy(src.at[pl.ds(start, PLE_TM)], buf.at[slot], ring_sem.at[n, slot])
                for n, (src, buf) in enumerate(streams)]

    def start_rows(step):
        for copy in rows_in(step):
            copy.start()

    @pl.when(i == 0)
    def _():
        for ahead in range(PLE_SLOTS - 1):
            pl.when(ahead < steps)(functools.partial(start_rows, ahead))
        wp_ref[...] = wp_f32[...].astype(BF16)
        wgate_ref[...] = wgate_f32[...].astype(BF16)

    pl.when(i + (PLE_SLOTS - 1) < steps)(functools.partial(start_rows, i + (PLE_SLOTS - 1)))

    for copy in rows_in(i):
        copy.wait()
    slot = lax.rem(i, PLE_SLOTS)
    x1_ref, g1_ref, g2_ref = x1_buf.at[slot], g1_buf.at[slot], g2_buf.at[slot]

    route = route_ref[...]
    lane = lax.broadcasted_iota(jnp.int32, route.shape, 1)
    pick = lambda ln: jnp.sum(jnp.where(lane == ln, route, 0.0), axis=-1, keepdims=True)
    c1 = pick(LANE_C1)
    c2 = pick(LANE_C2)
    a_hi, a_lo = _unpack_halves(g1_ref[...])
    b_hi, b_lo = _unpack_halves(g2_ref[...])
    x = x1_ref[...] + jnp.concatenate([c1 * a_hi + c2 * b_hi, c1 * a_lo + c2 * b_lo], axis=1)
    ple = _rms(_dot(p_ref[...].astype(BF16), wp_ref[...]), pg_ref[...])
    gate = _sigmoid(_dot(_rms(x, gg_ref[...]).astype(BF16), wgate_ref[...]))
    o_ref[...] = x + gate * ple


def _ple(x1, g1, g2, route, p2d, w_ple, ple_g, gate_g, w_gate):
    t = x1.shape[0]
    row = lambda w: pl.BlockSpec((PLE_TM, w), lambda i: (i, 0))
    full = lambda a: pl.BlockSpec(a.shape, lambda i: (0,) * a.ndim)
    hbm = pl.BlockSpec(memory_space=pl.ANY)
    return pl.pallas_call(
        _ple_kernel,
        grid=(t // PLE_TM,),
        in_specs=[hbm, hbm, hbm, row(ROUTE_LANES), row(PLE_DIM),
                  full(w_ple), full(ple_g), full(gate_g), full(w_gate)],
        out_specs=row(D_MODEL),
        out_shape=jax.ShapeDtypeStruct((t, D_MODEL), F32),
        scratch_shapes=[pltpu.VMEM(w_ple.shape, BF16), pltpu.VMEM(w_gate.shape, BF16),
                        pltpu.VMEM((PLE_SLOTS, PLE_TM, D_MODEL), F32),
                        pltpu.VMEM((PLE_SLOTS, PLE_TM, HALF_D), I32), pltpu.VMEM((PLE_SLOTS, PLE_TM, HALF_D), I32),
                        pltpu.SemaphoreType.DMA((3, PLE_SLOTS))],
        compiler_params=pltpu.CompilerParams(dimension_semantics=("arbitrary",),
                                             vmem_limit_bytes=VMEM_LIMIT),
        name="ple",
    )(x1, g1, g2, route, p2d, w_ple, ple_g, gate_g, w_gate)


def kernel(x, p, mix_norm_g, w_in, ret_gn_g, ret_gn_b, w_ret_o, q_norm_g, k_norm_g, attn_sinks,
           w_swa_o, w_out, ffn_norm_g, w_router_group, b_router_group, w_router_expert,
           b_router_expert, w_exp_gate, w_exp_up, w_exp_down, ple_gate_norm_g, w_ple_gate,
           w_ple, ple_norm_g):
    batch, seq, d = x.shape
    t = batch * seq
    depth = w_in.shape[0]
    n_tiles = 2 * t // ROW_TILE + N_EXPERTS
    row = lambda a: a.reshape(1, -1)
    x2d = x.reshape(t, d)
    for i in range(depth):
        pad = ROUTE_LANES - N_GROUPS - N_EXPERTS
        w_router = jnp.pad(jnp.concatenate([w_router_expert[i], w_router_group[i]], axis=1), ((0, 0), (0, pad)))
        b_router = jnp.pad(jnp.concatenate([b_router_expert[i], b_router_group[i]]), (0, pad)).reshape(1, -1)
        w_router = w_router.astype(BF16)
        x1, hp, route, cnt = _block(x2d, row(mix_norm_g[i]), w_in[i], row(ret_gn_g[i]),
                                    row(ret_gn_b[i]), attn_sinks[i], q_norm_g[i], k_norm_g[i],
                                    w_ret_o[i], w_swa_o[i], w_out[i],
                                    row(ffn_norm_g[i]), w_router, b_router, seq)
        meta, pos1, pos2 = _tile_plan(cnt, route, n_tiles)
        xs = _dispatch(hp, pos1, pos2, n_tiles * ROW_TILE)
        ys = _experts(xs, meta, n_tiles, w_exp_gate[i], w_exp_up[i], w_exp_down[i])
        g1, g2 = _combine_gather(ys, pos1, pos2, t)
        x2d = _ple(x1, g1, g2, route, p[i].reshape(t, PLE_DIM), w_ple[i], row(ple_norm_g[i]),
                   row(ple_gate_norm_g[i]), w_ple_gate[i])
    return x2d.reshape(batch, seq, d)
```

```python
import functools

import jax
import jax.numpy as jnp
import numpy as np
from jax import lax
from jax.experimental import pallas as pl
from jax.experimental.pallas import tpu as pltpu
from jax.experimental.pallas import tpu_sc as plsc

F32 = jnp.float32
BF16 = jnp.bfloat16
I32 = jnp.int32

EPS = 1e-6
D_MODEL = 1024
PLE_DIM = 256
RET_HEADS = 4
RET_DK = 128
RET_DV = 128
RET_CHUNK = 128
SWA_HEADS = 8
SWA_KV_HEADS = 2
SWA_GROUP = SWA_HEADS // SWA_KV_HEADS
SWA_HD = 64
WINDOW = 128
N_GROUPS = 4
EXPERTS_PER_GROUP = 8
N_EXPERTS = N_GROUPS * EXPERTS_PER_GROUP
EXPERT_FF = 256

RET_W = RET_HEADS * RET_DK
SWA_Q = SWA_HEADS * SWA_HD
SWA_KV = SWA_KV_HEADS * SWA_HD
Z_RET_W = 4 * RET_W
Z_SKV_W = 2 * SWA_KV
Z_GATE_W = 2 * D_MODEL
COL_SQ = Z_RET_W
COL_SKV = COL_SQ + SWA_Q
COL_GATE = COL_SKV + Z_SKV_W
IN_WIDTH = COL_GATE + Z_GATE_W

ROUTE_LANES = 128
LANE_GROUP0 = N_EXPERTS
LANE_C1, LANE_C2 = 34, 35
LANE_SECOND0 = 64
NEG_INF = -1e30

HALF_D = D_MODEL // 2
ROW_TILE = 512
XS_SLOTS = 4
SC_CORES = 2
SC_SUBCORES = 16
SC_WORKERS = SC_CORES * SC_SUBCORES
SC_CHUNK = 128
SC_IDX_ROWS = 8

VMEM_LIMIT = 56 * 1024 * 1024

BLOCK_ROWS = 512
W_SLAB = 256
RET_ROWS = 512
MERGE_SUB = 256
PLE_TM = 1024
PLE_SLOTS = 3


def _rms(x, g):
    ms = jnp.mean(x * x, axis=-1, keepdims=True)
    return x * lax.rsqrt(ms + EPS) * g


def _sigmoid(x):
    return 1.0 / (1.0 + jnp.exp(-x))


def _dot(a, b):
    return jnp.dot(a, b, preferred_element_type=F32)


def _dot_nt(a, b):
    return lax.dot_general(a, b, (((1,), (1,)), ((), ())), preferred_element_type=F32)


def _dot_tn(a, b):
    return lax.dot_general(a, b, (((0,), (0,)), ((), ())), preferred_element_type=F32)


def _pack_halves(v):
    return pltpu.pack_elementwise([v[:, :HALF_D], v[:, HALF_D:]], packed_dtype=BF16)


def _unpack_halves(p):
    words = lax.bitcast_convert_type(p, jnp.uint32)
    return tuple(pltpu.unpack_elementwise(words, index=k, packed_dtype=BF16, unpacked_dtype=F32) for k in range(2))


def _retention_tables():
    f32 = np.float32
    h = RET_HEADS
    c = RET_CHUNK
    log_gamma = np.log1p(-np.exp2(f32(-5.0) - np.arange(h, dtype=f32))).astype(f32)
    pos = np.arange(c, dtype=f32)
    diff = pos[:, None] - pos[None, :]
    decay = np.where(diff[None] >= 0.0, np.exp(np.maximum(diff, f32(0.0))[None] * log_gamma[:, None, None]), f32(0.0))
    scale = f32(RET_DK ** -0.5)
    dmask = decay * scale
    zeta = np.exp((f32(c - 1.0) - pos)[None, :] * log_gamma[:, None]) * scale
    xi = np.exp((pos + f32(1.0))[None, :] * log_gamma[:, None])
    cdec = np.exp(f32(c) * log_gamma)
    bc = lambda a: np.ascontiguousarray(np.broadcast_to(a[:, :, None], (h, c, c)), dtype=f32)
    return (dmask.astype(f32), bc(zeta), bc(xi),
            np.ascontiguousarray(np.broadcast_to(cdec[:, None, None], (h, c, c)), dtype=f32))


def _swa_tables(q_g, k_g):
    f32 = np.float32
    qi = np.arange(WINDOW)[:, None]
    sj = np.arange(2 * WINDOW)[None, :]
    rel = qi + WINDOW - sj
    ok = (rel >= 0) & (rel < WINDOW)
    slopes = np.exp2(f32(-8.0) * np.arange(1, SWA_HEADS + 1, dtype=f32) / f32(SWA_HEADS)).astype(f32)
    bias = np.where(ok[None], -slopes[:, None, None] * rel.astype(f32)[None], f32(NEG_INF)).astype(f32)
    bias = bias.reshape(SWA_KV_HEADS, SWA_GROUP * WINDOW, 2 * WINDOW)
    avg = lambda heads: jnp.asarray(np.kron(np.eye(heads, dtype=f32), np.full((SWA_HD, SWA_HD), 1.0 / SWA_HD, f32)),
                                    dtype=BF16)
    qg = jnp.tile(q_g.reshape(1, SWA_HD), (1, SWA_HEADS)) * (SWA_HD ** -0.5)
    kg = jnp.tile(k_g.reshape(1, SWA_HD), (1, SWA_KV_HEADS))
    return qg, kg, avg(SWA_HEADS), avg(SWA_KV_HEADS), jnp.asarray(bias)


def _block_kernel(per_seq, sinks_ref, x_ref, xp_ref, g_ref, w_hbm, dmask_ref, zeta_ref, xi_ref, cdec_ref,
                  gng_ref, gnb_ref, qg_ref, kg_ref, bq_ref, bk_ref, bias_ref,
                  wro_hbm, wso_hbm, wout_hbm, fg_ref, wr_ref, br_ref,
                  x1_ref, hp_ref, route_ref, cnt_ref,
                  w_ref, wro_ref, wso_ref, wout_ref, wtmp_ref, w_sem,
                  zc_ref, zn_ref, yr_ref, ys_ref, state_ref, prev_ref, band_ref, tail_ref):
    g = pl.program_id(0)
    seq_start = lax.rem(g - 1, per_seq) == 0

    @pl.when(g == 0)
    def _():
        slabs = [(src, dst, c0) for src, dst in ((w_hbm, w_ref), (wro_hbm, wro_ref), (wso_hbm, wso_ref),
                                                 (wout_hbm, wout_ref))
                 for c0 in range(0, dst.shape[1], W_SLAB)]

        def slab(k):
            src, dst, c0 = slabs[k]
            return pltpu.make_async_copy(src.at[:, pl.ds(c0, W_SLAB)], wtmp_ref.at[k % 2, pl.ds(0, dst.shape[0])],
                                         w_sem.at[k % 2])

        slab(0).start()
        for k, (_, dst, c0) in enumerate(slabs):
            if k + 1 < len(slabs):
                slab(k + 1).start()
            slab(k).wait()
            dst[:, c0:c0 + W_SLAB] = wtmp_ref[k % 2, :dst.shape[0]].astype(BF16)
        zc_ref[...] = jnp.zeros_like(zc_ref)
        tail_ref[...] = jnp.zeros_like(tail_ref)
        state_ref[...] = jnp.zeros_like(state_ref)

    @pl.when(seq_start)
    def _():
        state_ref[...] = jnp.zeros_like(state_ref)

    band_ref[:WINDOW, :] = tail_ref[...]
    band_ref[WINDOW:, :] = zc_ref[:, COL_SKV:COL_GATE]

    h = _rms(x_ref[...], g_ref[...]).astype(BF16)

    def project(c0, width):
        zn_ref[:, c0:c0 + width] = _dot(h, w_ref[:, c0:c0 + width]).astype(BF16)

    cw = 512
    proj_pieces = [functools.partial(project, j * cw, cw) for j in range(COL_SKV // cw)]
    proj_pieces.append(functools.partial(project, COL_SKV, Z_SKV_W))
    proj_pieces += [functools.partial(project, COL_GATE + j * cw, cw) for j in range(Z_GATE_W // cw)]

    def retention_rows(t):
        r_base = t * RET_ROWS
        n_chunks = RET_ROWS // RET_CHUNK
        part = lambda c, p, hd: zc_ref[r_base + c * RET_CHUNK:r_base + (c + 1) * RET_CHUNK,
                                       p * RET_W + hd * RET_DK:p * RET_W + (hd + 1) * RET_DK]
        for hd in range(RET_HEADS):
            state = state_ref[hd]
            for c in range(n_chunks):
                prev_ref[t * n_chunks + c, hd] = state.astype(BF16)
                kz = (part(c, 1, hd).astype(F32) * zeta_ref[hd]).astype(BF16)
                state = cdec_ref[hd] * state + _dot_tn(kz, part(c, 2, hd))
            state_ref[hd] = state
        out_rows = []
        for c in range(n_chunks):
            heads = []
            for hd in range(RET_HEADS):
                q = part(c, 0, hd)
                scores = _dot_nt(q, part(c, 1, hd)) * dmask_ref[hd]
                y = _dot(scores.astype(BF16), part(c, 2, hd)) + _dot(q, prev_ref[t * n_chunks + c, hd]) * xi_ref[hd]
                mu = jnp.mean(y, axis=-1, keepdims=True)
                d = y - mu
                var = jnp.mean(d * d, axis=-1, keepdims=True)
                hs = slice(hd * RET_DV, (hd + 1) * RET_DV)
                yn = d * lax.rsqrt(var + EPS) * gng_ref[:, hs] + gnb_ref[:, hs]
                gate = part(c, 3, hd).astype(F32)
                heads.append((gate * _sigmoid(gate) * yn).astype(BF16))
            out_rows.append(jnp.concatenate(heads, axis=1))
        yr_ref[r_base:r_base + RET_ROWS, :] = jnp.concatenate(out_rows, axis=0)

    group_rows = SWA_GROUP * WINDOW
    lo_q = lax.broadcasted_iota(jnp.int32, (WINDOW, 2 * SWA_HD), 1) < SWA_HD
    lo_k = lax.broadcasted_iota(jnp.int32, (2 * WINDOW, 2 * SWA_HD), 1) < SWA_HD
    in_prev = lax.broadcasted_iota(jnp.int32, (group_rows, 2 * WINDOW), 1) < WINDOW
    row_head = lax.broadcasted_iota(jnp.int32, (group_rows, 1), 0) // WINDOW

    def both_heads(v):
        swapped = pltpu.roll(v, SWA_HD, axis=1)
        return jnp.where(lo_k, v, swapped), jnp.where(lo_k, swapped, v)

    def swa_block(n):
        rows = slice(n * WINDOW, (n + 1) * WINDOW)
        kv = band_ref[n * WINDOW:(n + 2) * WINDOW, :]
        kf = kv[:, :SWA_KV].astype(F32)
        kn = kf * lax.rsqrt(_dot((kf * kf).astype(BF16), bk_ref[...]) + EPS) * kg_ref[...]
        keys = [a.astype(BF16) for a in both_heads(kn)]
        vals = [a.astype(BF16) for a in both_heads(kv[:, SWA_KV:].astype(F32))]
        qf = zc_ref[rows, COL_SQ:COL_SKV].astype(F32)
        qn = qf * lax.rsqrt(_dot((qf * qf).astype(BF16), bq_ref[...]) + EPS) * qg_ref[...]
        outs = []
        for kh in range(SWA_KV_HEADS):
            parts = []
            for j in range(SWA_GROUP // 2):
                c0 = (kh * SWA_GROUP + 2 * j) * SWA_HD
                two = qn[:, c0:c0 + 2 * SWA_HD]
                parts += [jnp.where(lo_q, two, 0.0), jnp.where(lo_q, 0.0, two)]
            qs = jnp.concatenate(parts, axis=0).astype(BF16)
            s = _dot_nt(qs, keys[kh]) + bias_ref[kh]
            if n == 0:
                s = jnp.where(jnp.logical_and(seq_start, in_prev), NEG_INF, s)
            sink = sinks_ref[kh * SWA_GROUP]
            for gi in range(1, SWA_GROUP):
                sink = jnp.where(row_head == gi, sinks_ref[kh * SWA_GROUP + gi], sink)
            m = jnp.maximum(jnp.max(s, axis=-1, keepdims=True), sink)
            p = jnp.exp(s - m)
            denom = jnp.sum(p, axis=-1, keepdims=True) + jnp.exp(sink - m)
            o = _dot(p.astype(BF16), vals[kh]) * (1.0 / denom)
            for j in range(SWA_GROUP // 2):
                even = o[(2 * j) * WINDOW:(2 * j + 1) * WINDOW]
                odd = o[(2 * j + 1) * WINDOW:(2 * j + 2) * WINDOW]
                outs.append(jnp.where(lo_q, even, odd).astype(BF16))
        ys_ref[rows, :] = jnp.concatenate(outs, axis=1)

    mix_pieces = [functools.partial(retention_rows, t) for t in range(BLOCK_ROWS // RET_ROWS)]
    mix_pieces += [functools.partial(swa_block, n) for n in range(BLOCK_ROWS // WINDOW)]

    subs = []

    def merge_sub(s):
        subs.append(_merge_rows(slice(s * MERGE_SUB, (s + 1) * MERGE_SUB), yr_ref, ys_ref, zc_ref, xp_ref, wro_ref,
                                wso_ref, wout_ref, fg_ref, wr_ref, br_ref))

    other_pieces = mix_pieces + [functools.partial(merge_sub, s) for s in range(BLOCK_ROWS // MERGE_SUB)]

    n_other = len(other_pieces)
    for k, piece in enumerate(other_pieces):
        for proj_piece in proj_pieces[k * len(proj_pieces) // n_other:(k + 1) * len(proj_pieces) // n_other]:
            proj_piece()
        piece()
    x1_ref[...] = jnp.concatenate([s[0] for s in subs], axis=0)
    hp_ref[...] = jnp.concatenate([s[1] for s in subs], axis=0)
    route_ref[...] = jnp.concatenate([s[2] for s in subs], axis=0)
    for s, sub in enumerate(subs):
        cnt_ref[s] = sub[3]

    tail_ref[...] = zc_ref[BLOCK_ROWS - WINDOW:, COL_SKV:COL_GATE]
    zc_ref[...] = zn_ref[...]


def _merge_rows(rows, yr_ref, ys_ref, z_ref, x_ref, wro_ref, wso_ref, wout_ref, fg_ref, wr_ref, br_ref):
    a = _dot(yr_ref[rows, :], wro_ref[...])
    b = _dot(ys_ref[rows, :], wso_ref[...])
    gate_r = z_ref[rows, COL_GATE:COL_GATE + D_MODEL].astype(F32)
    gate_s = z_ref[rows, COL_GATE + D_MODEL:].astype(F32)
    merged = (_sigmoid(gate_r) * a + _sigmoid(gate_s) * b).astype(BF16)
    x1 = x_ref[rows, :] + _dot(merged, wout_ref[...])
    h2 = _rms(x1, fg_ref[...])
    packed = _pack_halves(h2)

    logits = _dot(h2.astype(BF16), wr_ref[...]) + br_ref[...]
    lane = lax.broadcasted_iota(jnp.int32, logits.shape, 1)
    big = jnp.int32(ROUTE_LANES)
    is_group = jnp.logical_and(lane >= LANE_GROUP0, lane < LANE_GROUP0 + N_GROUPS)
    gl = jnp.where(is_group, logits, NEG_INF)
    gmax = jnp.max(gl, axis=-1, keepdims=True)
    g_w = 1.0 / jnp.sum(jnp.exp(gl - gmax), axis=-1, keepdims=True)
    g_sel = jnp.min(jnp.where(gl == gmax, lane, big), axis=-1, keepdims=True) - LANE_GROUP0
    in_group = jnp.logical_and(lane < N_EXPERTS, (lane >> 3) == g_sel)
    el = jnp.where(in_group, logits, NEG_INF)
    m1 = jnp.max(el, axis=-1, keepdims=True)
    i1 = jnp.min(jnp.where(el == m1, lane, big), axis=-1, keepdims=True)
    el2 = jnp.where(lane == i1, NEG_INF, el)
    m2 = jnp.max(el2, axis=-1, keepdims=True)
    i2 = jnp.min(jnp.where(el2 == m2, lane, big), axis=-1, keepdims=True)
    e2 = jnp.exp(m2 - m1)
    c1 = g_w / (1.0 + e2)
    c2 = g_w * e2 / (1.0 + e2)
    chosen = jnp.where(jnp.logical_or(lane == i1, lane == i2), 1.0, 0.0)
    route = (chosen
             + jnp.where(lane == LANE_C1, c1, 0.0)
             + jnp.where(lane == LANE_C2, c2, 0.0)
             + jnp.where(lane == i2 + LANE_SECOND0, 1.0, 0.0))
    return x1, packed, route, jnp.sum(chosen, axis=0, keepdims=True)


def _block(x2d, mix_g, w_in, gn_g, gn_b, sinks, q_g, k_g, w_ret_o, w_swa_o, w_out, ffn_g, w_router, b_router,
           seq):
    t = x2d.shape[0]
    n = t // BLOCK_ROWS
    assert seq % BLOCK_ROWS == 0
    tables = _retention_tables()
    swa_tables = _swa_tables(q_g, k_g)
    consts = tables + (gn_g, gn_b) + swa_tables
    merge_weights = (w_ret_o, w_swa_o, w_out)
    small = (ffn_g, w_router, b_router)
    hbm = pl.BlockSpec(memory_space=pl.ANY)
    full = lambda a: pl.BlockSpec(a.shape, lambda g: (0,) * a.ndim)
    cur = lambda w: pl.BlockSpec((BLOCK_ROWS, w), lambda g: (jnp.minimum(g, n - 1), 0))
    prv = lambda w: pl.BlockSpec((BLOCK_ROWS, w), lambda g: (jnp.maximum(g - 1, 0), 0))
    chunks = BLOCK_ROWS // RET_CHUNK
    subs_per_block = BLOCK_ROWS // MERGE_SUB
    return pl.pallas_call(
        functools.partial(_block_kernel, seq // BLOCK_ROWS),
        grid=(n + 1,),
        in_specs=([pl.BlockSpec(memory_space=pltpu.SMEM), cur(D_MODEL), prv(D_MODEL), full(mix_g), hbm]
                  + [full(a) for a in consts] + [hbm] * len(merge_weights) + [full(a) for a in small]),
        out_specs=[prv(D_MODEL), prv(HALF_D), prv(ROUTE_LANES),
                   pl.BlockSpec((subs_per_block, 1, ROUTE_LANES), lambda g: (jnp.maximum(g - 1, 0), 0, 0))],
        out_shape=[jax.ShapeDtypeStruct((t, D_MODEL), F32),
                   jax.ShapeDtypeStruct((t, HALF_D), I32),
                   jax.ShapeDtypeStruct((t, ROUTE_LANES), F32),
                   jax.ShapeDtypeStruct((n * subs_per_block, 1, ROUTE_LANES), F32)],
        scratch_shapes=[pltpu.VMEM(w_in.shape, BF16)] + [pltpu.VMEM(w.shape, BF16) for w in merge_weights]
                       + [pltpu.VMEM((2, D_MODEL, W_SLAB), F32), pltpu.SemaphoreType.DMA((2,)),
                        pltpu.VMEM((BLOCK_ROWS, IN_WIDTH), BF16), pltpu.VMEM((BLOCK_ROWS, IN_WIDTH), BF16),
                        pltpu.VMEM((BLOCK_ROWS, RET_W), BF16), pltpu.VMEM((BLOCK_ROWS, SWA_Q), BF16),
                        pltpu.VMEM((RET_HEADS, RET_DK, RET_DV), F32),
                        pltpu.VMEM((chunks, RET_HEADS, RET_DK, RET_DV), BF16),
                        pltpu.VMEM((BLOCK_ROWS + WINDOW, Z_SKV_W), BF16), pltpu.VMEM((WINDOW, Z_SKV_W), BF16)],
        compiler_params=pltpu.CompilerParams(dimension_semantics=("arbitrary",),
                                             vmem_limit_bytes=VMEM_LIMIT),
        name="block",
    )(sinks, x2d, x2d, mix_g, w_in, *consts, *merge_weights, *small)


def _plan_kernel(cnt_ref, route_ref, base_ref, meta_ref, pos1_ref, pos2_ref):
    counts = cnt_ref[...]
    nt = counts.shape[0]
    exact = functools.partial(jnp.dot, precision=lax.Precision.HIGHEST, preferred_element_type=F32)
    total = jnp.sum(counts, axis=0, keepdims=True)
    tiles_e = jnp.ceil(total * (1.0 / ROW_TILE))
    k = lax.broadcasted_iota(jnp.int32, (ROUTE_LANES, ROUTE_LANES), 0)
    lane = lax.broadcasted_iota(jnp.int32, (ROUTE_LANES, ROUTE_LANES), 1)
    tile_end = exact(jnp.broadcast_to(tiles_e, (8, ROUTE_LANES)), jnp.where(k <= lane, 1.0, 0.0))[0:1]
    tile_start = tile_end - tiles_e
    r = lax.broadcasted_iota(jnp.int32, (nt, nt), 0)
    c = lax.broadcasted_iota(jnp.int32, (nt, nt), 1)
    before = exact(jnp.where(c < r, 1.0, 0.0), counts)
    base_ref[...] = tile_start * ROW_TILE + before

    used = jnp.max(tile_end, axis=-1, keepdims=True)
    tile = k.astype(F32)
    clamped = jnp.minimum(tile, jnp.maximum(used - 1.0, 0.0))
    is_expert = lane < N_EXPERTS
    owner = jnp.sum(jnp.where(jnp.logical_and(is_expert, tile_end <= clamped), 1.0, 0.0), axis=-1, keepdims=True)
    owner = jnp.minimum(owner, N_EXPERTS - 1.0)
    mine = lane.astype(F32) == owner
    pick = lambda v: jnp.sum(jnp.where(mine, v, 0.0), axis=-1, keepdims=True)
    t1 = tile[:, 0:1]
    valid = jnp.clip(pick(total) - (t1 - pick(tile_start)) * ROW_TILE, 0.0, float(ROW_TILE))
    valid = jnp.where(t1 < used, valid, 0.0)

    nonempty = jnp.where(jnp.logical_and(lane[0:1] < N_EXPERTS, tiles_e > 0.0), 1.0, 0.0)
    rank = exact(jnp.broadcast_to(nonempty, (8, ROUTE_LANES)), jnp.where(k < lane, 1.0, 0.0))[0:1]
    nonempty_col = jnp.sum(jnp.where(k == lane, jnp.broadcast_to(nonempty, k.shape), 0.0), axis=-1, keepdims=True)
    later = jnp.where(jnp.logical_and(k > lane, nonempty_col > 0.0), tile, float(ROUTE_LANES))
    nxt_e = jnp.min(later, axis=0, keepdims=True)
    nxt_e = jnp.where(nxt_e < float(ROUTE_LANES), nxt_e, -1.0)
    first = jnp.where(jnp.logical_and(t1 == pick(tile_start), t1 < used), 1.0, 0.0)
    my_rank = pick(rank)
    slot = my_rank - 2.0 * jnp.floor(my_rank * 0.5)
    fields = (owner, valid, first, pick(nxt_e), slot, jnp.broadcast_to(used, owner.shape))
    by_tile = sum(jnp.where(lane == n, f, 0.0) for n, f in enumerate(fields))
    meta_ref[...] = by_tile.T[:8, :].astype(I32)

    sel_row = lax.broadcasted_iota(jnp.int32, (8, ROUTE_LANES), 0)
    sel_lane = lax.broadcasted_iota(jnp.int32, (8, ROUTE_LANES), 1)
    tok_r = lax.broadcasted_iota(jnp.int32, (MERGE_SUB, MERGE_SUB), 0)
    tok_c = lax.broadcasted_iota(jnp.int32, (MERGE_SUB, MERGE_SUB), 1)
    eye = jnp.where(tok_r == tok_c, 1.0, 0.0).astype(BF16)
    upto = jnp.where(tok_r <= tok_c, 1.0, 0.0).astype(BF16)
    sum_experts = lambda a: jnp.sum(a, axis=0, keepdims=True)

    def sub_positions(s, carry):
        route = route_ref[pl.ds(pl.multiple_of(s * MERGE_SUB, MERGE_SUB), MERGE_SUB), :].astype(BF16)
        record_t = _dot_tn(route, eye)
        incl_t = _dot_tn(route, upto)[:N_EXPERTS]
        both_t = record_t[:N_EXPERTS]
        second_t = record_t[LANE_SECOND0:LANE_SECOND0 + N_EXPERTS]
        rank2 = sum_experts(second_t * incl_t) - 1.0
        rank1 = sum_experts(both_t * incl_t) - rank2 - 2.0
        base_row = jnp.where(sel_lane < N_EXPERTS, jnp.broadcast_to(base_ref[pl.ds(s, 1), :], (8, ROUTE_LANES)), 0.0)
        hi = jnp.floor(base_row * (1.0 / 256.0))
        lo = base_row - 256.0 * hi
        hi2 = pltpu.roll(hi, LANE_SECOND0, axis=1)
        lo2 = pltpu.roll(lo, LANE_SECOND0, axis=1)
        table = (jnp.where(sel_row == 0, hi, 0.0) + jnp.where(sel_row == 1, lo, 0.0)
                 + jnp.where(sel_row == 2, hi2, 0.0) + jnp.where(sel_row == 3, lo2, 0.0))
        r = _dot_nt(table.astype(BF16), route)
        base_both = 256.0 * r[0:1] + r[1:2]
        base2 = 256.0 * r[2:3] + r[3:4]
        worker = s // subs_per_worker
        chunk0 = lax.rem(s, subs_per_worker) * chunks_per_sub
        for ref, row in ((pos1_ref, base_both - base2 + rank1), (pos2_ref, base2 + rank2)):
            for c in range(chunks_per_sub):
                ref[worker, pl.ds(chunk0 + c, 1), :] = row[:, c * SC_CHUNK:(c + 1) * SC_CHUNK].astype(I32)
        return carry

    chunks_per_sub = MERGE_SUB // SC_CHUNK
    subs_per_worker = nt // SC_WORKERS
    pos1_ref[...] = jnp.zeros_like(pos1_ref)
    pos2_ref[...] = jnp.zeros_like(pos2_ref)
    lax.fori_loop(0, nt, sub_positions, 0, unroll=4)


def _tile_plan(cnt, route, n_tiles):
    assert n_tiles <= ROUTE_LANES
    nt = cnt.shape[0]
    assert nt % SC_WORKERS == 0 and MERGE_SUB % SC_CHUNK == 0 and nt * MERGE_SUB <= SC_WORKERS * SC_IDX_ROWS * SC_CHUNK
    pos = jax.ShapeDtypeStruct((SC_WORKERS, SC_IDX_ROWS, SC_CHUNK), I32)
    _, meta, pos1, pos2 = pl.pallas_call(
        _plan_kernel,
        out_shape=[jax.ShapeDtypeStruct((nt, ROUTE_LANES), F32), jax.ShapeDtypeStruct((8, ROUTE_LANES), I32), pos, pos],
        compiler_params=pltpu.CompilerParams(vmem_limit_bytes=VMEM_LIMIT),
        name="tile_plan",
    )(cnt.reshape(nt, ROUTE_LANES), route)
    return meta, pos1, pos2


def _sc_mesh():
    return plsc.VectorSubcoreMesh(core_axis_name="c", subcore_axis_name="s")


def _sc_worker():
    return lax.axis_index("s") * SC_CORES + lax.axis_index("c")


def _dispatch(hp, pos1, pos2, n_rows):
    t = hp.shape[0]
    per_worker = t // SC_WORKERS
    k = per_worker // SC_CHUNK
    idx = pltpu.VMEM((SC_IDX_ROWS, SC_CHUNK), I32)

    @functools.partial(pl.kernel, mesh=_sc_mesh(), out_type=jax.ShapeDtypeStruct((n_rows, HALF_D), I32),
                       scratch_types=[idx, idx, pltpu.VMEM((SC_CHUNK, HALF_D), I32)], name="moe_dispatch")
    def run(h_hbm, p1_hbm, p2_hbm, xs_hbm, p1_v, p2_v, rows_v):
        wid = _sc_worker()
        pltpu.sync_copy(p1_hbm.at[wid], p1_v)
        pltpu.sync_copy(p2_hbm.at[wid], p2_v)
        for j in range(k):
            pltpu.sync_copy(h_hbm.at[pl.ds(wid * per_worker + j * SC_CHUNK, SC_CHUNK)], rows_v)
            pltpu.sync_copy(rows_v, xs_hbm.at[p1_v.at[j]])
            pltpu.sync_copy(rows_v, xs_hbm.at[p2_v.at[j]])

    return run(hp, pos1, pos2)


def _combine_gather(ys, pos1, pos2, t):
    per_worker = t // SC_WORKERS
    k = per_worker // SC_CHUNK
    idx = pltpu.VMEM((SC_IDX_ROWS, SC_CHUNK), I32)
    out = jax.ShapeDtypeStruct((t, HALF_D), I32)

    @functools.partial(pl.kernel, mesh=_sc_mesh(), out_type=(out, out),
                       scratch_types=[idx, idx, pltpu.VMEM((SC_CHUNK, HALF_D), I32)], name="moe_combine")
    def run(ys_hbm, p1_hbm, p2_hbm, g1_hbm, g2_hbm, p1_v, p2_v, rows_v):
        wid = _sc_worker()
        pltpu.sync_copy(p1_hbm.at[wid], p1_v)
        pltpu.sync_copy(p2_hbm.at[wid], p2_v)
        for j in range(k):
            dst = pl.ds(wid * per_worker + j * SC_CHUNK, SC_CHUNK)
            pltpu.sync_copy(ys_hbm.at[p1_v.at[j]], rows_v)
            pltpu.sync_copy(rows_v, g1_hbm.at[dst])
            pltpu.sync_copy(ys_hbm.at[p2_v.at[j]], rows_v)
            pltpu.sync_copy(rows_v, g2_hbm.at[dst])

    return run(ys, pos1, pos2)


META_OWNER, META_VALID, META_FIRST, META_NEXT, META_SLOT, META_USED = range(6)


def _experts_kernel(meta_ref, xs_hbm, wg_hbm, wu_hbm, wd_hbm, ys_ref,
                    xs_buf, xs_sem, wg_f, wu_f, wd_f, wg_bf, wu_bf, wd_bf, w_sem):
    i = pl.program_id(0)
    used = meta_ref[META_USED, 0]
    valid = meta_ref[META_VALID, i]
    owner = meta_ref[META_OWNER, i]
    nxt = meta_ref[META_NEXT, i]

    def rows_in(tile):
        start = pl.multiple_of(tile * ROW_TILE, ROW_TILE)
        slot = lax.rem(tile, XS_SLOTS)
        return pltpu.make_async_copy(xs_hbm.at[pl.ds(start, ROW_TILE)], xs_buf.at[slot], xs_sem.at[slot])

    @pl.when(i == 0)
    def _():
        for ahead in range(XS_SLOTS - 1):
            pl.when(ahead < used)(rows_in(ahead).start)

    @pl.when(i + (XS_SLOTS - 1) < used)
    def _():
        rows_in(i + (XS_SLOTS - 1)).start()

    def fetch(expert, slot):
        pairs = ((wg_hbm, wg_f), (wu_hbm, wu_f), (wd_hbm, wd_f))
        return [pltpu.make_async_copy(w.at[expert], buf.at[slot], w_sem.at[slot, n])
                for n, (w, buf) in enumerate(pairs)]

    @pl.when(i == 0)
    def _():
        for copy in fetch(owner, 0):
            copy.start()

    @pl.when(meta_ref[META_FIRST, i] == 1)
    def _():
        slot = meta_ref[META_SLOT, i]
        for copy in fetch(owner, slot):
            copy.wait()
        wg_bf[...] = wg_f[slot].astype(BF16)
        wu_bf[...] = wu_f[slot].astype(BF16)
        wd_bf[...] = wd_f[slot].astype(BF16)

        @pl.when(nxt >= 0)
        def _():
            for copy in fetch(nxt, 1 - slot):
                copy.start()

    @pl.when(i < used)
    def _():
        rows_in(i).wait()
        row = lax.broadcasted_iota(jnp.int32, (ROW_TILE, HALF_D), 0)
        words = xs_buf[lax.rem(i, XS_SLOTS)]
        packed = jnp.where(row < valid, words, 0)
        hi, lo = _unpack_halves(packed)
        hi = hi.astype(BF16)
        lo = lo.astype(BF16)
        hg = _dot(hi, wg_bf[:HALF_D, :]) + _dot(lo, wg_bf[HALF_D:, :])
        hu = _dot(hi, wu_bf[:HALF_D, :]) + _dot(lo, wu_bf[HALF_D:, :])
        hid = (hg * _sigmoid(hg) * hu).astype(BF16)
        ys_ref[...] = _pack_halves(_dot(hid, wd_bf[...]))


def _experts(xs, meta, n_tiles, wg, wu, wd):
    tile = lambda i, meta: (jnp.minimum(i, meta[META_USED, 0] - 1), 0)
    hbm = pl.BlockSpec(memory_space=pl.ANY)
    up_shape, down_shape = (D_MODEL, EXPERT_FF), (EXPERT_FF, D_MODEL)
    grid_spec = pltpu.PrefetchScalarGridSpec(
        num_scalar_prefetch=1,
        grid=(n_tiles,),
        in_specs=[hbm, hbm, hbm, hbm],
        out_specs=pl.BlockSpec((ROW_TILE, HALF_D), tile),
        scratch_shapes=[pltpu.VMEM((XS_SLOTS, ROW_TILE, HALF_D), I32), pltpu.SemaphoreType.DMA((XS_SLOTS,)),
                        pltpu.VMEM((2,) + up_shape, F32), pltpu.VMEM((2,) + up_shape, F32),
                        pltpu.VMEM((2,) + down_shape, F32),
                        pltpu.VMEM(up_shape, BF16), pltpu.VMEM(up_shape, BF16), pltpu.VMEM(down_shape, BF16),
                        pltpu.SemaphoreType.DMA((2, 3))],
    )
    return pl.pallas_call(
        _experts_kernel,
        grid_spec=grid_spec,
        out_shape=jax.ShapeDtypeStruct((n_tiles * ROW_TILE, HALF_D), I32),
        compiler_params=pltpu.CompilerParams(dimension_semantics=("arbitrary",),
                                             vmem_limit_bytes=VMEM_LIMIT),
        name="experts",
    )(meta, xs, wg, wu, wd)


def _ple_kernel(x1_hbm, g1_hbm, g2_hbm, route_ref, p_ref, wp_f32, pg_ref, gg_ref, wgate_f32, o_ref,
                wp_ref, wgate_ref, x1_buf, g1_buf, g2_buf, ring_sem):
    i = pl.program_id(0)
    steps = pl.num_programs(0)
    streams = ((x1_hbm, x1_buf), (g1_hbm, g1_buf), (g2_hbm, g2_buf))

    def rows_in(step):
        start = pl.multiple_of(step * PLE_TM, PLE_TM)
        slot = lax.rem(step, PLE_SLOTS)
        return [pltpu.make_async_copy(src.at[pl.ds(start, PLE_TM)], buf.at[slot], ring_sem.at[n, slot])
                for n, (src, buf) in enumerate(streams)]

    def start_rows(step):
        for copy in rows_in(step):
            copy.start()

    @pl.when(i == 0)
    def _():
        for ahead in range(PLE_SLOTS - 1):
            pl.when(ahead < steps)(functools.partial(start_rows, ahead))
        wp_ref[...] = wp_f32[...].astype(BF16)
        wgate_ref[...] = wgate_f32[...].astype(BF16)

    pl.when(i + (PLE_SLOTS - 1) < steps)(functools.partial(start_rows, i + (PLE_SLOTS - 1)))

    for copy in rows_in(i):
        copy.wait()
    slot = lax.rem(i, PLE_SLOTS)
    x1_ref, g1_ref, g2_ref = x1_buf.at[slot], g1_buf.at[slot], g2_buf.at[slot]

    route = route_ref[...]
    lane = lax.broadcasted_iota(jnp.int32, route.shape, 1)
    pick = lambda ln: jnp.sum(jnp.where(lane == ln, route, 0.0), axis=-1, keepdims=True)
    c1 = pick(LANE_C1)
    c2 = pick(LANE_C2)
    a_hi, a_lo = _unpack_halves(g1_ref[...])
    b_hi, b_lo = _unpack_halves(g2_ref[...])
    x = x1_ref[...] + jnp.concatenate([c1 * a_hi + c2 * b_hi, c1 * a_lo + c2 * b_lo], axis=1)
    ple = _rms(_dot(p_ref[...].astype(BF16), wp_ref[...]), pg_ref[...])
    gate = _sigmoid(_dot(_rms(x, gg_ref[...]).astype(BF16), wgate_ref[...]))
    o_ref[...] = x + gate * ple


def _ple(x1, g1, g2, route, p2d, w_ple, ple_g, gate_g, w_gate):
    t = x1.shape[0]
    row = lambda w: pl.BlockSpec((PLE_TM, w), lambda i: (i, 0))
    full = lambda a: pl.BlockSpec(a.shape, lambda i: (0,) * a.ndim)
    hbm = pl.BlockSpec(memory_space=pl.ANY)
    return pl.pallas_call(
        _ple_kernel,
        grid=(t // PLE_TM,),
        in_specs=[hbm, hbm, hbm, row(ROUTE_LANES), row(PLE_DIM),
                  full(w_ple), full(ple_g), full(gate_g), full(w_gate)],
        out_specs=row(D_MODEL),
        out_shape=jax.ShapeDtypeStruct((t, D_MODEL), F32),
        scratch_shapes=[pltpu.VMEM(w_ple.shape, BF16), pltpu.VMEM(w_gate.shape, BF16),
                        pltpu.VMEM((PLE_SLOTS, PLE_TM, D_MODEL), F32),
                        pltpu.VMEM((PLE_SLOTS, PLE_TM, HALF_D), I32), pltpu.VMEM((PLE_SLOTS, PLE_TM, HALF_D), I32),
                        pltpu.SemaphoreType.DMA((3, PLE_SLOTS))],
        compiler_params=pltpu.CompilerParams(dimension_semantics=("arbitrary",),
                                             vmem_limit_bytes=VMEM_LIMIT),
        name="ple",
    )(x1, g1, g2, route, p2d, w_ple, ple_g, gate_g, w_gate)


def kernel(x, p, mix_norm_g, w_in, ret_gn_g, ret_gn_b, w_ret_o, q_norm_g, k_norm_g, attn_sinks,
           w_swa_o, w_out, ffn_norm_g, w_router_group, b_router_group, w_router_expert,
           b_router_expert, w_exp_gate, w_exp_up, w_exp_down, ple_gate_norm_g, w_ple_gate,
           w_ple, ple_norm_g):
    batch, seq, d = x.shape
    t = batch * seq
    depth = w_in.shape[0]
    n_tiles = 2 * t // ROW_TILE + N_EXPERTS
    row = lambda a: a.reshape(1, -1)
    x2d = x.reshape(t, d)
    for i in range(depth):
        pad = ROUTE_LANES - N_GROUPS - N_EXPERTS
        w_router = jnp.pad(jnp.concatenate([w_router_expert[i], w_router_group[i]], axis=1), ((0, 0), (0, pad)))
        b_router = jnp.pad(jnp.concatenate([b_router_expert[i], b_router_group[i]]), (0, pad)).reshape(1, -1)
        w_router = w_router.astype(BF16)
        x1, hp, route, cnt = _block(x2d, row(mix_norm_g[i]), w_in[i], row(ret_gn_g[i]),
                                    row(ret_gn_b[i]), attn_sinks[i], q_norm_g[i], k_norm_g[i],
                                    w_ret_o[i], w_swa_o[i], w_out[i],
                                    row(ffn_norm_g[i]), w_router, b_router, seq)
        meta, pos1, pos2 = _tile_plan(cnt, route, n_tiles)
        xs = _dispatch(hp, pos1, pos2, n_tiles * ROW_TILE)
        ys = _experts(xs, meta, n_tiles, w_exp_gate[i], w_exp_up[i], w_exp_down[i])
        g1, g2 = _combine_gather(ys, pos1, pos2, t)
        x2d = _ple(x1, g1, g2, route, p[i].reshape(t, PLE_DIM), w_ple[i], row(ple_norm_g[i]),
                   row(ple_gate_norm_g[i]), w_ple_gate[i])
    return x2d.reshape(batch, seq, d)
```

```python
import functools

import jax
import jax.numpy as jnp
import numpy as np
from jax import lax
from jax.experimental import pallas as pl
from jax.experimental.pallas import tpu as pltpu
from jax.experimental.pallas import tpu_sc as plsc

F32 = jnp.float32
BF16 = jnp.bfloat16
I32 = jnp.int32

EPS = 1e-6
D_MODEL = 1024
PLE_DIM = 256
RET_HEADS = 4
RET_DK = 128
RET_DV = 128
RET_CHUNK = 128
SWA_HEADS = 8
SWA_KV_HEADS = 2
SWA_GROUP = SWA_HEADS // SWA_KV_HEADS
SWA_HD = 64
WINDOW = 128
N_GROUPS = 4
EXPERTS_PER_GROUP = 8
N_EXPERTS = N_GROUPS * EXPERTS_PER_GROUP
EXPERT_FF = 256

RET_W = RET_HEADS * RET_DK
SWA_Q = SWA_HEADS * SWA_HD
SWA_KV = SWA_KV_HEADS * SWA_HD
Z_RET_W = 4 * RET_W
Z_SKV_W = 2 * SWA_KV
Z_GATE_W = 2 * D_MODEL
COL_SQ = Z_RET_W
COL_SKV = COL_SQ + SWA_Q
COL_GATE = COL_SKV + Z_SKV_W
IN_WIDTH = COL_GATE + Z_GATE_W

ROUTE_LANES = 128
LANE_GROUP0 = N_EXPERTS
LANE_C1, LANE_C2 = 34, 35
LANE_SECOND0 = 64
NEG_INF = -1e30

HALF_D = D_MODEL // 2
ROW_TILE = 512
XS_SLOTS = 3
SC_CORES = 2
SC_SUBCORES = 16
SC_WORKERS = SC_CORES * SC_SUBCORES
SC_CHUNK = 128
SC_IDX_ROWS = 8

VMEM_LIMIT = 56 * 1024 * 1024

BLOCK_ROWS = 512
W_SLAB = 256
RET_ROWS = 512
MERGE_SUB = 256
PLE_TM = 1024
PLE_SLOTS = 3


def _rms(x, g):
    ms = jnp.mean(x * x, axis=-1, keepdims=True)
    return x * lax.rsqrt(ms + EPS) * g


def _sigmoid(x):
    return 1.0 / (1.0 + jnp.exp(-x))


def _dot(a, b):
    return jnp.dot(a, b, preferred_element_type=F32)


def _dot_nt(a, b):
    return lax.dot_general(a, b, (((1,), (1,)), ((), ())), preferred_element_type=F32)


def _dot_tn(a, b):
    return lax.dot_general(a, b, (((0,), (0,)), ((), ())), preferred_element_type=F32)


def _pack_halves(v):
    return pltpu.pack_elementwise([v[:, :HALF_D], v[:, HALF_D:]], packed_dtype=BF16)


def _unpack_halves(p):
    words = lax.bitcast_convert_type(p, jnp.uint32)
    return tuple(pltpu.unpack_elementwise(words, index=k, packed_dtype=BF16, unpacked_dtype=F32) for k in range(2))


def _retention_tables():
    f32 = np.float32
    h = RET_HEADS
    c = RET_CHUNK
    log_gamma = np.log1p(-np.exp2(f32(-5.0) - np.arange(h, dtype=f32))).astype(f32)
    pos = np.arange(c, dtype=f32)
    diff = pos[:, None] - pos[None, :]
    decay = np.where(diff[None] >= 0.0, np.exp(np.maximum(diff, f32(0.0))[None] * log_gamma[:, None, None]), f32(0.0))
    scale = f32(RET_DK ** -0.5)
    dmask = decay * scale
    zeta = np.exp((f32(c - 1.0) - pos)[None, :] * log_gamma[:, None]) * scale
    xi = np.exp((pos + f32(1.0))[None, :] * log_gamma[:, None])
    cdec = np.exp(f32(c) * log_gamma)
    bc = lambda a: np.ascontiguousarray(np.broadcast_to(a[:, :, None], (h, c, c)), dtype=f32)
    return (dmask.astype(f32), bc(zeta), bc(xi),
            np.ascontiguousarray(np.broadcast_to(cdec[:, None, None], (h, c, c)), dtype=f32))


def _swa_tables():
    f32 = np.float32
    qi = np.arange(WINDOW)[:, None]
    sj = np.arange(2 * WINDOW)[None, :]
    rel = qi + WINDOW - sj
    ok = (rel >= 0) & (rel < WINDOW)
    slopes = np.exp2(f32(-8.0) * np.arange(1, SWA_HEADS + 1, dtype=f32) / f32(SWA_HEADS)).astype(f32)
    bias = np.where(ok[None], -slopes[:, None, None] * rel.astype(f32)[None], f32(NEG_INF)).astype(f32)
    bias = bias.reshape(SWA_KV_HEADS, SWA_GROUP * WINDOW, 2 * WINDOW)
    avg = lambda heads: jnp.asarray(np.kron(np.eye(heads, dtype=f32), np.full((SWA_HD, SWA_HD), 1.0 / SWA_HD, f32)),
                                    dtype=BF16)
    return avg(SWA_HEADS), avg(SWA_KV_HEADS), jnp.asarray(bias)


def _block_kernel(per_seq, sinks_ref, x_ref, xp_ref, g_ref, w_hbm, dmask_ref, zeta_ref, xi_ref, cdec_ref,
                  gng_ref, gnb_ref, qgain_ref, kgain_ref, bq_ref, bk_ref, bias_ref,
                  wro_hbm, wso_hbm, wout_hbm, fg_ref, wre_ref, wrg_ref, bre_ref, brg_ref,
                  x1_ref, hp_ref, route_ref, cnt_ref,
                  w_ref, wro_ref, wso_ref, wout_ref, wtmp_ref, w_sem, qg_ref, kg_ref, wr_ref, br_ref,
                  zc_ref, zn_ref, yr_ref, ys_ref, state_ref, prev_ref, band_ref, tail_ref):
    g = pl.program_id(0)
    seq_start = lax.rem(g - 1, per_seq) == 0

    @pl.when(g == 0)
    def _():
        slabs = [(src, dst, c0) for src, dst in ((w_hbm, w_ref), (wro_hbm, wro_ref), (wso_hbm, wso_ref),
                                                 (wout_hbm, wout_ref))
                 for c0 in range(0, dst.shape[1], W_SLAB)]

        def slab(k):
            src, dst, c0 = slabs[k]
            return pltpu.make_async_copy(src.at[:, pl.ds(c0, W_SLAB)], wtmp_ref.at[k % 2, pl.ds(0, dst.shape[0])],
                                         w_sem.at[k % 2])

        slab(0).start()
        for k, (_, dst, c0) in enumerate(slabs):
            if k + 1 < len(slabs):
                slab(k + 1).start()
            slab(k).wait()
            dst[:, c0:c0 + W_SLAB] = wtmp_ref[k % 2, :dst.shape[0]].astype(BF16)
        for hd in range(SWA_HEADS):
            qg_ref[:, hd * SWA_HD:(hd + 1) * SWA_HD] = qgain_ref[...] * (SWA_HD ** -0.5)
        for kh in range(SWA_KV_HEADS):
            kg_ref[:, kh * SWA_HD:(kh + 1) * SWA_HD] = kgain_ref[...]
        wr_ref[...] = jnp.zeros_like(wr_ref)
        wr_ref[:, :N_EXPERTS] = wre_ref[...].astype(BF16)
        wr_ref[:, LANE_GROUP0:LANE_GROUP0 + N_GROUPS] = wrg_ref[...].astype(BF16)
        br_ref[...] = jnp.zeros_like(br_ref)
        br_ref[:, :N_EXPERTS] = bre_ref[...]
        br_ref[:, LANE_GROUP0:LANE_GROUP0 + N_GROUPS] = brg_ref[...]
        zc_ref[...] = jnp.zeros_like(zc_ref)
        tail_ref[...] = jnp.zeros_like(tail_ref)
        state_ref[...] = jnp.zeros_like(state_ref)

    @pl.when(seq_start)
    def _():
        state_ref[...] = jnp.zeros_like(state_ref)

    band_ref[:WINDOW, :] = tail_ref[...]
    band_ref[WINDOW:, :] = zc_ref[:, COL_SKV:COL_GATE]

    h = _rms(x_ref[...], g_ref[...]).astype(BF16)

    def project(c0, width):
        zn_ref[:, c0:c0 + width] = _dot(h, w_ref[:, c0:c0 + width]).astype(BF16)

    cw = 512
    proj_pieces = [functools.partial(project, j * cw, cw) for j in range(COL_SKV // cw)]
    proj_pieces.append(functools.partial(project, COL_SKV, Z_SKV_W))
    proj_pieces += [functools.partial(project, COL_GATE + j * cw, cw) for j in range(Z_GATE_W // cw)]

    def retention_rows(t):
        r_base = t * RET_ROWS
        n_chunks = RET_ROWS // RET_CHUNK
        part = lambda c, p, hd: zc_ref[r_base + c * RET_CHUNK:r_base + (c + 1) * RET_CHUNK,
                                       p * RET_W + hd * RET_DK:p * RET_W + (hd + 1) * RET_DK]
        for hd in range(RET_HEADS):
            state = state_ref[hd]
            for c in range(n_chunks):
                prev_ref[t * n_chunks + c, hd] = state.astype(BF16)
                kz = (part(c, 1, hd).astype(F32) * zeta_ref[hd]).astype(BF16)
                state = cdec_ref[hd] * state + _dot_tn(kz, part(c, 2, hd))
            state_ref[hd] = state
        out_rows = []
        for c in range(n_chunks):
            heads = []
            for hd in range(RET_HEADS):
                q = part(c, 0, hd)
                scores = _dot_nt(q, part(c, 1, hd)) * dmask_ref[hd]
                y = _dot(scores.astype(BF16), part(c, 2, hd)) + _dot(q, prev_ref[t * n_chunks + c, hd]) * xi_ref[hd]
                mu = jnp.mean(y, axis=-1, keepdims=True)
                d = y - mu
                var = jnp.mean(d * d, axis=-1, keepdims=True)
                hs = slice(hd * RET_DV, (hd + 1) * RET_DV)
                yn = d * lax.rsqrt(var + EPS) * gng_ref[:, hs] + gnb_ref[:, hs]
                gate = part(c, 3, hd).astype(F32)
                heads.append((gate * _sigmoid(gate) * yn).astype(BF16))
            out_rows.append(jnp.concatenate(heads, axis=1))
        yr_ref[r_base:r_base + RET_ROWS, :] = jnp.concatenate(out_rows, axis=0)

    group_rows = SWA_GROUP * WINDOW
    lo_q = lax.broadcasted_iota(jnp.int32, (WINDOW, 2 * SWA_HD), 1) < SWA_HD
    lo_k = lax.broadcasted_iota(jnp.int32, (2 * WINDOW, 2 * SWA_HD), 1) < SWA_HD
    in_prev = lax.broadcasted_iota(jnp.int32, (group_rows, 2 * WINDOW), 1) < WINDOW
    row_head = lax.broadcasted_iota(jnp.int32, (group_rows, 1), 0) // WINDOW

    def both_heads(v):
        swapped = pltpu.roll(v, SWA_HD, axis=1)
        return jnp.where(lo_k, v, swapped), jnp.where(lo_k, swapped, v)

    def swa_block(n):
        rows = slice(n * WINDOW, (n + 1) * WINDOW)
        kv = band_ref[n * WINDOW:(n + 2) * WINDOW, :]
        kf = kv[:, :SWA_KV].astype(F32)
        kn = kf * lax.rsqrt(_dot((kf * kf).astype(BF16), bk_ref[...]) + EPS) * kg_ref[...]
        keys = [a.astype(BF16) for a in both_heads(kn)]
        vals = [a.astype(BF16) for a in both_heads(kv[:, SWA_KV:].astype(F32))]
        qf = zc_ref[rows, COL_SQ:COL_SKV].astype(F32)
        qn = qf * lax.rsqrt(_dot((qf * qf).astype(BF16), bq_ref[...]) + EPS) * qg_ref[...]
        outs = []
        for kh in range(SWA_KV_HEADS):
            parts = []
            for j in range(SWA_GROUP // 2):
                c0 = (kh * SWA_GROUP + 2 * j) * SWA_HD
                two = qn[:, c0:c0 + 2 * SWA_HD]
                parts += [jnp.where(lo_q, two, 0.0), jnp.where(lo_q, 0.0, two)]
            qs = jnp.concatenate(parts, axis=0).astype(BF16)
            s = _dot_nt(qs, keys[kh]) + bias_ref[kh]
            if n == 0:
                s = jnp.where(jnp.logical_and(seq_start, in_prev), NEG_INF, s)
            sink = sinks_ref[kh * SWA_GROUP]
            for gi in range(1, SWA_GROUP):
                sink = jnp.where(row_head == gi, sinks_ref[kh * SWA_GROUP + gi], sink)
            m = jnp.maximum(jnp.max(s, axis=-1, keepdims=True), sink)
            p = jnp.exp(s - m)
            denom = jnp.sum(p, axis=-1, keepdims=True) + jnp.exp(sink - m)
            o = _dot(p.astype(BF16), vals[kh]) * (1.0 / denom)
            for j in range(SWA_GROUP // 2):
                even = o[(2 * j) * WINDOW:(2 * j + 1) * WINDOW]
                odd = o[(2 * j + 1) * WINDOW:(2 * j + 2) * WINDOW]
                outs.append(jnp.where(lo_q, even, odd).astype(BF16))
        ys_ref[rows, :] = jnp.concatenate(outs, axis=1)

    mix_pieces = [functools.partial(retention_rows, t) for t in range(BLOCK_ROWS // RET_ROWS)]
    mix_pieces += [functools.partial(swa_block, n) for n in range(BLOCK_ROWS // WINDOW)]

    subs = []

    def merge_sub(s):
        subs.append(_merge_rows(slice(s * MERGE_SUB, (s + 1) * MERGE_SUB), yr_ref, ys_ref, zc_ref, xp_ref, wro_ref,
                                wso_ref, wout_ref, fg_ref, wr_ref, br_ref))

    other_pieces = mix_pieces + [functools.partial(merge_sub, s) for s in range(BLOCK_ROWS // MERGE_SUB)]

    n_other = len(other_pieces)
    for k, piece in enumerate(other_pieces):
        for proj_piece in proj_pieces[k * len(proj_pieces) // n_other:(k + 1) * len(proj_pieces) // n_other]:
            proj_piece()
        piece()
    x1_ref[...] = jnp.concatenate([s[0] for s in subs], axis=0)
    hp_ref[...] = jnp.concatenate([s[1] for s in subs], axis=0)
    route_ref[...] = jnp.concatenate([s[2] for s in subs], axis=0)
    for s, sub in enumerate(subs):
        cnt_ref[s] = sub[3]

    tail_ref[...] = zc_ref[BLOCK_ROWS - WINDOW:, COL_SKV:COL_GATE]
    zc_ref[...] = zn_ref[...]


def _merge_rows(rows, yr_ref, ys_ref, z_ref, x_ref, wro_ref, wso_ref, wout_ref, fg_ref, wr_ref, br_ref):
    a = _dot(yr_ref[rows, :], wro_ref[...])
    b = _dot(ys_ref[rows, :], wso_ref[...])
    gate_r = z_ref[rows, COL_GATE:COL_GATE + D_MODEL].astype(F32)
    gate_s = z_ref[rows, COL_GATE + D_MODEL:].astype(F32)
    merged = (_sigmoid(gate_r) * a + _sigmoid(gate_s) * b).astype(BF16)
    x1 = x_ref[rows, :] + _dot(merged, wout_ref[...])
    h2 = _rms(x1, fg_ref[...])
    packed = _pack_halves(h2)

    logits = _dot(h2.astype(BF16), wr_ref[...]) + br_ref[...]
    lane = lax.broadcasted_iota(jnp.int32, logits.shape, 1)
    big = jnp.int32(ROUTE_LANES)
    is_group = jnp.logical_and(lane >= LANE_GROUP0, lane < LANE_GROUP0 + N_GROUPS)
    gl = jnp.where(is_group, logits, NEG_INF)
    gmax = jnp.max(gl, axis=-1, keepdims=True)
    g_w = 1.0 / jnp.sum(jnp.exp(gl - gmax), axis=-1, keepdims=True)
    g_sel = jnp.min(jnp.where(gl == gmax, lane, big), axis=-1, keepdims=True) - LANE_GROUP0
    in_group = jnp.logical_and(lane < N_EXPERTS, (lane >> 3) == g_sel)
    el = jnp.where(in_group, logits, NEG_INF)
    m1 = jnp.max(el, axis=-1, keepdims=True)
    i1 = jnp.min(jnp.where(el == m1, lane, big), axis=-1, keepdims=True)
    el2 = jnp.where(lane == i1, NEG_INF, el)
    m2 = jnp.max(el2, axis=-1, keepdims=True)
    i2 = jnp.min(jnp.where(el2 == m2, lane, big), axis=-1, keepdims=True)
    e2 = jnp.exp(m2 - m1)
    c1 = g_w / (1.0 + e2)
    c2 = g_w * e2 / (1.0 + e2)
    chosen = jnp.where(jnp.logical_or(lane == i1, lane == i2), 1.0, 0.0)
    route = (chosen
             + jnp.where(lane == LANE_C1, c1, 0.0)
             + jnp.where(lane == LANE_C2, c2, 0.0)
             + jnp.where(lane == i2 + LANE_SECOND0, 1.0, 0.0))
    return x1, packed, route, jnp.sum(chosen, axis=0, keepdims=True)


def _block(x2d, mix_g, w_in, gn_g, gn_b, sinks, q_g, k_g, w_ret_o, w_swa_o, w_out, ffn_g, router, seq):
    t = x2d.shape[0]
    n = t // BLOCK_ROWS
    assert seq % BLOCK_ROWS == 0
    consts = _retention_tables() + (gn_g, gn_b, q_g, k_g) + _swa_tables()
    merge_weights = (w_ret_o, w_swa_o, w_out)
    small = (ffn_g,) + tuple(router)
    hbm = pl.BlockSpec(memory_space=pl.ANY)
    full = lambda a: pl.BlockSpec(a.shape, lambda g: (0,) * a.ndim)
    cur = lambda w: pl.BlockSpec((BLOCK_ROWS, w), lambda g: (jnp.minimum(g, n - 1), 0))
    prv = lambda w: pl.BlockSpec((BLOCK_ROWS, w), lambda g: (jnp.maximum(g - 1, 0), 0))
    chunks = BLOCK_ROWS // RET_CHUNK
    subs_per_block = BLOCK_ROWS // MERGE_SUB
    return pl.pallas_call(
        functools.partial(_block_kernel, seq // BLOCK_ROWS),
        grid=(n + 1,),
        in_specs=([pl.BlockSpec(memory_space=pltpu.SMEM), cur(D_MODEL), prv(D_MODEL), full(mix_g), hbm]
                  + [full(a) for a in consts] + [hbm] * len(merge_weights) + [full(a) for a in small]),
        out_specs=[prv(D_MODEL), prv(HALF_D), prv(ROUTE_LANES),
                   pl.BlockSpec((subs_per_block, 1, ROUTE_LANES), lambda g: (jnp.maximum(g - 1, 0), 0, 0))],
        out_shape=[jax.ShapeDtypeStruct((t, D_MODEL), F32),
                   jax.ShapeDtypeStruct((t, HALF_D), I32),
                   jax.ShapeDtypeStruct((t, ROUTE_LANES), F32),
                   jax.ShapeDtypeStruct((n * subs_per_block, 1, ROUTE_LANES), F32)],
        scratch_shapes=[pltpu.VMEM(w_in.shape, BF16)] + [pltpu.VMEM(w.shape, BF16) for w in merge_weights]
                       + [pltpu.VMEM((2, D_MODEL, W_SLAB), F32), pltpu.SemaphoreType.DMA((2,)),
                        pltpu.VMEM((1, SWA_Q), F32), pltpu.VMEM((1, SWA_KV), F32),
                        pltpu.VMEM((D_MODEL, ROUTE_LANES), BF16), pltpu.VMEM((1, ROUTE_LANES), F32),
                        pltpu.VMEM((BLOCK_ROWS, IN_WIDTH), BF16), pltpu.VMEM((BLOCK_ROWS, IN_WIDTH), BF16),
                        pltpu.VMEM((BLOCK_ROWS, RET_W), BF16), pltpu.VMEM((BLOCK_ROWS, SWA_Q), BF16),
                        pltpu.VMEM((RET_HEADS, RET_DK, RET_DV), F32),
                        pltpu.VMEM((chunks, RET_HEADS, RET_DK, RET_DV), BF16),
                        pltpu.VMEM((BLOCK_ROWS + WINDOW, Z_SKV_W), BF16), pltpu.VMEM((WINDOW, Z_SKV_W), BF16)],
        compiler_params=pltpu.CompilerParams(dimension_semantics=("arbitrary",),
                                             vmem_limit_bytes=VMEM_LIMIT),
        name="block",
    )(sinks, x2d, x2d, mix_g, w_in, *consts, *merge_weights, *small)


def _plan_kernel(cnt_ref, route_ref, base_ref, meta_ref, pos1_ref, pos2_ref):
    counts = cnt_ref[...]
    nt = counts.shape[0]
    exact = functools.partial(jnp.dot, precision=lax.Precision.HIGHEST, preferred_element_type=F32)
    total = jnp.sum(counts, axis=0, keepdims=True)
    tiles_e = jnp.ceil(total * (1.0 / ROW_TILE))
    k = lax.broadcasted_iota(jnp.int32, (ROUTE_LANES, ROUTE_LANES), 0)
    lane = lax.broadcasted_iota(jnp.int32, (ROUTE_LANES, ROUTE_LANES), 1)
    tile_end = exact(jnp.broadcast_to(tiles_e, (8, ROUTE_LANES)), jnp.where(k <= lane, 1.0, 0.0))[0:1]
    tile_start = tile_end - tiles_e
    r = lax.broadcasted_iota(jnp.int32, (nt, nt), 0)
    c = lax.broadcasted_iota(jnp.int32, (nt, nt), 1)
    before = exact(jnp.where(c < r, 1.0, 0.0), counts)
    base_ref[...] = tile_start * ROW_TILE + before

    used = jnp.max(tile_end, axis=-1, keepdims=True)
    tile = k.astype(F32)
    clamped = jnp.minimum(tile, jnp.maximum(used - 1.0, 0.0))
    is_expert = lane < N_EXPERTS
    owner = jnp.sum(jnp.where(jnp.logical_and(is_expert, tile_end <= clamped), 1.0, 0.0), axis=-1, keepdims=True)
    owner = jnp.minimum(owner, N_EXPERTS - 1.0)
    mine = lane.astype(F32) == owner
    pick = lambda v: jnp.sum(jnp.where(mine, v, 0.0), axis=-1, keepdims=True)
    t1 = tile[:, 0:1]
    valid = jnp.clip(pick(total) - (t1 - pick(tile_start)) * ROW_TILE, 0.0, float(ROW_TILE))
    valid = jnp.where(t1 < used, valid, 0.0)

    nonempty = jnp.where(jnp.logical_and(lane[0:1] < N_EXPERTS, tiles_e > 0.0), 1.0, 0.0)
    rank = exact(jnp.broadcast_to(nonempty, (8, ROUTE_LANES)), jnp.where(k < lane, 1.0, 0.0))[0:1]
    nonempty_col = jnp.sum(jnp.where(k == lane, jnp.broadcast_to(nonempty, k.shape), 0.0), axis=-1, keepdims=True)
    later = jnp.where(jnp.logical_and(k > lane, nonempty_col > 0.0), tile, float(ROUTE_LANES))
    nxt_e = jnp.min(later, axis=0, keepdims=True)
    nxt_e = jnp.where(nxt_e < float(ROUTE_LANES), nxt_e, -1.0)
    first = jnp.where(jnp.logical_and(t1 == pick(tile_start), t1 < used), 1.0, 0.0)
    my_rank = pick(rank)
    slot = my_rank - 2.0 * jnp.floor(my_rank * 0.5)
    fields = (owner, valid, first, pick(nxt_e), slot, jnp.broadcast_to(used, owner.shape))
    by_tile = sum(jnp.where(lane == n, f, 0.0) for n, f in enumerate(fields))
    meta_ref[...] = by_tile.T[:8, :].astype(I32)

    sel_row = lax.broadcasted_iota(jnp.int32, (8, ROUTE_LANES), 0)
    sel_lane = lax.broadcasted_iota(jnp.int32, (8, ROUTE_LANES), 1)
    tok_r = lax.broadcasted_iota(jnp.int32, (MERGE_SUB, MERGE_SUB), 0)
    tok_c = lax.broadcasted_iota(jnp.int32, (MERGE_SUB, MERGE_SUB), 1)
    eye = jnp.where(tok_r == tok_c, 1.0, 0.0).astype(BF16)
    upto = jnp.where(tok_r <= tok_c, 1.0, 0.0).astype(BF16)
    sum_experts = lambda a: jnp.sum(a, axis=0, keepdims=True)

    def sub_positions(s, carry):
        route = route_ref[pl.ds(pl.multiple_of(s * MERGE_SUB, MERGE_SUB), MERGE_SUB), :].astype(BF16)
        record_t = _dot_tn(route, eye)
        incl_t = _dot_tn(route, upto)[:N_EXPERTS]
        both_t = record_t[:N_EXPERTS]
        second_t = record_t[LANE_SECOND0:LANE_SECOND0 + N_EXPERTS]
        rank2 = sum_experts(second_t * incl_t) - 1.0
        rank1 = sum_experts(both_t * incl_t) - rank2 - 2.0
        base_row = jnp.where(sel_lane < N_EXPERTS, jnp.broadcast_to(base_ref[pl.ds(s, 1), :], (8, ROUTE_LANES)), 0.0)
        hi = jnp.floor(base_row * (1.0 / 256.0))
        lo = base_row - 256.0 * hi
        hi2 = pltpu.roll(hi, LANE_SECOND0, axis=1)
        lo2 = pltpu.roll(lo, LANE_SECOND0, axis=1)
        table = (jnp.where(sel_row == 0, hi, 0.0) + jnp.where(sel_row == 1, lo, 0.0)
                 + jnp.where(sel_row == 2, hi2, 0.0) + jnp.where(sel_row == 3, lo2, 0.0))
        r = _dot_nt(table.astype(BF16), route)
        base_both = 256.0 * r[0:1] + r[1:2]
        base2 = 256.0 * r[2:3] + r[3:4]
        worker = s // subs_per_worker
        chunk0 = lax.rem(s, subs_per_worker) * chunks_per_sub
        for ref, row in ((pos1_ref, base_both - base2 + rank1), (pos2_ref, base2 + rank2)):
            for c in range(chunks_per_sub):
                ref[worker, pl.ds(chunk0 + c, 1), :] = row[:, c * SC_CHUNK:(c + 1) * SC_CHUNK].astype(I32)
        return carry

    chunks_per_sub = MERGE_SUB // SC_CHUNK
    subs_per_worker = nt // SC_WORKERS
    pos1_ref[...] = jnp.zeros_like(pos1_ref)
    pos2_ref[...] = jnp.zeros_like(pos2_ref)
    lax.fori_loop(0, nt, sub_positions, 0, unroll=4)


def _tile_plan(cnt, route, n_tiles):
    assert n_tiles <= ROUTE_LANES
    nt = cnt.shape[0]
    assert nt % SC_WORKERS == 0 and MERGE_SUB % SC_CHUNK == 0 and nt * MERGE_SUB <= SC_WORKERS * SC_IDX_ROWS * SC_CHUNK
    pos = jax.ShapeDtypeStruct((SC_WORKERS, SC_IDX_ROWS, SC_CHUNK), I32)
    _, meta, pos1, pos2 = pl.pallas_call(
        _plan_kernel,
        out_shape=[jax.ShapeDtypeStruct((nt, ROUTE_LANES), F32), jax.ShapeDtypeStruct((8, ROUTE_LANES), I32), pos, pos],
        compiler_params=pltpu.CompilerParams(vmem_limit_bytes=VMEM_LIMIT),
        name="tile_plan",
    )(cnt.reshape(nt, ROUTE_LANES), route)
    return meta, pos1, pos2


def _sc_mesh():
    return plsc.VectorSubcoreMesh(core_axis_name="c", subcore_axis_name="s")


def _sc_worker():
    return lax.axis_index("s") * SC_CORES + lax.axis_index("c")


def _dispatch(hp, pos1, pos2, n_rows):
    t = hp.shape[0]
    per_worker = t // SC_WORKERS
    k = per_worker // SC_CHUNK
    idx = pltpu.VMEM((SC_IDX_ROWS, SC_CHUNK), I32)

    @functools.partial(pl.kernel, mesh=_sc_mesh(), out_type=jax.ShapeDtypeStruct((n_rows, HALF_D), I32),
                       scratch_types=[idx, idx, pltpu.VMEM((SC_CHUNK, HALF_D), I32)], name="moe_dispatch")
    def run(h_hbm, p1_hbm, p2_hbm, xs_hbm, p1_v, p2_v, rows_v):
        wid = _sc_worker()
        pltpu.sync_copy(p1_hbm.at[wid], p1_v)
        pltpu.sync_copy(p2_hbm.at[wid], p2_v)
        for j in range(k):
            pltpu.sync_copy(h_hbm.at[pl.ds(wid * per_worker + j * SC_CHUNK, SC_CHUNK)], rows_v)
            pltpu.sync_copy(rows_v, xs_hbm.at[p1_v.at[j]])
            pltpu.sync_copy(rows_v, xs_hbm.at[p2_v.at[j]])

    return run(hp, pos1, pos2)


def _combine_gather(ys, pos1, pos2, t):
    per_worker = t // SC_WORKERS
    k = per_worker // SC_CHUNK
    idx = pltpu.VMEM((SC_IDX_ROWS, SC_CHUNK), I32)
    out = jax.ShapeDtypeStruct((t, HALF_D), I32)

    @functools.partial(pl.kernel, mesh=_sc_mesh(), out_type=(out, out),
                       scratch_types=[idx, idx, pltpu.VMEM((SC_CHUNK, HALF_D), I32)], name="moe_combine")
    def run(ys_hbm, p1_hbm, p2_hbm, g1_hbm, g2_hbm, p1_v, p2_v, rows_v):
        wid = _sc_worker()
        pltpu.sync_copy(p1_hbm.at[wid], p1_v)
        pltpu.sync_copy(p2_hbm.at[wid], p2_v)
        for j in range(k):
            dst = pl.ds(wid * per_worker + j * SC_CHUNK, SC_CHUNK)
            pltpu.sync_copy(ys_hbm.at[p1_v.at[j]], rows_v)
            pltpu.sync_copy(rows_v, g1_hbm.at[dst])
            pltpu.sync_copy(ys_hbm.at[p2_v.at[j]], rows_v)
            pltpu.sync_copy(rows_v, g2_hbm.at[dst])

    return run(ys, pos1, pos2)


META_OWNER, META_VALID, META_FIRST, META_NEXT, META_SLOT, META_USED = range(6)


def _experts_kernel(meta_ref, xs_hbm, wg_hbm, wu_hbm, wd_hbm, ys_ref,
                    xs_buf, xs_sem, wg_f, wu_f, wd_f, wg_bf, wu_bf, wd_bf, w_sem):
    i = pl.program_id(0)
    used = meta_ref[META_USED, 0]
    valid = meta_ref[META_VALID, i]
    owner = meta_ref[META_OWNER, i]
    nxt = meta_ref[META_NEXT, i]

    def rows_in(tile):
        start = pl.multiple_of(tile * ROW_TILE, ROW_TILE)
        slot = lax.rem(tile, XS_SLOTS)
        return pltpu.make_async_copy(xs_hbm.at[pl.ds(start, ROW_TILE)], xs_buf.at[slot], xs_sem.at[slot])

    @pl.when(i == 0)
    def _():
        for ahead in range(XS_SLOTS - 1):
            pl.when(ahead < used)(rows_in(ahead).start)

    @pl.when(i + (XS_SLOTS - 1) < used)
    def _():
        rows_in(i + (XS_SLOTS - 1)).start()

    def fetch(expert, slot):
        pairs = ((wg_hbm, wg_f), (wu_hbm, wu_f), (wd_hbm, wd_f))
        return [pltpu.make_async_copy(w.at[expert], buf.at[slot], w_sem.at[slot, n])
                for n, (w, buf) in enumerate(pairs)]

    @pl.when(i == 0)
    def _():
        for copy in fetch(owner, 0):
            copy.start()

    @pl.when(meta_ref[META_FIRST, i] == 1)
    def _():
        slot = meta_ref[META_SLOT, i]
        for copy in fetch(owner, slot):
            copy.wait()
        wg_bf[...] = wg_f[slot].astype(BF16)
        wu_bf[...] = wu_f[slot].astype(BF16)
        wd_bf[...] = wd_f[slot].astype(BF16)

        @pl.when(nxt >= 0)
        def _():
            for copy in fetch(nxt, 1 - slot):
                copy.start()

    @pl.when(i < used)
    def _():
        rows_in(i).wait()
        row = lax.broadcasted_iota(jnp.int32, (ROW_TILE, HALF_D), 0)
        words = xs_buf[lax.rem(i, XS_SLOTS)]
        packed = jnp.where(row < valid, words, 0)
        hi, lo = _unpack_halves(packed)
        hi = hi.astype(BF16)
        lo = lo.astype(BF16)
        hg = _dot(hi, wg_bf[:HALF_D, :]) + _dot(lo, wg_bf[HALF_D:, :])
        hu = _dot(hi, wu_bf[:HALF_D, :]) + _dot(lo, wu_bf[HALF_D:, :])
        hid = (hg * _sigmoid(hg) * hu).astype(BF16)
        ys_ref[...] = _pack_halves(_dot(hid, wd_bf[...]))


def _experts(xs, meta, n_tiles, wg, wu, wd):
    tile = lambda i, meta: (jnp.minimum(i, meta[META_USED, 0] - 1), 0)
    hbm = pl.BlockSpec(memory_space=pl.ANY)
    up_shape, down_shape = (D_MODEL, EXPERT_FF), (EXPERT_FF, D_MODEL)
    grid_spec = pltpu.PrefetchScalarGridSpec(
        num_scalar_prefetch=1,
        grid=(n_tiles,),
        in_specs=[hbm, hbm, hbm, hbm],
        out_specs=pl.BlockSpec((ROW_TILE, HALF_D), tile),
        scratch_shapes=[pltpu.VMEM((XS_SLOTS, ROW_TILE, HALF_D), I32), pltpu.SemaphoreType.DMA((XS_SLOTS,)),
                        pltpu.VMEM((2,) + up_shape, F32), pltpu.VMEM((2,) + up_shape, F32),
                        pltpu.VMEM((2,) + down_shape, F32),
                        pltpu.VMEM(up_shape, BF16), pltpu.VMEM(up_shape, BF16), pltpu.VMEM(down_shape, BF16),
                        pltpu.SemaphoreType.DMA((2, 3))],
    )
    return pl.pallas_call(
        _experts_kernel,
        grid_spec=grid_spec,
        out_shape=jax.ShapeDtypeStruct((n_tiles * ROW_TILE, HALF_D), I32),
        compiler_params=pltpu.CompilerParams(dimension_semantics=("arbitrary",),
                                             vmem_limit_bytes=VMEM_LIMIT),
        name="experts",
    )(meta, xs, wg, wu, wd)


def _ple_kernel(x1_hbm, g1_hbm, g2_hbm, route_ref, p_ref, wp_f32, pg_ref, gg_ref, wgate_f32, o_ref,
                wp_ref, wgate_ref, x1_buf, g1_buf, g2_buf, ring_sem):
    i = pl.program_id(0)
    steps = pl.num_programs(0)
    streams = ((x1_hbm, x1_buf), (g1_hbm, g1_buf), (g2_hbm, g2_buf))

    def rows_in(step):
        start = pl.multiple_of(step * PLE_TM, PLE_TM)
        slot = lax.rem(step, PLE_SLOTS)
        return [pltpu.make_async_copy(src.at[pl.ds(start, PLE_TM)], buf.at[slot], ring_sem.at[n, slot])
                for n, (src, buf) in enumerate(streams)]

    def start_rows(step):
        for copy in rows_in(step):
            copy.start()

    @pl.when(i == 0)
    def _():
        for ahead in range(PLE_SLOTS - 1):
            pl.when(ahead < steps)(functools.partial(start_rows, ahead))
        wp_ref[...] = wp_f32[...].astype(BF16)
        wgate_ref[...] = wgate_f32[...].astype(BF16)

    pl.when(i + (PLE_SLOTS - 1) < steps)(functools.partial(start_rows, i + (PLE_SLOTS - 1)))

    for copy in rows_in(i):
        copy.wait()
    slot = lax.rem(i, PLE_SLOTS)
    x1_ref, g1_ref, g2_ref = x1_buf.at[slot], g1_buf.at[slot], g2_buf.at[slot]

    route = route_ref[...]
    lane = lax.broadcasted_iota(jnp.int32, route.shape, 1)
    pick = lambda ln: jnp.sum(jnp.where(lane == ln, route, 0.0), axis=-1, keepdims=True)
    c1 = pick(LANE_C1)
    c2 = pick(LANE_C2)
    a_hi, a_lo = _unpack_halves(g1_ref[...])
    b_hi, b_lo = _unpack_halves(g2_ref[...])
    x = x1_ref[...] + jnp.concatenate([c1 * a_hi + c2 * b_hi, c1 * a_lo + c2 * b_lo], axis=1)
    ple = _rms(_dot(p_ref[...].astype(BF16), wp_ref[...]), pg_ref[...])
    gate = _sigmoid(_dot(_rms(x, gg_ref[...]).astype(BF16), wgate_ref[...]))
    o_ref[...] = x + gate * ple


def _ple(x1, g1, g2, route, p2d, w_ple, ple_g, gate_g, w_gate):
    t = x1.shape[0]
    row = lambda w: pl.BlockSpec((PLE_TM, w), lambda i: (i, 0))
    full = lambda a: pl.BlockSpec(a.shape, lambda i: (0,) * a.ndim)
    hbm = pl.BlockSpec(memory_space=pl.ANY)
    return pl.pallas_call(
        _ple_kernel,
        grid=(t // PLE_TM,),
        in_specs=[hbm, hbm, hbm, row(ROUTE_LANES), row(PLE_DIM),
                  full(w_ple), full(ple_g), full(gate_g), full(w_gate)],
        out_specs=row(D_MODEL),
        out_shape=jax.ShapeDtypeStruct((t, D_MODEL), F32),
        scratch_shapes=[pltpu.VMEM(w_ple.shape, BF16), pltpu.VMEM(w_gate.shape, BF16),
                        pltpu.VMEM((PLE_SLOTS, PLE_TM, D_MODEL), F32),
                        pltpu.VMEM((PLE_SLOTS, PLE_TM, HALF_D), I32), pltpu.VMEM((PLE_SLOTS, PLE_TM, HALF_D), I32),
                        pltpu.SemaphoreType.DMA((3, PLE_SLOTS))],
        compiler_params=pltpu.CompilerParams(dimension_semantics=("arbitrary",),
                                             vmem_limit_bytes=VMEM_LIMIT),
        name="ple",
    )(x1, g1, g2, route, p2d, w_ple, ple_g, gate_g, w_gate)


def kernel(x, p, mix_norm_g, w_in, ret_gn_g, ret_gn_b, w_ret_o, q_norm_g, k_norm_g, attn_sinks,
           w_swa_o, w_out, ffn_norm_g, w_router_group, b_router_group, w_router_expert,
           b_router_expert, w_exp_gate, w_exp_up, w_exp_down, ple_gate_norm_g, w_ple_gate,
           w_ple, ple_norm_g):
    batch, seq, d = x.shape
    t = batch * seq
    depth = w_in.shape[0]
    n_tiles = 2 * t // ROW_TILE + N_EXPERTS
    row = lambda a: a.reshape(1, -1)
    x2d = x.reshape(t, d)
    for i in range(depth):
        router = (w_router_expert[i], w_router_group[i], row(b_router_expert[i]), row(b_router_group[i]))
        x1, hp, route, cnt = _block(x2d, row(mix_norm_g[i]), w_in[i], row(ret_gn_g[i]),
                                    row(ret_gn_b[i]), attn_sinks[i], row(q_norm_g[i]), row(k_norm_g[i]),
                                    w_ret_o[i], w_swa_o[i], w_out[i], row(ffn_norm_g[i]), router, seq)
        meta, pos1, pos2 = _tile_plan(cnt, route, n_tiles)
        xs = _dispatch(hp, pos1, pos2, n_tiles * ROW_TILE)
        ys = _experts(xs, meta, n_tiles, w_exp_gate[i], w_exp_up[i], w_exp_down[i])
        g1, g2 = _combine_gather(ys, pos1, pos2, t)
        x2d = _ple(x1, g1, g2, route, p[i].reshape(t, PLE_DIM), w_ple[i], row(ple_norm_g[i]),
                   row(ple_gate_norm_g[i]), w_ple_gate[i])
    return x2d.reshape(batch, seq, d)
```

```python
import functools

import jax
import jax.numpy as jnp
import numpy as np
from jax import lax
from jax.experimental import pallas as pl
from jax.experimental.pallas import tpu as pltpu
from jax.experimental.pallas import tpu_sc as plsc

F32 = jnp.float32
BF16 = jnp.bfloat16
I32 = jnp.int32

EPS = 1e-6
D_MODEL = 1024
PLE_DIM = 256
RET_HEADS = 4
RET_DK = 128
RET_DV = 128
RET_CHUNK = 128
SWA_HEADS = 8
SWA_KV_HEADS = 2
SWA_GROUP = SWA_HEADS // SWA_KV_HEADS
SWA_HD = 64
WINDOW = 128
N_GROUPS = 4
EXPERTS_PER_GROUP = 8
N_EXPERTS = N_GROUPS * EXPERTS_PER_GROUP
EXPERT_FF = 256

RET_W = RET_HEADS * RET_DK
SWA_Q = SWA_HEADS * SWA_HD
SWA_KV = SWA_KV_HEADS * SWA_HD
Z_RET_W = 4 * RET_W
Z_SKV_W = 2 * SWA_KV
Z_GATE_W = 2 * D_MODEL
COL_SQ = Z_RET_W
COL_SKV = COL_SQ + SWA_Q
COL_GATE = COL_SKV + Z_SKV_W
IN_WIDTH = COL_GATE + Z_GATE_W

ROUTE_LANES = 128
LANE_GROUP0 = N_EXPERTS
LANE_C1, LANE_C2 = 34, 35
LANE_SECOND0 = 64
NEG_INF = -1e30

HALF_D = D_MODEL // 2
ROW_TILE = 512
XS_SLOTS = 3
SC_CORES = 2
SC_SUBCORES = 16
SC_WORKERS = SC_CORES * SC_SUBCORES
SC_CHUNK = 128
SC_IDX_ROWS = 8

VMEM_LIMIT = 56 * 1024 * 1024

BLOCK_ROWS = 512
W_SLAB = 256
RET_ROWS = 512
MERGE_SUB = 256
PLAN_STEPS = 4
PLE_TM = 1024
PLE_SLOTS = 3


def _rms(x, g):
    ms = jnp.mean(x * x, axis=-1, keepdims=True)
    return x * lax.rsqrt(ms + EPS) * g


def _sigmoid(x):
    return 1.0 / (1.0 + jnp.exp(-x))


def _dot(a, b):
    return jnp.dot(a, b, preferred_element_type=F32)


def _dot_nt(a, b):
    return lax.dot_general(a, b, (((1,), (1,)), ((), ())), preferred_element_type=F32)


def _dot_tn(a, b):
    return lax.dot_general(a, b, (((0,), (0,)), ((), ())), preferred_element_type=F32)


def _pack_halves(v):
    return pltpu.pack_elementwise([v[:, :HALF_D], v[:, HALF_D:]], packed_dtype=BF16)


def _unpack_halves(p):
    words = lax.bitcast_convert_type(p, jnp.uint32)
    return tuple(pltpu.unpack_elementwise(words, index=k, packed_dtype=BF16, unpacked_dtype=F32) for k in range(2))


def _retention_tables():
    f32 = np.float32
    h = RET_HEADS
    c = RET_CHUNK
    log_gamma = np.log1p(-np.exp2(f32(-5.0) - np.arange(h, dtype=f32))).astype(f32)
    pos = np.arange(c, dtype=f32)
    diff = pos[:, None] - pos[None, :]
    decay = np.where(diff[None] >= 0.0, np.exp(np.maximum(diff, f32(0.0))[None] * log_gamma[:, None, None]), f32(0.0))
    scale = f32(RET_DK ** -0.5)
    dmask = decay * scale
    zeta = np.exp((f32(c - 1.0) - pos)[None, :] * log_gamma[:, None]) * scale
    xi = np.exp((pos + f32(1.0))[None, :] * log_gamma[:, None])
    cdec = np.exp(f32(c) * log_gamma)
    bc = lambda a: np.ascontiguousarray(np.broadcast_to(a[:, :, None], (h, c, c)), dtype=f32)
    return (dmask.astype(f32), bc(zeta), bc(xi),
            np.ascontiguousarray(np.broadcast_to(cdec[:, None, None], (h, c, c)), dtype=f32))


def _swa_tables():
    f32 = np.float32
    qi = np.arange(WINDOW)[:, None]
    sj = np.arange(2 * WINDOW)[None, :]
    rel = qi + WINDOW - sj
    ok = (rel >= 0) & (rel < WINDOW)
    slopes = np.exp2(f32(-8.0) * np.arange(1, SWA_HEADS + 1, dtype=f32) / f32(SWA_HEADS)).astype(f32)
    bias = np.where(ok[None], -slopes[:, None, None] * rel.astype(f32)[None], f32(NEG_INF)).astype(f32)
    bias = bias.reshape(SWA_KV_HEADS, SWA_GROUP * WINDOW, 2 * WINDOW)
    avg = lambda heads: jnp.asarray(np.kron(np.eye(heads, dtype=f32), np.full((SWA_HD, SWA_HD), 1.0 / SWA_HD, f32)),
                                    dtype=BF16)
    return avg(SWA_HEADS), avg(SWA_KV_HEADS), jnp.asarray(bias)


def _block_kernel(per_seq, sinks_ref, x_ref, xp_ref, g_ref, w_hbm, dmask_ref, zeta_ref, xi_ref, cdec_ref,
                  gng_ref, gnb_ref, qgain_ref, kgain_ref, bq_ref, bk_ref, bias_ref,
                  wro_hbm, wso_hbm, wout_hbm, fg_ref, wre_ref, wrg_ref, bre_ref, brg_ref,
                  x1_ref, hp_ref, route_ref, cnt_ref,
                  w_ref, wro_ref, wso_ref, wout_ref, wtmp_ref, w_sem, qg_ref, kg_ref, wr_ref, br_ref,
                  zc_ref, zn_ref, yr_ref, ys_ref, state_ref, prev_ref, band_ref, tail_ref):
    g = pl.program_id(0)
    seq_start = lax.rem(g - 1, per_seq) == 0

    @pl.when(g == 0)
    def _():
        slabs = [(src, dst, c0) for src, dst in ((w_hbm, w_ref), (wro_hbm, wro_ref), (wso_hbm, wso_ref),
                                                 (wout_hbm, wout_ref))
                 for c0 in range(0, dst.shape[1], W_SLAB)]

        def slab(k):
            src, dst, c0 = slabs[k]
            return pltpu.make_async_copy(src.at[:, pl.ds(c0, W_SLAB)], wtmp_ref.at[k % 2, pl.ds(0, dst.shape[0])],
                                         w_sem.at[k % 2])

        slab(0).start()
        for k, (_, dst, c0) in enumerate(slabs):
            if k + 1 < len(slabs):
                slab(k + 1).start()
            slab(k).wait()
            dst[:, c0:c0 + W_SLAB] = wtmp_ref[k % 2, :dst.shape[0]].astype(BF16)
        for hd in range(SWA_HEADS):
            qg_ref[:, hd * SWA_HD:(hd + 1) * SWA_HD] = qgain_ref[...] * (SWA_HD ** -0.5)
        for kh in range(SWA_KV_HEADS):
            kg_ref[:, kh * SWA_HD:(kh + 1) * SWA_HD] = kgain_ref[...]
        wr_ref[...] = jnp.zeros_like(wr_ref)
        wr_ref[:, :N_EXPERTS] = wre_ref[...].astype(BF16)
        wr_ref[:, LANE_GROUP0:LANE_GROUP0 + N_GROUPS] = wrg_ref[...].astype(BF16)
        br_ref[...] = jnp.zeros_like(br_ref)
        br_ref[:, :N_EXPERTS] = bre_ref[...]
        br_ref[:, LANE_GROUP0:LANE_GROUP0 + N_GROUPS] = brg_ref[...]
        zc_ref[...] = jnp.zeros_like(zc_ref)
        tail_ref[...] = jnp.zeros_like(tail_ref)
        state_ref[...] = jnp.zeros_like(state_ref)

    @pl.when(seq_start)
    def _():
        state_ref[...] = jnp.zeros_like(state_ref)

    band_ref[:WINDOW, :] = tail_ref[...]
    band_ref[WINDOW:, :] = zc_ref[:, COL_SKV:COL_GATE]

    h = _rms(x_ref[...], g_ref[...]).astype(BF16)

    def project(c0, width):
        zn_ref[:, c0:c0 + width] = _dot(h, w_ref[:, c0:c0 + width]).astype(BF16)

    cw = 512
    proj_pieces = [functools.partial(project, j * cw, cw) for j in range(COL_SKV // cw)]
    proj_pieces.append(functools.partial(project, COL_SKV, Z_SKV_W))
    proj_pieces += [functools.partial(project, COL_GATE + j * cw, cw) for j in range(Z_GATE_W // cw)]

    def retention_rows(t):
        r_base = t * RET_ROWS
        n_chunks = RET_ROWS // RET_CHUNK
        part = lambda c, p, hd: zc_ref[r_base + c * RET_CHUNK:r_base + (c + 1) * RET_CHUNK,
                                       p * RET_W + hd * RET_DK:p * RET_W + (hd + 1) * RET_DK]
        for hd in range(RET_HEADS):
            state = state_ref[hd]
            for c in range(n_chunks):
                prev_ref[t * n_chunks + c, hd] = state.astype(BF16)
                kz = (part(c, 1, hd).astype(F32) * zeta_ref[hd]).astype(BF16)
                state = cdec_ref[hd] * state + _dot_tn(kz, part(c, 2, hd))
            state_ref[hd] = state
        out_rows = []
        for c in range(n_chunks):
            heads = []
            for hd in range(RET_HEADS):
                q = part(c, 0, hd)
                scores = _dot_nt(q, part(c, 1, hd)) * dmask_ref[hd]
                y = _dot(scores.astype(BF16), part(c, 2, hd)) + _dot(q, prev_ref[t * n_chunks + c, hd]) * xi_ref[hd]
                mu = jnp.mean(y, axis=-1, keepdims=True)
                d = y - mu
                var = jnp.mean(d * d, axis=-1, keepdims=True)
                hs = slice(hd * RET_DV, (hd + 1) * RET_DV)
                yn = d * lax.rsqrt(var + EPS) * gng_ref[:, hs] + gnb_ref[:, hs]
                gate = part(c, 3, hd).astype(F32)
                heads.append((gate * _sigmoid(gate) * yn).astype(BF16))
            out_rows.append(jnp.concatenate(heads, axis=1))
        yr_ref[r_base:r_base + RET_ROWS, :] = jnp.concatenate(out_rows, axis=0)

    group_rows = SWA_GROUP * WINDOW
    lo_q = lax.broadcasted_iota(jnp.int32, (WINDOW, 2 * SWA_HD), 1) < SWA_HD
    lo_k = lax.broadcasted_iota(jnp.int32, (2 * WINDOW, 2 * SWA_HD), 1) < SWA_HD
    in_prev = lax.broadcasted_iota(jnp.int32, (group_rows, 2 * WINDOW), 1) < WINDOW
    row_head = lax.broadcasted_iota(jnp.int32, (group_rows, 1), 0) // WINDOW

    def both_heads(v):
        swapped = pltpu.roll(v, SWA_HD, axis=1)
        return jnp.where(lo_k, v, swapped), jnp.where(lo_k, swapped, v)

    def swa_block(n):
        rows = slice(n * WINDOW, (n + 1) * WINDOW)
        kv = band_ref[n * WINDOW:(n + 2) * WINDOW, :]
        kf = kv[:, :SWA_KV].astype(F32)
        kn = kf * lax.rsqrt(_dot((kf * kf).astype(BF16), bk_ref[...]) + EPS) * kg_ref[...]
        keys = [a.astype(BF16) for a in both_heads(kn)]
        vals = [a.astype(BF16) for a in both_heads(kv[:, SWA_KV:].astype(F32))]
        qf = zc_ref[rows, COL_SQ:COL_SKV].astype(F32)
        qn = qf * lax.rsqrt(_dot((qf * qf).astype(BF16), bq_ref[...]) + EPS) * qg_ref[...]
        outs = []
        for kh in range(SWA_KV_HEADS):
            parts = []
            for j in range(SWA_GROUP // 2):
                c0 = (kh * SWA_GROUP + 2 * j) * SWA_HD
                two = qn[:, c0:c0 + 2 * SWA_HD]
                parts += [jnp.where(lo_q, two, 0.0), jnp.where(lo_q, 0.0, two)]
            qs = jnp.concatenate(parts, axis=0).astype(BF16)
            s = _dot_nt(qs, keys[kh]) + bias_ref[kh]
            if n == 0:
                s = jnp.where(jnp.logical_and(seq_start, in_prev), NEG_INF, s)
            sink = sinks_ref[kh * SWA_GROUP]
            for gi in range(1, SWA_GROUP):
                sink = jnp.where(row_head == gi, sinks_ref[kh * SWA_GROUP + gi], sink)
            m = jnp.maximum(jnp.max(s, axis=-1, keepdims=True), sink)
            p = jnp.exp(s - m)
            denom = jnp.sum(p, axis=-1, keepdims=True) + jnp.exp(sink - m)
            o = _dot(p.astype(BF16), vals[kh]) * (1.0 / denom)
            for j in range(SWA_GROUP // 2):
                even = o[(2 * j) * WINDOW:(2 * j + 1) * WINDOW]
                odd = o[(2 * j + 1) * WINDOW:(2 * j + 2) * WINDOW]
                outs.append(jnp.where(lo_q, even, odd).astype(BF16))
        ys_ref[rows, :] = jnp.concatenate(outs, axis=1)

    mix_pieces = [functools.partial(retention_rows, t) for t in range(BLOCK_ROWS // RET_ROWS)]
    mix_pieces += [functools.partial(swa_block, n) for n in range(BLOCK_ROWS // WINDOW)]

    subs = []

    def merge_sub(s):
        subs.append(_merge_rows(slice(s * MERGE_SUB, (s + 1) * MERGE_SUB), yr_ref, ys_ref, zc_ref, xp_ref, wro_ref,
                                wso_ref, wout_ref, fg_ref, wr_ref, br_ref))

    other_pieces = mix_pieces + [functools.partial(merge_sub, s) for s in range(BLOCK_ROWS // MERGE_SUB)]

    n_other = len(other_pieces)
    for k, piece in enumerate(other_pieces):
        for proj_piece in proj_pieces[k * len(proj_pieces) // n_other:(k + 1) * len(proj_pieces) // n_other]:
            proj_piece()
        piece()
    x1_ref[...] = jnp.concatenate([s[0] for s in subs], axis=0)
    hp_ref[...] = jnp.concatenate([s[1] for s in subs], axis=0)
    route_ref[...] = jnp.concatenate([s[2] for s in subs], axis=0)
    for s, sub in enumerate(subs):
        cnt_ref[s] = sub[3]

    tail_ref[...] = zc_ref[BLOCK_ROWS - WINDOW:, COL_SKV:COL_GATE]
    zc_ref[...] = zn_ref[...]


def _merge_rows(rows, yr_ref, ys_ref, z_ref, x_ref, wro_ref, wso_ref, wout_ref, fg_ref, wr_ref, br_ref):
    a = _dot(yr_ref[rows, :], wro_ref[...])
    b = _dot(ys_ref[rows, :], wso_ref[...])
    gate_r = z_ref[rows, COL_GATE:COL_GATE + D_MODEL].astype(F32)
    gate_s = z_ref[rows, COL_GATE + D_MODEL:].astype(F32)
    merged = (_sigmoid(gate_r) * a + _sigmoid(gate_s) * b).astype(BF16)
    x1 = x_ref[rows, :] + _dot(merged, wout_ref[...])
    h2 = _rms(x1, fg_ref[...])
    packed = _pack_halves(h2)

    logits = _dot(h2.astype(BF16), wr_ref[...]) + br_ref[...]
    lane = lax.broadcasted_iota(jnp.int32, logits.shape, 1)
    big = jnp.int32(ROUTE_LANES)
    is_group = jnp.logical_and(lane >= LANE_GROUP0, lane < LANE_GROUP0 + N_GROUPS)
    gl = jnp.where(is_group, logits, NEG_INF)
    gmax = jnp.max(gl, axis=-1, keepdims=True)
    g_w = 1.0 / jnp.sum(jnp.exp(gl - gmax), axis=-1, keepdims=True)
    g_sel = jnp.min(jnp.where(gl == gmax, lane, big), axis=-1, keepdims=True) - LANE_GROUP0
    in_group = jnp.logical_and(lane < N_EXPERTS, (lane >> 3) == g_sel)
    el = jnp.where(in_group, logits, NEG_INF)
    m1 = jnp.max(el, axis=-1, keepdims=True)
    i1 = jnp.min(jnp.where(el == m1, lane, big), axis=-1, keepdims=True)
    el2 = jnp.where(lane == i1, NEG_INF, el)
    m2 = jnp.max(el2, axis=-1, keepdims=True)
    i2 = jnp.min(jnp.where(el2 == m2, lane, big), axis=-1, keepdims=True)
    e2 = jnp.exp(m2 - m1)
    c1 = g_w / (1.0 + e2)
    c2 = g_w * e2 / (1.0 + e2)
    chosen = jnp.where(jnp.logical_or(lane == i1, lane == i2), 1.0, 0.0)
    route = (chosen
             + jnp.where(lane == LANE_C1, c1, 0.0)
             + jnp.where(lane == LANE_C2, c2, 0.0)
             + jnp.where(lane == i2 + LANE_SECOND0, 1.0, 0.0))
    return x1, packed, route, jnp.sum(chosen, axis=0, keepdims=True)


def _block(x2d, mix_g, w_in, gn_g, gn_b, sinks, q_g, k_g, w_ret_o, w_swa_o, w_out, ffn_g, router, seq):
    t = x2d.shape[0]
    n = t // BLOCK_ROWS
    assert seq % BLOCK_ROWS == 0
    consts = _retention_tables() + (gn_g, gn_b, q_g, k_g) + _swa_tables()
    merge_weights = (w_ret_o, w_swa_o, w_out)
    small = (ffn_g,) + tuple(router)
    hbm = pl.BlockSpec(memory_space=pl.ANY)
    full = lambda a: pl.BlockSpec(a.shape, lambda g: (0,) * a.ndim)
    cur = lambda w: pl.BlockSpec((BLOCK_ROWS, w), lambda g: (jnp.minimum(g, n - 1), 0))
    prv = lambda w: pl.BlockSpec((BLOCK_ROWS, w), lambda g: (jnp.maximum(g - 1, 0), 0))
    chunks = BLOCK_ROWS // RET_CHUNK
    subs_per_block = BLOCK_ROWS // MERGE_SUB
    return pl.pallas_call(
        functools.partial(_block_kernel, seq // BLOCK_ROWS),
        grid=(n + 1,),
        in_specs=([pl.BlockSpec(memory_space=pltpu.SMEM), cur(D_MODEL), prv(D_MODEL), full(mix_g), hbm]
                  + [full(a) for a in consts] + [hbm] * len(merge_weights) + [full(a) for a in small]),
        out_specs=[prv(D_MODEL), prv(HALF_D), prv(ROUTE_LANES),
                   pl.BlockSpec((subs_per_block, 1, ROUTE_LANES), lambda g: (jnp.maximum(g - 1, 0), 0, 0))],
        out_shape=[jax.ShapeDtypeStruct((t, D_MODEL), F32),
                   jax.ShapeDtypeStruct((t, HALF_D), I32),
                   jax.ShapeDtypeStruct((t, ROUTE_LANES), F32),
                   jax.ShapeDtypeStruct((n * subs_per_block, 1, ROUTE_LANES), F32)],
        scratch_shapes=[pltpu.VMEM(w_in.shape, BF16)] + [pltpu.VMEM(w.shape, BF16) for w in merge_weights]
                       + [pltpu.VMEM((2, D_MODEL, W_SLAB), F32), pltpu.SemaphoreType.DMA((2,)),
                        pltpu.VMEM((1, SWA_Q), F32), pltpu.VMEM((1, SWA_KV), F32),
                        pltpu.VMEM((D_MODEL, ROUTE_LANES), BF16), pltpu.VMEM((1, ROUTE_LANES), F32),
                        pltpu.VMEM((BLOCK_ROWS, IN_WIDTH), BF16), pltpu.VMEM((BLOCK_ROWS, IN_WIDTH), BF16),
                        pltpu.VMEM((BLOCK_ROWS, RET_W), BF16), pltpu.VMEM((BLOCK_ROWS, SWA_Q), BF16),
                        pltpu.VMEM((RET_HEADS, RET_DK, RET_DV), F32),
                        pltpu.VMEM((chunks, RET_HEADS, RET_DK, RET_DV), BF16),
                        pltpu.VMEM((BLOCK_ROWS + WINDOW, Z_SKV_W), BF16), pltpu.VMEM((WINDOW, Z_SKV_W), BF16)],
        compiler_params=pltpu.CompilerParams(dimension_semantics=("arbitrary",),
                                             vmem_limit_bytes=VMEM_LIMIT),
        name="block",
    )(sinks, x2d, x2d, mix_g, w_in, *consts, *merge_weights, *small)


def _plan_tables(cnt_ref, base_ref, meta_ref):
    counts = cnt_ref[...]
    nt = counts.shape[0]
    exact = functools.partial(jnp.dot, precision=lax.Precision.HIGHEST, preferred_element_type=F32)
    total = jnp.sum(counts, axis=0, keepdims=True)
    tiles_e = jnp.ceil(total * (1.0 / ROW_TILE))
    k = lax.broadcasted_iota(jnp.int32, (ROUTE_LANES, ROUTE_LANES), 0)
    lane = lax.broadcasted_iota(jnp.int32, (ROUTE_LANES, ROUTE_LANES), 1)
    tile_end = exact(jnp.broadcast_to(tiles_e, (8, ROUTE_LANES)), jnp.where(k <= lane, 1.0, 0.0))[0:1]
    tile_start = tile_end - tiles_e
    r = lax.broadcasted_iota(jnp.int32, (nt, nt), 0)
    c = lax.broadcasted_iota(jnp.int32, (nt, nt), 1)
    before = exact(jnp.where(c < r, 1.0, 0.0), counts)
    base_ref[...] = tile_start * ROW_TILE + before

    used = jnp.max(tile_end, axis=-1, keepdims=True)
    tile = k.astype(F32)
    clamped = jnp.minimum(tile, jnp.maximum(used - 1.0, 0.0))
    is_expert = lane < N_EXPERTS
    owner = jnp.sum(jnp.where(jnp.logical_and(is_expert, tile_end <= clamped), 1.0, 0.0), axis=-1, keepdims=True)
    owner = jnp.minimum(owner, N_EXPERTS - 1.0)
    mine = lane.astype(F32) == owner
    pick = lambda v: jnp.sum(jnp.where(mine, v, 0.0), axis=-1, keepdims=True)
    t1 = tile[:, 0:1]
    valid = jnp.clip(pick(total) - (t1 - pick(tile_start)) * ROW_TILE, 0.0, float(ROW_TILE))
    valid = jnp.where(t1 < used, valid, 0.0)

    nonempty = jnp.where(jnp.logical_and(lane[0:1] < N_EXPERTS, tiles_e > 0.0), 1.0, 0.0)
    rank = exact(jnp.broadcast_to(nonempty, (8, ROUTE_LANES)), jnp.where(k < lane, 1.0, 0.0))[0:1]
    nonempty_col = jnp.sum(jnp.where(k == lane, jnp.broadcast_to(nonempty, k.shape), 0.0), axis=-1, keepdims=True)
    later = jnp.where(jnp.logical_and(k > lane, nonempty_col > 0.0), tile, float(ROUTE_LANES))
    nxt_e = jnp.min(later, axis=0, keepdims=True)
    nxt_e = jnp.where(nxt_e < float(ROUTE_LANES), nxt_e, -1.0)
    first = jnp.where(jnp.logical_and(t1 == pick(tile_start), t1 < used), 1.0, 0.0)
    my_rank = pick(rank)
    slot = my_rank - 2.0 * jnp.floor(my_rank * 0.5)
    fields = (owner, valid, first, pick(nxt_e), slot, jnp.broadcast_to(used, owner.shape))
    by_tile = sum(jnp.where(lane == n, f, 0.0) for n, f in enumerate(fields))
    meta_ref[...] = by_tile.T[:8, :].astype(I32)


def _plan_kernel(cnt_ref, route_ref, base_ref, meta_ref, pos1_ref, pos2_ref):
    step = pl.program_id(0)
    nt = cnt_ref.shape[0]
    subs_per_step = nt // PLAN_STEPS

    @pl.when(step == 0)
    def _():
        _plan_tables(cnt_ref, base_ref, meta_ref)
        pos1_ref[...] = jnp.zeros_like(pos1_ref)
        pos2_ref[...] = jnp.zeros_like(pos2_ref)

    sel_row = lax.broadcasted_iota(jnp.int32, (8, ROUTE_LANES), 0)
    sel_lane = lax.broadcasted_iota(jnp.int32, (8, ROUTE_LANES), 1)
    tok_r = lax.broadcasted_iota(jnp.int32, (MERGE_SUB, MERGE_SUB), 0)
    tok_c = lax.broadcasted_iota(jnp.int32, (MERGE_SUB, MERGE_SUB), 1)
    eye = jnp.where(tok_r == tok_c, 1.0, 0.0).astype(BF16)
    upto = jnp.where(tok_r <= tok_c, 1.0, 0.0).astype(BF16)
    sum_experts = lambda a: jnp.sum(a, axis=0, keepdims=True)

    def sub_positions(local, carry):
        s = step * subs_per_step + local
        route = route_ref[pl.ds(pl.multiple_of(local * MERGE_SUB, MERGE_SUB), MERGE_SUB), :].astype(BF16)
        record_t = _dot_tn(route, eye)
        incl_t = _dot_tn(route, upto)[:N_EXPERTS]
        both_t = record_t[:N_EXPERTS]
        second_t = record_t[LANE_SECOND0:LANE_SECOND0 + N_EXPERTS]
        rank2 = sum_experts(second_t * incl_t) - 1.0
        rank1 = sum_experts(both_t * incl_t) - rank2 - 2.0
        base_row = jnp.where(sel_lane < N_EXPERTS, jnp.broadcast_to(base_ref[pl.ds(s, 1), :], (8, ROUTE_LANES)), 0.0)
        hi = jnp.floor(base_row * (1.0 / 256.0))
        lo = base_row - 256.0 * hi
        hi2 = pltpu.roll(hi, LANE_SECOND0, axis=1)
        lo2 = pltpu.roll(lo, LANE_SECOND0, axis=1)
        table = (jnp.where(sel_row == 0, hi, 0.0) + jnp.where(sel_row == 1, lo, 0.0)
                 + jnp.where(sel_row == 2, hi2, 0.0) + jnp.where(sel_row == 3, lo2, 0.0))
        r = _dot_nt(table.astype(BF16), route)
        base_both = 256.0 * r[0:1] + r[1:2]
        base2 = 256.0 * r[2:3] + r[3:4]
        worker = s // subs_per_worker
        chunk0 = lax.rem(s, subs_per_worker) * chunks_per_sub
        for ref, row in ((pos1_ref, base_both - base2 + rank1), (pos2_ref, base2 + rank2)):
            for c in range(chunks_per_sub):
                ref[worker, pl.ds(chunk0 + c, 1), :] = row[:, c * SC_CHUNK:(c + 1) * SC_CHUNK].astype(I32)
        return carry

    chunks_per_sub = MERGE_SUB // SC_CHUNK
    subs_per_worker = nt // SC_WORKERS
    lax.fori_loop(0, subs_per_step, sub_positions, 0, unroll=4)


def _tile_plan(cnt, route, n_tiles):
    assert n_tiles <= ROUTE_LANES
    nt = cnt.shape[0]
    assert nt % SC_WORKERS == 0 and MERGE_SUB % SC_CHUNK == 0 and nt * MERGE_SUB <= SC_WORKERS * SC_IDX_ROWS * SC_CHUNK
    assert nt % PLAN_STEPS == 0
    pos = jax.ShapeDtypeStruct((SC_WORKERS, SC_IDX_ROWS, SC_CHUNK), I32)
    outs = [jax.ShapeDtypeStruct((nt, ROUTE_LANES), F32), jax.ShapeDtypeStruct((8, ROUTE_LANES), I32), pos, pos]
    whole = lambda a: pl.BlockSpec(a.shape, lambda i: (0,) * len(a.shape))
    _, meta, pos1, pos2 = pl.pallas_call(
        _plan_kernel,
        grid=(PLAN_STEPS,),
        in_specs=[pl.BlockSpec((nt, ROUTE_LANES), lambda i: (0, 0)),
                  pl.BlockSpec((route.shape[0] // PLAN_STEPS, ROUTE_LANES), lambda i: (i, 0))],
        out_specs=[whole(o) for o in outs],
        out_shape=outs,
        compiler_params=pltpu.CompilerParams(dimension_semantics=("arbitrary",), vmem_limit_bytes=VMEM_LIMIT),
        name="tile_plan",
    )(cnt.reshape(nt, ROUTE_LANES), route)
    return meta, pos1, pos2


def _sc_mesh():
    return plsc.VectorSubcoreMesh(core_axis_name="c", subcore_axis_name="s")


def _sc_worker():
    return lax.axis_index("s") * SC_CORES + lax.axis_index("c")


def _dispatch(hp, pos1, pos2, n_rows):
    t = hp.shape[0]
    per_worker = t // SC_WORKERS
    k = per_worker // SC_CHUNK
    idx = pltpu.VMEM((SC_IDX_ROWS, SC_CHUNK), I32)

    @functools.partial(pl.kernel, mesh=_sc_mesh(), out_type=jax.ShapeDtypeStruct((n_rows, HALF_D), I32),
                       scratch_types=[idx, idx, pltpu.VMEM((SC_CHUNK, HALF_D), I32)], name="moe_dispatch")
    def run(h_hbm, p1_hbm, p2_hbm, xs_hbm, p1_v, p2_v, rows_v):
        wid = _sc_worker()
        pltpu.sync_copy(p1_hbm.at[wid], p1_v)
        pltpu.sync_copy(p2_hbm.at[wid], p2_v)
        for j in range(k):
            pltpu.sync_copy(h_hbm.at[pl.ds(wid * per_worker + j * SC_CHUNK, SC_CHUNK)], rows_v)
            pltpu.sync_copy(rows_v, xs_hbm.at[p1_v.at[j]])
            pltpu.sync_copy(rows_v, xs_hbm.at[p2_v.at[j]])

    return run(hp, pos1, pos2)


def _combine_gather(ys, pos1, pos2, t):
    per_worker = t // SC_WORKERS
    k = per_worker // SC_CHUNK
    idx = pltpu.VMEM((SC_IDX_ROWS, SC_CHUNK), I32)
    out = jax.ShapeDtypeStruct((t, HALF_D), I32)

    @functools.partial(pl.kernel, mesh=_sc_mesh(), out_type=(out, out),
                       scratch_types=[idx, idx, pltpu.VMEM((SC_CHUNK, HALF_D), I32)], name="moe_combine")
    def run(ys_hbm, p1_hbm, p2_hbm, g1_hbm, g2_hbm, p1_v, p2_v, rows_v):
        wid = _sc_worker()
        pltpu.sync_copy(p1_hbm.at[wid], p1_v)
        pltpu.sync_copy(p2_hbm.at[wid], p2_v)
        for j in range(k):
            dst = pl.ds(wid * per_worker + j * SC_CHUNK, SC_CHUNK)
            pltpu.sync_copy(ys_hbm.at[p1_v.at[j]], rows_v)
            pltpu.sync_copy(rows_v, g1_hbm.at[dst])
            pltpu.sync_copy(ys_hbm.at[p2_v.at[j]], rows_v)
            pltpu.sync_copy(rows_v, g2_hbm.at[dst])

    return run(ys, pos1, pos2)


META_OWNER, META_VALID, META_FIRST, META_NEXT, META_SLOT, META_USED = range(6)


def _experts_kernel(meta_ref, xs_hbm, wg_hbm, wu_hbm, wd_hbm, ys_ref,
                    xs_buf, xs_sem, wg_f, wu_f, wd_f, wg_bf, wu_bf, wd_bf, w_sem):
    i = pl.program_id(0)
    used = meta_ref[META_USED, 0]
    valid = meta_ref[META_VALID, i]
    owner = meta_ref[META_OWNER, i]
    nxt = meta_ref[META_NEXT, i]

    def rows_in(tile):
        start = pl.multiple_of(tile * ROW_TILE, ROW_TILE)
        slot = lax.rem(tile, XS_SLOTS)
        return pltpu.make_async_copy(xs_hbm.at[pl.ds(start, ROW_TILE)], xs_buf.at[slot], xs_sem.at[slot])

    @pl.when(i == 0)
    def _():
        for ahead in range(XS_SLOTS - 1):
            pl.when(ahead < used)(rows_in(ahead).start)

    @pl.when(i + (XS_SLOTS - 1) < used)
    def _():
        rows_in(i + (XS_SLOTS - 1)).start()

    def fetch(expert, slot):
        pairs = ((wg_hbm, wg_f), (wu_hbm, wu_f), (wd_hbm, wd_f))
        return [pltpu.make_async_copy(w.at[expert], buf.at[slot], w_sem.at[slot, n])
                for n, (w, buf) in enumerate(pairs)]

    @pl.when(i == 0)
    def _():
        for copy in fetch(owner, 0):
            copy.start()

    @pl.when(meta_ref[META_FIRST, i] == 1)
    def _():
        slot = meta_ref[META_SLOT, i]
        for copy in fetch(owner, slot):
            copy.wait()
        wg_bf[...] = wg_f[slot].astype(BF16)
        wu_bf[...] = wu_f[slot].astype(BF16)
        wd_bf[...] = wd_f[slot].astype(BF16)

        @pl.when(nxt >= 0)
        def _():
            for copy in fetch(nxt, 1 - slot):
                copy.start()

    @pl.when(i < used)
    def _():
        rows_in(i).wait()
        row = lax.broadcasted_iota(jnp.int32, (ROW_TILE, HALF_D), 0)
        words = xs_buf[lax.rem(i, XS_SLOTS)]
        packed = jnp.where(row < valid, words, 0)
        hi, lo = _unpack_halves(packed)
        hi = hi.astype(BF16)
        lo = lo.astype(BF16)
        hg = _dot(hi, wg_bf[:HALF_D, :]) + _dot(lo, wg_bf[HALF_D:, :])
        hu = _dot(hi, wu_bf[:HALF_D, :]) + _dot(lo, wu_bf[HALF_D:, :])
        hid = (hg * _sigmoid(hg) * hu).astype(BF16)
        ys_ref[...] = _pack_halves(_dot(hid, wd_bf[...]))


def _experts(xs, meta, n_tiles, wg, wu, wd):
    tile = lambda i, meta: (jnp.minimum(i, meta[META_USED, 0] - 1), 0)
    hbm = pl.BlockSpec(memory_space=pl.ANY)
    up_shape, down_shape = (D_MODEL, EXPERT_FF), (EXPERT_FF, D_MODEL)
    grid_spec = pltpu.PrefetchScalarGridSpec(
        num_scalar_prefetch=1,
        grid=(n_tiles,),
        in_specs=[hbm, hbm, hbm, hbm],
        out_specs=pl.BlockSpec((ROW_TILE, HALF_D), tile),
        scratch_shapes=[pltpu.VMEM((XS_SLOTS, ROW_TILE, HALF_D), I32), pltpu.SemaphoreType.DMA((XS_SLOTS,)),
                        pltpu.VMEM((2,) + up_shape, F32), pltpu.VMEM((2,) + up_shape, F32),
                        pltpu.VMEM((2,) + down_shape, F32),
                        pltpu.VMEM(up_shape, BF16), pltpu.VMEM(up_shape, BF16), pltpu.VMEM(down_shape, BF16),
                        pltpu.SemaphoreType.DMA((2, 3))],
    )
    return pl.pallas_call(
        _experts_kernel,
        grid_spec=grid_spec,
        out_shape=jax.ShapeDtypeStruct((n_tiles * ROW_TILE, HALF_D), I32),
        compiler_params=pltpu.CompilerParams(dimension_semantics=("arbitrary",),
                                             vmem_limit_bytes=VMEM_LIMIT),
        name="experts",
    )(meta, xs, wg, wu, wd)


def _ple_kernel(x1_hbm, g1_hbm, g2_hbm, route_ref, p_ref, wp_f32, pg_ref, gg_ref, wgate_f32, o_ref,
                wp_ref, wgate_ref, x1_buf, g1_buf, g2_buf, ring_sem):
    i = pl.program_id(0)
    steps = pl.num_programs(0)
    streams = ((x1_hbm, x1_buf), (g1_hbm, g1_buf), (g2_hbm, g2_buf))

    def rows_in(step):
        start = pl.multiple_of(step * PLE_TM, PLE_TM)
        slot = lax.rem(step, PLE_SLOTS)
        return [pltpu.make_async_copy(src.at[pl.ds(start, PLE_TM)], buf.at[slot], ring_sem.at[n, slot])
                for n, (src, buf) in enumerate(streams)]

    def start_rows(step):
        for copy in rows_in(step):
            copy.start()

    @pl.when(i == 0)
    def _():
        for ahead in range(PLE_SLOTS - 1):
            pl.when(ahead < steps)(functools.partial(start_rows, ahead))
        wp_ref[...] = wp_f32[...].astype(BF16)
        wgate_ref[...] = wgate_f32[...].astype(BF16)

    pl.when(i + (PLE_SLOTS - 1) < steps)(functools.partial(start_rows, i + (PLE_SLOTS - 1)))

    for copy in rows_in(i):
        copy.wait()
    slot = lax.rem(i, PLE_SLOTS)
    x1_ref, g1_ref, g2_ref = x1_buf.at[slot], g1_buf.at[slot], g2_buf.at[slot]

    route = route_ref[...]
    lane = lax.broadcasted_iota(jnp.int32, route.shape, 1)
    pick = lambda ln: jnp.sum(jnp.where(lane == ln, route, 0.0), axis=-1, keepdims=True)
    c1 = pick(LANE_C1)
    c2 = pick(LANE_C2)
    a_hi, a_lo = _unpack_halves(g1_ref[...])
    b_hi, b_lo = _unpack_halves(g2_ref[...])
    x = x1_ref[...] + jnp.concatenate([c1 * a_hi + c2 * b_hi, c1 * a_lo + c2 * b_lo], axis=1)
    ple = _rms(_dot(p_ref[...].astype(BF16), wp_ref[...]), pg_ref[...])
    gate = _sigmoid(_dot(_rms(x, gg_ref[...]).astype(BF16), wgate_ref[...]))
    o_ref[...] = x + gate * ple


def _ple(x1, g1, g2, route, p2d, w_ple, ple_g, gate_g, w_gate):
    t = x1.shape[0]
    row = lambda w: pl.BlockSpec((PLE_TM, w), lambda i: (i, 0))
    full = lambda a: pl.BlockSpec(a.shape, lambda i: (0,) * a.ndim)
    hbm = pl.BlockSpec(memory_space=pl.ANY)
    return pl.pallas_call(
        _ple_kernel,
        grid=(t // PLE_TM,),
        in_specs=[hbm, hbm, hbm, row(ROUTE_LANES), row(PLE_DIM),
                  full(w_ple), full(ple_g), full(gate_g), full(w_gate)],
        out_specs=row(D_MODEL),
        out_shape=jax.ShapeDtypeStruct((t, D_MODEL), F32),
        scratch_shapes=[pltpu.VMEM(w_ple.shape, BF16), pltpu.VMEM(w_gate.shape, BF16),
                        pltpu.VMEM((PLE_SLOTS, PLE_TM, D_MODEL), F32),
                        pltpu.VMEM((PLE_SLOTS, PLE_TM, HALF_D), I32), pltpu.VMEM((PLE_SLOTS, PLE_TM, HALF_D), I32),
                        pltpu.SemaphoreType.DMA((3, PLE_SLOTS))],
        compiler_params=pltpu.CompilerParams(dimension_semantics=("arbitrary",),
                                             vmem_limit_bytes=VMEM_LIMIT),
        name="ple",
    )(x1, g1, g2, route, p2d, w_ple, ple_g, gate_g, w_gate)


def kernel(x, p, mix_norm_g, w_in, ret_gn_g, ret_gn_b, w_ret_o, q_norm_g, k_norm_g, attn_sinks,
           w_swa_o, w_out, ffn_norm_g, w_router_group, b_router_group, w_router_expert,
           b_router_expert, w_exp_gate, w_exp_up, w_exp_down, ple_gate_norm_g, w_ple_gate,
           w_ple, ple_norm_g):
    batch, seq, d = x.shape
    t = batch * seq
    depth = w_in.shape[0]
    n_tiles = 2 * t // ROW_TILE + N_EXPERTS
    row = lambda a: a.reshape(1, -1)
    x2d = x.reshape(t, d)
    for i in range(depth):
        router = (w_router_expert[i], w_router_group[i], row(b_router_expert[i]), row(b_router_group[i]))
        x1, hp, route, cnt = _block(x2d, row(mix_norm_g[i]), w_in[i], row(ret_gn_g[i]),
                                    row(ret_gn_b[i]), attn_sinks[i], row(q_norm_g[i]), row(k_norm_g[i]),
                                    w_ret_o[i], w_swa_o[i], w_out[i], row(ffn_norm_g[i]), router, seq)
        meta, pos1, pos2 = _tile_plan(cnt, route, n_tiles)
        xs = _dispatch(hp, pos1, pos2, n_tiles * ROW_TILE)
        ys = _experts(xs, meta, n_tiles, w_exp_gate[i], w_exp_up[i], w_exp_down[i])
        g1, g2 = _combine_gather(ys, pos1, pos2, t)
        x2d = _ple(x1, g1, g2, route, p[i].reshape(t, PLE_DIM), w_ple[i], row(ple_norm_g[i]),
                   row(ple_gate_norm_g[i]), w_ple_gate[i])
    return x2d.reshape(batch, seq, d)
```

```python
import functools

import jax
import jax.numpy as jnp
import numpy as np
from jax import lax
from jax.experimental import pallas as pl
from jax.experimental.pallas import tpu as pltpu
from jax.experimental.pallas import tpu_sc as plsc

F32 = jnp.float32
BF16 = jnp.bfloat16
I32 = jnp.int32

EPS = 1e-6
D_MODEL = 1024
PLE_DIM = 256
RET_HEADS = 4
RET_DK = 128
RET_DV = 128
RET_CHUNK = 128
SWA_HEADS = 8
SWA_KV_HEADS = 2
SWA_GROUP = SWA_HEADS // SWA_KV_HEADS
SWA_HD = 64
WINDOW = 128
N_GROUPS = 4
EXPERTS_PER_GROUP = 8
N_EXPERTS = N_GROUPS * EXPERTS_PER_GROUP
EXPERT_FF = 256

RET_W = RET_HEADS * RET_DK
SWA_Q = SWA_HEADS * SWA_HD
SWA_KV = SWA_KV_HEADS * SWA_HD
Z_RET_W = 4 * RET_W
Z_SKV_W = 2 * SWA_KV
Z_GATE_W = 2 * D_MODEL
COL_SQ = Z_RET_W
COL_SKV = COL_SQ + SWA_Q
COL_GATE = COL_SKV + Z_SKV_W
IN_WIDTH = COL_GATE + Z_GATE_W

ROUTE_LANES = 128
LANE_GROUP0 = N_EXPERTS
LANE_C1, LANE_C2 = 34, 35
LANE_SECOND0 = 64
NEG_INF = -1e30

HALF_D = D_MODEL // 2
ROW_TILE = 512
XS_SLOTS = 3
SC_CORES = 2
SC_SUBCORES = 16
SC_WORKERS = SC_CORES * SC_SUBCORES
SC_CHUNK = 128
SC_IDX_ROWS = 8

VMEM_LIMIT = 56 * 1024 * 1024

BLOCK_ROWS = 512
W_SLAB = 256
W_SLOTS = 4
RET_ROWS = 512
MERGE_SUB = 256
PLAN_STEPS = 4
PLE_TM = 1024
PLE_SLOTS = 3


def _rms(x, g):
    ms = jnp.mean(x * x, axis=-1, keepdims=True)
    return x * lax.rsqrt(ms + EPS) * g


def _sigmoid(x):
    return 1.0 / (1.0 + jnp.exp(-x))


def _dot(a, b):
    return jnp.dot(a, b, preferred_element_type=F32)


def _dot_nt(a, b):
    return lax.dot_general(a, b, (((1,), (1,)), ((), ())), preferred_element_type=F32)


def _dot_tn(a, b):
    return lax.dot_general(a, b, (((0,), (0,)), ((), ())), preferred_element_type=F32)


def _pack_halves(v):
    return pltpu.pack_elementwise([v[:, :HALF_D], v[:, HALF_D:]], packed_dtype=BF16)


def _unpack_halves(p):
    words = lax.bitcast_convert_type(p, jnp.uint32)
    return tuple(pltpu.unpack_elementwise(words, index=k, packed_dtype=BF16, unpacked_dtype=F32) for k in range(2))


def _retention_tables():
    f32 = np.float32
    h = RET_HEADS
    c = RET_CHUNK
    log_gamma = np.log1p(-np.exp2(f32(-5.0) - np.arange(h, dtype=f32))).astype(f32)
    pos = np.arange(c, dtype=f32)
    diff = pos[:, None] - pos[None, :]
    decay = np.where(diff[None] >= 0.0, np.exp(np.maximum(diff, f32(0.0))[None] * log_gamma[:, None, None]), f32(0.0))
    scale = f32(RET_DK ** -0.5)
    dmask = decay * scale
    zeta = np.exp((f32(c - 1.0) - pos)[None, :] * log_gamma[:, None]) * scale
    xi = np.exp((pos + f32(1.0))[None, :] * log_gamma[:, None])
    cdec = np.exp(f32(c) * log_gamma)
    bc = lambda a: np.ascontiguousarray(np.broadcast_to(a[:, :, None], (h, c, c)), dtype=f32)
    return (dmask.astype(f32), bc(zeta), bc(xi),
            np.ascontiguousarray(np.broadcast_to(cdec[:, None, None], (h, c, c)), dtype=f32))


def _swa_tables():
    f32 = np.float32
    qi = np.arange(WINDOW)[:, None]
    sj = np.arange(2 * WINDOW)[None, :]
    rel = qi + WINDOW - sj
    ok = (rel >= 0) & (rel < WINDOW)
    slopes = np.exp2(f32(-8.0) * np.arange(1, SWA_HEADS + 1, dtype=f32) / f32(SWA_HEADS)).astype(f32)
    bias = np.where(ok[None], -slopes[:, None, None] * rel.astype(f32)[None], f32(NEG_INF)).astype(f32)
    bias = bias.reshape(SWA_KV_HEADS, SWA_GROUP * WINDOW, 2 * WINDOW)
    avg = lambda heads: jnp.asarray(np.kron(np.eye(heads, dtype=f32), np.full((SWA_HD, SWA_HD), 1.0 / SWA_HD, f32)),
                                    dtype=BF16)
    return avg(SWA_HEADS), avg(SWA_KV_HEADS), jnp.asarray(bias)


def _block_kernel(per_seq, sinks_ref, x_ref, xp_ref, g_ref, w_hbm, dmask_ref, zeta_ref, xi_ref, cdec_ref,
                  gng_ref, gnb_ref, qgain_ref, kgain_ref, bq_ref, bk_ref, bias_ref,
                  wro_hbm, wso_hbm, wout_hbm, fg_ref, wre_ref, wrg_ref, bre_ref, brg_ref,
                  x1_ref, hp_ref, route_ref, cnt_ref,
                  w_ref, wro_ref, wso_ref, wout_ref, wtmp_ref, w_sem, qg_ref, kg_ref, wr_ref, br_ref,
                  zc_ref, zn_ref, yr_ref, ys_ref, state_ref, prev_ref, band_ref, tail_ref):
    g = pl.program_id(0)
    seq_start = lax.rem(g - 1, per_seq) == 0

    @pl.when(g == 0)
    def _():
        slabs = [(src, dst, c0) for src, dst in ((w_hbm, w_ref), (wro_hbm, wro_ref), (wso_hbm, wso_ref),
                                                 (wout_hbm, wout_ref))
                 for c0 in range(0, dst.shape[1], W_SLAB)]

        def slab(k):
            src, dst, c0 = slabs[k]
            return pltpu.make_async_copy(src.at[:, pl.ds(c0, W_SLAB)],
                                         wtmp_ref.at[k % W_SLOTS, pl.ds(0, dst.shape[0])], w_sem.at[k % W_SLOTS])

        for k in range(W_SLOTS - 1):
            slab(k).start()
        for k, (_, dst, c0) in enumerate(slabs):
            if k + W_SLOTS - 1 < len(slabs):
                slab(k + W_SLOTS - 1).start()
            slab(k).wait()
            dst[:, c0:c0 + W_SLAB] = wtmp_ref[k % W_SLOTS, :dst.shape[0]].astype(BF16)
        for hd in range(SWA_HEADS):
            qg_ref[:, hd * SWA_HD:(hd + 1) * SWA_HD] = qgain_ref[...] * (SWA_HD ** -0.5)
        for kh in range(SWA_KV_HEADS):
            kg_ref[:, kh * SWA_HD:(kh + 1) * SWA_HD] = kgain_ref[...]
        wr_ref[...] = jnp.zeros_like(wr_ref)
        wr_ref[:, :N_EXPERTS] = wre_ref[...].astype(BF16)
        wr_ref[:, LANE_GROUP0:LANE_GROUP0 + N_GROUPS] = wrg_ref[...].astype(BF16)
        br_ref[...] = jnp.zeros_like(br_ref)
        br_ref[:, :N_EXPERTS] = bre_ref[...]
        br_ref[:, LANE_GROUP0:LANE_GROUP0 + N_GROUPS] = brg_ref[...]
        zc_ref[...] = jnp.zeros_like(zc_ref)
        tail_ref[...] = jnp.zeros_like(tail_ref)
        state_ref[...] = jnp.zeros_like(state_ref)

    @pl.when(seq_start)
    def _():
        state_ref[...] = jnp.zeros_like(state_ref)

    band_ref[:WINDOW, :] = tail_ref[...]
    band_ref[WINDOW:, :] = zc_ref[:, COL_SKV:COL_GATE]

    h = _rms(x_ref[...], g_ref[...]).astype(BF16)

    def project(c0, width):
        zn_ref[:, c0:c0 + width] = _dot(h, w_ref[:, c0:c0 + width]).astype(BF16)

    cw = 512
    proj_pieces = [functools.partial(project, j * cw, cw) for j in range(COL_SKV // cw)]
    proj_pieces.append(functools.partial(project, COL_SKV, Z_SKV_W))
    proj_pieces += [functools.partial(project, COL_GATE + j * cw, cw) for j in range(Z_GATE_W // cw)]

    def retention_rows(t):
        r_base = t * RET_ROWS
        n_chunks = RET_ROWS // RET_CHUNK
        part = lambda c, p, hd: zc_ref[r_base + c * RET_CHUNK:r_base + (c + 1) * RET_CHUNK,
                                       p * RET_W + hd * RET_DK:p * RET_W + (hd + 1) * RET_DK]
        for hd in range(RET_HEADS):
            state = state_ref[hd]
            for c in range(n_chunks):
                prev_ref[t * n_chunks + c, hd] = state.astype(BF16)
                kz = (part(c, 1, hd).astype(F32) * zeta_ref[hd]).astype(BF16)
                state = cdec_ref[hd] * state + _dot_tn(kz, part(c, 2, hd))
            state_ref[hd] = state
        out_rows = []
        for c in range(n_chunks):
            heads = []
            for hd in range(RET_HEADS):
                q = part(c, 0, hd)
                scores = _dot_nt(q, part(c, 1, hd)) * dmask_ref[hd]
                y = _dot(scores.astype(BF16), part(c, 2, hd)) + _dot(q, prev_ref[t * n_chunks + c, hd]) * xi_ref[hd]
                mu = jnp.mean(y, axis=-1, keepdims=True)
                d = y - mu
                var = jnp.mean(d * d, axis=-1, keepdims=True)
                hs = slice(hd * RET_DV, (hd + 1) * RET_DV)
                yn = d * lax.rsqrt(var + EPS) * gng_ref[:, hs] + gnb_ref[:, hs]
                gate = part(c, 3, hd).astype(F32)
                heads.append((gate * _sigmoid(gate) * yn).astype(BF16))
            out_rows.append(jnp.concatenate(heads, axis=1))
        yr_ref[r_base:r_base + RET_ROWS, :] = jnp.concatenate(out_rows, axis=0)

    group_rows = SWA_GROUP * WINDOW
    lo_q = lax.broadcasted_iota(jnp.int32, (WINDOW, 2 * SWA_HD), 1) < SWA_HD
    lo_k = lax.broadcasted_iota(jnp.int32, (2 * WINDOW, 2 * SWA_HD), 1) < SWA_HD
    in_prev = lax.broadcasted_iota(jnp.int32, (group_rows, 2 * WINDOW), 1) < WINDOW
    row_head = lax.broadcasted_iota(jnp.int32, (group_rows, 1), 0) // WINDOW

    def both_heads(v):
        swapped = pltpu.roll(v, SWA_HD, axis=1)
        return jnp.where(lo_k, v, swapped), jnp.where(lo_k, swapped, v)

    def swa_block(n):
        rows = slice(n * WINDOW, (n + 1) * WINDOW)
        kv = band_ref[n * WINDOW:(n + 2) * WINDOW, :]
        kf = kv[:, :SWA_KV].astype(F32)
        kn = kf * lax.rsqrt(_dot((kf * kf).astype(BF16), bk_ref[...]) + EPS) * kg_ref[...]
        keys = [a.astype(BF16) for a in both_heads(kn)]
        vals = [a.astype(BF16) for a in both_heads(kv[:, SWA_KV:].astype(F32))]
        qf = zc_ref[rows, COL_SQ:COL_SKV].astype(F32)
        qn = qf * lax.rsqrt(_dot((qf * qf).astype(BF16), bq_ref[...]) + EPS) * qg_ref[...]
        outs = []
        for kh in range(SWA_KV_HEADS):
            parts = []
            for j in range(SWA_GROUP // 2):
                c0 = (kh * SWA_GROUP + 2 * j) * SWA_HD
                two = qn[:, c0:c0 + 2 * SWA_HD]
                parts += [jnp.where(lo_q, two, 0.0), jnp.where(lo_q, 0.0, two)]
            qs = jnp.concatenate(parts, axis=0).astype(BF16)
            s = _dot_nt(qs, keys[kh]) + bias_ref[kh]
            if n == 0:
                s = jnp.where(jnp.logical_and(seq_start, in_prev), NEG_INF, s)
            sink = sinks_ref[kh * SWA_GROUP]
            for gi in range(1, SWA_GROUP):
                sink = jnp.where(row_head == gi, sinks_ref[kh * SWA_GROUP + gi], sink)
            m = jnp.maximum(jnp.max(s, axis=-1, keepdims=True), sink)
            p = jnp.exp(s - m)
            denom = jnp.sum(p, axis=-1, keepdims=True) + jnp.exp(sink - m)
            o = _dot(p.astype(BF16), vals[kh]) * (1.0 / denom)
            for j in range(SWA_GROUP // 2):
                even = o[(2 * j) * WINDOW:(2 * j + 1) * WINDOW]
                odd = o[(2 * j + 1) * WINDOW:(2 * j + 2) * WINDOW]
                outs.append(jnp.where(lo_q, even, odd).astype(BF16))
        ys_ref[rows, :] = jnp.concatenate(outs, axis=1)

    mix_pieces = [functools.partial(retention_rows, t) for t in range(BLOCK_ROWS // RET_ROWS)]
    mix_pieces += [functools.partial(swa_block, n) for n in range(BLOCK_ROWS // WINDOW)]

    subs = []

    def merge_sub(s):
        subs.append(_merge_rows(slice(s * MERGE_SUB, (s + 1) * MERGE_SUB), yr_ref, ys_ref, zc_ref, xp_ref, wro_ref,
                                wso_ref, wout_ref, fg_ref, wr_ref, br_ref))

    other_pieces = mix_pieces + [functools.partial(merge_sub, s) for s in range(BLOCK_ROWS // MERGE_SUB)]

    n_other = len(other_pieces)
    for k, piece in enumerate(other_pieces):
        for proj_piece in proj_pieces[k * len(proj_pieces) // n_other:(k + 1) * len(proj_pieces) // n_other]:
            proj_piece()
        piece()
    x1_ref[...] = jnp.concatenate([s[0] for s in subs], axis=0)
    hp_ref[...] = jnp.concatenate([s[1] for s in subs], axis=0)
    route_ref[...] = jnp.concatenate([s[2] for s in subs], axis=0)
    for s, sub in enumerate(subs):
        cnt_ref[s] = sub[3]

    tail_ref[...] = zc_ref[BLOCK_ROWS - WINDOW:, COL_SKV:COL_GATE]
    zc_ref[...] = zn_ref[...]


def _merge_rows(rows, yr_ref, ys_ref, z_ref, x_ref, wro_ref, wso_ref, wout_ref, fg_ref, wr_ref, br_ref):
    a = _dot(yr_ref[rows, :], wro_ref[...])
    b = _dot(ys_ref[rows, :], wso_ref[...])
    gate_r = z_ref[rows, COL_GATE:COL_GATE + D_MODEL].astype(F32)
    gate_s = z_ref[rows, COL_GATE + D_MODEL:].astype(F32)
    merged = (_sigmoid(gate_r) * a + _sigmoid(gate_s) * b).astype(BF16)
    x1 = x_ref[rows, :] + _dot(merged, wout_ref[...])
    h2 = _rms(x1, fg_ref[...])
    packed = _pack_halves(h2)

    logits = _dot(h2.astype(BF16), wr_ref[...]) + br_ref[...]
    lane = lax.broadcasted_iota(jnp.int32, logits.shape, 1)
    big = jnp.int32(ROUTE_LANES)
    is_group = jnp.logical_and(lane >= LANE_GROUP0, lane < LANE_GROUP0 + N_GROUPS)
    gl = jnp.where(is_group, logits, NEG_INF)
    gmax = jnp.max(gl, axis=-1, keepdims=True)
    g_w = 1.0 / jnp.sum(jnp.exp(gl - gmax), axis=-1, keepdims=True)
    g_sel = jnp.min(jnp.where(gl == gmax, lane, big), axis=-1, keepdims=True) - LANE_GROUP0
    in_group = jnp.logical_and(lane < N_EXPERTS, (lane >> 3) == g_sel)
    el = jnp.where(in_group, logits, NEG_INF)
    m1 = jnp.max(el, axis=-1, keepdims=True)
    i1 = jnp.min(jnp.where(el == m1, lane, big), axis=-1, keepdims=True)
    el2 = jnp.where(lane == i1, NEG_INF, el)
    m2 = jnp.max(el2, axis=-1, keepdims=True)
    i2 = jnp.min(jnp.where(el2 == m2, lane, big), axis=-1, keepdims=True)
    e2 = jnp.exp(m2 - m1)
    c1 = g_w / (1.0 + e2)
    c2 = g_w * e2 / (1.0 + e2)
    chosen = jnp.where(jnp.logical_or(lane == i1, lane == i2), 1.0, 0.0)
    route = (chosen
             + jnp.where(lane == LANE_C1, c1, 0.0)
             + jnp.where(lane == LANE_C2, c2, 0.0)
             + jnp.where(lane == i2 + LANE_SECOND0, 1.0, 0.0))
    return x1, packed, route, jnp.sum(chosen, axis=0, keepdims=True)


def _block(x2d, mix_g, w_in, gn_g, gn_b, sinks, q_g, k_g, w_ret_o, w_swa_o, w_out, ffn_g, router, seq):
    t = x2d.shape[0]
    n = t // BLOCK_ROWS
    assert seq % BLOCK_ROWS == 0
    consts = _retention_tables() + (gn_g, gn_b, q_g, k_g) + _swa_tables()
    merge_weights = (w_ret_o, w_swa_o, w_out)
    small = (ffn_g,) + tuple(router)
    hbm = pl.BlockSpec(memory_space=pl.ANY)
    full = lambda a: pl.BlockSpec(a.shape, lambda g: (0,) * a.ndim)
    cur = lambda w: pl.BlockSpec((BLOCK_ROWS, w), lambda g: (jnp.minimum(g, n - 1), 0))
    prv = lambda w: pl.BlockSpec((BLOCK_ROWS, w), lambda g: (jnp.maximum(g - 1, 0), 0))
    chunks = BLOCK_ROWS // RET_CHUNK
    subs_per_block = BLOCK_ROWS // MERGE_SUB
    return pl.pallas_call(
        functools.partial(_block_kernel, seq // BLOCK_ROWS),
        grid=(n + 1,),
        in_specs=([pl.BlockSpec(memory_space=pltpu.SMEM), cur(D_MODEL), prv(D_MODEL), full(mix_g), hbm]
                  + [full(a) for a in consts] + [hbm] * len(merge_weights) + [full(a) for a in small]),
        out_specs=[prv(D_MODEL), prv(HALF_D), prv(ROUTE_LANES),
                   pl.BlockSpec((subs_per_block, 1, ROUTE_LANES), lambda g: (jnp.maximum(g - 1, 0), 0, 0))],
        out_shape=[jax.ShapeDtypeStruct((t, D_MODEL), F32),
                   jax.ShapeDtypeStruct((t, HALF_D), I32),
                   jax.ShapeDtypeStruct((t, ROUTE_LANES), F32),
                   jax.ShapeDtypeStruct((n * subs_per_block, 1, ROUTE_LANES), F32)],
        scratch_shapes=[pltpu.VMEM(w_in.shape, BF16)] + [pltpu.VMEM(w.shape, BF16) for w in merge_weights]
                       + [pltpu.VMEM((W_SLOTS, D_MODEL, W_SLAB), F32), pltpu.SemaphoreType.DMA((W_SLOTS,)),
                        pltpu.VMEM((1, SWA_Q), F32), pltpu.VMEM((1, SWA_KV), F32),
                        pltpu.VMEM((D_MODEL, ROUTE_LANES), BF16), pltpu.VMEM((1, ROUTE_LANES), F32),
                        pltpu.VMEM((BLOCK_ROWS, IN_WIDTH), BF16), pltpu.VMEM((BLOCK_ROWS, IN_WIDTH), BF16),
                        pltpu.VMEM((BLOCK_ROWS, RET_W), BF16), pltpu.VMEM((BLOCK_ROWS, SWA_Q), BF16),
                        pltpu.VMEM((RET_HEADS, RET_DK, RET_DV), F32),
                        pltpu.VMEM((chunks, RET_HEADS, RET_DK, RET_DV), BF16),
                        pltpu.VMEM((BLOCK_ROWS + WINDOW, Z_SKV_W), BF16), pltpu.VMEM((WINDOW, Z_SKV_W), BF16)],
        compiler_params=pltpu.CompilerParams(dimension_semantics=("arbitrary",),
                                             vmem_limit_bytes=VMEM_LIMIT),
        name="block",
    )(sinks, x2d, x2d, mix_g, w_in, *consts, *merge_weights, *small)


def _plan_tables(cnt_ref, base_ref, meta_ref):
    counts = cnt_ref[...]
    nt = counts.shape[0]
    exact = functools.partial(jnp.dot, precision=lax.Precision.HIGHEST, preferred_element_type=F32)
    total = jnp.sum(counts, axis=0, keepdims=True)
    tiles_e = jnp.ceil(total * (1.0 / ROW_TILE))
    k = lax.broadcasted_iota(jnp.int32, (ROUTE_LANES, ROUTE_LANES), 0)
    lane = lax.broadcasted_iota(jnp.int32, (ROUTE_LANES, ROUTE_LANES), 1)
    tile_end = exact(jnp.broadcast_to(tiles_e, (8, ROUTE_LANES)), jnp.where(k <= lane, 1.0, 0.0))[0:1]
    tile_start = tile_end - tiles_e
    r = lax.broadcasted_iota(jnp.int32, (nt, nt), 0)
    c = lax.broadcasted_iota(jnp.int32, (nt, nt), 1)
    before = exact(jnp.where(c < r, 1.0, 0.0), counts)
    base_ref[...] = tile_start * ROW_TILE + before

    used = jnp.max(tile_end, axis=-1, keepdims=True)
    tile = k.astype(F32)
    clamped = jnp.minimum(tile, jnp.maximum(used - 1.0, 0.0))
    is_expert = lane < N_EXPERTS
    owner = jnp.sum(jnp.where(jnp.logical_and(is_expert, tile_end <= clamped), 1.0, 0.0), axis=-1, keepdims=True)
    owner = jnp.minimum(owner, N_EXPERTS - 1.0)
    mine = lane.astype(F32) == owner
    pick = lambda v: jnp.sum(jnp.where(mine, v, 0.0), axis=-1, keepdims=True)
    t1 = tile[:, 0:1]
    valid = jnp.clip(pick(total) - (t1 - pick(tile_start)) * ROW_TILE, 0.0, float(ROW_TILE))
    valid = jnp.where(t1 < used, valid, 0.0)

    nonempty = jnp.where(jnp.logical_and(lane[0:1] < N_EXPERTS, tiles_e > 0.0), 1.0, 0.0)
    rank = exact(jnp.broadcast_to(nonempty, (8, ROUTE_LANES)), jnp.where(k < lane, 1.0, 0.0))[0:1]
    nonempty_col = jnp.sum(jnp.where(k == lane, jnp.broadcast_to(nonempty, k.shape), 0.0), axis=-1, keepdims=True)
    later = jnp.where(jnp.logical_and(k > lane, nonempty_col > 0.0), tile, float(ROUTE_LANES))
    nxt_e = jnp.min(later, axis=0, keepdims=True)
    nxt_e = jnp.where(nxt_e < float(ROUTE_LANES), nxt_e, -1.0)
    first = jnp.where(jnp.logical_and(t1 == pick(tile_start), t1 < used), 1.0, 0.0)
    my_rank = pick(rank)
    slot = my_rank - 2.0 * jnp.floor(my_rank * 0.5)
    fields = (owner, valid, first, pick(nxt_e), slot, jnp.broadcast_to(used, owner.shape))
    by_tile = sum(jnp.where(lane == n, f, 0.0) for n, f in enumerate(fields))
    meta_ref[...] = by_tile.T[:8, :].astype(I32)


def _plan_kernel(cnt_ref, route_ref, base_ref, meta_ref, pos1_ref, pos2_ref):
    step = pl.program_id(0)
    nt = cnt_ref.shape[0]
    subs_per_step = nt // PLAN_STEPS

    @pl.when(step == 0)
    def _():
        _plan_tables(cnt_ref, base_ref, meta_ref)
        pos1_ref[...] = jnp.zeros_like(pos1_ref)
        pos2_ref[...] = jnp.zeros_like(pos2_ref)

    sel_row = lax.broadcasted_iota(jnp.int32, (8, ROUTE_LANES), 0)
    sel_lane = lax.broadcasted_iota(jnp.int32, (8, ROUTE_LANES), 1)
    tok_r = lax.broadcasted_iota(jnp.int32, (MERGE_SUB, MERGE_SUB), 0)
    tok_c = lax.broadcasted_iota(jnp.int32, (MERGE_SUB, MERGE_SUB), 1)
    eye = jnp.where(tok_r == tok_c, 1.0, 0.0).astype(BF16)
    upto = jnp.where(tok_r <= tok_c, 1.0, 0.0).astype(BF16)
    sum_experts = lambda a: jnp.sum(a, axis=0, keepdims=True)

    def sub_positions(local, carry):
        s = step * subs_per_step + local
        route = route_ref[pl.ds(pl.multiple_of(local * MERGE_SUB, MERGE_SUB), MERGE_SUB), :].astype(BF16)
        record_t = _dot_tn(route, eye)
        incl_t = _dot_tn(route, upto)[:N_EXPERTS]
        both_t = record_t[:N_EXPERTS]
        second_t = record_t[LANE_SECOND0:LANE_SECOND0 + N_EXPERTS]
        rank2 = sum_experts(second_t * incl_t) - 1.0
        rank1 = sum_experts(both_t * incl_t) - rank2 - 2.0
        base_row = jnp.where(sel_lane < N_EXPERTS, jnp.broadcast_to(base_ref[pl.ds(s, 1), :], (8, ROUTE_LANES)), 0.0)
        hi = jnp.floor(base_row * (1.0 / 256.0))
        lo = base_row - 256.0 * hi
        hi2 = pltpu.roll(hi, LANE_SECOND0, axis=1)
        lo2 = pltpu.roll(lo, LANE_SECOND0, axis=1)
        table = (jnp.where(sel_row == 0, hi, 0.0) + jnp.where(sel_row == 1, lo, 0.0)
                 + jnp.where(sel_row == 2, hi2, 0.0) + jnp.where(sel_row == 3, lo2, 0.0))
        r = _dot_nt(table.astype(BF16), route)
        base_both = 256.0 * r[0:1] + r[1:2]
        base2 = 256.0 * r[2:3] + r[3:4]
        worker = s // subs_per_worker
        chunk0 = lax.rem(s, subs_per_worker) * chunks_per_sub
        for ref, row in ((pos1_ref, base_both - base2 + rank1), (pos2_ref, base2 + rank2)):
            for c in range(chunks_per_sub):
                ref[worker, pl.ds(chunk0 + c, 1), :] = row[:, c * SC_CHUNK:(c + 1) * SC_CHUNK].astype(I32)
        return carry

    chunks_per_sub = MERGE_SUB // SC_CHUNK
    subs_per_worker = nt // SC_WORKERS
    lax.fori_loop(0, subs_per_step, sub_positions, 0, unroll=4)


def _tile_plan(cnt, route, n_tiles):
    assert n_tiles <= ROUTE_LANES
    nt = cnt.shape[0]
    assert nt % SC_WORKERS == 0 and MERGE_SUB % SC_CHUNK == 0 and nt * MERGE_SUB <= SC_WORKERS * SC_IDX_ROWS * SC_CHUNK
    assert nt % PLAN_STEPS == 0
    pos = jax.ShapeDtypeStruct((SC_WORKERS, SC_IDX_ROWS, SC_CHUNK), I32)
    outs = [jax.ShapeDtypeStruct((nt, ROUTE_LANES), F32), jax.ShapeDtypeStruct((8, ROUTE_LANES), I32), pos, pos]
    whole = lambda a: pl.BlockSpec(a.shape, lambda i: (0,) * len(a.shape))
    _, meta, pos1, pos2 = pl.pallas_call(
        _plan_kernel,
        grid=(PLAN_STEPS,),
        in_specs=[pl.BlockSpec((nt, ROUTE_LANES), lambda i: (0, 0)),
                  pl.BlockSpec((route.shape[0] // PLAN_STEPS, ROUTE_LANES), lambda i: (i, 0))],
        out_specs=[whole(o) for o in outs],
        out_shape=outs,
        compiler_params=pltpu.CompilerParams(dimension_semantics=("arbitrary",), vmem_limit_bytes=VMEM_LIMIT),
        name="tile_plan",
    )(cnt.reshape(nt, ROUTE_LANES), route)
    return meta, pos1, pos2


def _sc_mesh():
    return plsc.VectorSubcoreMesh(core_axis_name="c", subcore_axis_name="s")


def _sc_worker():
    return lax.axis_index("s") * SC_CORES + lax.axis_index("c")


def _dispatch(hp, pos1, pos2, n_rows):
    t = hp.shape[0]
    per_worker = t // SC_WORKERS
    k = per_worker // SC_CHUNK
    idx = pltpu.VMEM((SC_IDX_ROWS, SC_CHUNK), I32)

    @functools.partial(pl.kernel, mesh=_sc_mesh(), out_type=jax.ShapeDtypeStruct((n_rows, HALF_D), I32),
                       scratch_types=[idx, idx, pltpu.VMEM((SC_CHUNK, HALF_D), I32)], name="moe_dispatch")
    def run(h_hbm, p1_hbm, p2_hbm, xs_hbm, p1_v, p2_v, rows_v):
        wid = _sc_worker()
        pltpu.sync_copy(p1_hbm.at[wid], p1_v)
        pltpu.sync_copy(p2_hbm.at[wid], p2_v)
        for j in range(k):
            pltpu.sync_copy(h_hbm.at[pl.ds(wid * per_worker + j * SC_CHUNK, SC_CHUNK)], rows_v)
            pltpu.sync_copy(rows_v, xs_hbm.at[p1_v.at[j]])
            pltpu.sync_copy(rows_v, xs_hbm.at[p2_v.at[j]])

    return run(hp, pos1, pos2)


def _combine_gather(ys, pos1, pos2, t):
    per_worker = t // SC_WORKERS
    k = per_worker // SC_CHUNK
    idx = pltpu.VMEM((SC_IDX_ROWS, SC_CHUNK), I32)
    out = jax.ShapeDtypeStruct((t, HALF_D), I32)

    @functools.partial(pl.kernel, mesh=_sc_mesh(), out_type=(out, out),
                       scratch_types=[idx, idx, pltpu.VMEM((SC_CHUNK, HALF_D), I32)], name="moe_combine")
    def run(ys_hbm, p1_hbm, p2_hbm, g1_hbm, g2_hbm, p1_v, p2_v, rows_v):
        wid = _sc_worker()
        pltpu.sync_copy(p1_hbm.at[wid], p1_v)
        pltpu.sync_copy(p2_hbm.at[wid], p2_v)
        for j in range(k):
            dst = pl.ds(wid * per_worker + j * SC_CHUNK, SC_CHUNK)
            pltpu.sync_copy(ys_hbm.at[p1_v.at[j]], rows_v)
            pltpu.sync_copy(rows_v, g1_hbm.at[dst])
            pltpu.sync_copy(ys_hbm.at[p2_v.at[j]], rows_v)
            pltpu.sync_copy(rows_v, g2_hbm.at[dst])

    return run(ys, pos1, pos2)


META_OWNER, META_VALID, META_FIRST, META_NEXT, META_SLOT, META_USED = range(6)


def _experts_kernel(meta_ref, xs_hbm, wg_hbm, wu_hbm, wd_hbm, ys_ref,
                    xs_buf, xs_sem, wg_f, wu_f, wd_f, wg_bf, wu_bf, wd_bf, w_sem):
    i = pl.program_id(0)
    used = meta_ref[META_USED, 0]
    valid = meta_ref[META_VALID, i]
    owner = meta_ref[META_OWNER, i]
    nxt = meta_ref[META_NEXT, i]

    def rows_in(tile):
        start = pl.multiple_of(tile * ROW_TILE, ROW_TILE)
        slot = lax.rem(tile, XS_SLOTS)
        return pltpu.make_async_copy(xs_hbm.at[pl.ds(start, ROW_TILE)], xs_buf.at[slot], xs_sem.at[slot])

    @pl.when(i == 0)
    def _():
        for ahead in range(XS_SLOTS - 1):
            pl.when(ahead < used)(rows_in(ahead).start)

    @pl.when(i + (XS_SLOTS - 1) < used)
    def _():
        rows_in(i + (XS_SLOTS - 1)).start()

    def fetch(expert, slot):
        pairs = ((wg_hbm, wg_f), (wu_hbm, wu_f), (wd_hbm, wd_f))
        return [pltpu.make_async_copy(w.at[expert], buf.at[slot], w_sem.at[slot, n])
                for n, (w, buf) in enumerate(pairs)]

    @pl.when(i == 0)
    def _():
        for copy in fetch(owner, 0):
            copy.start()

    @pl.when(meta_ref[META_FIRST, i] == 1)
    def _():
        slot = meta_ref[META_SLOT, i]
        for copy in fetch(owner, slot):
            copy.wait()
        wg_bf[...] = wg_f[slot].astype(BF16)
        wu_bf[...] = wu_f[slot].astype(BF16)
        wd_bf[...] = wd_f[slot].astype(BF16)

        @pl.when(nxt >= 0)
        def _():
            for copy in fetch(nxt, 1 - slot):
                copy.start()

    @pl.when(i < used)
    def _():
        rows_in(i).wait()
        row = lax.broadcasted_iota(jnp.int32, (ROW_TILE, HALF_D), 0)
        words = xs_buf[lax.rem(i, XS_SLOTS)]
        packed = jnp.where(row < valid, words, 0)
        hi, lo = _unpack_halves(packed)
        hi = hi.astype(BF16)
        lo = lo.astype(BF16)
        hg = _dot(hi, wg_bf[:HALF_D, :]) + _dot(lo, wg_bf[HALF_D:, :])
        hu = _dot(hi, wu_bf[:HALF_D, :]) + _dot(lo, wu_bf[HALF_D:, :])
        hid = (hg * _sigmoid(hg) * hu).astype(BF16)
        ys_ref[...] = _pack_halves(_dot(hid, wd_bf[...]))


def _experts(xs, meta, n_tiles, wg, wu, wd):
    tile = lambda i, meta: (jnp.minimum(i, meta[META_USED, 0] - 1), 0)
    hbm = pl.BlockSpec(memory_space=pl.ANY)
    up_shape, down_shape = (D_MODEL, EXPERT_FF), (EXPERT_FF, D_MODEL)
    grid_spec = pltpu.PrefetchScalarGridSpec(
        num_scalar_prefetch=1,
        grid=(n_tiles,),
        in_specs=[hbm, hbm, hbm, hbm],
        out_specs=pl.BlockSpec((ROW_TILE, HALF_D), tile),
        scratch_shapes=[pltpu.VMEM((XS_SLOTS, ROW_TILE, HALF_D), I32), pltpu.SemaphoreType.DMA((XS_SLOTS,)),
                        pltpu.VMEM((2,) + up_shape, F32), pltpu.VMEM((2,) + up_shape, F32),
                        pltpu.VMEM((2,) + down_shape, F32),
                        pltpu.VMEM(up_shape, BF16), pltpu.VMEM(up_shape, BF16), pltpu.VMEM(down_shape, BF16),
                        pltpu.SemaphoreType.DMA((2, 3))],
    )
    return pl.pallas_call(
        _experts_kernel,
        grid_spec=grid_spec,
        out_shape=jax.ShapeDtypeStruct((n_tiles * ROW_TILE, HALF_D), I32),
        compiler_params=pltpu.CompilerParams(dimension_semantics=("arbitrary",),
                                             vmem_limit_bytes=VMEM_LIMIT),
        name="experts",
    )(meta, xs, wg, wu, wd)


def _ple_kernel(x1_hbm, g1_hbm, g2_hbm, route_ref, p_ref, wp_f32, pg_ref, gg_ref, wgate_f32, o_ref,
                wp_ref, wgate_ref, x1_buf, g1_buf, g2_buf, ring_sem):
    i = pl.program_id(0)
    steps = pl.num_programs(0)
    streams = ((x1_hbm, x1_buf), (g1_hbm, g1_buf), (g2_hbm, g2_buf))

    def rows_in(step):
        start = pl.multiple_of(step * PLE_TM, PLE_TM)
        slot = lax.rem(step, PLE_SLOTS)
        return [pltpu.make_async_copy(src.at[pl.ds(start, PLE_TM)], buf.at[slot], ring_sem.at[n, slot])
                for n, (src, buf) in enumerate(streams)]

    def start_rows(step):
        for copy in rows_in(step):
            copy.start()

    @pl.when(i == 0)
    def _():
        for ahead in range(PLE_SLOTS - 1):
            pl.when(ahead < steps)(functools.partial(start_rows, ahead))
        wp_ref[...] = wp_f32[...].astype(BF16)
        wgate_ref[...] = wgate_f32[...].astype(BF16)

    pl.when(i + (PLE_SLOTS - 1) < steps)(functools.partial(start_rows, i + (PLE_SLOTS - 1)))

    for copy in rows_in(i):
        copy.wait()
    slot = lax.rem(i, PLE_SLOTS)
    x1_ref, g1_ref, g2_ref = x1_buf.at[slot], g1_buf.at[slot], g2_buf.at[slot]

    route = route_ref[...]
    lane = lax.broadcasted_iota(jnp.int32, route.shape, 1)
    pick = lambda ln: jnp.sum(jnp.where(lane == ln, route, 0.0), axis=-1, keepdims=True)
    c1 = pick(LANE_C1)
    c2 = pick(LANE_C2)
    a_hi, a_lo = _unpack_halves(g1_ref[...])
    b_hi, b_lo = _unpack_halves(g2_ref[...])
    x = x1_ref[...] + jnp.concatenate([c1 * a_hi + c2 * b_hi, c1 * a_lo + c2 * b_lo], axis=1)
    ple = _rms(_dot(p_ref[...].astype(BF16), wp_ref[...]), pg_ref[...])
    gate = _sigmoid(_dot(_rms(x, gg_ref[...]).astype(BF16), wgate_ref[...]))
    o_ref[...] = x + gate * ple


def _ple(x1, g1, g2, route, p2d, w_ple, ple_g, gate_g, w_gate):
    t = x1.shape[0]
    row = lambda w: pl.BlockSpec((PLE_TM, w), lambda i: (i, 0))
    full = lambda a: pl.BlockSpec(a.shape, lambda i: (0,) * a.ndim)
    hbm = pl.BlockSpec(memory_space=pl.ANY)
    return pl.pallas_call(
        _ple_kernel,
        grid=(t // PLE_TM,),
        in_specs=[hbm, hbm, hbm, row(ROUTE_LANES), row(PLE_DIM),
                  full(w_ple), full(ple_g), full(gate_g), full(w_gate)],
        out_specs=row(D_MODEL),
        out_shape=jax.ShapeDtypeStruct((t, D_MODEL), F32),
        scratch_shapes=[pltpu.VMEM(w_ple.shape, BF16), pltpu.VMEM(w_gate.shape, BF16),
                        pltpu.VMEM((PLE_SLOTS, PLE_TM, D_MODEL), F32),
                        pltpu.VMEM((PLE_SLOTS, PLE_TM, HALF_D), I32), pltpu.VMEM((PLE_SLOTS, PLE_TM, HALF_D), I32),
                        pltpu.SemaphoreType.DMA((3, PLE_SLOTS))],
        compiler_params=pltpu.CompilerParams(dimension_semantics=("arbitrary",),
                                             vmem_limit_bytes=VMEM_LIMIT),
        name="ple",
    )(x1, g1, g2, route, p2d, w_ple, ple_g, gate_g, w_gate)


def kernel(x, p, mix_norm_g, w_in, ret_gn_g, ret_gn_b, w_ret_o, q_norm_g, k_norm_g, attn_sinks,
           w_swa_o, w_out, ffn_norm_g, w_router_group, b_router_group, w_router_expert,
           b_router_expert, w_exp_gate, w_exp_up, w_exp_down, ple_gate_norm_g, w_ple_gate,
           w_ple, ple_norm_g):
    batch, seq, d = x.shape
    t = batch * seq
    depth = w_in.shape[0]
    n_tiles = 2 * t // ROW_TILE + N_EXPERTS
    row = lambda a: a.reshape(1, -1)
    x2d = x.reshape(t, d)
    for i in range(depth):
        router = (w_router_expert[i], w_router_group[i], row(b_router_expert[i]), row(b_router_group[i]))
        x1, hp, route, cnt = _block(x2d, row(mix_norm_g[i]), w_in[i], row(ret_gn_g[i]),
                                    row(ret_gn_b[i]), attn_sinks[i], row(q_norm_g[i]), row(k_norm_g[i]),
                                    w_ret_o[i], w_swa_o[i], w_out[i], row(ffn_norm_g[i]), router, seq)
        meta, pos1, pos2 = _tile_plan(cnt, route, n_tiles)
        xs = _dispatch(hp, pos1, pos2, n_tiles * ROW_TILE)
        ys = _experts(xs, meta, n_tiles, w_exp_gate[i], w_exp_up[i], w_exp_down[i])
        g1, g2 = _combine_gather(ys, pos1, pos2, t)
        x2d = _ple(x1, g1, g2, route, p[i].reshape(t, PLE_DIM), w_ple[i], row(ple_norm_g[i]),
                   row(ple_gate_norm_g[i]), w_ple_gate[i])
    return x2d.reshape(batch, seq, d)
```

```python
import functools

import jax
import jax.numpy as jnp
import numpy as np
from jax import lax
from jax.experimental import pallas as pl
from jax.experimental.pallas import tpu as pltpu
from jax.experimental.pallas import tpu_sc as plsc

F32 = jnp.float32
BF16 = jnp.bfloat16
I32 = jnp.int32

EPS = 1e-6
D_MODEL = 1024
PLE_DIM = 256
RET_HEADS = 4
RET_DK = 128
RET_DV = 128
RET_CHUNK = 128
SWA_HEADS = 8
SWA_KV_HEADS = 2
SWA_GROUP = SWA_HEADS // SWA_KV_HEADS
SWA_HD = 64
WINDOW = 128
N_GROUPS = 4
EXPERTS_PER_GROUP = 8
N_EXPERTS = N_GROUPS * EXPERTS_PER_GROUP
EXPERT_FF = 256

RET_W = RET_HEADS * RET_DK
SWA_Q = SWA_HEADS * SWA_HD
SWA_KV = SWA_KV_HEADS * SWA_HD
Z_RET_W = 4 * RET_W
Z_SKV_W = 2 * SWA_KV
Z_GATE_W = 2 * D_MODEL
COL_SQ = Z_RET_W
COL_SKV = COL_SQ + SWA_Q
COL_GATE = COL_SKV + Z_SKV_W
IN_WIDTH = COL_GATE + Z_GATE_W

ROUTE_LANES = 128
LANE_GROUP0 = N_EXPERTS
LANE_C1, LANE_C2 = 34, 35
LANE_SECOND0 = 64
NEG_INF = -1e30

HALF_D = D_MODEL // 2
ROW_TILE = 512
XS_SLOTS = 3
SC_CORES = 2
SC_SUBCORES = 16
SC_WORKERS = SC_CORES * SC_SUBCORES
SC_CHUNK = 128
SC_IDX_ROWS = 8

VMEM_LIMIT = 56 * 1024 * 1024

BLOCK_ROWS = 512
W_SLAB = 256
W_SLOTS = 4
RET_ROWS = 512
MERGE_SUB = 256
PLAN_STEPS = 4
PLE_TM = 1024
PLE_SLOTS = 3


def _rms(x, g):
    ms = jnp.mean(x * x, axis=-1, keepdims=True)
    return x * lax.rsqrt(ms + EPS) * g


def _sigmoid(x):
    return 1.0 / (1.0 + jnp.exp(-x))


def _dot(a, b):
    return jnp.dot(a, b, preferred_element_type=F32)


def _dot_nt(a, b):
    return lax.dot_general(a, b, (((1,), (1,)), ((), ())), preferred_element_type=F32)


def _dot_tn(a, b):
    return lax.dot_general(a, b, (((0,), (0,)), ((), ())), preferred_element_type=F32)


def _pack_halves(v):
    return pltpu.pack_elementwise([v[:, :HALF_D], v[:, HALF_D:]], packed_dtype=BF16)


def _unpack_halves(p):
    words = lax.bitcast_convert_type(p, jnp.uint32)
    return tuple(pltpu.unpack_elementwise(words, index=k, packed_dtype=BF16, unpacked_dtype=F32) for k in range(2))


def _retention_tables():
    f32 = np.float32
    h = RET_HEADS
    c = RET_CHUNK
    log_gamma = np.log1p(-np.exp2(f32(-5.0) - np.arange(h, dtype=f32))).astype(f32)
    pos = np.arange(c, dtype=f32)
    diff = pos[:, None] - pos[None, :]
    decay = np.where(diff[None] >= 0.0, np.exp(np.maximum(diff, f32(0.0))[None] * log_gamma[:, None, None]), f32(0.0))
    scale = f32(RET_DK ** -0.5)
    dmask = decay * scale
    zeta = np.exp((f32(c - 1.0) - pos)[None, :] * log_gamma[:, None]) * scale
    xi = np.exp((pos + f32(1.0))[None, :] * log_gamma[:, None])
    cdec = np.exp(f32(c) * log_gamma)
    bc = lambda a: np.ascontiguousarray(np.broadcast_to(a[:, :, None], (h, c, c)), dtype=f32)
    return (dmask.astype(f32), bc(zeta), bc(xi),
            np.ascontiguousarray(np.broadcast_to(cdec[:, None, None], (h, c, c)), dtype=f32))


def _swa_tables():
    f32 = np.float32
    qi = np.arange(WINDOW)[:, None]
    sj = np.arange(2 * WINDOW)[None, :]
    rel = qi + WINDOW - sj
    ok = (rel >= 0) & (rel < WINDOW)
    slopes = np.exp2(f32(-8.0) * np.arange(1, SWA_HEADS + 1, dtype=f32) / f32(SWA_HEADS)).astype(f32)
    bias = np.where(ok[None], -slopes[:, None, None] * rel.astype(f32)[None], f32(NEG_INF)).astype(f32)
    bias = bias.reshape(SWA_KV_HEADS, SWA_GROUP * WINDOW, 2 * WINDOW)
    avg = lambda heads: jnp.asarray(np.kron(np.eye(heads, dtype=f32), np.full((SWA_HD, SWA_HD), 1.0 / SWA_HD, f32)),
                                    dtype=BF16)
    return avg(SWA_HEADS), avg(SWA_KV_HEADS), jnp.asarray(bias)


def _block_kernel(per_seq, sinks_ref, x_ref, xp_ref, g_ref, w_hbm, dmask_ref, zeta_ref, xi_ref, cdec_ref,
                  gng_ref, gnb_ref, qgain_ref, kgain_ref, bq_ref, bk_ref, bias_ref,
                  wro_hbm, wso_hbm, wout_hbm, fg_ref, wre_ref, wrg_ref, bre_ref, brg_ref,
                  x1_ref, hp_ref, route_ref, cnt_ref,
                  w_ref, wro_ref, wso_ref, wout_ref, wtmp_ref, w_sem, qg_ref, kg_ref, wr_ref, br_ref,
                  zc_ref, zn_ref, yr_ref, ys_ref, state_ref, prev_ref, band_ref, tail_ref):
    g = pl.program_id(0)
    seq_start = lax.rem(g - 1, per_seq) == 0

    @pl.when(g == 0)
    def _():
        slabs = [(src, dst, c0) for src, dst in ((w_hbm, w_ref), (wro_hbm, wro_ref), (wso_hbm, wso_ref),
                                                 (wout_hbm, wout_ref))
                 for c0 in range(0, dst.shape[1], W_SLAB)]

        def slab(k):
            src, dst, c0 = slabs[k]
            return pltpu.make_async_copy(src.at[:, pl.ds(c0, W_SLAB)],
                                         wtmp_ref.at[k % W_SLOTS, pl.ds(0, dst.shape[0])], w_sem.at[k % W_SLOTS])

        for k in range(W_SLOTS - 1):
            slab(k).start()
        for k, (_, dst, c0) in enumerate(slabs):
            if k + W_SLOTS - 1 < len(slabs):
                slab(k + W_SLOTS - 1).start()
            slab(k).wait()
            dst[:, c0:c0 + W_SLAB] = wtmp_ref[k % W_SLOTS, :dst.shape[0]].astype(BF16)
        for hd in range(SWA_HEADS):
            qg_ref[:, hd * SWA_HD:(hd + 1) * SWA_HD] = qgain_ref[...] * (SWA_HD ** -0.5)
        for kh in range(SWA_KV_HEADS):
            kg_ref[:, kh * SWA_HD:(kh + 1) * SWA_HD] = kgain_ref[...]
        wr_ref[...] = jnp.zeros_like(wr_ref)
        wr_ref[:, :N_EXPERTS] = wre_ref[...].astype(BF16)
        wr_ref[:, LANE_GROUP0:LANE_GROUP0 + N_GROUPS] = wrg_ref[...].astype(BF16)
        br_ref[...] = jnp.zeros_like(br_ref)
        br_ref[:, :N_EXPERTS] = bre_ref[...]
        br_ref[:, LANE_GROUP0:LANE_GROUP0 + N_GROUPS] = brg_ref[...]
        zc_ref[...] = jnp.zeros_like(zc_ref)
        tail_ref[...] = jnp.zeros_like(tail_ref)
        state_ref[...] = jnp.zeros_like(state_ref)

    @pl.when(seq_start)
    def _():
        state_ref[...] = jnp.zeros_like(state_ref)

    band_ref[:WINDOW, :] = tail_ref[...]
    band_ref[WINDOW:, :] = zc_ref[:, COL_SKV:COL_GATE]

    h = _rms(x_ref[...], g_ref[...]).astype(BF16)

    def project(c0, width):
        zn_ref[:, c0:c0 + width] = _dot(h, w_ref[:, c0:c0 + width]).astype(BF16)

    cw = 512
    proj_pieces = [functools.partial(project, j * cw, cw) for j in range(COL_SKV // cw)]
    proj_pieces.append(functools.partial(project, COL_SKV, Z_SKV_W))
    proj_pieces += [functools.partial(project, COL_GATE + j * cw, cw) for j in range(Z_GATE_W // cw)]

    def retention_rows(t):
        r_base = t * RET_ROWS
        n_chunks = RET_ROWS // RET_CHUNK
        part = lambda c, p, hd: zc_ref[r_base + c * RET_CHUNK:r_base + (c + 1) * RET_CHUNK,
                                       p * RET_W + hd * RET_DK:p * RET_W + (hd + 1) * RET_DK]
        for hd in range(RET_HEADS):
            state = state_ref[hd]
            for c in range(n_chunks):
                prev_ref[t * n_chunks + c, hd] = state.astype(BF16)
                kz = (part(c, 1, hd).astype(F32) * zeta_ref[hd]).astype(BF16)
                state = cdec_ref[hd] * state + _dot_tn(kz, part(c, 2, hd))
            state_ref[hd] = state
        out_rows = []
        for c in range(n_chunks):
            heads = []
            for hd in range(RET_HEADS):
                q = part(c, 0, hd)
                scores = _dot_nt(q, part(c, 1, hd)) * dmask_ref[hd]
                y = _dot(scores.astype(BF16), part(c, 2, hd)) + _dot(q, prev_ref[t * n_chunks + c, hd]) * xi_ref[hd]
                mu = jnp.mean(y, axis=-1, keepdims=True)
                d = y - mu
                var = jnp.mean(d * d, axis=-1, keepdims=True)
                hs = slice(hd * RET_DV, (hd + 1) * RET_DV)
                yn = d * lax.rsqrt(var + EPS) * gng_ref[:, hs] + gnb_ref[:, hs]
                gate = part(c, 3, hd).astype(F32)
                heads.append((gate * _sigmoid(gate) * yn).astype(BF16))
            out_rows.append(jnp.concatenate(heads, axis=1))
        yr_ref[r_base:r_base + RET_ROWS, :] = jnp.concatenate(out_rows, axis=0)

    group_rows = SWA_GROUP * WINDOW
    lo_q = lax.broadcasted_iota(jnp.int32, (WINDOW, 2 * SWA_HD), 1) < SWA_HD
    lo_k = lax.broadcasted_iota(jnp.int32, (2 * WINDOW, 2 * SWA_HD), 1) < SWA_HD
    in_prev = lax.broadcasted_iota(jnp.int32, (group_rows, 2 * WINDOW), 1) < WINDOW
    row_head = lax.broadcasted_iota(jnp.int32, (group_rows, 1), 0) // WINDOW

    def both_heads(v):
        swapped = pltpu.roll(v, SWA_HD, axis=1)
        return jnp.where(lo_k, v, swapped), jnp.where(lo_k, swapped, v)

    def swa_block(n):
        rows = slice(n * WINDOW, (n + 1) * WINDOW)
        kv = band_ref[n * WINDOW:(n + 2) * WINDOW, :]
        kf = kv[:, :SWA_KV].astype(F32)
        kn = kf * lax.rsqrt(_dot((kf * kf).astype(BF16), bk_ref[...]) + EPS) * kg_ref[...]
        keys = [a.astype(BF16) for a in both_heads(kn)]
        vals = [a.astype(BF16) for a in both_heads(kv[:, SWA_KV:].astype(F32))]
        qf = zc_ref[rows, COL_SQ:COL_SKV].astype(F32)
        qn = qf * lax.rsqrt(_dot((qf * qf).astype(BF16), bq_ref[...]) + EPS) * qg_ref[...]
        outs = []
        for kh in range(SWA_KV_HEADS):
            parts = []
            for j in range(SWA_GROUP // 2):
                c0 = (kh * SWA_GROUP + 2 * j) * SWA_HD
                two = qn[:, c0:c0 + 2 * SWA_HD]
                parts += [jnp.where(lo_q, two, 0.0), jnp.where(lo_q, 0.0, two)]
            qs = jnp.concatenate(parts, axis=0).astype(BF16)
            s = _dot_nt(qs, keys[kh]) + bias_ref[kh]
            if n == 0:
                s = jnp.where(jnp.logical_and(seq_start, in_prev), NEG_INF, s)
            sink = sinks_ref[kh * SWA_GROUP]
            for gi in range(1, SWA_GROUP):
                sink = jnp.where(row_head == gi, sinks_ref[kh * SWA_GROUP + gi], sink)
            m = jnp.maximum(jnp.max(s, axis=-1, keepdims=True), sink)
            p = jnp.exp(s - m)
            denom = jnp.sum(p, axis=-1, keepdims=True) + jnp.exp(sink - m)
            o = _dot(p.astype(BF16), vals[kh]) * (1.0 / denom)
            for j in range(SWA_GROUP // 2):
                even = o[(2 * j) * WINDOW:(2 * j + 1) * WINDOW]
                odd = o[(2 * j + 1) * WINDOW:(2 * j + 2) * WINDOW]
                outs.append(jnp.where(lo_q, even, odd).astype(BF16))
        ys_ref[rows, :] = jnp.concatenate(outs, axis=1)

    mix_pieces = [functools.partial(retention_rows, t) for t in range(BLOCK_ROWS // RET_ROWS)]
    mix_pieces += [functools.partial(swa_block, n) for n in range(BLOCK_ROWS // WINDOW)]

    subs = []

    def merge_sub(s):
        subs.append(_merge_rows(slice(s * MERGE_SUB, (s + 1) * MERGE_SUB), yr_ref, ys_ref, zc_ref, xp_ref, wro_ref,
                                wso_ref, wout_ref, fg_ref, wr_ref, br_ref))

    other_pieces = mix_pieces + [functools.partial(merge_sub, s) for s in range(BLOCK_ROWS // MERGE_SUB)]

    n_other = len(other_pieces)
    for k, piece in enumerate(other_pieces):
        for proj_piece in proj_pieces[k * len(proj_pieces) // n_other:(k + 1) * len(proj_pieces) // n_other]:
            proj_piece()
        piece()
    x1_ref[...] = jnp.concatenate([s[0] for s in subs], axis=0)
    hp_ref[...] = jnp.concatenate([s[1] for s in subs], axis=0)
    route_ref[...] = jnp.concatenate([s[2] for s in subs], axis=0)
    for s, sub in enumerate(subs):
        cnt_ref[s] = sub[3]

    tail_ref[...] = zc_ref[BLOCK_ROWS - WINDOW:, COL_SKV:COL_GATE]
    zc_ref[...] = zn_ref[...]


def _merge_rows(rows, yr_ref, ys_ref, z_ref, x_ref, wro_ref, wso_ref, wout_ref, fg_ref, wr_ref, br_ref):
    a = _dot(yr_ref[rows, :], wro_ref[...])
    b = _dot(ys_ref[rows, :], wso_ref[...])
    gate_r = z_ref[rows, COL_GATE:COL_GATE + D_MODEL].astype(F32)
    gate_s = z_ref[rows, COL_GATE + D_MODEL:].astype(F32)
    merged = (_sigmoid(gate_r) * a + _sigmoid(gate_s) * b).astype(BF16)
    x1 = x_ref[rows, :] + _dot(merged, wout_ref[...])
    h2 = _rms(x1, fg_ref[...])
    packed = _pack_halves(h2)

    logits = _dot(h2.astype(BF16), wr_ref[...]) + br_ref[...]
    lane = lax.broadcasted_iota(jnp.int32, logits.shape, 1)
    big = jnp.int32(ROUTE_LANES)
    is_group = jnp.logical_and(lane >= LANE_GROUP0, lane < LANE_GROUP0 + N_GROUPS)
    gl = jnp.where(is_group, logits, NEG_INF)
    gmax = jnp.max(gl, axis=-1, keepdims=True)
    g_w = 1.0 / jnp.sum(jnp.exp(gl - gmax), axis=-1, keepdims=True)
    g_sel = jnp.min(jnp.where(gl == gmax, lane, big), axis=-1, keepdims=True) - LANE_GROUP0
    in_group = jnp.logical_and(lane < N_EXPERTS, (lane >> 3) == g_sel)
    el = jnp.where(in_group, logits, NEG_INF)
    m1 = jnp.max(el, axis=-1, keepdims=True)
    i1 = jnp.min(jnp.where(el == m1, lane, big), axis=-1, keepdims=True)
    el2 = jnp.where(lane == i1, NEG_INF, el)
    m2 = jnp.max(el2, axis=-1, keepdims=True)
    i2 = jnp.min(jnp.where(el2 == m2, lane, big), axis=-1, keepdims=True)
    e2 = jnp.exp(m2 - m1)
    c1 = g_w / (1.0 + e2)
    c2 = g_w * e2 / (1.0 + e2)
    chosen = jnp.where(jnp.logical_or(lane == i1, lane == i2), 1.0, 0.0)
    route = (chosen
             + jnp.where(lane == LANE_C1, c1, 0.0)
             + jnp.where(lane == LANE_C2, c2, 0.0)
             + jnp.where(lane == i2 + LANE_SECOND0, 1.0, 0.0))
    return x1, packed, route, jnp.sum(chosen, axis=0, keepdims=True)


def _block(x2d, mix_g, w_in, gn_g, gn_b, sinks, q_g, k_g, w_ret_o, w_swa_o, w_out, ffn_g, router, seq):
    t = x2d.shape[0]
    n = t // BLOCK_ROWS
    assert seq % BLOCK_ROWS == 0
    consts = _retention_tables() + (gn_g, gn_b, q_g, k_g) + _swa_tables()
    merge_weights = (w_ret_o, w_swa_o, w_out)
    small = (ffn_g,) + tuple(router)
    hbm = pl.BlockSpec(memory_space=pl.ANY)
    full = lambda a: pl.BlockSpec(a.shape, lambda g: (0,) * a.ndim)
    cur = lambda w: pl.BlockSpec((BLOCK_ROWS, w), lambda g: (jnp.minimum(g, n - 1), 0))
    prv = lambda w: pl.BlockSpec((BLOCK_ROWS, w), lambda g: (jnp.maximum(g - 1, 0), 0))
    chunks = BLOCK_ROWS // RET_CHUNK
    subs_per_block = BLOCK_ROWS // MERGE_SUB
    return pl.pallas_call(
        functools.partial(_block_kernel, seq // BLOCK_ROWS),
        grid=(n + 1,),
        in_specs=([pl.BlockSpec(memory_space=pltpu.SMEM), cur(D_MODEL), prv(D_MODEL), full(mix_g), hbm]
                  + [full(a) for a in consts] + [hbm] * len(merge_weights) + [full(a) for a in small]),
        out_specs=[prv(D_MODEL), prv(HALF_D), prv(ROUTE_LANES),
                   pl.BlockSpec((subs_per_block, 1, ROUTE_LANES), lambda g: (jnp.maximum(g - 1, 0), 0, 0))],
        out_shape=[jax.ShapeDtypeStruct((t, D_MODEL), F32),
                   jax.ShapeDtypeStruct((t, HALF_D), I32),
                   jax.ShapeDtypeStruct((t, ROUTE_LANES), F32),
                   jax.ShapeDtypeStruct((n * subs_per_block, 1, ROUTE_LANES), F32)],
        scratch_shapes=[pltpu.VMEM(w_in.shape, BF16)] + [pltpu.VMEM(w.shape, BF16) for w in merge_weights]
                       + [pltpu.VMEM((W_SLOTS, D_MODEL, W_SLAB), F32), pltpu.SemaphoreType.DMA((W_SLOTS,)),
                        pltpu.VMEM((1, SWA_Q), F32), pltpu.VMEM((1, SWA_KV), F32),
                        pltpu.VMEM((D_MODEL, ROUTE_LANES), BF16), pltpu.VMEM((1, ROUTE_LANES), F32),
                        pltpu.VMEM((BLOCK_ROWS, IN_WIDTH), BF16), pltpu.VMEM((BLOCK_ROWS, IN_WIDTH), BF16),
                        pltpu.VMEM((BLOCK_ROWS, RET_W), BF16), pltpu.VMEM((BLOCK_ROWS, SWA_Q), BF16),
                        pltpu.VMEM((RET_HEADS, RET_DK, RET_DV), F32),
                        pltpu.VMEM((chunks, RET_HEADS, RET_DK, RET_DV), BF16),
                        pltpu.VMEM((BLOCK_ROWS + WINDOW, Z_SKV_W), BF16), pltpu.VMEM((WINDOW, Z_SKV_W), BF16)],
        compiler_params=pltpu.CompilerParams(dimension_semantics=("arbitrary",),
                                             vmem_limit_bytes=VMEM_LIMIT),
        name="block",
    )(sinks, x2d, x2d, mix_g, w_in, *consts, *merge_weights, *small)


def _plan_tables(cnt_ref, base_ref, meta_ref):
    counts = cnt_ref[...]
    nt = counts.shape[0]
    exact = functools.partial(jnp.dot, precision=lax.Precision.HIGHEST, preferred_element_type=F32)
    total = jnp.sum(counts, axis=0, keepdims=True)
    tiles_e = jnp.ceil(total * (1.0 / ROW_TILE))
    k = lax.broadcasted_iota(jnp.int32, (ROUTE_LANES, ROUTE_LANES), 0)
    lane = lax.broadcasted_iota(jnp.int32, (ROUTE_LANES, ROUTE_LANES), 1)
    tile_end = exact(jnp.broadcast_to(tiles_e, (8, ROUTE_LANES)), jnp.where(k <= lane, 1.0, 0.0))[0:1]
    tile_start = tile_end - tiles_e
    r = lax.broadcasted_iota(jnp.int32, (nt, nt), 0)
    c = lax.broadcasted_iota(jnp.int32, (nt, nt), 1)
    before = exact(jnp.where(c < r, 1.0, 0.0), counts)
    base_ref[...] = tile_start * ROW_TILE + before

    used = jnp.max(tile_end, axis=-1, keepdims=True)
    tile = k.astype(F32)
    clamped = jnp.minimum(tile, jnp.maximum(used - 1.0, 0.0))
    is_expert = lane < N_EXPERTS
    owner = jnp.sum(jnp.where(jnp.logical_and(is_expert, tile_end <= clamped), 1.0, 0.0), axis=-1, keepdims=True)
    owner = jnp.minimum(owner, N_EXPERTS - 1.0)
    mine = lane.astype(F32) == owner
    pick = lambda v: jnp.sum(jnp.where(mine, v, 0.0), axis=-1, keepdims=True)
    t1 = tile[:, 0:1]
    valid = jnp.clip(pick(total) - (t1 - pick(tile_start)) * ROW_TILE, 0.0, float(ROW_TILE))
    valid = jnp.where(t1 < used, valid, 0.0)

    nonempty = jnp.where(jnp.logical_and(lane[0:1] < N_EXPERTS, tiles_e > 0.0), 1.0, 0.0)
    rank = exact(jnp.broadcast_to(nonempty, (8, ROUTE_LANES)), jnp.where(k < lane, 1.0, 0.0))[0:1]
    nonempty_col = jnp.sum(jnp.where(k == lane, jnp.broadcast_to(nonempty, k.shape), 0.0), axis=-1, keepdims=True)
    later = jnp.where(jnp.logical_and(k > lane, nonempty_col > 0.0), tile, float(ROUTE_LANES))
    nxt_e = jnp.min(later, axis=0, keepdims=True)
    nxt_e = jnp.where(nxt_e < float(ROUTE_LANES), nxt_e, -1.0)
    first = jnp.where(jnp.logical_and(t1 == pick(tile_start), t1 < used), 1.0, 0.0)
    my_rank = pick(rank)
    slot = my_rank - 2.0 * jnp.floor(my_rank * 0.5)
    fields = (owner, valid, first, pick(nxt_e), slot, jnp.broadcast_to(used, owner.shape))
    by_tile = sum(jnp.where(lane == n, f, 0.0) for n, f in enumerate(fields))
    meta_ref[...] = by_tile.T[:8, :].astype(I32)


def _plan_kernel(cnt_ref, route_ref, base_ref, meta_ref, pos1_ref, pos2_ref):
    step = pl.program_id(0)
    nt = cnt_ref.shape[0]
    subs_per_step = nt // PLAN_STEPS

    @pl.when(step == 0)
    def _():
        _plan_tables(cnt_ref, base_ref, meta_ref)
        pos1_ref[...] = jnp.zeros_like(pos1_ref)
        pos2_ref[...] = jnp.zeros_like(pos2_ref)

    sel_row = lax.broadcasted_iota(jnp.int32, (8, ROUTE_LANES), 0)
    sel_lane = lax.broadcasted_iota(jnp.int32, (8, ROUTE_LANES), 1)
    tok_r = lax.broadcasted_iota(jnp.int32, (MERGE_SUB, MERGE_SUB), 0)
    tok_c = lax.broadcasted_iota(jnp.int32, (MERGE_SUB, MERGE_SUB), 1)
    eye = jnp.where(tok_r == tok_c, 1.0, 0.0).astype(BF16)
    upto = jnp.where(tok_r <= tok_c, 1.0, 0.0).astype(BF16)
    sum_experts = lambda a: jnp.sum(a, axis=0, keepdims=True)

    def sub_positions(local, carry):
        s = step * subs_per_step + local
        route = route_ref[pl.ds(pl.multiple_of(local * MERGE_SUB, MERGE_SUB), MERGE_SUB), :].astype(BF16)
        record_t = _dot_tn(route, eye)
        incl_t = _dot_tn(route, upto)[:N_EXPERTS]
        both_t = record_t[:N_EXPERTS]
        second_t = record_t[LANE_SECOND0:LANE_SECOND0 + N_EXPERTS]
        rank2 = sum_experts(second_t * incl_t) - 1.0
        rank1 = sum_experts(both_t * incl_t) - rank2 - 2.0
        base_row = jnp.where(sel_lane < N_EXPERTS, jnp.broadcast_to(base_ref[pl.ds(s, 1), :], (8, ROUTE_LANES)), 0.0)
        hi = jnp.floor(base_row * (1.0 / 256.0))
        lo = base_row - 256.0 * hi
        hi2 = pltpu.roll(hi, LANE_SECOND0, axis=1)
        lo2 = pltpu.roll(lo, LANE_SECOND0, axis=1)
        table = (jnp.where(sel_row == 0, hi, 0.0) + jnp.where(sel_row == 1, lo, 0.0)
                 + jnp.where(sel_row == 2, hi2, 0.0) + jnp.where(sel_row == 3, lo2, 0.0))
        r = _dot_nt(table.astype(BF16), route)
        base_both = 256.0 * r[0:1] + r[1:2]
        base2 = 256.0 * r[2:3] + r[3:4]
        worker = s // subs_per_worker
        chunk0 = lax.rem(s, subs_per_worker) * chunks_per_sub
        for ref, row in ((pos1_ref, base_both - base2 + rank1), (pos2_ref, base2 + rank2)):
            for c in range(chunks_per_sub):
                ref[worker, pl.ds(chunk0 + c, 1), :] = row[:, c * SC_CHUNK:(c + 1) * SC_CHUNK].astype(I32)
        return carry

    chunks_per_sub = MERGE_SUB // SC_CHUNK
    subs_per_worker = nt // SC_WORKERS
    lax.fori_loop(0, subs_per_step, sub_positions, 0, unroll=4)


def _tile_plan(cnt, route, n_tiles):
    assert n_tiles <= ROUTE_LANES
    nt = cnt.shape[0]
    assert nt % SC_WORKERS == 0 and MERGE_SUB % SC_CHUNK == 0 and nt * MERGE_SUB <= SC_WORKERS * SC_IDX_ROWS * SC_CHUNK
    assert nt % PLAN_STEPS == 0
    pos = jax.ShapeDtypeStruct((SC_WORKERS, SC_IDX_ROWS, SC_CHUNK), I32)
    outs = [jax.ShapeDtypeStruct((nt, ROUTE_LANES), F32), jax.ShapeDtypeStruct((8, ROUTE_LANES), I32), pos, pos]
    whole = lambda a: pl.BlockSpec(a.shape, lambda i: (0,) * len(a.shape))
    _, meta, pos1, pos2 = pl.pallas_call(
        _plan_kernel,
        grid=(PLAN_STEPS,),
        in_specs=[pl.BlockSpec((nt, ROUTE_LANES), lambda i: (0, 0)),
                  pl.BlockSpec((route.shape[0] // PLAN_STEPS, ROUTE_LANES), lambda i: (i, 0))],
        out_specs=[whole(o) for o in outs],
        out_shape=outs,
        compiler_params=pltpu.CompilerParams(dimension_semantics=("arbitrary",), vmem_limit_bytes=VMEM_LIMIT),
        name="tile_plan",
    )(cnt.reshape(nt, ROUTE_LANES), route)
    return meta, pos1, pos2


def _sc_mesh():
    return plsc.VectorSubcoreMesh(core_axis_name="c", subcore_axis_name="s")


def _sc_worker():
    return lax.axis_index("s") * SC_CORES + lax.axis_index("c")


def _dispatch(hp, pos1, pos2, n_rows):
    t = hp.shape[0]
    per_worker = t // SC_WORKERS
    k = per_worker // SC_CHUNK
    idx = pltpu.VMEM((SC_IDX_ROWS, SC_CHUNK), I32)

    @functools.partial(pl.kernel, mesh=_sc_mesh(), out_type=jax.ShapeDtypeStruct((n_rows, HALF_D), I32),
                       scratch_types=[idx, idx, pltpu.VMEM((SC_CHUNK, HALF_D), I32), pltpu.SemaphoreType.DMA((2,))],
                       name="moe_dispatch")
    def run(h_hbm, p1_hbm, p2_hbm, xs_hbm, p1_v, p2_v, rows_v, sem):
        wid = _sc_worker()
        pltpu.sync_copy(p1_hbm.at[wid], p1_v)
        pltpu.sync_copy(p2_hbm.at[wid], p2_v)
        for j in range(k):
            pltpu.sync_copy(h_hbm.at[pl.ds(wid * per_worker + j * SC_CHUNK, SC_CHUNK)], rows_v)
            both = [pltpu.async_copy(rows_v, xs_hbm.at[p1_v.at[j]], sem.at[0]),
                    pltpu.async_copy(rows_v, xs_hbm.at[p2_v.at[j]], sem.at[1])]
            for copy in both:
                copy.wait()

    return run(hp, pos1, pos2)


def _combine_gather(ys, pos1, pos2, t):
    per_worker = t // SC_WORKERS
    k = per_worker // SC_CHUNK
    idx = pltpu.VMEM((SC_IDX_ROWS, SC_CHUNK), I32)
    out = jax.ShapeDtypeStruct((t, HALF_D), I32)

    @functools.partial(pl.kernel, mesh=_sc_mesh(), out_type=(out, out),
                       scratch_types=[idx, idx, pltpu.VMEM((SC_CHUNK, HALF_D), I32)], name="moe_combine")
    def run(ys_hbm, p1_hbm, p2_hbm, g1_hbm, g2_hbm, p1_v, p2_v, rows_v):
        wid = _sc_worker()
        pltpu.sync_copy(p1_hbm.at[wid], p1_v)
        pltpu.sync_copy(p2_hbm.at[wid], p2_v)
        for j in range(k):
            dst = pl.ds(wid * per_worker + j * SC_CHUNK, SC_CHUNK)
            pltpu.sync_copy(ys_hbm.at[p1_v.at[j]], rows_v)
            pltpu.sync_copy(rows_v, g1_hbm.at[dst])
            pltpu.sync_copy(ys_hbm.at[p2_v.at[j]], rows_v)
            pltpu.sync_copy(rows_v, g2_hbm.at[dst])

    return run(ys, pos1, pos2)


META_OWNER, META_VALID, META_FIRST, META_NEXT, META_SLOT, META_USED = range(6)


def _experts_kernel(meta_ref, xs_hbm, wg_hbm, wu_hbm, wd_hbm, ys_ref,
                    xs_buf, xs_sem, wg_f, wu_f, wd_f, wg_bf, wu_bf, wd_bf, w_sem):
    i = pl.program_id(0)
    used = meta_ref[META_USED, 0]
    valid = meta_ref[META_VALID, i]
    owner = meta_ref[META_OWNER, i]
    nxt = meta_ref[META_NEXT, i]

    def rows_in(tile):
        start = pl.multiple_of(tile * ROW_TILE, ROW_TILE)
        slot = lax.rem(tile, XS_SLOTS)
        return pltpu.make_async_copy(xs_hbm.at[pl.ds(start, ROW_TILE)], xs_buf.at[slot], xs_sem.at[slot])

    @pl.when(i == 0)
    def _():
        for ahead in range(XS_SLOTS - 1):
            pl.when(ahead < used)(rows_in(ahead).start)

    @pl.when(i + (XS_SLOTS - 1) < used)
    def _():
        rows_in(i + (XS_SLOTS - 1)).start()

    def fetch(expert, slot):
        pairs = ((wg_hbm, wg_f), (wu_hbm, wu_f), (wd_hbm, wd_f))
        return [pltpu.make_async_copy(w.at[expert], buf.at[slot], w_sem.at[slot, n])
                for n, (w, buf) in enumerate(pairs)]

    @pl.when(i == 0)
    def _():
        for copy in fetch(owner, 0):
            copy.start()

    @pl.when(meta_ref[META_FIRST, i] == 1)
    def _():
        slot = meta_ref[META_SLOT, i]
        for copy in fetch(owner, slot):
            copy.wait()
        wg_bf[...] = wg_f[slot].astype(BF16)
        wu_bf[...] = wu_f[slot].astype(BF16)
        wd_bf[...] = wd_f[slot].astype(BF16)

        @pl.when(nxt >= 0)
        def _():
            for copy in fetch(nxt, 1 - slot):
                copy.start()

    @pl.when(i < used)
    def _():
        rows_in(i).wait()
        row = lax.broadcasted_iota(jnp.int32, (ROW_TILE, HALF_D), 0)
        words = xs_buf[lax.rem(i, XS_SLOTS)]
        packed = jnp.where(row < valid, words, 0)
        hi, lo = _unpack_halves(packed)
        hi = hi.astype(BF16)
        lo = lo.astype(BF16)
        hg = _dot(hi, wg_bf[:HALF_D, :]) + _dot(lo, wg_bf[HALF_D:, :])
        hu = _dot(hi, wu_bf[:HALF_D, :]) + _dot(lo, wu_bf[HALF_D:, :])
        hid = (hg * _sigmoid(hg) * hu).astype(BF16)
        ys_ref[...] = _pack_halves(_dot(hid, wd_bf[...]))


def _experts(xs, meta, n_tiles, wg, wu, wd):
    tile = lambda i, meta: (jnp.minimum(i, meta[META_USED, 0] - 1), 0)
    hbm = pl.BlockSpec(memory_space=pl.ANY)
    up_shape, down_shape = (D_MODEL, EXPERT_FF), (EXPERT_FF, D_MODEL)
    grid_spec = pltpu.PrefetchScalarGridSpec(
        num_scalar_prefetch=1,
        grid=(n_tiles,),
        in_specs=[hbm, hbm, hbm, hbm],
        out_specs=pl.BlockSpec((ROW_TILE, HALF_D), tile),
        scratch_shapes=[pltpu.VMEM((XS_SLOTS, ROW_TILE, HALF_D), I32), pltpu.SemaphoreType.DMA((XS_SLOTS,)),
                        pltpu.VMEM((2,) + up_shape, F32), pltpu.VMEM((2,) + up_shape, F32),
                        pltpu.VMEM((2,) + down_shape, F32),
                        pltpu.VMEM(up_shape, BF16), pltpu.VMEM(up_shape, BF16), pltpu.VMEM(down_shape, BF16),
                        pltpu.SemaphoreType.DMA((2, 3))],
    )
    return pl.pallas_call(
        _experts_kernel,
        grid_spec=grid_spec,
        out_shape=jax.ShapeDtypeStruct((n_tiles * ROW_TILE, HALF_D), I32),
        compiler_params=pltpu.CompilerParams(dimension_semantics=("arbitrary",),
                                             vmem_limit_bytes=VMEM_LIMIT),
        name="experts",
    )(meta, xs, wg, wu, wd)


def _ple_kernel(x1_hbm, g1_hbm, g2_hbm, route_ref, p_ref, wp_f32, pg_ref, gg_ref, wgate_f32, o_ref,
                wp_ref, wgate_ref, x1_buf, g1_buf, g2_buf, ring_sem):
    i = pl.program_id(0)
    steps = pl.num_programs(0)
    streams = ((x1_hbm, x1_buf), (g1_hbm, g1_buf), (g2_hbm, g2_buf))

    def rows_in(step):
        start = pl.multiple_of(step * PLE_TM, PLE_TM)
        slot = lax.rem(step, PLE_SLOTS)
        return [pltpu.make_async_copy(src.at[pl.ds(start, PLE_TM)], buf.at[slot], ring_sem.at[n, slot])
                for n, (src, buf) in enumerate(streams)]

    def start_rows(step):
        for copy in rows_in(step):
            copy.start()

    @pl.when(i == 0)
    def _():
        for ahead in range(PLE_SLOTS - 1):
            pl.when(ahead < steps)(functools.partial(start_rows, ahead))
        wp_ref[...] = wp_f32[...].astype(BF16)
        wgate_ref[...] = wgate_f32[...].astype(BF16)

    pl.when(i + (PLE_SLOTS - 1) < steps)(functools.partial(start_rows, i + (PLE_SLOTS - 1)))

    for copy in rows_in(i):
        copy.wait()
    slot = lax.rem(i, PLE_SLOTS)
    x1_ref, g1_ref, g2_ref = x1_buf.at[slot], g1_buf.at[slot], g2_buf.at[slot]

    route = route_ref[...]
    lane = lax.broadcasted_iota(jnp.int32, route.shape, 1)
    pick = lambda ln: jnp.sum(jnp.where(lane == ln, route, 0.0), axis=-1, keepdims=True)
    c1 = pick(LANE_C1)
    c2 = pick(LANE_C2)
    a_hi, a_lo = _unpack_halves(g1_ref[...])
    b_hi, b_lo = _unpack_halves(g2_ref[...])
    x = x1_ref[...] + jnp.concatenate([c1 * a_hi + c2 * b_hi, c1 * a_lo + c2 * b_lo], axis=1)
    ple = _rms(_dot(p_ref[...].astype(BF16), wp_ref[...]), pg_ref[...])
    gate = _sigmoid(_dot(_rms(x, gg_ref[...]).astype(BF16), wgate_ref[...]))
    o_ref[...] = x + gate * ple


def _ple(x1, g1, g2, route, p2d, w_ple, ple_g, gate_g, w_gate):
    t = x1.shape[0]
    row = lambda w: pl.BlockSpec((PLE_TM, w), lambda i: (i, 0))
    full = lambda a: pl.BlockSpec(a.shape, lambda i: (0,) * a.ndim)
    hbm = pl.BlockSpec(memory_space=pl.ANY)
    return pl.pallas_call(
        _ple_kernel,
        grid=(t // PLE_TM,),
        in_specs=[hbm, hbm, hbm, row(ROUTE_LANES), row(PLE_DIM),
                  full(w_ple), full(ple_g), full(gate_g), full(w_gate)],
        out_specs=row(D_MODEL),
        out_shape=jax.ShapeDtypeStruct((t, D_MODEL), F32),
        scratch_shapes=[pltpu.VMEM(w_ple.shape, BF16), pltpu.VMEM(w_gate.shape, BF16),
                        pltpu.VMEM((PLE_SLOTS, PLE_TM, D_MODEL), F32),
                        pltpu.VMEM((PLE_SLOTS, PLE_TM, HALF_D), I32), pltpu.VMEM((PLE_SLOTS, PLE_TM, HALF_D), I32),
                        pltpu.SemaphoreType.DMA((3, PLE_SLOTS))],
        compiler_params=pltpu.CompilerParams(dimension_semantics=("arbitrary",),
                                             vmem_limit_bytes=VMEM_LIMIT),
        name="ple",
    )(x1, g1, g2, route, p2d, w_ple, ple_g, gate_g, w_gate)


def kernel(x, p, mix_norm_g, w_in, ret_gn_g, ret_gn_b, w_ret_o, q_norm_g, k_norm_g, attn_sinks,
           w_swa_o, w_out, ffn_norm_g, w_router_group, b_router_group, w_router_expert,
           b_router_expert, w_exp_gate, w_exp_up, w_exp_down, ple_gate_norm_g, w_ple_gate,
           w_ple, ple_norm_g):
    batch, seq, d = x.shape
    t = batch * seq
    depth = w_in.shape[0]
    n_tiles = 2 * t // ROW_TILE + N_EXPERTS
    row = lambda a: a.reshape(1, -1)
    x2d = x.reshape(t, d)
    for i in range(depth):
        router = (w_router_expert[i], w_router_group[i], row(b_router_expert[i]), row(b_router_group[i]))
        x1, hp, route, cnt = _block(x2d, row(mix_norm_g[i]), w_in[i], row(ret_gn_g[i]),
                                    row(ret_gn_b[i]), attn_sinks[i], row(q_norm_g[i]), row(k_norm_g[i]),
                                    w_ret_o[i], w_swa_o[i], w_out[i], row(ffn_norm_g[i]), router, seq)
        meta, pos1, pos2 = _tile_plan(cnt, route, n_tiles)
        xs = _dispatch(hp, pos1, pos2, n_tiles * ROW_TILE)
        ys = _experts(xs, meta, n_tiles, w_exp_gate[i], w_exp_up[i], w_exp_down[i])
        g1, g2 = _combine_gather(ys, pos1, pos2, t)
        x2d = _ple(x1, g1, g2, route, p[i].reshape(t, PLE_DIM), w_ple[i], row(ple_norm_g[i]),
                   row(ple_gate_norm_g[i]), w_ple_gate[i])
    return x2d.reshape(batch, seq, d)
```

```python
import functools

import jax
import jax.numpy as jnp
import numpy as np
from jax import lax
from jax.experimental import pallas as pl
from jax.experimental.pallas import tpu as pltpu
from jax.experimental.pallas import tpu_sc as plsc

F32 = jnp.float32
BF16 = jnp.bfloat16
I32 = jnp.int32

EPS = 1e-6
D_MODEL = 1024
PLE_DIM = 256
RET_HEADS = 4
RET_DK = 128
RET_DV = 128
RET_CHUNK = 128
SWA_HEADS = 8
SWA_KV_HEADS = 2
SWA_GROUP = SWA_HEADS // SWA_KV_HEADS
SWA_HD = 64
WINDOW = 128
N_GROUPS = 4
EXPERTS_PER_GROUP = 8
N_EXPERTS = N_GROUPS * EXPERTS_PER_GROUP
EXPERT_FF = 256

RET_W = RET_HEADS * RET_DK
SWA_Q = SWA_HEADS * SWA_HD
SWA_KV = SWA_KV_HEADS * SWA_HD
Z_RET_W = 4 * RET_W
Z_SKV_W = 2 * SWA_KV
Z_GATE_W = 2 * D_MODEL
COL_SQ = Z_RET_W
COL_SKV = COL_SQ + SWA_Q
COL_GATE = COL_SKV + Z_SKV_W
IN_WIDTH = COL_GATE + Z_GATE_W

ROUTE_LANES = 128
LANE_GROUP0 = N_EXPERTS
LANE_C1, LANE_C2 = 34, 35
LANE_SECOND0 = 64
NEG_INF = -1e30

HALF_D = D_MODEL // 2
ROW_TILE = 512
XS_SLOTS = 3
SC_CORES = 2
SC_SUBCORES = 16
SC_WORKERS = SC_CORES * SC_SUBCORES
SC_CHUNK = 128
SC_IDX_ROWS = 8

VMEM_LIMIT = 56 * 1024 * 1024

BLOCK_ROWS = 512
W_SLAB = 256
W_SLOTS = 6
RET_ROWS = 512
MERGE_SUB = 256
PLAN_STEPS = 4
PLE_TM = 1024
PLE_SLOTS = 3


def _rms(x, g):
    ms = jnp.mean(x * x, axis=-1, keepdims=True)
    return x * lax.rsqrt(ms + EPS) * g


def _sigmoid(x):
    return 1.0 / (1.0 + jnp.exp(-x))


def _dot(a, b):
    return jnp.dot(a, b, preferred_element_type=F32)


def _dot_nt(a, b):
    return lax.dot_general(a, b, (((1,), (1,)), ((), ())), preferred_element_type=F32)


def _dot_tn(a, b):
    return lax.dot_general(a, b, (((0,), (0,)), ((), ())), preferred_element_type=F32)


def _pack_halves(v):
    return pltpu.pack_elementwise([v[:, :HALF_D], v[:, HALF_D:]], packed_dtype=BF16)


def _unpack_halves(p):
    words = lax.bitcast_convert_type(p, jnp.uint32)
    return tuple(pltpu.unpack_elementwise(words, index=k, packed_dtype=BF16, unpacked_dtype=F32) for k in range(2))


def _retention_tables():
    f32 = np.float32
    h = RET_HEADS
    c = RET_CHUNK
    log_gamma = np.log1p(-np.exp2(f32(-5.0) - np.arange(h, dtype=f32))).astype(f32)
    pos = np.arange(c, dtype=f32)
    diff = pos[:, None] - pos[None, :]
    decay = np.where(diff[None] >= 0.0, np.exp(np.maximum(diff, f32(0.0))[None] * log_gamma[:, None, None]), f32(0.0))
    scale = f32(RET_DK ** -0.5)
    dmask = decay * scale
    zeta = np.exp((f32(c - 1.0) - pos)[None, :] * log_gamma[:, None]) * scale
    xi = np.exp((pos + f32(1.0))[None, :] * log_gamma[:, None])
    cdec = np.exp(f32(c) * log_gamma)
    bc = lambda a: np.ascontiguousarray(np.broadcast_to(a[:, :, None], (h, c, c)), dtype=f32)
    return (dmask.astype(f32), bc(zeta), bc(xi),
            np.ascontiguousarray(np.broadcast_to(cdec[:, None, None], (h, c, c)), dtype=f32))


def _swa_tables():
    f32 = np.float32
    qi = np.arange(WINDOW)[:, None]
    sj = np.arange(2 * WINDOW)[None, :]
    rel = qi + WINDOW - sj
    ok = (rel >= 0) & (rel < WINDOW)
    slopes = np.exp2(f32(-8.0) * np.arange(1, SWA_HEADS + 1, dtype=f32) / f32(SWA_HEADS)).astype(f32)
    bias = np.where(ok[None], -slopes[:, None, None] * rel.astype(f32)[None], f32(NEG_INF)).astype(f32)
    bias = bias.reshape(SWA_KV_HEADS, SWA_GROUP * WINDOW, 2 * WINDOW)
    avg = lambda heads: jnp.asarray(np.kron(np.eye(heads, dtype=f32), np.full((SWA_HD, SWA_HD), 1.0 / SWA_HD, f32)),
                                    dtype=BF16)
    return avg(SWA_HEADS), avg(SWA_KV_HEADS), jnp.asarray(bias)


def _block_kernel(per_seq, sinks_ref, x_ref, xp_ref, g_ref, w_hbm, dmask_ref, zeta_ref, xi_ref, cdec_ref,
                  gng_ref, gnb_ref, qgain_ref, kgain_ref, bq_ref, bk_ref, bias_ref,
                  wro_hbm, wso_hbm, wout_hbm, fg_ref, wre_ref, wrg_ref, bre_ref, brg_ref,
                  x1_ref, hp_ref, route_ref, cnt_ref,
                  w_ref, wro_ref, wso_ref, wout_ref, wtmp_ref, w_sem, qg_ref, kg_ref, wr_ref, br_ref,
                  zc_ref, zn_ref, yr_ref, ys_ref, state_ref, prev_ref, band_ref, tail_ref):
    g = pl.program_id(0)
    seq_start = lax.rem(g - 1, per_seq) == 0

    @pl.when(g == 0)
    def _():
        slabs = [(src, dst, c0) for src, dst in ((w_hbm, w_ref), (wro_hbm, wro_ref), (wso_hbm, wso_ref),
                                                 (wout_hbm, wout_ref))
                 for c0 in range(0, dst.shape[1], W_SLAB)]

        def slab(k):
            src, dst, c0 = slabs[k]
            return pltpu.make_async_copy(src.at[:, pl.ds(c0, W_SLAB)],
                                         wtmp_ref.at[k % W_SLOTS, pl.ds(0, dst.shape[0])], w_sem.at[k % W_SLOTS])

        for k in range(W_SLOTS - 1):
            slab(k).start()
        for k, (_, dst, c0) in enumerate(slabs):
            if k + W_SLOTS - 1 < len(slabs):
                slab(k + W_SLOTS - 1).start()
            slab(k).wait()
            dst[:, c0:c0 + W_SLAB] = wtmp_ref[k % W_SLOTS, :dst.shape[0]].astype(BF16)
        for hd in range(SWA_HEADS):
            qg_ref[:, hd * SWA_HD:(hd + 1) * SWA_HD] = qgain_ref[...] * (SWA_HD ** -0.5)
        for kh in range(SWA_KV_HEADS):
            kg_ref[:, kh * SWA_HD:(kh + 1) * SWA_HD] = kgain_ref[...]
        wr_ref[...] = jnp.zeros_like(wr_ref)
        wr_ref[:, :N_EXPERTS] = wre_ref[...].astype(BF16)
        wr_ref[:, LANE_GROUP0:LANE_GROUP0 + N_GROUPS] = wrg_ref[...].astype(BF16)
        br_ref[...] = jnp.zeros_like(br_ref)
        br_ref[:, :N_EXPERTS] = bre_ref[...]
        br_ref[:, LANE_GROUP0:LANE_GROUP0 + N_GROUPS] = brg_ref[...]
        zc_ref[...] = jnp.zeros_like(zc_ref)
        tail_ref[...] = jnp.zeros_like(tail_ref)
        state_ref[...] = jnp.zeros_like(state_ref)

    @pl.when(seq_start)
    def _():
        state_ref[...] = jnp.zeros_like(state_ref)

    band_ref[:WINDOW, :] = tail_ref[...]
    band_ref[WINDOW:, :] = zc_ref[:, COL_SKV:COL_GATE]

    h = _rms(x_ref[...], g_ref[...]).astype(BF16)

    def project(c0, width):
        zn_ref[:, c0:c0 + width] = _dot(h, w_ref[:, c0:c0 + width]).astype(BF16)

    cw = 512
    proj_pieces = [functools.partial(project, j * cw, cw) for j in range(COL_SKV // cw)]
    proj_pieces.append(functools.partial(project, COL_SKV, Z_SKV_W))
    proj_pieces += [functools.partial(project, COL_GATE + j * cw, cw) for j in range(Z_GATE_W // cw)]

    def retention_rows(t):
        r_base = t * RET_ROWS
        n_chunks = RET_ROWS // RET_CHUNK
        part = lambda c, p, hd: zc_ref[r_base + c * RET_CHUNK:r_base + (c + 1) * RET_CHUNK,
                                       p * RET_W + hd * RET_DK:p * RET_W + (hd + 1) * RET_DK]
        for hd in range(RET_HEADS):
            state = state_ref[hd]
            for c in range(n_chunks):
                prev_ref[t * n_chunks + c, hd] = state.astype(BF16)
                kz = (part(c, 1, hd).astype(F32) * zeta_ref[hd]).astype(BF16)
                state = cdec_ref[hd] * state + _dot_tn(kz, part(c, 2, hd))
            state_ref[hd] = state
        out_rows = []
        for c in range(n_chunks):
            heads = []
            for hd in range(RET_HEADS):
                q = part(c, 0, hd)
                scores = _dot_nt(q, part(c, 1, hd)) * dmask_ref[hd]
                y = _dot(scores.astype(BF16), part(c, 2, hd)) + _dot(q, prev_ref[t * n_chunks + c, hd]) * xi_ref[hd]
                mu = jnp.mean(y, axis=-1, keepdims=True)
                d = y - mu
                var = jnp.mean(d * d, axis=-1, keepdims=True)
                hs = slice(hd * RET_DV, (hd + 1) * RET_DV)
                yn = d * lax.rsqrt(var + EPS) * gng_ref[:, hs] + gnb_ref[:, hs]
                gate = part(c, 3, hd).astype(F32)
                heads.append((gate * _sigmoid(gate) * yn).astype(BF16))
            out_rows.append(jnp.concatenate(heads, axis=1))
        yr_ref[r_base:r_base + RET_ROWS, :] = jnp.concatenate(out_rows, axis=0)

    group_rows = SWA_GROUP * WINDOW
    lo_q = lax.broadcasted_iota(jnp.int32, (WINDOW, 2 * SWA_HD), 1) < SWA_HD
    lo_k = lax.broadcasted_iota(jnp.int32, (2 * WINDOW, 2 * SWA_HD), 1) < SWA_HD
    in_prev = lax.broadcasted_iota(jnp.int32, (group_rows, 2 * WINDOW), 1) < WINDOW
    row_head = lax.broadcasted_iota(jnp.int32, (group_rows, 1), 0) // WINDOW

    def both_heads(v):
        swapped = pltpu.roll(v, SWA_HD, axis=1)
        return jnp.where(lo_k, v, swapped), jnp.where(lo_k, swapped, v)

    def swa_block(n):
        rows = slice(n * WINDOW, (n + 1) * WINDOW)
        kv = band_ref[n * WINDOW:(n + 2) * WINDOW, :]
        kf = kv[:, :SWA_KV].astype(F32)
        kn = kf * lax.rsqrt(_dot((kf * kf).astype(BF16), bk_ref[...]) + EPS) * kg_ref[...]
        keys = [a.astype(BF16) for a in both_heads(kn)]
        vals = [a.astype(BF16) for a in both_heads(kv[:, SWA_KV:].astype(F32))]
        qf = zc_ref[rows, COL_SQ:COL_SKV].astype(F32)
        qn = qf * lax.rsqrt(_dot((qf * qf).astype(BF16), bq_ref[...]) + EPS) * qg_ref[...]
        outs = []
        for kh in range(SWA_KV_HEADS):
            parts = []
            for j in range(SWA_GROUP // 2):
                c0 = (kh * SWA_GROUP + 2 * j) * SWA_HD
                two = qn[:, c0:c0 + 2 * SWA_HD]
                parts += [jnp.where(lo_q, two, 0.0), jnp.where(lo_q, 0.0, two)]
            qs = jnp.concatenate(parts, axis=0).astype(BF16)
            s = _dot_nt(qs, keys[kh]) + bias_ref[kh]
            if n == 0:
                s = jnp.where(jnp.logical_and(seq_start, in_prev), NEG_INF, s)
            sink = sinks_ref[kh * SWA_GROUP]
            for gi in range(1, SWA_GROUP):
                sink = jnp.where(row_head == gi, sinks_ref[kh * SWA_GROUP + gi], sink)
            m = jnp.maximum(jnp.max(s, axis=-1, keepdims=True), sink)
            p = jnp.exp(s - m)
            denom = jnp.sum(p, axis=-1, keepdims=True) + jnp.exp(sink - m)
            o = _dot(p.astype(BF16), vals[kh]) * (1.0 / denom)
            for j in range(SWA_GROUP // 2):
                even = o[(2 * j) * WINDOW:(2 * j + 1) * WINDOW]
                odd = o[(2 * j + 1) * WINDOW:(2 * j + 2) * WINDOW]
                outs.append(jnp.where(lo_q, even, odd).astype(BF16))
        ys_ref[rows, :] = jnp.concatenate(outs, axis=1)

    mix_pieces = [functools.partial(retention_rows, t) for t in range(BLOCK_ROWS // RET_ROWS)]
    mix_pieces += [functools.partial(swa_block, n) for n in range(BLOCK_ROWS // WINDOW)]

    subs = []

    def merge_sub(s):
        subs.append(_merge_rows(slice(s * MERGE_SUB, (s + 1) * MERGE_SUB), yr_ref, ys_ref, zc_ref, xp_ref, wro_ref,
                                wso_ref, wout_ref, fg_ref, wr_ref, br_ref))

    other_pieces = mix_pieces + [functools.partial(merge_sub, s) for s in range(BLOCK_ROWS // MERGE_SUB)]

    n_other = len(other_pieces)
    for k, piece in enumerate(other_pieces):
        for proj_piece in proj_pieces[k * len(proj_pieces) // n_other:(k + 1) * len(proj_pieces) // n_other]:
            proj_piece()
        piece()
    x1_ref[...] = jnp.concatenate([s[0] for s in subs], axis=0)
    hp_ref[...] = jnp.concatenate([s[1] for s in subs], axis=0)
    route_ref[...] = jnp.concatenate([s[2] for s in subs], axis=0)
    for s, sub in enumerate(subs):
        cnt_ref[s] = sub[3]

    tail_ref[...] = zc_ref[BLOCK_ROWS - WINDOW:, COL_SKV:COL_GATE]
    zc_ref[...] = zn_ref[...]


def _merge_rows(rows, yr_ref, ys_ref, z_ref, x_ref, wro_ref, wso_ref, wout_ref, fg_ref, wr_ref, br_ref):
    a = _dot(yr_ref[rows, :], wro_ref[...])
    b = _dot(ys_ref[rows, :], wso_ref[...])
    gate_r = z_ref[rows, COL_GATE:COL_GATE + D_MODEL].astype(F32)
    gate_s = z_ref[rows, COL_GATE + D_MODEL:].astype(F32)
    merged = (_sigmoid(gate_r) * a + _sigmoid(gate_s) * b).astype(BF16)
    x1 = x_ref[rows, :] + _dot(merged, wout_ref[...])
    h2 = _rms(x1, fg_ref[...])
    packed = _pack_halves(h2)

    logits = _dot(h2.astype(BF16), wr_ref[...]) + br_ref[...]
    lane = lax.broadcasted_iota(jnp.int32, logits.shape, 1)
    big = jnp.int32(ROUTE_LANES)
    is_group = jnp.logical_and(lane >= LANE_GROUP0, lane < LANE_GROUP0 + N_GROUPS)
    gl = jnp.where(is_group, logits, NEG_INF)
    gmax = jnp.max(gl, axis=-1, keepdims=True)
    g_w = 1.0 / jnp.sum(jnp.exp(gl - gmax), axis=-1, keepdims=True)
    g_sel = jnp.min(jnp.where(gl == gmax, lane, big), axis=-1, keepdims=True) - LANE_GROUP0
    in_group = jnp.logical_and(lane < N_EXPERTS, (lane >> 3) == g_sel)
    el = jnp.where(in_group, logits, NEG_INF)
    m1 = jnp.max(el, axis=-1, keepdims=True)
    i1 = jnp.min(jnp.where(el == m1, lane, big), axis=-1, keepdims=True)
    el2 = jnp.where(lane == i1, NEG_INF, el)
    m2 = jnp.max(el2, axis=-1, keepdims=True)
    i2 = jnp.min(jnp.where(el2 == m2, lane, big), axis=-1, keepdims=True)
    e2 = jnp.exp(m2 - m1)
    c1 = g_w / (1.0 + e2)
    c2 = g_w * e2 / (1.0 + e2)
    chosen = jnp.where(jnp.logical_or(lane == i1, lane == i2), 1.0, 0.0)
    route = (chosen
             + jnp.where(lane == LANE_C1, c1, 0.0)
             + jnp.where(lane == LANE_C2, c2, 0.0)
             + jnp.where(lane == i2 + LANE_SECOND0, 1.0, 0.0))
    return x1, packed, route, jnp.sum(chosen, axis=0, keepdims=True)


def _block(x2d, mix_g, w_in, gn_g, gn_b, sinks, q_g, k_g, w_ret_o, w_swa_o, w_out, ffn_g, router, seq):
    t = x2d.shape[0]
    n = t // BLOCK_ROWS
    assert seq % BLOCK_ROWS == 0
    consts = _retention_tables() + (gn_g, gn_b, q_g, k_g) + _swa_tables()
    merge_weights = (w_ret_o, w_swa_o, w_out)
    small = (ffn_g,) + tuple(router)
    hbm = pl.BlockSpec(memory_space=pl.ANY)
    full = lambda a: pl.BlockSpec(a.shape, lambda g: (0,) * a.ndim)
    cur = lambda w: pl.BlockSpec((BLOCK_ROWS, w), lambda g: (jnp.minimum(g, n - 1), 0))
    prv = lambda w: pl.BlockSpec((BLOCK_ROWS, w), lambda g: (jnp.maximum(g - 1, 0), 0))
    chunks = BLOCK_ROWS // RET_CHUNK
    subs_per_block = BLOCK_ROWS // MERGE_SUB
    return pl.pallas_call(
        functools.partial(_block_kernel, seq // BLOCK_ROWS),
        grid=(n + 1,),
        in_specs=([pl.BlockSpec(memory_space=pltpu.SMEM), cur(D_MODEL), prv(D_MODEL), full(mix_g), hbm]
                  + [full(a) for a in consts] + [hbm] * len(merge_weights) + [full(a) for a in small]),
        out_specs=[prv(D_MODEL), prv(HALF_D), prv(ROUTE_LANES),
                   pl.BlockSpec((subs_per_block, 1, ROUTE_LANES), lambda g: (jnp.maximum(g - 1, 0), 0, 0))],
        out_shape=[jax.ShapeDtypeStruct((t, D_MODEL), F32),
                   jax.ShapeDtypeStruct((t, HALF_D), I32),
                   jax.ShapeDtypeStruct((t, ROUTE_LANES), F32),
                   jax.ShapeDtypeStruct((n * subs_per_block, 1, ROUTE_LANES), F32)],
        scratch_shapes=[pltpu.VMEM(w_in.shape, BF16)] + [pltpu.VMEM(w.shape, BF16) for w in merge_weights]
                       + [pltpu.VMEM((W_SLOTS, D_MODEL, W_SLAB), F32), pltpu.SemaphoreType.DMA((W_SLOTS,)),
                        pltpu.VMEM((1, SWA_Q), F32), pltpu.VMEM((1, SWA_KV), F32),
                        pltpu.VMEM((D_MODEL, ROUTE_LANES), BF16), pltpu.VMEM((1, ROUTE_LANES), F32),
                        pltpu.VMEM((BLOCK_ROWS, IN_WIDTH), BF16), pltpu.VMEM((BLOCK_ROWS, IN_WIDTH), BF16),
                        pltpu.VMEM((BLOCK_ROWS, RET_W), BF16), pltpu.VMEM((BLOCK_ROWS, SWA_Q), BF16),
                        pltpu.VMEM((RET_HEADS, RET_DK, RET_DV), F32),
                        pltpu.VMEM((chunks, RET_HEADS, RET_DK, RET_DV), BF16),
                        pltpu.VMEM((BLOCK_ROWS + WINDOW, Z_SKV_W), BF16), pltpu.VMEM((WINDOW, Z_SKV_W), BF16)],
        compiler_params=pltpu.CompilerParams(dimension_semantics=("arbitrary",),
                                             vmem_limit_bytes=VMEM_LIMIT),
        name="block",
    )(sinks, x2d, x2d, mix_g, w_in, *consts, *merge_weights, *small)


def _plan_tables(cnt_ref, base_ref, meta_ref):
    counts = cnt_ref[...]
    nt = counts.shape[0]
    exact = functools.partial(jnp.dot, precision=lax.Precision.HIGHEST, preferred_element_type=F32)
    total = jnp.sum(counts, axis=0, keepdims=True)
    tiles_e = jnp.ceil(total * (1.0 / ROW_TILE))
    k = lax.broadcasted_iota(jnp.int32, (ROUTE_LANES, ROUTE_LANES), 0)
    lane = lax.broadcasted_iota(jnp.int32, (ROUTE_LANES, ROUTE_LANES), 1)
    tile_end = exact(jnp.broadcast_to(tiles_e, (8, ROUTE_LANES)), jnp.where(k <= lane, 1.0, 0.0))[0:1]
    tile_start = tile_end - tiles_e
    r = lax.broadcasted_iota(jnp.int32, (nt, nt), 0)
    c = lax.broadcasted_iota(jnp.int32, (nt, nt), 1)
    before = exact(jnp.where(c < r, 1.0, 0.0), counts)
    base_ref[...] = tile_start * ROW_TILE + before

    used = jnp.max(tile_end, axis=-1, keepdims=True)
    tile = k.astype(F32)
    clamped = jnp.minimum(tile, jnp.maximum(used - 1.0, 0.0))
    is_expert = lane < N_EXPERTS
    owner = jnp.sum(jnp.where(jnp.logical_and(is_expert, tile_end <= clamped), 1.0, 0.0), axis=-1, keepdims=True)
    owner = jnp.minimum(owner, N_EXPERTS - 1.0)
    mine = lane.astype(F32) == owner
    pick = lambda v: jnp.sum(jnp.where(mine, v, 0.0), axis=-1, keepdims=True)
    t1 = tile[:, 0:1]
    valid = jnp.clip(pick(total) - (t1 - pick(tile_start)) * ROW_TILE, 0.0, float(ROW_TILE))
    valid = jnp.where(t1 < used, valid, 0.0)

    nonempty = jnp.where(jnp.logical_and(lane[0:1] < N_EXPERTS, tiles_e > 0.0), 1.0, 0.0)
    rank = exact(jnp.broadcast_to(nonempty, (8, ROUTE_LANES)), jnp.where(k < lane, 1.0, 0.0))[0:1]
    nonempty_col = jnp.sum(jnp.where(k == lane, jnp.broadcast_to(nonempty, k.shape), 0.0), axis=-1, keepdims=True)
    later = jnp.where(jnp.logical_and(k > lane, nonempty_col > 0.0), tile, float(ROUTE_LANES))
    nxt_e = jnp.min(later, axis=0, keepdims=True)
    nxt_e = jnp.where(nxt_e < float(ROUTE_LANES), nxt_e, -1.0)
    first = jnp.where(jnp.logical_and(t1 == pick(tile_start), t1 < used), 1.0, 0.0)
    my_rank = pick(rank)
    slot = my_rank - 2.0 * jnp.floor(my_rank * 0.5)
    fields = (owner, valid, first, pick(nxt_e), slot, jnp.broadcast_to(used, owner.shape))
    by_tile = sum(jnp.where(lane == n, f, 0.0) for n, f in enumerate(fields))
    meta_ref[...] = by_tile.T[:8, :].astype(I32)


def _plan_kernel(cnt_ref, route_ref, base_ref, meta_ref, pos1_ref, pos2_ref):
    step = pl.program_id(0)
    nt = cnt_ref.shape[0]
    subs_per_step = nt // PLAN_STEPS

    @pl.when(step == 0)
    def _():
        _plan_tables(cnt_ref, base_ref, meta_ref)
        pos1_ref[...] = jnp.zeros_like(pos1_ref)
        pos2_ref[...] = jnp.zeros_like(pos2_ref)

    sel_row = lax.broadcasted_iota(jnp.int32, (8, ROUTE_LANES), 0)
    sel_lane = lax.broadcasted_iota(jnp.int32, (8, ROUTE_LANES), 1)
    tok_r = lax.broadcasted_iota(jnp.int32, (MERGE_SUB, MERGE_SUB), 0)
    tok_c = lax.broadcasted_iota(jnp.int32, (MERGE_SUB, MERGE_SUB), 1)
    eye = jnp.where(tok_r == tok_c, 1.0, 0.0).astype(BF16)
    upto = jnp.where(tok_r <= tok_c, 1.0, 0.0).astype(BF16)
    sum_experts = lambda a: jnp.sum(a, axis=0, keepdims=True)

    def sub_positions(local, carry):
        s = step * subs_per_step + local
        route = route_ref[pl.ds(pl.multiple_of(local * MERGE_SUB, MERGE_SUB), MERGE_SUB), :].astype(BF16)
        record_t = _dot_tn(route, eye)
        incl_t = _dot_tn(route, upto)[:N_EXPERTS]
        both_t = record_t[:N_EXPERTS]
        second_t = record_t[LANE_SECOND0:LANE_SECOND0 + N_EXPERTS]
        rank2 = sum_experts(second_t * incl_t) - 1.0
        rank1 = sum_experts(both_t * incl_t) - rank2 - 2.0
        base_row = jnp.where(sel_lane < N_EXPERTS, jnp.broadcast_to(base_ref[pl.ds(s, 1), :], (8, ROUTE_LANES)), 0.0)
        hi = jnp.floor(base_row * (1.0 / 256.0))
        lo = base_row - 256.0 * hi
        hi2 = pltpu.roll(hi, LANE_SECOND0, axis=1)
        lo2 = pltpu.roll(lo, LANE_SECOND0, axis=1)
        table = (jnp.where(sel_row == 0, hi, 0.0) + jnp.where(sel_row == 1, lo, 0.0)
                 + jnp.where(sel_row == 2, hi2, 0.0) + jnp.where(sel_row == 3, lo2, 0.0))
        r = _dot_nt(table.astype(BF16), route)
        base_both = 256.0 * r[0:1] + r[1:2]
        base2 = 256.0 * r[2:3] + r[3:4]
        worker = s // subs_per_worker
        chunk0 = lax.rem(s, subs_per_worker) * chunks_per_sub
        for ref, row in ((pos1_ref, base_both - base2 + rank1), (pos2_ref, base2 + rank2)):
            for c in range(chunks_per_sub):
                ref[worker, pl.ds(chunk0 + c, 1), :] = row[:, c * SC_CHUNK:(c + 1) * SC_CHUNK].astype(I32)
        return carry

    chunks_per_sub = MERGE_SUB // SC_CHUNK
    subs_per_worker = nt // SC_WORKERS
    lax.fori_loop(0, subs_per_step, sub_positions, 0, unroll=8)


def _tile_plan(cnt, route, n_tiles):
    assert n_tiles <= ROUTE_LANES
    nt = cnt.shape[0]
    assert nt % SC_WORKERS == 0 and MERGE_SUB % SC_CHUNK == 0 and nt * MERGE_SUB <= SC_WORKERS * SC_IDX_ROWS * SC_CHUNK
    assert nt % PLAN_STEPS == 0
    pos = jax.ShapeDtypeStruct((SC_WORKERS, SC_IDX_ROWS, SC_CHUNK), I32)
    outs = [jax.ShapeDtypeStruct((nt, ROUTE_LANES), F32), jax.ShapeDtypeStruct((8, ROUTE_LANES), I32), pos, pos]
    whole = lambda a: pl.BlockSpec(a.shape, lambda i: (0,) * len(a.shape))
    _, meta, pos1, pos2 = pl.pallas_call(
        _plan_kernel,
        grid=(PLAN_STEPS,),
        in_specs=[pl.BlockSpec((nt, ROUTE_LANES), lambda i: (0, 0)),
                  pl.BlockSpec((route.shape[0] // PLAN_STEPS, ROUTE_LANES), lambda i: (i, 0))],
        out_specs=[whole(o) for o in outs],
        out_shape=outs,
        compiler_params=pltpu.CompilerParams(dimension_semantics=("arbitrary",), vmem_limit_bytes=VMEM_LIMIT),
        name="tile_plan",
    )(cnt.reshape(nt, ROUTE_LANES), route)
    return meta, pos1, pos2


def _sc_mesh():
    return plsc.VectorSubcoreMesh(core_axis_name="c", subcore_axis_name="s")


def _sc_worker():
    return lax.axis_index("s") * SC_CORES + lax.axis_index("c")


def _dispatch(hp, pos1, pos2, n_rows):
    t = hp.shape[0]
    per_worker = t // SC_WORKERS
    k = per_worker // SC_CHUNK
    idx = pltpu.VMEM((SC_IDX_ROWS, SC_CHUNK), I32)

    @functools.partial(pl.kernel, mesh=_sc_mesh(), out_type=jax.ShapeDtypeStruct((n_rows, HALF_D), I32),
                       scratch_types=[idx, idx, pltpu.VMEM((SC_CHUNK, HALF_D), I32)], name="moe_dispatch")
    def run(h_hbm, p1_hbm, p2_hbm, xs_hbm, p1_v, p2_v, rows_v):
        wid = _sc_worker()
        pltpu.sync_copy(p1_hbm.at[wid], p1_v)
        pltpu.sync_copy(p2_hbm.at[wid], p2_v)
        for j in range(k):
            pltpu.sync_copy(h_hbm.at[pl.ds(wid * per_worker + j * SC_CHUNK, SC_CHUNK)], rows_v)
            pltpu.sync_copy(rows_v, xs_hbm.at[p1_v.at[j]])
            pltpu.sync_copy(rows_v, xs_hbm.at[p2_v.at[j]])

    return run(hp, pos1, pos2)


def _combine_gather(ys, pos1, pos2, t):
    per_worker = t // SC_WORKERS
    k = per_worker // SC_CHUNK
    idx = pltpu.VMEM((SC_IDX_ROWS, SC_CHUNK), I32)
    out = jax.ShapeDtypeStruct((t, HALF_D), I32)

    @functools.partial(pl.kernel, mesh=_sc_mesh(), out_type=(out, out),
                       scratch_types=[idx, idx, pltpu.VMEM((SC_CHUNK, HALF_D), I32)], name="moe_combine")
    def run(ys_hbm, p1_hbm, p2_hbm, g1_hbm, g2_hbm, p1_v, p2_v, rows_v):
        wid = _sc_worker()
        pltpu.sync_copy(p1_hbm.at[wid], p1_v)
        pltpu.sync_copy(p2_hbm.at[wid], p2_v)
        for j in range(k):
            dst = pl.ds(wid * per_worker + j * SC_CHUNK, SC_CHUNK)
            pltpu.sync_copy(ys_hbm.at[p1_v.at[j]], rows_v)
            pltpu.sync_copy(rows_v, g1_hbm.at[dst])
            pltpu.sync_copy(ys_hbm.at[p2_v.at[j]], rows_v)
            pltpu.sync_copy(rows_v, g2_hbm.at[dst])

    return run(ys, pos1, pos2)


META_OWNER, META_VALID, META_FIRST, META_NEXT, META_SLOT, META_USED = range(6)


def _experts_kernel(meta_ref, xs_hbm, wg_hbm, wu_hbm, wd_hbm, ys_ref,
                    xs_buf, xs_sem, wg_f, wu_f, wd_f, wg_bf, wu_bf, wd_bf, w_sem):
    i = pl.program_id(0)
    used = meta_ref[META_USED, 0]
    valid = meta_ref[META_VALID, i]
    owner = meta_ref[META_OWNER, i]
    nxt = meta_ref[META_NEXT, i]

    def rows_in(tile):
        start = pl.multiple_of(tile * ROW_TILE, ROW_TILE)
        slot = lax.rem(tile, XS_SLOTS)
        return pltpu.make_async_copy(xs_hbm.at[pl.ds(start, ROW_TILE)], xs_buf.at[slot], xs_sem.at[slot])

    @pl.when(i == 0)
    def _():
        for ahead in range(XS_SLOTS - 1):
            pl.when(ahead < used)(rows_in(ahead).start)

    @pl.when(i + (XS_SLOTS - 1) < used)
    def _():
        rows_in(i + (XS_SLOTS - 1)).start()

    def fetch(expert, slot):
        pairs = ((wg_hbm, wg_f), (wu_hbm, wu_f), (wd_hbm, wd_f))
        return [pltpu.make_async_copy(w.at[expert], buf.at[slot], w_sem.at[slot, n])
                for n, (w, buf) in enumerate(pairs)]

    @pl.when(i == 0)
    def _():
        for copy in fetch(owner, 0):
            copy.start()

    @pl.when(meta_ref[META_FIRST, i] == 1)
    def _():
        slot = meta_ref[META_SLOT, i]
        for copy in fetch(owner, slot):
            copy.wait()
        wg_bf[...] = wg_f[slot].astype(BF16)
        wu_bf[...] = wu_f[slot].astype(BF16)
        wd_bf[...] = wd_f[slot].astype(BF16)

        @pl.when(nxt >= 0)
        def _():
            for copy in fetch(nxt, 1 - slot):
                copy.start()

    @pl.when(i < used)
    def _():
        rows_in(i).wait()
        row = lax.broadcasted_iota(jnp.int32, (ROW_TILE, HALF_D), 0)
        words = xs_buf[lax.rem(i, XS_SLOTS)]
        packed = jnp.where(row < valid, words, 0)
        hi, lo = _unpack_halves(packed)
        hi = hi.astype(BF16)
        lo = lo.astype(BF16)
        hg = _dot(hi, wg_bf[:HALF_D, :]) + _dot(lo, wg_bf[HALF_D:, :])
        hu = _dot(hi, wu_bf[:HALF_D, :]) + _dot(lo, wu_bf[HALF_D:, :])
        hid = (hg * _sigmoid(hg) * hu).astype(BF16)
        ys_ref[...] = _pack_halves(_dot(hid, wd_bf[...]))


def _experts(xs, meta, n_tiles, wg, wu, wd):
    tile = lambda i, meta: (jnp.minimum(i, meta[META_USED, 0] - 1), 0)
    hbm = pl.BlockSpec(memory_space=pl.ANY)
    up_shape, down_shape = (D_MODEL, EXPERT_FF), (EXPERT_FF, D_MODEL)
    grid_spec = pltpu.PrefetchScalarGridSpec(
        num_scalar_prefetch=1,
        grid=(n_tiles,),
        in_specs=[hbm, hbm, hbm, hbm],
        out_specs=pl.BlockSpec((ROW_TILE, HALF_D), tile),
        scratch_shapes=[pltpu.VMEM((XS_SLOTS, ROW_TILE, HALF_D), I32), pltpu.SemaphoreType.DMA((XS_SLOTS,)),
                        pltpu.VMEM((2,) + up_shape, F32), pltpu.VMEM((2,) + up_shape, F32),
                        pltpu.VMEM((2,) + down_shape, F32),
                        pltpu.VMEM(up_shape, BF16), pltpu.VMEM(up_shape, BF16), pltpu.VMEM(down_shape, BF16),
                        pltpu.SemaphoreType.DMA((2, 3))],
    )
    return pl.pallas_call(
        _experts_kernel,
        grid_spec=grid_spec,
        out_shape=jax.ShapeDtypeStruct((n_tiles * ROW_TILE, HALF_D), I32),
        compiler_params=pltpu.CompilerParams(dimension_semantics=("arbitrary",),
                                             vmem_limit_bytes=VMEM_LIMIT),
        name="experts",
    )(meta, xs, wg, wu, wd)


def _ple_kernel(x1_hbm, g1_hbm, g2_hbm, route_ref, p_ref, wp_f32, pg_ref, gg_ref, wgate_f32, o_ref,
                wp_ref, wgate_ref, x1_buf, g1_buf, g2_buf, ring_sem):
    i = pl.program_id(0)
    steps = pl.num_programs(0)
    streams = ((x1_hbm, x1_buf), (g1_hbm, g1_buf), (g2_hbm, g2_buf))

    def rows_in(step):
        start = pl.multiple_of(step * PLE_TM, PLE_TM)
        slot = lax.rem(step, PLE_SLOTS)
        return [pltpu.make_async_copy(src.at[pl.ds(start, PLE_TM)], buf.at[slot], ring_sem.at[n, slot])
                for n, (src, buf) in enumerate(streams)]

    def start_rows(step):
        for copy in rows_in(step):
            copy.start()

    @pl.when(i == 0)
    def _():
        for ahead in range(PLE_SLOTS - 1):
            pl.when(ahead < steps)(functools.partial(start_rows, ahead))
        wp_ref[...] = wp_f32[...].astype(BF16)
        wgate_ref[...] = wgate_f32[...].astype(BF16)

    pl.when(i + (PLE_SLOTS - 1) < steps)(functools.partial(start_rows, i + (PLE_SLOTS - 1)))

    for copy in rows_in(i):
        copy.wait()
    slot = lax.rem(i, PLE_SLOTS)
    x1_ref, g1_ref, g2_ref = x1_buf.at[slot], g1_buf.at[slot], g2_buf.at[slot]

    route = route_ref[...]
    lane = lax.broadcasted_iota(jnp.int32, route.shape, 1)
    pick = lambda ln: jnp.sum(jnp.where(lane == ln, route, 0.0), axis=-1, keepdims=True)
    c1 = pick(LANE_C1)
    c2 = pick(LANE_C2)
    a_hi, a_lo = _unpack_halves(g1_ref[...])
    b_hi, b_lo = _unpack_halves(g2_ref[...])
    x = x1_ref[...] + jnp.concatenate([c1 * a_hi + c2 * b_hi, c1 * a_lo + c2 * b_lo], axis=1)
    ple = _rms(_dot(p_ref[...].astype(BF16), wp_ref[...]), pg_ref[...])
    gate = _sigmoid(_dot(_rms(x, gg_ref[...]).astype(BF16), wgate_ref[...]))
    o_ref[...] = x + gate * ple


def _ple(x1, g1, g2, route, p2d, w_ple, ple_g, gate_g, w_gate):
    t = x1.shape[0]
    row = lambda w: pl.BlockSpec((PLE_TM, w), lambda i: (i, 0))
    full = lambda a: pl.BlockSpec(a.shape, lambda i: (0,) * a.ndim)
    hbm = pl.BlockSpec(memory_space=pl.ANY)
    return pl.pallas_call(
        _ple_kernel,
        grid=(t // PLE_TM,),
        in_specs=[hbm, hbm, hbm, row(ROUTE_LANES), row(PLE_DIM),
                  full(w_ple), full(ple_g), full(gate_g), full(w_gate)],
        out_specs=row(D_MODEL),
        out_shape=jax.ShapeDtypeStruct((t, D_MODEL), F32),
        scratch_shapes=[pltpu.VMEM(w_ple.shape, BF16), pltpu.VMEM(w_gate.shape, BF16),
                        pltpu.VMEM((PLE_SLOTS, PLE_TM, D_MODEL), F32),
                        pltpu.VMEM((PLE_SLOTS, PLE_TM, HALF_D), I32), pltpu.VMEM((PLE_SLOTS, PLE_TM, HALF_D), I32),
                        pltpu.SemaphoreType.DMA((3, PLE_SLOTS))],
        compiler_params=pltpu.CompilerParams(dimension_semantics=("arbitrary",),
                                             vmem_limit_bytes=VMEM_LIMIT),
        name="ple",
    )(x1, g1, g2, route, p2d, w_ple, ple_g, gate_g, w_gate)


def kernel(x, p, mix_norm_g, w_in, ret_gn_g, ret_gn_b, w_ret_o, q_norm_g, k_norm_g, attn_sinks,
           w_swa_o, w_out, ffn_norm_g, w_router_group, b_router_group, w_router_expert,
           b_router_expert, w_exp_gate, w_exp_up, w_exp_down, ple_gate_norm_g, w_ple_gate,
           w_ple, ple_norm_g):
    batch, seq, d = x.shape
    t = batch * seq
    depth = w_in.shape[0]
    n_tiles = 2 * t // ROW_TILE + N_EXPERTS
    row = lambda a: a.reshape(1, -1)
    x2d = x.reshape(t, d)
    for i in range(depth):
        router = (w_router_expert[i], w_router_group[i], row(b_router_expert[i]), row(b_router_group[i]))
        x1, hp, route, cnt = _block(x2d, row(mix_norm_g[i]), w_in[i], row(ret_gn_g[i]),
                                    row(ret_gn_b[i]), attn_sinks[i], row(q_norm_g[i]), row(k_norm_g[i]),
                                    w_ret_o[i], w_swa_o[i], w_out[i], row(ffn_norm_g[i]), router, seq)
        meta, pos1, pos2 = _tile_plan(cnt, route, n_tiles)
        xs = _dispatch(hp, pos1, pos2, n_tiles * ROW_TILE)
        ys = _experts(xs, meta, n_tiles, w_exp_gate[i], w_exp_up[i], w_exp_down[i])
        g1, g2 = _combine_gather(ys, pos1, pos2, t)
        x2d = _ple(x1, g1, g2, route, p[i].reshape(t, PLE_DIM), w_ple[i], row(ple_norm_g[i]),
                   row(ple_gate_norm_g[i]), w_ple_gate[i])
    return x2d.reshape(batch, seq, d)
```

```python
import functools

import jax
import jax.numpy as jnp
import numpy as np
from jax import lax
from jax.experimental import pallas as pl
from jax.experimental.pallas import tpu as pltpu
from jax.experimental.pallas import tpu_sc as plsc

F32 = jnp.float32
BF16 = jnp.bfloat16
I32 = jnp.int32

EPS = 1e-6
D_MODEL = 1024
PLE_DIM = 256
RET_HEADS = 4
RET_DK = 128
RET_DV = 128
RET_CHUNK = 128
SWA_HEADS = 8
SWA_KV_HEADS = 2
SWA_GROUP = SWA_HEADS // SWA_KV_HEADS
SWA_HD = 64
WINDOW = 128
N_GROUPS = 4
EXPERTS_PER_GROUP = 8
N_EXPERTS = N_GROUPS * EXPERTS_PER_GROUP
EXPERT_FF = 256

RET_W = RET_HEADS * RET_DK
SWA_Q = SWA_HEADS * SWA_HD
SWA_KV = SWA_KV_HEADS * SWA_HD
Z_RET_W = 4 * RET_W
Z_SKV_W = 2 * SWA_KV
Z_GATE_W = 2 * D_MODEL
COL_SQ = Z_RET_W
COL_SKV = COL_SQ + SWA_Q
COL_GATE = COL_SKV + Z_SKV_W
IN_WIDTH = COL_GATE + Z_GATE_W

ROUTE_LANES = 128
LANE_GROUP0 = N_EXPERTS
LANE_C1, LANE_C2 = 34, 35
LANE_SECOND0 = 64
NEG_INF = -1e30

HALF_D = D_MODEL // 2
ROW_TILE = 512
XS_SLOTS = 3
SC_CORES = 2
SC_SUBCORES = 16
SC_WORKERS = SC_CORES * SC_SUBCORES
SC_CHUNK = 128
SC_IDX_ROWS = 8

VMEM_LIMIT = 56 * 1024 * 1024

BLOCK_ROWS = 512
W_SLAB = 256
W_SLOTS = 6
RET_ROWS = 512
MERGE_SUB = 256
PLAN_STEPS = 8
PLE_TM = 1024
PLE_SLOTS = 3


def _rms(x, g):
    ms = jnp.mean(x * x, axis=-1, keepdims=True)
    return x * lax.rsqrt(ms + EPS) * g


def _sigmoid(x):
    return 1.0 / (1.0 + jnp.exp(-x))


def _dot(a, b):
    return jnp.dot(a, b, preferred_element_type=F32)


def _dot_nt(a, b):
    return lax.dot_general(a, b, (((1,), (1,)), ((), ())), preferred_element_type=F32)


def _dot_tn(a, b):
    return lax.dot_general(a, b, (((0,), (0,)), ((), ())), preferred_element_type=F32)


def _pack_halves(v):
    return pltpu.pack_elementwise([v[:, :HALF_D], v[:, HALF_D:]], packed_dtype=BF16)


def _unpack_halves(p):
    words = lax.bitcast_convert_type(p, jnp.uint32)
    return tuple(pltpu.unpack_elementwise(words, index=k, packed_dtype=BF16, unpacked_dtype=F32) for k in range(2))


def _retention_tables():
    f32 = np.float32
    h = RET_HEADS
    c = RET_CHUNK
    log_gamma = np.log1p(-np.exp2(f32(-5.0) - np.arange(h, dtype=f32))).astype(f32)
    pos = np.arange(c, dtype=f32)
    diff = pos[:, None] - pos[None, :]
    decay = np.where(diff[None] >= 0.0, np.exp(np.maximum(diff, f32(0.0))[None] * log_gamma[:, None, None]), f32(0.0))
    scale = f32(RET_DK ** -0.5)
    dmask = decay * scale
    zeta = np.exp((f32(c - 1.0) - pos)[None, :] * log_gamma[:, None]) * scale
    xi = np.exp((pos + f32(1.0))[None, :] * log_gamma[:, None])
    cdec = np.exp(f32(c) * log_gamma)
    bc = lambda a: np.ascontiguousarray(np.broadcast_to(a[:, :, None], (h, c, c)), dtype=f32)
    return (dmask.astype(f32), bc(zeta), bc(xi),
            np.ascontiguousarray(np.broadcast_to(cdec[:, None, None], (h, c, c)), dtype=f32))


def _swa_tables():
    f32 = np.float32
    qi = np.arange(WINDOW)[:, None]
    sj = np.arange(2 * WINDOW)[None, :]
    rel = qi + WINDOW - sj
    ok = (rel >= 0) & (rel < WINDOW)
    slopes = np.exp2(f32(-8.0) * np.arange(1, SWA_HEADS + 1, dtype=f32) / f32(SWA_HEADS)).astype(f32)
    bias = np.where(ok[None], -slopes[:, None, None] * rel.astype(f32)[None], f32(NEG_INF)).astype(f32)
    bias = bias.reshape(SWA_KV_HEADS, SWA_GROUP * WINDOW, 2 * WINDOW)
    avg = lambda heads: jnp.asarray(np.kron(np.eye(heads, dtype=f32), np.full((SWA_HD, SWA_HD), 1.0 / SWA_HD, f32)),
                                    dtype=BF16)
    return avg(SWA_HEADS), avg(SWA_KV_HEADS), jnp.asarray(bias)


def _block_kernel(per_seq, sinks_ref, x_ref, xp_ref, g_ref, w_hbm, dmask_ref, zeta_ref, xi_ref, cdec_ref,
                  gng_ref, gnb_ref, qgain_ref, kgain_ref, bq_ref, bk_ref, bias_ref,
                  wro_hbm, wso_hbm, wout_hbm, fg_ref, wre_ref, wrg_ref, bre_ref, brg_ref,
                  x1_ref, hp_ref, route_ref, cnt_ref,
                  w_ref, wro_ref, wso_ref, wout_ref, wtmp_ref, w_sem, qg_ref, kg_ref, wr_ref, br_ref,
                  zc_ref, zn_ref, yr_ref, ys_ref, state_ref, prev_ref, band_ref, tail_ref):
    g = pl.program_id(0)
    seq_start = lax.rem(g - 1, per_seq) == 0

    @pl.when(g == 0)
    def _():
        slabs = [(src, dst, c0) for src, dst in ((w_hbm, w_ref), (wro_hbm, wro_ref), (wso_hbm, wso_ref),
                                                 (wout_hbm, wout_ref))
                 for c0 in range(0, dst.shape[1], W_SLAB)]

        def slab(k):
            src, dst, c0 = slabs[k]
            return pltpu.make_async_copy(src.at[:, pl.ds(c0, W_SLAB)],
                                         wtmp_ref.at[k % W_SLOTS, pl.ds(0, dst.shape[0])], w_sem.at[k % W_SLOTS])

        for k in range(W_SLOTS - 1):
            slab(k).start()
        for k, (_, dst, c0) in enumerate(slabs):
            if k + W_SLOTS - 1 < len(slabs):
                slab(k + W_SLOTS - 1).start()
            slab(k).wait()
            dst[:, c0:c0 + W_SLAB] = wtmp_ref[k % W_SLOTS, :dst.shape[0]].astype(BF16)
        for hd in range(SWA_HEADS):
            qg_ref[:, hd * SWA_HD:(hd + 1) * SWA_HD] = qgain_ref[...] * (SWA_HD ** -0.5)
        for kh in range(SWA_KV_HEADS):
            kg_ref[:, kh * SWA_HD:(kh + 1) * SWA_HD] = kgain_ref[...]
        wr_ref[...] = jnp.zeros_like(wr_ref)
        wr_ref[:, :N_EXPERTS] = wre_ref[...].astype(BF16)
        wr_ref[:, LANE_GROUP0:LANE_GROUP0 + N_GROUPS] = wrg_ref[...].astype(BF16)
        br_ref[...] = jnp.zeros_like(br_ref)
        br_ref[:, :N_EXPERTS] = bre_ref[...]
        br_ref[:, LANE_GROUP0:LANE_GROUP0 + N_GROUPS] = brg_ref[...]
        zc_ref[...] = jnp.zeros_like(zc_ref)
        tail_ref[...] = jnp.zeros_like(tail_ref)
        state_ref[...] = jnp.zeros_like(state_ref)

    @pl.when(seq_start)
    def _():
        state_ref[...] = jnp.zeros_like(state_ref)

    band_ref[:WINDOW, :] = tail_ref[...]
    band_ref[WINDOW:, :] = zc_ref[:, COL_SKV:COL_GATE]

    h = _rms(x_ref[...], g_ref[...]).astype(BF16)

    def project(c0, width):
        zn_ref[:, c0:c0 + width] = _dot(h, w_ref[:, c0:c0 + width]).astype(BF16)

    cw = 512
    proj_pieces = [functools.partial(project, j * cw, cw) for j in range(COL_SKV // cw)]
    proj_pieces.append(functools.partial(project, COL_SKV, Z_SKV_W))
    proj_pieces += [functools.partial(project, COL_GATE + j * cw, cw) for j in range(Z_GATE_W // cw)]

    def retention_rows(t):
        r_base = t * RET_ROWS
        n_chunks = RET_ROWS // RET_CHUNK
        part = lambda c, p, hd: zc_ref[r_base + c * RET_CHUNK:r_base + (c + 1) * RET_CHUNK,
                                       p * RET_W + hd * RET_DK:p * RET_W + (hd + 1) * RET_DK]
        for hd in range(RET_HEADS):
            state = state_ref[hd]
            for c in range(n_chunks):
                prev_ref[t * n_chunks + c, hd] = state.astype(BF16)
                kz = (part(c, 1, hd).astype(F32) * zeta_ref[hd]).astype(BF16)
                state = cdec_ref[hd] * state + _dot_tn(kz, part(c, 2, hd))
            state_ref[hd] = state
        out_rows = []
        for c in range(n_chunks):
            heads = []
            for hd in range(RET_HEADS):
                q = part(c, 0, hd)
                scores = _dot_nt(q, part(c, 1, hd)) * dmask_ref[hd]
                y = _dot(scores.astype(BF16), part(c, 2, hd)) + _dot(q, prev_ref[t * n_chunks + c, hd]) * xi_ref[hd]
                mu = jnp.mean(y, axis=-1, keepdims=True)
                d = y - mu
                var = jnp.mean(d * d, axis=-1, keepdims=True)
                hs = slice(hd * RET_DV, (hd + 1) * RET_DV)
                yn = d * lax.rsqrt(var + EPS) * gng_ref[:, hs] + gnb_ref[:, hs]
                gate = part(c, 3, hd).astype(F32)
                heads.append((gate * _sigmoid(gate) * yn).astype(BF16))
            out_rows.append(jnp.concatenate(heads, axis=1))
        yr_ref[r_base:r_base + RET_ROWS, :] = jnp.concatenate(out_rows, axis=0)

    group_rows = SWA_GROUP * WINDOW
    lo_q = lax.broadcasted_iota(jnp.int32, (WINDOW, 2 * SWA_HD), 1) < SWA_HD
    lo_k = lax.broadcasted_iota(jnp.int32, (2 * WINDOW, 2 * SWA_HD), 1) < SWA_HD
    in_prev = lax.broadcasted_iota(jnp.int32, (group_rows, 2 * WINDOW), 1) < WINDOW
    row_head = lax.broadcasted_iota(jnp.int32, (group_rows, 1), 0) // WINDOW

    def both_heads(v):
        swapped = pltpu.roll(v, SWA_HD, axis=1)
        return jnp.where(lo_k, v, swapped), jnp.where(lo_k, swapped, v)

    def swa_block(n):
        rows = slice(n * WINDOW, (n + 1) * WINDOW)
        kv = band_ref[n * WINDOW:(n + 2) * WINDOW, :]
        kf = kv[:, :SWA_KV].astype(F32)
        kn = kf * lax.rsqrt(_dot((kf * kf).astype(BF16), bk_ref[...]) + EPS) * kg_ref[...]
        keys = [a.astype(BF16) for a in both_heads(kn)]
        vals = [a.astype(BF16) for a in both_heads(kv[:, SWA_KV:].astype(F32))]
        qf = zc_ref[rows, COL_SQ:COL_SKV].astype(F32)
        qn = qf * lax.rsqrt(_dot((qf * qf).astype(BF16), bq_ref[...]) + EPS) * qg_ref[...]
        outs = []
        for kh in range(SWA_KV_HEADS):
            parts = []
            for j in range(SWA_GROUP // 2):
                c0 = (kh * SWA_GROUP + 2 * j) * SWA_HD
                two = qn[:, c0:c0 + 2 * SWA_HD]
                parts += [jnp.where(lo_q, two, 0.0), jnp.where(lo_q, 0.0, two)]
            qs = jnp.concatenate(parts, axis=0).astype(BF16)
            s = _dot_nt(qs, keys[kh]) + bias_ref[kh]
            if n == 0:
                s = jnp.where(jnp.logical_and(seq_start, in_prev), NEG_INF, s)
            sink = sinks_ref[kh * SWA_GROUP]
            for gi in range(1, SWA_GROUP):
                sink = jnp.where(row_head == gi, sinks_ref[kh * SWA_GROUP + gi], sink)
            m = jnp.maximum(jnp.max(s, axis=-1, keepdims=True), sink)
            p = jnp.exp(s - m)
            denom = jnp.sum(p, axis=-1, keepdims=True) + jnp.exp(sink - m)
            o = _dot(p.astype(BF16), vals[kh]) * (1.0 / denom)
            for j in range(SWA_GROUP // 2):
                even = o[(2 * j) * WINDOW:(2 * j + 1) * WINDOW]
                odd = o[(2 * j + 1) * WINDOW:(2 * j + 2) * WINDOW]
                outs.append(jnp.where(lo_q, even, odd).astype(BF16))
        ys_ref[rows, :] = jnp.concatenate(outs, axis=1)

    mix_pieces = [functools.partial(retention_rows, t) for t in range(BLOCK_ROWS // RET_ROWS)]
    mix_pieces += [functools.partial(swa_block, n) for n in range(BLOCK_ROWS // WINDOW)]

    subs = []

    def merge_sub(s):
        subs.append(_merge_rows(slice(s * MERGE_SUB, (s + 1) * MERGE_SUB), yr_ref, ys_ref, zc_ref, xp_ref, wro_ref,
                                wso_ref, wout_ref, fg_ref, wr_ref, br_ref))

    other_pieces = mix_pieces + [functools.partial(merge_sub, s) for s in range(BLOCK_ROWS // MERGE_SUB)]

    n_other = len(other_pieces)
    for k, piece in enumerate(other_pieces):
        for proj_piece in proj_pieces[k * len(proj_pieces) // n_other:(k + 1) * len(proj_pieces) // n_other]:
            proj_piece()
        piece()
    x1_ref[...] = jnp.concatenate([s[0] for s in subs], axis=0)
    hp_ref[...] = jnp.concatenate([s[1] for s in subs], axis=0)
    route_ref[...] = jnp.concatenate([s[2] for s in subs], axis=0)
    for s, sub in enumerate(subs):
        cnt_ref[s] = sub[3]

    tail_ref[...] = zc_ref[BLOCK_ROWS - WINDOW:, COL_SKV:COL_GATE]
    zc_ref[...] = zn_ref[...]


def _merge_rows(rows, yr_ref, ys_ref, z_ref, x_ref, wro_ref, wso_ref, wout_ref, fg_ref, wr_ref, br_ref):
    a = _dot(yr_ref[rows, :], wro_ref[...])
    b = _dot(ys_ref[rows, :], wso_ref[...])
    gate_r = z_ref[rows, COL_GATE:COL_GATE + D_MODEL].astype(F32)
    gate_s = z_ref[rows, COL_GATE + D_MODEL:].astype(F32)
    merged = (_sigmoid(gate_r) * a + _sigmoid(gate_s) * b).astype(BF16)
    x1 = x_ref[rows, :] + _dot(merged, wout_ref[...])
    h2 = _rms(x1, fg_ref[...])
    packed = _pack_halves(h2)

    logits = _dot(h2.astype(BF16), wr_ref[...]) + br_ref[...]
    lane = lax.broadcasted_iota(jnp.int32, logits.shape, 1)
    big = jnp.int32(ROUTE_LANES)
    is_group = jnp.logical_and(lane >= LANE_GROUP0, lane < LANE_GROUP0 + N_GROUPS)
    gl = jnp.where(is_group, logits, NEG_INF)
    gmax = jnp.max(gl, axis=-1, keepdims=True)
    g_w = 1.0 / jnp.sum(jnp.exp(gl - gmax), axis=-1, keepdims=True)
    g_sel = jnp.min(jnp.where(gl == gmax, lane, big), axis=-1, keepdims=True) - LANE_GROUP0
    in_group = jnp.logical_and(lane < N_EXPERTS, (lane >> 3) == g_sel)
    el = jnp.where(in_group, logits, NEG_INF)
    m1 = jnp.max(el, axis=-1, keepdims=True)
    i1 = jnp.min(jnp.where(el == m1, lane, big), axis=-1, keepdims=True)
    el2 = jnp.where(lane == i1, NEG_INF, el)
    m2 = jnp.max(el2, axis=-1, keepdims=True)
    i2 = jnp.min(jnp.where(el2 == m2, lane, big), axis=-1, keepdims=True)
    e2 = jnp.exp(m2 - m1)
    c1 = g_w / (1.0 + e2)
    c2 = g_w * e2 / (1.0 + e2)
    chosen = jnp.where(jnp.logical_or(lane == i1, lane == i2), 1.0, 0.0)
    route = (chosen
             + jnp.where(lane == LANE_C1, c1, 0.0)
             + jnp.where(lane == LANE_C2, c2, 0.0)
             + jnp.where(lane == i2 + LANE_SECOND0, 1.0, 0.0))
    return x1, packed, route, jnp.sum(chosen, axis=0, keepdims=True)


def _block(x2d, mix_g, w_in, gn_g, gn_b, sinks, q_g, k_g, w_ret_o, w_swa_o, w_out, ffn_g, router, seq):
    t = x2d.shape[0]
    n = t // BLOCK_ROWS
    assert seq % BLOCK_ROWS == 0
    consts = _retention_tables() + (gn_g, gn_b, q_g, k_g) + _swa_tables()
    merge_weights = (w_ret_o, w_swa_o, w_out)
    small = (ffn_g,) + tuple(router)
    hbm = pl.BlockSpec(memory_space=pl.ANY)
    full = lambda a: pl.BlockSpec(a.shape, lambda g: (0,) * a.ndim)
    cur = lambda w: pl.BlockSpec((BLOCK_ROWS, w), lambda g: (jnp.minimum(g, n - 1), 0))
    prv = lambda w: pl.BlockSpec((BLOCK_ROWS, w), lambda g: (jnp.maximum(g - 1, 0), 0))
    chunks = BLOCK_ROWS // RET_CHUNK
    subs_per_block = BLOCK_ROWS // MERGE_SUB
    return pl.pallas_call(
        functools.partial(_block_kernel, seq // BLOCK_ROWS),
        grid=(n + 1,),
        in_specs=([pl.BlockSpec(memory_space=pltpu.SMEM), cur(D_MODEL), prv(D_MODEL), full(mix_g), hbm]
                  + [full(a) for a in consts] + [hbm] * len(merge_weights) + [full(a) for a in small]),
        out_specs=[prv(D_MODEL), prv(HALF_D), prv(ROUTE_LANES),
                   pl.BlockSpec((subs_per_block, 1, ROUTE_LANES), lambda g: (jnp.maximum(g - 1, 0), 0, 0))],
        out_shape=[jax.ShapeDtypeStruct((t, D_MODEL), F32),
                   jax.ShapeDtypeStruct((t, HALF_D), I32),
                   jax.ShapeDtypeStruct((t, ROUTE_LANES), F32),
                   jax.ShapeDtypeStruct((n * subs_per_block, 1, ROUTE_LANES), F32)],
        scratch_shapes=[pltpu.VMEM(w_in.shape, BF16)] + [pltpu.VMEM(w.shape, BF16) for w in merge_weights]
                       + [pltpu.VMEM((W_SLOTS, D_MODEL, W_SLAB), F32), pltpu.SemaphoreType.DMA((W_SLOTS,)),
                        pltpu.VMEM((1, SWA_Q), F32), pltpu.VMEM((1, SWA_KV), F32),
                        pltpu.VMEM((D_MODEL, ROUTE_LANES), BF16), pltpu.VMEM((1, ROUTE_LANES), F32),
                        pltpu.VMEM((BLOCK_ROWS, IN_WIDTH), BF16), pltpu.VMEM((BLOCK_ROWS, IN_WIDTH), BF16),
                        pltpu.VMEM((BLOCK_ROWS, RET_W), BF16), pltpu.VMEM((BLOCK_ROWS, SWA_Q), BF16),
                        pltpu.VMEM((RET_HEADS, RET_DK, RET_DV), F32),
                        pltpu.VMEM((chunks, RET_HEADS, RET_DK, RET_DV), BF16),
                        pltpu.VMEM((BLOCK_ROWS + WINDOW, Z_SKV_W), BF16), pltpu.VMEM((WINDOW, Z_SKV_W), BF16)],
        compiler_params=pltpu.CompilerParams(dimension_semantics=("arbitrary",),
                                             vmem_limit_bytes=VMEM_LIMIT),
        name="block",
    )(sinks, x2d, x2d, mix_g, w_in, *consts, *merge_weights, *small)


def _plan_tables(cnt_ref, base_ref, meta_ref):
    counts = cnt_ref[...]
    nt = counts.shape[0]
    exact = functools.partial(jnp.dot, precision=lax.Precision.HIGHEST, preferred_element_type=F32)
    total = jnp.sum(counts, axis=0, keepdims=True)
    tiles_e = jnp.ceil(total * (1.0 / ROW_TILE))
    k = lax.broadcasted_iota(jnp.int32, (ROUTE_LANES, ROUTE_LANES), 0)
    lane = lax.broadcasted_iota(jnp.int32, (ROUTE_LANES, ROUTE_LANES), 1)
    tile_end = exact(jnp.broadcast_to(tiles_e, (8, ROUTE_LANES)), jnp.where(k <= lane, 1.0, 0.0))[0:1]
    tile_start = tile_end - tiles_e
    r = lax.broadcasted_iota(jnp.int32, (nt, nt), 0)
    c = lax.broadcasted_iota(jnp.int32, (nt, nt), 1)
    before = exact(jnp.where(c < r, 1.0, 0.0), counts)
    base_ref[...] = tile_start * ROW_TILE + before

    used = jnp.max(tile_end, axis=-1, keepdims=True)
    tile = k.astype(F32)
    clamped = jnp.minimum(tile, jnp.maximum(used - 1.0, 0.0))
    is_expert = lane < N_EXPERTS
    owner = jnp.sum(jnp.where(jnp.logical_and(is_expert, tile_end <= clamped), 1.0, 0.0), axis=-1, keepdims=True)
    owner = jnp.minimum(owner, N_EXPERTS - 1.0)
    mine = lane.astype(F32) == owner
    pick = lambda v: jnp.sum(jnp.where(mine, v, 0.0), axis=-1, keepdims=True)
    t1 = tile[:, 0:1]
    valid = jnp.clip(pick(total) - (t1 - pick(tile_start)) * ROW_TILE, 0.0, float(ROW_TILE))
    valid = jnp.where(t1 < used, valid, 0.0)

    nonempty = jnp.where(jnp.logical_and(lane[0:1] < N_EXPERTS, tiles_e > 0.0), 1.0, 0.0)
    rank = exact(jnp.broadcast_to(nonempty, (8, ROUTE_LANES)), jnp.where(k < lane, 1.0, 0.0))[0:1]
    nonempty_col = jnp.sum(jnp.where(k == lane, jnp.broadcast_to(nonempty, k.shape), 0.0), axis=-1, keepdims=True)
    later = jnp.where(jnp.logical_and(k > lane, nonempty_col > 0.0), tile, float(ROUTE_LANES))
    nxt_e = jnp.min(later, axis=0, keepdims=True)
    nxt_e = jnp.where(nxt_e < float(ROUTE_LANES), nxt_e, -1.0)
    first = jnp.where(jnp.logical_and(t1 == pick(tile_start), t1 < used), 1.0, 0.0)
    my_rank = pick(rank)
    slot = my_rank - 2.0 * jnp.floor(my_rank * 0.5)
    fields = (owner, valid, first, pick(nxt_e), slot, jnp.broadcast_to(used, owner.shape))
    by_tile = sum(jnp.where(lane == n, f, 0.0) for n, f in enumerate(fields))
    meta_ref[...] = by_tile.T[:8, :].astype(I32)


def _plan_kernel(cnt_ref, route_ref, base_ref, meta_ref, pos1_ref, pos2_ref):
    step = pl.program_id(0)
    nt = cnt_ref.shape[0]
    subs_per_step = nt // PLAN_STEPS

    @pl.when(step == 0)
    def _():
        _plan_tables(cnt_ref, base_ref, meta_ref)
        pos1_ref[...] = jnp.zeros_like(pos1_ref)
        pos2_ref[...] = jnp.zeros_like(pos2_ref)

    sel_row = lax.broadcasted_iota(jnp.int32, (8, ROUTE_LANES), 0)
    sel_lane = lax.broadcasted_iota(jnp.int32, (8, ROUTE_LANES), 1)
    tok_r = lax.broadcasted_iota(jnp.int32, (MERGE_SUB, MERGE_SUB), 0)
    tok_c = lax.broadcasted_iota(jnp.int32, (MERGE_SUB, MERGE_SUB), 1)
    eye = jnp.where(tok_r == tok_c, 1.0, 0.0).astype(BF16)
    upto = jnp.where(tok_r <= tok_c, 1.0, 0.0).astype(BF16)
    sum_experts = lambda a: jnp.sum(a, axis=0, keepdims=True)

    def sub_positions(local, carry):
        s = step * subs_per_step + local
        route = route_ref[pl.ds(pl.multiple_of(local * MERGE_SUB, MERGE_SUB), MERGE_SUB), :].astype(BF16)
        record_t = _dot_tn(route, eye)
        incl_t = _dot_tn(route, upto)[:N_EXPERTS]
        both_t = record_t[:N_EXPERTS]
        second_t = record_t[LANE_SECOND0:LANE_SECOND0 + N_EXPERTS]
        rank2 = sum_experts(second_t * incl_t) - 1.0
        rank1 = sum_experts(both_t * incl_t) - rank2 - 2.0
        base_row = jnp.where(sel_lane < N_EXPERTS, jnp.broadcast_to(base_ref[pl.ds(s, 1), :], (8, ROUTE_LANES)), 0.0)
        hi = jnp.floor(base_row * (1.0 / 256.0))
        lo = base_row - 256.0 * hi
        hi2 = pltpu.roll(hi, LANE_SECOND0, axis=1)
        lo2 = pltpu.roll(lo, LANE_SECOND0, axis=1)
        table = (jnp.where(sel_row == 0, hi, 0.0) + jnp.where(sel_row == 1, lo, 0.0)
                 + jnp.where(sel_row == 2, hi2, 0.0) + jnp.where(sel_row == 3, lo2, 0.0))
        r = _dot_nt(table.astype(BF16), route)
        base_both = 256.0 * r[0:1] + r[1:2]
        base2 = 256.0 * r[2:3] + r[3:4]
        worker = s // subs_per_worker
        chunk0 = lax.rem(s, subs_per_worker) * chunks_per_sub
        for ref, row in ((pos1_ref, base_both - base2 + rank1), (pos2_ref, base2 + rank2)):
            for c in range(chunks_per_sub):
                ref[worker, pl.ds(chunk0 + c, 1), :] = row[:, c * SC_CHUNK:(c + 1) * SC_CHUNK].astype(I32)
        return carry

    chunks_per_sub = MERGE_SUB // SC_CHUNK
    subs_per_worker = nt // SC_WORKERS
    lax.fori_loop(0, subs_per_step, sub_positions, 0, unroll=8)


def _tile_plan(cnt, route, n_tiles):
    assert n_tiles <= ROUTE_LANES
    nt = cnt.shape[0]
    assert nt % SC_WORKERS == 0 and MERGE_SUB % SC_CHUNK == 0 and nt * MERGE_SUB <= SC_WORKERS * SC_IDX_ROWS * SC_CHUNK
    assert nt % PLAN_STEPS == 0
    pos = jax.ShapeDtypeStruct((SC_WORKERS, SC_IDX_ROWS, SC_CHUNK), I32)
    outs = [jax.ShapeDtypeStruct((nt, ROUTE_LANES), F32), jax.ShapeDtypeStruct((8, ROUTE_LANES), I32), pos, pos]
    whole = lambda a: pl.BlockSpec(a.shape, lambda i: (0,) * len(a.shape))
    _, meta, pos1, pos2 = pl.pallas_call(
        _plan_kernel,
        grid=(PLAN_STEPS,),
        in_specs=[pl.BlockSpec((nt, ROUTE_LANES), lambda i: (0, 0)),
                  pl.BlockSpec((route.shape[0] // PLAN_STEPS, ROUTE_LANES), lambda i: (i, 0))],
        out_specs=[whole(o) for o in outs],
        out_shape=outs,
        compiler_params=pltpu.CompilerParams(dimension_semantics=("arbitrary",), vmem_limit_bytes=VMEM_LIMIT),
        name="tile_plan",
    )(cnt.reshape(nt, ROUTE_LANES), route)
    return meta, pos1, pos2


def _sc_mesh():
    return plsc.VectorSubcoreMesh(core_axis_name="c", subcore_axis_name="s")


def _sc_worker():
    return lax.axis_index("s") * SC_CORES + lax.axis_index("c")


def _dispatch(hp, pos1, pos2, n_rows):
    t = hp.shape[0]
    per_worker = t // SC_WORKERS
    k = per_worker // SC_CHUNK
    idx = pltpu.VMEM((SC_IDX_ROWS, SC_CHUNK), I32)

    @functools.partial(pl.kernel, mesh=_sc_mesh(), out_type=jax.ShapeDtypeStruct((n_rows, HALF_D), I32),
                       scratch_types=[idx, idx, pltpu.VMEM((SC_CHUNK, HALF_D), I32)], name="moe_dispatch")
    def run(h_hbm, p1_hbm, p2_hbm, xs_hbm, p1_v, p2_v, rows_v):
        wid = _sc_worker()
        pltpu.sync_copy(p1_hbm.at[wid], p1_v)
        pltpu.sync_copy(p2_hbm.at[wid], p2_v)
        for j in range(k):
            pltpu.sync_copy(h_hbm.at[pl.ds(wid * per_worker + j * SC_CHUNK, SC_CHUNK)], rows_v)
            pltpu.sync_copy(rows_v, xs_hbm.at[p1_v.at[j]])
            pltpu.sync_copy(rows_v, xs_hbm.at[p2_v.at[j]])

    return run(hp, pos1, pos2)


def _combine_gather(ys, pos1, pos2, t):
    per_worker = t // SC_WORKERS
    k = per_worker // SC_CHUNK
    idx = pltpu.VMEM((SC_IDX_ROWS, SC_CHUNK), I32)
    out = jax.ShapeDtypeStruct((t, HALF_D), I32)

    @functools.partial(pl.kernel, mesh=_sc_mesh(), out_type=(out, out),
                       scratch_types=[idx, idx, pltpu.VMEM((SC_CHUNK, HALF_D), I32)], name="moe_combine")
    def run(ys_hbm, p1_hbm, p2_hbm, g1_hbm, g2_hbm, p1_v, p2_v, rows_v):
        wid = _sc_worker()
        pltpu.sync_copy(p1_hbm.at[wid], p1_v)
        pltpu.sync_copy(p2_hbm.at[wid], p2_v)
        for j in range(k):
            dst = pl.ds(wid * per_worker + j * SC_CHUNK, SC_CHUNK)
            pltpu.sync_copy(ys_hbm.at[p1_v.at[j]], rows_v)
            pltpu.sync_copy(rows_v, g1_hbm.at[dst])
            pltpu.sync_copy(ys_hbm.at[p2_v.at[j]], rows_v)
            pltpu.sync_copy(rows_v, g2_hbm.at[dst])

    return run(ys, pos1, pos2)


META_OWNER, META_VALID, META_FIRST, META_NEXT, META_SLOT, META_USED = range(6)


def _experts_kernel(meta_ref, xs_hbm, wg_hbm, wu_hbm, wd_hbm, ys_ref,
                    xs_buf, xs_sem, wg_f, wu_f, wd_f, wg_bf, wu_bf, wd_bf, w_sem):
    i = pl.program_id(0)
    used = meta_ref[META_USED, 0]
    valid = meta_ref[META_VALID, i]
    owner = meta_ref[META_OWNER, i]
    nxt = meta_ref[META_NEXT, i]

    def rows_in(tile):
        start = pl.multiple_of(tile * ROW_TILE, ROW_TILE)
        slot = lax.rem(tile, XS_SLOTS)
        return pltpu.make_async_copy(xs_hbm.at[pl.ds(start, ROW_TILE)], xs_buf.at[slot], xs_sem.at[slot])

    @pl.when(i == 0)
    def _():
        for ahead in range(XS_SLOTS - 1):
            pl.when(ahead < used)(rows_in(ahead).start)

    @pl.when(i + (XS_SLOTS - 1) < used)
    def _():
        rows_in(i + (XS_SLOTS - 1)).start()

    def fetch(expert, slot):
        pairs = ((wg_hbm, wg_f), (wu_hbm, wu_f), (wd_hbm, wd_f))
        return [pltpu.make_async_copy(w.at[expert], buf.at[slot], w_sem.at[slot, n])
                for n, (w, buf) in enumerate(pairs)]

    @pl.when(i == 0)
    def _():
        for copy in fetch(owner, 0):
            copy.start()

    @pl.when(meta_ref[META_FIRST, i] == 1)
    def _():
        slot = meta_ref[META_SLOT, i]
        for copy in fetch(owner, slot):
            copy.wait()
        wg_bf[...] = wg_f[slot].astype(BF16)
        wu_bf[...] = wu_f[slot].astype(BF16)
        wd_bf[...] = wd_f[slot].astype(BF16)

        @pl.when(nxt >= 0)
        def _():
            for copy in fetch(nxt, 1 - slot):
                copy.start()

    @pl.when(i < used)
    def _():
        rows_in(i).wait()
        row = lax.broadcasted_iota(jnp.int32, (ROW_TILE, HALF_D), 0)
        words = xs_buf[lax.rem(i, XS_SLOTS)]
        packed = jnp.where(row < valid, words, 0)
        hi, lo = _unpack_halves(packed)
        hi = hi.astype(BF16)
        lo = lo.astype(BF16)
        hg = _dot(hi, wg_bf[:HALF_D, :]) + _dot(lo, wg_bf[HALF_D:, :])
        hu = _dot(hi, wu_bf[:HALF_D, :]) + _dot(lo, wu_bf[HALF_D:, :])
        hid = (hg * _sigmoid(hg) * hu).astype(BF16)
        ys_ref[...] = _pack_halves(_dot(hid, wd_bf[...]))


def _experts(xs, meta, n_tiles, wg, wu, wd):
    tile = lambda i, meta: (jnp.minimum(i, meta[META_USED, 0] - 1), 0)
    hbm = pl.BlockSpec(memory_space=pl.ANY)
    up_shape, down_shape = (D_MODEL, EXPERT_FF), (EXPERT_FF, D_MODEL)
    grid_spec = pltpu.PrefetchScalarGridSpec(
        num_scalar_prefetch=1,
        grid=(n_tiles,),
        in_specs=[hbm, hbm, hbm, hbm],
        out_specs=pl.BlockSpec((ROW_TILE, HALF_D), tile),
        scratch_shapes=[pltpu.VMEM((XS_SLOTS, ROW_TILE, HALF_D), I32), pltpu.SemaphoreType.DMA((XS_SLOTS,)),
                        pltpu.VMEM((2,) + up_shape, F32), pltpu.VMEM((2,) + up_shape, F32),
                        pltpu.VMEM((2,) + down_shape, F32),
                        pltpu.VMEM(up_shape, BF16), pltpu.VMEM(up_shape, BF16), pltpu.VMEM(down_shape, BF16),
                        pltpu.SemaphoreType.DMA((2, 3))],
    )
    return pl.pallas_call(
        _experts_kernel,
        grid_spec=grid_spec,
        out_shape=jax.ShapeDtypeStruct((n_tiles * ROW_TILE, HALF_D), I32),
        compiler_params=pltpu.CompilerParams(dimension_semantics=("arbitrary",),
                                             vmem_limit_bytes=VMEM_LIMIT),
        name="experts",
    )(meta, xs, wg, wu, wd)


def _ple_kernel(x1_hbm, g1_hbm, g2_hbm, route_ref, p_ref, wp_f32, pg_ref, gg_ref, wgate_f32, o_ref,
                wp_ref, wgate_ref, x1_buf, g1_buf, g2_buf, ring_sem):
    i = pl.program_id(0)
    steps = pl.num_programs(0)
    streams = ((x1_hbm, x1_buf), (g1_hbm, g1_buf), (g2_hbm, g2_buf))

    def rows_in(step):
        start = pl.multiple_of(step * PLE_TM, PLE_TM)
        slot = lax.rem(step, PLE_SLOTS)
        return [pltpu.make_async_copy(src.at[pl.ds(start, PLE_TM)], buf.at[slot], ring_sem.at[n, slot])
                for n, (src, buf) in enumerate(streams)]

    def start_rows(step):
        for copy in rows_in(step):
            copy.start()

    @pl.when(i == 0)
    def _():
        for ahead in range(PLE_SLOTS - 1):
            pl.when(ahead < steps)(functools.partial(start_rows, ahead))
        wp_ref[...] = wp_f32[...].astype(BF16)
        wgate_ref[...] = wgate_f32[...].astype(BF16)

    pl.when(i + (PLE_SLOTS - 1) < steps)(functools.partial(start_rows, i + (PLE_SLOTS - 1)))

    for copy in rows_in(i):
        copy.wait()
    slot = lax.rem(i, PLE_SLOTS)
    x1_ref, g1_ref, g2_ref = x1_buf.at[slot], g1_buf.at[slot], g2_buf.at[slot]

    route = route_ref[...]
    lane = lax.broadcasted_iota(jnp.int32, route.shape, 1)
    pick = lambda ln: jnp.sum(jnp.where(lane == ln, route, 0.0), axis=-1, keepdims=True)
    c1 = pick(LANE_C1)
    c2 = pick(LANE_C2)
    a_hi, a_lo = _unpack_halves(g1_ref[...])
    b_hi, b_lo = _unpack_halves(g2_ref[...])
    x = x1_ref[...] + jnp.concatenate([c1 * a_hi + c2 * b_hi, c1 * a_lo + c2 * b_lo], axis=1)
    ple = _rms(_dot(p_ref[...].astype(BF16), wp_ref[...]), pg_ref[...])
    gate = _sigmoid(_dot(_rms(x, gg_ref[...]).astype(BF16), wgate_ref[...]))
    o_ref[...] = x + gate * ple


def _ple(x1, g1, g2, route, p2d, w_ple, ple_g, gate_g, w_gate):
    t = x1.shape[0]
    row = lambda w: pl.BlockSpec((PLE_TM, w), lambda i: (i, 0))
    full = lambda a: pl.BlockSpec(a.shape, lambda i: (0,) * a.ndim)
    hbm = pl.BlockSpec(memory_space=pl.ANY)
    return pl.pallas_call(
        _ple_kernel,
        grid=(t // PLE_TM,),
        in_specs=[hbm, hbm, hbm, row(ROUTE_LANES), row(PLE_DIM),
                  full(w_ple), full(ple_g), full(gate_g), full(w_gate)],
        out_specs=row(D_MODEL),
        out_shape=jax.ShapeDtypeStruct((t, D_MODEL), F32),
        scratch_shapes=[pltpu.VMEM(w_ple.shape, BF16), pltpu.VMEM(w_gate.shape, BF16),
                        pltpu.VMEM((PLE_SLOTS, PLE_TM, D_MODEL), F32),
                        pltpu.VMEM((PLE_SLOTS, PLE_TM, HALF_D), I32), pltpu.VMEM((PLE_SLOTS, PLE_TM, HALF_D), I32),
                        pltpu.SemaphoreType.DMA((3, PLE_SLOTS))],
        compiler_params=pltpu.CompilerParams(dimension_semantics=("arbitrary",),
                                             vmem_limit_bytes=VMEM_LIMIT),
        name="ple",
    )(x1, g1, g2, route, p2d, w_ple, ple_g, gate_g, w_gate)


def kernel(x, p, mix_norm_g, w_in, ret_gn_g, ret_gn_b, w_ret_o, q_norm_g, k_norm_g, attn_sinks,
           w_swa_o, w_out, ffn_norm_g, w_router_group, b_router_group, w_router_expert,
           b_router_expert, w_exp_gate, w_exp_up, w_exp_down, ple_gate_norm_g, w_ple_gate,
           w_ple, ple_norm_g):
    batch, seq, d = x.shape
    t = batch * seq
    depth = w_in.shape[0]
    n_tiles = 2 * t // ROW_TILE + N_EXPERTS
    row = lambda a: a.reshape(1, -1)
    x2d = x.reshape(t, d)
    for i in range(depth):
        router = (w_router_expert[i], w_router_group[i], row(b_router_expert[i]), row(b_router_group[i]))
        x1, hp, route, cnt = _block(x2d, row(mix_norm_g[i]), w_in[i], row(ret_gn_g[i]),
                                    row(ret_gn_b[i]), attn_sinks[i], row(q_norm_g[i]), row(k_norm_g[i]),
                                    w_ret_o[i], w_swa_o[i], w_out[i], row(ffn_norm_g[i]), router, seq)
        meta, pos1, pos2 = _tile_plan(cnt, route, n_tiles)
        xs = _dispatch(hp, pos1, pos2, n_tiles * ROW_TILE)
        ys = _experts(xs, meta, n_tiles, w_exp_gate[i], w_exp_up[i], w_exp_down[i])
        g1, g2 = _combine_gather(ys, pos1, pos2, t)
        x2d = _ple(x1, g1, g2, route, p[i].reshape(t, PLE_DIM), w_ple[i], row(ple_norm_g[i]),
                   row(ple_gate_norm_g[i]), w_ple_gate[i])
    return x2d.reshape(batch, seq, d)
```

```python
import functools

import jax
import jax.numpy as jnp
import numpy as np
from jax import lax
from jax.experimental import pallas as pl
from jax.experimental.pallas import tpu as pltpu
from jax.experimental.pallas import tpu_sc as plsc

F32 = jnp.float32
BF16 = jnp.bfloat16
I32 = jnp.int32

EPS = 1e-6
D_MODEL = 1024
PLE_DIM = 256
RET_HEADS = 4
RET_DK = 128
RET_DV = 128
RET_CHUNK = 128
SWA_HEADS = 8
SWA_KV_HEADS = 2
SWA_GROUP = SWA_HEADS // SWA_KV_HEADS
SWA_HD = 64
WINDOW = 128
N_GROUPS = 4
EXPERTS_PER_GROUP = 8
N_EXPERTS = N_GROUPS * EXPERTS_PER_GROUP
EXPERT_FF = 256

RET_W = RET_HEADS * RET_DK
SWA_Q = SWA_HEADS * SWA_HD
SWA_KV = SWA_KV_HEADS * SWA_HD
Z_RET_W = 4 * RET_W
Z_SKV_W = 2 * SWA_KV
Z_GATE_W = 2 * D_MODEL
COL_SQ = Z_RET_W
COL_SKV = COL_SQ + SWA_Q
COL_GATE = COL_SKV + Z_SKV_W
IN_WIDTH = COL_GATE + Z_GATE_W

ROUTE_LANES = 128
LANE_GROUP0 = N_EXPERTS
LANE_C1, LANE_C2 = 34, 35
LANE_SECOND0 = 64
NEG_INF = -1e30

HALF_D = D_MODEL // 2
ROW_TILE = 512
XS_SLOTS = 3
SC_CORES = 2
SC_SUBCORES = 16
SC_WORKERS = SC_CORES * SC_SUBCORES
SC_CHUNK = 128
SC_IDX_ROWS = 8

VMEM_LIMIT = 56 * 1024 * 1024

BLOCK_ROWS = 512
W_SLAB = 256
W_SLOTS = 6
RET_ROWS = 512
MERGE_SUB = 256
W_DMA_PRIORITY = 1
PLAN_STEPS = 4
PLE_TM = 1024
PLE_SLOTS = 3


def _rms(x, g):
    ms = jnp.mean(x * x, axis=-1, keepdims=True)
    return x * lax.rsqrt(ms + EPS) * g


def _sigmoid(x):
    return 1.0 / (1.0 + jnp.exp(-x))


def _dot(a, b):
    return jnp.dot(a, b, preferred_element_type=F32)


def _dot_nt(a, b):
    return lax.dot_general(a, b, (((1,), (1,)), ((), ())), preferred_element_type=F32)


def _dot_tn(a, b):
    return lax.dot_general(a, b, (((0,), (0,)), ((), ())), preferred_element_type=F32)


def _pack_halves(v):
    return pltpu.pack_elementwise([v[:, :HALF_D], v[:, HALF_D:]], packed_dtype=BF16)


def _unpack_halves(p):
    words = lax.bitcast_convert_type(p, jnp.uint32)
    return tuple(pltpu.unpack_elementwise(words, index=k, packed_dtype=BF16, unpacked_dtype=F32) for k in range(2))


def _retention_tables():
    f32 = np.float32
    h = RET_HEADS
    c = RET_CHUNK
    log_gamma = np.log1p(-np.exp2(f32(-5.0) - np.arange(h, dtype=f32))).astype(f32)
    pos = np.arange(c, dtype=f32)
    diff = pos[:, None] - pos[None, :]
    decay = np.where(diff[None] >= 0.0, np.exp(np.maximum(diff, f32(0.0))[None] * log_gamma[:, None, None]), f32(0.0))
    scale = f32(RET_DK ** -0.5)
    dmask = decay * scale
    zeta = np.exp((f32(c - 1.0) - pos)[None, :] * log_gamma[:, None]) * scale
    xi = np.exp((pos + f32(1.0))[None, :] * log_gamma[:, None])
    cdec = np.exp(f32(c) * log_gamma)
    bc = lambda a: np.ascontiguousarray(np.broadcast_to(a[:, :, None], (h, c, c)), dtype=f32)
    return (dmask.astype(f32), bc(zeta), bc(xi),
            np.ascontiguousarray(np.broadcast_to(cdec[:, None, None], (h, c, c)), dtype=f32))


def _swa_tables():
    f32 = np.float32
    qi = np.arange(WINDOW)[:, None]
    sj = np.arange(2 * WINDOW)[None, :]
    rel = qi + WINDOW - sj
    ok = (rel >= 0) & (rel < WINDOW)
    slopes = np.exp2(f32(-8.0) * np.arange(1, SWA_HEADS + 1, dtype=f32) / f32(SWA_HEADS)).astype(f32)
    bias = np.where(ok[None], -slopes[:, None, None] * rel.astype(f32)[None], f32(NEG_INF)).astype(f32)
    bias = bias.reshape(SWA_KV_HEADS, SWA_GROUP * WINDOW, 2 * WINDOW)
    avg = lambda heads: jnp.asarray(np.kron(np.eye(heads, dtype=f32), np.full((SWA_HD, SWA_HD), 1.0 / SWA_HD, f32)),
                                    dtype=BF16)
    return avg(SWA_HEADS), avg(SWA_KV_HEADS), jnp.asarray(bias)


def _block_kernel(per_seq, sinks_ref, x_ref, xp_ref, g_ref, w_hbm, dmask_ref, zeta_ref, xi_ref, cdec_ref,
                  gng_ref, gnb_ref, qgain_ref, kgain_ref, bq_ref, bk_ref, bias_ref,
                  wro_hbm, wso_hbm, wout_hbm, fg_ref, wre_ref, wrg_ref, bre_ref, brg_ref,
                  x1_ref, hp_ref, route_ref, cnt_ref,
                  w_ref, wro_ref, wso_ref, wout_ref, wtmp_ref, w_sem, qg_ref, kg_ref, wr_ref, br_ref,
                  zc_ref, zn_ref, yr_ref, ys_ref, state_ref, prev_ref, band_ref, tail_ref):
    g = pl.program_id(0)
    seq_start = lax.rem(g - 1, per_seq) == 0

    @pl.when(g == 0)
    def _():
        slabs = [(src, dst, c0) for src, dst in ((w_hbm, w_ref), (wro_hbm, wro_ref), (wso_hbm, wso_ref),
                                                 (wout_hbm, wout_ref))
                 for c0 in range(0, dst.shape[1], W_SLAB)]

        def slab(k):
            src, dst, c0 = slabs[k]
            return pltpu.make_async_copy(src.at[:, pl.ds(c0, W_SLAB)],
                                         wtmp_ref.at[k % W_SLOTS, pl.ds(0, dst.shape[0])], w_sem.at[k % W_SLOTS])

        for k in range(W_SLOTS - 1):
            slab(k).start()
        for k, (_, dst, c0) in enumerate(slabs):
            if k + W_SLOTS - 1 < len(slabs):
                slab(k + W_SLOTS - 1).start()
            slab(k).wait()
            dst[:, c0:c0 + W_SLAB] = wtmp_ref[k % W_SLOTS, :dst.shape[0]].astype(BF16)
        for hd in range(SWA_HEADS):
            qg_ref[:, hd * SWA_HD:(hd + 1) * SWA_HD] = qgain_ref[...] * (SWA_HD ** -0.5)
        for kh in range(SWA_KV_HEADS):
            kg_ref[:, kh * SWA_HD:(kh + 1) * SWA_HD] = kgain_ref[...]
        wr_ref[...] = jnp.zeros_like(wr_ref)
        wr_ref[:, :N_EXPERTS] = wre_ref[...].astype(BF16)
        wr_ref[:, LANE_GROUP0:LANE_GROUP0 + N_GROUPS] = wrg_ref[...].astype(BF16)
        br_ref[...] = jnp.zeros_like(br_ref)
        br_ref[:, :N_EXPERTS] = bre_ref[...]
        br_ref[:, LANE_GROUP0:LANE_GROUP0 + N_GROUPS] = brg_ref[...]
        zc_ref[...] = jnp.zeros_like(zc_ref)
        tail_ref[...] = jnp.zeros_like(tail_ref)
        state_ref[...] = jnp.zeros_like(state_ref)

    @pl.when(seq_start)
    def _():
        state_ref[...] = jnp.zeros_like(state_ref)

    band_ref[:WINDOW, :] = tail_ref[...]
    band_ref[WINDOW:, :] = zc_ref[:, COL_SKV:COL_GATE]

    h = _rms(x_ref[...], g_ref[...]).astype(BF16)

    def project(c0, width):
        zn_ref[:, c0:c0 + width] = _dot(h, w_ref[:, c0:c0 + width]).astype(BF16)

    cw = 512
    proj_pieces = [functools.partial(project, j * cw, cw) for j in range(COL_SKV // cw)]
    proj_pieces.append(functools.partial(project, COL_SKV, Z_SKV_W))
    proj_pieces += [functools.partial(project, COL_GATE + j * cw, cw) for j in range(Z_GATE_W // cw)]

    def retention_rows(t):
        r_base = t * RET_ROWS
        n_chunks = RET_ROWS // RET_CHUNK
        part = lambda c, p, hd: zc_ref[r_base + c * RET_CHUNK:r_base + (c + 1) * RET_CHUNK,
                                       p * RET_W + hd * RET_DK:p * RET_W + (hd + 1) * RET_DK]
        for hd in range(RET_HEADS):
            state = state_ref[hd]
            for c in range(n_chunks):
                prev_ref[t * n_chunks + c, hd] = state.astype(BF16)
                kz = (part(c, 1, hd).astype(F32) * zeta_ref[hd]).astype(BF16)
                state = cdec_ref[hd] * state + _dot_tn(kz, part(c, 2, hd))
            state_ref[hd] = state
        out_rows = []
        for c in range(n_chunks):
            heads = []
            for hd in range(RET_HEADS):
                q = part(c, 0, hd)
                scores = _dot_nt(q, part(c, 1, hd)) * dmask_ref[hd]
                y = _dot(scores.astype(BF16), part(c, 2, hd)) + _dot(q, prev_ref[t * n_chunks + c, hd]) * xi_ref[hd]
                mu = jnp.mean(y, axis=-1, keepdims=True)
                d = y - mu
                var = jnp.mean(d * d, axis=-1, keepdims=True)
                hs = slice(hd * RET_DV, (hd + 1) * RET_DV)
                yn = d * lax.rsqrt(var + EPS) * gng_ref[:, hs] + gnb_ref[:, hs]
                gate = part(c, 3, hd).astype(F32)
                heads.append((gate * _sigmoid(gate) * yn).astype(BF16))
            out_rows.append(jnp.concatenate(heads, axis=1))
        yr_ref[r_base:r_base + RET_ROWS, :] = jnp.concatenate(out_rows, axis=0)

    group_rows = SWA_GROUP * WINDOW
    lo_q = lax.broadcasted_iota(jnp.int32, (WINDOW, 2 * SWA_HD), 1) < SWA_HD
    lo_k = lax.broadcasted_iota(jnp.int32, (2 * WINDOW, 2 * SWA_HD), 1) < SWA_HD
    in_prev = lax.broadcasted_iota(jnp.int32, (group_rows, 2 * WINDOW), 1) < WINDOW
    row_head = lax.broadcasted_iota(jnp.int32, (group_rows, 1), 0) // WINDOW

    def both_heads(v):
        swapped = pltpu.roll(v, SWA_HD, axis=1)
        return jnp.where(lo_k, v, swapped), jnp.where(lo_k, swapped, v)

    def swa_block(n):
        rows = slice(n * WINDOW, (n + 1) * WINDOW)
        kv = band_ref[n * WINDOW:(n + 2) * WINDOW, :]
        kf = kv[:, :SWA_KV].astype(F32)
        kn = kf * lax.rsqrt(_dot((kf * kf).astype(BF16), bk_ref[...]) + EPS) * kg_ref[...]
        keys = [a.astype(BF16) for a in both_heads(kn)]
        vals = [a.astype(BF16) for a in both_heads(kv[:, SWA_KV:].astype(F32))]
        qf = zc_ref[rows, COL_SQ:COL_SKV].astype(F32)
        qn = qf * lax.rsqrt(_dot((qf * qf).astype(BF16), bq_ref[...]) + EPS) * qg_ref[...]
        outs = []
        for kh in range(SWA_KV_HEADS):
            parts = []
            for j in range(SWA_GROUP // 2):
                c0 = (kh * SWA_GROUP + 2 * j) * SWA_HD
                two = qn[:, c0:c0 + 2 * SWA_HD]
                parts += [jnp.where(lo_q, two, 0.0), jnp.where(lo_q, 0.0, two)]
            qs = jnp.concatenate(parts, axis=0).astype(BF16)
            s = _dot_nt(qs, keys[kh]) + bias_ref[kh]
            if n == 0:
                s = jnp.where(jnp.logical_and(seq_start, in_prev), NEG_INF, s)
            sink = sinks_ref[kh * SWA_GROUP]
            for gi in range(1, SWA_GROUP):
                sink = jnp.where(row_head == gi, sinks_ref[kh * SWA_GROUP + gi], sink)
            m = jnp.maximum(jnp.max(s, axis=-1, keepdims=True), sink)
            p = jnp.exp(s - m)
            denom = jnp.sum(p, axis=-1, keepdims=True) + jnp.exp(sink - m)
            o = _dot(p.astype(BF16), vals[kh]) * (1.0 / denom)
            for j in range(SWA_GROUP // 2):
                even = o[(2 * j) * WINDOW:(2 * j + 1) * WINDOW]
                odd = o[(2 * j + 1) * WINDOW:(2 * j + 2) * WINDOW]
                outs.append(jnp.where(lo_q, even, odd).astype(BF16))
        ys_ref[rows, :] = jnp.concatenate(outs, axis=1)

    mix_pieces = [functools.partial(retention_rows, t) for t in range(BLOCK_ROWS // RET_ROWS)]
    mix_pieces += [functools.partial(swa_block, n) for n in range(BLOCK_ROWS // WINDOW)]

    subs = []

    def merge_sub(s):
        subs.append(_merge_rows(slice(s * MERGE_SUB, (s + 1) * MERGE_SUB), yr_ref, ys_ref, zc_ref, xp_ref, wro_ref,
                                wso_ref, wout_ref, fg_ref, wr_ref, br_ref))

    other_pieces = mix_pieces + [functools.partial(merge_sub, s) for s in range(BLOCK_ROWS // MERGE_SUB)]

    n_other = len(other_pieces)
    for k, piece in enumerate(other_pieces):
        for proj_piece in proj_pieces[k * len(proj_pieces) // n_other:(k + 1) * len(proj_pieces) // n_other]:
            proj_piece()
        piece()
    x1_ref[...] = jnp.concatenate([s[0] for s in subs], axis=0)
    hp_ref[...] = jnp.concatenate([s[1] for s in subs], axis=0)
    route_ref[...] = jnp.concatenate([s[2] for s in subs], axis=0)
    for s, sub in enumerate(subs):
        cnt_ref[s] = sub[3]

    tail_ref[...] = zc_ref[BLOCK_ROWS - WINDOW:, COL_SKV:COL_GATE]
    zc_ref[...] = zn_ref[...]


def _merge_rows(rows, yr_ref, ys_ref, z_ref, x_ref, wro_ref, wso_ref, wout_ref, fg_ref, wr_ref, br_ref):
    a = _dot(yr_ref[rows, :], wro_ref[...])
    b = _dot(ys_ref[rows, :], wso_ref[...])
    gate_r = z_ref[rows, COL_GATE:COL_GATE + D_MODEL].astype(F32)
    gate_s = z_ref[rows, COL_GATE + D_MODEL:].astype(F32)
    merged = (_sigmoid(gate_r) * a + _sigmoid(gate_s) * b).astype(BF16)
    x1 = x_ref[rows, :] + _dot(merged, wout_ref[...])
    h2 = _rms(x1, fg_ref[...])
    packed = _pack_halves(h2)

    logits = _dot(h2.astype(BF16), wr_ref[...]) + br_ref[...]
    lane = lax.broadcasted_iota(jnp.int32, logits.shape, 1)
    big = jnp.int32(ROUTE_LANES)
    is_group = jnp.logical_and(lane >= LANE_GROUP0, lane < LANE_GROUP0 + N_GROUPS)
    gl = jnp.where(is_group, logits, NEG_INF)
    gmax = jnp.max(gl, axis=-1, keepdims=True)
    g_w = 1.0 / jnp.sum(jnp.exp(gl - gmax), axis=-1, keepdims=True)
    g_sel = jnp.min(jnp.where(gl == gmax, lane, big), axis=-1, keepdims=True) - LANE_GROUP0
    in_group = jnp.logical_and(lane < N_EXPERTS, (lane >> 3) == g_sel)
    el = jnp.where(in_group, logits, NEG_INF)
    m1 = jnp.max(el, axis=-1, keepdims=True)
    i1 = jnp.min(jnp.where(el == m1, lane, big), axis=-1, keepdims=True)
    el2 = jnp.where(lane == i1, NEG_INF, el)
    m2 = jnp.max(el2, axis=-1, keepdims=True)
    i2 = jnp.min(jnp.where(el2 == m2, lane, big), axis=-1, keepdims=True)
    e2 = jnp.exp(m2 - m1)
    c1 = g_w / (1.0 + e2)
    c2 = g_w * e2 / (1.0 + e2)
    chosen = jnp.where(jnp.logical_or(lane == i1, lane == i2), 1.0, 0.0)
    route = (chosen
             + jnp.where(lane == LANE_C1, c1, 0.0)
             + jnp.where(lane == LANE_C2, c2, 0.0)
             + jnp.where(lane == i2 + LANE_SECOND0, 1.0, 0.0))
    return x1, packed, route, jnp.sum(chosen, axis=0, keepdims=True)


def _block(x2d, mix_g, w_in, gn_g, gn_b, sinks, q_g, k_g, w_ret_o, w_swa_o, w_out, ffn_g, router, seq):
    t = x2d.shape[0]
    n = t // BLOCK_ROWS
    assert seq % BLOCK_ROWS == 0
    consts = _retention_tables() + (gn_g, gn_b, q_g, k_g) + _swa_tables()
    merge_weights = (w_ret_o, w_swa_o, w_out)
    small = (ffn_g,) + tuple(router)
    hbm = pl.BlockSpec(memory_space=pl.ANY)
    full = lambda a: pl.BlockSpec(a.shape, lambda g: (0,) * a.ndim)
    cur = lambda w: pl.BlockSpec((BLOCK_ROWS, w), lambda g: (jnp.minimum(g, n - 1), 0))
    prv = lambda w: pl.BlockSpec((BLOCK_ROWS, w), lambda g: (jnp.maximum(g - 1, 0), 0))
    chunks = BLOCK_ROWS // RET_CHUNK
    subs_per_block = BLOCK_ROWS // MERGE_SUB
    return pl.pallas_call(
        functools.partial(_block_kernel, seq // BLOCK_ROWS),
        grid=(n + 1,),
        in_specs=([pl.BlockSpec(memory_space=pltpu.SMEM), cur(D_MODEL), prv(D_MODEL), full(mix_g), hbm]
                  + [full(a) for a in consts] + [hbm] * len(merge_weights) + [full(a) for a in small]),
        out_specs=[prv(D_MODEL), prv(HALF_D), prv(ROUTE_LANES),
                   pl.BlockSpec((subs_per_block, 1, ROUTE_LANES), lambda g: (jnp.maximum(g - 1, 0), 0, 0))],
        out_shape=[jax.ShapeDtypeStruct((t, D_MODEL), F32),
                   jax.ShapeDtypeStruct((t, HALF_D), I32),
                   jax.ShapeDtypeStruct((t, ROUTE_LANES), F32),
                   jax.ShapeDtypeStruct((n * subs_per_block, 1, ROUTE_LANES), F32)],
        scratch_shapes=[pltpu.VMEM(w_in.shape, BF16)] + [pltpu.VMEM(w.shape, BF16) for w in merge_weights]
                       + [pltpu.VMEM((W_SLOTS, D_MODEL, W_SLAB), F32), pltpu.SemaphoreType.DMA((W_SLOTS,)),
                        pltpu.VMEM((1, SWA_Q), F32), pltpu.VMEM((1, SWA_KV), F32),
                        pltpu.VMEM((D_MODEL, ROUTE_LANES), BF16), pltpu.VMEM((1, ROUTE_LANES), F32),
                        pltpu.VMEM((BLOCK_ROWS, IN_WIDTH), BF16), pltpu.VMEM((BLOCK_ROWS, IN_WIDTH), BF16),
                        pltpu.VMEM((BLOCK_ROWS, RET_W), BF16), pltpu.VMEM((BLOCK_ROWS, SWA_Q), BF16),
                        pltpu.VMEM((RET_HEADS, RET_DK, RET_DV), F32),
                        pltpu.VMEM((chunks, RET_HEADS, RET_DK, RET_DV), BF16),
                        pltpu.VMEM((BLOCK_ROWS + WINDOW, Z_SKV_W), BF16), pltpu.VMEM((WINDOW, Z_SKV_W), BF16)],
        compiler_params=pltpu.CompilerParams(dimension_semantics=("arbitrary",),
                                             vmem_limit_bytes=VMEM_LIMIT),
        name="block",
    )(sinks, x2d, x2d, mix_g, w_in, *consts, *merge_weights, *small)


def _plan_tables(cnt_ref, base_ref, meta_ref):
    counts = cnt_ref[...]
    nt = counts.shape[0]
    exact = functools.partial(jnp.dot, precision=lax.Precision.HIGHEST, preferred_element_type=F32)
    total = jnp.sum(counts, axis=0, keepdims=True)
    tiles_e = jnp.ceil(total * (1.0 / ROW_TILE))
    k = lax.broadcasted_iota(jnp.int32, (ROUTE_LANES, ROUTE_LANES), 0)
    lane = lax.broadcasted_iota(jnp.int32, (ROUTE_LANES, ROUTE_LANES), 1)
    tile_end = exact(jnp.broadcast_to(tiles_e, (8, ROUTE_LANES)), jnp.where(k <= lane, 1.0, 0.0))[0:1]
    tile_start = tile_end - tiles_e
    r = lax.broadcasted_iota(jnp.int32, (nt, nt), 0)
    c = lax.broadcasted_iota(jnp.int32, (nt, nt), 1)
    before = exact(jnp.where(c < r, 1.0, 0.0), counts)
    base_ref[...] = tile_start * ROW_TILE + before

    used = jnp.max(tile_end, axis=-1, keepdims=True)
    tile = k.astype(F32)
    clamped = jnp.minimum(tile, jnp.maximum(used - 1.0, 0.0))
    is_expert = lane < N_EXPERTS
    owner = jnp.sum(jnp.where(jnp.logical_and(is_expert, tile_end <= clamped), 1.0, 0.0), axis=-1, keepdims=True)
    owner = jnp.minimum(owner, N_EXPERTS - 1.0)
    mine = lane.astype(F32) == owner
    pick = lambda v: jnp.sum(jnp.where(mine, v, 0.0), axis=-1, keepdims=True)
    t1 = tile[:, 0:1]
    valid = jnp.clip(pick(total) - (t1 - pick(tile_start)) * ROW_TILE, 0.0, float(ROW_TILE))
    valid = jnp.where(t1 < used, valid, 0.0)

    nonempty = jnp.where(jnp.logical_and(lane[0:1] < N_EXPERTS, tiles_e > 0.0), 1.0, 0.0)
    rank = exact(jnp.broadcast_to(nonempty, (8, ROUTE_LANES)), jnp.where(k < lane, 1.0, 0.0))[0:1]
    nonempty_col = jnp.sum(jnp.where(k == lane, jnp.broadcast_to(nonempty, k.shape), 0.0), axis=-1, keepdims=True)
    later = jnp.where(jnp.logical_and(k > lane, nonempty_col > 0.0), tile, float(ROUTE_LANES))
    nxt_e = jnp.min(later, axis=0, keepdims=True)
    nxt_e = jnp.where(nxt_e < float(ROUTE_LANES), nxt_e, -1.0)
    first = jnp.where(jnp.logical_and(t1 == pick(tile_start), t1 < used), 1.0, 0.0)
    my_rank = pick(rank)
    slot = my_rank - 2.0 * jnp.floor(my_rank * 0.5)
    fields = (owner, valid, first, pick(nxt_e), slot, jnp.broadcast_to(used, owner.shape))
    by_tile = sum(jnp.where(lane == n, f, 0.0) for n, f in enumerate(fields))
    meta_ref[...] = by_tile.T[:8, :].astype(I32)


def _plan_kernel(cnt_ref, route_ref, base_ref, meta_ref, pos1_ref, pos2_ref):
    step = pl.program_id(0)
    nt = cnt_ref.shape[0]
    subs_per_step = nt // PLAN_STEPS

    @pl.when(step == 0)
    def _():
        _plan_tables(cnt_ref, base_ref, meta_ref)
        pos1_ref[...] = jnp.zeros_like(pos1_ref)
        pos2_ref[...] = jnp.zeros_like(pos2_ref)

    sel_row = lax.broadcasted_iota(jnp.int32, (8, ROUTE_LANES), 0)
    sel_lane = lax.broadcasted_iota(jnp.int32, (8, ROUTE_LANES), 1)
    tok_r = lax.broadcasted_iota(jnp.int32, (MERGE_SUB, MERGE_SUB), 0)
    tok_c = lax.broadcasted_iota(jnp.int32, (MERGE_SUB, MERGE_SUB), 1)
    eye = jnp.where(tok_r == tok_c, 1.0, 0.0).astype(BF16)
    upto = jnp.where(tok_r <= tok_c, 1.0, 0.0).astype(BF16)
    sum_experts = lambda a: jnp.sum(a, axis=0, keepdims=True)

    def sub_positions(local, carry):
        s = step * subs_per_step + local
        route = route_ref[pl.ds(pl.multiple_of(local * MERGE_SUB, MERGE_SUB), MERGE_SUB), :].astype(BF16)
        record_t = _dot_tn(route, eye)
        incl_t = _dot_tn(route, upto)[:N_EXPERTS]
        both_t = record_t[:N_EXPERTS]
        second_t = record_t[LANE_SECOND0:LANE_SECOND0 + N_EXPERTS]
        rank2 = sum_experts(second_t * incl_t) - 1.0
        rank1 = sum_experts(both_t * incl_t) - rank2 - 2.0
        base_row = jnp.where(sel_lane < N_EXPERTS, jnp.broadcast_to(base_ref[pl.ds(s, 1), :], (8, ROUTE_LANES)), 0.0)
        hi = jnp.floor(base_row * (1.0 / 256.0))
        lo = base_row - 256.0 * hi
        hi2 = pltpu.roll(hi, LANE_SECOND0, axis=1)
        lo2 = pltpu.roll(lo, LANE_SECOND0, axis=1)
        table = (jnp.where(sel_row == 0, hi, 0.0) + jnp.where(sel_row == 1, lo, 0.0)
                 + jnp.where(sel_row == 2, hi2, 0.0) + jnp.where(sel_row == 3, lo2, 0.0))
        r = _dot_nt(table.astype(BF16), route)
        base_both = 256.0 * r[0:1] + r[1:2]
        base2 = 256.0 * r[2:3] + r[3:4]
        worker = s // subs_per_worker
        chunk0 = lax.rem(s, subs_per_worker) * chunks_per_sub
        for ref, row in ((pos1_ref, base_both - base2 + rank1), (pos2_ref, base2 + rank2)):
            for c in range(chunks_per_sub):
                ref[worker, pl.ds(chunk0 + c, 1), :] = row[:, c * SC_CHUNK:(c + 1) * SC_CHUNK].astype(I32)
        return carry

    chunks_per_sub = MERGE_SUB // SC_CHUNK
    subs_per_worker = nt // SC_WORKERS
    lax.fori_loop(0, subs_per_step, sub_positions, 0, unroll=8)


def _tile_plan(cnt, route, n_tiles):
    assert n_tiles <= ROUTE_LANES
    nt = cnt.shape[0]
    assert nt % SC_WORKERS == 0 and MERGE_SUB % SC_CHUNK == 0 and nt * MERGE_SUB <= SC_WORKERS * SC_IDX_ROWS * SC_CHUNK
    assert nt % PLAN_STEPS == 0
    pos = jax.ShapeDtypeStruct((SC_WORKERS, SC_IDX_ROWS, SC_CHUNK), I32)
    outs = [jax.ShapeDtypeStruct((nt, ROUTE_LANES), F32), jax.ShapeDtypeStruct((8, ROUTE_LANES), I32), pos, pos]
    whole = lambda a: pl.BlockSpec(a.shape, lambda i: (0,) * len(a.shape))
    _, meta, pos1, pos2 = pl.pallas_call(
        _plan_kernel,
        grid=(PLAN_STEPS,),
        in_specs=[pl.BlockSpec((nt, ROUTE_LANES), lambda i: (0, 0)),
                  pl.BlockSpec((route.shape[0] // PLAN_STEPS, ROUTE_LANES), lambda i: (i, 0))],
        out_specs=[whole(o) for o in outs],
        out_shape=outs,
        compiler_params=pltpu.CompilerParams(dimension_semantics=("arbitrary",), vmem_limit_bytes=VMEM_LIMIT),
        name="tile_plan",
    )(cnt.reshape(nt, ROUTE_LANES), route)
    return meta, pos1, pos2


def _sc_mesh():
    return plsc.VectorSubcoreMesh(core_axis_name="c", subcore_axis_name="s")


def _sc_worker():
    return lax.axis_index("s") * SC_CORES + lax.axis_index("c")


def _dispatch(hp, pos1, pos2, n_rows):
    t = hp.shape[0]
    per_worker = t // SC_WORKERS
    k = per_worker // SC_CHUNK
    idx = pltpu.VMEM((SC_IDX_ROWS, SC_CHUNK), I32)

    @functools.partial(pl.kernel, mesh=_sc_mesh(), out_type=jax.ShapeDtypeStruct((n_rows, HALF_D), I32),
                       scratch_types=[idx, idx, pltpu.VMEM((SC_CHUNK, HALF_D), I32)], name="moe_dispatch")
    def run(h_hbm, p1_hbm, p2_hbm, xs_hbm, p1_v, p2_v, rows_v):
        wid = _sc_worker()
        pltpu.sync_copy(p1_hbm.at[wid], p1_v)
        pltpu.sync_copy(p2_hbm.at[wid], p2_v)
        for j in range(k):
            pltpu.sync_copy(h_hbm.at[pl.ds(wid * per_worker + j * SC_CHUNK, SC_CHUNK)], rows_v)
            pltpu.sync_copy(rows_v, xs_hbm.at[p1_v.at[j]])
            pltpu.sync_copy(rows_v, xs_hbm.at[p2_v.at[j]])

    return run(hp, pos1, pos2)


def _combine_gather(ys, pos1, pos2, t):
    per_worker = t // SC_WORKERS
    k = per_worker // SC_CHUNK
    idx = pltpu.VMEM((SC_IDX_ROWS, SC_CHUNK), I32)
    out = jax.ShapeDtypeStruct((t, HALF_D), I32)

    @functools.partial(pl.kernel, mesh=_sc_mesh(), out_type=(out, out),
                       scratch_types=[idx, idx, pltpu.VMEM((SC_CHUNK, HALF_D), I32)], name="moe_combine")
    def run(ys_hbm, p1_hbm, p2_hbm, g1_hbm, g2_hbm, p1_v, p2_v, rows_v):
        wid = _sc_worker()
        pltpu.sync_copy(p1_hbm.at[wid], p1_v)
        pltpu.sync_copy(p2_hbm.at[wid], p2_v)
        for j in range(k):
            dst = pl.ds(wid * per_worker + j * SC_CHUNK, SC_CHUNK)
            pltpu.sync_copy(ys_hbm.at[p1_v.at[j]], rows_v)
            pltpu.sync_copy(rows_v, g1_hbm.at[dst])
            pltpu.sync_copy(ys_hbm.at[p2_v.at[j]], rows_v)
            pltpu.sync_copy(rows_v, g2_hbm.at[dst])

    return run(ys, pos1, pos2)


META_OWNER, META_VALID, META_FIRST, META_NEXT, META_SLOT, META_USED = range(6)


def _experts_kernel(meta_ref, xs_hbm, wg_hbm, wu_hbm, wd_hbm, ys_ref,
                    xs_buf, xs_sem, wg_f, wu_f, wd_f, wg_bf, wu_bf, wd_bf, w_sem):
    i = pl.program_id(0)
    used = meta_ref[META_USED, 0]
    valid = meta_ref[META_VALID, i]
    owner = meta_ref[META_OWNER, i]
    nxt = meta_ref[META_NEXT, i]

    def rows_in(tile):
        start = pl.multiple_of(tile * ROW_TILE, ROW_TILE)
        slot = lax.rem(tile, XS_SLOTS)
        return pltpu.make_async_copy(xs_hbm.at[pl.ds(start, ROW_TILE)], xs_buf.at[slot], xs_sem.at[slot])

    @pl.when(i == 0)
    def _():
        for ahead in range(XS_SLOTS - 1):
            pl.when(ahead < used)(rows_in(ahead).start)

    @pl.when(i + (XS_SLOTS - 1) < used)
    def _():
        rows_in(i + (XS_SLOTS - 1)).start()

    def fetch(expert, slot):
        pairs = ((wg_hbm, wg_f), (wu_hbm, wu_f), (wd_hbm, wd_f))
        return [pltpu.make_async_copy(w.at[expert], buf.at[slot], w_sem.at[slot, n])
                for n, (w, buf) in enumerate(pairs)]

    @pl.when(i == 0)
    def _():
        for copy in fetch(owner, 0):
            copy.start(priority=W_DMA_PRIORITY)

    @pl.when(meta_ref[META_FIRST, i] == 1)
    def _():
        slot = meta_ref[META_SLOT, i]
        for copy in fetch(owner, slot):
            copy.wait()
        wg_bf[...] = wg_f[slot].astype(BF16)
        wu_bf[...] = wu_f[slot].astype(BF16)
        wd_bf[...] = wd_f[slot].astype(BF16)

        @pl.when(nxt >= 0)
        def _():
            for copy in fetch(nxt, 1 - slot):
                copy.start(priority=W_DMA_PRIORITY)

    @pl.when(i < used)
    def _():
        rows_in(i).wait()
        row = lax.broadcasted_iota(jnp.int32, (ROW_TILE, HALF_D), 0)
        words = xs_buf[lax.rem(i, XS_SLOTS)]
        packed = jnp.where(row < valid, words, 0)
        hi, lo = _unpack_halves(packed)
        hi = hi.astype(BF16)
        lo = lo.astype(BF16)
        hg = _dot(hi, wg_bf[:HALF_D, :]) + _dot(lo, wg_bf[HALF_D:, :])
        hu = _dot(hi, wu_bf[:HALF_D, :]) + _dot(lo, wu_bf[HALF_D:, :])
        hid = (hg * _sigmoid(hg) * hu).astype(BF16)
        ys_ref[...] = _pack_halves(_dot(hid, wd_bf[...]))


def _experts(xs, meta, n_tiles, wg, wu, wd):
    tile = lambda i, meta: (jnp.minimum(i, meta[META_USED, 0] - 1), 0)
    hbm = pl.BlockSpec(memory_space=pl.ANY)
    up_shape, down_shape = (D_MODEL, EXPERT_FF), (EXPERT_FF, D_MODEL)
    grid_spec = pltpu.PrefetchScalarGridSpec(
        num_scalar_prefetch=1,
        grid=(n_tiles,),
        in_specs=[hbm, hbm, hbm, hbm],
        out_specs=pl.BlockSpec((ROW_TILE, HALF_D), tile),
        scratch_shapes=[pltpu.VMEM((XS_SLOTS, ROW_TILE, HALF_D), I32), pltpu.SemaphoreType.DMA((XS_SLOTS,)),
                        pltpu.VMEM((2,) + up_shape, F32), pltpu.VMEM((2,) + up_shape, F32),
                        pltpu.VMEM((2,) + down_shape, F32),
                        pltpu.VMEM(up_shape, BF16), pltpu.VMEM(up_shape, BF16), pltpu.VMEM(down_shape, BF16),
                        pltpu.SemaphoreType.DMA((2, 3))],
    )
    return pl.pallas_call(
        _experts_kernel,
        grid_spec=grid_spec,
        out_shape=jax.ShapeDtypeStruct((n_tiles * ROW_TILE, HALF_D), I32),
        compiler_params=pltpu.CompilerParams(dimension_semantics=("arbitrary",),
                                             vmem_limit_bytes=VMEM_LIMIT),
        name="experts",
    )(meta, xs, wg, wu, wd)


def _ple_kernel(x1_hbm, g1_hbm, g2_hbm, route_ref, p_ref, wp_f32, pg_ref, gg_ref, wgate_f32, o_ref,
                wp_ref, wgate_ref, x1_buf, g1_buf, g2_buf, ring_sem):
    i = pl.program_id(0)
    steps = pl.num_programs(0)
    streams = ((x1_hbm, x1_buf), (g1_hbm, g1_buf), (g2_hbm, g2_buf))

    def rows_in(step):
        start = pl.multiple_of(step * PLE_TM, PLE_TM)
        slot = lax.rem(step, PLE_SLOTS)
        return [pltpu.make_async_copy(src.at[pl.ds(start, PLE_TM)], buf.at[slot], ring_sem.at[n, slot])
                for n, (src, buf) in enumerate(streams)]

    def start_rows(step):
        for copy in rows_in(step):
            copy.start()

    @pl.when(i == 0)
    def _():
        for ahead in range(PLE_SLOTS - 1):
            pl.when(ahead < steps)(functools.partial(start_rows, ahead))
        wp_ref[...] = wp_f32[...].astype(BF16)
        wgate_ref[...] = wgate_f32[...].astype(BF16)

    pl.when(i + (PLE_SLOTS - 1) < steps)(functools.partial(start_rows, i + (PLE_SLOTS - 1)))

    for copy in rows_in(i):
        copy.wait()
    slot = lax.rem(i, PLE_SLOTS)
    x1_ref, g1_ref, g2_ref = x1_buf.at[slot], g1_buf.at[slot], g2_buf.at[slot]

    route = route_ref[...]
    lane = lax.broadcasted_iota(jnp.int32, route.shape, 1)
    pick = lambda ln: jnp.sum(jnp.where(lane == ln, route, 0.0), axis=-1, keepdims=True)
    c1 = pick(LANE_C1)
    c2 = pick(LANE_C2)
    a_hi, a_lo = _unpack_halves(g1_ref[...])
    b_hi, b_lo = _unpack_halves(g2_ref[...])
    x = x1_ref[...] + jnp.concatenate([c1 * a_hi + c2 * b_hi, c1 * a_lo + c2 * b_lo], axis=1)
    ple = _rms(_dot(p_ref[...].astype(BF16), wp_ref[...]), pg_ref[...])
    gate = _sigmoid(_dot(_rms(x, gg_ref[...]).astype(BF16), wgate_ref[...]))
    o_ref[...] = x + gate * ple


def _ple(x1, g1, g2, route, p2d, w_ple, ple_g, gate_g, w_gate):
    t = x1.shape[0]
    row = lambda w: pl.BlockSpec((PLE_TM, w), lambda i: (i, 0))
    full = lambda a: pl.BlockSpec(a.shape, lambda i: (0,) * a.ndim)
    hbm = pl.BlockSpec(memory_space=pl.ANY)
    return pl.pallas_call(
        _ple_kernel,
        grid=(t // PLE_TM,),
        in_specs=[hbm, hbm, hbm, row(ROUTE_LANES), row(PLE_DIM),
                  full(w_ple), full(ple_g), full(gate_g), full(w_gate)],
        out_specs=row(D_MODEL),
        out_shape=jax.ShapeDtypeStruct((t, D_MODEL), F32),
        scratch_shapes=[pltpu.VMEM(w_ple.shape, BF16), pltpu.VMEM(w_gate.shape, BF16),
                        pltpu.VMEM((PLE_SLOTS, PLE_TM, D_MODEL), F32),
                        pltpu.VMEM((PLE_SLOTS, PLE_TM, HALF_D), I32), pltpu.VMEM((PLE_SLOTS, PLE_TM, HALF_D), I32),
                        pltpu.SemaphoreType.DMA((3, PLE_SLOTS))],
        compiler_params=pltpu.CompilerParams(dimension_semantics=("arbitrary",),
                                             vmem_limit_bytes=VMEM_LIMIT),
        name="ple",
    )(x1, g1, g2, route, p2d, w_ple, ple_g, gate_g, w_gate)


def kernel(x, p, mix_norm_g, w_in, ret_gn_g, ret_gn_b, w_ret_o, q_norm_g, k_norm_g, attn_sinks,
           w_swa_o, w_out, ffn_norm_g, w_router_group, b_router_group, w_router_expert,
           b_router_expert, w_exp_gate, w_exp_up, w_exp_down, ple_gate_norm_g, w_ple_gate,
           w_ple, ple_norm_g):
    batch, seq, d = x.shape
    t = batch * seq
    depth = w_in.shape[0]
    n_tiles = 2 * t // ROW_TILE + N_EXPERTS
    row = lambda a: a.reshape(1, -1)
    x2d = x.reshape(t, d)
    for i in range(depth):
        router = (w_router_expert[i], w_router_group[i], row(b_router_expert[i]), row(b_router_group[i]))
        x1, hp, route, cnt = _block(x2d, row(mix_norm_g[i]), w_in[i], row(ret_gn_g[i]),
                                    row(ret_gn_b[i]), attn_sinks[i], row(q_norm_g[i]), row(k_norm_g[i]),
                                    w_ret_o[i], w_swa_o[i], w_out[i], row(ffn_norm_g[i]), router, seq)
        meta, pos1, pos2 = _tile_plan(cnt, route, n_tiles)
        xs = _dispatch(hp, pos1, pos2, n_tiles * ROW_TILE)
        ys = _experts(xs, meta, n_tiles, w_exp_gate[i], w_exp_up[i], w_exp_down[i])
        g1, g2 = _combine_gather(ys, pos1, pos2, t)
        x2d = _ple(x1, g1, g2, route, p[i].reshape(t, PLE_DIM), w_ple[i], row(ple_norm_g[i]),
                   row(ple_gate_norm_g[i]), w_ple_gate[i])
    return x2d.reshape(batch, seq, d)
```

```python
import functools

import jax
import jax.numpy as jnp
import numpy as np
from jax import lax
from jax.experimental import pallas as pl
from jax.experimental.pallas import tpu as pltpu
from jax.experimental.pallas import tpu_sc as plsc

F32 = jnp.float32
BF16 = jnp.bfloat16
I32 = jnp.int32

EPS = 1e-6
D_MODEL = 1024
PLE_DIM = 256
RET_HEADS = 4
RET_DK = 128
RET_DV = 128
RET_CHUNK = 128
SWA_HEADS = 8
SWA_KV_HEADS = 2
SWA_GROUP = SWA_HEADS // SWA_KV_HEADS
SWA_HD = 64
WINDOW = 128
N_GROUPS = 4
EXPERTS_PER_GROUP = 8
N_EXPERTS = N_GROUPS * EXPERTS_PER_GROUP
EXPERT_FF = 256

RET_W = RET_HEADS * RET_DK
SWA_Q = SWA_HEADS * SWA_HD
SWA_KV = SWA_KV_HEADS * SWA_HD
Z_RET_W = 4 * RET_W
Z_SKV_W = 2 * SWA_KV
Z_GATE_W = 2 * D_MODEL
COL_SQ = Z_RET_W
COL_SKV = COL_SQ + SWA_Q
COL_GATE = COL_SKV + Z_SKV_W
IN_WIDTH = COL_GATE + Z_GATE_W

ROUTE_LANES = 128
LANE_GROUP0 = N_EXPERTS
LANE_C1, LANE_C2 = 34, 35
LANE_SECOND0 = 64
NEG_INF = -1e30

HALF_D = D_MODEL // 2
ROW_TILE = 512
XS_SLOTS = 3
SC_CORES = 2
SC_SUBCORES = 16
SC_WORKERS = SC_CORES * SC_SUBCORES
SC_CHUNK = 128
SC_IDX_ROWS = 8

VMEM_LIMIT = 56 * 1024 * 1024

BLOCK_ROWS = 512
W_SLAB = 256
W_SLOTS = 6
RET_ROWS = 512
MERGE_SUB = 256
PLAN_STEPS = 4
PLE_TM = 1024
PLE_SLOTS = 3


def _rms(x, g):
    ms = jnp.mean(x * x, axis=-1, keepdims=True)
    return x * lax.rsqrt(ms + EPS) * g


def _sigmoid(x):
    return 1.0 / (1.0 + jnp.exp(-x))


def _dot(a, b):
    return jnp.dot(a, b, preferred_element_type=F32)


def _dot_nt(a, b):
    return lax.dot_general(a, b, (((1,), (1,)), ((), ())), preferred_element_type=F32)


def _dot_tn(a, b):
    return lax.dot_general(a, b, (((0,), (0,)), ((), ())), preferred_element_type=F32)


def _pack_halves(v):
    return pltpu.pack_elementwise([v[:, :HALF_D], v[:, HALF_D:]], packed_dtype=BF16)


def _unpack_halves(p):
    words = lax.bitcast_convert_type(p, jnp.uint32)
    return tuple(pltpu.unpack_elementwise(words, index=k, packed_dtype=BF16, unpacked_dtype=F32) for k in range(2))


def _retention_tables():
    f32 = np.float32
    h = RET_HEADS
    c = RET_CHUNK
    log_gamma = np.log1p(-np.exp2(f32(-5.0) - np.arange(h, dtype=f32))).astype(f32)
    pos = np.arange(c, dtype=f32)
    diff = pos[:, None] - pos[None, :]
    decay = np.where(diff[None] >= 0.0, np.exp(np.maximum(diff, f32(0.0))[None] * log_gamma[:, None, None]), f32(0.0))
    scale = f32(RET_DK ** -0.5)
    dmask = decay * scale
    zeta = np.exp((f32(c - 1.0) - pos)[None, :] * log_gamma[:, None]) * scale
    xi = np.exp((pos + f32(1.0))[None, :] * log_gamma[:, None])
    cdec = np.exp(f32(c) * log_gamma)
    bc = lambda a: np.ascontiguousarray(np.broadcast_to(a[:, :, None], (h, c, c)), dtype=f32)
    return (dmask.astype(f32), bc(zeta), bc(xi),
            np.ascontiguousarray(np.broadcast_to(cdec[:, None, None], (h, c, c)), dtype=f32))


def _swa_tables():
    f32 = np.float32
    qi = np.arange(WINDOW)[:, None]
    sj = np.arange(2 * WINDOW)[None, :]
    rel = qi + WINDOW - sj
    ok = (rel >= 0) & (rel < WINDOW)
    slopes = np.exp2(f32(-8.0) * np.arange(1, SWA_HEADS + 1, dtype=f32) / f32(SWA_HEADS)).astype(f32)
    bias = np.where(ok[None], -slopes[:, None, None] * rel.astype(f32)[None], f32(NEG_INF)).astype(f32)
    bias = bias.reshape(SWA_KV_HEADS, SWA_GROUP * WINDOW, 2 * WINDOW)
    avg = lambda heads: jnp.asarray(np.kron(np.eye(heads, dtype=f32), np.full((SWA_HD, SWA_HD), 1.0 / SWA_HD, f32)),
                                    dtype=BF16)
    return avg(SWA_HEADS), avg(SWA_KV_HEADS), jnp.asarray(bias)


def _block_kernel(per_seq, sinks_ref, x_ref, xp_ref, g_ref, w_hbm, dmask_ref, zeta_ref, xi_ref, cdec_ref,
                  gng_ref, gnb_ref, qgain_ref, kgain_ref, bq_ref, bk_ref, bias_ref,
                  wro_hbm, wso_hbm, wout_hbm, fg_ref, wrt_ref, bre_ref, brg_ref,
                  x1_ref, hp_ref, route_ref, cnt_ref,
                  w_ref, wro_ref, wso_ref, wout_ref, wtmp_ref, w_sem, qg_ref, kg_ref, wr_ref, br_ref,
                  zc_ref, zn_ref, yr_ref, ys_ref, state_ref, prev_ref, band_ref, tail_ref):
    g = pl.program_id(0)
    seq_start = lax.rem(g - 1, per_seq) == 0

    @pl.when(g == 0)
    def _():
        slabs = [(src, dst, c0) for src, dst in ((w_hbm, w_ref), (wro_hbm, wro_ref), (wso_hbm, wso_ref),
                                                 (wout_hbm, wout_ref))
                 for c0 in range(0, dst.shape[1], W_SLAB)]

        def slab(k):
            src, dst, c0 = slabs[k]
            return pltpu.make_async_copy(src.at[:, pl.ds(c0, W_SLAB)],
                                         wtmp_ref.at[k % W_SLOTS, pl.ds(0, dst.shape[0])], w_sem.at[k % W_SLOTS])

        for k in range(W_SLOTS - 1):
            slab(k).start()
        for k, (_, dst, c0) in enumerate(slabs):
            if k + W_SLOTS - 1 < len(slabs):
                slab(k + W_SLOTS - 1).start()
            slab(k).wait()
            dst[:, c0:c0 + W_SLAB] = wtmp_ref[k % W_SLOTS, :dst.shape[0]].astype(BF16)
        for hd in range(SWA_HEADS):
            qg_ref[:, hd * SWA_HD:(hd + 1) * SWA_HD] = qgain_ref[...] * (SWA_HD ** -0.5)
        for kh in range(SWA_KV_HEADS):
            kg_ref[:, kh * SWA_HD:(kh + 1) * SWA_HD] = kgain_ref[...]
        wr_ref[...] = wrt_ref[...].astype(BF16)
        br_ref[...] = jnp.zeros_like(br_ref)
        br_ref[:, :N_EXPERTS] = bre_ref[...]
        br_ref[:, LANE_GROUP0:LANE_GROUP0 + N_GROUPS] = brg_ref[...]
        zc_ref[...] = jnp.zeros_like(zc_ref)
        tail_ref[...] = jnp.zeros_like(tail_ref)
        state_ref[...] = jnp.zeros_like(state_ref)

    @pl.when(seq_start)
    def _():
        state_ref[...] = jnp.zeros_like(state_ref)

    band_ref[:WINDOW, :] = tail_ref[...]
    band_ref[WINDOW:, :] = zc_ref[:, COL_SKV:COL_GATE]

    h = _rms(x_ref[...], g_ref[...]).astype(BF16)

    def project(c0, width):
        zn_ref[:, c0:c0 + width] = _dot(h, w_ref[:, c0:c0 + width]).astype(BF16)

    cw = 512
    proj_pieces = [functools.partial(project, j * cw, cw) for j in range(COL_SKV // cw)]
    proj_pieces.append(functools.partial(project, COL_SKV, Z_SKV_W))
    proj_pieces += [functools.partial(project, COL_GATE + j * cw, cw) for j in range(Z_GATE_W // cw)]

    def retention_rows(t):
        r_base = t * RET_ROWS
        n_chunks = RET_ROWS // RET_CHUNK
        part = lambda c, p, hd: zc_ref[r_base + c * RET_CHUNK:r_base + (c + 1) * RET_CHUNK,
                                       p * RET_W + hd * RET_DK:p * RET_W + (hd + 1) * RET_DK]
        for hd in range(RET_HEADS):
            state = state_ref[hd]
            for c in range(n_chunks):
                prev_ref[t * n_chunks + c, hd] = state.astype(BF16)
                kz = (part(c, 1, hd).astype(F32) * zeta_ref[hd]).astype(BF16)
                state = cdec_ref[hd] * state + _dot_tn(kz, part(c, 2, hd))
            state_ref[hd] = state
        out_rows = []
        for c in range(n_chunks):
            heads = []
            for hd in range(RET_HEADS):
                q = part(c, 0, hd)
                scores = _dot_nt(q, part(c, 1, hd)) * dmask_ref[hd]
                y = _dot(scores.astype(BF16), part(c, 2, hd)) + _dot(q, prev_ref[t * n_chunks + c, hd]) * xi_ref[hd]
                mu = jnp.mean(y, axis=-1, keepdims=True)
                d = y - mu
                var = jnp.mean(d * d, axis=-1, keepdims=True)
                hs = slice(hd * RET_DV, (hd + 1) * RET_DV)
                yn = d * lax.rsqrt(var + EPS) * gng_ref[:, hs] + gnb_ref[:, hs]
                gate = part(c, 3, hd).astype(F32)
                heads.append((gate * _sigmoid(gate) * yn).astype(BF16))
            out_rows.append(jnp.concatenate(heads, axis=1))
        yr_ref[r_base:r_base + RET_ROWS, :] = jnp.concatenate(out_rows, axis=0)

    group_rows = SWA_GROUP * WINDOW
    lo_q = lax.broadcasted_iota(jnp.int32, (WINDOW, 2 * SWA_HD), 1) < SWA_HD
    lo_k = lax.broadcasted_iota(jnp.int32, (2 * WINDOW, 2 * SWA_HD), 1) < SWA_HD
    in_prev = lax.broadcasted_iota(jnp.int32, (group_rows, 2 * WINDOW), 1) < WINDOW
    row_head = lax.broadcasted_iota(jnp.int32, (group_rows, 1), 0) // WINDOW

    def both_heads(v):
        swapped = pltpu.roll(v, SWA_HD, axis=1)
        return jnp.where(lo_k, v, swapped), jnp.where(lo_k, swapped, v)

    def swa_block(n):
        rows = slice(n * WINDOW, (n + 1) * WINDOW)
        kv = band_ref[n * WINDOW:(n + 2) * WINDOW, :]
        kf = kv[:, :SWA_KV].astype(F32)
        kn = kf * lax.rsqrt(_dot((kf * kf).astype(BF16), bk_ref[...]) + EPS) * kg_ref[...]
        keys = [a.astype(BF16) for a in both_heads(kn)]
        vals = [a.astype(BF16) for a in both_heads(kv[:, SWA_KV:].astype(F32))]
        qf = zc_ref[rows, COL_SQ:COL_SKV].astype(F32)
        qn = qf * lax.rsqrt(_dot((qf * qf).astype(BF16), bq_ref[...]) + EPS) * qg_ref[...]
        outs = []
        for kh in range(SWA_KV_HEADS):
            parts = []
            for j in range(SWA_GROUP // 2):
                c0 = (kh * SWA_GROUP + 2 * j) * SWA_HD
                two = qn[:, c0:c0 + 2 * SWA_HD]
                parts += [jnp.where(lo_q, two, 0.0), jnp.where(lo_q, 0.0, two)]
            qs = jnp.concatenate(parts, axis=0).astype(BF16)
            s = _dot_nt(qs, keys[kh]) + bias_ref[kh]
            if n == 0:
                s = jnp.where(jnp.logical_and(seq_start, in_prev), NEG_INF, s)
            sink = sinks_ref[kh * SWA_GROUP]
            for gi in range(1, SWA_GROUP):
                sink = jnp.where(row_head == gi, sinks_ref[kh * SWA_GROUP + gi], sink)
            m = jnp.maximum(jnp.max(s, axis=-1, keepdims=True), sink)
            p = jnp.exp(s - m)
            denom = jnp.sum(p, axis=-1, keepdims=True) + jnp.exp(sink - m)
            o = _dot(p.astype(BF16), vals[kh]) * (1.0 / denom)
            for j in range(SWA_GROUP // 2):
                even = o[(2 * j) * WINDOW:(2 * j + 1) * WINDOW]
                odd = o[(2 * j + 1) * WINDOW:(2 * j + 2) * WINDOW]
                outs.append(jnp.where(lo_q, even, odd).astype(BF16))
        ys_ref[rows, :] = jnp.concatenate(outs, axis=1)

    mix_pieces = [functools.partial(retention_rows, t) for t in range(BLOCK_ROWS // RET_ROWS)]
    mix_pieces += [functools.partial(swa_block, n) for n in range(BLOCK_ROWS // WINDOW)]

    subs = []

    def merge_sub(s):
        subs.append(_merge_rows(slice(s * MERGE_SUB, (s + 1) * MERGE_SUB), yr_ref, ys_ref, zc_ref, xp_ref, wro_ref,
                                wso_ref, wout_ref, fg_ref, wr_ref, br_ref))

    other_pieces = mix_pieces + [functools.partial(merge_sub, s) for s in range(BLOCK_ROWS // MERGE_SUB)]

    n_other = len(other_pieces)
    for k, piece in enumerate(other_pieces):
        for proj_piece in proj_pieces[k * len(proj_pieces) // n_other:(k + 1) * len(proj_pieces) // n_other]:
            proj_piece()
        piece()
    x1_ref[...] = jnp.concatenate([s[0] for s in subs], axis=0)
    hp_ref[...] = jnp.concatenate([s[1] for s in subs], axis=0)
    route_ref[...] = jnp.concatenate([s[2] for s in subs], axis=0)
    for s, sub in enumerate(subs):
        cnt_ref[s] = sub[3]

    tail_ref[...] = zc_ref[BLOCK_ROWS - WINDOW:, COL_SKV:COL_GATE]
    zc_ref[...] = zn_ref[...]


def _merge_rows(rows, yr_ref, ys_ref, z_ref, x_ref, wro_ref, wso_ref, wout_ref, fg_ref, wr_ref, br_ref):
    a = _dot(yr_ref[rows, :], wro_ref[...])
    b = _dot(ys_ref[rows, :], wso_ref[...])
    gate_r = z_ref[rows, COL_GATE:COL_GATE + D_MODEL].astype(F32)
    gate_s = z_ref[rows, COL_GATE + D_MODEL:].astype(F32)
    merged = (_sigmoid(gate_r) * a + _sigmoid(gate_s) * b).astype(BF16)
    x1 = x_ref[rows, :] + _dot(merged, wout_ref[...])
    h2 = _rms(x1, fg_ref[...])
    packed = _pack_halves(h2)

    logits = _dot(h2.astype(BF16), wr_ref[...]) + br_ref[...]
    lane = lax.broadcasted_iota(jnp.int32, logits.shape, 1)
    big = jnp.int32(ROUTE_LANES)
    is_group = jnp.logical_and(lane >= LANE_GROUP0, lane < LANE_GROUP0 + N_GROUPS)
    gl = jnp.where(is_group, logits, NEG_INF)
    gmax = jnp.max(gl, axis=-1, keepdims=True)
    g_w = 1.0 / jnp.sum(jnp.exp(gl - gmax), axis=-1, keepdims=True)
    g_sel = jnp.min(jnp.where(gl == gmax, lane, big), axis=-1, keepdims=True) - LANE_GROUP0
    in_group = jnp.logical_and(lane < N_EXPERTS, (lane >> 3) == g_sel)
    el = jnp.where(in_group, logits, NEG_INF)
    m1 = jnp.max(el, axis=-1, keepdims=True)
    i1 = jnp.min(jnp.where(el == m1, lane, big), axis=-1, keepdims=True)
    el2 = jnp.where(lane == i1, NEG_INF, el)
    m2 = jnp.max(el2, axis=-1, keepdims=True)
    i2 = jnp.min(jnp.where(el2 == m2, lane, big), axis=-1, keepdims=True)
    e2 = jnp.exp(m2 - m1)
    c1 = g_w / (1.0 + e2)
    c2 = g_w * e2 / (1.0 + e2)
    chosen = jnp.where(jnp.logical_or(lane == i1, lane == i2), 1.0, 0.0)
    route = (chosen
             + jnp.where(lane == LANE_C1, c1, 0.0)
             + jnp.where(lane == LANE_C2, c2, 0.0)
             + jnp.where(lane == i2 + LANE_SECOND0, 1.0, 0.0))
    return x1, packed, route, jnp.sum(chosen, axis=0, keepdims=True)


def _block(x2d, mix_g, w_in, gn_g, gn_b, sinks, q_g, k_g, w_ret_o, w_swa_o, w_out, ffn_g, router, seq):
    t = x2d.shape[0]
    n = t // BLOCK_ROWS
    assert seq % BLOCK_ROWS == 0 and LANE_GROUP0 == N_EXPERTS and router[0].shape == (D_MODEL, ROUTE_LANES)
    consts = _retention_tables() + (gn_g, gn_b, q_g, k_g) + _swa_tables()
    merge_weights = (w_ret_o, w_swa_o, w_out)
    small = (ffn_g,) + tuple(router)
    hbm = pl.BlockSpec(memory_space=pl.ANY)
    full = lambda a: pl.BlockSpec(a.shape, lambda g: (0,) * a.ndim)
    cur = lambda w: pl.BlockSpec((BLOCK_ROWS, w), lambda g: (jnp.minimum(g, n - 1), 0))
    prv = lambda w: pl.BlockSpec((BLOCK_ROWS, w), lambda g: (jnp.maximum(g - 1, 0), 0))
    chunks = BLOCK_ROWS // RET_CHUNK
    subs_per_block = BLOCK_ROWS // MERGE_SUB
    return pl.pallas_call(
        functools.partial(_block_kernel, seq // BLOCK_ROWS),
        grid=(n + 1,),
        in_specs=([pl.BlockSpec(memory_space=pltpu.SMEM), cur(D_MODEL), prv(D_MODEL), full(mix_g), hbm]
                  + [full(a) for a in consts] + [hbm] * len(merge_weights) + [full(a) for a in small]),
        out_specs=[prv(D_MODEL), prv(HALF_D), prv(ROUTE_LANES),
                   pl.BlockSpec((subs_per_block, 1, ROUTE_LANES), lambda g: (jnp.maximum(g - 1, 0), 0, 0))],
        out_shape=[jax.ShapeDtypeStruct((t, D_MODEL), F32),
                   jax.ShapeDtypeStruct((t, HALF_D), I32),
                   jax.ShapeDtypeStruct((t, ROUTE_LANES), F32),
                   jax.ShapeDtypeStruct((n * subs_per_block, 1, ROUTE_LANES), F32)],
        scratch_shapes=[pltpu.VMEM(w_in.shape, BF16)] + [pltpu.VMEM(w.shape, BF16) for w in merge_weights]
                       + [pltpu.VMEM((W_SLOTS, D_MODEL, W_SLAB), F32), pltpu.SemaphoreType.DMA((W_SLOTS,)),
                        pltpu.VMEM((1, SWA_Q), F32), pltpu.VMEM((1, SWA_KV), F32),
                        pltpu.VMEM((D_MODEL, ROUTE_LANES), BF16), pltpu.VMEM((1, ROUTE_LANES), F32),
                        pltpu.VMEM((BLOCK_ROWS, IN_WIDTH), BF16), pltpu.VMEM((BLOCK_ROWS, IN_WIDTH), BF16),
                        pltpu.VMEM((BLOCK_ROWS, RET_W), BF16), pltpu.VMEM((BLOCK_ROWS, SWA_Q), BF16),
                        pltpu.VMEM((RET_HEADS, RET_DK, RET_DV), F32),
                        pltpu.VMEM((chunks, RET_HEADS, RET_DK, RET_DV), BF16),
                        pltpu.VMEM((BLOCK_ROWS + WINDOW, Z_SKV_W), BF16), pltpu.VMEM((WINDOW, Z_SKV_W), BF16)],
        compiler_params=pltpu.CompilerParams(dimension_semantics=("arbitrary",),
                                             vmem_limit_bytes=VMEM_LIMIT),
        name="block",
    )(sinks, x2d, x2d, mix_g, w_in, *consts, *merge_weights, *small)


def _plan_tables(cnt_ref, base_ref, meta_ref):
    counts = cnt_ref[...]
    nt = counts.shape[0]
    exact = functools.partial(jnp.dot, precision=lax.Precision.HIGHEST, preferred_element_type=F32)
    total = jnp.sum(counts, axis=0, keepdims=True)
    tiles_e = jnp.ceil(total * (1.0 / ROW_TILE))
    k = lax.broadcasted_iota(jnp.int32, (ROUTE_LANES, ROUTE_LANES), 0)
    lane = lax.broadcasted_iota(jnp.int32, (ROUTE_LANES, ROUTE_LANES), 1)
    tile_end = exact(jnp.broadcast_to(tiles_e, (8, ROUTE_LANES)), jnp.where(k <= lane, 1.0, 0.0))[0:1]
    tile_start = tile_end - tiles_e
    r = lax.broadcasted_iota(jnp.int32, (nt, nt), 0)
    c = lax.broadcasted_iota(jnp.int32, (nt, nt), 1)
    before = exact(jnp.where(c < r, 1.0, 0.0), counts)
    base_ref[...] = tile_start * ROW_TILE + before

    used = jnp.max(tile_end, axis=-1, keepdims=True)
    tile = k.astype(F32)
    clamped = jnp.minimum(tile, jnp.maximum(used - 1.0, 0.0))
    is_expert = lane < N_EXPERTS
    owner = jnp.sum(jnp.where(jnp.logical_and(is_expert, tile_end <= clamped), 1.0, 0.0), axis=-1, keepdims=True)
    owner = jnp.minimum(owner, N_EXPERTS - 1.0)
    mine = lane.astype(F32) == owner
    pick = lambda v: jnp.sum(jnp.where(mine, v, 0.0), axis=-1, keepdims=True)
    t1 = tile[:, 0:1]
    valid = jnp.clip(pick(total) - (t1 - pick(tile_start)) * ROW_TILE, 0.0, float(ROW_TILE))
    valid = jnp.where(t1 < used, valid, 0.0)

    nonempty = jnp.where(jnp.logical_and(lane[0:1] < N_EXPERTS, tiles_e > 0.0), 1.0, 0.0)
    rank = exact(jnp.broadcast_to(nonempty, (8, ROUTE_LANES)), jnp.where(k < lane, 1.0, 0.0))[0:1]
    nonempty_col = jnp.sum(jnp.where(k == lane, jnp.broadcast_to(nonempty, k.shape), 0.0), axis=-1, keepdims=True)
    later = jnp.where(jnp.logical_and(k > lane, nonempty_col > 0.0), tile, float(ROUTE_LANES))
    nxt_e = jnp.min(later, axis=0, keepdims=True)
    nxt_e = jnp.where(nxt_e < float(ROUTE_LANES), nxt_e, -1.0)
    first = jnp.where(jnp.logical_and(t1 == pick(tile_start), t1 < used), 1.0, 0.0)
    my_rank = pick(rank)
    slot = my_rank - 2.0 * jnp.floor(my_rank * 0.5)
    fields = (owner, valid, first, pick(nxt_e), slot, jnp.broadcast_to(used, owner.shape))
    by_tile = sum(jnp.where(lane == n, f, 0.0) for n, f in enumerate(fields))
    meta_ref[...] = by_tile.T[:8, :].astype(I32)


def _plan_kernel(cnt_ref, route_ref, base_ref, meta_ref, pos1_ref, pos2_ref):
    step = pl.program_id(0)
    nt = cnt_ref.shape[0]
    subs_per_step = nt // PLAN_STEPS

    @pl.when(step == 0)
    def _():
        _plan_tables(cnt_ref, base_ref, meta_ref)
        pos1_ref[...] = jnp.zeros_like(pos1_ref)
        pos2_ref[...] = jnp.zeros_like(pos2_ref)

    sel_row = lax.broadcasted_iota(jnp.int32, (8, ROUTE_LANES), 0)
    sel_lane = lax.broadcasted_iota(jnp.int32, (8, ROUTE_LANES), 1)
    tok_r = lax.broadcasted_iota(jnp.int32, (MERGE_SUB, MERGE_SUB), 0)
    tok_c = lax.broadcasted_iota(jnp.int32, (MERGE_SUB, MERGE_SUB), 1)
    eye = jnp.where(tok_r == tok_c, 1.0, 0.0).astype(BF16)
    upto = jnp.where(tok_r <= tok_c, 1.0, 0.0).astype(BF16)
    sum_experts = lambda a: jnp.sum(a, axis=0, keepdims=True)

    def sub_positions(local, carry):
        s = step * subs_per_step + local
        route = route_ref[pl.ds(pl.multiple_of(local * MERGE_SUB, MERGE_SUB), MERGE_SUB), :].astype(BF16)
        record_t = _dot_tn(route, eye)
        incl_t = _dot_tn(route, upto)[:N_EXPERTS]
        both_t = record_t[:N_EXPERTS]
        second_t = record_t[LANE_SECOND0:LANE_SECOND0 + N_EXPERTS]
        rank2 = sum_experts(second_t * incl_t) - 1.0
        rank1 = sum_experts(both_t * incl_t) - rank2 - 2.0
        base_row = jnp.where(sel_lane < N_EXPERTS, jnp.broadcast_to(base_ref[pl.ds(s, 1), :], (8, ROUTE_LANES)), 0.0)
        hi = jnp.floor(base_row * (1.0 / 256.0))
        lo = base_row - 256.0 * hi
        hi2 = pltpu.roll(hi, LANE_SECOND0, axis=1)
        lo2 = pltpu.roll(lo, LANE_SECOND0, axis=1)
        table = (jnp.where(sel_row == 0, hi, 0.0) + jnp.where(sel_row == 1, lo, 0.0)
                 + jnp.where(sel_row == 2, hi2, 0.0) + jnp.where(sel_row == 3, lo2, 0.0))
        r = _dot_nt(table.astype(BF16), route)
        base_both = 256.0 * r[0:1] + r[1:2]
        base2 = 256.0 * r[2:3] + r[3:4]
        worker = s // subs_per_worker
        chunk0 = lax.rem(s, subs_per_worker) * chunks_per_sub
        for ref, row in ((pos1_ref, base_both - base2 + rank1), (pos2_ref, base2 + rank2)):
            for c in range(chunks_per_sub):
                ref[worker, pl.ds(chunk0 + c, 1), :] = row[:, c * SC_CHUNK:(c + 1) * SC_CHUNK].astype(I32)
        return carry

    chunks_per_sub = MERGE_SUB // SC_CHUNK
    subs_per_worker = nt // SC_WORKERS
    lax.fori_loop(0, subs_per_step, sub_positions, 0, unroll=8)


def _tile_plan(cnt, route, n_tiles):
    assert n_tiles <= ROUTE_LANES
    nt = cnt.shape[0]
    assert nt % SC_WORKERS == 0 and MERGE_SUB % SC_CHUNK == 0 and nt * MERGE_SUB <= SC_WORKERS * SC_IDX_ROWS * SC_CHUNK
    assert nt % PLAN_STEPS == 0
    pos = jax.ShapeDtypeStruct((SC_WORKERS, SC_IDX_ROWS, SC_CHUNK), I32)
    outs = [jax.ShapeDtypeStruct((nt, ROUTE_LANES), F32), jax.ShapeDtypeStruct((8, ROUTE_LANES), I32), pos, pos]
    whole = lambda a: pl.BlockSpec(a.shape, lambda i: (0,) * len(a.shape))
    _, meta, pos1, pos2 = pl.pallas_call(
        _plan_kernel,
        grid=(PLAN_STEPS,),
        in_specs=[pl.BlockSpec((nt, ROUTE_LANES), lambda i: (0, 0)),
                  pl.BlockSpec((route.shape[0] // PLAN_STEPS, ROUTE_LANES), lambda i: (i, 0))],
        out_specs=[whole(o) for o in outs],
        out_shape=outs,
        compiler_params=pltpu.CompilerParams(dimension_semantics=("arbitrary",), vmem_limit_bytes=VMEM_LIMIT),
        name="tile_plan",
    )(cnt.reshape(nt, ROUTE_LANES), route)
    return meta, pos1, pos2


def _sc_mesh():
    return plsc.VectorSubcoreMesh(core_axis_name="c", subcore_axis_name="s")


def _sc_worker():
    return lax.axis_index("s") * SC_CORES + lax.axis_index("c")


def _dispatch(hp, pos1, pos2, n_rows):
    t = hp.shape[0]
    per_worker = t // SC_WORKERS
    k = per_worker // SC_CHUNK
    idx = pltpu.VMEM((SC_IDX_ROWS, SC_CHUNK), I32)

    @functools.partial(pl.kernel, mesh=_sc_mesh(), out_type=jax.ShapeDtypeStruct((n_rows, HALF_D), I32),
                       scratch_types=[idx, idx, pltpu.VMEM((SC_CHUNK, HALF_D), I32)], name="moe_dispatch")
    def run(h_hbm, p1_hbm, p2_hbm, xs_hbm, p1_v, p2_v, rows_v):
        wid = _sc_worker()
        pltpu.sync_copy(p1_hbm.at[wid], p1_v)
        pltpu.sync_copy(p2_hbm.at[wid], p2_v)
        for j in range(k):
            pltpu.sync_copy(h_hbm.at[pl.ds(wid * per_worker + j * SC_CHUNK, SC_CHUNK)], rows_v)
            pltpu.sync_copy(rows_v, xs_hbm.at[p1_v.at[j]])
            pltpu.sync_copy(rows_v, xs_hbm.at[p2_v.at[j]])

    return run(hp, pos1, pos2)


def _combine_gather(ys, pos1, pos2, t):
    per_worker = t // SC_WORKERS
    k = per_worker // SC_CHUNK
    idx = pltpu.VMEM((SC_IDX_ROWS, SC_CHUNK), I32)
    out = jax.ShapeDtypeStruct((t, HALF_D), I32)

    @functools.partial(pl.kernel, mesh=_sc_mesh(), out_type=(out, out),
                       scratch_types=[idx, idx, pltpu.VMEM((SC_CHUNK, HALF_D), I32)], name="moe_combine")
    def run(ys_hbm, p1_hbm, p2_hbm, g1_hbm, g2_hbm, p1_v, p2_v, rows_v):
        wid = _sc_worker()
        pltpu.sync_copy(p1_hbm.at[wid], p1_v)
        pltpu.sync_copy(p2_hbm.at[wid], p2_v)
        for j in range(k):
            dst = pl.ds(wid * per_worker + j * SC_CHUNK, SC_CHUNK)
            pltpu.sync_copy(ys_hbm.at[p1_v.at[j]], rows_v)
            pltpu.sync_copy(rows_v, g1_hbm.at[dst])
            pltpu.sync_copy(ys_hbm.at[p2_v.at[j]], rows_v)
            pltpu.sync_copy(rows_v, g2_hbm.at[dst])

    return run(ys, pos1, pos2)


META_OWNER, META_VALID, META_FIRST, META_NEXT, META_SLOT, META_USED = range(6)


def _experts_kernel(meta_ref, xs_hbm, wg_hbm, wu_hbm, wd_hbm, ys_ref,
                    xs_buf, xs_sem, wg_f, wu_f, wd_f, wg_bf, wu_bf, wd_bf, w_sem):
    i = pl.program_id(0)
    used = meta_ref[META_USED, 0]
    valid = meta_ref[META_VALID, i]
    owner = meta_ref[META_OWNER, i]
    nxt = meta_ref[META_NEXT, i]

    def rows_in(tile):
        start = pl.multiple_of(tile * ROW_TILE, ROW_TILE)
        slot = lax.rem(tile, XS_SLOTS)
        return pltpu.make_async_copy(xs_hbm.at[pl.ds(start, ROW_TILE)], xs_buf.at[slot], xs_sem.at[slot])

    @pl.when(i == 0)
    def _():
        for ahead in range(XS_SLOTS - 1):
            pl.when(ahead < used)(rows_in(ahead).start)

    @pl.when(i + (XS_SLOTS - 1) < used)
    def _():
        rows_in(i + (XS_SLOTS - 1)).start()

    def fetch(expert, slot):
        pairs = ((wg_hbm, wg_f), (wu_hbm, wu_f), (wd_hbm, wd_f))
        return [pltpu.make_async_copy(w.at[expert], buf.at[slot], w_sem.at[slot, n])
                for n, (w, buf) in enumerate(pairs)]

    @pl.when(i == 0)
    def _():
        for copy in fetch(owner, 0):
            copy.start()

    @pl.when(meta_ref[META_FIRST, i] == 1)
    def _():
        slot = meta_ref[META_SLOT, i]
        for copy in fetch(owner, slot):
            copy.wait()
        wg_bf[...] = wg_f[slot].astype(BF16)
        wu_bf[...] = wu_f[slot].astype(BF16)
        wd_bf[...] = wd_f[slot].astype(BF16)

        @pl.when(nxt >= 0)
        def _():
            for copy in fetch(nxt, 1 - slot):
                copy.start()

    @pl.when(i < used)
    def _():
        rows_in(i).wait()
        row = lax.broadcasted_iota(jnp.int32, (ROW_TILE, HALF_D), 0)
        words = xs_buf[lax.rem(i, XS_SLOTS)]
        packed = jnp.where(row < valid, words, 0)
        hi, lo = _unpack_halves(packed)
        hi = hi.astype(BF16)
        lo = lo.astype(BF16)
        hg = _dot(hi, wg_bf[:HALF_D, :]) + _dot(lo, wg_bf[HALF_D:, :])
        hu = _dot(hi, wu_bf[:HALF_D, :]) + _dot(lo, wu_bf[HALF_D:, :])
        hid = (hg * _sigmoid(hg) * hu).astype(BF16)
        ys_ref[...] = _pack_halves(_dot(hid, wd_bf[...]))


def _experts(xs, meta, n_tiles, wg, wu, wd):
    tile = lambda i, meta: (jnp.minimum(i, meta[META_USED, 0] - 1), 0)
    hbm = pl.BlockSpec(memory_space=pl.ANY)
    up_shape, down_shape = (D_MODEL, EXPERT_FF), (EXPERT_FF, D_MODEL)
    grid_spec = pltpu.PrefetchScalarGridSpec(
        num_scalar_prefetch=1,
        grid=(n_tiles,),
        in_specs=[hbm, hbm, hbm, hbm],
        out_specs=pl.BlockSpec((ROW_TILE, HALF_D), tile),
        scratch_shapes=[pltpu.VMEM((XS_SLOTS, ROW_TILE, HALF_D), I32), pltpu.SemaphoreType.DMA((XS_SLOTS,)),
                        pltpu.VMEM((2,) + up_shape, F32), pltpu.VMEM((2,) + up_shape, F32),
                        pltpu.VMEM((2,) + down_shape, F32),
                        pltpu.VMEM(up_shape, BF16), pltpu.VMEM(up_shape, BF16), pltpu.VMEM(down_shape, BF16),
                        pltpu.SemaphoreType.DMA((2, 3))],
    )
    return pl.pallas_call(
        _experts_kernel,
        grid_spec=grid_spec,
        out_shape=jax.ShapeDtypeStruct((n_tiles * ROW_TILE, HALF_D), I32),
        compiler_params=pltpu.CompilerParams(dimension_semantics=("arbitrary",),
                                             vmem_limit_bytes=VMEM_LIMIT),
        name="experts",
    )(meta, xs, wg, wu, wd)


def _ple_kernel(x1_hbm, g1_hbm, g2_hbm, route_ref, p_ref, wp_f32, pg_ref, gg_ref, wgate_f32, o_ref,
                wp_ref, wgate_ref, x1_buf, g1_buf, g2_buf, ring_sem):
    i = pl.program_id(0)
    steps = pl.num_programs(0)
    streams = ((x1_hbm, x1_buf), (g1_hbm, g1_buf), (g2_hbm, g2_buf))

    def rows_in(step):
        start = pl.multiple_of(step * PLE_TM, PLE_TM)
        slot = lax.rem(step, PLE_SLOTS)
        return [pltpu.make_async_copy(src.at[pl.ds(start, PLE_TM)], buf.at[slot], ring_sem.at[n, slot])
                for n, (src, buf) in enumerate(streams)]

    def start_rows(step):
        for copy in rows_in(step):
            copy.start()

    @pl.when(i == 0)
    def _():
        for ahead in range(PLE_SLOTS - 1):
            pl.when(ahead < steps)(functools.partial(start_rows, ahead))
        wp_ref[...] = wp_f32[...].astype(BF16)
        wgate_ref[...] = wgate_f32[...].astype(BF16)

    pl.when(i + (PLE_SLOTS - 1) < steps)(functools.partial(start_rows, i + (PLE_SLOTS - 1)))

    for copy in rows_in(i):
        copy.wait()
    slot = lax.rem(i, PLE_SLOTS)
    x1_ref, g1_ref, g2_ref = x1_buf.at[slot], g1_buf.at[slot], g2_buf.at[slot]

    route = route_ref[...]
    lane = lax.broadcasted_iota(jnp.int32, route.shape, 1)
    pick = lambda ln: jnp.sum(jnp.where(lane == ln, route, 0.0), axis=-1, keepdims=True)
    c1 = pick(LANE_C1)
    c2 = pick(LANE_C2)
    a_hi, a_lo = _unpack_halves(g1_ref[...])
    b_hi, b_lo = _unpack_halves(g2_ref[...])
    x = x1_ref[...] + jnp.concatenate([c1 * a_hi + c2 * b_hi, c1 * a_lo + c2 * b_lo], axis=1)
    ple = _rms(_dot(p_ref[...].astype(BF16), wp_ref[...]), pg_ref[...])
    gate = _sigmoid(_dot(_rms(x, gg_ref[...]).astype(BF16), wgate_ref[...]))
    o_ref[...] = x + gate * ple


def _ple(x1, g1, g2, route, p2d, w_ple, ple_g, gate_g, w_gate):
    t = x1.shape[0]
    row = lambda w: pl.BlockSpec((PLE_TM, w), lambda i: (i, 0))
    full = lambda a: pl.BlockSpec(a.shape, lambda i: (0,) * a.ndim)
    hbm = pl.BlockSpec(memory_space=pl.ANY)
    return pl.pallas_call(
        _ple_kernel,
        grid=(t // PLE_TM,),
        in_specs=[hbm, hbm, hbm, row(ROUTE_LANES), row(PLE_DIM),
                  full(w_ple), full(ple_g), full(gate_g), full(w_gate)],
        out_specs=row(D_MODEL),
        out_shape=jax.ShapeDtypeStruct((t, D_MODEL), F32),
        scratch_shapes=[pltpu.VMEM(w_ple.shape, BF16), pltpu.VMEM(w_gate.shape, BF16),
                        pltpu.VMEM((PLE_SLOTS, PLE_TM, D_MODEL), F32),
                        pltpu.VMEM((PLE_SLOTS, PLE_TM, HALF_D), I32), pltpu.VMEM((PLE_SLOTS, PLE_TM, HALF_D), I32),
                        pltpu.SemaphoreType.DMA((3, PLE_SLOTS))],
        compiler_params=pltpu.CompilerParams(dimension_semantics=("arbitrary",),
                                             vmem_limit_bytes=VMEM_LIMIT),
        name="ple",
    )(x1, g1, g2, route, p2d, w_ple, ple_g, gate_g, w_gate)


def kernel(x, p, mix_norm_g, w_in, ret_gn_g, ret_gn_b, w_ret_o, q_norm_g, k_norm_g, attn_sinks,
           w_swa_o, w_out, ffn_norm_g, w_router_group, b_router_group, w_router_expert,
           b_router_expert, w_exp_gate, w_exp_up, w_exp_down, ple_gate_norm_g, w_ple_gate,
           w_ple, ple_norm_g):
    batch, seq, d = x.shape
    t = batch * seq
    depth = w_in.shape[0]
    n_tiles = 2 * t // ROW_TILE + N_EXPERTS
    row = lambda a: a.reshape(1, -1)
    x2d = x.reshape(t, d)
    for i in range(depth):
        unused_lanes = jnp.zeros((d, ROUTE_LANES - N_EXPERTS - N_GROUPS), F32)
        router = (jnp.concatenate([w_router_expert[i], w_router_group[i], unused_lanes], axis=1),
                  row(b_router_expert[i]), row(b_router_group[i]))
        x1, hp, route, cnt = _block(x2d, row(mix_norm_g[i]), w_in[i], row(ret_gn_g[i]),
                                    row(ret_gn_b[i]), attn_sinks[i], row(q_norm_g[i]), row(k_norm_g[i]),
                                    w_ret_o[i], w_swa_o[i], w_out[i], row(ffn_norm_g[i]), router, seq)
        meta, pos1, pos2 = _tile_plan(cnt, route, n_tiles)
        xs = _dispatch(hp, pos1, pos2, n_tiles * ROW_TILE)
        ys = _experts(xs, meta, n_tiles, w_exp_gate[i], w_exp_up[i], w_exp_down[i])
        g1, g2 = _combine_gather(ys, pos1, pos2, t)
        x2d = _ple(x1, g1, g2, route, p[i].reshape(t, PLE_DIM), w_ple[i], row(ple_norm_g[i]),
                   row(ple_gate_norm_g[i]), w_ple_gate[i])
    return x2d.reshape(batch, seq, d)
```

```python
import functools

import jax
import jax.numpy as jnp
import numpy as np
from jax import lax
from jax.experimental import pallas as pl
from jax.experimental.pallas import tpu as pltpu
from jax.experimental.pallas import tpu_sc as plsc

F32 = jnp.float32
BF16 = jnp.bfloat16
I32 = jnp.int32

EPS = 1e-6
D_MODEL = 1024
PLE_DIM = 256
RET_HEADS = 4
RET_DK = 128
RET_DV = 128
RET_CHUNK = 128
SWA_HEADS = 8
SWA_KV_HEADS = 2
SWA_GROUP = SWA_HEADS // SWA_KV_HEADS
SWA_HD = 64
WINDOW = 128
N_GROUPS = 4
EXPERTS_PER_GROUP = 8
N_EXPERTS = N_GROUPS * EXPERTS_PER_GROUP
EXPERT_FF = 256

RET_W = RET_HEADS * RET_DK
SWA_Q = SWA_HEADS * SWA_HD
SWA_KV = SWA_KV_HEADS * SWA_HD
Z_RET_W = 4 * RET_W
Z_SKV_W = 2 * SWA_KV
Z_GATE_W = 2 * D_MODEL
COL_SQ = Z_RET_W
COL_SKV = COL_SQ + SWA_Q
COL_GATE = COL_SKV + Z_SKV_W
IN_WIDTH = COL_GATE + Z_GATE_W

ROUTE_LANES = 128
LANE_GROUP0 = N_EXPERTS
LANE_C1, LANE_C2 = 34, 35
LANE_SECOND0 = 64
NEG_INF = -1e30

HALF_D = D_MODEL // 2
ROW_TILE = 512
XS_SLOTS = 3
SC_CORES = 2
SC_SUBCORES = 16
SC_WORKERS = SC_CORES * SC_SUBCORES
SC_CHUNK = 128
SC_IDX_ROWS = 8

VMEM_LIMIT = 56 * 1024 * 1024

BLOCK_ROWS = 512
W_SLAB = 256
W_SLOTS = 6
RET_ROWS = 512
MERGE_SUB = 256
PLAN_STEPS = 4
PLE_TM = 1024
PLE_SLOTS = 3


def _rms(x, g):
    ms = jnp.mean(x * x, axis=-1, keepdims=True)
    return x * lax.rsqrt(ms + EPS) * g


def _sigmoid(x):
    return 1.0 / (1.0 + jnp.exp(-x))


def _dot(a, b):
    return jnp.dot(a, b, preferred_element_type=F32)


def _dot_nt(a, b):
    return lax.dot_general(a, b, (((1,), (1,)), ((), ())), preferred_element_type=F32)


def _dot_tn(a, b):
    return lax.dot_general(a, b, (((0,), (0,)), ((), ())), preferred_element_type=F32)


def _pack_halves(v):
    return pltpu.pack_elementwise([v[:, :HALF_D], v[:, HALF_D:]], packed_dtype=BF16)


def _unpack_halves(p):
    words = lax.bitcast_convert_type(p, jnp.uint32)
    return tuple(pltpu.unpack_elementwise(words, index=k, packed_dtype=BF16, unpacked_dtype=F32) for k in range(2))


def _retention_tables():
    f32 = np.float32
    h = RET_HEADS
    c = RET_CHUNK
    log_gamma = np.log1p(-np.exp2(f32(-5.0) - np.arange(h, dtype=f32))).astype(f32)
    pos = np.arange(c, dtype=f32)
    diff = pos[:, None] - pos[None, :]
    decay = np.where(diff[None] >= 0.0, np.exp(np.maximum(diff, f32(0.0))[None] * log_gamma[:, None, None]), f32(0.0))
    scale = f32(RET_DK ** -0.5)
    dmask = decay * scale
    zeta = np.exp((f32(c - 1.0) - pos)[None, :] * log_gamma[:, None]) * scale
    xi = np.exp((pos + f32(1.0))[None, :] * log_gamma[:, None])
    cdec = np.exp(f32(c) * log_gamma)
    bc = lambda a: np.ascontiguousarray(np.broadcast_to(a[:, :, None], (h, c, c)), dtype=f32)
    return (dmask.astype(f32), bc(zeta), bc(xi),
            np.ascontiguousarray(np.broadcast_to(cdec[:, None, None], (h, c, c)), dtype=f32))


def _swa_tables():
    f32 = np.float32
    qi = np.arange(WINDOW)[:, None]
    sj = np.arange(2 * WINDOW)[None, :]
    rel = qi + WINDOW - sj
    ok = (rel >= 0) & (rel < WINDOW)
    slopes = np.exp2(f32(-8.0) * np.arange(1, SWA_HEADS + 1, dtype=f32) / f32(SWA_HEADS)).astype(f32)
    bias = np.where(ok[None], -slopes[:, None, None] * rel.astype(f32)[None], f32(NEG_INF)).astype(f32)
    bias = bias.reshape(SWA_KV_HEADS, SWA_GROUP * WINDOW, 2 * WINDOW)
    avg = lambda heads: jnp.asarray(np.kron(np.eye(heads, dtype=f32), np.full((SWA_HD, SWA_HD), 1.0 / SWA_HD, f32)),
                                    dtype=BF16)
    return avg(SWA_HEADS), avg(SWA_KV_HEADS), jnp.asarray(bias)


def _block_kernel(per_seq, sinks_ref, x_ref, xp_ref, g_ref, w_hbm, dmask_ref, zeta_ref, xi_ref, cdec_ref,
                  gng_ref, gnb_ref, qgain_ref, kgain_ref, bq_ref, bk_ref, bias_ref,
                  wro_hbm, wso_hbm, wout_hbm, fg_ref, wre_ref, wrg_ref, bre_ref, brg_ref,
                  x1_ref, hp_ref, route_ref, cnt_ref,
                  w_ref, wro_ref, wso_ref, wout_ref, wtmp_ref, w_sem, qg_ref, kg_ref, wr_ref, br_ref,
                  zc_ref, zn_ref, yr_ref, ys_ref, state_ref, prev_ref, band_ref, tail_ref):
    g = pl.program_id(0)
    seq_start = lax.rem(g - 1, per_seq) == 0

    @pl.when(g == 0)
    def _():
        slabs = [(src, dst, c0) for src, dst in ((w_hbm, w_ref), (wro_hbm, wro_ref), (wso_hbm, wso_ref),
                                                 (wout_hbm, wout_ref))
                 for c0 in range(0, dst.shape[1], W_SLAB)]

        def slab(k):
            src, dst, c0 = slabs[k]
            return pltpu.make_async_copy(src.at[:, pl.ds(c0, W_SLAB)],
                                         wtmp_ref.at[k % W_SLOTS, pl.ds(0, dst.shape[0])], w_sem.at[k % W_SLOTS])

        for k in range(W_SLOTS - 1):
            slab(k).start()
        for k, (_, dst, c0) in enumerate(slabs):
            if k + W_SLOTS - 1 < len(slabs):
                slab(k + W_SLOTS - 1).start()
            slab(k).wait()
            dst[:, c0:c0 + W_SLAB] = wtmp_ref[k % W_SLOTS, :dst.shape[0]].astype(BF16)
        for hd in range(SWA_HEADS):
            qg_ref[:, hd * SWA_HD:(hd + 1) * SWA_HD] = qgain_ref[...] * (SWA_HD ** -0.5)
        for kh in range(SWA_KV_HEADS):
            kg_ref[:, kh * SWA_HD:(kh + 1) * SWA_HD] = kgain_ref[...]
        wr_ref[...] = jnp.zeros_like(wr_ref)
        wr_ref[:, :N_EXPERTS] = wre_ref[...].astype(BF16)
        wr_ref[:, LANE_GROUP0:LANE_GROUP0 + N_GROUPS] = wrg_ref[...].astype(BF16)
        br_ref[...] = jnp.zeros_like(br_ref)
        br_ref[:, :N_EXPERTS] = bre_ref[...]
        br_ref[:, LANE_GROUP0:LANE_GROUP0 + N_GROUPS] = brg_ref[...]
        zc_ref[...] = jnp.zeros_like(zc_ref)
        tail_ref[...] = jnp.zeros_like(tail_ref)
        state_ref[...] = jnp.zeros_like(state_ref)

    @pl.when(seq_start)
    def _():
        state_ref[...] = jnp.zeros_like(state_ref)

    band_ref[:WINDOW, :] = tail_ref[...]
    band_ref[WINDOW:, :] = zc_ref[:, COL_SKV:COL_GATE]

    h = _rms(x_ref[...], g_ref[...]).astype(BF16)

    def project(c0, width):
        zn_ref[:, c0:c0 + width] = _dot(h, w_ref[:, c0:c0 + width]).astype(BF16)

    cw = 512
    proj_pieces = [functools.partial(project, j * cw, cw) for j in range(COL_SKV // cw)]
    proj_pieces.append(functools.partial(project, COL_SKV, Z_SKV_W))
    proj_pieces += [functools.partial(project, COL_GATE + j * cw, cw) for j in range(Z_GATE_W // cw)]

    def retention_rows(t):
        r_base = t * RET_ROWS
        n_chunks = RET_ROWS // RET_CHUNK
        part = lambda c, p, hd: zc_ref[r_base + c * RET_CHUNK:r_base + (c + 1) * RET_CHUNK,
                                       p * RET_W + hd * RET_DK:p * RET_W + (hd + 1) * RET_DK]
        for hd in range(RET_HEADS):
            state = state_ref[hd]
            for c in range(n_chunks):
                prev_ref[t * n_chunks + c, hd] = state.astype(BF16)
                kz = (part(c, 1, hd).astype(F32) * zeta_ref[hd]).astype(BF16)
                state = cdec_ref[hd] * state + _dot_tn(kz, part(c, 2, hd))
            state_ref[hd] = state
        out_rows = []
        for c in range(n_chunks):
            heads = []
            for hd in range(RET_HEADS):
                q = part(c, 0, hd)
                scores = _dot_nt(q, part(c, 1, hd)) * dmask_ref[hd]
                y = _dot(scores.astype(BF16), part(c, 2, hd)) + _dot(q, prev_ref[t * n_chunks + c, hd]) * xi_ref[hd]
                mu = jnp.mean(y, axis=-1, keepdims=True)
                d = y - mu
                var = jnp.mean(d * d, axis=-1, keepdims=True)
                hs = slice(hd * RET_DV, (hd + 1) * RET_DV)
                yn = d * lax.rsqrt(var + EPS) * gng_ref[:, hs] + gnb_ref[:, hs]
                gate = part(c, 3, hd).astype(F32)
                heads.append((gate * _sigmoid(gate) * yn).astype(BF16))
            out_rows.append(jnp.concatenate(heads, axis=1))
        yr_ref[r_base:r_base + RET_ROWS, :] = jnp.concatenate(out_rows, axis=0)

    group_rows = SWA_GROUP * WINDOW
    lo_q = lax.broadcasted_iota(jnp.int32, (WINDOW, 2 * SWA_HD), 1) < SWA_HD
    lo_k = lax.broadcasted_iota(jnp.int32, (2 * WINDOW, 2 * SWA_HD), 1) < SWA_HD
    in_prev = lax.broadcasted_iota(jnp.int32, (group_rows, 2 * WINDOW), 1) < WINDOW
    row_head = lax.broadcasted_iota(jnp.int32, (group_rows, 1), 0) // WINDOW

    def both_heads(v):
        swapped = pltpu.roll(v, SWA_HD, axis=1)
        return jnp.where(lo_k, v, swapped), jnp.where(lo_k, swapped, v)

    def swa_block(n):
        rows = slice(n * WINDOW, (n + 1) * WINDOW)
        kv = band_ref[n * WINDOW:(n + 2) * WINDOW, :]
        kf = kv[:, :SWA_KV].astype(F32)
        kn = kf * lax.rsqrt(_dot((kf * kf).astype(BF16), bk_ref[...]) + EPS) * kg_ref[...]
        keys = [a.astype(BF16) for a in both_heads(kn)]
        vals = [a.astype(BF16) for a in both_heads(kv[:, SWA_KV:].astype(F32))]
        qf = zc_ref[rows, COL_SQ:COL_SKV].astype(F32)
        qn = qf * lax.rsqrt(_dot((qf * qf).astype(BF16), bq_ref[...]) + EPS) * qg_ref[...]
        outs = []
        for kh in range(SWA_KV_HEADS):
            parts = []
            for j in range(SWA_GROUP // 2):
                c0 = (kh * SWA_GROUP + 2 * j) * SWA_HD
                two = qn[:, c0:c0 + 2 * SWA_HD]
                parts += [jnp.where(lo_q, two, 0.0), jnp.where(lo_q, 0.0, two)]
            qs = jnp.concatenate(parts, axis=0).astype(BF16)
            s = _dot_nt(qs, keys[kh]) + bias_ref[kh]
            if n == 0:
                s = jnp.where(jnp.logical_and(seq_start, in_prev), NEG_INF, s)
            sink = sinks_ref[kh * SWA_GROUP]
            for gi in range(1, SWA_GROUP):
                sink = jnp.where(row_head == gi, sinks_ref[kh * SWA_GROUP + gi], sink)
            m = jnp.maximum(jnp.max(s, axis=-1, keepdims=True), sink)
            p = jnp.exp(s - m)
            denom = jnp.sum(p, axis=-1, keepdims=True) + jnp.exp(sink - m)
            o = _dot(p.astype(BF16), vals[kh]) * (1.0 / denom)
            for j in range(SWA_GROUP // 2):
                even = o[(2 * j) * WINDOW:(2 * j + 1) * WINDOW]
                odd = o[(2 * j + 1) * WINDOW:(2 * j + 2) * WINDOW]
                outs.append(jnp.where(lo_q, even, odd).astype(BF16))
        ys_ref[rows, :] = jnp.concatenate(outs, axis=1)

    mix_pieces = [functools.partial(retention_rows, t) for t in range(BLOCK_ROWS // RET_ROWS)]
    mix_pieces += [functools.partial(swa_block, n) for n in range(BLOCK_ROWS // WINDOW)]

    subs = []

    def merge_sub(s):
        subs.append(_merge_rows(slice(s * MERGE_SUB, (s + 1) * MERGE_SUB), yr_ref, ys_ref, zc_ref, xp_ref, wro_ref,
                                wso_ref, wout_ref, fg_ref, wr_ref, br_ref))

    other_pieces = mix_pieces + [functools.partial(merge_sub, s) for s in range(BLOCK_ROWS // MERGE_SUB)]

    n_other = len(other_pieces)
    for k, piece in enumerate(other_pieces):
        for proj_piece in proj_pieces[k * len(proj_pieces) // n_other:(k + 1) * len(proj_pieces) // n_other]:
            proj_piece()
        piece()
    x1_ref[...] = jnp.concatenate([s[0] for s in subs], axis=0)
    hp_ref[...] = jnp.concatenate([s[1] for s in subs], axis=0)
    route_ref[...] = jnp.concatenate([s[2] for s in subs], axis=0)
    for s, sub in enumerate(subs):
        cnt_ref[s] = sub[3]

    tail_ref[...] = zc_ref[BLOCK_ROWS - WINDOW:, COL_SKV:COL_GATE]
    zc_ref[...] = zn_ref[...]


def _merge_rows(rows, yr_ref, ys_ref, z_ref, x_ref, wro_ref, wso_ref, wout_ref, fg_ref, wr_ref, br_ref):
    a = _dot(yr_ref[rows, :], wro_ref[...])
    b = _dot(ys_ref[rows, :], wso_ref[...])
    gate_r = z_ref[rows, COL_GATE:COL_GATE + D_MODEL].astype(F32)
    gate_s = z_ref[rows, COL_GATE + D_MODEL:].astype(F32)
    merged = (_sigmoid(gate_r) * a + _sigmoid(gate_s) * b).astype(BF16)
    x1 = x_ref[rows, :] + _dot(merged, wout_ref[...])
    h2 = _rms(x1, fg_ref[...])
    packed = _pack_halves(h2)

    logits = _dot(h2.astype(BF16), wr_ref[...]) + br_ref[...]
    lane = lax.broadcasted_iota(jnp.int32, logits.shape, 1)
    big = jnp.int32(ROUTE_LANES)
    is_group = jnp.logical_and(lane >= LANE_GROUP0, lane < LANE_GROUP0 + N_GROUPS)
    gl = jnp.where(is_group, logits, NEG_INF)
    gmax = jnp.max(gl, axis=-1, keepdims=True)
    g_w = 1.0 / jnp.sum(jnp.exp(gl - gmax), axis=-1, keepdims=True)
    g_sel = jnp.min(jnp.where(gl == gmax, lane, big), axis=-1, keepdims=True) - LANE_GROUP0
    in_group = jnp.logical_and(lane < N_EXPERTS, (lane >> 3) == g_sel)
    el = jnp.where(in_group, logits, NEG_INF)
    m1 = jnp.max(el, axis=-1, keepdims=True)
    i1 = jnp.min(jnp.where(el == m1, lane, big), axis=-1, keepdims=True)
    el2 = jnp.where(lane == i1, NEG_INF, el)
    m2 = jnp.max(el2, axis=-1, keepdims=True)
    i2 = jnp.min(jnp.where(el2 == m2, lane, big), axis=-1, keepdims=True)
    e2 = jnp.exp(m2 - m1)
    c1 = g_w / (1.0 + e2)
    c2 = g_w * e2 / (1.0 + e2)
    chosen = jnp.where(jnp.logical_or(lane == i1, lane == i2), 1.0, 0.0)
    route = (chosen
             + jnp.where(lane == LANE_C1, c1, 0.0)
             + jnp.where(lane == LANE_C2, c2, 0.0)
             + jnp.where(lane == i2 + LANE_SECOND0, 1.0, 0.0))
    return x1, packed, route, jnp.sum(chosen, axis=0, keepdims=True)


def _block(x2d, mix_g, w_in, gn_g, gn_b, sinks, q_g, k_g, w_ret_o, w_swa_o, w_out, ffn_g, router, seq):
    t = x2d.shape[0]
    n = t // BLOCK_ROWS
    assert seq % BLOCK_ROWS == 0
    consts = _retention_tables() + (gn_g, gn_b, q_g, k_g) + _swa_tables()
    merge_weights = (w_ret_o, w_swa_o, w_out)
    small = (ffn_g,) + tuple(router)
    hbm = pl.BlockSpec(memory_space=pl.ANY)
    full = lambda a: pl.BlockSpec(a.shape, lambda g: (0,) * a.ndim)
    cur = lambda w: pl.BlockSpec((BLOCK_ROWS, w), lambda g: (jnp.minimum(g, n - 1), 0))
    prv = lambda w: pl.BlockSpec((BLOCK_ROWS, w), lambda g: (jnp.maximum(g - 1, 0), 0))
    chunks = BLOCK_ROWS // RET_CHUNK
    subs_per_block = BLOCK_ROWS // MERGE_SUB
    return pl.pallas_call(
        functools.partial(_block_kernel, seq // BLOCK_ROWS),
        grid=(n + 1,),
        in_specs=([pl.BlockSpec(memory_space=pltpu.SMEM), cur(D_MODEL), prv(D_MODEL), full(mix_g), hbm]
                  + [full(a) for a in consts] + [hbm] * len(merge_weights) + [full(a) for a in small]),
        out_specs=[prv(D_MODEL), prv(HALF_D), prv(ROUTE_LANES),
                   pl.BlockSpec((subs_per_block, 1, ROUTE_LANES), lambda g: (jnp.maximum(g - 1, 0), 0, 0))],
        out_shape=[jax.ShapeDtypeStruct((t, D_MODEL), F32),
                   jax.ShapeDtypeStruct((t, HALF_D), I32),
                   jax.ShapeDtypeStruct((t, ROUTE_LANES), F32),
                   jax.ShapeDtypeStruct((n * subs_per_block, 1, ROUTE_LANES), F32)],
        scratch_shapes=[pltpu.VMEM(w_in.shape, BF16)] + [pltpu.VMEM(w.shape, BF16) for w in merge_weights]
                       + [pltpu.VMEM((W_SLOTS, D_MODEL, W_SLAB), F32), pltpu.SemaphoreType.DMA((W_SLOTS,)),
                        pltpu.VMEM((1, SWA_Q), F32), pltpu.VMEM((1, SWA_KV), F32),
                        pltpu.VMEM((D_MODEL, ROUTE_LANES), BF16), pltpu.VMEM((1, ROUTE_LANES), F32),
                        pltpu.VMEM((BLOCK_ROWS, IN_WIDTH), BF16), pltpu.VMEM((BLOCK_ROWS, IN_WIDTH), BF16),
                        pltpu.VMEM((BLOCK_ROWS, RET_W), BF16), pltpu.VMEM((BLOCK_ROWS, SWA_Q), BF16),
                        pltpu.VMEM((RET_HEADS, RET_DK, RET_DV), F32),
                        pltpu.VMEM((chunks, RET_HEADS, RET_DK, RET_DV), BF16),
                        pltpu.VMEM((BLOCK_ROWS + WINDOW, Z_SKV_W), BF16), pltpu.VMEM((WINDOW, Z_SKV_W), BF16)],
        compiler_params=pltpu.CompilerParams(dimension_semantics=("arbitrary",),
                                             vmem_limit_bytes=VMEM_LIMIT),
        name="block",
    )(sinks, x2d, x2d, mix_g, w_in, *consts, *merge_weights, *small)


def _plan_tables(cnt_ref, base_ref, meta_ref):
    counts = cnt_ref[...]
    nt = counts.shape[0]
    exact = functools.partial(jnp.dot, precision=lax.Precision.HIGHEST, preferred_element_type=F32)
    total = jnp.sum(counts, axis=0, keepdims=True)
    tiles_e = jnp.ceil(total * (1.0 / ROW_TILE))
    k = lax.broadcasted_iota(jnp.int32, (ROUTE_LANES, ROUTE_LANES), 0)
    lane = lax.broadcasted_iota(jnp.int32, (ROUTE_LANES, ROUTE_LANES), 1)
    tile_end = exact(jnp.broadcast_to(tiles_e, (8, ROUTE_LANES)), jnp.where(k <= lane, 1.0, 0.0))[0:1]
    tile_start = tile_end - tiles_e
    r = lax.broadcasted_iota(jnp.int32, (nt, nt), 0)
    c = lax.broadcasted_iota(jnp.int32, (nt, nt), 1)
    before = exact(jnp.where(c < r, 1.0, 0.0), counts)
    base_ref[...] = tile_start * ROW_TILE + before

    used = jnp.max(tile_end, axis=-1, keepdims=True)
    tile = k.astype(F32)
    clamped = jnp.minimum(tile, jnp.maximum(used - 1.0, 0.0))
    is_expert = lane < N_EXPERTS
    owner = jnp.sum(jnp.where(jnp.logical_and(is_expert, tile_end <= clamped), 1.0, 0.0), axis=-1, keepdims=True)
    owner = jnp.minimum(owner, N_EXPERTS - 1.0)
    mine = lane.astype(F32) == owner
    pick = lambda v: jnp.sum(jnp.where(mine, v, 0.0), axis=-1, keepdims=True)
    t1 = tile[:, 0:1]
    valid = jnp.clip(pick(total) - (t1 - pick(tile_start)) * ROW_TILE, 0.0, float(ROW_TILE))
    valid = jnp.where(t1 < used, valid, 0.0)

    nonempty = jnp.where(jnp.logical_and(lane[0:1] < N_EXPERTS, tiles_e > 0.0), 1.0, 0.0)
    rank = exact(jnp.broadcast_to(nonempty, (8, ROUTE_LANES)), jnp.where(k < lane, 1.0, 0.0))[0:1]
    nonempty_col = jnp.sum(jnp.where(k == lane, jnp.broadcast_to(nonempty, k.shape), 0.0), axis=-1, keepdims=True)
    later = jnp.where(jnp.logical_and(k > lane, nonempty_col > 0.0), tile, float(ROUTE_LANES))
    nxt_e = jnp.min(later, axis=0, keepdims=True)
    nxt_e = jnp.where(nxt_e < float(ROUTE_LANES), nxt_e, -1.0)
    first = jnp.where(jnp.logical_and(t1 == pick(tile_start), t1 < used), 1.0, 0.0)
    my_rank = pick(rank)
    slot = my_rank - 2.0 * jnp.floor(my_rank * 0.5)
    fields = (owner, valid, first, pick(nxt_e), slot, jnp.broadcast_to(used, owner.shape))
    by_tile = sum(jnp.where(lane == n, f, 0.0) for n, f in enumerate(fields))
    meta_ref[...] = by_tile.T[:8, :].astype(I32)


def _plan_kernel(cnt_ref, route_ref, base_ref, meta_ref, pos1_ref, pos2_ref):
    step = pl.program_id(0)
    nt = cnt_ref.shape[0]
    subs_per_step = nt // PLAN_STEPS

    @pl.when(step == 0)
    def _():
        _plan_tables(cnt_ref, base_ref, meta_ref)
        pos1_ref[...] = jnp.zeros_like(pos1_ref)
        pos2_ref[...] = jnp.zeros_like(pos2_ref)

    sel_row = lax.broadcasted_iota(jnp.int32, (8, ROUTE_LANES), 0)
    sel_lane = lax.broadcasted_iota(jnp.int32, (8, ROUTE_LANES), 1)
    tok_r = lax.broadcasted_iota(jnp.int32, (MERGE_SUB, MERGE_SUB), 0)
    tok_c = lax.broadcasted_iota(jnp.int32, (MERGE_SUB, MERGE_SUB), 1)
    eye = jnp.where(tok_r == tok_c, 1.0, 0.0).astype(BF16)
    upto = jnp.where(tok_r <= tok_c, 1.0, 0.0).astype(BF16)
    sum_experts = lambda a: jnp.sum(a, axis=0, keepdims=True)

    def sub_positions(local, carry):
        s = step * subs_per_step + local
        route = route_ref[pl.ds(pl.multiple_of(local * MERGE_SUB, MERGE_SUB), MERGE_SUB), :].astype(BF16)
        record_t = _dot_tn(route, eye)
        incl_t = _dot_tn(route, upto)[:N_EXPERTS]
        both_t = record_t[:N_EXPERTS]
        second_t = record_t[LANE_SECOND0:LANE_SECOND0 + N_EXPERTS]
        rank2 = sum_experts(second_t * incl_t) - 1.0
        rank1 = sum_experts(both_t * incl_t) - rank2 - 2.0
        base_row = jnp.where(sel_lane < N_EXPERTS, jnp.broadcast_to(base_ref[pl.ds(s, 1), :], (8, ROUTE_LANES)), 0.0)
        hi = jnp.floor(base_row * (1.0 / 256.0))
        lo = base_row - 256.0 * hi
        hi2 = pltpu.roll(hi, LANE_SECOND0, axis=1)
        lo2 = pltpu.roll(lo, LANE_SECOND0, axis=1)
        table = (jnp.where(sel_row == 0, hi, 0.0) + jnp.where(sel_row == 1, lo, 0.0)
                 + jnp.where(sel_row == 2, hi2, 0.0) + jnp.where(sel_row == 3, lo2, 0.0))
        r = _dot_nt(table.astype(BF16), route)
        base_both = 256.0 * r[0:1] + r[1:2]
        base2 = 256.0 * r[2:3] + r[3:4]
        worker = s // subs_per_worker
        chunk0 = lax.rem(s, subs_per_worker) * chunks_per_sub
        for ref, row in ((pos1_ref, base_both - base2 + rank1), (pos2_ref, base2 + rank2)):
            for c in range(chunks_per_sub):
                ref[worker, pl.ds(chunk0 + c, 1), :] = row[:, c * SC_CHUNK:(c + 1) * SC_CHUNK].astype(I32)
        return carry

    chunks_per_sub = MERGE_SUB // SC_CHUNK
    subs_per_worker = nt // SC_WORKERS
    lax.fori_loop(0, subs_per_step, sub_positions, 0, unroll=8)


def _tile_plan(cnt, route, n_tiles):
    assert n_tiles <= ROUTE_LANES
    nt = cnt.shape[0]
    assert nt % SC_WORKERS == 0 and MERGE_SUB % SC_CHUNK == 0 and nt * MERGE_SUB <= SC_WORKERS * SC_IDX_ROWS * SC_CHUNK
    assert nt % PLAN_STEPS == 0
    pos = jax.ShapeDtypeStruct((SC_WORKERS, SC_IDX_ROWS, SC_CHUNK), I32)
    outs = [jax.ShapeDtypeStruct((nt, ROUTE_LANES), F32), jax.ShapeDtypeStruct((8, ROUTE_LANES), I32), pos, pos]
    whole = lambda a: pl.BlockSpec(a.shape, lambda i: (0,) * len(a.shape))
    _, meta, pos1, pos2 = pl.pallas_call(
        _plan_kernel,
        grid=(PLAN_STEPS,),
        in_specs=[pl.BlockSpec((nt, ROUTE_LANES), lambda i: (0, 0)),
                  pl.BlockSpec((route.shape[0] // PLAN_STEPS, ROUTE_LANES), lambda i: (i, 0))],
        out_specs=[whole(o) for o in outs],
        out_shape=outs,
        compiler_params=pltpu.CompilerParams(dimension_semantics=("arbitrary",), vmem_limit_bytes=VMEM_LIMIT),
        name="tile_plan",
    )(cnt.reshape(nt, ROUTE_LANES), route)
    return meta, pos1, pos2


def _sc_mesh():
    return plsc.VectorSubcoreMesh(core_axis_name="c", subcore_axis_name="s")


def _sc_worker():
    return lax.axis_index("s") * SC_CORES + lax.axis_index("c")


def _dispatch(hp, pos1, pos2, n_rows):
    t = hp.shape[0]
    per_worker = t // SC_WORKERS
    k = per_worker // SC_CHUNK
    idx = pltpu.VMEM((SC_IDX_ROWS, SC_CHUNK), I32)

    @functools.partial(pl.kernel, mesh=_sc_mesh(), out_type=jax.ShapeDtypeStruct((n_rows, HALF_D), I32),
                       scratch_types=[idx, idx, pltpu.VMEM((SC_CHUNK, HALF_D), I32)], name="moe_dispatch")
    def run(h_hbm, p1_hbm, p2_hbm, xs_hbm, p1_v, p2_v, rows_v):
        wid = _sc_worker()
        pltpu.sync_copy(p1_hbm.at[wid], p1_v)
        pltpu.sync_copy(p2_hbm.at[wid], p2_v)
        for j in range(k):
            pltpu.sync_copy(h_hbm.at[pl.ds(wid * per_worker + j * SC_CHUNK, SC_CHUNK)], rows_v)
            pltpu.sync_copy(rows_v, xs_hbm.at[p1_v.at[j]])
            pltpu.sync_copy(rows_v, xs_hbm.at[p2_v.at[j]])

    return run(hp, pos1, pos2)


def _combine_gather(ys, pos1, pos2, t):
    per_worker = t // SC_WORKERS
    k = per_worker // SC_CHUNK
    idx = pltpu.VMEM((SC_IDX_ROWS, SC_CHUNK), I32)
    out = jax.ShapeDtypeStruct((t, HALF_D), I32)

    @functools.partial(pl.kernel, mesh=_sc_mesh(), out_type=(out, out),
                       scratch_types=[idx, idx, pltpu.VMEM((SC_CHUNK, HALF_D), I32)], name="moe_combine")
    def run(ys_hbm, p1_hbm, p2_hbm, g1_hbm, g2_hbm, p1_v, p2_v, rows_v):
        wid = _sc_worker()
        pltpu.sync_copy(p1_hbm.at[wid], p1_v)
        pltpu.sync_copy(p2_hbm.at[wid], p2_v)
        for j in range(k):
            dst = pl.ds(wid * per_worker + j * SC_CHUNK, SC_CHUNK)
            pltpu.sync_copy(ys_hbm.at[p1_v.at[j]], rows_v)
            pltpu.sync_copy(rows_v, g1_hbm.at[dst])
            pltpu.sync_copy(ys_hbm.at[p2_v.at[j]], rows_v)
            pltpu.sync_copy(rows_v, g2_hbm.at[dst])

    return run(ys, pos1, pos2)


META_OWNER, META_VALID, META_FIRST, META_NEXT, META_SLOT, META_USED = range(6)


def _experts_kernel(meta_ref, xs_hbm, wg_hbm, wu_hbm, wd_hbm, ys_ref,
                    xs_buf, xs_sem, wg_f, wu_f, wd_f, wg_bf, wu_bf, wd_bf, w_sem):
    i = pl.program_id(0)
    used = meta_ref[META_USED, 0]
    valid = meta_ref[META_VALID, i]
    owner = meta_ref[META_OWNER, i]
    nxt = meta_ref[META_NEXT, i]

    def rows_in(tile):
        start = pl.multiple_of(tile * ROW_TILE, ROW_TILE)
        slot = lax.rem(tile, XS_SLOTS)
        return pltpu.make_async_copy(xs_hbm.at[pl.ds(start, ROW_TILE)], xs_buf.at[slot], xs_sem.at[slot])

    @pl.when(i == 0)
    def _():
        for ahead in range(XS_SLOTS - 1):
            pl.when(ahead < used)(rows_in(ahead).start)

    @pl.when(i + (XS_SLOTS - 1) < used)
    def _():
        rows_in(i + (XS_SLOTS - 1)).start()

    def fetch(expert, slot):
        pairs = ((wg_hbm, wg_f), (wu_hbm, wu_f), (wd_hbm, wd_f))
        return [pltpu.make_async_copy(w.at[expert], buf.at[slot], w_sem.at[slot, n])
                for n, (w, buf) in enumerate(pairs)]

    @pl.when(i == 0)
    def _():
        for copy in fetch(owner, 0):
            copy.start()

    @pl.when(meta_ref[META_FIRST, i] == 1)
    def _():
        slot = meta_ref[META_SLOT, i]
        for copy in fetch(owner, slot):
            copy.wait()
        wg_bf[...] = wg_f[slot].astype(BF16)
        wu_bf[...] = wu_f[slot].astype(BF16)
        wd_bf[...] = wd_f[slot].astype(BF16)

        @pl.when(nxt >= 0)
        def _():
            for copy in fetch(nxt, 1 - slot):
                copy.start()

    @pl.when(i < used)
    def _():
        rows_in(i).wait()
        row = lax.broadcasted_iota(jnp.int32, (ROW_TILE, HALF_D), 0)
        words = xs_buf[lax.rem(i, XS_SLOTS)]
        packed = jnp.where(row < valid, words, 0)
        hi, lo = _unpack_halves(packed)
        hi = hi.astype(BF16)
        lo = lo.astype(BF16)
        hg = _dot(hi, wg_bf[:HALF_D, :]) + _dot(lo, wg_bf[HALF_D:, :])
        hu = _dot(hi, wu_bf[:HALF_D, :]) + _dot(lo, wu_bf[HALF_D:, :])
        hid = (hg * _sigmoid(hg) * hu).astype(BF16)
        ys_ref[...] = _pack_halves(_dot(hid, wd_bf[...]))


def _experts(xs, meta, n_tiles, wg, wu, wd):
    tile = lambda i, meta: (jnp.minimum(i, meta[META_USED, 0] - 1), 0)
    hbm = pl.BlockSpec(memory_space=pl.ANY)
    up_shape, down_shape = (D_MODEL, EXPERT_FF), (EXPERT_FF, D_MODEL)
    grid_spec = pltpu.PrefetchScalarGridSpec(
        num_scalar_prefetch=1,
        grid=(n_tiles,),
        in_specs=[hbm, hbm, hbm, hbm],
        out_specs=pl.BlockSpec((ROW_TILE, HALF_D), tile),
        scratch_shapes=[pltpu.VMEM((XS_SLOTS, ROW_TILE, HALF_D), I32), pltpu.SemaphoreType.DMA((XS_SLOTS,)),
                        pltpu.VMEM((2,) + up_shape, F32), pltpu.VMEM((2,) + up_shape, F32),
                        pltpu.VMEM((2,) + down_shape, F32),
                        pltpu.VMEM(up_shape, BF16), pltpu.VMEM(up_shape, BF16), pltpu.VMEM(down_shape, BF16),
                        pltpu.SemaphoreType.DMA((2, 3))],
    )
    return pl.pallas_call(
        _experts_kernel,
        grid_spec=grid_spec,
        out_shape=jax.ShapeDtypeStruct((n_tiles * ROW_TILE, HALF_D), I32),
        compiler_params=pltpu.CompilerParams(dimension_semantics=("arbitrary",),
                                             vmem_limit_bytes=VMEM_LIMIT),
        name="experts",
    )(meta, xs, wg, wu, wd)


def _ple_kernel(x1_hbm, g1_hbm, g2_hbm, route_ref, p_ref, wp_f32, pg_ref, gg_ref, wgate_f32, o_ref,
                wp_ref, wgate_ref, x1_buf, g1_buf, g2_buf, ring_sem):
    i = pl.program_id(0)
    steps = pl.num_programs(0)
    streams = ((x1_hbm, x1_buf), (g1_hbm, g1_buf), (g2_hbm, g2_buf))

    def rows_in(step):
        start = pl.multiple_of(step * PLE_TM, PLE_TM)
        slot = lax.rem(step, PLE_SLOTS)
        return [pltpu.make_async_copy(src.at[pl.ds(start, PLE_TM)], buf.at[slot], ring_sem.at[n, slot])
                for n, (src, buf) in enumerate(streams)]

    def start_rows(step):
        for n, copy in enumerate(rows_in(step)):
            copy.start(priority=min(n, 1))

    @pl.when(i == 0)
    def _():
        for ahead in range(PLE_SLOTS - 1):
            pl.when(ahead < steps)(functools.partial(start_rows, ahead))
        wp_ref[...] = wp_f32[...].astype(BF16)
        wgate_ref[...] = wgate_f32[...].astype(BF16)

    pl.when(i + (PLE_SLOTS - 1) < steps)(functools.partial(start_rows, i + (PLE_SLOTS - 1)))

    for copy in rows_in(i):
        copy.wait()
    slot = lax.rem(i, PLE_SLOTS)
    x1_ref, g1_ref, g2_ref = x1_buf.at[slot], g1_buf.at[slot], g2_buf.at[slot]

    route = route_ref[...]
    lane = lax.broadcasted_iota(jnp.int32, route.shape, 1)
    pick = lambda ln: jnp.sum(jnp.where(lane == ln, route, 0.0), axis=-1, keepdims=True)
    c1 = pick(LANE_C1)
    c2 = pick(LANE_C2)
    a_hi, a_lo = _unpack_halves(g1_ref[...])
    b_hi, b_lo = _unpack_halves(g2_ref[...])
    x = x1_ref[...] + jnp.concatenate([c1 * a_hi + c2 * b_hi, c1 * a_lo + c2 * b_lo], axis=1)
    ple = _rms(_dot(p_ref[...].astype(BF16), wp_ref[...]), pg_ref[...])
    gate = _sigmoid(_dot(_rms(x, gg_ref[...]).astype(BF16), wgate_ref[...]))
    o_ref[...] = x + gate * ple


def _ple(x1, g1, g2, route, p2d, w_ple, ple_g, gate_g, w_gate):
    t = x1.shape[0]
    row = lambda w: pl.BlockSpec((PLE_TM, w), lambda i: (i, 0))
    full = lambda a: pl.BlockSpec(a.shape, lambda i: (0,) * a.ndim)
    hbm = pl.BlockSpec(memory_space=pl.ANY)
    return pl.pallas_call(
        _ple_kernel,
        grid=(t // PLE_TM,),
        in_specs=[hbm, hbm, hbm, row(ROUTE_LANES), row(PLE_DIM),
                  full(w_ple), full(ple_g), full(gate_g), full(w_gate)],
        out_specs=row(D_MODEL),
        out_shape=jax.ShapeDtypeStruct((t, D_MODEL), F32),
        scratch_shapes=[pltpu.VMEM(w_ple.shape, BF16), pltpu.VMEM(w_gate.shape, BF16),
                        pltpu.VMEM((PLE_SLOTS, PLE_TM, D_MODEL), F32),
                        pltpu.VMEM((PLE_SLOTS, PLE_TM, HALF_D), I32), pltpu.VMEM((PLE_SLOTS, PLE_TM, HALF_D), I32),
                        pltpu.SemaphoreType.DMA((3, PLE_SLOTS))],
        compiler_params=pltpu.CompilerParams(dimension_semantics=("arbitrary",),
                                             vmem_limit_bytes=VMEM_LIMIT),
        name="ple",
    )(x1, g1, g2, route, p2d, w_ple, ple_g, gate_g, w_gate)


def kernel(x, p, mix_norm_g, w_in, ret_gn_g, ret_gn_b, w_ret_o, q_norm_g, k_norm_g, attn_sinks,
           w_swa_o, w_out, ffn_norm_g, w_router_group, b_router_group, w_router_expert,
           b_router_expert, w_exp_gate, w_exp_up, w_exp_down, ple_gate_norm_g, w_ple_gate,
           w_ple, ple_norm_g):
    batch, seq, d = x.shape
    t = batch * seq
    depth = w_in.shape[0]
    n_tiles = 2 * t // ROW_TILE + N_EXPERTS
    row = lambda a: a.reshape(1, -1)
    x2d = x.reshape(t, d)
    for i in range(depth):
        router = (w_router_expert[i], w_router_group[i], row(b_router_expert[i]), row(b_router_group[i]))
        x1, hp, route, cnt = _block(x2d, row(mix_norm_g[i]), w_in[i], row(ret_gn_g[i]),
                                    row(ret_gn_b[i]), attn_sinks[i], row(q_norm_g[i]), row(k_norm_g[i]),
                                    w_ret_o[i], w_swa_o[i], w_out[i], row(ffn_norm_g[i]), router, seq)
        meta, pos1, pos2 = _tile_plan(cnt, route, n_tiles)
        xs = _dispatch(hp, pos1, pos2, n_tiles * ROW_TILE)
        ys = _experts(xs, meta, n_tiles, w_exp_gate[i], w_exp_up[i], w_exp_down[i])
        g1, g2 = _combine_gather(ys, pos1, pos2, t)
        x2d = _ple(x1, g1, g2, route, p[i].reshape(t, PLE_DIM), w_ple[i], row(ple_norm_g[i]),
                   row(ple_gate_norm_g[i]), w_ple_gate[i])
    return x2d.reshape(batch, seq, d)
```

```python
import functools

import jax
import jax.numpy as jnp
import numpy as np
from jax import lax
from jax.experimental import pallas as pl
from jax.experimental.pallas import tpu as pltpu
from jax.experimental.pallas import tpu_sc as plsc

F32 = jnp.float32
BF16 = jnp.bfloat16
I32 = jnp.int32

EPS = 1e-6
D_MODEL = 1024
PLE_DIM = 256
RET_HEADS = 4
RET_DK = 128
RET_DV = 128
RET_CHUNK = 128
SWA_HEADS = 8
SWA_KV_HEADS = 2
SWA_GROUP = SWA_HEADS // SWA_KV_HEADS
SWA_HD = 64
WINDOW = 128
N_GROUPS = 4
EXPERTS_PER_GROUP = 8
N_EXPERTS = N_GROUPS * EXPERTS_PER_GROUP
EXPERT_FF = 256

RET_W = RET_HEADS * RET_DK
SWA_Q = SWA_HEADS * SWA_HD
SWA_KV = SWA_KV_HEADS * SWA_HD
Z_RET_W = 4 * RET_W
Z_SKV_W = 2 * SWA_KV
Z_GATE_W = 2 * D_MODEL
COL_SQ = Z_RET_W
COL_SKV = COL_SQ + SWA_Q
COL_GATE = COL_SKV + Z_SKV_W
IN_WIDTH = COL_GATE + Z_GATE_W

ROUTE_LANES = 128
LANE_GROUP0 = N_EXPERTS
LANE_C1, LANE_C2 = 34, 35
LANE_SECOND0 = 64
NEG_INF = -1e30

HALF_D = D_MODEL // 2
ROW_TILE = 512
XS_SLOTS = 3
SC_CORES = 2
SC_SUBCORES = 16
SC_WORKERS = SC_CORES * SC_SUBCORES
SC_CHUNK = 128
SC_IDX_ROWS = 8

VMEM_LIMIT = 56 * 1024 * 1024

BLOCK_ROWS = 512
W_SLAB = 256
W_SLOTS = 6
RET_ROWS = 512
MERGE_SUB = 256
PLAN_STEPS = 4
PLE_TM = 1024
PLE_SLOTS = 3


def _rms(x, g):
    ms = jnp.mean(x * x, axis=-1, keepdims=True)
    return x * lax.rsqrt(ms + EPS) * g


def _sigmoid(x):
    return 1.0 / (1.0 + jnp.exp(-x))


def _dot(a, b):
    return jnp.dot(a, b, preferred_element_type=F32)


def _dot_nt(a, b):
    return lax.dot_general(a, b, (((1,), (1,)), ((), ())), preferred_element_type=F32)


def _dot_tn(a, b):
    return lax.dot_general(a, b, (((0,), (0,)), ((), ())), preferred_element_type=F32)


def _pack_halves(v):
    return pltpu.pack_elementwise([v[:, :HALF_D], v[:, HALF_D:]], packed_dtype=BF16)


def _unpack_halves(p):
    words = lax.bitcast_convert_type(p, jnp.uint32)
    return tuple(pltpu.unpack_elementwise(words, index=k, packed_dtype=BF16, unpacked_dtype=F32) for k in range(2))


def _retention_tables():
    f32 = np.float32
    h = RET_HEADS
    c = RET_CHUNK
    log_gamma = np.log1p(-np.exp2(f32(-5.0) - np.arange(h, dtype=f32))).astype(f32)
    pos = np.arange(c, dtype=f32)
    diff = pos[:, None] - pos[None, :]
    decay = np.where(diff[None] >= 0.0, np.exp(np.maximum(diff, f32(0.0))[None] * log_gamma[:, None, None]), f32(0.0))
    scale = f32(RET_DK ** -0.5)
    dmask = decay * scale
    zeta = np.exp((f32(c - 1.0) - pos)[None, :] * log_gamma[:, None]) * scale
    xi = np.exp((pos + f32(1.0))[None, :] * log_gamma[:, None])
    cdec = np.exp(f32(c) * log_gamma)
    bc = lambda a: np.ascontiguousarray(np.broadcast_to(a[:, :, None], (h, c, c)), dtype=f32)
    return (dmask.astype(f32), bc(zeta), bc(xi),
            np.ascontiguousarray(np.broadcast_to(cdec[:, None, None], (h, c, c)), dtype=f32))


def _swa_tables():
    f32 = np.float32
    qi = np.arange(WINDOW)[:, None]
    sj = np.arange(2 * WINDOW)[None, :]
    rel = qi + WINDOW - sj
    ok = (rel >= 0) & (rel < WINDOW)
    slopes = np.exp2(f32(-8.0) * np.arange(1, SWA_HEADS + 1, dtype=f32) / f32(SWA_HEADS)).astype(f32)
    bias = np.where(ok[None], -slopes[:, None, None] * rel.astype(f32)[None], f32(NEG_INF)).astype(f32)
    bias = bias.reshape(SWA_KV_HEADS, SWA_GROUP * WINDOW, 2 * WINDOW)
    avg = lambda heads: jnp.asarray(np.kron(np.eye(heads, dtype=f32), np.full((SWA_HD, SWA_HD), 1.0 / SWA_HD, f32)),
                                    dtype=BF16)
    return avg(SWA_HEADS), avg(SWA_KV_HEADS), jnp.asarray(bias)


def _block_kernel(per_seq, sinks_ref, x_ref, xp_ref, g_ref, w_hbm, dmask_ref, zeta_ref, xi_ref, cdec_ref,
                  gng_ref, gnb_ref, qgain_ref, kgain_ref, bq_ref, bk_ref, bias_ref,
                  wro_hbm, wso_hbm, wout_hbm, fg_ref, wre_ref, wrg_ref, bre_ref, brg_ref,
                  x1_ref, hp_ref, route_ref, cnt_ref,
                  w_ref, wro_ref, wso_ref, wout_ref, wtmp_ref, w_sem, qg_ref, kg_ref, wr_ref, br_ref,
                  zc_ref, zn_ref, yr_ref, ys_ref, state_ref, prev_ref, band_ref, tail_ref):
    g = pl.program_id(0)
    seq_start = lax.rem(g - 1, per_seq) == 0

    @pl.when(g == 0)
    def _():
        slabs = [(src, dst, c0) for src, dst in ((w_hbm, w_ref), (wro_hbm, wro_ref), (wso_hbm, wso_ref),
                                                 (wout_hbm, wout_ref))
                 for c0 in range(0, dst.shape[1], W_SLAB)]

        def slab(k):
            src, dst, c0 = slabs[k]
            return pltpu.make_async_copy(src.at[:, pl.ds(c0, W_SLAB)],
                                         wtmp_ref.at[k % W_SLOTS, pl.ds(0, dst.shape[0])], w_sem.at[k % W_SLOTS])

        for k in range(W_SLOTS - 1):
            slab(k).start()
        for k, (_, dst, c0) in enumerate(slabs):
            if k + W_SLOTS - 1 < len(slabs):
                slab(k + W_SLOTS - 1).start()
            slab(k).wait()
            dst[:, c0:c0 + W_SLAB] = wtmp_ref[k % W_SLOTS, :dst.shape[0]].astype(BF16)
        for hd in range(SWA_HEADS):
            qg_ref[:, hd * SWA_HD:(hd + 1) * SWA_HD] = qgain_ref[...] * (SWA_HD ** -0.5)
        for kh in range(SWA_KV_HEADS):
            kg_ref[:, kh * SWA_HD:(kh + 1) * SWA_HD] = kgain_ref[...]
        wr_ref[...] = jnp.zeros_like(wr_ref)
        wr_ref[:, :N_EXPERTS] = wre_ref[...].astype(BF16)
        wr_ref[:, LANE_GROUP0:LANE_GROUP0 + N_GROUPS] = wrg_ref[...].astype(BF16)
        br_ref[...] = jnp.zeros_like(br_ref)
        br_ref[:, :N_EXPERTS] = bre_ref[...]
        br_ref[:, LANE_GROUP0:LANE_GROUP0 + N_GROUPS] = brg_ref[...]
        zc_ref[...] = jnp.zeros_like(zc_ref)
        tail_ref[...] = jnp.zeros_like(tail_ref)
        state_ref[...] = jnp.zeros_like(state_ref)

    @pl.when(seq_start)
    def _():
        state_ref[...] = jnp.zeros_like(state_ref)

    band_ref[:WINDOW, :] = tail_ref[...]
    band_ref[WINDOW:, :] = zc_ref[:, COL_SKV:COL_GATE]

    h = _rms(x_ref[...], g_ref[...]).astype(BF16)

    def project(c0, width):
        zn_ref[:, c0:c0 + width] = _dot(h, w_ref[:, c0:c0 + width]).astype(BF16)

    cw = 512
    proj_pieces = [functools.partial(project, j * cw, cw) for j in range(COL_SKV // cw)]
    proj_pieces.append(functools.partial(project, COL_SKV, Z_SKV_W))
    proj_pieces += [functools.partial(project, COL_GATE + j * cw, cw) for j in range(Z_GATE_W // cw)]

    def retention_rows(t):
        r_base = t * RET_ROWS
        n_chunks = RET_ROWS // RET_CHUNK
        part = lambda c, p, hd: zc_ref[r_base + c * RET_CHUNK:r_base + (c + 1) * RET_CHUNK,
                                       p * RET_W + hd * RET_DK:p * RET_W + (hd + 1) * RET_DK]
        for hd in range(RET_HEADS):
            state = state_ref[hd]
            for c in range(n_chunks):
                prev_ref[t * n_chunks + c, hd] = state.astype(BF16)
                kz = (part(c, 1, hd).astype(F32) * zeta_ref[hd]).astype(BF16)
                state = cdec_ref[hd] * state + _dot_tn(kz, part(c, 2, hd))
            state_ref[hd] = state
        out_rows = []
        for c in range(n_chunks):
            heads = []
            for hd in range(RET_HEADS):
                q = part(c, 0, hd)
                scores = _dot_nt(q, part(c, 1, hd)) * dmask_ref[hd]
                y = _dot(scores.astype(BF16), part(c, 2, hd)) + _dot(q, prev_ref[t * n_chunks + c, hd]) * xi_ref[hd]
                mu = jnp.mean(y, axis=-1, keepdims=True)
                d = y - mu
                var = jnp.mean(d * d, axis=-1, keepdims=True)
                hs = slice(hd * RET_DV, (hd + 1) * RET_DV)
                yn = d * lax.rsqrt(var + EPS) * gng_ref[:, hs] + gnb_ref[:, hs]
                gate = part(c, 3, hd).astype(F32)
                heads.append((gate * _sigmoid(gate) * yn).astype(BF16))
            out_rows.append(jnp.concatenate(heads, axis=1))
        yr_ref[r_base:r_base + RET_ROWS, :] = jnp.concatenate(out_rows, axis=0)

    group_rows = SWA_GROUP * WINDOW
    lo_q = lax.broadcasted_iota(jnp.int32, (WINDOW, 2 * SWA_HD), 1) < SWA_HD
    lo_k = lax.broadcasted_iota(jnp.int32, (2 * WINDOW, 2 * SWA_HD), 1) < SWA_HD
    in_prev = lax.broadcasted_iota(jnp.int32, (group_rows, 2 * WINDOW), 1) < WINDOW
    row_head = lax.broadcasted_iota(jnp.int32, (group_rows, 1), 0) // WINDOW

    def both_heads(v):
        swapped = pltpu.roll(v, SWA_HD, axis=1)
        return jnp.where(lo_k, v, swapped), jnp.where(lo_k, swapped, v)

    def swa_block(n):
        rows = slice(n * WINDOW, (n + 1) * WINDOW)
        kv = band_ref[n * WINDOW:(n + 2) * WINDOW, :]
        kf = kv[:, :SWA_KV].astype(F32)
        kn = kf * lax.rsqrt(_dot((kf * kf).astype(BF16), bk_ref[...]) + EPS) * kg_ref[...]
        keys = [a.astype(BF16) for a in both_heads(kn)]
        vals = [a.astype(BF16) for a in both_heads(kv[:, SWA_KV:].astype(F32))]
        qf = zc_ref[rows, COL_SQ:COL_SKV].astype(F32)
        qn = qf * lax.rsqrt(_dot((qf * qf).astype(BF16), bq_ref[...]) + EPS) * qg_ref[...]
        outs = []
        for kh in range(SWA_KV_HEADS):
            parts = []
            for j in range(SWA_GROUP // 2):
                c0 = (kh * SWA_GROUP + 2 * j) * SWA_HD
                two = qn[:, c0:c0 + 2 * SWA_HD]
                parts += [jnp.where(lo_q, two, 0.0), jnp.where(lo_q, 0.0, two)]
            qs = jnp.concatenate(parts, axis=0).astype(BF16)
            s = _dot_nt(qs, keys[kh]) + bias_ref[kh]
            if n == 0:
                s = jnp.where(jnp.logical_and(seq_start, in_prev), NEG_INF, s)
            sink = sinks_ref[kh * SWA_GROUP]
            for gi in range(1, SWA_GROUP):
                sink = jnp.where(row_head == gi, sinks_ref[kh * SWA_GROUP + gi], sink)
            m = jnp.maximum(jnp.max(s, axis=-1, keepdims=True), sink)
            p = jnp.exp(s - m)
            denom = jnp.sum(p, axis=-1, keepdims=True) + jnp.exp(sink - m)
            o = _dot(p.astype(BF16), vals[kh]) * (1.0 / denom)
            for j in range(SWA_GROUP // 2):
                even = o[(2 * j) * WINDOW:(2 * j + 1) * WINDOW]
                odd = o[(2 * j + 1) * WINDOW:(2 * j + 2) * WINDOW]
                outs.append(jnp.where(lo_q, even, odd).astype(BF16))
        ys_ref[rows, :] = jnp.concatenate(outs, axis=1)

    mix_pieces = [functools.partial(retention_rows, t) for t in range(BLOCK_ROWS // RET_ROWS)]
    mix_pieces += [functools.partial(swa_block, n) for n in range(BLOCK_ROWS // WINDOW)]

    subs = []

    def merge_sub(s):
        subs.append(_merge_rows(slice(s * MERGE_SUB, (s + 1) * MERGE_SUB), yr_ref, ys_ref, zc_ref, xp_ref, wro_ref,
                                wso_ref, wout_ref, fg_ref, wr_ref, br_ref))

    other_pieces = mix_pieces + [functools.partial(merge_sub, s) for s in range(BLOCK_ROWS // MERGE_SUB)]

    n_other = len(other_pieces)
    for k, piece in enumerate(other_pieces):
        for proj_piece in proj_pieces[k * len(proj_pieces) // n_other:(k + 1) * len(proj_pieces) // n_other]:
            proj_piece()
        piece()
    x1_ref[...] = jnp.concatenate([s[0] for s in subs], axis=0)
    hp_ref[...] = jnp.concatenate([s[1] for s in subs], axis=0)
    route_ref[...] = jnp.concatenate([s[2] for s in subs], axis=0)
    for s, sub in enumerate(subs):
        cnt_ref[s] = sub[3]

    tail_ref[...] = zc_ref[BLOCK_ROWS - WINDOW:, COL_SKV:COL_GATE]
    zc_ref[...] = zn_ref[...]


def _merge_rows(rows, yr_ref, ys_ref, z_ref, x_ref, wro_ref, wso_ref, wout_ref, fg_ref, wr_ref, br_ref):
    a = _dot(yr_ref[rows, :], wro_ref[...])
    b = _dot(ys_ref[rows, :], wso_ref[...])
    gate_r = z_ref[rows, COL_GATE:COL_GATE + D_MODEL].astype(F32)
    gate_s = z_ref[rows, COL_GATE + D_MODEL:].astype(F32)
    merged = (_sigmoid(gate_r) * a + _sigmoid(gate_s) * b).astype(BF16)
    x1 = x_ref[rows, :] + _dot(merged, wout_ref[...])
    h2 = _rms(x1, fg_ref[...])
    packed = _pack_halves(h2)

    logits = _dot(h2.astype(BF16), wr_ref[...]) + br_ref[...]
    lane = lax.broadcasted_iota(jnp.int32, logits.shape, 1)
    big = jnp.int32(ROUTE_LANES)
    is_group = jnp.logical_and(lane >= LANE_GROUP0, lane < LANE_GROUP0 + N_GROUPS)
    gl = jnp.where(is_group, logits, NEG_INF)
    gmax = jnp.max(gl, axis=-1, keepdims=True)
    g_w = 1.0 / jnp.sum(jnp.exp(gl - gmax), axis=-1, keepdims=True)
    g_sel = jnp.min(jnp.where(gl == gmax, lane, big), axis=-1, keepdims=True) - LANE_GROUP0
    in_group = jnp.logical_and(lane < N_EXPERTS, (lane >> 3) == g_sel)
    el = jnp.where(in_group, logits, NEG_INF)
    m1 = jnp.max(el, axis=-1, keepdims=True)
    i1 = jnp.min(jnp.where(el == m1, lane, big), axis=-1, keepdims=True)
    el2 = jnp.where(lane == i1, NEG_INF, el)
    m2 = jnp.max(el2, axis=-1, keepdims=True)
    i2 = jnp.min(jnp.where(el2 == m2, lane, big), axis=-1, keepdims=True)
    e2 = jnp.exp(m2 - m1)
    c1 = g_w / (1.0 + e2)
    c2 = g_w * e2 / (1.0 + e2)
    chosen = jnp.where(jnp.logical_or(lane == i1, lane == i2), 1.0, 0.0)
    route = (chosen
             + jnp.where(lane == LANE_C1, c1, 0.0)
             + jnp.where(lane == LANE_C2, c2, 0.0)
             + jnp.where(lane == i2 + LANE_SECOND0, 1.0, 0.0))
    return x1, packed, route, jnp.sum(chosen, axis=0, keepdims=True)


def _block(x2d, mix_g, w_in, gn_g, gn_b, sinks, q_g, k_g, w_ret_o, w_swa_o, w_out, ffn_g, router, seq):
    t = x2d.shape[0]
    n = t // BLOCK_ROWS
    assert seq % BLOCK_ROWS == 0
    consts = _retention_tables() + (gn_g, gn_b, q_g, k_g) + _swa_tables()
    merge_weights = (w_ret_o, w_swa_o, w_out)
    small = (ffn_g,) + tuple(router)
    hbm = pl.BlockSpec(memory_space=pl.ANY)
    full = lambda a: pl.BlockSpec(a.shape, lambda g: (0,) * a.ndim)
    cur = lambda w: pl.BlockSpec((BLOCK_ROWS, w), lambda g: (jnp.minimum(g, n - 1), 0))
    prv = lambda w: pl.BlockSpec((BLOCK_ROWS, w), lambda g: (jnp.maximum(g - 1, 0), 0))
    chunks = BLOCK_ROWS // RET_CHUNK
    subs_per_block = BLOCK_ROWS // MERGE_SUB
    return pl.pallas_call(
        functools.partial(_block_kernel, seq // BLOCK_ROWS),
        grid=(n + 1,),
        in_specs=([pl.BlockSpec(memory_space=pltpu.SMEM), cur(D_MODEL), prv(D_MODEL), full(mix_g), hbm]
                  + [full(a) for a in consts] + [hbm] * len(merge_weights) + [full(a) for a in small]),
        out_specs=[prv(D_MODEL), prv(HALF_D), prv(ROUTE_LANES),
                   pl.BlockSpec((subs_per_block, 1, ROUTE_LANES), lambda g: (jnp.maximum(g - 1, 0), 0, 0))],
        out_shape=[jax.ShapeDtypeStruct((t, D_MODEL), F32),
                   jax.ShapeDtypeStruct((t, HALF_D), I32),
                   jax.ShapeDtypeStruct((t, ROUTE_LANES), F32),
                   jax.ShapeDtypeStruct((n * subs_per_block, 1, ROUTE_LANES), F32)],
        scratch_shapes=[pltpu.VMEM(w_in.shape, BF16)] + [pltpu.VMEM(w.shape, BF16) for w in merge_weights]
                       + [pltpu.VMEM((W_SLOTS, D_MODEL, W_SLAB), F32), pltpu.SemaphoreType.DMA((W_SLOTS,)),
                        pltpu.VMEM((1, SWA_Q), F32), pltpu.VMEM((1, SWA_KV), F32),
                        pltpu.VMEM((D_MODEL, ROUTE_LANES), BF16), pltpu.VMEM((1, ROUTE_LANES), F32),
                        pltpu.VMEM((BLOCK_ROWS, IN_WIDTH), BF16), pltpu.VMEM((BLOCK_ROWS, IN_WIDTH), BF16),
                        pltpu.VMEM((BLOCK_ROWS, RET_W), BF16), pltpu.VMEM((BLOCK_ROWS, SWA_Q), BF16),
                        pltpu.VMEM((RET_HEADS, RET_DK, RET_DV), F32),
                        pltpu.VMEM((chunks, RET_HEADS, RET_DK, RET_DV), BF16),
                        pltpu.VMEM((BLOCK_ROWS + WINDOW, Z_SKV_W), BF16), pltpu.VMEM((WINDOW, Z_SKV_W), BF16)],
        compiler_params=pltpu.CompilerParams(
            dimension_semantics=("arbitrary",), vmem_limit_bytes=VMEM_LIMIT,
            allow_input_fusion=[False] * (5 + len(consts) + len(merge_weights)) + [False, True, True, False, False]),
        name="block",
    )(sinks, x2d, x2d, mix_g, w_in, *consts, *merge_weights, *small)


def _plan_tables(cnt_ref, base_ref, meta_ref):
    counts = cnt_ref[...]
    nt = counts.shape[0]
    exact = functools.partial(jnp.dot, precision=lax.Precision.HIGHEST, preferred_element_type=F32)
    total = jnp.sum(counts, axis=0, keepdims=True)
    tiles_e = jnp.ceil(total * (1.0 / ROW_TILE))
    k = lax.broadcasted_iota(jnp.int32, (ROUTE_LANES, ROUTE_LANES), 0)
    lane = lax.broadcasted_iota(jnp.int32, (ROUTE_LANES, ROUTE_LANES), 1)
    tile_end = exact(jnp.broadcast_to(tiles_e, (8, ROUTE_LANES)), jnp.where(k <= lane, 1.0, 0.0))[0:1]
    tile_start = tile_end - tiles_e
    r = lax.broadcasted_iota(jnp.int32, (nt, nt), 0)
    c = lax.broadcasted_iota(jnp.int32, (nt, nt), 1)
    before = exact(jnp.where(c < r, 1.0, 0.0), counts)
    base_ref[...] = tile_start * ROW_TILE + before

    used = jnp.max(tile_end, axis=-1, keepdims=True)
    tile = k.astype(F32)
    clamped = jnp.minimum(tile, jnp.maximum(used - 1.0, 0.0))
    is_expert = lane < N_EXPERTS
    owner = jnp.sum(jnp.where(jnp.logical_and(is_expert, tile_end <= clamped), 1.0, 0.0), axis=-1, keepdims=True)
    owner = jnp.minimum(owner, N_EXPERTS - 1.0)
    mine = lane.astype(F32) == owner
    pick = lambda v: jnp.sum(jnp.where(mine, v, 0.0), axis=-1, keepdims=True)
    t1 = tile[:, 0:1]
    valid = jnp.clip(pick(total) - (t1 - pick(tile_start)) * ROW_TILE, 0.0, float(ROW_TILE))
    valid = jnp.where(t1 < used, valid, 0.0)

    nonempty = jnp.where(jnp.logical_and(lane[0:1] < N_EXPERTS, tiles_e > 0.0), 1.0, 0.0)
    rank = exact(jnp.broadcast_to(nonempty, (8, ROUTE_LANES)), jnp.where(k < lane, 1.0, 0.0))[0:1]
    nonempty_col = jnp.sum(jnp.where(k == lane, jnp.broadcast_to(nonempty, k.shape), 0.0), axis=-1, keepdims=True)
    later = jnp.where(jnp.logical_and(k > lane, nonempty_col > 0.0), tile, float(ROUTE_LANES))
    nxt_e = jnp.min(later, axis=0, keepdims=True)
    nxt_e = jnp.where(nxt_e < float(ROUTE_LANES), nxt_e, -1.0)
    first = jnp.where(jnp.logical_and(t1 == pick(tile_start), t1 < used), 1.0, 0.0)
    my_rank = pick(rank)
    slot = my_rank - 2.0 * jnp.floor(my_rank * 0.5)
    fields = (owner, valid, first, pick(nxt_e), slot, jnp.broadcast_to(used, owner.shape))
    by_tile = sum(jnp.where(lane == n, f, 0.0) for n, f in enumerate(fields))
    meta_ref[...] = by_tile.T[:8, :].astype(I32)


def _plan_kernel(cnt_ref, route_ref, base_ref, meta_ref, pos1_ref, pos2_ref):
    step = pl.program_id(0)
    nt = cnt_ref.shape[0]
    subs_per_step = nt // PLAN_STEPS

    @pl.when(step == 0)
    def _():
        _plan_tables(cnt_ref, base_ref, meta_ref)
        pos1_ref[...] = jnp.zeros_like(pos1_ref)
        pos2_ref[...] = jnp.zeros_like(pos2_ref)

    sel_row = lax.broadcasted_iota(jnp.int32, (8, ROUTE_LANES), 0)
    sel_lane = lax.broadcasted_iota(jnp.int32, (8, ROUTE_LANES), 1)
    tok_r = lax.broadcasted_iota(jnp.int32, (MERGE_SUB, MERGE_SUB), 0)
    tok_c = lax.broadcasted_iota(jnp.int32, (MERGE_SUB, MERGE_SUB), 1)
    eye = jnp.where(tok_r == tok_c, 1.0, 0.0).astype(BF16)
    upto = jnp.where(tok_r <= tok_c, 1.0, 0.0).astype(BF16)
    sum_experts = lambda a: jnp.sum(a, axis=0, keepdims=True)

    def sub_positions(local, carry):
        s = step * subs_per_step + local
        route = route_ref[pl.ds(pl.multiple_of(local * MERGE_SUB, MERGE_SUB), MERGE_SUB), :].astype(BF16)
        record_t = _dot_tn(route, eye)
        incl_t = _dot_tn(route, upto)[:N_EXPERTS]
        both_t = record_t[:N_EXPERTS]
        second_t = record_t[LANE_SECOND0:LANE_SECOND0 + N_EXPERTS]
        rank2 = sum_experts(second_t * incl_t) - 1.0
        rank1 = sum_experts(both_t * incl_t) - rank2 - 2.0
        base_row = jnp.where(sel_lane < N_EXPERTS, jnp.broadcast_to(base_ref[pl.ds(s, 1), :], (8, ROUTE_LANES)), 0.0)
        hi = jnp.floor(base_row * (1.0 / 256.0))
        lo = base_row - 256.0 * hi
        hi2 = pltpu.roll(hi, LANE_SECOND0, axis=1)
        lo2 = pltpu.roll(lo, LANE_SECOND0, axis=1)
        table = (jnp.where(sel_row == 0, hi, 0.0) + jnp.where(sel_row == 1, lo, 0.0)
                 + jnp.where(sel_row == 2, hi2, 0.0) + jnp.where(sel_row == 3, lo2, 0.0))
        r = _dot_nt(table.astype(BF16), route)
        base_both = 256.0 * r[0:1] + r[1:2]
        base2 = 256.0 * r[2:3] + r[3:4]
        worker = s // subs_per_worker
        chunk0 = lax.rem(s, subs_per_worker) * chunks_per_sub
        for ref, row in ((pos1_ref, base_both - base2 + rank1), (pos2_ref, base2 + rank2)):
            for c in range(chunks_per_sub):
                ref[worker, pl.ds(chunk0 + c, 1), :] = row[:, c * SC_CHUNK:(c + 1) * SC_CHUNK].astype(I32)
        return carry

    chunks_per_sub = MERGE_SUB // SC_CHUNK
    subs_per_worker = nt // SC_WORKERS
    lax.fori_loop(0, subs_per_step, sub_positions, 0, unroll=8)


def _tile_plan(cnt, route, n_tiles):
    assert n_tiles <= ROUTE_LANES
    nt = cnt.shape[0]
    assert nt % SC_WORKERS == 0 and MERGE_SUB % SC_CHUNK == 0 and nt * MERGE_SUB <= SC_WORKERS * SC_IDX_ROWS * SC_CHUNK
    assert nt % PLAN_STEPS == 0
    pos = jax.ShapeDtypeStruct((SC_WORKERS, SC_IDX_ROWS, SC_CHUNK), I32)
    outs = [jax.ShapeDtypeStruct((nt, ROUTE_LANES), F32), jax.ShapeDtypeStruct((8, ROUTE_LANES), I32), pos, pos]
    whole = lambda a: pl.BlockSpec(a.shape, lambda i: (0,) * len(a.shape))
    _, meta, pos1, pos2 = pl.pallas_call(
        _plan_kernel,
        grid=(PLAN_STEPS,),
        in_specs=[pl.BlockSpec((nt, ROUTE_LANES), lambda i: (0, 0)),
                  pl.BlockSpec((route.shape[0] // PLAN_STEPS, ROUTE_LANES), lambda i: (i, 0))],
        out_specs=[whole(o) for o in outs],
        out_shape=outs,
        compiler_params=pltpu.CompilerParams(dimension_semantics=("arbitrary",), vmem_limit_bytes=VMEM_LIMIT),
        name="tile_plan",
    )(cnt.reshape(nt, ROUTE_LANES), route)
    return meta, pos1, pos2


def _sc_mesh():
    return plsc.VectorSubcoreMesh(core_axis_name="c", subcore_axis_name="s")


def _sc_worker():
    return lax.axis_index("s") * SC_CORES + lax.axis_index("c")


def _dispatch(hp, pos1, pos2, n_rows):
    t = hp.shape[0]
    per_worker = t // SC_WORKERS
    k = per_worker // SC_CHUNK
    idx = pltpu.VMEM((SC_IDX_ROWS, SC_CHUNK), I32)

    @functools.partial(pl.kernel, mesh=_sc_mesh(), out_type=jax.ShapeDtypeStruct((n_rows, HALF_D), I32),
                       scratch_types=[idx, idx, pltpu.VMEM((SC_CHUNK, HALF_D), I32)], name="moe_dispatch")
    def run(h_hbm, p1_hbm, p2_hbm, xs_hbm, p1_v, p2_v, rows_v):
        wid = _sc_worker()
        pltpu.sync_copy(p1_hbm.at[wid], p1_v)
        pltpu.sync_copy(p2_hbm.at[wid], p2_v)
        for j in range(k):
            pltpu.sync_copy(h_hbm.at[pl.ds(wid * per_worker + j * SC_CHUNK, SC_CHUNK)], rows_v)
            pltpu.sync_copy(rows_v, xs_hbm.at[p1_v.at[j]])
            pltpu.sync_copy(rows_v, xs_hbm.at[p2_v.at[j]])

    return run(hp, pos1, pos2)


def _combine_gather(ys, pos1, pos2, t):
    per_worker = t // SC_WORKERS
    k = per_worker // SC_CHUNK
    idx = pltpu.VMEM((SC_IDX_ROWS, SC_CHUNK), I32)
    out = jax.ShapeDtypeStruct((t, HALF_D), I32)

    @functools.partial(pl.kernel, mesh=_sc_mesh(), out_type=(out, out),
                       scratch_types=[idx, idx, pltpu.VMEM((SC_CHUNK, HALF_D), I32)], name="moe_combine")
    def run(ys_hbm, p1_hbm, p2_hbm, g1_hbm, g2_hbm, p1_v, p2_v, rows_v):
        wid = _sc_worker()
        pltpu.sync_copy(p1_hbm.at[wid], p1_v)
        pltpu.sync_copy(p2_hbm.at[wid], p2_v)
        for j in range(k):
            dst = pl.ds(wid * per_worker + j * SC_CHUNK, SC_CHUNK)
            pltpu.sync_copy(ys_hbm.at[p1_v.at[j]], rows_v)
            pltpu.sync_copy(rows_v, g1_hbm.at[dst])
            pltpu.sync_copy(ys_hbm.at[p2_v.at[j]], rows_v)
            pltpu.sync_copy(rows_v, g2_hbm.at[dst])

    return run(ys, pos1, pos2)


META_OWNER, META_VALID, META_FIRST, META_NEXT, META_SLOT, META_USED = range(6)


def _experts_kernel(meta_ref, xs_hbm, wg_hbm, wu_hbm, wd_hbm, ys_ref,
                    xs_buf, xs_sem, wg_f, wu_f, wd_f, wg_bf, wu_bf, wd_bf, w_sem):
    i = pl.program_id(0)
    used = meta_ref[META_USED, 0]
    valid = meta_ref[META_VALID, i]
    owner = meta_ref[META_OWNER, i]
    nxt = meta_ref[META_NEXT, i]

    def rows_in(tile):
        start = pl.multiple_of(tile * ROW_TILE, ROW_TILE)
        slot = lax.rem(tile, XS_SLOTS)
        return pltpu.make_async_copy(xs_hbm.at[pl.ds(start, ROW_TILE)], xs_buf.at[slot], xs_sem.at[slot])

    @pl.when(i == 0)
    def _():
        for ahead in range(XS_SLOTS - 1):
            pl.when(ahead < used)(rows_in(ahead).start)

    @pl.when(i + (XS_SLOTS - 1) < used)
    def _():
        rows_in(i + (XS_SLOTS - 1)).start()

    def fetch(expert, slot):
        pairs = ((wg_hbm, wg_f), (wu_hbm, wu_f), (wd_hbm, wd_f))
        return [pltpu.make_async_copy(w.at[expert], buf.at[slot], w_sem.at[slot, n])
                for n, (w, buf) in enumerate(pairs)]

    @pl.when(i == 0)
    def _():
        for copy in fetch(owner, 0):
            copy.start()

    @pl.when(meta_ref[META_FIRST, i] == 1)
    def _():
        slot = meta_ref[META_SLOT, i]
        for copy in fetch(owner, slot):
            copy.wait()
        wg_bf[...] = wg_f[slot].astype(BF16)
        wu_bf[...] = wu_f[slot].astype(BF16)
        wd_bf[...] = wd_f[slot].astype(BF16)

        @pl.when(nxt >= 0)
        def _():
            for copy in fetch(nxt, 1 - slot):
                copy.start()

    @pl.when(i < used)
    def _():
        rows_in(i).wait()
        row = lax.broadcasted_iota(jnp.int32, (ROW_TILE, HALF_D), 0)
        words = xs_buf[lax.rem(i, XS_SLOTS)]
        packed = jnp.where(row < valid, words, 0)
        hi, lo = _unpack_halves(packed)
        hi = hi.astype(BF16)
        lo = lo.astype(BF16)
        hg = _dot(hi, wg_bf[:HALF_D, :]) + _dot(lo, wg_bf[HALF_D:, :])
        hu = _dot(hi, wu_bf[:HALF_D, :]) + _dot(lo, wu_bf[HALF_D:, :])
        hid = (hg * _sigmoid(hg) * hu).astype(BF16)
        ys_ref[...] = _pack_halves(_dot(hid, wd_bf[...]))


def _experts(xs, meta, n_tiles, wg, wu, wd):
    tile = lambda i, meta: (jnp.minimum(i, meta[META_USED, 0] - 1), 0)
    hbm = pl.BlockSpec(memory_space=pl.ANY)
    up_shape, down_shape = (D_MODEL, EXPERT_FF), (EXPERT_FF, D_MODEL)
    grid_spec = pltpu.PrefetchScalarGridSpec(
        num_scalar_prefetch=1,
        grid=(n_tiles,),
        in_specs=[hbm, hbm, hbm, hbm],
        out_specs=pl.BlockSpec((ROW_TILE, HALF_D), tile),
        scratch_shapes=[pltpu.VMEM((XS_SLOTS, ROW_TILE, HALF_D), I32), pltpu.SemaphoreType.DMA((XS_SLOTS,)),
                        pltpu.VMEM((2,) + up_shape, F32), pltpu.VMEM((2,) + up_shape, F32),
                        pltpu.VMEM((2,) + down_shape, F32),
                        pltpu.VMEM(up_shape, BF16), pltpu.VMEM(up_shape, BF16), pltpu.VMEM(down_shape, BF16),
                        pltpu.SemaphoreType.DMA((2, 3))],
    )
    return pl.pallas_call(
        _experts_kernel,
        grid_spec=grid_spec,
        out_shape=jax.ShapeDtypeStruct((n_tiles * ROW_TILE, HALF_D), I32),
        compiler_params=pltpu.CompilerParams(dimension_semantics=("arbitrary",),
                                             vmem_limit_bytes=VMEM_LIMIT),
        name="experts",
    )(meta, xs, wg, wu, wd)


def _ple_kernel(x1_hbm, g1_hbm, g2_hbm, route_ref, p_ref, wp_f32, pg_ref, gg_ref, wgate_f32, o_ref,
                wp_ref, wgate_ref, x1_buf, g1_buf, g2_buf, ring_sem):
    i = pl.program_id(0)
    steps = pl.num_programs(0)
    streams = ((x1_hbm, x1_buf), (g1_hbm, g1_buf), (g2_hbm, g2_buf))

    def rows_in(step):
        start = pl.multiple_of(step * PLE_TM, PLE_TM)
        slot = lax.rem(step, PLE_SLOTS)
        return [pltpu.make_async_copy(src.at[pl.ds(start, PLE_TM)], buf.at[slot], ring_sem.at[n, slot])
                for n, (src, buf) in enumerate(streams)]

    def start_rows(step):
        for copy in rows_in(step):
            copy.start()

    @pl.when(i == 0)
    def _():
        for ahead in range(PLE_SLOTS - 1):
            pl.when(ahead < steps)(functools.partial(start_rows, ahead))
        wp_ref[...] = wp_f32[...].astype(BF16)
        wgate_ref[...] = wgate_f32[...].astype(BF16)

    pl.when(i + (PLE_SLOTS - 1) < steps)(functools.partial(start_rows, i + (PLE_SLOTS - 1)))

    for copy in rows_in(i):
        copy.wait()
    slot = lax.rem(i, PLE_SLOTS)
    x1_ref, g1_ref, g2_ref = x1_buf.at[slot], g1_buf.at[slot], g2_buf.at[slot]

    route = route_ref[...]
    lane = lax.broadcasted_iota(jnp.int32, route.shape, 1)
    pick = lambda ln: jnp.sum(jnp.where(lane == ln, route, 0.0), axis=-1, keepdims=True)
    c1 = pick(LANE_C1)
    c2 = pick(LANE_C2)
    a_hi, a_lo = _unpack_halves(g1_ref[...])
    b_hi, b_lo = _unpack_halves(g2_ref[...])
    x = x1_ref[...] + jnp.concatenate([c1 * a_hi + c2 * b_hi, c1 * a_lo + c2 * b_lo], axis=1)
    ple = _rms(_dot(p_ref[...].astype(BF16), wp_ref[...]), pg_ref[...])
    gate = _sigmoid(_dot(_rms(x, gg_ref[...]).astype(BF16), wgate_ref[...]))
    o_ref[...] = x + gate * ple


def _ple(x1, g1, g2, route, p2d, w_ple, ple_g, gate_g, w_gate):
    t = x1.shape[0]
    row = lambda w: pl.BlockSpec((PLE_TM, w), lambda i: (i, 0))
    full = lambda a: pl.BlockSpec(a.shape, lambda i: (0,) * a.ndim)
    hbm = pl.BlockSpec(memory_space=pl.ANY)
    return pl.pallas_call(
        _ple_kernel,
        grid=(t // PLE_TM,),
        in_specs=[hbm, hbm, hbm, row(ROUTE_LANES), row(PLE_DIM),
                  full(w_ple), full(ple_g), full(gate_g), full(w_gate)],
        out_specs=row(D_MODEL),
        out_shape=jax.ShapeDtypeStruct((t, D_MODEL), F32),
        scratch_shapes=[pltpu.VMEM(w_ple.shape, BF16), pltpu.VMEM(w_gate.shape, BF16),
                        pltpu.VMEM((PLE_SLOTS, PLE_TM, D_MODEL), F32),
                        pltpu.VMEM((PLE_SLOTS, PLE_TM, HALF_D), I32), pltpu.VMEM((PLE_SLOTS, PLE_TM, HALF_D), I32),
                        pltpu.SemaphoreType.DMA((3, PLE_SLOTS))],
        compiler_params=pltpu.CompilerParams(dimension_semantics=("arbitrary",),
                                             vmem_limit_bytes=VMEM_LIMIT),
        name="ple",
    )(x1, g1, g2, route, p2d, w_ple, ple_g, gate_g, w_gate)


def kernel(x, p, mix_norm_g, w_in, ret_gn_g, ret_gn_b, w_ret_o, q_norm_g, k_norm_g, attn_sinks,
           w_swa_o, w_out, ffn_norm_g, w_router_group, b_router_group, w_router_expert,
           b_router_expert, w_exp_gate, w_exp_up, w_exp_down, ple_gate_norm_g, w_ple_gate,
           w_ple, ple_norm_g):
    batch, seq, d = x.shape
    t = batch * seq
    depth = w_in.shape[0]
    n_tiles = 2 * t // ROW_TILE + N_EXPERTS
    row = lambda a: a.reshape(1, -1)
    x2d = x.reshape(t, d)
    for i in range(depth):
        router = (w_router_expert[i], w_router_group[i], row(b_router_expert[i]), row(b_router_group[i]))
        x1, hp, route, cnt = _block(x2d, row(mix_norm_g[i]), w_in[i], row(ret_gn_g[i]),
                                    row(ret_gn_b[i]), attn_sinks[i], row(q_norm_g[i]), row(k_norm_g[i]),
                                    w_ret_o[i], w_swa_o[i], w_out[i], row(ffn_norm_g[i]), router, seq)
        meta, pos1, pos2 = _tile_plan(cnt, route, n_tiles)
        xs = _dispatch(hp, pos1, pos2, n_tiles * ROW_TILE)
        ys = _experts(xs, meta, n_tiles, w_exp_gate[i], w_exp_up[i], w_exp_down[i])
        g1, g2 = _combine_gather(ys, pos1, pos2, t)
        x2d = _ple(x1, g1, g2, route, p[i].reshape(t, PLE_DIM), w_ple[i], row(ple_norm_g[i]),
                   row(ple_gate_norm_g[i]), w_ple_gate[i])
    return x2d.reshape(batch, seq, d)
```
